```python
import jax, jax.numpy as jnp
from jax import lax
import numpy as np

D_MODEL = 1024
BATCH = 8
SEQ = 8192
DEPTH = 1

D_MIX = D_MODEL
GDN_HEADS = 4
GDN_HEAD_DIM = 128
GDN_WIDTH = GDN_HEADS * GDN_HEAD_DIM
POOL_WINDOWS = (2, 4, 8, 16)
POOL_GROUPS = len(POOL_WINDOWS)
POOL_WIDTH = D_MIX - GDN_WIDTH
POOL_GROUP_DIM = POOL_WIDTH // POOL_GROUPS
CONV_K = 4
CHUNK = 64
D_FF = ((8 * D_MODEL // 3 + 255) // 256) * 256
D_IN = 4 * GDN_WIDTH + 2 * GDN_HEADS + POOL_WIDTH
N_MOD = 9
EPS = 1e-6

kernel_name = "hybrid_gdn_pool_macaron_adaln"


def rms_norm(x, gain):
    xf = x.astype(jnp.float32)
    y = xf * lax.rsqrt(jnp.mean(xf * xf, axis=-1, keepdims=True) + EPS)
    return (y * gain.astype(jnp.float32)).astype(x.dtype)


def l2_normalize(x):
    xf = x.astype(jnp.float32)
    return xf * lax.rsqrt(jnp.sum(xf * xf, axis=-1, keepdims=True) + EPS)


def modulate(h, shift, scale):
    return h * (1 + scale[:, None, :]) + shift[:, None, :]


def swiglu(h, w_gate, w_up, w_down):
    return (jax.nn.silu(h @ w_gate) * (h @ w_up)) @ w_down


def causal_depthwise_conv_silu(x, w):
    C = x.shape[-1]
    y = lax.conv_general_dilated(
        x, w[:, None, :].astype(x.dtype), window_strides=(1,), padding=[(CONV_K - 1, 0)],
        dimension_numbers=("NWC", "WIO", "NWC"), feature_group_count=C)
    return jax.nn.silu(y)


def gated_delta_rule_chunked(q, k, v, g, beta):
    B, T, H, Dk = q.shape
    Dv = v.shape[-1]
    N = T // CHUNK

    def chunks(t):
        t = t.reshape((B, N, CHUNK, H) + t.shape[3:])
        return jnp.moveaxis(t, 3, 1)

    q, k, v, g, beta = chunks(q), chunks(k), chunks(v), chunks(g), chunks(beta)
    g_cum = jnp.cumsum(g, axis=-1)
    causal = jnp.tril(jnp.ones((CHUNK, CHUNK), dtype=bool))
    strict = jnp.tril(jnp.ones((CHUNK, CHUNK), dtype=bool), -1)
    diff = g_cum[..., :, None] - g_cum[..., None, :]
    decay = jnp.where(causal, jnp.exp(jnp.where(causal, diff, 0.0)), 0.0)
    k_beta = k * beta[..., None]
    m = jnp.where(strict, jnp.einsum("bhnid,bhnjd->bhnij", k_beta, k) * decay, 0.0)
    a = m + jnp.eye(CHUNK, dtype=m.dtype)
    u = lax.linalg.triangular_solve(a, v * beta[..., None], left_side=True, lower=True)
    w = lax.linalg.triangular_solve(a, k_beta * jnp.exp(g_cum)[..., None], left_side=True, lower=True)
    intra = jnp.einsum("bhnid,bhnjd->bhnij", q, k) * decay
    g_last = g_cum[..., -1:]
    q_dec = q * jnp.exp(g_cum)[..., None]
    k_dec = k * jnp.exp(g_last - g_cum)[..., None]
    chunk_decay = jnp.exp(g_last[..., 0])
    xs = tuple(jnp.moveaxis(t, 2, 0) for t in (q_dec, k_dec, u, w, intra, chunk_decay))

    def step(S, inp):
        qd, kd, un, wn, an, cd = inp
        v_new = un - jnp.einsum("bhck,bhkv->bhcv", wn, S)
        o = jnp.einsum("bhck,bhkv->bhcv", qd, S) + jnp.einsum("bhij,bhjv->bhiv", an, v_new)
        S = S * cd[..., None, None] + jnp.einsum("bhck,bhcv->bhkv", kd, v_new)
        return S, o

    S0 = jnp.zeros((B, H, Dk, Dv), jnp.float32)
    _, o = lax.scan(step, S0, xs)
    return jnp.transpose(o, (1, 0, 3, 2, 4)).reshape(B, T, H, Dv)


def multiscale_causal_pool(p):
    B, T, _ = p.shape
    pf = p.astype(jnp.float32).reshape(B, T, POOL_GROUPS, POOL_GROUP_DIM)
    cs0 = jnp.concatenate([jnp.zeros_like(pf[:, :1]), jnp.cumsum(pf, axis=1)], axis=1)
    t1 = jnp.arange(1, T + 1, dtype=jnp.float32)
    outs = []
    for gi, win in enumerate(POOL_WINDOWS):
        cur = cs0[:, 1:, gi]
        lag = jnp.concatenate(
            [jnp.zeros((B, win - 1, POOL_GROUP_DIM), jnp.float32), cs0[:, :T - win + 1, gi]], axis=1)
        cnt = jnp.minimum(t1, win)[None, :, None]
        outs.append((cur - lag) / cnt - pf[:, :, gi])
    return jnp.stack(outs, axis=2)


def _fwd_setup_inputs(seed: int = 0) -> dict:
    key = jax.random.key(seed)
    ks = jax.random.split(key, 24)
    f32 = jnp.float32
    L, D = DEPTH, D_MODEL

    def nrm(k, shape, scale):
        return jax.random.normal(k, shape, f32) * scale

    def gain(k, shape):
        return 1.0 + 0.02 * jax.random.normal(k, shape, f32)

    dt = jnp.exp(jax.random.uniform(ks[11], (L, GDN_HEADS), f32, np.log(1e-3), np.log(1e-1)))
    return {
        "x": nrm(ks[0], (BATCH, SEQ, D), 1.0),
        "c": nrm(ks[1], (BATCH, D), 1.0),
        "w_ada": nrm(ks[2], (L, D, N_MOD * D), 0.5 * D ** -0.5),
        "b_ada": nrm(ks[3], (L, N_MOD * D), 0.01),
        "norm_ffn1": gain(ks[4], (L, D)),
        "ffn1_gate": nrm(ks[5], (L, D, D_FF), D ** -0.5),
        "ffn1_up": nrm(ks[6], (L, D, D_FF), D ** -0.5),
        "ffn1_down": nrm(ks[7], (L, D_FF, D), D_FF ** -0.5),
        "norm_mix": gain(ks[8], (L, D)),
        "w_in": nrm(ks[9], (L, D, D_IN), D ** -0.5),
        "conv_w": nrm(ks[10], (L, CONV_K, 3 * GDN_WIDTH), CONV_K ** -0.5),
        "a_log": jnp.log(jax.random.uniform(ks[12], (L, GDN_HEADS), f32, 1.0, 16.0)),
        "dt_bias": dt + jnp.log(-jnp.expm1(-dt)),
        "gdn_norm": gain(ks[13], (L, GDN_HEAD_DIM)),
        "pool_w": nrm(ks[14], (L, POOL_GROUPS, POOL_GROUP_DIM, POOL_GROUP_DIM), POOL_GROUP_DIM ** -0.5),
        "pool_scale": gain(ks[15], (L, POOL_WIDTH)),
        "w_out": nrm(ks[16], (L, D_MIX, D), D_MIX ** -0.5),
        "norm_ffn2": gain(ks[17], (L, D)),
        "ffn2_gate": nrm(ks[18], (L, D, D_FF), D ** -0.5),
        "ffn2_up": nrm(ks[19], (L, D, D_FF), D ** -0.5),
        "ffn2_down": nrm(ks[20], (L, D_FF, D), D_FF ** -0.5),
        "final_norm": gain(ks[21], (D,)),
    }


def _fwd_reference(x, c, w_ada, b_ada, norm_ffn1, ffn1_gate, ffn1_up, ffn1_down, norm_mix, w_in, conv_w,
              a_log, dt_bias, gdn_norm, pool_w, pool_scale, w_out, norm_ffn2, ffn2_gate, ffn2_up,
              ffn2_down, final_norm):
    B, T, _ = x.shape
    H, Dh, GW = GDN_HEADS, GDN_HEAD_DIM, GDN_WIDTH
    split_at = [3 * GW, 4 * GW, 4 * GW + H, 4 * GW + 2 * H]
    for l in range(DEPTH):
        mod = (jax.nn.silu(c) @ w_ada[l] + b_ada[l]).reshape(B, N_MOD, D_MODEL)

        h = modulate(rms_norm(x, norm_ffn1[l]), mod[:, 0], mod[:, 1])
        x = x + 0.5 * mod[:, 2][:, None, :] * swiglu(h, ffn1_gate[l], ffn1_up[l], ffn1_down[l])

        h = modulate(rms_norm(x, norm_mix[l]), mod[:, 3], mod[:, 4])
        proj = h @ w_in[l]
        qkv, z, b_raw, a_raw, p = jnp.split(proj, split_at, axis=-1)
        qkv = causal_depthwise_conv_silu(qkv, conv_w[l])
        q, k, v = jnp.split(qkv, 3, axis=-1)
        q = l2_normalize(q.reshape(B, T, H, Dh)) * (Dh ** -0.5)
        k = l2_normalize(k.reshape(B, T, H, Dh))
        v = v.reshape(B, T, H, Dh).astype(jnp.float32)
        beta = jax.nn.sigmoid(b_raw.astype(jnp.float32))
        g = -jnp.exp(a_log[l].astype(jnp.float32)) * jax.nn.softplus(
            a_raw.astype(jnp.float32) + dt_bias[l].astype(jnp.float32))
        o = gated_delta_rule_chunked(q, k, v, g, beta)
        o = rms_norm(o, gdn_norm[l]) * jax.nn.silu(z.reshape(B, T, H, Dh).astype(jnp.float32))
        gdn_out = o.reshape(B, T, GW).astype(x.dtype)

        pooled = multiscale_causal_pool(p)
        pooled = jnp.einsum("btgi,gio->btgo", pooled, pool_w[l].astype(jnp.float32))
        pool_out = (pooled.reshape(B, T, POOL_WIDTH) * pool_scale[l].astype(jnp.float32)).astype(x.dtype)

        mixed = jnp.concatenate([gdn_out, pool_out], axis=-1) @ w_out[l]
        x = x + mod[:, 5][:, None, :] * mixed

        h = modulate(rms_norm(x, norm_ffn2[l]), mod[:, 6], mod[:, 7])
        x = x + 0.5 * mod[:, 8][:, None, :] * swiglu(h, ffn2_gate[l], ffn2_up[l], ffn2_down[l])
    return rms_norm(x, final_norm)


import jax as _jax
import jax.numpy as _jnp

TWIN_FORMAT = 'train_step'
FWD_PARAMS = ['x', 'c', 'w_ada', 'b_ada', 'norm_ffn1', 'ffn1_gate', 'ffn1_up', 'ffn1_down', 'norm_mix', 'w_in', 'conv_w', 'a_log', 'dt_bias', 'gdn_norm', 'pool_w', 'pool_scale', 'w_out', 'norm_ffn2', 'ffn2_gate', 'ffn2_up', 'ffn2_down', 'final_norm']
TWIN_WEIGHTS = ['w_ada', 'b_ada', 'norm_ffn1', 'ffn1_gate', 'ffn1_up', 'ffn1_down', 'norm_mix', 'w_in', 'conv_w', 'a_log', 'dt_bias', 'gdn_norm', 'pool_w', 'pool_scale', 'w_out', 'norm_ffn2', 'ffn2_gate', 'ffn2_up', 'ffn2_down', 'final_norm']
TWIN_DIFF_INPUT = 'x'
TWIN_INPUTS = ['x', 'c', 'w_ada', 'b_ada', 'norm_ffn1', 'ffn1_gate', 'ffn1_up', 'ffn1_down', 'norm_mix', 'w_in', 'conv_w', 'a_log', 'dt_bias', 'gdn_norm', 'pool_w', 'pool_scale', 'w_out', 'norm_ffn2', 'ffn2_gate', 'ffn2_up', 'ffn2_down', 'final_norm', 'loss_target', 'm_w_ada', 'm_b_ada', 'm_norm_ffn1', 'm_ffn1_gate', 'm_ffn1_up', 'm_ffn1_down', 'm_norm_mix', 'm_w_in', 'm_conv_w', 'm_a_log', 'm_dt_bias', 'm_gdn_norm', 'm_pool_w', 'm_pool_scale', 'm_w_out', 'm_norm_ffn2', 'm_ffn2_gate', 'm_ffn2_up', 'm_ffn2_down', 'm_final_norm', 'v_w_ada', 'v_b_ada', 'v_norm_ffn1', 'v_ffn1_gate', 'v_ffn1_up', 'v_ffn1_down', 'v_norm_mix', 'v_w_in', 'v_conv_w', 'v_a_log', 'v_dt_bias', 'v_gdn_norm', 'v_pool_w', 'v_pool_scale', 'v_w_out', 'v_norm_ffn2', 'v_ffn2_gate', 'v_ffn2_up', 'v_ffn2_down', 'v_final_norm']
TWIN_OUTPUTS = ['loss', 'grad_x', 'grad_w_ada', 'grad_b_ada', 'grad_norm_ffn1', 'grad_ffn1_gate', 'grad_ffn1_up', 'grad_ffn1_down', 'grad_norm_mix', 'grad_w_in', 'grad_conv_w', 'grad_a_log', 'grad_dt_bias', 'grad_gdn_norm', 'grad_pool_w', 'grad_pool_scale', 'grad_w_out', 'grad_norm_ffn2', 'grad_ffn2_gate', 'grad_ffn2_up', 'grad_ffn2_down', 'grad_final_norm', 'delta_w_ada', 'delta_b_ada', 'delta_norm_ffn1', 'delta_ffn1_gate', 'delta_ffn1_up', 'delta_ffn1_down', 'delta_norm_mix', 'delta_w_in', 'delta_conv_w', 'delta_a_log', 'delta_dt_bias', 'delta_gdn_norm', 'delta_pool_w', 'delta_pool_scale', 'delta_w_out', 'delta_norm_ffn2', 'delta_ffn2_gate', 'delta_ffn2_up', 'delta_ffn2_down', 'delta_final_norm', 'new_m_w_ada', 'new_m_b_ada', 'new_m_norm_ffn1', 'new_m_ffn1_gate', 'new_m_ffn1_up', 'new_m_ffn1_down', 'new_m_norm_mix', 'new_m_w_in', 'new_m_conv_w', 'new_m_a_log', 'new_m_dt_bias', 'new_m_gdn_norm', 'new_m_pool_w', 'new_m_pool_scale', 'new_m_w_out', 'new_m_norm_ffn2', 'new_m_ffn2_gate', 'new_m_ffn2_up', 'new_m_ffn2_down', 'new_m_final_norm', 'new_v_w_ada', 'new_v_b_ada', 'new_v_norm_ffn1', 'new_v_ffn1_gate', 'new_v_ffn1_up', 'new_v_ffn1_down', 'new_v_norm_mix', 'new_v_w_in', 'new_v_conv_w', 'new_v_a_log', 'new_v_dt_bias', 'new_v_gdn_norm', 'new_v_pool_w', 'new_v_pool_scale', 'new_v_w_out', 'new_v_norm_ffn2', 'new_v_ffn2_gate', 'new_v_ffn2_up', 'new_v_ffn2_down', 'new_v_final_norm']
TWIN_LEAF_KINDS = {'loss': 'loss', 'grad_x': 'grad_x', 'grad_w_ada': 'grad_w', 'grad_b_ada': 'grad_w', 'grad_norm_ffn1': 'grad_w', 'grad_ffn1_gate': 'grad_w', 'grad_ffn1_up': 'grad_w', 'grad_ffn1_down': 'grad_w', 'grad_norm_mix': 'grad_w', 'grad_w_in': 'grad_w', 'grad_conv_w': 'grad_w', 'grad_a_log': 'grad_w', 'grad_dt_bias': 'grad_w', 'grad_gdn_norm': 'grad_w', 'grad_pool_w': 'grad_w', 'grad_pool_scale': 'grad_w', 'grad_w_out': 'grad_w', 'grad_norm_ffn2': 'grad_w', 'grad_ffn2_gate': 'grad_w', 'grad_ffn2_up': 'grad_w', 'grad_ffn2_down': 'grad_w', 'grad_final_norm': 'grad_w', 'delta_w_ada': 'delta_w', 'delta_b_ada': 'delta_w', 'delta_norm_ffn1': 'delta_w', 'delta_ffn1_gate': 'delta_w', 'delta_ffn1_up': 'delta_w', 'delta_ffn1_down': 'delta_w', 'delta_norm_mix': 'delta_w', 'delta_w_in': 'delta_w', 'delta_conv_w': 'delta_w', 'delta_a_log': 'delta_w', 'delta_dt_bias': 'delta_w', 'delta_gdn_norm': 'delta_w', 'delta_pool_w': 'delta_w', 'delta_pool_scale': 'delta_w', 'delta_w_out': 'delta_w', 'delta_norm_ffn2': 'delta_w', 'delta_ffn2_gate': 'delta_w', 'delta_ffn2_up': 'delta_w', 'delta_ffn2_down': 'delta_w', 'delta_final_norm': 'delta_w', 'new_m_w_ada': 'new_m', 'new_m_b_ada': 'new_m', 'new_m_norm_ffn1': 'new_m', 'new_m_ffn1_gate': 'new_m', 'new_m_ffn1_up': 'new_m', 'new_m_ffn1_down': 'new_m', 'new_m_norm_mix': 'new_m', 'new_m_w_in': 'new_m', 'new_m_conv_w': 'new_m', 'new_m_a_log': 'new_m', 'new_m_dt_bias': 'new_m', 'new_m_gdn_norm': 'new_m', 'new_m_pool_w': 'new_m', 'new_m_pool_scale': 'new_m', 'new_m_w_out': 'new_m', 'new_m_norm_ffn2': 'new_m', 'new_m_ffn2_gate': 'new_m', 'new_m_ffn2_up': 'new_m', 'new_m_ffn2_down': 'new_m', 'new_m_final_norm': 'new_m', 'new_v_w_ada': 'new_v', 'new_v_b_ada': 'new_v', 'new_v_norm_ffn1': 'new_v', 'new_v_ffn1_gate': 'new_v', 'new_v_ffn1_up': 'new_v', 'new_v_ffn1_down': 'new_v', 'new_v_norm_mix': 'new_v', 'new_v_w_in': 'new_v', 'new_v_conv_w': 'new_v', 'new_v_a_log': 'new_v', 'new_v_dt_bias': 'new_v', 'new_v_gdn_norm': 'new_v', 'new_v_pool_w': 'new_v', 'new_v_pool_scale': 'new_v', 'new_v_w_out': 'new_v', 'new_v_norm_ffn2': 'new_v', 'new_v_ffn2_gate': 'new_v', 'new_v_ffn2_up': 'new_v', 'new_v_ffn2_down': 'new_v', 'new_v_final_norm': 'new_v'}


def _forward(args):
    return _fwd_reference(*[args[k] for k in FWD_PARAMS])


def _output_shape():
    out = _jax.eval_shape(lambda: _forward(_fwd_setup_inputs(0)))
    return out.shape, out.dtype

N_MICROBATCH = 1
ADAM_LR = 0.001
ADAM_B1 = 0.9
ADAM_B2 = 0.999
ADAM_EPS = 1e-08
ADAM_WD = 0.01
ADAM_STEP = 10
PER_EXAMPLE_BATCH_AXIS = {'x': 0, 'c': 0, 'loss_target': 0}
SHARED_INPUTS = []
_WEIGHT_DTYPES = {'w_ada': _jnp.float32, 'b_ada': _jnp.float32, 'norm_ffn1': _jnp.float32, 'ffn1_gate': _jnp.float32, 'ffn1_up': _jnp.float32, 'ffn1_down': _jnp.float32, 'norm_mix': _jnp.float32, 'w_in': _jnp.float32, 'conv_w': _jnp.float32, 'a_log': _jnp.float32, 'dt_bias': _jnp.float32, 'gdn_norm': _jnp.float32, 'pool_w': _jnp.float32, 'pool_scale': _jnp.float32, 'w_out': _jnp.float32, 'norm_ffn2': _jnp.float32, 'ffn2_gate': _jnp.float32, 'ffn2_up': _jnp.float32, 'ffn2_down': _jnp.float32, 'final_norm': _jnp.float32}
MOMENT_SCALE = {'w_ada': 5.807419e-02, 'b_ada': 9.379970e-02, 'norm_ffn1': 3.961636e-02, 'ffn1_gate': 1.729303e-02, 'ffn1_up': 1.671160e-02, 'ffn1_down': 2.772689e-02, 'norm_mix': 7.588569e-02, 'w_in': 4.772390e-02, 'conv_w': 3.733727e-02, 'a_log': 2.763755e-01, 'dt_bias': 3.456641e-01, 'gdn_norm': 1.094359e-01, 'pool_w': 6.908170e-02, 'pool_scale': 7.516372e-02, 'w_out': 6.019989e-02, 'norm_ffn2': 3.749942e-02, 'ffn2_gate': 1.660511e-02, 'ffn2_up': 1.610184e-02, 'ffn2_down': 2.664963e-02, 'final_norm': 6.395903e+01}


def _to_microbatches(a, axis):
    t = _jnp.moveaxis(a, axis, 0)
    t = t.reshape((N_MICROBATCH, t.shape[0] // N_MICROBATCH) + t.shape[1:])
    return _jnp.moveaxis(t, 1, axis + 1)


def setup_inputs(seed: int = 0) -> dict:
    inp = _fwd_setup_inputs(seed)
    key = _jax.random.fold_in(_jax.random.key(seed), 7919)
    shape, _ = _output_shape()
    out = dict(inp)
    out["loss_target"] = _jax.random.normal(_jax.random.fold_in(key, 0), shape, _jnp.float32)
    for i, name in enumerate(TWIN_WEIGHTS):
        w = inp[name].astype(_jnp.float32)
        if MOMENT_SCALE is None:
            s = _jnp.sqrt(_jnp.mean(_jnp.square(w)) + 1e-30)
        else:
            s = MOMENT_SCALE[name]
        km, kv = _jax.random.split(_jax.random.fold_in(key, i + 1))
        out[name] = w
        out["m_" + name] = s * _jax.random.normal(km, w.shape, _jnp.float32)
        out["v_" + name] = (s * s) * _jax.random.uniform(kv, w.shape, _jnp.float32, 0.5, 1.5)
    if N_MICROBATCH > 1:
        for name, axis in PER_EXAMPLE_BATCH_AXIS.items():
            out[name] = _to_microbatches(out[name], axis)
    return {'x': out['x'], 'c': out['c'], 'w_ada': out['w_ada'], 'b_ada': out['b_ada'], 'norm_ffn1': out['norm_ffn1'], 'ffn1_gate': out['ffn1_gate'], 'ffn1_up': out['ffn1_up'], 'ffn1_down': out['ffn1_down'], 'norm_mix': out['norm_mix'], 'w_in': out['w_in'], 'conv_w': out['conv_w'], 'a_log': out['a_log'], 'dt_bias': out['dt_bias'], 'gdn_norm': out['gdn_norm'], 'pool_w': out['pool_w'], 'pool_scale': out['pool_scale'], 'w_out': out['w_out'], 'norm_ffn2': out['norm_ffn2'], 'ffn2_gate': out['ffn2_gate'], 'ffn2_up': out['ffn2_up'], 'ffn2_down': out['ffn2_down'], 'final_norm': out['final_norm'], 'loss_target': out['loss_target'], 'm_w_ada': out['m_w_ada'], 'm_b_ada': out['m_b_ada'], 'm_norm_ffn1': out['m_norm_ffn1'], 'm_ffn1_gate': out['m_ffn1_gate'], 'm_ffn1_up': out['m_ffn1_up'], 'm_ffn1_down': out['m_ffn1_down'], 'm_norm_mix': out['m_norm_mix'], 'm_w_in': out['m_w_in'], 'm_conv_w': out['m_conv_w'], 'm_a_log': out['m_a_log'], 'm_dt_bias': out['m_dt_bias'], 'm_gdn_norm': out['m_gdn_norm'], 'm_pool_w': out['m_pool_w'], 'm_pool_scale': out['m_pool_scale'], 'm_w_out': out['m_w_out'], 'm_norm_ffn2': out['m_norm_ffn2'], 'm_ffn2_gate': out['m_ffn2_gate'], 'm_ffn2_up': out['m_ffn2_up'], 'm_ffn2_down': out['m_ffn2_down'], 'm_final_norm': out['m_final_norm'], 'v_w_ada': out['v_w_ada'], 'v_b_ada': out['v_b_ada'], 'v_norm_ffn1': out['v_norm_ffn1'], 'v_ffn1_gate': out['v_ffn1_gate'], 'v_ffn1_up': out['v_ffn1_up'], 'v_ffn1_down': out['v_ffn1_down'], 'v_norm_mix': out['v_norm_mix'], 'v_w_in': out['v_w_in'], 'v_conv_w': out['v_conv_w'], 'v_a_log': out['v_a_log'], 'v_dt_bias': out['v_dt_bias'], 'v_gdn_norm': out['v_gdn_norm'], 'v_pool_w': out['v_pool_w'], 'v_pool_scale': out['v_pool_scale'], 'v_w_out': out['v_w_out'], 'v_norm_ffn2': out['v_norm_ffn2'], 'v_ffn2_gate': out['v_ffn2_gate'], 'v_ffn2_up': out['v_ffn2_up'], 'v_ffn2_down': out['v_ffn2_down'], 'v_final_norm': out['v_final_norm']}


def _loss(weights, diff, rest, loss_target):
    with _jax.named_scope("forward"):
        args = {**rest, TWIN_DIFF_INPUT: diff, **{k: w.astype(_WEIGHT_DTYPES[k]) for k, w in weights.items()}}
        y = _forward(args)
    with _jax.named_scope("loss_head"):
        err = _jnp.square(y.astype(_jnp.float32) - loss_target)
        return 0.5 * _jnp.sum(_jnp.mean(err, axis=-1)) if err.ndim else 0.5 * err


def _adamw(w, g, m, v):
    m = ADAM_B1 * m + (1.0 - ADAM_B1) * g
    v = ADAM_B2 * v + (1.0 - ADAM_B2) * _jnp.square(g)
    m_hat = m / (1.0 - ADAM_B1 ** ADAM_STEP)
    v_hat = v / (1.0 - ADAM_B2 ** ADAM_STEP)
    delta = -ADAM_LR * (m_hat / (_jnp.sqrt(v_hat) + ADAM_EPS) + ADAM_WD * w)
    return delta, m, v


def reference(x, c, w_ada, b_ada, norm_ffn1, ffn1_gate, ffn1_up, ffn1_down, norm_mix, w_in, conv_w, a_log, dt_bias, gdn_norm, pool_w, pool_scale, w_out, norm_ffn2, ffn2_gate, ffn2_up, ffn2_down, final_norm, loss_target, m_w_ada, m_b_ada, m_norm_ffn1, m_ffn1_gate, m_ffn1_up, m_ffn1_down, m_norm_mix, m_w_in, m_conv_w, m_a_log, m_dt_bias, m_gdn_norm, m_pool_w, m_pool_scale, m_w_out, m_norm_ffn2, m_ffn2_gate, m_ffn2_up, m_ffn2_down, m_final_norm, v_w_ada, v_b_ada, v_norm_ffn1, v_ffn1_gate, v_ffn1_up, v_ffn1_down, v_norm_mix, v_w_in, v_conv_w, v_a_log, v_dt_bias, v_gdn_norm, v_pool_w, v_pool_scale, v_w_out, v_norm_ffn2, v_ffn2_gate, v_ffn2_up, v_ffn2_down, v_final_norm):
    given = dict(x=x, c=c, w_ada=w_ada, b_ada=b_ada, norm_ffn1=norm_ffn1, ffn1_gate=ffn1_gate, ffn1_up=ffn1_up, ffn1_down=ffn1_down, norm_mix=norm_mix, w_in=w_in, conv_w=conv_w, a_log=a_log, dt_bias=dt_bias, gdn_norm=gdn_norm, pool_w=pool_w, pool_scale=pool_scale, w_out=w_out, norm_ffn2=norm_ffn2, ffn2_gate=ffn2_gate, ffn2_up=ffn2_up, ffn2_down=ffn2_down, final_norm=final_norm, loss_target=loss_target, m_w_ada=m_w_ada, m_b_ada=m_b_ada, m_norm_ffn1=m_norm_ffn1, m_ffn1_gate=m_ffn1_gate, m_ffn1_up=m_ffn1_up, m_ffn1_down=m_ffn1_down, m_norm_mix=m_norm_mix, m_w_in=m_w_in, m_conv_w=m_conv_w, m_a_log=m_a_log, m_dt_bias=m_dt_bias, m_gdn_norm=m_gdn_norm, m_pool_w=m_pool_w, m_pool_scale=m_pool_scale, m_w_out=m_w_out, m_norm_ffn2=m_norm_ffn2, m_ffn2_gate=m_ffn2_gate, m_ffn2_up=m_ffn2_up, m_ffn2_down=m_ffn2_down, m_final_norm=m_final_norm, v_w_ada=v_w_ada, v_b_ada=v_b_ada, v_norm_ffn1=v_norm_ffn1, v_ffn1_gate=v_ffn1_gate, v_ffn1_up=v_ffn1_up, v_ffn1_down=v_ffn1_down, v_norm_mix=v_norm_mix, v_w_in=v_w_in, v_conv_w=v_conv_w, v_a_log=v_a_log, v_dt_bias=v_dt_bias, v_gdn_norm=v_gdn_norm, v_pool_w=v_pool_w, v_pool_scale=v_pool_scale, v_w_out=v_w_out, v_norm_ffn2=v_norm_ffn2, v_ffn2_gate=v_ffn2_gate, v_ffn2_up=v_ffn2_up, v_ffn2_down=v_ffn2_down, v_final_norm=v_final_norm)
    weights = {n: given[n] for n in TWIN_WEIGHTS}
    shared = {n: given[n] for n in SHARED_INPUTS}
    per_example = {n: given[n] for n in ['x', 'c']}
    grad_fn = _jax.value_and_grad(_loss, argnums=(0, 1))

    def one_microbatch(ex, loss_target):
        ex = dict(ex)
        diff = ex.pop(TWIN_DIFF_INPUT)
        return grad_fn(weights, diff, {**shared, **ex}, loss_target)

    if N_MICROBATCH == 1:
        loss, (grad_w, grad_x) = one_microbatch(per_example, given["loss_target"])
    else:
        def body(carry, xs):
            loss_sum, grad_sum = carry
            l_k, (gw_k, gx_k) = one_microbatch(xs[0], xs[1])
            with _jax.named_scope("update"):
                return (loss_sum + l_k, _jax.tree.map(_jnp.add, grad_sum, gw_k)), gx_k

        init = (_jnp.zeros((), _jnp.float32), _jax.tree.map(_jnp.zeros_like, weights))
        (loss, grad_w), grad_x = _jax.lax.scan(body, init, (per_example, given["loss_target"]))
    with _jax.named_scope("update"):
        delta_w, new_m, new_v = {}, {}, {}
        for n in TWIN_WEIGHTS:
            delta_w[n], new_m[n], new_v[n] = _adamw(weights[n], grad_w[n], given["m_" + n], given["v_" + n])
    return (loss, grad_x, *[grad_w[n] for n in TWIN_WEIGHTS], *[delta_w[n] for n in TWIN_WEIGHTS],
            *[new_m[n] for n in TWIN_WEIGHTS], *[new_v[n] for n in TWIN_WEIGHTS])
```

```python
import functools

import jax
import jax.numpy as jnp
from jax import lax
from jax.experimental import pallas as pl
from jax.experimental.pallas import tpu as pltpu

F32 = jnp.float32
MXU_DTYPE = jnp.bfloat16
WIRE_DTYPE = jnp.bfloat16
EPS = 1e-6
N_DEV = 8
GDN_HEADS = 4
HEAD_DIM = 128
GDN_WIDTH = GDN_HEADS * HEAD_DIM
POOL_WINDOWS = (2, 4, 8, 16)
POOL_GROUPS = len(POOL_WINDOWS)
POOL_WIDTH = 512
CONV_K = 4
CHUNK = 64
QKV_WIDTH = 3 * GDN_WIDTH
D_IN = 4 * GDN_WIDTH + 2 * GDN_HEADS + POOL_WIDTH
D_IN_PAD = 4 * GDN_WIDTH + POOL_WIDTH + 128
COL_Z = QKV_WIDTH
COL_P = 4 * GDN_WIDTH
COL_BA = 4 * GDN_WIDTH + POOL_WIDTH
N_MOD = 9
HALO = 16
VMEM_LIMIT = 56 * 1024 * 1024
ADAM_LR, ADAM_B1, ADAM_B2, ADAM_EPS, ADAM_WD, ADAM_STEP = 0.001, 0.9, 0.999, 1e-08, 0.01, 10
HI = lax.Precision.HIGHEST

NT = (((1,), (1,)), ((), ()))
NN = (((1,), (0,)), ((), ()))
TN = (((0,), (0,)), ((), ()))


def _params(*sem):
    return pltpu.CompilerParams(dimension_semantics=tuple(sem), vmem_limit_bytes=VMEM_LIMIT)


def _dot(a, b, dims, precision=None):
    return lax.dot_general(a, b, dims, precision=precision, preferred_element_type=F32)


def _mdot(a, b, dims):
    return _dot(a.astype(MXU_DTYPE), b.astype(MXU_DTYPE), dims)


def _sigmoid(v):
    return 1.0 / (1.0 + jnp.exp(-v))


def _softplus(v):
    return jnp.maximum(v, 0.0) + jnp.log(1.0 + jnp.exp(-jnp.abs(v)))


def _shift_rows(v, s):
    n = v.shape[0]
    s = s % n
    return v if s == 0 else pltpu.roll(v, s, 0)


def _tile(n, want):
    t = min(n, want)
    while n % t:
        t //= 2
    return t


def _all_gather(block, name):
    shape, dtype = block.shape, block.dtype

    def body(x_ref, out_ref, send_sems, recv_sems, local_sem):
        x, y, c = lax.axis_index("x"), lax.axis_index("y"), lax.axis_index("c")
        me, sibling = (x, y, c), (x, y, 1 - c)
        chips = [(1 - x, y), (x, 1 - y), (1 - x, 1 - y)]

        def rows(px, py, pc):
            return out_ref.at[4 * px + 2 * py + pc]

        def copy(k, blk, to, src=None):
            return pltpu.make_async_remote_copy(
                src_ref=rows(*blk) if src is None else src, dst_ref=rows(*blk),
                send_sem=send_sems.at[k], recv_sem=recv_sems.at[k],
                device_id=to, device_id_type=pl.DeviceIdType.MESH)

        mine = pltpu.make_async_copy(x_ref, rows(*me), local_sem)
        mine.start()
        first = [copy(0, me, sibling, src=x_ref)]
        first += [copy(1 + j, me, (*chip, c), src=x_ref) for j, chip in enumerate(chips)]
        for cp in first:
            cp.start()
        passed = [copy(4 + j, (*chip, c), sibling) for j, chip in enumerate(chips)]
        for j, chip in enumerate(chips):
            copy(1 + j, (*chip, c), me).wait_recv()
            passed[j].start()
        copy(0, sibling, me).wait_recv()
        for j, chip in enumerate(chips):
            copy(4 + j, (*chip, 1 - c), me).wait_recv()
        for cp in first + passed:
            cp.wait_send()
        mine.wait()

    return pl.pallas_call(
        body, name=name,
        out_shape=jax.ShapeDtypeStruct((N_DEV,) + shape, dtype),
        in_specs=[pl.BlockSpec(memory_space=pltpu.HBM)],
        out_specs=pl.BlockSpec(memory_space=pltpu.HBM),
        scratch_shapes=[pltpu.SemaphoreType.DMA((7,)), pltpu.SemaphoreType.DMA((7,)),
                        pltpu.SemaphoreType.DMA(())],
    )(block)


def _all_to_all(parts, name):
    shape, dtype = parts.shape, parts.dtype
    flips = [(0, 0, 1), (0, 1, 0), (0, 1, 1), (1, 0, 0), (1, 0, 1), (1, 1, 0), (1, 1, 1)]

    def body(x_ref, out_ref, send_sems, recv_sems, local_sem):
        x, y, c = lax.axis_index("x"), lax.axis_index("y"), lax.axis_index("c")
        me = 4 * x + 2 * y + c
        mine = pltpu.make_async_copy(x_ref.at[me], out_ref.at[me], local_sem)
        mine.start()
        copies = []
        for k, (fx, fy, fc) in enumerate(flips):
            px = 1 - x if fx else x
            py = 1 - y if fy else y
            pc = 1 - c if fc else c
            copies.append(pltpu.make_async_remote_copy(
                src_ref=x_ref.at[4 * px + 2 * py + pc], dst_ref=out_ref.at[me],
                send_sem=send_sems.at[k], recv_sem=recv_sems.at[k],
                device_id=(px, py, pc), device_id_type=pl.DeviceIdType.MESH))
        for cp in copies:
            cp.start()
        for cp in copies:
            cp.wait_recv()
        for cp in copies:
            cp.wait_send()
        mine.wait()

    return pl.pallas_call(
        body, name=name,
        out_shape=jax.ShapeDtypeStruct(shape, dtype),
        in_specs=[pl.BlockSpec(memory_space=pltpu.HBM)],
        out_specs=pl.BlockSpec(memory_space=pltpu.HBM),
        scratch_shapes=[pltpu.SemaphoreType.DMA((7,)), pltpu.SemaphoreType.DMA((7,)),
                        pltpu.SemaphoreType.DMA(())],
    )(parts)


def _matmul(pairs, dims, out_dtype, name, tm=512, tn=512, tk=512):
    a0, b0 = pairs[0]
    if dims == TN:
        K, M = a0.shape
    else:
        M, K = a0.shape
    N = b0.shape[0] if dims == NT else b0.shape[1]
    tm, tn, tk = _tile(M, tm), _tile(N, tn), _tile(K, tk)
    nk = K // tk
    n_pairs = len(pairs)

    def body(*refs):
        out_ref, acc_ref = refs[2 * n_pairs], refs[2 * n_pairs + 1]
        k = pl.program_id(2)

        @pl.when(k == 0)
        def _():
            acc_ref[...] = jnp.zeros_like(acc_ref)

        acc = acc_ref[...]
        for p in range(n_pairs):
            acc += _dot(refs[2 * p][...], refs[2 * p + 1][...], dims)
        acc_ref[...] = acc

        @pl.when(k == nk - 1)
        def _():
            out_ref[...] = acc_ref[...].astype(out_ref.dtype)

    if dims == TN:
        a_spec = pl.BlockSpec((tk, tm), lambda i, j, k: (k, i))
    else:
        a_spec = pl.BlockSpec((tm, tk), lambda i, j, k: (i, k))
    if dims == NT:
        b_spec = pl.BlockSpec((tn, tk), lambda i, j, k: (j, k))
    else:
        b_spec = pl.BlockSpec((tk, tn), lambda i, j, k: (k, j))
    args, specs = [], []
    for a, b in pairs:
        args += [a, b]
        specs += [a_spec, b_spec]
    return pl.pallas_call(
        body, name=name, grid=(M // tm, N // tn, nk),
        in_specs=specs, out_specs=pl.BlockSpec((tm, tn), lambda i, j, k: (i, j)),
        out_shape=jax.ShapeDtypeStruct((M, N), out_dtype),
        scratch_shapes=[pltpu.VMEM((tm, tn), F32)],
        compiler_params=_params("parallel", "parallel", "arbitrary"),
    )(*args)


def _vec_spec(d):
    return pl.BlockSpec((1, d), lambda i: (0, 0))


def _norm_mod(x, nw, scale, shift, name):
    T, D = x.shape
    tm = _tile(T, 512)

    def body(x_ref, nw_ref, sc_ref, sh_ref, h_ref):
        xf = x_ref[...]
        r = lax.rsqrt(jnp.mean(xf * xf, axis=-1, keepdims=True) + EPS)
        h_ref[...] = ((xf * r) * nw_ref[...] * (1.0 + sc_ref[...]) + sh_ref[...]).astype(h_ref.dtype)

    row = pl.BlockSpec((tm, D), lambda i: (i, 0))
    return pl.pallas_call(
        body, name=name, grid=(T // tm,),
        in_specs=[row, _vec_spec(D), _vec_spec(D), _vec_spec(D)], out_specs=row,
        out_shape=jax.ShapeDtypeStruct((T, D), MXU_DTYPE),
        compiler_params=_params("parallel"),
    )(x, nw, scale, shift)


def _resid_norm_mod(x, y, gate, coef, nw, scale, shift, name):
    T, D = x.shape
    tm = _tile(T, 512)

    def body(x_ref, y_ref, g_ref, nw_ref, sc_ref, sh_ref, xo_ref, h_ref):
        xf = x_ref[...] + (coef * g_ref[...]) * y_ref[...]
        xo_ref[...] = xf
        r = lax.rsqrt(jnp.mean(xf * xf, axis=-1, keepdims=True) + EPS)
        h_ref[...] = ((xf * r) * nw_ref[...] * (1.0 + sc_ref[...]) + sh_ref[...]).astype(h_ref.dtype)

    row = pl.BlockSpec((tm, D), lambda i: (i, 0))
    return pl.pallas_call(
        body, name=name, grid=(T // tm,),
        in_specs=[row, row, _vec_spec(D), _vec_spec(D), _vec_spec(D), _vec_spec(D)],
        out_specs=[row, row],
        out_shape=[jax.ShapeDtypeStruct((T, D), F32), jax.ShapeDtypeStruct((T, D), MXU_DTYPE)],
        compiler_params=_params("parallel"),
    )(x, y, gate, nw, scale, shift)


def _gate_bwd(dxo, y, gate, coef, name):
    T, D = dxo.shape
    tm = _tile(T, 512)

    def body(d_ref, y_ref, g_ref, dy_ref, dg_ref):
        @pl.when(pl.program_id(0) == 0)
        def _():
            dg_ref[...] = jnp.zeros_like(dg_ref)

        d = d_ref[...]
        dy_ref[...] = ((coef * g_ref[...]) * d).astype(dy_ref.dtype)
        dg_ref[...] += coef * jnp.sum(d * y_ref[...], axis=0, keepdims=True)

    row = pl.BlockSpec((tm, D), lambda i: (i, 0))
    return pl.pallas_call(
        body, name=name, grid=(T // tm,),
        in_specs=[row, row, _vec_spec(D)], out_specs=[row, _vec_spec(D)],
        out_shape=[jax.ShapeDtypeStruct((T, D), MXU_DTYPE), jax.ShapeDtypeStruct((1, D), F32)],
        compiler_params=_params("arbitrary"),
    )(dxo, y, gate)


def _norm_bwd(dh, x, nw, scale, dres, name):
    T, D = x.shape
    tm = _tile(T, 512)

    def body(dh_ref, x_ref, nw_ref, sc_ref, dr_ref, dx_ref, dnw_ref, dsc_ref, dsh_ref):
        @pl.when(pl.program_id(0) == 0)
        def _():
            dnw_ref[...] = jnp.zeros_like(dnw_ref)
            dsc_ref[...] = jnp.zeros_like(dsc_ref)
            dsh_ref[...] = jnp.zeros_like(dsh_ref)

        xf, dh_ = x_ref[...], dh_ref[...]
        r = lax.rsqrt(jnp.mean(xf * xf, axis=-1, keepdims=True) + EPS)
        xn = xf * r
        one_sc = 1.0 + sc_ref[...]
        dsh_ref[...] += jnp.sum(dh_, axis=0, keepdims=True)
        t = dh_ * xn
        dsc_ref[...] += jnp.sum(t, axis=0, keepdims=True) * nw_ref[...]
        dnw_ref[...] += jnp.sum(t, axis=0, keepdims=True) * one_sc
        dxn = dh_ * (nw_ref[...] * one_sc)
        dx_ref[...] = dr_ref[...] + r * (dxn - xn * jnp.mean(dxn * xn, axis=-1, keepdims=True))

    row = pl.BlockSpec((tm, D), lambda i: (i, 0))
    vec = _vec_spec(D)
    return pl.pallas_call(
        body, name=name, grid=(T // tm,),
        in_specs=[row, row, vec, vec, row], out_specs=[row, vec, vec, vec],
        out_shape=[jax.ShapeDtypeStruct((T, D), F32)] + [jax.ShapeDtypeStruct((1, D), F32)] * 3,
        compiler_params=_params("arbitrary"),
    )(dh, x, nw, scale, dres)


def _final_loss(x, y, gate, fw, target, name):
    T, D = x.shape
    tm = _tile(T, 512)

    def body(x_ref, y_ref, g_ref, fw_ref, t_ref, loss_ref, dx_ref, dfw_ref):
        @pl.when(pl.program_id(0) == 0)
        def _():
            loss_ref[...] = jnp.zeros_like(loss_ref)
            dfw_ref[...] = jnp.zeros_like(dfw_ref)

        xf = x_ref[...] + (0.5 * g_ref[...]) * y_ref[...]
        r = lax.rsqrt(jnp.mean(xf * xf, axis=-1, keepdims=True) + EPS)
        xn = xf * r
        err = xn * fw_ref[...] - t_ref[...]
        per_tok = jnp.mean(err * err, axis=-1, keepdims=True)
        loss_ref[...] += 0.5 * jnp.sum(per_tok, axis=0, keepdims=True)
        dy = err * (1.0 / D)
        dfw_ref[...] += jnp.sum(dy * xn, axis=0, keepdims=True)
        dxn = dy * fw_ref[...]
        dx_ref[...] = r * (dxn - xn * jnp.mean(dxn * xn, axis=-1, keepdims=True))

    row = pl.BlockSpec((tm, D), lambda i: (i, 0))
    vec = _vec_spec(D)
    return pl.pallas_call(
        body, name=name, grid=(T // tm,),
        in_specs=[row, row, vec, vec, row],
        out_specs=[pl.BlockSpec((1, 128), lambda i: (0, 0)), row, vec],
        out_shape=[jax.ShapeDtypeStruct((1, 128), F32), jax.ShapeDtypeStruct((T, D), F32),
                   jax.ShapeDtypeStruct((1, D), F32)],
        compiler_params=_params("arbitrary"),
    )(x, y, gate, fw, target)


def _ffn_fwd(h, wg_t, wu_t, wd, name):
    T, D = h.shape
    Fdim = wd.shape[0]
    tm, tf = _tile(T, 1024), _tile(Fdim, 256)
    nf = Fdim // tf

    def body(h_ref, wg_ref, wu_ref, wd_ref, y_ref, g_ref, u_ref, a_ref, acc_ref):
        k = pl.program_id(1)

        @pl.when(k == 0)
        def _():
            acc_ref[...] = jnp.zeros_like(acc_ref)

        hh = h_ref[...]
        g = _dot(hh, wg_ref[...], NT)
        u = _dot(hh, wu_ref[...], NT)
        a = ((g * _sigmoid(g)) * u).astype(a_ref.dtype)
        g_ref[...] = g
        u_ref[...] = u
        a_ref[...] = a
        acc_ref[...] += _dot(a, wd_ref[...], NN)

        @pl.when(k == nf - 1)
        def _():
            y_ref[...] = acc_ref[...]

    hrow = pl.BlockSpec((tm, D), lambda i, k: (i, 0))
    wspec = pl.BlockSpec((tf, D), lambda i, k: (k, 0))
    fspec = pl.BlockSpec((tm, tf), lambda i, k: (i, k))
    return pl.pallas_call(
        body, name=name, grid=(T // tm, nf),
        in_specs=[hrow, wspec, wspec, wspec], out_specs=[hrow, fspec, fspec, fspec],
        out_shape=[jax.ShapeDtypeStruct((T, D), F32), jax.ShapeDtypeStruct((T, Fdim), F32),
                   jax.ShapeDtypeStruct((T, Fdim), F32), jax.ShapeDtypeStruct((T, Fdim), MXU_DTYPE)],
        scratch_shapes=[pltpu.VMEM((tm, D), F32)],
        compiler_params=_params("parallel", "arbitrary"),
    )(h, wg_t, wu_t, wd)


def _ffn_bwd_act(dy, wd, g, u, name):
    T, D = dy.shape
    Fdim = wd.shape[0]
    tm, tf = _tile(T, 1024), _tile(Fdim, 256)

    def body(dy_ref, wd_ref, g_ref, u_ref, dg_ref, du_ref):
        da = _dot(dy_ref[...], wd_ref[...], NT)
        gg = g_ref[...]
        sig = _sigmoid(gg)
        dg_ref[...] = (da * u_ref[...] * (sig * (1.0 + gg * (1.0 - sig)))).astype(dg_ref.dtype)
        du_ref[...] = (da * (gg * sig)).astype(du_ref.dtype)

    fspec = pl.BlockSpec((tm, tf), lambda i, k: (i, k))
    return pl.pallas_call(
        body, name=name, grid=(T // tm, Fdim // tf),
        in_specs=[pl.BlockSpec((tm, D), lambda i, k: (i, 0)), pl.BlockSpec((tf, D), lambda i, k: (k, 0)),
                  fspec, fspec],
        out_specs=[fspec, fspec],
        out_shape=[jax.ShapeDtypeStruct((T, Fdim), MXU_DTYPE)] * 2,
        compiler_params=_params("parallel", "parallel"),
    )(dy, wd, g, u)


def _conv_act(window, w):
    y = window * w[CONV_K - 1:CONV_K, :]
    for j in range(CONV_K - 1):
        y += _shift_rows(window, CONV_K - 1 - j) * w[j:j + 1, :]
    return y


def _gdn_prep(proj, conv_w, a_log_l, dt_bias_l, name):
    T = proj.shape[0]
    tm = _tile(T, 256)
    hb = tm // 8

    def body(cur_ref, halo_ref, ba_ref, w_ref, al_ref, dtb_ref, qkv_ref, bg_ref):
        i = pl.program_id(0)
        halo = jnp.where(i == 0, 0.0, halo_ref[...])
        window = jnp.concatenate([halo, cur_ref[...]], axis=0)
        y = _conv_act(window, w_ref[...])[8:, :]
        act = y * _sigmoid(y)
        for hh in range(3 * GDN_HEADS):
            blk = act[:, hh * HEAD_DIM:(hh + 1) * HEAD_DIM]
            if hh < 2 * GDN_HEADS:
                rn = lax.rsqrt(jnp.sum(blk * blk, axis=-1, keepdims=True) + EPS)
                blk = blk * rn
                if hh < GDN_HEADS:
                    blk = blk * (HEAD_DIM ** -0.5)
            qkv_ref[:, hh * HEAD_DIM:(hh + 1) * HEAD_DIM] = blk
        ba = ba_ref[...]
        lane = lax.broadcasted_iota(jnp.int32, ba.shape, 1)
        beta = _sigmoid(ba)
        g = -jnp.exp(al_ref[...]) * _softplus(ba + dtb_ref[...])
        bg_ref[...] = jnp.where(lane < GDN_HEADS, beta, jnp.where(lane < 2 * GDN_HEADS, g, 0.0))

    return pl.pallas_call(
        body, name=name, grid=(T // tm,),
        in_specs=[pl.BlockSpec((tm, QKV_WIDTH), lambda i: (i, 0)),
                  pl.BlockSpec((8, QKV_WIDTH), lambda i: (jnp.maximum(i * hb - 1, 0), 0)),
                  pl.BlockSpec((tm, 128), lambda i: (i, COL_BA // 128)),
                  pl.BlockSpec((CONV_K, QKV_WIDTH), lambda i: (0, 0)),
                  pl.BlockSpec((1, 128), lambda i: (0, 0)), pl.BlockSpec((1, 128), lambda i: (0, 0))],
        out_specs=[pl.BlockSpec((tm, QKV_WIDTH), lambda i: (i, 0)), pl.BlockSpec((tm, 128), lambda i: (i, 0))],
        out_shape=[jax.ShapeDtypeStruct((T, QKV_WIDTH), F32), jax.ShapeDtypeStruct((T, 128), F32)],
        compiler_params=_params("parallel"),
    )(proj, proj, proj, conv_w, a_log_l, dt_bias_l)


def _chunk_cumsum(v, reverse=False):
    row = lax.broadcasted_iota(jnp.int32, v.shape, 0)
    s = 1
    while s < CHUNK:
        if reverse:
            v = v + jnp.where(row < CHUNK - s, _shift_rows(v, -s), 0.0)
        else:
            v = v + jnp.where(row >= s, _shift_rows(v, s), 0.0)
        s *= 2
    return v


def _row_form(cols):
    padded = jnp.concatenate([cols, jnp.zeros((128 - CHUNK, 128), F32)], axis=0)
    return padded.T[:, :CHUNK]


def _chunk_masks():
    ri = lax.broadcasted_iota(jnp.int32, (CHUNK, CHUNK), 0)
    ci = lax.broadcasted_iota(jnp.int32, (CHUNK, CHUNK), 1)
    return ri >= ci, ri > ci, (ri == ci).astype(F32)


def _unit_lower_inverse(m, eye):
    n = -m
    r = eye + n
    p = n
    s = 2
    while s < CHUNK:
        p = _dot(p, p, NN, HI)
        r = r + _dot(p, r, NN, HI)
        s *= 2
    return r


def _head_common(q, k, v, beta, gc, gcr, masks):
    causal, strict, _ = masks
    diff = gc - gcr
    decay = jnp.where(causal, jnp.exp(jnp.where(causal, diff, 0.0)), 0.0)
    kb = k * beta
    kk = _mdot(kb, k, NT)
    m = jnp.where(strict, kk * decay, 0.0)
    eg = jnp.exp(gc)
    return decay, kb, kk, m, eg


def _gdn_fwd(qkv, bg, name):
    T = qkv.shape[0]
    n_chunks = T // CHUNK

    def body(qkv_ref, bg_ref, o_ref, tinv_ref, s_all_ref, s_ref):
        @pl.when(pl.program_id(0) == 0)
        def _():
            s_ref[...] = jnp.zeros_like(s_ref)

        masks = _chunk_masks()
        causal, _, eye = masks
        bgv = bg_ref[...]
        gc_all = _chunk_cumsum(bgv)
        gc_rows = _row_form(gc_all)
        for hh in range(GDN_HEADS):
            q = qkv_ref[:, hh * HEAD_DIM:(hh + 1) * HEAD_DIM]
            k = qkv_ref[:, GDN_WIDTH + hh * HEAD_DIM:GDN_WIDTH + (hh + 1) * HEAD_DIM]
            v = qkv_ref[:, 2 * GDN_WIDTH + hh * HEAD_DIM:2 * GDN_WIDTH + (hh + 1) * HEAD_DIM]
            beta = bgv[:, hh:hh + 1]
            gc = gc_all[:, GDN_HEADS + hh:GDN_HEADS + hh + 1]
            gcr = gc_rows[GDN_HEADS + hh:GDN_HEADS + hh + 1, :]
            decay, kb, _, m, eg = _head_common(q, k, v, beta, gc, gcr, masks)
            tinv = _unit_lower_inverse(m, eye)
            u = _dot(tinv, v * beta, NN, HI)
            w = _dot(tinv, kb * eg, NN, HI)
            p = jnp.where(causal, _mdot(q, k, NT) * decay, 0.0)
            gl = gc[CHUNK - 1:CHUNK, :]
            s_old = s_ref[hh]
            v_new = u - _mdot(w, s_old, NN)
            o_ref[:, hh * HEAD_DIM:(hh + 1) * HEAD_DIM] = _mdot(q * eg, s_old, NN) + _mdot(p, v_new, NN)
            s_ref[hh] = s_old * jnp.exp(gl) + _mdot(k * jnp.exp(gl - gc), v_new, TN)
            s_all_ref[hh * HEAD_DIM:(hh + 1) * HEAD_DIM, :] = s_old
            tinv_ref[:, hh * CHUNK:(hh + 1) * CHUNK] = tinv

    return pl.pallas_call(
        body, name=name, grid=(n_chunks,),
        in_specs=[pl.BlockSpec((CHUNK, QKV_WIDTH), lambda n: (n, 0)), pl.BlockSpec((CHUNK, 128), lambda n: (n, 0))],
        out_specs=[pl.BlockSpec((CHUNK, GDN_WIDTH), lambda n: (n, 0)),
                   pl.BlockSpec((CHUNK, GDN_HEADS * CHUNK), lambda n: (n, 0)),
                   pl.BlockSpec((GDN_WIDTH, HEAD_DIM), lambda n: (n, 0))],
        out_shape=[jax.ShapeDtypeStruct((T, GDN_WIDTH), F32),
                   jax.ShapeDtypeStruct((T, GDN_HEADS * CHUNK), F32),
                   jax.ShapeDtypeStruct((n_chunks * GDN_WIDTH, HEAD_DIM), F32)],
        scratch_shapes=[pltpu.VMEM((GDN_HEADS, HEAD_DIM, HEAD_DIM), F32)],
        compiler_params=_params("arbitrary"),
    )(qkv, bg)


def _gdn_bwd(qkv, bg, tinv_all, s_all, do, name):
    T = qkv.shape[0]
    n_chunks = T // CHUNK

    def body(qkv_ref, bg_ref, tinv_ref, s_all_ref, do_ref, dqkv_ref, dbg_ref, ds_ref):
        @pl.when(pl.program_id(0) == 0)
        def _():
            ds_ref[...] = jnp.zeros_like(ds_ref)

        masks = _chunk_masks()
        causal, strict, _ = masks
        bgv = bg_ref[...]
        gc_all = _chunk_cumsum(bgv)
        gc_rows = _row_form(gc_all)
        lane = lax.broadcasted_iota(jnp.int32, (CHUNK, 128), 1)
        row = lax.broadcasted_iota(jnp.int32, (CHUNK, 128), 0)
        dbeta_tile = jnp.zeros((CHUNK, 128), F32)
        dgc_tile = jnp.zeros((CHUNK, 128), F32)
        for hh in range(GDN_HEADS):
            q = qkv_ref[:, hh * HEAD_DIM:(hh + 1) * HEAD_DIM]
            k = qkv_ref[:, GDN_WIDTH + hh * HEAD_DIM:GDN_WIDTH + (hh + 1) * HEAD_DIM]
            v = qkv_ref[:, 2 * GDN_WIDTH + hh * HEAD_DIM:2 * GDN_WIDTH + (hh + 1) * HEAD_DIM]
            d_o = do_ref[:, hh * HEAD_DIM:(hh + 1) * HEAD_DIM]
            beta = bgv[:, hh:hh + 1]
            gc = gc_all[:, GDN_HEADS + hh:GDN_HEADS + hh + 1]
            gcr = gc_rows[GDN_HEADS + hh:GDN_HEADS + hh + 1, :]
            decay, kb, kk, m, eg = _head_common(q, k, v, beta, gc, gcr, masks)
            tinv = tinv_ref[:, hh * CHUNK:(hh + 1) * CHUNK]
            s_old = s_all_ref[hh * HEAD_DIM:(hh + 1) * HEAD_DIM, :]
            ds_new = ds_ref[hh]
            kbeg = kb * eg
            u = _dot(tinv, v * beta, NN, HI)
            w = _dot(tinv, kbeg, NN, HI)
            qk = _mdot(q, k, NT)
            p = jnp.where(causal, qk * decay, 0.0)
            gl = gc[CHUNK - 1:CHUNK, :]
            qd = q * eg
            ek = jnp.exp(gl - gc)
            kd = k * ek
            cd = jnp.exp(gl)
            v_new = u - _mdot(w, s_old, NN)

            dv_new = _mdot(p, d_o, TN) + _mdot(kd, ds_new, NN)
            dp = jnp.where(causal, _mdot(d_o, v_new, NT), 0.0)
            dqd = _mdot(d_o, s_old, NT)
            ds_ref[hh] = cd * ds_new + _mdot(qd, d_o, TN) - _mdot(w, dv_new, TN)
            dcd = jnp.sum(jnp.sum(s_old * ds_new, axis=1, keepdims=True), axis=0, keepdims=True)
            dkd = _mdot(v_new, ds_new, NT)
            dw = -_mdot(dv_new, s_old, NT)
            dvb = _dot(tinv, dv_new, TN, HI)
            dkbeg = _dot(tinv, dw, TN, HI)
            da = -(_mdot(dvb, u, NT) + _mdot(dkbeg, w, NT))
            dm = jnp.where(strict, da, 0.0)
            dkk = dm * decay
            dkb = _mdot(dkk, k, NN) + dkbeg * eg
            dk = _mdot(dkk, kb, TN)
            dqk = dp * decay
            dq = _mdot(dqk, k, NN) + dqd * eg
            dk += _mdot(dqk, q, TN)
            e = dm * m + dp * p
            deg = jnp.sum(dqd * q, axis=1, keepdims=True) + jnp.sum(dkbeg * kb, axis=1, keepdims=True)
            dek = jnp.sum(dkd * k, axis=1, keepdims=True)
            dk += dkd * ek
            dgl = jnp.sum(dek * ek, axis=0, keepdims=True) + dcd * cd
            dgc = jnp.sum(e, axis=1, keepdims=True) + deg * eg - dek * ek
            onehot = (lane == GDN_HEADS + hh).astype(F32)
            dgc_tile += jnp.where(lane == GDN_HEADS + hh, dgc, 0.0) - _dot(e, onehot, TN, HI)
            dgc_tile += jnp.where((lane == GDN_HEADS + hh) & (row == CHUNK - 1), dgl, 0.0)
            dbeta = jnp.sum(dkb * k, axis=1, keepdims=True) + jnp.sum(dvb * v, axis=1, keepdims=True)
            dbeta_tile += jnp.where(lane == hh, dbeta, 0.0)
            dk += dkb * beta
            dqkv_ref[:, hh * HEAD_DIM:(hh + 1) * HEAD_DIM] = dq
            dqkv_ref[:, GDN_WIDTH + hh * HEAD_DIM:GDN_WIDTH + (hh + 1) * HEAD_DIM] = dk
            dqkv_ref[:, 2 * GDN_WIDTH + hh * HEAD_DIM:2 * GDN_WIDTH + (hh + 1) * HEAD_DIM] = dvb * beta
        dbg_ref[...] = dbeta_tile + _chunk_cumsum(dgc_tile, reverse=True)

    rev = lambda n: (n_chunks - 1 - n, 0)
    return pl.pallas_call(
        body, name=name, grid=(n_chunks,),
        in_specs=[pl.BlockSpec((CHUNK, QKV_WIDTH), rev), pl.BlockSpec((CHUNK, 128), rev),
                  pl.BlockSpec((CHUNK, GDN_HEADS * CHUNK), rev), pl.BlockSpec((GDN_WIDTH, HEAD_DIM), rev),
                  pl.BlockSpec((CHUNK, GDN_WIDTH), rev)],
        out_specs=[pl.BlockSpec((CHUNK, QKV_WIDTH), rev), pl.BlockSpec((CHUNK, 128), rev)],
        out_shape=[jax.ShapeDtypeStruct((T, QKV_WIDTH), F32), jax.ShapeDtypeStruct((T, 128), F32)],
        scratch_shapes=[pltpu.VMEM((GDN_HEADS, HEAD_DIM, HEAD_DIM), F32)],
        compiler_params=_params("arbitrary"),
    )(qkv, bg, tinv_all, s_all, do)


def _pool_counts(i, tm, rows, offset):
    t = i * tm - offset + lax.broadcasted_iota(jnp.int32, (rows, 1), 0)
    return [jnp.minimum(t + 1, w).astype(F32) for w in POOL_WINDOWS]


def _window_sums(window, forward):
    sums, s, step = [], window, 1
    for _ in POOL_WINDOWS:
        s = s + _shift_rows(s, -step if forward else step)
        sums.append(s)
        step *= 2
    return sums


def _pooled(window, counts):
    sums = _window_sums(window, forward=False)
    out = []
    for gi in range(POOL_GROUPS):
        sl = slice(gi * 128, (gi + 1) * 128)
        out.append(sums[gi][HALO:, sl] / counts[gi] - window[HALO:, sl])
    return out


def _mix_post(o, proj, gdn_norm, pool_w, pool_scale, name):
    T = o.shape[0]
    tm = _tile(T, 256)
    hb = tm // HALO

    def body(o_ref, z_ref, p_ref, ph_ref, gn_ref, pw_ref, ps_ref, out_ref):
        i = pl.program_id(0)
        for hh in range(GDN_HEADS):
            sl = slice(hh * HEAD_DIM, (hh + 1) * HEAD_DIM)
            oh, zh = o_ref[:, sl], z_ref[:, sl]
            ro = lax.rsqrt(jnp.mean(oh * oh, axis=-1, keepdims=True) + EPS)
            out_ref[:, sl] = (((oh * ro) * gn_ref[...]) * (zh * _sigmoid(zh))).astype(out_ref.dtype)
        halo = jnp.where(i == 0, 0.0, ph_ref[...])
        window = jnp.concatenate([halo, p_ref[...]], axis=0)
        pooled = _pooled(window, _pool_counts(i, tm, tm, 0))
        for gi in range(POOL_GROUPS):
            pm = _mdot(pooled[gi], pw_ref[gi], NN)
            out_ref[:, GDN_WIDTH + gi * 128:GDN_WIDTH + (gi + 1) * 128] = (
                pm * ps_ref[:, gi * 128:(gi + 1) * 128]).astype(out_ref.dtype)

    return pl.pallas_call(
        body, name=name, grid=(T // tm,),
        in_specs=[pl.BlockSpec((tm, GDN_WIDTH), lambda i: (i, 0)),
                  pl.BlockSpec((tm, GDN_WIDTH), lambda i: (i, COL_Z // GDN_WIDTH)),
                  pl.BlockSpec((tm, POOL_WIDTH), lambda i: (i, COL_P // POOL_WIDTH)),
                  pl.BlockSpec((HALO, POOL_WIDTH), lambda i: (jnp.maximum(i * hb - 1, 0), COL_P // POOL_WIDTH)),
                  pl.BlockSpec((1, HEAD_DIM), lambda i: (0, 0)),
                  pl.BlockSpec((POOL_GROUPS, 128, 128), lambda i: (0, 0, 0)),
                  pl.BlockSpec((1, POOL_WIDTH), lambda i: (0, 0))],
        out_specs=pl.BlockSpec((tm, GDN_WIDTH + POOL_WIDTH), lambda i: (i, 0)),
        out_shape=jax.ShapeDtypeStruct((T, GDN_WIDTH + POOL_WIDTH), MXU_DTYPE),
        compiler_params=_params("parallel"),
    )(o, proj, proj, proj, gdn_norm, pool_w, pool_scale)


def _mix_post_bwd(dmix, o, proj, gdn_norm, pool_w, pool_scale, name):
    T = o.shape[0]
    tm = _tile(T, 256)
    hb = tm // HALO
    n_tiles = T // tm

    def body(dg_ref, dpo_ref, dpo_next_ref, o_ref, z_ref, p_ref, ph_ref, gn_ref, pw_ref, ps_ref,
             do_ref, dzp_ref, dgn_ref, dpw_ref, dps_ref):
        i = pl.program_id(0)

        @pl.when(i == 0)
        def _():
            dgn_ref[...] = jnp.zeros_like(dgn_ref)
            dpw_ref[...] = jnp.zeros_like(dpw_ref)
            dps_ref[...] = jnp.zeros_like(dps_ref)

        gn = gn_ref[...]
        dgn = jnp.zeros((1, HEAD_DIM), F32)
        for hh in range(GDN_HEADS):
            sl = slice(hh * HEAD_DIM, (hh + 1) * HEAD_DIM)
            oh, zh, dy = o_ref[:, sl], z_ref[:, sl], dg_ref[:, sl]
            ro = lax.rsqrt(jnp.mean(oh * oh, axis=-1, keepdims=True) + EPS)
            on = oh * ro
            sig = _sigmoid(zh)
            sz = zh * sig
            dzp_ref[:, sl] = (dy * (on * gn) * (sig * (1.0 + zh * (1.0 - sig)))).astype(dzp_ref.dtype)
            dgn += jnp.sum(dy * on * sz, axis=0, keepdims=True)
            don = dy * gn * sz
            do_ref[:, sl] = ro * (don - on * jnp.mean(don * on, axis=-1, keepdims=True))
        dgn_ref[...] += dgn

        halo = jnp.where(i == 0, 0.0, ph_ref[...])
        window = jnp.concatenate([halo, p_ref[...]], axis=0)
        counts = _pool_counts(i, tm, tm + HALO, 0)
        pooled = _pooled(window, [cn[:tm] for cn in counts])
        nxt = jnp.where(i == n_tiles - 1, 0.0, dpo_next_ref[...])
        dpo_w = jnp.concatenate([dpo_ref[...], nxt], axis=0)
        ps = ps_ref[...]
        dps = []
        scaled = []
        for gi in range(POOL_GROUPS):
            sl = slice(gi * 128, (gi + 1) * 128)
            dpm = dpo_w[:, sl] * ps[:, sl]
            pm = _mdot(pooled[gi], pw_ref[gi], NN)
            dps.append(jnp.sum(dpo_w[:tm, sl] * pm, axis=0, keepdims=True))
            dpw_ref[gi] += _mdot(pooled[gi], dpm[:tm], TN)
            dpooled = _mdot(dpm, pw_ref[gi], NT)
            scaled.append((dpooled, dpooled / counts[gi]))
        dps_ref[...] += jnp.concatenate(dps, axis=1)
        lead = _window_sums(jnp.concatenate([sc for _, sc in scaled], axis=1), forward=True)
        for gi in range(POOL_GROUPS):
            sl = slice(gi * 128, (gi + 1) * 128)
            dzp_ref[:, GDN_WIDTH + gi * 128:GDN_WIDTH + (gi + 1) * 128] = (
                lead[gi][:tm, sl] - scaled[gi][0][:tm]).astype(dzp_ref.dtype)

    last_halo = T // HALO - 1
    return pl.pallas_call(
        body, name=name, grid=(n_tiles,),
        in_specs=[pl.BlockSpec((tm, GDN_WIDTH), lambda i: (i, 0)),
                  pl.BlockSpec((tm, POOL_WIDTH), lambda i: (i, 1)),
                  pl.BlockSpec((HALO, POOL_WIDTH), lambda i: (jnp.minimum((i + 1) * hb, last_halo), 1)),
                  pl.BlockSpec((tm, GDN_WIDTH), lambda i: (i, 0)),
                  pl.BlockSpec((tm, GDN_WIDTH), lambda i: (i, COL_Z // GDN_WIDTH)),
                  pl.BlockSpec((tm, POOL_WIDTH), lambda i: (i, COL_P // POOL_WIDTH)),
                  pl.BlockSpec((HALO, POOL_WIDTH), lambda i: (jnp.maximum(i * hb - 1, 0), COL_P // POOL_WIDTH)),
                  pl.BlockSpec((1, HEAD_DIM), lambda i: (0, 0)),
                  pl.BlockSpec((POOL_GROUPS, 128, 128), lambda i: (0, 0, 0)),
                  pl.BlockSpec((1, POOL_WIDTH), lambda i: (0, 0))],
        out_specs=[pl.BlockSpec((tm, GDN_WIDTH), lambda i: (i, 0)),
                   pl.BlockSpec((tm, GDN_WIDTH + POOL_WIDTH), lambda i: (i, 0)),
                   pl.BlockSpec((1, HEAD_DIM), lambda i: (0, 0)),
                   pl.BlockSpec((POOL_GROUPS, 128, 128), lambda i: (0, 0, 0)),
                   pl.BlockSpec((1, POOL_WIDTH), lambda i: (0, 0))],
        out_shape=[jax.ShapeDtypeStruct((T, GDN_WIDTH), F32),
                   jax.ShapeDtypeStruct((T, GDN_WIDTH + POOL_WIDTH), MXU_DTYPE),
                   jax.ShapeDtypeStruct((1, HEAD_DIM), F32),
                   jax.ShapeDtypeStruct((POOL_GROUPS, 128, 128), F32),
                   jax.ShapeDtypeStruct((1, POOL_WIDTH), F32)],
        compiler_params=_params("arbitrary"),
    )(dmix, dmix, dmix, o, proj, proj, proj, gdn_norm, pool_w, pool_scale)


def _gdn_prep_bwd(proj, conv_w, a_log_l, dt_bias_l, dqkv, dbg, dzp, name):
    T = proj.shape[0]
    tm = _tile(T, 256)
    hb = tm // 8
    n_tiles = T // tm
    last_halo = T // 8 - 1

    def body(cur_ref, before_ref, after_ref, ba_ref, w_ref, al_ref, dtb_ref, dq_ref, dq_after_ref, dbg_ref,
             dzp_ref, dproj_ref, dw_ref, dal_ref, ddtb_ref):
        i = pl.program_id(0)

        @pl.when(i == 0)
        def _():
            dw_ref[...] = jnp.zeros_like(dw_ref)
            dal_ref[...] = jnp.zeros_like(dal_ref)
            ddtb_ref[...] = jnp.zeros_like(ddtb_ref)

        last = i == n_tiles - 1
        w = w_ref[...]
        before = jnp.where(i == 0, 0.0, before_ref[...])
        after = jnp.where(last, 0.0, after_ref[...])
        window = jnp.concatenate([before, cur_ref[...], after], axis=0)
        y = _conv_act(window, w)
        sig = _sigmoid(y)
        act = y * sig
        dq_w = jnp.concatenate([jnp.zeros((8, QKV_WIDTH), F32), dq_ref[...],
                                jnp.where(last, 0.0, dq_after_ref[...])], axis=0)
        dact = []
        for hh in range(3 * GDN_HEADS):
            sl = slice(hh * HEAD_DIM, (hh + 1) * HEAD_DIM)
            blk, dblk = act[:, sl], dq_w[:, sl]
            if hh < 2 * GDN_HEADS:
                rn = lax.rsqrt(jnp.sum(blk * blk, axis=-1, keepdims=True) + EPS)
                unit = blk * rn
                if hh < GDN_HEADS:
                    dblk = dblk * (HEAD_DIM ** -0.5)
                dblk = rn * (dblk - unit * jnp.sum(dblk * unit, axis=-1, keepdims=True))
            dact.append(dblk)
        dy = jnp.concatenate(dact, axis=1) * (sig * (1.0 + y * (1.0 - sig)))
        dx = dy * w[CONV_K - 1:CONV_K, :]
        dws = [None] * CONV_K
        dws[CONV_K - 1] = jnp.sum(dy[8:8 + tm] * window[8:8 + tm], axis=0, keepdims=True)
        for j in range(CONV_K - 1):
            s = CONV_K - 1 - j
            dx += _shift_rows(dy, -s) * w[j:j + 1, :]
            dws[j] = jnp.sum(dy[8:8 + tm] * _shift_rows(window, s)[8:8 + tm], axis=0, keepdims=True)
        dw_ref[...] += jnp.concatenate(dws, axis=0)
        dproj_ref[:, :QKV_WIDTH] = dx[8:8 + tm].astype(dproj_ref.dtype)
        dproj_ref[:, COL_Z:COL_BA] = dzp_ref[...]

        ba = ba_ref[...]
        dbg_ = dbg_ref[...]
        lane = lax.broadcasted_iota(jnp.int32, ba.shape, 1)
        beta = _sigmoid(ba)
        pre = ba + dtb_ref[...]
        neg_a = -jnp.exp(al_ref[...])
        g = neg_a * _softplus(pre)
        is_g = (lane >= GDN_HEADS) & (lane < 2 * GDN_HEADS)
        da_raw = jnp.where(is_g, dbg_ * neg_a * _sigmoid(pre), 0.0)
        dba = jnp.where(lane < GDN_HEADS, dbg_ * beta * (1.0 - beta), da_raw)
        dproj_ref[:, COL_BA:] = dba.astype(dproj_ref.dtype)
        dal_ref[...] += jnp.sum(jnp.where(is_g, dbg_ * g, 0.0), axis=0, keepdims=True)
        ddtb_ref[...] += jnp.sum(da_raw, axis=0, keepdims=True)

    lane_vec = pl.BlockSpec((1, 128), lambda i: (0, 0))
    return pl.pallas_call(
        body, name=name, grid=(n_tiles,),
        in_specs=[pl.BlockSpec((tm, QKV_WIDTH), lambda i: (i, 0)),
                  pl.BlockSpec((8, QKV_WIDTH), lambda i: (jnp.maximum(i * hb - 1, 0), 0)),
                  pl.BlockSpec((8, QKV_WIDTH), lambda i: (jnp.minimum((i + 1) * hb, last_halo), 0)),
                  pl.BlockSpec((tm, 128), lambda i: (i, COL_BA // 128)),
                  pl.BlockSpec((CONV_K, QKV_WIDTH), lambda i: (0, 0)), lane_vec, lane_vec,
                  pl.BlockSpec((tm, QKV_WIDTH), lambda i: (i, 0)),
                  pl.BlockSpec((8, QKV_WIDTH), lambda i: (jnp.minimum((i + 1) * hb, last_halo), 0)),
                  pl.BlockSpec((tm, 128), lambda i: (i, 0)),
                  pl.BlockSpec((tm, GDN_WIDTH + POOL_WIDTH), lambda i: (i, 0))],
        out_specs=[pl.BlockSpec((tm, D_IN_PAD), lambda i: (i, 0)),
                   pl.BlockSpec((CONV_K, QKV_WIDTH), lambda i: (0, 0)), lane_vec, lane_vec],
        out_shape=[jax.ShapeDtypeStruct((T, D_IN_PAD), MXU_DTYPE),
                   jax.ShapeDtypeStruct((CONV_K, QKV_WIDTH), F32),
                   jax.ShapeDtypeStruct((1, 128), F32), jax.ShapeDtypeStruct((1, 128), F32)],
        compiler_params=_params("arbitrary"),
    )(proj, proj, proj, proj, conv_w, a_log_l, dt_bias_l, dqkv, dqkv, dbg, dzp)


def _mod_part(c_all, w_ada, b_part, name):
    def body(c_ref, w_ref, b_ref, out_ref):
        cc = c_ref[...]
        out_ref[...] = _mdot(cc * _sigmoid(cc), w_ref[...], NN) + b_ref[...]

    return pl.pallas_call(
        body, name=name, out_shape=jax.ShapeDtypeStruct((c_all.shape[0], w_ada.shape[1]), F32),
        compiler_params=_params(),
    )(c_all, w_ada, b_part)


def _w_ada_grad(c_all, dmod_part, name):
    def body(c_ref, d_ref, out_ref):
        cc = c_ref[...]
        out_ref[...] = _mdot(cc * _sigmoid(cc), d_ref[...], TN)

    return pl.pallas_call(
        body, name=name, out_shape=jax.ShapeDtypeStruct((c_all.shape[1], dmod_part.shape[1]), F32),
        compiler_params=_params(),
    )(c_all, dmod_part)


def _sum_parts(parts, name):
    _, R, C = parts.shape
    tr = _tile(R, 256)

    def body(p_ref, out_ref):
        acc = p_ref[0].astype(F32)
        for s in range(1, N_DEV):
            acc += p_ref[s].astype(F32)
        out_ref[...] = acc

    return pl.pallas_call(
        body, name=name, grid=(R // tr,),
        in_specs=[pl.BlockSpec((N_DEV, tr, C), lambda i: (0, i, 0))],
        out_specs=pl.BlockSpec((tr, C), lambda i: (i, 0)),
        out_shape=jax.ShapeDtypeStruct((R, C), F32),
        compiler_params=_params("parallel"),
    )(parts)


def _adamw(w, g, m, v, name):
    R, C = w.shape
    tr = _tile(R, 256)

    def body(w_ref, g_ref, m_ref, v_ref, d_ref, mo_ref, vo_ref):
        gg = g_ref[...]
        mm = ADAM_B1 * m_ref[...] + (1.0 - ADAM_B1) * gg
        vv = ADAM_B2 * v_ref[...] + (1.0 - ADAM_B2) * (gg * gg)
        m_hat = mm / (1.0 - ADAM_B1 ** ADAM_STEP)
        v_hat = vv / (1.0 - ADAM_B2 ** ADAM_STEP)
        d_ref[...] = -ADAM_LR * (m_hat / (jnp.sqrt(v_hat) + ADAM_EPS) + ADAM_WD * w_ref[...])
        mo_ref[...] = mm
        vo_ref[...] = vv

    spec = pl.BlockSpec((tr, C), lambda i: (i, 0))
    return pl.pallas_call(
        body, name=name, grid=(R // tr,),
        in_specs=[spec] * 4, out_specs=[spec] * 3,
        out_shape=[jax.ShapeDtypeStruct((R, C), F32)] * 3,
        compiler_params=_params("parallel"),
    )(w, g, m, v)


def _swiglu_backward(tag, dxo, y, gate, h, g, u, a, wg_t, wu_t, wd):
    dy, dgate = _gate_bwd(dxo, y, gate, 0.5, f"{tag}_gate_bwd")
    dg, du = _ffn_bwd_act(dy, wd, g, u, f"{tag}_bwd_act")
    d_wd = _matmul([(a, dy)], TN, F32, f"{tag}_dwd", tm=1408, tn=1024, tk=512)
    d_wg_t = _matmul([(dg, h)], TN, F32, f"{tag}_dwg", tm=1408, tn=1024, tk=512)
    d_wu_t = _matmul([(du, h)], TN, F32, f"{tag}_dwu", tm=1408, tn=1024, tk=512)
    dh = _matmul([(dg, wg_t), (du, wu_t)], NN, F32, f"{tag}_dh", tm=512, tn=1024, tk=1408)
    return dh, dgate, d_wg_t, d_wu_t, d_wd


def _rows_of(flat, lanes=1024):
    flat = flat.reshape(-1)
    n = -(-flat.shape[0] // lanes) * lanes
    return jnp.pad(flat, (0, n - flat.shape[0])).reshape(n // lanes, lanes)


def _pad_rows(a, rows):
    return jnp.pad(a, ((0, rows - a.shape[0]), (0, 0)))


def kernel(x, c, w_ada, b_ada, norm_ffn1, ffn1_gate, ffn1_up, ffn1_down, norm_mix, w_in, conv_w, a_log, dt_bias, gdn_norm, pool_w, pool_scale, w_out, norm_ffn2, ffn2_gate, ffn2_up, ffn2_down, final_norm, loss_target, m_w_ada, m_b_ada, m_norm_ffn1, m_ffn1_gate, m_ffn1_up, m_ffn1_down, m_norm_mix, m_w_in, m_conv_w, m_a_log, m_dt_bias, m_gdn_norm, m_pool_w, m_pool_scale, m_w_out, m_norm_ffn2, m_ffn2_gate, m_ffn2_up, m_ffn2_down, m_final_norm, v_w_ada, v_b_ada, v_norm_ffn1, v_ffn1_gate, v_ffn1_up, v_ffn1_down, v_norm_mix, v_w_in, v_conv_w, v_a_log, v_dt_bias, v_gdn_norm, v_pool_w, v_pool_scale, v_w_out, v_norm_ffn2, v_ffn2_gate, v_ffn2_up, v_ffn2_down, v_final_norm):
    T, D = x.shape[1], x.shape[2]
    Fs = ffn1_gate.shape[2]
    Ws = w_in.shape[2]
    Ws_pad = -(-Ws // 16) * 16
    Os = w_out.shape[1]
    Ms = w_ada.shape[2]
    Cs = conv_w.shape[2]
    me = 4 * lax.axis_index("x") + 2 * lax.axis_index("y") + lax.axis_index("c")
    x0, target = x[0], loss_target[0]

    small = jnp.concatenate([_pad_rows(c, 8), _pad_rows(jnp.pad(conv_w[0], ((0, 0), (0, D - Cs))), 8)], axis=0)
    got = _all_gather(small, "gather_small")
    c_all = got[:, 0, :]
    conv_full = jnp.transpose(got[:, 8:8 + CONV_K, :Cs], (1, 0, 2)).reshape(CONV_K, QKV_WIDTH)

    def wire(a):
        return a.astype(WIRE_DTYPE)

    blocks = [wire(ffn1_gate[0].T), wire(ffn1_up[0].T), wire(ffn1_down[0]),
              wire(_pad_rows(w_in[0].T, Ws_pad)), wire(w_out[0]),
              wire(ffn2_gate[0].T), wire(ffn2_up[0].T), wire(ffn2_down[0])]
    sizes = [b.shape[0] for b in blocks]
    offs = [sum(sizes[:i]) for i in range(len(sizes))]
    gathered = _all_gather(jnp.concatenate(blocks, axis=0), "gather_weights")

    def full(idx, keep=None):
        blk = gathered[:, offs[idx]:offs[idx] + sizes[idx], :]
        if keep is not None:
            blk = blk[:, :keep, :]
        return blk.reshape(-1, D).astype(MXU_DTYPE)

    wg1_t, wu1_t, wd1 = full(0), full(1), full(2)
    w_in_t = full(3, Ws)
    wo = full(4)
    wg2_t, wu2_t, wd2 = full(5), full(6), full(7)
    w_in_re = jnp.concatenate([w_in_t[:COL_Z + GDN_WIDTH], w_in_t[D_IN - POOL_WIDTH:],
                               w_in_t[4 * GDN_WIDTH:4 * GDN_WIDTH + 2 * GDN_HEADS],
                               jnp.zeros((128 - 2 * GDN_HEADS, D), MXU_DTYPE)], axis=0)

    b_part = lax.dynamic_slice(b_ada, (0, me * Ms), (1, Ms))
    mod_parts = _all_gather(_mod_part(c_all, w_ada[0], b_part, "mod_part"), "gather_mod")
    mod_all = jnp.transpose(mod_parts, (1, 0, 2)).reshape(N_DEV, N_MOD * D)
    mod = lax.dynamic_slice(mod_all, (me, 0), (1, N_MOD * D)).reshape(N_MOD, 1, D)
    sh1, sc1, gt1, sh2, sc2, gt2, sh3, sc3, gt3 = [mod[i] for i in range(N_MOD)]

    lane_pad = lambda a: jnp.pad(a, ((0, 0), (GDN_HEADS, 128 - 2 * GDN_HEADS)))
    a_log_l, dt_bias_l = lane_pad(a_log), lane_pad(dt_bias)
    pool_w_m = pool_w[0].astype(MXU_DTYPE)

    h1 = _norm_mod(x0, norm_ffn1, sc1, sh1, "norm1")
    y1, g1, u1, a1 = _ffn_fwd(h1, wg1_t, wu1_t, wd1, "ffn1_fwd")
    x1, h2 = _resid_norm_mod(x0, y1, gt1, 0.5, norm_mix, sc2, sh2, "resid_norm2")
    proj = _matmul([(h2, w_in_re)], NT, F32, "proj_in", tm=512, tn=D_IN_PAD // 3, tk=D)
    qkv, bg = _gdn_prep(proj, conv_full, a_log_l, dt_bias_l, "gdn_prep")
    o, tinv, s_all = _gdn_fwd(qkv, bg, "gdn_fwd")
    mix_in = _mix_post(o, proj, gdn_norm, pool_w_m, pool_scale, "mix_post")
    mixed = _matmul([(mix_in, wo)], NN, F32, "mix_out", tm=512, tn=D, tk=GDN_WIDTH + POOL_WIDTH)
    x2, h3 = _resid_norm_mod(x1, mixed, gt2, 1.0, norm_ffn2, sc3, sh3, "resid_norm3")
    y3, g3, u3, a3 = _ffn_fwd(h3, wg2_t, wu2_t, wd2, "ffn2_fwd")
    loss_row, d3, d_final = _final_loss(x2, y3, gt3, final_norm.reshape(1, D), target, "final_loss")

    dh3, dgt3, d_wg2, d_wu2, d_wd2 = _swiglu_backward("ffn2", d3, y3, gt3, h3, g3, u3, a3, wg2_t, wu2_t, wd2)
    d2, d_n3, dsc3, dsh3 = _norm_bwd(dh3, x2, norm_ffn2, sc3, d3, "norm3_bwd")
    dmixed, dgt2 = _gate_bwd(d2, mixed, gt2, 1.0, "mix_gate_bwd")
    dmix_in = _matmul([(dmixed, wo)], NT, F32, "mix_out_bwd", tm=512, tn=GDN_WIDTH + POOL_WIDTH, tk=D)
    d_wo = _matmul([(mix_in, dmixed)], TN, F32, "mix_dwo", tm=512, tn=D, tk=512)
    do, dzp, d_gn, d_pw, d_ps = _mix_post_bwd(dmix_in, o, proj, gdn_norm, pool_w_m, pool_scale, "mix_post_bwd")
    dqkv, dbg = _gdn_bwd(qkv, bg, tinv, s_all, do, "gdn_bwd")
    dproj, d_conv, d_al, d_dtb = _gdn_prep_bwd(proj, conv_full, a_log_l, dt_bias_l, dqkv, dbg, dzp, "gdn_prep_bwd")
    dh2 = _matmul([(dproj, w_in_re)], NN, F32, "proj_in_bwd", tm=512, tn=D, tk=D_IN_PAD // 3)
    d_win_re = _matmul([(dproj, h2)], TN, F32, "proj_in_dw", tm=D_IN_PAD // 3, tn=D, tk=512)
    d1, d_n2, dsc2, dsh2 = _norm_bwd(dh2, x1, norm_mix, sc2, d2, "norm2_bwd")
    dh1, dgt1, d_wg1, d_wu1, d_wd1 = _swiglu_backward("ffn1", d1, y1, gt1, h1, g1, u1, a1, wg1_t, wu1_t, wd1)
    grad_x, d_n1, dsc1, dsh1 = _norm_bwd(dh1, x0, norm_ffn1, sc1, d1, "norm1_bwd")

    d_win_t = jnp.concatenate([d_win_re[:COL_Z + GDN_WIDTH], d_win_re[COL_BA:COL_BA + 2 * GDN_HEADS],
                               d_win_re[COL_P:COL_P + POOL_WIDTH]], axis=0)
    dmod = jnp.concatenate([dsh1, dsc1, dgt1, dsh2, dsc2, dgt2, dsh3, dsc3, dgt3], axis=0)
    small_rows = [dmod.reshape(-1), d_n1[0], d_n2[0], d_n3[0], d_final[0], d_gn[0], d_ps[0],
                  d_al[0, GDN_HEADS:2 * GDN_HEADS], d_dtb[0, GDN_HEADS:2 * GDN_HEADS], loss_row[0, :1],
                  d_conv.reshape(-1), d_pw.reshape(-1)]
    lanes = 1024
    small_rows = [_rows_of(r, lanes) for r in small_rows]
    n_rows = [r.shape[0] for r in small_rows]
    row_off = [sum(n_rows[:i]) for i in range(len(n_rows))]
    total = -(-sum(n_rows) // 8) * 8
    slab = _pad_rows(jnp.concatenate(small_rows, axis=0), total)
    slab_all = _all_gather(slab, "gather_small_grads")
    summed = _sum_parts(slab_all, "sum_small_grads")

    def piece(idx, n):
        return summed[row_off[idx]:row_off[idx] + n_rows[idx]].reshape(-1)[:n]

    g_b_ada = piece(0, N_MOD * D).reshape(1, N_MOD * D)
    g_n1, g_n2, g_n3 = piece(1, D).reshape(1, D), piece(2, D).reshape(1, D), piece(3, D).reshape(1, D)
    g_final = piece(4, D)
    g_gn = piece(5, HEAD_DIM).reshape(1, HEAD_DIM)
    g_ps = piece(6, POOL_WIDTH).reshape(1, POOL_WIDTH)
    g_al = piece(7, GDN_HEADS).reshape(1, GDN_HEADS)
    g_dtb = piece(8, GDN_HEADS).reshape(1, GDN_HEADS)
    loss = piece(9, 1)[0]
    g_conv = lax.dynamic_slice(piece(10, CONV_K * QKV_WIDTH).reshape(1, CONV_K, QKV_WIDTH), (0, 0, me * Cs),
                               (1, CONV_K, Cs))
    g_pw = piece(11, POOL_GROUPS * 128 * 128).reshape(1, POOL_GROUPS, 128, 128)

    dmod_all = slab_all[:, row_off[0]:row_off[0] + n_rows[0], :].reshape(N_DEV, -1)[:, :N_MOD * D]
    g_w_ada = _w_ada_grad(c_all, lax.dynamic_slice(dmod_all, (0, me * Ms), (N_DEV, Ms)), "w_ada_grad")[None]

    d_win_blocks = jnp.pad(d_win_t.reshape(N_DEV, Ws, D), ((0, 0), (0, Ws_pad - Ws), (0, 0)))
    parts = jnp.concatenate(
        [wire(d_wg1.reshape(N_DEV, Fs, D)), wire(d_wu1.reshape(N_DEV, Fs, D)), wire(d_wd1.reshape(N_DEV, Fs, D)),
         wire(d_win_blocks), wire(d_wo.reshape(N_DEV, Os, D)),
         wire(d_wg2.reshape(N_DEV, Fs, D)), wire(d_wu2.reshape(N_DEV, Fs, D)), wire(d_wd2.reshape(N_DEV, Fs, D))],
        axis=1)
    big = _sum_parts(_all_to_all(parts, "exchange_grads"), "sum_grads")

    def mine(idx, keep=None):
        blk = big[offs[idx]:offs[idx] + (sizes[idx] if keep is None else keep)]
        return blk

    g_ffn1_gate, g_ffn1_up, g_ffn1_down = mine(0).T[None], mine(1).T[None], mine(2)[None]
    g_w_in, g_w_out = mine(3, Ws).T[None], mine(4)[None]
    g_ffn2_gate, g_ffn2_up, g_ffn2_down = mine(5).T[None], mine(6).T[None], mine(7)[None]

    names = ["w_ada", "b_ada", "norm_ffn1", "ffn1_gate", "ffn1_up", "ffn1_down", "norm_mix", "w_in", "conv_w",
             "a_log", "dt_bias", "gdn_norm", "pool_w", "pool_scale", "w_out", "norm_ffn2", "ffn2_gate", "ffn2_up",
             "ffn2_down", "final_norm"]
    weights = dict(zip(names, [w_ada, b_ada, norm_ffn1, ffn1_gate, ffn1_up, ffn1_down, norm_mix, w_in, conv_w,
                               a_log, dt_bias, gdn_norm, pool_w, pool_scale, w_out, norm_ffn2, ffn2_gate, ffn2_up,
                               ffn2_down, final_norm]))
    ms = dict(zip(names, [m_w_ada, m_b_ada, m_norm_ffn1, m_ffn1_gate, m_ffn1_up, m_ffn1_down, m_norm_mix, m_w_in,
                          m_conv_w, m_a_log, m_dt_bias, m_gdn_norm, m_pool_w, m_pool_scale, m_w_out, m_norm_ffn2,
                          m_ffn2_gate, m_ffn2_up, m_ffn2_down, m_final_norm]))
    vs = dict(zip(names, [v_w_ada, v_b_ada, v_norm_ffn1, v_ffn1_gate, v_ffn1_up, v_ffn1_down, v_norm_mix, v_w_in,
                          v_conv_w, v_a_log, v_dt_bias, v_gdn_norm, v_pool_w, v_pool_scale, v_w_out, v_norm_ffn2,
                          v_ffn2_gate, v_ffn2_up, v_ffn2_down, v_final_norm]))
    grads = dict(w_ada=g_w_ada, b_ada=g_b_ada, norm_ffn1=g_n1, ffn1_gate=g_ffn1_gate, ffn1_up=g_ffn1_up,
                 ffn1_down=g_ffn1_down, norm_mix=g_n2, w_in=g_w_in, conv_w=g_conv, a_log=g_al, dt_bias=g_dtb,
                 gdn_norm=g_gn, pool_w=g_pw, pool_scale=g_ps, w_out=g_w_out, norm_ffn2=g_n3,
                 ffn2_gate=g_ffn2_gate, ffn2_up=g_ffn2_up, ffn2_down=g_ffn2_down, final_norm=g_final)
    big_names = ["w_ada", "ffn1_gate", "ffn1_up", "ffn1_down", "w_in", "w_out", "ffn2_gate", "ffn2_up", "ffn2_down"]
    delta, new_m, new_v = {}, {}, {}
    for n in big_names:
        shp = weights[n].shape
        two_d = lambda a: a.reshape(shp[-2], shp[-1])
        d_, m_, v_ = _adamw(two_d(weights[n]), two_d(grads[n]), two_d(ms[n]), two_d(vs[n]), f"adamw_{n}")
        delta[n], new_m[n], new_v[n] = d_.reshape(shp), m_.reshape(shp), v_.reshape(shp)
    small_names = [n for n in names if n not in big_names]
    pack = lambda src: jnp.concatenate([_rows_of(src[n]) for n in small_names], axis=0)
    p_rows = [_rows_of(weights[n]).shape[0] for n in small_names]
    p_total = -(-sum(p_rows) // 8) * 8
    packed = [_pad_rows(pack(src), p_total) for src in (weights, grads, ms, vs)]
    d_s, m_s, v_s = _adamw(*packed, "adamw_small")
    off = 0
    for n, r in zip(small_names, p_rows):
        shp = weights[n].shape
        size = weights[n].size
        for dst, src in ((delta, d_s), (new_m, m_s), (new_v, v_s)):
            dst[n] = src[off:off + r].reshape(-1)[:size].reshape(shp)
        off += r

    return (loss, grad_x[None], *[grads[n] for n in names], *[delta[n] for n in names],
            *[new_m[n] for n in names], *[new_v[n] for n in names])
```

```python
import functools

import jax
import jax.numpy as jnp
from jax import lax
from jax.experimental import pallas as pl
from jax.experimental.pallas import tpu as pltpu

F32 = jnp.float32
MXU_DTYPE = jnp.bfloat16
WIRE_DTYPE = jnp.bfloat16
EPS = 1e-6
N_DEV = 8
GDN_HEADS = 4
HEAD_DIM = 128
GDN_WIDTH = GDN_HEADS * HEAD_DIM
POOL_WINDOWS = (2, 4, 8, 16)
POOL_GROUPS = len(POOL_WINDOWS)
POOL_WIDTH = 512
CONV_K = 4
CHUNK = 64
QKV_WIDTH = 3 * GDN_WIDTH
D_IN = 4 * GDN_WIDTH + 2 * GDN_HEADS + POOL_WIDTH
D_IN_PAD = 4 * GDN_WIDTH + POOL_WIDTH + 128
COL_Z = QKV_WIDTH
COL_P = 4 * GDN_WIDTH
COL_BA = 4 * GDN_WIDTH + POOL_WIDTH
N_MOD = 9
HALO = 16
VMEM_LIMIT = 56 * 1024 * 1024
ADAM_LR, ADAM_B1, ADAM_B2, ADAM_EPS, ADAM_WD, ADAM_STEP = 0.001, 0.9, 0.999, 1e-08, 0.01, 10
CHUNKS_PER_STEP = 2
SCAN_CHUNKS_PER_STEP = 4

NT = (((1,), (1,)), ((), ()))
NN = (((1,), (0,)), ((), ()))
TN = (((0,), (0,)), ((), ()))


def _params(*sem):
    return pltpu.CompilerParams(dimension_semantics=tuple(sem), vmem_limit_bytes=VMEM_LIMIT)


def _dot(a, b, dims):
    return lax.dot_general(a, b, dims, preferred_element_type=F32)


def _mdot(a, b, dims):
    return _dot(a.astype(MXU_DTYPE), b.astype(MXU_DTYPE), dims)


def _split(a):
    hi = a.astype(jnp.bfloat16)
    return hi, (a - hi.astype(F32)).astype(jnp.bfloat16)


def _dot3(a, b, dims):
    (ah, al), (bh, bl) = a, b
    return (_dot(al, bh, dims) + _dot(ah, bl, dims)) + _dot(ah, bh, dims)


def _sigmoid(v):
    return 1.0 / (1.0 + jnp.exp(-v))


def _softplus(v):
    return jnp.maximum(v, 0.0) + jnp.log(1.0 + jnp.exp(-jnp.abs(v)))


def _shift_rows(v, s):
    n = v.shape[0]
    s = s % n
    return v if s == 0 else pltpu.roll(v, s, 0)


def _tile(n, want):
    t = min(n, want)
    while n % t:
        t //= 2
    return t


def _all_gather(block, name):
    shape, dtype = block.shape, block.dtype

    def body(x_ref, out_ref, send_sems, recv_sems, local_sem):
        x, y, c = lax.axis_index("x"), lax.axis_index("y"), lax.axis_index("c")
        me, sibling = (x, y, c), (x, y, 1 - c)
        chips = [(1 - x, y), (x, 1 - y), (1 - x, 1 - y)]

        def rows(px, py, pc):
            return out_ref.at[4 * px + 2 * py + pc]

        def copy(k, blk, to, src=None):
            return pltpu.make_async_remote_copy(
                src_ref=rows(*blk) if src is None else src, dst_ref=rows(*blk),
                send_sem=send_sems.at[k], recv_sem=recv_sems.at[k],
                device_id=to, device_id_type=pl.DeviceIdType.MESH)

        mine = pltpu.make_async_copy(x_ref, rows(*me), local_sem)
        mine.start()
        first = [copy(0, me, sibling, src=x_ref)]
        first += [copy(1 + j, me, (*chip, c), src=x_ref) for j, chip in enumerate(chips)]
        for cp in first:
            cp.start()
        passed = [copy(4 + j, (*chip, c), sibling) for j, chip in enumerate(chips)]
        for j, chip in enumerate(chips):
            copy(1 + j, (*chip, c), me).wait_recv()
            passed[j].start()
        copy(0, sibling, me).wait_recv()
        for j, chip in enumerate(chips):
            copy(4 + j, (*chip, 1 - c), me).wait_recv()
        for cp in first + passed:
            cp.wait_send()
        mine.wait()

    return pl.pallas_call(
        body, name=name,
        out_shape=jax.ShapeDtypeStruct((N_DEV,) + shape, dtype),
        in_specs=[pl.BlockSpec(memory_space=pltpu.HBM)],
        out_specs=pl.BlockSpec(memory_space=pltpu.HBM),
        scratch_shapes=[pltpu.SemaphoreType.DMA((7,)), pltpu.SemaphoreType.DMA((7,)),
                        pltpu.SemaphoreType.DMA(())],
    )(block)


def _all_to_all(parts, name):
    shape, dtype = parts.shape, parts.dtype
    flips = [(0, 0, 1), (0, 1, 0), (0, 1, 1), (1, 0, 0), (1, 0, 1), (1, 1, 0), (1, 1, 1)]

    def body(x_ref, out_ref, send_sems, recv_sems, local_sem):
        x, y, c = lax.axis_index("x"), lax.axis_index("y"), lax.axis_index("c")
        me = 4 * x + 2 * y + c
        mine = pltpu.make_async_copy(x_ref.at[me], out_ref.at[me], local_sem)
        mine.start()
        copies = []
        for k, (fx, fy, fc) in enumerate(flips):
            px = 1 - x if fx else x
            py = 1 - y if fy else y
            pc = 1 - c if fc else c
            copies.append(pltpu.make_async_remote_copy(
                src_ref=x_ref.at[4 * px + 2 * py + pc], dst_ref=out_ref.at[me],
                send_sem=send_sems.at[k], recv_sem=recv_sems.at[k],
                device_id=(px, py, pc), device_id_type=pl.DeviceIdType.MESH))
        for cp in copies:
            cp.start()
        for cp in copies:
            cp.wait_recv()
        for cp in copies:
            cp.wait_send()
        mine.wait()

    return pl.pallas_call(
        body, name=name,
        out_shape=jax.ShapeDtypeStruct(shape, dtype),
        in_specs=[pl.BlockSpec(memory_space=pltpu.HBM)],
        out_specs=pl.BlockSpec(memory_space=pltpu.HBM),
        scratch_shapes=[pltpu.SemaphoreType.DMA((7,)), pltpu.SemaphoreType.DMA((7,)),
                        pltpu.SemaphoreType.DMA(())],
    )(parts)


def _matmul(pairs, dims, out_dtype, name, tm=512, tn=512, tk=512):
    a0, b0 = pairs[0]
    if dims == TN:
        K, M = a0.shape
    else:
        M, K = a0.shape
    N = b0.shape[0] if dims == NT else b0.shape[1]
    tm, tn, tk = _tile(M, tm), _tile(N, tn), _tile(K, tk)
    nk = K // tk
    n_pairs = len(pairs)

    def body(*refs):
        out_ref, acc_ref = refs[2 * n_pairs], refs[2 * n_pairs + 1]
        k = pl.program_id(2)

        @pl.when(k == 0)
        def _():
            acc_ref[...] = jnp.zeros_like(acc_ref)

        acc = acc_ref[...]
        for p in range(n_pairs):
            acc += _dot(refs[2 * p][...], refs[2 * p + 1][...], dims)
        acc_ref[...] = acc

        @pl.when(k == nk - 1)
        def _():
            out_ref[...] = acc_ref[...].astype(out_ref.dtype)

    if dims == TN:
        a_spec = pl.BlockSpec((tk, tm), lambda i, j, k: (k, i))
    else:
        a_spec = pl.BlockSpec((tm, tk), lambda i, j, k: (i, k))
    if dims == NT:
        b_spec = pl.BlockSpec((tn, tk), lambda i, j, k: (j, k))
    else:
        b_spec = pl.BlockSpec((tk, tn), lambda i, j, k: (k, j))
    args, specs = [], []
    for a, b in pairs:
        args += [a, b]
        specs += [a_spec, b_spec]
    return pl.pallas_call(
        body, name=name, grid=(M // tm, N // tn, nk),
        in_specs=specs, out_specs=pl.BlockSpec((tm, tn), lambda i, j, k: (i, j)),
        out_shape=jax.ShapeDtypeStruct((M, N), out_dtype),
        scratch_shapes=[pltpu.VMEM((tm, tn), F32)],
        compiler_params=_params("parallel", "parallel", "arbitrary"),
    )(*args)


def _vec_spec(d):
    return pl.BlockSpec((1, d), lambda i: (0, 0))


def _norm_mod(x, nw, scale, shift, name):
    T, D = x.shape
    tm = _tile(T, 512)

    def body(x_ref, nw_ref, sc_ref, sh_ref, h_ref):
        xf = x_ref[...]
        r = lax.rsqrt(jnp.mean(xf * xf, axis=-1, keepdims=True) + EPS)
        h_ref[...] = ((xf * r) * nw_ref[...] * (1.0 + sc_ref[...]) + sh_ref[...]).astype(h_ref.dtype)

    row = pl.BlockSpec((tm, D), lambda i: (i, 0))
    return pl.pallas_call(
        body, name=name, grid=(T // tm,),
        in_specs=[row, _vec_spec(D), _vec_spec(D), _vec_spec(D)], out_specs=row,
        out_shape=jax.ShapeDtypeStruct((T, D), MXU_DTYPE),
        compiler_params=_params("parallel"),
    )(x, nw, scale, shift)


def _resid_norm_mod(x, y, gate, coef, nw, scale, shift, name):
    T, D = x.shape
    tm = _tile(T, 512)

    def body(x_ref, y_ref, g_ref, nw_ref, sc_ref, sh_ref, xo_ref, h_ref):
        xf = x_ref[...] + (coef * g_ref[...]) * y_ref[...]
        xo_ref[...] = xf
        r = lax.rsqrt(jnp.mean(xf * xf, axis=-1, keepdims=True) + EPS)
        h_ref[...] = ((xf * r) * nw_ref[...] * (1.0 + sc_ref[...]) + sh_ref[...]).astype(h_ref.dtype)

    row = pl.BlockSpec((tm, D), lambda i: (i, 0))
    return pl.pallas_call(
        body, name=name, grid=(T // tm,),
        in_specs=[row, row, _vec_spec(D), _vec_spec(D), _vec_spec(D), _vec_spec(D)],
        out_specs=[row, row],
        out_shape=[jax.ShapeDtypeStruct((T, D), F32), jax.ShapeDtypeStruct((T, D), MXU_DTYPE)],
        compiler_params=_params("parallel"),
    )(x, y, gate, nw, scale, shift)


def _gate_bwd(dxo, y, gate, coef, name):
    T, D = dxo.shape
    tm = _tile(T, 512)

    def body(d_ref, y_ref, g_ref, dy_ref, dg_ref):
        @pl.when(pl.program_id(0) == 0)
        def _():
            dg_ref[...] = jnp.zeros_like(dg_ref)

        d = d_ref[...]
        dy_ref[...] = ((coef * g_ref[...]) * d).astype(dy_ref.dtype)
        dg_ref[...] += coef * jnp.sum(d * y_ref[...], axis=0, keepdims=True)

    row = pl.BlockSpec((tm, D), lambda i: (i, 0))
    return pl.pallas_call(
        body, name=name, grid=(T // tm,),
        in_specs=[row, row, _vec_spec(D)], out_specs=[row, _vec_spec(D)],
        out_shape=[jax.ShapeDtypeStruct((T, D), MXU_DTYPE), jax.ShapeDtypeStruct((1, D), F32)],
        compiler_params=_params("arbitrary"),
    )(dxo, y, gate)


def _norm_bwd(dh, x, nw, scale, dres, name):
    T, D = x.shape
    tm = _tile(T, 512)

    def body(dh_ref, x_ref, nw_ref, sc_ref, dr_ref, dx_ref, dnw_ref, dsc_ref, dsh_ref):
        @pl.when(pl.program_id(0) == 0)
        def _():
            dnw_ref[...] = jnp.zeros_like(dnw_ref)
            dsc_ref[...] = jnp.zeros_like(dsc_ref)
            dsh_ref[...] = jnp.zeros_like(dsh_ref)

        xf, dh_ = x_ref[...], dh_ref[...]
        r = lax.rsqrt(jnp.mean(xf * xf, axis=-1, keepdims=True) + EPS)
        xn = xf * r
        one_sc = 1.0 + sc_ref[...]
        dsh_ref[...] += jnp.sum(dh_, axis=0, keepdims=True)
        t = dh_ * xn
        dsc_ref[...] += jnp.sum(t, axis=0, keepdims=True) * nw_ref[...]
        dnw_ref[...] += jnp.sum(t, axis=0, keepdims=True) * one_sc
        dxn = dh_ * (nw_ref[...] * one_sc)
        dx_ref[...] = dr_ref[...] + r * (dxn - xn * jnp.mean(dxn * xn, axis=-1, keepdims=True))

    row = pl.BlockSpec((tm, D), lambda i: (i, 0))
    vec = _vec_spec(D)
    return pl.pallas_call(
        body, name=name, grid=(T // tm,),
        in_specs=[row, row, vec, vec, row], out_specs=[row, vec, vec, vec],
        out_shape=[jax.ShapeDtypeStruct((T, D), F32)] + [jax.ShapeDtypeStruct((1, D), F32)] * 3,
        compiler_params=_params("arbitrary"),
    )(dh, x, nw, scale, dres)


def _final_loss(x, y, gate, fw, target, name):
    T, D = x.shape
    tm = _tile(T, 512)

    def body(x_ref, y_ref, g_ref, fw_ref, t_ref, loss_ref, dx_ref, dfw_ref):
        @pl.when(pl.program_id(0) == 0)
        def _():
            loss_ref[...] = jnp.zeros_like(loss_ref)
            dfw_ref[...] = jnp.zeros_like(dfw_ref)

        xf = x_ref[...] + (0.5 * g_ref[...]) * y_ref[...]
        r = lax.rsqrt(jnp.mean(xf * xf, axis=-1, keepdims=True) + EPS)
        xn = xf * r
        err = xn * fw_ref[...] - t_ref[...]
        per_tok = jnp.mean(err * err, axis=-1, keepdims=True)
        loss_ref[...] += 0.5 * jnp.sum(per_tok, axis=0, keepdims=True)
        dy = err * (1.0 / D)
        dfw_ref[...] += jnp.sum(dy * xn, axis=0, keepdims=True)
        dxn = dy * fw_ref[...]
        dx_ref[...] = r * (dxn - xn * jnp.mean(dxn * xn, axis=-1, keepdims=True))

    row = pl.BlockSpec((tm, D), lambda i: (i, 0))
    vec = _vec_spec(D)
    return pl.pallas_call(
        body, name=name, grid=(T // tm,),
        in_specs=[row, row, vec, vec, row],
        out_specs=[pl.BlockSpec((1, 128), lambda i: (0, 0)), row, vec],
        out_shape=[jax.ShapeDtypeStruct((1, 128), F32), jax.ShapeDtypeStruct((T, D), F32),
                   jax.ShapeDtypeStruct((1, D), F32)],
        compiler_params=_params("arbitrary"),
    )(x, y, gate, fw, target)


def _ffn_fwd(h, wg_t, wu_t, wd, name):
    T, D = h.shape
    Fdim = wd.shape[0]
    tm, tf = _tile(T, 1024), _tile(Fdim, 256)
    nf = Fdim // tf

    def body(h_ref, wg_ref, wu_ref, wd_ref, y_ref, g_ref, u_ref, a_ref, acc_ref):
        k = pl.program_id(1)

        @pl.when(k == 0)
        def _():
            acc_ref[...] = jnp.zeros_like(acc_ref)

        hh = h_ref[...]
        g = _dot(hh, wg_ref[...], NT)
        u = _dot(hh, wu_ref[...], NT)
        a = ((g * _sigmoid(g)) * u).astype(a_ref.dtype)
        g_ref[...] = g
        u_ref[...] = u
        a_ref[...] = a
        acc_ref[...] += _dot(a, wd_ref[...], NN)

        @pl.when(k == nf - 1)
        def _():
            y_ref[...] = acc_ref[...]

    hrow = pl.BlockSpec((tm, D), lambda i, k: (i, 0))
    wspec = pl.BlockSpec((tf, D), lambda i, k: (k, 0))
    fspec = pl.BlockSpec((tm, tf), lambda i, k: (i, k))
    return pl.pallas_call(
        body, name=name, grid=(T // tm, nf),
        in_specs=[hrow, wspec, wspec, wspec], out_specs=[hrow, fspec, fspec, fspec],
        out_shape=[jax.ShapeDtypeStruct((T, D), F32), jax.ShapeDtypeStruct((T, Fdim), F32),
                   jax.ShapeDtypeStruct((T, Fdim), F32), jax.ShapeDtypeStruct((T, Fdim), MXU_DTYPE)],
        scratch_shapes=[pltpu.VMEM((tm, D), F32)],
        compiler_params=_params("parallel", "arbitrary"),
    )(h, wg_t, wu_t, wd)


def _ffn_bwd_act(dy, wd, g, u, name):
    T, D = dy.shape
    Fdim = wd.shape[0]
    tm, tf = _tile(T, 1024), _tile(Fdim, 256)

    def body(dy_ref, wd_ref, g_ref, u_ref, dg_ref, du_ref):
        da = _dot(dy_ref[...], wd_ref[...], NT)
        gg = g_ref[...]
        sig = _sigmoid(gg)
        dg_ref[...] = (da * u_ref[...] * (sig * (1.0 + gg * (1.0 - sig)))).astype(dg_ref.dtype)
        du_ref[...] = (da * (gg * sig)).astype(du_ref.dtype)

    fspec = pl.BlockSpec((tm, tf), lambda i, k: (i, k))
    return pl.pallas_call(
        body, name=name, grid=(T // tm, Fdim // tf),
        in_specs=[pl.BlockSpec((tm, D), lambda i, k: (i, 0)), pl.BlockSpec((tf, D), lambda i, k: (k, 0)),
                  fspec, fspec],
        out_specs=[fspec, fspec],
        out_shape=[jax.ShapeDtypeStruct((T, Fdim), MXU_DTYPE)] * 2,
        compiler_params=_params("parallel", "parallel"),
    )(dy, wd, g, u)


def _conv_act(window, w):
    y = window * w[CONV_K - 1:CONV_K, :]
    for j in range(CONV_K - 1):
        y += _shift_rows(window, CONV_K - 1 - j) * w[j:j + 1, :]
    return y


def _gdn_prep(proj, conv_w, a_log_l, dt_bias_l, name):
    T = proj.shape[0]
    tm = _tile(T, 256)
    hb = tm // 8

    def body(cur_ref, halo_ref, ba_ref, w_ref, al_ref, dtb_ref, qkv_ref, bg_ref):
        i = pl.program_id(0)
        halo = jnp.where(i == 0, 0.0, halo_ref[...])
        window = jnp.concatenate([halo, cur_ref[...]], axis=0)
        y = _conv_act(window, w_ref[...])[8:, :]
        act = y * _sigmoid(y)
        for hh in range(3 * GDN_HEADS):
            blk = act[:, hh * HEAD_DIM:(hh + 1) * HEAD_DIM]
            if hh < 2 * GDN_HEADS:
                rn = lax.rsqrt(jnp.sum(blk * blk, axis=-1, keepdims=True) + EPS)
                blk = blk * rn
                if hh < GDN_HEADS:
                    blk = blk * (HEAD_DIM ** -0.5)
            qkv_ref[:, hh * HEAD_DIM:(hh + 1) * HEAD_DIM] = blk
        ba = ba_ref[...]
        lane = lax.broadcasted_iota(jnp.int32, ba.shape, 1)
        beta = _sigmoid(ba)
        g = -jnp.exp(al_ref[...]) * _softplus(ba + dtb_ref[...])
        bg_ref[...] = jnp.where(lane < GDN_HEADS, beta, jnp.where(lane < 2 * GDN_HEADS, g, 0.0))

    return pl.pallas_call(
        body, name=name, grid=(T // tm,),
        in_specs=[pl.BlockSpec((tm, QKV_WIDTH), lambda i: (i, 0)),
                  pl.BlockSpec((8, QKV_WIDTH), lambda i: (jnp.maximum(i * hb - 1, 0), 0)),
                  pl.BlockSpec((tm, 128), lambda i: (i, COL_BA // 128)),
                  pl.BlockSpec((CONV_K, QKV_WIDTH), lambda i: (0, 0)),
                  pl.BlockSpec((1, 128), lambda i: (0, 0)), pl.BlockSpec((1, 128), lambda i: (0, 0))],
        out_specs=[pl.BlockSpec((tm, QKV_WIDTH), lambda i: (i, 0)), pl.BlockSpec((tm, 128), lambda i: (i, 0))],
        out_shape=[jax.ShapeDtypeStruct((T, QKV_WIDTH), F32), jax.ShapeDtypeStruct((T, 128), F32)],
        compiler_params=_params("parallel"),
    )(proj, proj, proj, conv_w, a_log_l, dt_bias_l)


def _chunk_cumsum(v, reverse=False):
    row = lax.broadcasted_iota(jnp.int32, v.shape, 0)
    s = 1
    while s < CHUNK:
        if reverse:
            v = v + jnp.where(row < CHUNK - s, _shift_rows(v, -s), 0.0)
        else:
            v = v + jnp.where(row >= s, _shift_rows(v, s), 0.0)
        s *= 2
    return v


def _row_form(cols):
    padded = jnp.concatenate([cols, jnp.zeros((128 - CHUNK, 128), F32)], axis=0)
    return padded.T[:, :CHUNK]


def _chunk_masks():
    ri = lax.broadcasted_iota(jnp.int32, (CHUNK, CHUNK), 0)
    ci = lax.broadcasted_iota(jnp.int32, (CHUNK, CHUNK), 1)
    return ri >= ci, ri > ci, (ri == ci).astype(F32)


def _unit_lower_inverses(ms, eye):
    rs = [eye - m for m in ms]
    ps = [_split(-m) for m in ms]
    s = 2
    while s < CHUNK:
        ps = [_split(_dot3(p, p, NN)) for p in ps]
        r_parts = [_split(r) for r in rs]
        rs = [r + _dot3(p, rp, NN) for r, p, rp in zip(rs, ps, r_parts)]
        s *= 2
    return rs


def _head_elementwise(k, beta, gc, gcr, causal):
    decay = jnp.where(causal, jnp.exp(jnp.where(causal, gc - gcr, 0.0)), 0.0)
    return decay, k * beta, jnp.exp(gc)


def _head_slices(hh):
    return (slice(hh * HEAD_DIM, (hh + 1) * HEAD_DIM),
            slice(GDN_WIDTH + hh * HEAD_DIM, GDN_WIDTH + (hh + 1) * HEAD_DIM),
            slice(2 * GDN_WIDTH + hh * HEAD_DIM, 2 * GDN_WIDTH + (hh + 1) * HEAD_DIM))


def _gdn_chunk_fwd(qkv, bg, name):
    T = qkv.shape[0]
    cb = _tile(T // CHUNK, CHUNKS_PER_STEP)
    rows = cb * CHUNK

    def body(qkv_ref, bg_ref, tinv_ref, u_ref, w_ref, qd_ref, kd_ref, p_ref, cd_ref):
        masks = _chunk_masks()
        causal, strict, eye = masks
        heads = []
        for ci in range(cb):
            rs = slice(ci * CHUNK, (ci + 1) * CHUNK)
            bgv = bg_ref[rs, :]
            gc_all = _chunk_cumsum(bgv)
            gc_rows = _row_form(gc_all)
            cd_ref[rs, :] = jnp.exp(jnp.broadcast_to(gc_all[CHUNK - 1:CHUNK, :], (CHUNK, 128)))
            for hh in range(GDN_HEADS):
                qs, ks, vs = _head_slices(hh)
                q, k, v = qkv_ref[rs, qs], qkv_ref[rs, ks], qkv_ref[rs, vs]
                beta = bgv[:, hh:hh + 1]
                gc = gc_all[:, GDN_HEADS + hh:GDN_HEADS + hh + 1]
                decay, kb, eg = _head_elementwise(k, beta, gc, gc_rows[GDN_HEADS + hh:GDN_HEADS + hh + 1, :], causal)
                hs = slice(hh * HEAD_DIM, (hh + 1) * HEAD_DIM)
                cs = slice(hh * CHUNK, (hh + 1) * CHUNK)
                qd_ref[rs, hs] = (q * eg).astype(qd_ref.dtype)
                kd_ref[rs, hs] = (k * jnp.exp(gc[CHUNK - 1:CHUNK, :] - gc)).astype(kd_ref.dtype)
                heads.append((rs, hs, cs, q, k, v * beta, kb, kb * eg, decay))
        kks = [_mdot(kb, k, NT) for (_, _, _, _, k, _, kb, _, _) in heads]
        qks = [_mdot(q, k, NT) for (_, _, _, q, k, _, _, _, _) in heads]
        tinvs = _unit_lower_inverses([jnp.where(strict, kk * hd[8], 0.0) for kk, hd in zip(kks, heads)], eye)
        t_parts = [_split(t) for t in tinvs]
        us = [_dot3(tp, _split(hd[5]), NN) for tp, hd in zip(t_parts, heads)]
        ws = [_dot3(tp, _split(hd[7]), NN) for tp, hd in zip(t_parts, heads)]
        for hd, tinv, u, w, qk in zip(heads, tinvs, us, ws, qks):
            rs, hs, cs = hd[0], hd[1], hd[2]
            tinv_ref[rs, cs] = tinv
            u_ref[rs, hs] = u
            w_ref[rs, hs] = w.astype(w_ref.dtype)
            p_ref[rs, cs] = jnp.where(causal, qk * hd[8], 0.0).astype(p_ref.dtype)

    def spec(width):
        return pl.BlockSpec((rows, width), lambda n: (n, 0))

    hw, cw = GDN_WIDTH, GDN_HEADS * CHUNK
    return pl.pallas_call(
        body, name=name, grid=(T // rows,),
        in_specs=[spec(QKV_WIDTH), spec(128)],
        out_specs=[spec(cw), spec(hw), spec(hw), spec(hw), spec(hw), spec(cw), spec(128)],
        out_shape=[jax.ShapeDtypeStruct((T, cw), F32), jax.ShapeDtypeStruct((T, hw), F32),
                   jax.ShapeDtypeStruct((T, hw), MXU_DTYPE), jax.ShapeDtypeStruct((T, hw), MXU_DTYPE),
                   jax.ShapeDtypeStruct((T, hw), MXU_DTYPE), jax.ShapeDtypeStruct((T, cw), MXU_DTYPE),
                   jax.ShapeDtypeStruct((T, 128), F32)],
        compiler_params=_params("parallel"),
    )(qkv, bg)


def _gdn_scan_fwd(u, w, qd, kd, p, cd, name):
    T = u.shape[0]
    cb = _tile(T // CHUNK, SCAN_CHUNKS_PER_STEP)
    rows = cb * CHUNK

    def body(u_ref, w_ref, qd_ref, kd_ref, p_ref, cd_ref, o_ref, s_all_ref, vn_ref, s_ref):
        @pl.when(pl.program_id(0) == 0)
        def _():
            s_ref[...] = jnp.zeros_like(s_ref)

        hss = [slice(hh * HEAD_DIM, (hh + 1) * HEAD_DIM) for hh in range(GDN_HEADS)]
        css = [slice(hh * CHUNK, (hh + 1) * CHUNK) for hh in range(GDN_HEADS)]
        s_cur = [s_ref[hh] for hh in range(GDN_HEADS)]
        for ci in range(cb):
            rs = slice(ci * CHUNK, (ci + 1) * CHUNK)
            for hh in range(GDN_HEADS):
                s_all_ref[ci * GDN_WIDTH + hh * HEAD_DIM:ci * GDN_WIDTH + (hh + 1) * HEAD_DIM, :] = s_cur[hh]
            s_ms = [s.astype(MXU_DTYPE) for s in s_cur]
            w_s = [_dot(w_ref[rs, hs], s_m, NN) for hs, s_m in zip(hss, s_ms)]
            q_s = [_dot(qd_ref[rs, hs], s_m, NN) for hs, s_m in zip(hss, s_ms)]
            v_ms = [(u_ref[rs, hs] - ws_).astype(MXU_DTYPE) for hs, ws_ in zip(hss, w_s)]
            k_v = [_dot(kd_ref[rs, hs], v_m, TN) for hs, v_m in zip(hss, v_ms)]
            p_v = [_dot(p_ref[rs, cs], v_m, NN) for cs, v_m in zip(css, v_ms)]
            for hh in range(GDN_HEADS):
                vn_ref[rs, hss[hh]] = v_ms[hh]
                o_ref[rs, hss[hh]] = q_s[hh] + p_v[hh]
                c_dec = cd_ref[ci * CHUNK:ci * CHUNK + 1, GDN_HEADS + hh:GDN_HEADS + hh + 1]
                s_cur[hh] = s_cur[hh] * c_dec + k_v[hh]
        for hh in range(GDN_HEADS):
            s_ref[hh] = s_cur[hh]

    def spec(width):
        return pl.BlockSpec((rows, width), lambda n: (n, 0))

    hw, cw = GDN_WIDTH, GDN_HEADS * CHUNK
    return pl.pallas_call(
        body, name=name, grid=(T // rows,),
        in_specs=[spec(hw), spec(hw), spec(hw), spec(hw), spec(cw), spec(128)],
        out_specs=[spec(hw), pl.BlockSpec((cb * GDN_WIDTH, HEAD_DIM), lambda n: (n, 0)), spec(hw)],
        out_shape=[jax.ShapeDtypeStruct((T, hw), F32),
                   jax.ShapeDtypeStruct((T // CHUNK * GDN_WIDTH, HEAD_DIM), F32),
                   jax.ShapeDtypeStruct((T, hw), MXU_DTYPE)],
        scratch_shapes=[pltpu.VMEM((GDN_HEADS, HEAD_DIM, HEAD_DIM), F32)],
        compiler_params=_params("arbitrary"),
    )(u, w, qd, kd, p, cd)


def _gdn_scan_bwd(do, w, qd, kd, p, cd, s_all, vn, name):
    T = do.shape[0]
    cb = _tile(T // CHUNK, SCAN_CHUNKS_PER_STEP)
    rows = cb * CHUNK
    n_steps = T // rows

    def body(do_ref, w_ref, qd_ref, kd_ref, p_ref, cd_ref, s_all_ref, vn_ref,
             dvn_ref, dw_ref, dqd_ref, dkd_ref, dp_ref, dcd_ref, ds_ref):
        @pl.when(pl.program_id(0) == 0)
        def _():
            ds_ref[...] = jnp.zeros_like(ds_ref)

        causal, _, _ = _chunk_masks()
        lane = lax.broadcasted_iota(jnp.int32, (CHUNK, 128), 1)
        heads = range(GDN_HEADS)
        hss = [slice(hh * HEAD_DIM, (hh + 1) * HEAD_DIM) for hh in heads]
        css = [slice(hh * CHUNK, (hh + 1) * CHUNK) for hh in heads]
        ds_cur = [ds_ref[hh] for hh in heads]
        for ci in reversed(range(cb)):
            rs = slice(ci * CHUNK, (ci + 1) * CHUNK)
            ds_ms = [d.astype(MXU_DTYPE) for d in ds_cur]
            s_olds = [s_all_ref[ci * GDN_WIDTH + hh * HEAD_DIM:ci * GDN_WIDTH + (hh + 1) * HEAD_DIM, :] for hh in heads]
            s_ms = [s.astype(MXU_DTYPE) for s in s_olds]
            do_ms = [do_ref[rs, hs].astype(MXU_DTYPE) for hs in hss]
            p_do = [_dot(p_ref[rs, cs], do_m, TN) for cs, do_m in zip(css, do_ms)]
            k_ds = [_dot(kd_ref[rs, hs], ds_m, NN) for hs, ds_m in zip(hss, ds_ms)]
            q_do = [_dot(qd_ref[rs, hs], do_m, TN) for hs, do_m in zip(hss, do_ms)]
            dqds = [_dot(do_m, s_m, NT) for do_m, s_m in zip(do_ms, s_ms)]
            dkds = [_dot(vn_ref[rs, hs], ds_m, NT) for hs, ds_m in zip(hss, ds_ms)]
            dps = [_dot(do_m, vn_ref[rs, hs], NT) for hs, do_m in zip(hss, do_ms)]
            dv_news = [a + b for a, b in zip(p_do, k_ds)]
            dvn_ms = [d.astype(MXU_DTYPE) for d in dv_news]
            w_dv = [_dot(w_ref[rs, hs], dvn_m, TN) for hs, dvn_m in zip(hss, dvn_ms)]
            dws = [_dot(dvn_m, s_m, NT) for dvn_m, s_m in zip(dvn_ms, s_ms)]
            dcd_tile = jnp.zeros((CHUNK, 128), F32)
            for hh in heads:
                dvn_ref[rs, hss[hh]] = dv_news[hh]
                dw_ref[rs, hss[hh]] = -dws[hh]
                dqd_ref[rs, hss[hh]] = dqds[hh]
                dkd_ref[rs, hss[hh]] = dkds[hh]
                dp_ref[rs, css[hh]] = jnp.where(causal, dps[hh], 0.0)
                dcd = jnp.sum(jnp.sum(s_olds[hh] * ds_cur[hh], axis=1, keepdims=True), axis=0, keepdims=True)
                dcd_tile = jnp.where(lane == GDN_HEADS + hh, dcd, dcd_tile)
                c_dec = cd_ref[ci * CHUNK:ci * CHUNK + 1, GDN_HEADS + hh:GDN_HEADS + hh + 1]
                ds_cur[hh] = c_dec * ds_cur[hh] + q_do[hh] - w_dv[hh]
            dcd_ref[rs, :] = dcd_tile
        for hh in heads:
            ds_ref[hh] = ds_cur[hh]

    def spec(width):
        return pl.BlockSpec((rows, width), lambda n: (n_steps - 1 - n, 0))

    hw, cw = GDN_WIDTH, GDN_HEADS * CHUNK
    return pl.pallas_call(
        body, name=name, grid=(n_steps,),
        in_specs=[spec(hw), spec(hw), spec(hw), spec(hw), spec(cw), spec(128),
                  pl.BlockSpec((cb * GDN_WIDTH, HEAD_DIM), lambda n: (n_steps - 1 - n, 0)), spec(hw)],
        out_specs=[spec(hw), spec(hw), spec(hw), spec(hw), spec(cw), spec(128)],
        out_shape=[jax.ShapeDtypeStruct((T, hw), F32)] * 4
        + [jax.ShapeDtypeStruct((T, cw), F32), jax.ShapeDtypeStruct((T, 128), F32)],
        scratch_shapes=[pltpu.VMEM((GDN_HEADS, HEAD_DIM, HEAD_DIM), F32)],
        compiler_params=_params("arbitrary"),
    )(do, w, qd, kd, p, cd, s_all, vn)


def _gdn_chunk_bwd(qkv, bg, tinv_all, u, w, dvn, dw, dqd, dkd, dp, dcd, name):
    T = qkv.shape[0]
    cb = _tile(T // CHUNK, CHUNKS_PER_STEP)
    rows = cb * CHUNK

    def body(qkv_ref, bg_ref, tinv_ref, u_ref, w_ref, dvn_ref, dw_ref, dqd_ref, dkd_ref, dp_ref, dcd_ref,
             dqkv_ref, dbg_ref):
        masks = _chunk_masks()
        causal, strict, _ = masks
        lane = lax.broadcasted_iota(jnp.int32, (CHUNK, 128), 1)
        row = lax.broadcasted_iota(jnp.int32, (CHUNK, 128), 0)
        heads = []
        for ci in range(cb):
            rs = slice(ci * CHUNK, (ci + 1) * CHUNK)
            bgv = bg_ref[rs, :]
            gc_all = _chunk_cumsum(bgv)
            gc_rows = _row_form(gc_all)
            for hh in range(GDN_HEADS):
                qs, ks, vs = _head_slices(hh)
                q, k = qkv_ref[rs, qs], qkv_ref[rs, ks]
                beta = bgv[:, hh:hh + 1]
                gc = gc_all[:, GDN_HEADS + hh:GDN_HEADS + hh + 1]
                decay, kb, eg = _head_elementwise(k, beta, gc, gc_rows[GDN_HEADS + hh:GDN_HEADS + hh + 1, :], causal)
                heads.append(dict(ci=ci, hh=hh, rs=rs, hs=slice(hh * HEAD_DIM, (hh + 1) * HEAD_DIM),
                                  cs=slice(hh * CHUNK, (hh + 1) * CHUNK), q=q, k=k, beta=beta, gc=gc,
                                  decay=decay, kb=kb, eg=eg))
        for hd in heads:
            hd["t"] = _split(tinv_ref[hd["rs"], hd["cs"]])
        for hd in heads:
            hd["kk"] = _mdot(hd["kb"], hd["k"], NT)
            hd["qk"] = _mdot(hd["q"], hd["k"], NT)
        for hd in heads:
            hd["dvb"] = _dot3(hd["t"], _split(dvn_ref[hd["rs"], hd["hs"]]), TN)
            hd["dkbeg"] = _dot3(hd["t"], _split(dw_ref[hd["rs"], hd["hs"]]), TN)
        for hd in heads:
            rs, hs = hd["rs"], hd["hs"]
            da = -(_mdot(hd["dvb"], u_ref[rs, hs], NT) + _mdot(hd["dkbeg"], w_ref[rs, hs], NT))
            dm = jnp.where(strict, da, 0.0)
            dp_ = dp_ref[rs, hd["cs"]]
            hd["dkk"] = dm * hd["decay"]
            hd["dqk"] = dp_ * hd["decay"]
            hd["e"] = (hd["dkk"] * hd["kk"] + hd["dqk"] * hd["qk"])
        for hd in heads:
            hd["dkb"] = _mdot(hd["dkk"], hd["k"], NN)
            hd["dk"] = _mdot(hd["dkk"], hd["kb"], TN) + _mdot(hd["dqk"], hd["q"], TN)
            hd["dq"] = _mdot(hd["dqk"], hd["k"], NN)
            onehot = (lane == GDN_HEADS + hd["hh"]).astype(jnp.bfloat16)
            e_hi, e_lo = _split(hd["e"])
            hd["col_sums"] = _dot(e_lo, onehot, TN) + _dot(e_hi, onehot, TN)
        tiles = {}
        for hd in heads:
            ci, hh, rs, hs = hd["ci"], hd["hh"], hd["rs"], hd["hs"]
            qs, ks, vs = _head_slices(hh)
            q, k, beta, gc, eg, kb = hd["q"], hd["k"], hd["beta"], hd["gc"], hd["eg"], hd["kb"]
            v = qkv_ref[rs, vs]
            dqd_, dkd_ = dqd_ref[rs, hs], dkd_ref[rs, hs]
            gl = gc[CHUNK - 1:CHUNK, :]
            ek = jnp.exp(gl - gc)
            dkb = hd["dkb"] + hd["dkbeg"] * eg
            deg = jnp.sum(dqd_ * q, axis=1, keepdims=True) + jnp.sum(hd["dkbeg"] * kb, axis=1, keepdims=True)
            dek = jnp.sum(dkd_ * k, axis=1, keepdims=True)
            dcd_ = dcd_ref[ci * CHUNK:ci * CHUNK + 1, GDN_HEADS + hh:GDN_HEADS + hh + 1]
            dgl = jnp.sum(dek * ek, axis=0, keepdims=True) + dcd_ * jnp.exp(gl)
            dgc = jnp.sum(hd["e"], axis=1, keepdims=True) + deg * eg - dek * ek
            dbeta_tile, dgc_tile = tiles.get(ci, (jnp.zeros((CHUNK, 128), F32), jnp.zeros((CHUNK, 128), F32)))
            dgc_tile += jnp.where(lane == GDN_HEADS + hh, dgc, 0.0) - hd["col_sums"]
            dgc_tile += jnp.where((lane == GDN_HEADS + hh) & (row == CHUNK - 1), dgl, 0.0)
            dbeta = jnp.sum(dkb * k, axis=1, keepdims=True) + jnp.sum(hd["dvb"] * v, axis=1, keepdims=True)
            dbeta_tile += jnp.where(lane == hh, dbeta, 0.0)
            tiles[ci] = (dbeta_tile, dgc_tile)
            dqkv_ref[rs, qs] = hd["dq"] + dqd_ * eg
            dqkv_ref[rs, ks] = hd["dk"] + dkd_ * ek + dkb * beta
            dqkv_ref[rs, vs] = hd["dvb"] * beta
        for ci in range(cb):
            dbeta_tile, dgc_tile = tiles[ci]
            dbg_ref[ci * CHUNK:(ci + 1) * CHUNK, :] = dbeta_tile + _chunk_cumsum(dgc_tile, reverse=True)

    def spec(width):
        return pl.BlockSpec((rows, width), lambda n: (n, 0))

    hw, cw = GDN_WIDTH, GDN_HEADS * CHUNK
    return pl.pallas_call(
        body, name=name, grid=(T // rows,),
        in_specs=[spec(QKV_WIDTH), spec(128), spec(cw), spec(hw), spec(hw), spec(hw), spec(hw), spec(hw),
                  spec(hw), spec(cw), spec(128)],
        out_specs=[spec(QKV_WIDTH), spec(128)],
        out_shape=[jax.ShapeDtypeStruct((T, QKV_WIDTH), F32), jax.ShapeDtypeStruct((T, 128), F32)],
        compiler_params=_params("parallel"),
    )(qkv, bg, tinv_all, u, w, dvn, dw, dqd, dkd, dp, dcd)


def _pool_counts(i, tm, rows, offset):
    t = i * tm - offset + lax.broadcasted_iota(jnp.int32, (rows, 1), 0)
    return [jnp.minimum(t + 1, w).astype(F32) for w in POOL_WINDOWS]


def _window_sums(window, forward):
    sums, s, step = [], window, 1
    for _ in POOL_WINDOWS:
        s = s + _shift_rows(s, -step if forward else step)
        sums.append(s)
        step *= 2
    return sums


def _pooled(window, counts):
    sums = _window_sums(window, forward=False)
    out = []
    for gi in range(POOL_GROUPS):
        sl = slice(gi * 128, (gi + 1) * 128)
        out.append(sums[gi][HALO:, sl] / counts[gi] - window[HALO:, sl])
    return out


def _mix_post(o, proj, gdn_norm, pool_w, pool_scale, name):
    T = o.shape[0]
    tm = _tile(T, 256)
    hb = tm // HALO

    def body(o_ref, z_ref, p_ref, ph_ref, gn_ref, pw_ref, ps_ref, out_ref):
        i = pl.program_id(0)
        for hh in range(GDN_HEADS):
            sl = slice(hh * HEAD_DIM, (hh + 1) * HEAD_DIM)
            oh, zh = o_ref[:, sl], z_ref[:, sl]
            ro = lax.rsqrt(jnp.mean(oh * oh, axis=-1, keepdims=True) + EPS)
            out_ref[:, sl] = (((oh * ro) * gn_ref[...]) * (zh * _sigmoid(zh))).astype(out_ref.dtype)
        halo = jnp.where(i == 0, 0.0, ph_ref[...])
        window = jnp.concatenate([halo, p_ref[...]], axis=0)
        pooled = _pooled(window, _pool_counts(i, tm, tm, 0))
        for gi in range(POOL_GROUPS):
            pm = _mdot(pooled[gi], pw_ref[gi], NN)
            out_ref[:, GDN_WIDTH + gi * 128:GDN_WIDTH + (gi + 1) * 128] = (
                pm * ps_ref[:, gi * 128:(gi + 1) * 128]).astype(out_ref.dtype)

    return pl.pallas_call(
        body, name=name, grid=(T // tm,),
        in_specs=[pl.BlockSpec((tm, GDN_WIDTH), lambda i: (i, 0)),
                  pl.BlockSpec((tm, GDN_WIDTH), lambda i: (i, COL_Z // GDN_WIDTH)),
                  pl.BlockSpec((tm, POOL_WIDTH), lambda i: (i, COL_P // POOL_WIDTH)),
                  pl.BlockSpec((HALO, POOL_WIDTH), lambda i: (jnp.maximum(i * hb - 1, 0), COL_P // POOL_WIDTH)),
                  pl.BlockSpec((1, HEAD_DIM), lambda i: (0, 0)),
                  pl.BlockSpec((POOL_GROUPS, 128, 128), lambda i: (0, 0, 0)),
                  pl.BlockSpec((1, POOL_WIDTH), lambda i: (0, 0))],
        out_specs=pl.BlockSpec((tm, GDN_WIDTH + POOL_WIDTH), lambda i: (i, 0)),
        out_shape=jax.ShapeDtypeStruct((T, GDN_WIDTH + POOL_WIDTH), MXU_DTYPE),
        compiler_params=_params("parallel"),
    )(o, proj, proj, proj, gdn_norm, pool_w, pool_scale)


def _mix_post_bwd(dmix, o, proj, gdn_norm, pool_w, pool_scale, name):
    T = o.shape[0]
    tm = _tile(T, 256)
    hb = tm // HALO
    n_tiles = T // tm

    def body(dg_ref, dpo_ref, dpo_next_ref, o_ref, z_ref, p_ref, ph_ref, gn_ref, pw_ref, ps_ref,
             do_ref, dzp_ref, dgn_ref, dpw_ref, dps_ref):
        i = pl.program_id(0)

        @pl.when(i == 0)
        def _():
            dgn_ref[...] = jnp.zeros_like(dgn_ref)
            dpw_ref[...] = jnp.zeros_like(dpw_ref)
            dps_ref[...] = jnp.zeros_like(dps_ref)

        gn = gn_ref[...]
        dgn = jnp.zeros((1, HEAD_DIM), F32)
        for hh in range(GDN_HEADS):
            sl = slice(hh * HEAD_DIM, (hh + 1) * HEAD_DIM)
            oh, zh, dy = o_ref[:, sl], z_ref[:, sl], dg_ref[:, sl]
            ro = lax.rsqrt(jnp.mean(oh * oh, axis=-1, keepdims=True) + EPS)
            on = oh * ro
            sig = _sigmoid(zh)
            sz = zh * sig
            dzp_ref[:, sl] = (dy * (on * gn) * (sig * (1.0 + zh * (1.0 - sig)))).astype(dzp_ref.dtype)
            dgn += jnp.sum(dy * on * sz, axis=0, keepdims=True)
            don = dy * gn * sz
            do_ref[:, sl] = ro * (don - on * jnp.mean(don * on, axis=-1, keepdims=True))
        dgn_ref[...] += dgn

        halo = jnp.where(i == 0, 0.0, ph_ref[...])
        window = jnp.concatenate([halo, p_ref[...]], axis=0)
        counts = _pool_counts(i, tm, tm + HALO, 0)
        pooled = _pooled(window, [cn[:tm] for cn in counts])
        nxt = jnp.where(i == n_tiles - 1, 0.0, dpo_next_ref[...])
        dpo_w = jnp.concatenate([dpo_ref[...], nxt], axis=0)
        ps = ps_ref[...]
        dps = []
        scaled = []
        for gi in range(POOL_GROUPS):
            sl = slice(gi * 128, (gi + 1) * 128)
            dpm = dpo_w[:, sl] * ps[:, sl]
            pm = _mdot(pooled[gi], pw_ref[gi], NN)
            dps.append(jnp.sum(dpo_w[:tm, sl] * pm, axis=0, keepdims=True))
            dpw_ref[gi] += _mdot(pooled[gi], dpm[:tm], TN)
            dpooled = _mdot(dpm, pw_ref[gi], NT)
            scaled.append((dpooled, dpooled / counts[gi]))
        dps_ref[...] += jnp.concatenate(dps, axis=1)
        lead = _window_sums(jnp.concatenate([sc for _, sc in scaled], axis=1), forward=True)
        for gi in range(POOL_GROUPS):
            sl = slice(gi * 128, (gi + 1) * 128)
            dzp_ref[:, GDN_WIDTH + gi * 128:GDN_WIDTH + (gi + 1) * 128] = (
                lead[gi][:tm, sl] - scaled[gi][0][:tm]).astype(dzp_ref.dtype)

    last_halo = T // HALO - 1
    return pl.pallas_call(
        body, name=name, grid=(n_tiles,),
        in_specs=[pl.BlockSpec((tm, GDN_WIDTH), lambda i: (i, 0)),
                  pl.BlockSpec((tm, POOL_WIDTH), lambda i: (i, 1)),
                  pl.BlockSpec((HALO, POOL_WIDTH), lambda i: (jnp.minimum((i + 1) * hb, last_halo), 1)),
                  pl.BlockSpec((tm, GDN_WIDTH), lambda i: (i, 0)),
                  pl.BlockSpec((tm, GDN_WIDTH), lambda i: (i, COL_Z // GDN_WIDTH)),
                  pl.BlockSpec((tm, POOL_WIDTH), lambda i: (i, COL_P // POOL_WIDTH)),
                  pl.BlockSpec((HALO, POOL_WIDTH), lambda i: (jnp.maximum(i * hb - 1, 0), COL_P // POOL_WIDTH)),
                  pl.BlockSpec((1, HEAD_DIM), lambda i: (0, 0)),
                  pl.BlockSpec((POOL_GROUPS, 128, 128), lambda i: (0, 0, 0)),
                  pl.BlockSpec((1, POOL_WIDTH), lambda i: (0, 0))],
        out_specs=[pl.BlockSpec((tm, GDN_WIDTH), lambda i: (i, 0)),
                   pl.BlockSpec((tm, GDN_WIDTH + POOL_WIDTH), lambda i: (i, 0)),
                   pl.BlockSpec((1, HEAD_DIM), lambda i: (0, 0)),
                   pl.BlockSpec((POOL_GROUPS, 128, 128), lambda i: (0, 0, 0)),
                   pl.BlockSpec((1, POOL_WIDTH), lambda i: (0, 0))],
        out_shape=[jax.ShapeDtypeStruct((T, GDN_WIDTH), F32),
                   jax.ShapeDtypeStruct((T, GDN_WIDTH + POOL_WIDTH), MXU_DTYPE),
                   jax.ShapeDtypeStruct((1, HEAD_DIM), F32),
                   jax.ShapeDtypeStruct((POOL_GROUPS, 128, 128), F32),
                   jax.ShapeDtypeStruct((1, POOL_WIDTH), F32)],
        compiler_params=_params("arbitrary"),
    )(dmix, dmix, dmix, o, proj, proj, proj, gdn_norm, pool_w, pool_scale)


def _gdn_prep_bwd(proj, conv_w, a_log_l, dt_bias_l, dqkv, dbg, dzp, name):
    T = proj.shape[0]
    tm = _tile(T, 256)
    hb = tm // 8
    n_tiles = T // tm
    last_halo = T // 8 - 1

    def body(cur_ref, before_ref, after_ref, ba_ref, w_ref, al_ref, dtb_ref, dq_ref, dq_after_ref, dbg_ref,
             dzp_ref, dproj_ref, dw_ref, dal_ref, ddtb_ref):
        i = pl.program_id(0)

        @pl.when(i == 0)
        def _():
            dw_ref[...] = jnp.zeros_like(dw_ref)
            dal_ref[...] = jnp.zeros_like(dal_ref)
            ddtb_ref[...] = jnp.zeros_like(ddtb_ref)

        last = i == n_tiles - 1
        w = w_ref[...]
        before = jnp.where(i == 0, 0.0, before_ref[...])
        after = jnp.where(last, 0.0, after_ref[...])
        window = jnp.concatenate([before, cur_ref[...], after], axis=0)
        y = _conv_act(window, w)
        sig = _sigmoid(y)
        act = y * sig
        dq_w = jnp.concatenate([jnp.zeros((8, QKV_WIDTH), F32), dq_ref[...],
                                jnp.where(last, 0.0, dq_after_ref[...])], axis=0)
        dact = []
        for hh in range(3 * GDN_HEADS):
            sl = slice(hh * HEAD_DIM, (hh + 1) * HEAD_DIM)
            blk, dblk = act[:, sl], dq_w[:, sl]
            if hh < 2 * GDN_HEADS:
                rn = lax.rsqrt(jnp.sum(blk * blk, axis=-1, keepdims=True) + EPS)
                unit = blk * rn
                if hh < GDN_HEADS:
                    dblk = dblk * (HEAD_DIM ** -0.5)
                dblk = rn * (dblk - unit * jnp.sum(dblk * unit, axis=-1, keepdims=True))
            dact.append(dblk)
        dy = jnp.concatenate(dact, axis=1) * (sig * (1.0 + y * (1.0 - sig)))
        dx = dy * w[CONV_K - 1:CONV_K, :]
        dws = [None] * CONV_K
        dws[CONV_K - 1] = jnp.sum(dy[8:8 + tm] * window[8:8 + tm], axis=0, keepdims=True)
        for j in range(CONV_K - 1):
            s = CONV_K - 1 - j
            dx += _shift_rows(dy, -s) * w[j:j + 1, :]
            dws[j] = jnp.sum(dy[8:8 + tm] * _shift_rows(window, s)[8:8 + tm], axis=0, keepdims=True)
        dw_ref[...] += jnp.concatenate(dws, axis=0)
        dproj_ref[:, :QKV_WIDTH] = dx[8:8 + tm].astype(dproj_ref.dtype)
        dproj_ref[:, COL_Z:COL_BA] = dzp_ref[...]

        ba = ba_ref[...]
        dbg_ = dbg_ref[...]
        lane = lax.broadcasted_iota(jnp.int32, ba.shape, 1)
        beta = _sigmoid(ba)
        pre = ba + dtb_ref[...]
        neg_a = -jnp.exp(al_ref[...])
        g = neg_a * _softplus(pre)
        is_g = (lane >= GDN_HEADS) & (lane < 2 * GDN_HEADS)
        da_raw = jnp.where(is_g, dbg_ * neg_a * _sigmoid(pre), 0.0)
        dba = jnp.where(lane < GDN_HEADS, dbg_ * beta * (1.0 - beta), da_raw)
        dproj_ref[:, COL_BA:] = dba.astype(dproj_ref.dtype)
        dal_ref[...] += jnp.sum(jnp.where(is_g, dbg_ * g, 0.0), axis=0, keepdims=True)
        ddtb_ref[...] += jnp.sum(da_raw, axis=0, keepdims=True)

    lane_vec = pl.BlockSpec((1, 128), lambda i: (0, 0))
    return pl.pallas_call(
        body, name=name, grid=(n_tiles,),
        in_specs=[pl.BlockSpec((tm, QKV_WIDTH), lambda i: (i, 0)),
                  pl.BlockSpec((8, QKV_WIDTH), lambda i: (jnp.maximum(i * hb - 1, 0), 0)),
                  pl.BlockSpec((8, QKV_WIDTH), lambda i: (jnp.minimum((i + 1) * hb, last_halo), 0)),
                  pl.BlockSpec((tm, 128), lambda i: (i, COL_BA // 128)),
                  pl.BlockSpec((CONV_K, QKV_WIDTH), lambda i: (0, 0)), lane_vec, lane_vec,
                  pl.BlockSpec((tm, QKV_WIDTH), lambda i: (i, 0)),
                  pl.BlockSpec((8, QKV_WIDTH), lambda i: (jnp.minimum((i + 1) * hb, last_halo), 0)),
                  pl.BlockSpec((tm, 128), lambda i: (i, 0)),
                  pl.BlockSpec((tm, GDN_WIDTH + POOL_WIDTH), lambda i: (i, 0))],
        out_specs=[pl.BlockSpec((tm, D_IN_PAD), lambda i: (i, 0)),
                   pl.BlockSpec((CONV_K, QKV_WIDTH), lambda i: (0, 0)), lane_vec, lane_vec],
        out_shape=[jax.ShapeDtypeStruct((T, D_IN_PAD), MXU_DTYPE),
                   jax.ShapeDtypeStruct((CONV_K, QKV_WIDTH), F32),
                   jax.ShapeDtypeStruct((1, 128), F32), jax.ShapeDtypeStruct((1, 128), F32)],
        compiler_params=_params("arbitrary"),
    )(proj, proj, proj, proj, conv_w, a_log_l, dt_bias_l, dqkv, dqkv, dbg, dzp)


def _mod_part(c_all, w_ada, b_part, name):
    def body(c_ref, w_ref, b_ref, out_ref):
        cc = c_ref[...]
        out_ref[...] = _mdot(cc * _sigmoid(cc), w_ref[...], NN) + b_ref[...]

    return pl.pallas_call(
        body, name=name, out_shape=jax.ShapeDtypeStruct((c_all.shape[0], w_ada.shape[1]), F32),
        compiler_params=_params(),
    )(c_all, w_ada, b_part)


def _w_ada_grad(c_all, dmod_part, name):
    def body(c_ref, d_ref, out_ref):
        cc = c_ref[...]
        out_ref[...] = _mdot(cc * _sigmoid(cc), d_ref[...], TN)

    return pl.pallas_call(
        body, name=name, out_shape=jax.ShapeDtypeStruct((c_all.shape[1], dmod_part.shape[1]), F32),
        compiler_params=_params(),
    )(c_all, dmod_part)


def _sum_parts(parts, name):
    _, R, C = parts.shape
    tr = _tile(R, 256)

    def body(p_ref, out_ref):
        acc = p_ref[0].astype(F32)
        for s in range(1, N_DEV):
            acc += p_ref[s].astype(F32)
        out_ref[...] = acc

    return pl.pallas_call(
        body, name=name, grid=(R // tr,),
        in_specs=[pl.BlockSpec((N_DEV, tr, C), lambda i: (0, i, 0))],
        out_specs=pl.BlockSpec((tr, C), lambda i: (i, 0)),
        out_shape=jax.ShapeDtypeStruct((R, C), F32),
        compiler_params=_params("parallel"),
    )(parts)


def _adamw(w, g, m, v, name):
    R, C = w.shape
    tr = _tile(R, 256)

    def body(w_ref, g_ref, m_ref, v_ref, d_ref, mo_ref, vo_ref):
        gg = g_ref[...]
        mm = ADAM_B1 * m_ref[...] + (1.0 - ADAM_B1) * gg
        vv = ADAM_B2 * v_ref[...] + (1.0 - ADAM_B2) * (gg * gg)
        m_hat = mm / (1.0 - ADAM_B1 ** ADAM_STEP)
        v_hat = vv / (1.0 - ADAM_B2 ** ADAM_STEP)
        d_ref[...] = -ADAM_LR * (m_hat / (jnp.sqrt(v_hat) + ADAM_EPS) + ADAM_WD * w_ref[...])
        mo_ref[...] = mm
        vo_ref[...] = vv

    spec = pl.BlockSpec((tr, C), lambda i: (i, 0))
    return pl.pallas_call(
        body, name=name, grid=(R // tr,),
        in_specs=[spec] * 4, out_specs=[spec] * 3,
        out_shape=[jax.ShapeDtypeStruct((R, C), F32)] * 3,
        compiler_params=_params("parallel"),
    )(w, g, m, v)


def _swiglu_backward(tag, dxo, y, gate, h, g, u, a, wg_t, wu_t, wd):
    dy, dgate = _gate_bwd(dxo, y, gate, 0.5, f"{tag}_gate_bwd")
    dg, du = _ffn_bwd_act(dy, wd, g, u, f"{tag}_bwd_act")
    d_wd = _matmul([(a, dy)], TN, F32, f"{tag}_dwd", tm=1408, tn=1024, tk=512)
    d_wg_t = _matmul([(dg, h)], TN, F32, f"{tag}_dwg", tm=1408, tn=1024, tk=512)
    d_wu_t = _matmul([(du, h)], TN, F32, f"{tag}_dwu", tm=1408, tn=1024, tk=512)
    dh = _matmul([(dg, wg_t), (du, wu_t)], NN, F32, f"{tag}_dh", tm=512, tn=1024, tk=1408)
    return dh, dgate, d_wg_t, d_wu_t, d_wd


def _rows_of(flat, lanes=1024):
    flat = flat.reshape(-1)
    n = -(-flat.shape[0] // lanes) * lanes
    return jnp.pad(flat, (0, n - flat.shape[0])).reshape(n // lanes, lanes)


def _pad_rows(a, rows):
    return jnp.pad(a, ((0, rows - a.shape[0]), (0, 0)))


def kernel(x, c, w_ada, b_ada, norm_ffn1, ffn1_gate, ffn1_up, ffn1_down, norm_mix, w_in, conv_w, a_log, dt_bias, gdn_norm, pool_w, pool_scale, w_out, norm_ffn2, ffn2_gate, ffn2_up, ffn2_down, final_norm, loss_target, m_w_ada, m_b_ada, m_norm_ffn1, m_ffn1_gate, m_ffn1_up, m_ffn1_down, m_norm_mix, m_w_in, m_conv_w, m_a_log, m_dt_bias, m_gdn_norm, m_pool_w, m_pool_scale, m_w_out, m_norm_ffn2, m_ffn2_gate, m_ffn2_up, m_ffn2_down, m_final_norm, v_w_ada, v_b_ada, v_norm_ffn1, v_ffn1_gate, v_ffn1_up, v_ffn1_down, v_norm_mix, v_w_in, v_conv_w, v_a_log, v_dt_bias, v_gdn_norm, v_pool_w, v_pool_scale, v_w_out, v_norm_ffn2, v_ffn2_gate, v_ffn2_up, v_ffn2_down, v_final_norm):
    T, D = x.shape[1], x.shape[2]
    Fs = ffn1_gate.shape[2]
    Ws = w_in.shape[2]
    Ws_pad = -(-Ws // 16) * 16
    Os = w_out.shape[1]
    Ms = w_ada.shape[2]
    Cs = conv_w.shape[2]
    me = 4 * lax.axis_index("x") + 2 * lax.axis_index("y") + lax.axis_index("c")
    x0, target = x[0], loss_target[0]

    small = jnp.concatenate([_pad_rows(c, 8), _pad_rows(jnp.pad(conv_w[0], ((0, 0), (0, D - Cs))), 8)], axis=0)
    got = _all_gather(small, "gather_small")
    c_all = got[:, 0, :]
    conv_full = jnp.transpose(got[:, 8:8 + CONV_K, :Cs], (1, 0, 2)).reshape(CONV_K, QKV_WIDTH)

    def wire(a):
        return a.astype(WIRE_DTYPE)

    blocks = [wire(ffn1_gate[0].T), wire(ffn1_up[0].T), wire(ffn1_down[0]),
              wire(_pad_rows(w_in[0].T, Ws_pad)), wire(w_out[0]),
              wire(ffn2_gate[0].T), wire(ffn2_up[0].T), wire(ffn2_down[0])]
    sizes = [b.shape[0] for b in blocks]
    offs = [sum(sizes[:i]) for i in range(len(sizes))]
    gathered = _all_gather(jnp.concatenate(blocks, axis=0), "gather_weights")

    def full(idx, keep=None):
        blk = gathered[:, offs[idx]:offs[idx] + sizes[idx], :]
        if keep is not None:
            blk = blk[:, :keep, :]
        return blk.reshape(-1, D).astype(MXU_DTYPE)

    wg1_t, wu1_t, wd1 = full(0), full(1), full(2)
    w_in_t = full(3, Ws)
    wo = full(4)
    wg2_t, wu2_t, wd2 = full(5), full(6), full(7)
    w_in_re = jnp.concatenate([w_in_t[:COL_Z + GDN_WIDTH], w_in_t[D_IN - POOL_WIDTH:],
                               w_in_t[4 * GDN_WIDTH:4 * GDN_WIDTH + 2 * GDN_HEADS],
                               jnp.zeros((128 - 2 * GDN_HEADS, D), MXU_DTYPE)], axis=0)

    b_part = lax.dynamic_slice(b_ada, (0, me * Ms), (1, Ms))
    mod_parts = _all_gather(_mod_part(c_all, w_ada[0], b_part, "mod_part"), "gather_mod")
    mod_all = jnp.transpose(mod_parts, (1, 0, 2)).reshape(N_DEV, N_MOD * D)
    mod = lax.dynamic_slice(mod_all, (me, 0), (1, N_MOD * D)).reshape(N_MOD, 1, D)
    sh1, sc1, gt1, sh2, sc2, gt2, sh3, sc3, gt3 = [mod[i] for i in range(N_MOD)]

    lane_pad = lambda a: jnp.pad(a, ((0, 0), (GDN_HEADS, 128 - 2 * GDN_HEADS)))
    a_log_l, dt_bias_l = lane_pad(a_log), lane_pad(dt_bias)
    pool_w_m = pool_w[0].astype(MXU_DTYPE)

    h1 = _norm_mod(x0, norm_ffn1, sc1, sh1, "norm1")
    y1, g1, u1, a1 = _ffn_fwd(h1, wg1_t, wu1_t, wd1, "ffn1_fwd")
    x1, h2 = _resid_norm_mod(x0, y1, gt1, 0.5, norm_mix, sc2, sh2, "resid_norm2")
    proj = _matmul([(h2, w_in_re)], NT, F32, "proj_in", tm=512, tn=D_IN_PAD // 3, tk=D)
    qkv, bg = _gdn_prep(proj, conv_full, a_log_l, dt_bias_l, "gdn_prep")
    tinv, u_c, w_c, qd_c, kd_c, p_c, cd_c = _gdn_chunk_fwd(qkv, bg, "gdn_chunk_fwd")
    o, s_all, vn_c = _gdn_scan_fwd(u_c, w_c, qd_c, kd_c, p_c, cd_c, "gdn_scan_fwd")
    mix_in = _mix_post(o, proj, gdn_norm, pool_w_m, pool_scale, "mix_post")
    mixed = _matmul([(mix_in, wo)], NN, F32, "mix_out", tm=512, tn=D, tk=GDN_WIDTH + POOL_WIDTH)
    x2, h3 = _resid_norm_mod(x1, mixed, gt2, 1.0, norm_ffn2, sc3, sh3, "resid_norm3")
    y3, g3, u3, a3 = _ffn_fwd(h3, wg2_t, wu2_t, wd2, "ffn2_fwd")
    loss_row, d3, d_final = _final_loss(x2, y3, gt3, final_norm.reshape(1, D), target, "final_loss")

    dh3, dgt3, d_wg2, d_wu2, d_wd2 = _swiglu_backward("ffn2", d3, y3, gt3, h3, g3, u3, a3, wg2_t, wu2_t, wd2)
    d2, d_n3, dsc3, dsh3 = _norm_bwd(dh3, x2, norm_ffn2, sc3, d3, "norm3_bwd")
    dmixed, dgt2 = _gate_bwd(d2, mixed, gt2, 1.0, "mix_gate_bwd")
    dmix_in = _matmul([(dmixed, wo)], NT, F32, "mix_out_bwd", tm=512, tn=GDN_WIDTH + POOL_WIDTH, tk=D)
    d_wo = _matmul([(mix_in, dmixed)], TN, F32, "mix_dwo", tm=512, tn=D, tk=512)
    do, dzp, d_gn, d_pw, d_ps = _mix_post_bwd(dmix_in, o, proj, gdn_norm, pool_w_m, pool_scale, "mix_post_bwd")
    dvn, dw_c, dqd, dkd, dp_c, dcd = _gdn_scan_bwd(do, w_c, qd_c, kd_c, p_c, cd_c, s_all, vn_c, "gdn_scan_bwd")
    dqkv, dbg = _gdn_chunk_bwd(qkv, bg, tinv, u_c, w_c, dvn, dw_c, dqd, dkd, dp_c, dcd, "gdn_chunk_bwd")
    dproj, d_conv, d_al, d_dtb = _gdn_prep_bwd(proj, conv_full, a_log_l, dt_bias_l, dqkv, dbg, dzp, "gdn_prep_bwd")
    dh2 = _matmul([(dproj, w_in_re)], NN, F32, "proj_in_bwd", tm=512, tn=D, tk=D_IN_PAD // 3)
    d_win_re = _matmul([(dproj, h2)], TN, F32, "proj_in_dw", tm=D_IN_PAD // 3, tn=D, tk=512)
    d1, d_n2, dsc2, dsh2 = _norm_bwd(dh2, x1, norm_mix, sc2, d2, "norm2_bwd")
    dh1, dgt1, d_wg1, d_wu1, d_wd1 = _swiglu_backward("ffn1", d1, y1, gt1, h1, g1, u1, a1, wg1_t, wu1_t, wd1)
    grad_x, d_n1, dsc1, dsh1 = _norm_bwd(dh1, x0, norm_ffn1, sc1, d1, "norm1_bwd")

    d_win_t = jnp.concatenate([d_win_re[:COL_Z + GDN_WIDTH], d_win_re[COL_BA:COL_BA + 2 * GDN_HEADS],
                               d_win_re[COL_P:COL_P + POOL_WIDTH]], axis=0)
    dmod = jnp.concatenate([dsh1, dsc1, dgt1, dsh2, dsc2, dgt2, dsh3, dsc3, dgt3], axis=0)
    small_rows = [dmod.reshape(-1), d_n1[0], d_n2[0], d_n3[0], d_final[0], d_gn[0], d_ps[0],
                  d_al[0, GDN_HEADS:2 * GDN_HEADS], d_dtb[0, GDN_HEADS:2 * GDN_HEADS], loss_row[0, :1],
                  d_conv.reshape(-1), d_pw.reshape(-1)]
    lanes = 1024
    small_rows = [_rows_of(r, lanes) for r in small_rows]
    n_rows = [r.shape[0] for r in small_rows]
    row_off = [sum(n_rows[:i]) for i in range(len(n_rows))]
    total = -(-sum(n_rows) // 8) * 8
    slab = _pad_rows(jnp.concatenate(small_rows, axis=0), total)
    slab_all = _all_gather(slab, "gather_small_grads")
    summed = _sum_parts(slab_all, "sum_small_grads")

    def piece(idx, n):
        return summed[row_off[idx]:row_off[idx] + n_rows[idx]].reshape(-1)[:n]

    g_b_ada = piece(0, N_MOD * D).reshape(1, N_MOD * D)
    g_n1, g_n2, g_n3 = piece(1, D).reshape(1, D), piece(2, D).reshape(1, D), piece(3, D).reshape(1, D)
    g_final = piece(4, D)
    g_gn = piece(5, HEAD_DIM).reshape(1, HEAD_DIM)
    g_ps = piece(6, POOL_WIDTH).reshape(1, POOL_WIDTH)
    g_al = piece(7, GDN_HEADS).reshape(1, GDN_HEADS)
    g_dtb = piece(8, GDN_HEADS).reshape(1, GDN_HEADS)
    loss = piece(9, 1)[0]
    g_conv = lax.dynamic_slice(piece(10, CONV_K * QKV_WIDTH).reshape(1, CONV_K, QKV_WIDTH), (0, 0, me * Cs),
                               (1, CONV_K, Cs))
    g_pw = piece(11, POOL_GROUPS * 128 * 128).reshape(1, POOL_GROUPS, 128, 128)

    dmod_all = slab_all[:, row_off[0]:row_off[0] + n_rows[0], :].reshape(N_DEV, -1)[:, :N_MOD * D]
    g_w_ada = _w_ada_grad(c_all, lax.dynamic_slice(dmod_all, (0, me * Ms), (N_DEV, Ms)), "w_ada_grad")[None]

    d_win_blocks = jnp.pad(d_win_t.reshape(N_DEV, Ws, D), ((0, 0), (0, Ws_pad - Ws), (0, 0)))
    parts = jnp.concatenate(
        [wire(d_wg1.reshape(N_DEV, Fs, D)), wire(d_wu1.reshape(N_DEV, Fs, D)), wire(d_wd1.reshape(N_DEV, Fs, D)),
         wire(d_win_blocks), wire(d_wo.reshape(N_DEV, Os, D)),
         wire(d_wg2.reshape(N_DEV, Fs, D)), wire(d_wu2.reshape(N_DEV, Fs, D)), wire(d_wd2.reshape(N_DEV, Fs, D))],
        axis=1)
    big = _sum_parts(_all_to_all(parts, "exchange_grads"), "sum_grads")

    def mine(idx, keep=None):
        blk = big[offs[idx]:offs[idx] + (sizes[idx] if keep is None else keep)]
        return blk

    g_ffn1_gate, g_ffn1_up, g_ffn1_down = mine(0).T[None], mine(1).T[None], mine(2)[None]
    g_w_in, g_w_out = mine(3, Ws).T[None], mine(4)[None]
    g_ffn2_gate, g_ffn2_up, g_ffn2_down = mine(5).T[None], mine(6).T[None], mine(7)[None]

    names = ["w_ada", "b_ada", "norm_ffn1", "ffn1_gate", "ffn1_up", "ffn1_down", "norm_mix", "w_in", "conv_w",
             "a_log", "dt_bias", "gdn_norm", "pool_w", "pool_scale", "w_out", "norm_ffn2", "ffn2_gate", "ffn2_up",
             "ffn2_down", "final_norm"]
    weights = dict(zip(names, [w_ada, b_ada, norm_ffn1, ffn1_gate, ffn1_up, ffn1_down, norm_mix, w_in, conv_w,
                               a_log, dt_bias, gdn_norm, pool_w, pool_scale, w_out, norm_ffn2, ffn2_gate, ffn2_up,
                               ffn2_down, final_norm]))
    ms = dict(zip(names, [m_w_ada, m_b_ada, m_norm_ffn1, m_ffn1_gate, m_ffn1_up, m_ffn1_down, m_norm_mix, m_w_in,
                          m_conv_w, m_a_log, m_dt_bias, m_gdn_norm, m_pool_w, m_pool_scale, m_w_out, m_norm_ffn2,
                          m_ffn2_gate, m_ffn2_up, m_ffn2_down, m_final_norm]))
    vs = dict(zip(names, [v_w_ada, v_b_ada, v_norm_ffn1, v_ffn1_gate, v_ffn1_up, v_ffn1_down, v_norm_mix, v_w_in,
                          v_conv_w, v_a_log, v_dt_bias, v_gdn_norm, v_pool_w, v_pool_scale, v_w_out, v_norm_ffn2,
                          v_ffn2_gate, v_ffn2_up, v_ffn2_down, v_final_norm]))
    grads = dict(w_ada=g_w_ada, b_ada=g_b_ada, norm_ffn1=g_n1, ffn1_gate=g_ffn1_gate, ffn1_up=g_ffn1_up,
                 ffn1_down=g_ffn1_down, norm_mix=g_n2, w_in=g_w_in, conv_w=g_conv, a_log=g_al, dt_bias=g_dtb,
                 gdn_norm=g_gn, pool_w=g_pw, pool_scale=g_ps, w_out=g_w_out, norm_ffn2=g_n3,
                 ffn2_gate=g_ffn2_gate, ffn2_up=g_ffn2_up, ffn2_down=g_ffn2_down, final_norm=g_final)
    big_names = ["w_ada", "ffn1_gate", "ffn1_up", "ffn1_down", "w_in", "w_out", "ffn2_gate", "ffn2_up", "ffn2_down"]
    delta, new_m, new_v = {}, {}, {}
    for n in big_names:
        shp = weights[n].shape
        two_d = lambda a: a.reshape(shp[-2], shp[-1])
        d_, m_, v_ = _adamw(two_d(weights[n]), two_d(grads[n]), two_d(ms[n]), two_d(vs[n]), f"adamw_{n}")
        delta[n], new_m[n], new_v[n] = d_.reshape(shp), m_.reshape(shp), v_.reshape(shp)
    small_names = [n for n in names if n not in big_names]
    pack = lambda src: jnp.concatenate([_rows_of(src[n]) for n in small_names], axis=0)
    p_rows = [_rows_of(weights[n]).shape[0] for n in small_names]
    p_total = -(-sum(p_rows) // 8) * 8
    packed = [_pad_rows(pack(src), p_total) for src in (weights, grads, ms, vs)]
    d_s, m_s, v_s = _adamw(*packed, "adamw_small")
    off = 0
    for n, r in zip(small_names, p_rows):
        shp = weights[n].shape
        size = weights[n].size
        for dst, src in ((delta, d_s), (new_m, m_s), (new_v, v_s)):
            dst[n] = src[off:off + r].reshape(-1)[:size].reshape(shp)
        off += r

    return (loss, grad_x[None], *[grads[n] for n in names], *[delta[n] for n in names],
            *[new_m[n] for n in names], *[new_v[n] for n in names])
```

```python
import functools

import jax
import jax.numpy as jnp
from jax import lax
from jax.experimental import pallas as pl
from jax.experimental.pallas import tpu as pltpu

F32 = jnp.float32
MXU_DTYPE = jnp.bfloat16
WIRE_DTYPE = jnp.bfloat16
EPS = 1e-6
N_DEV = 8
GDN_HEADS = 4
HEAD_DIM = 128
GDN_WIDTH = GDN_HEADS * HEAD_DIM
POOL_WINDOWS = (2, 4, 8, 16)
POOL_GROUPS = len(POOL_WINDOWS)
POOL_WIDTH = 512
CONV_K = 4
CHUNK = 64
QKV_WIDTH = 3 * GDN_WIDTH
D_IN = 4 * GDN_WIDTH + 2 * GDN_HEADS + POOL_WIDTH
D_IN_PAD = 4 * GDN_WIDTH + POOL_WIDTH + 128
COL_Z = QKV_WIDTH
COL_P = 4 * GDN_WIDTH
COL_BA = 4 * GDN_WIDTH + POOL_WIDTH
N_MOD = 9
HALO = 16
VMEM_LIMIT = 56 * 1024 * 1024
ADAM_LR, ADAM_B1, ADAM_B2, ADAM_EPS, ADAM_WD, ADAM_STEP = 0.001, 0.9, 0.999, 1e-08, 0.01, 10
CHUNKS_PER_STEP = 2
SCAN_CHUNKS_PER_STEP = 4

NT = (((1,), (1,)), ((), ()))
NN = (((1,), (0,)), ((), ()))
TN = (((0,), (0,)), ((), ()))


def _params(*sem):
    return pltpu.CompilerParams(dimension_semantics=tuple(sem), vmem_limit_bytes=VMEM_LIMIT)


def _dot(a, b, dims):
    return lax.dot_general(a, b, dims, preferred_element_type=F32)


def _mdot(a, b, dims):
    return _dot(a.astype(MXU_DTYPE), b.astype(MXU_DTYPE), dims)


def _split(a):
    hi = a.astype(jnp.bfloat16)
    return hi, (a - hi.astype(F32)).astype(jnp.bfloat16)


def _dot3(a, b, dims):
    (ah, al), (bh, bl) = a, b
    return (_dot(al, bh, dims) + _dot(ah, bl, dims)) + _dot(ah, bh, dims)


def _sigmoid(v):
    return 1.0 / (1.0 + jnp.exp(-v))


def _softplus(v):
    return jnp.maximum(v, 0.0) + jnp.log(1.0 + jnp.exp(-jnp.abs(v)))


def _shift_rows(v, s):
    n = v.shape[0]
    s = s % n
    return v if s == 0 else pltpu.roll(v, s, 0)


def _tile(n, want):
    t = min(n, want)
    while n % t:
        t //= 2
    return t


def _all_gather(block, name):
    shape, dtype = block.shape, block.dtype

    def body(x_ref, out_ref, send_sems, recv_sems, local_sem):
        x, y, c = lax.axis_index("x"), lax.axis_index("y"), lax.axis_index("c")
        me, sibling = (x, y, c), (x, y, 1 - c)
        chips = [(1 - x, y), (x, 1 - y), (1 - x, 1 - y)]

        def rows(px, py, pc):
            return out_ref.at[4 * px + 2 * py + pc]

        def copy(k, blk, to, src=None):
            return pltpu.make_async_remote_copy(
                src_ref=rows(*blk) if src is None else src, dst_ref=rows(*blk),
                send_sem=send_sems.at[k], recv_sem=recv_sems.at[k],
                device_id=to, device_id_type=pl.DeviceIdType.MESH)

        mine = pltpu.make_async_copy(x_ref, rows(*me), local_sem)
        mine.start()
        first = [copy(0, me, sibling, src=x_ref)]
        first += [copy(1 + j, me, (*chip, c), src=x_ref) for j, chip in enumerate(chips)]
        for cp in first:
            cp.start()
        passed = [copy(4 + j, (*chip, c), sibling) for j, chip in enumerate(chips)]
        for j, chip in enumerate(chips):
            copy(1 + j, (*chip, c), me).wait_recv()
            passed[j].start()
        copy(0, sibling, me).wait_recv()
        for j, chip in enumerate(chips):
            copy(4 + j, (*chip, 1 - c), me).wait_recv()
        for cp in first + passed:
            cp.wait_send()
        mine.wait()

    return pl.pallas_call(
        body, name=name,
        out_shape=jax.ShapeDtypeStruct((N_DEV,) + shape, dtype),
        in_specs=[pl.BlockSpec(memory_space=pltpu.HBM)],
        out_specs=pl.BlockSpec(memory_space=pltpu.HBM),
        scratch_shapes=[pltpu.SemaphoreType.DMA((7,)), pltpu.SemaphoreType.DMA((7,)),
                        pltpu.SemaphoreType.DMA(())],
    )(block)


_HBM = pl.BlockSpec(memory_space=pltpu.HBM)
_SEM = pl.BlockSpec(memory_space=pltpu.SEMAPHORE)
_ANY = pl.BlockSpec(memory_space=pl.ANY)
_EFFECT = pltpu.SideEffectType.DATAFLOW_SIDE_EFFECTING
_FLIPS = [(0, 0, 1), (0, 1, 0), (0, 1, 1), (1, 0, 0), (1, 0, 1), (1, 1, 0), (1, 1, 1)]


def _peers():
    x, y, c = lax.axis_index("x"), lax.axis_index("y"), lax.axis_index("c")
    return 4 * x + 2 * y + c, [(1 - x if fx else x, 1 - y if fy else y, 1 - c if fc else c)
                               for fx, fy, fc in _FLIPS]


def _exchange_start(src, gather, dep, name):
    block = src.shape if gather else src.shape[1:]
    land = (N_DEV,) + tuple(block)

    def body(src_ref, land_ref, dep_ref, send_sems, recv_sems, src_thru, land_thru, token):
        me, peers = _peers()
        for k, (px, py, pc) in enumerate(peers):
            pltpu.make_async_remote_copy(
                src_ref=src_ref if gather else src_ref.at[4 * px + 2 * py + pc], dst_ref=land_ref.at[me],
                send_sem=send_sems.at[k], recv_sem=recv_sems.at[k],
                device_id=(px, py, pc), device_id_type=pl.DeviceIdType.MESH).start()
        token[...] = jnp.zeros_like(token)

    return pl.pallas_call(
        body, name=name,
        out_shape=(pltpu.SemaphoreType.DMA((7,)), pltpu.SemaphoreType.DMA((7,)),
                   pltpu.HBM(src.shape, src.dtype), pltpu.HBM(land, src.dtype),
                   jax.ShapeDtypeStruct((8, 128), F32)),
        in_specs=(_HBM, _HBM, _ANY),
        out_specs=(_SEM, _SEM, _HBM, _HBM, pl.BlockSpec(memory_space=pltpu.VMEM)),
        input_output_aliases={0: 2, 1: 3},
        compiler_params=pltpu.CompilerParams(has_side_effects=_EFFECT),
    )(pltpu.with_memory_space_constraint(src, pltpu.HBM),
      pltpu.with_memory_space_constraint(lax.empty(land, src.dtype), pltpu.HBM), dep)


def _exchange_wait(started, after, gather, name):
    send_sems, recv_sems, src_thru, land_thru, _ = started

    def body(src_ref, land_ref, send_sems, recv_sems, after_ref, src_dead, got_ref):
        _, peers = _peers()
        for k, peer in enumerate(peers):
            copy = pltpu.make_async_remote_copy(
                src_ref=src_ref if gather else src_ref.at[0], dst_ref=land_ref.at[0],
                send_sem=send_sems.at[k], recv_sem=recv_sems.at[k],
                device_id=peer, device_id_type=pl.DeviceIdType.MESH)
            copy.wait_send()
            copy.wait_recv()

    return pl.pallas_call(
        body, name=name,
        out_shape=(pltpu.HBM(src_thru.shape, src_thru.dtype), pltpu.HBM(land_thru.shape, land_thru.dtype)),
        in_specs=(_HBM, _HBM, _SEM, _SEM, _ANY), out_specs=(_HBM, _HBM),
        input_output_aliases={0: 0, 1: 1},
        compiler_params=pltpu.CompilerParams(has_side_effects=_EFFECT),
    )(src_thru, land_thru, send_sems, recv_sems, after)[1]


def _matmul(pairs, dims, out_dtype, name, tm=512, tn=512, tk=512, dep=None):
    a0, b0 = pairs[0]
    if dims == TN:
        K, M = a0.shape
    else:
        M, K = a0.shape
    N = b0.shape[0] if dims == NT else b0.shape[1]
    tm, tn, tk = _tile(M, tm), _tile(N, tn), _tile(K, tk)
    nk = K // tk
    n_pairs = len(pairs)
    n_in = 2 * n_pairs + (dep is not None)

    def body(*refs):
        out_ref, acc_ref = refs[n_in], refs[n_in + 1]
        k = pl.program_id(2)

        @pl.when(k == 0)
        def _():
            acc_ref[...] = jnp.zeros_like(acc_ref)

        acc = acc_ref[...]
        for p in range(n_pairs):
            acc += _dot(refs[2 * p][...], refs[2 * p + 1][...], dims)
        acc_ref[...] = acc

        @pl.when(k == nk - 1)
        def _():
            out_ref[...] = acc_ref[...].astype(out_ref.dtype)

    if dims == TN:
        a_spec = pl.BlockSpec((tk, tm), lambda i, j, k: (k, i))
    else:
        a_spec = pl.BlockSpec((tm, tk), lambda i, j, k: (i, k))
    if dims == NT:
        b_spec = pl.BlockSpec((tn, tk), lambda i, j, k: (j, k))
    else:
        b_spec = pl.BlockSpec((tk, tn), lambda i, j, k: (k, j))
    args, specs = [], []
    for a, b in pairs:
        args += [a, b]
        specs += [a_spec, b_spec]
    if dep is not None:
        args.append(dep)
        specs.append(_ANY)
    return pl.pallas_call(
        body, name=name, grid=(M // tm, N // tn, nk),
        in_specs=specs, out_specs=pl.BlockSpec((tm, tn), lambda i, j, k: (i, j)),
        out_shape=jax.ShapeDtypeStruct((M, N), out_dtype),
        scratch_shapes=[pltpu.VMEM((tm, tn), F32)],
        compiler_params=_params("parallel", "parallel", "arbitrary"),
    )(*args)


def _vec_spec(d):
    return pl.BlockSpec((1, d), lambda i: (0, 0))


def _norm_mod(x, nw, scale, shift, name):
    T, D = x.shape
    tm = _tile(T, 512)

    def body(x_ref, nw_ref, sc_ref, sh_ref, h_ref):
        xf = x_ref[...]
        r = lax.rsqrt(jnp.mean(xf * xf, axis=-1, keepdims=True) + EPS)
        h_ref[...] = ((xf * r) * nw_ref[...] * (1.0 + sc_ref[...]) + sh_ref[...]).astype(h_ref.dtype)

    row = pl.BlockSpec((tm, D), lambda i: (i, 0))
    return pl.pallas_call(
        body, name=name, grid=(T // tm,),
        in_specs=[row, _vec_spec(D), _vec_spec(D), _vec_spec(D)], out_specs=row,
        out_shape=jax.ShapeDtypeStruct((T, D), MXU_DTYPE),
        compiler_params=_params("parallel"),
    )(x, nw, scale, shift)


def _resid_norm_mod(x, y, gate, coef, nw, scale, shift, name):
    T, D = x.shape
    tm = _tile(T, 512)

    def body(x_ref, y_ref, g_ref, nw_ref, sc_ref, sh_ref, xo_ref, h_ref):
        xf = x_ref[...] + (coef * g_ref[...]) * y_ref[...]
        xo_ref[...] = xf
        r = lax.rsqrt(jnp.mean(xf * xf, axis=-1, keepdims=True) + EPS)
        h_ref[...] = ((xf * r) * nw_ref[...] * (1.0 + sc_ref[...]) + sh_ref[...]).astype(h_ref.dtype)

    row = pl.BlockSpec((tm, D), lambda i: (i, 0))
    return pl.pallas_call(
        body, name=name, grid=(T // tm,),
        in_specs=[row, row, _vec_spec(D), _vec_spec(D), _vec_spec(D), _vec_spec(D)],
        out_specs=[row, row],
        out_shape=[jax.ShapeDtypeStruct((T, D), F32), jax.ShapeDtypeStruct((T, D), MXU_DTYPE)],
        compiler_params=_params("parallel"),
    )(x, y, gate, nw, scale, shift)


def _gate_bwd(dxo, y, gate, coef, name):
    T, D = dxo.shape
    tm = _tile(T, 512)

    def body(d_ref, y_ref, g_ref, dy_ref, dg_ref):
        @pl.when(pl.program_id(0) == 0)
        def _():
            dg_ref[...] = jnp.zeros_like(dg_ref)

        d = d_ref[...]
        dy_ref[...] = ((coef * g_ref[...]) * d).astype(dy_ref.dtype)
        dg_ref[...] += coef * jnp.sum(d * y_ref[...], axis=0, keepdims=True)

    row = pl.BlockSpec((tm, D), lambda i: (i, 0))
    return pl.pallas_call(
        body, name=name, grid=(T // tm,),
        in_specs=[row, row, _vec_spec(D)], out_specs=[row, _vec_spec(D)],
        out_shape=[jax.ShapeDtypeStruct((T, D), MXU_DTYPE), jax.ShapeDtypeStruct((1, D), F32)],
        compiler_params=_params("arbitrary"),
    )(dxo, y, gate)


def _norm_bwd(dh, x, nw, scale, dres, name):
    T, D = x.shape
    tm = _tile(T, 512)

    def body(dh_ref, x_ref, nw_ref, sc_ref, dr_ref, dx_ref, dnw_ref, dsc_ref, dsh_ref):
        @pl.when(pl.program_id(0) == 0)
        def _():
            dnw_ref[...] = jnp.zeros_like(dnw_ref)
            dsc_ref[...] = jnp.zeros_like(dsc_ref)
            dsh_ref[...] = jnp.zeros_like(dsh_ref)

        xf, dh_ = x_ref[...], dh_ref[...]
        r = lax.rsqrt(jnp.mean(xf * xf, axis=-1, keepdims=True) + EPS)
        xn = xf * r
        one_sc = 1.0 + sc_ref[...]
        dsh_ref[...] += jnp.sum(dh_, axis=0, keepdims=True)
        t = dh_ * xn
        dsc_ref[...] += jnp.sum(t, axis=0, keepdims=True) * nw_ref[...]
        dnw_ref[...] += jnp.sum(t, axis=0, keepdims=True) * one_sc
        dxn = dh_ * (nw_ref[...] * one_sc)
        dx_ref[...] = dr_ref[...] + r * (dxn - xn * jnp.mean(dxn * xn, axis=-1, keepdims=True))

    row = pl.BlockSpec((tm, D), lambda i: (i, 0))
    vec = _vec_spec(D)
    return pl.pallas_call(
        body, name=name, grid=(T // tm,),
        in_specs=[row, row, vec, vec, row], out_specs=[row, vec, vec, vec],
        out_shape=[jax.ShapeDtypeStruct((T, D), F32)] + [jax.ShapeDtypeStruct((1, D), F32)] * 3,
        compiler_params=_params("arbitrary"),
    )(dh, x, nw, scale, dres)


def _final_loss(x, y, gate, fw, target, name):
    T, D = x.shape
    tm = _tile(T, 512)

    def body(x_ref, y_ref, g_ref, fw_ref, t_ref, loss_ref, dx_ref, dfw_ref):
        @pl.when(pl.program_id(0) == 0)
        def _():
            loss_ref[...] = jnp.zeros_like(loss_ref)
            dfw_ref[...] = jnp.zeros_like(dfw_ref)

        xf = x_ref[...] + (0.5 * g_ref[...]) * y_ref[...]
        r = lax.rsqrt(jnp.mean(xf * xf, axis=-1, keepdims=True) + EPS)
        xn = xf * r
        err = xn * fw_ref[...] - t_ref[...]
        per_tok = jnp.mean(err * err, axis=-1, keepdims=True)
        loss_ref[...] += 0.5 * jnp.sum(per_tok, axis=0, keepdims=True)
        dy = err * (1.0 / D)
        dfw_ref[...] += jnp.sum(dy * xn, axis=0, keepdims=True)
        dxn = dy * fw_ref[...]
        dx_ref[...] = r * (dxn - xn * jnp.mean(dxn * xn, axis=-1, keepdims=True))

    row = pl.BlockSpec((tm, D), lambda i: (i, 0))
    vec = _vec_spec(D)
    return pl.pallas_call(
        body, name=name, grid=(T // tm,),
        in_specs=[row, row, vec, vec, row],
        out_specs=[pl.BlockSpec((1, 128), lambda i: (0, 0)), row, vec],
        out_shape=[jax.ShapeDtypeStruct((1, 128), F32), jax.ShapeDtypeStruct((T, D), F32),
                   jax.ShapeDtypeStruct((1, D), F32)],
        compiler_params=_params("arbitrary"),
    )(x, y, gate, fw, target)


def _ffn_fwd(h, wg_t, wu_t, wd, name):
    T, D = h.shape
    Fdim = wd.shape[0]
    tm, tf = _tile(T, 1024), _tile(Fdim, 256)
    nf = Fdim // tf

    def body(h_ref, wg_ref, wu_ref, wd_ref, y_ref, g_ref, u_ref, a_ref, acc_ref):
        k = pl.program_id(1)

        @pl.when(k == 0)
        def _():
            acc_ref[...] = jnp.zeros_like(acc_ref)

        hh = h_ref[...]
        g = _dot(hh, wg_ref[...], NT)
        u = _dot(hh, wu_ref[...], NT)
        a = ((g * _sigmoid(g)) * u).astype(a_ref.dtype)
        g_ref[...] = g
        u_ref[...] = u
        a_ref[...] = a
        acc_ref[...] += _dot(a, wd_ref[...], NN)

        @pl.when(k == nf - 1)
        def _():
            y_ref[...] = acc_ref[...]

    hrow = pl.BlockSpec((tm, D), lambda i, k: (i, 0))
    wspec = pl.BlockSpec((tf, D), lambda i, k: (k, 0))
    fspec = pl.BlockSpec((tm, tf), lambda i, k: (i, k))
    return pl.pallas_call(
        body, name=name, grid=(T // tm, nf),
        in_specs=[hrow, wspec, wspec, wspec], out_specs=[hrow, fspec, fspec, fspec],
        out_shape=[jax.ShapeDtypeStruct((T, D), F32), jax.ShapeDtypeStruct((T, Fdim), F32),
                   jax.ShapeDtypeStruct((T, Fdim), F32), jax.ShapeDtypeStruct((T, Fdim), MXU_DTYPE)],
        scratch_shapes=[pltpu.VMEM((tm, D), F32)],
        compiler_params=_params("parallel", "arbitrary"),
    )(h, wg_t, wu_t, wd)


def _ffn_bwd_act(dy, wd, g, u, name):
    T, D = dy.shape
    Fdim = wd.shape[0]
    tm, tf = _tile(T, 1024), _tile(Fdim, 256)

    def body(dy_ref, wd_ref, g_ref, u_ref, dg_ref, du_ref):
        da = _dot(dy_ref[...], wd_ref[...], NT)
        gg = g_ref[...]
        sig = _sigmoid(gg)
        dg_ref[...] = (da * u_ref[...] * (sig * (1.0 + gg * (1.0 - sig)))).astype(dg_ref.dtype)
        du_ref[...] = (da * (gg * sig)).astype(du_ref.dtype)

    fspec = pl.BlockSpec((tm, tf), lambda i, k: (i, k))
    return pl.pallas_call(
        body, name=name, grid=(T // tm, Fdim // tf),
        in_specs=[pl.BlockSpec((tm, D), lambda i, k: (i, 0)), pl.BlockSpec((tf, D), lambda i, k: (k, 0)),
                  fspec, fspec],
        out_specs=[fspec, fspec],
        out_shape=[jax.ShapeDtypeStruct((T, Fdim), MXU_DTYPE)] * 2,
        compiler_params=_params("parallel", "parallel"),
    )(dy, wd, g, u)


def _conv_act(window, w):
    y = window * w[CONV_K - 1:CONV_K, :]
    for j in range(CONV_K - 1):
        y += _shift_rows(window, CONV_K - 1 - j) * w[j:j + 1, :]
    return y


def _gdn_prep(proj, conv_w, a_log_l, dt_bias_l, name):
    T = proj.shape[0]
    tm = _tile(T, 256)
    hb = tm // 8

    def body(cur_ref, halo_ref, ba_ref, w_ref, al_ref, dtb_ref, qkv_ref, bg_ref):
        i = pl.program_id(0)
        halo = jnp.where(i == 0, 0.0, halo_ref[...])
        window = jnp.concatenate([halo, cur_ref[...]], axis=0)
        y = _conv_act(window, w_ref[...])[8:, :]
        act = y * _sigmoid(y)
        for hh in range(3 * GDN_HEADS):
            blk = act[:, hh * HEAD_DIM:(hh + 1) * HEAD_DIM]
            if hh < 2 * GDN_HEADS:
                rn = lax.rsqrt(jnp.sum(blk * blk, axis=-1, keepdims=True) + EPS)
                blk = blk * rn
                if hh < GDN_HEADS:
                    blk = blk * (HEAD_DIM ** -0.5)
            qkv_ref[:, hh * HEAD_DIM:(hh + 1) * HEAD_DIM] = blk
        ba = ba_ref[...]
        lane = lax.broadcasted_iota(jnp.int32, ba.shape, 1)
        beta = _sigmoid(ba)
        g = -jnp.exp(al_ref[...]) * _softplus(ba + dtb_ref[...])
        bg_ref[...] = jnp.where(lane < GDN_HEADS, beta, jnp.where(lane < 2 * GDN_HEADS, g, 0.0))

    return pl.pallas_call(
        body, name=name, grid=(T // tm,),
        in_specs=[pl.BlockSpec((tm, QKV_WIDTH), lambda i: (i, 0)),
                  pl.BlockSpec((8, QKV_WIDTH), lambda i: (jnp.maximum(i * hb - 1, 0), 0)),
                  pl.BlockSpec((tm, 128), lambda i: (i, COL_BA // 128)),
                  pl.BlockSpec((CONV_K, QKV_WIDTH), lambda i: (0, 0)),
                  pl.BlockSpec((1, 128), lambda i: (0, 0)), pl.BlockSpec((1, 128), lambda i: (0, 0))],
        out_specs=[pl.BlockSpec((tm, QKV_WIDTH), lambda i: (i, 0)), pl.BlockSpec((tm, 128), lambda i: (i, 0))],
        out_shape=[jax.ShapeDtypeStruct((T, QKV_WIDTH), F32), jax.ShapeDtypeStruct((T, 128), F32)],
        compiler_params=_params("parallel"),
    )(proj, proj, proj, conv_w, a_log_l, dt_bias_l)


def _chunk_cumsum(v, reverse=False):
    row = lax.broadcasted_iota(jnp.int32, v.shape, 0)
    s = 1
    while s < CHUNK:
        if reverse:
            v = v + jnp.where(row < CHUNK - s, _shift_rows(v, -s), 0.0)
        else:
            v = v + jnp.where(row >= s, _shift_rows(v, s), 0.0)
        s *= 2
    return v


def _row_form(cols):
    padded = jnp.concatenate([cols, jnp.zeros((128 - CHUNK, 128), F32)], axis=0)
    return padded.T[:, :CHUNK]


def _chunk_masks():
    ri = lax.broadcasted_iota(jnp.int32, (CHUNK, CHUNK), 0)
    ci = lax.broadcasted_iota(jnp.int32, (CHUNK, CHUNK), 1)
    return ri >= ci, ri > ci, (ri == ci).astype(F32)


def _unit_lower_inverses(ms, eye):
    rs = [eye - m for m in ms]
    ps = [_split(-m) for m in ms]
    s = 2
    while s < CHUNK:
        ps = [_split(_dot3(p, p, NN)) for p in ps]
        r_parts = [_split(r) for r in rs]
        rs = [r + _dot3(p, rp, NN) for r, p, rp in zip(rs, ps, r_parts)]
        s *= 2
    return rs


def _head_elementwise(k, beta, gc, gcr, causal):
    decay = jnp.where(causal, jnp.exp(jnp.where(causal, gc - gcr, 0.0)), 0.0)
    return decay, k * beta, jnp.exp(gc)


def _head_slices(hh):
    return (slice(hh * HEAD_DIM, (hh + 1) * HEAD_DIM),
            slice(GDN_WIDTH + hh * HEAD_DIM, GDN_WIDTH + (hh + 1) * HEAD_DIM),
            slice(2 * GDN_WIDTH + hh * HEAD_DIM, 2 * GDN_WIDTH + (hh + 1) * HEAD_DIM))


def _gdn_chunk_fwd(qkv, bg, name):
    T = qkv.shape[0]
    cb = _tile(T // CHUNK, CHUNKS_PER_STEP)
    rows = cb * CHUNK

    def body(qkv_ref, bg_ref, tinv_ref, u_ref, w_ref, qd_ref, kd_ref, p_ref, cd_ref):
        masks = _chunk_masks()
        causal, strict, eye = masks
        heads = []
        for ci in range(cb):
            rs = slice(ci * CHUNK, (ci + 1) * CHUNK)
            bgv = bg_ref[rs, :]
            gc_all = _chunk_cumsum(bgv)
            gc_rows = _row_form(gc_all)
            cd_ref[rs, :] = jnp.exp(jnp.broadcast_to(gc_all[CHUNK - 1:CHUNK, :], (CHUNK, 128)))
            for hh in range(GDN_HEADS):
                qs, ks, vs = _head_slices(hh)
                q, k, v = qkv_ref[rs, qs], qkv_ref[rs, ks], qkv_ref[rs, vs]
                beta = bgv[:, hh:hh + 1]
                gc = gc_all[:, GDN_HEADS + hh:GDN_HEADS + hh + 1]
                decay, kb, eg = _head_elementwise(k, beta, gc, gc_rows[GDN_HEADS + hh:GDN_HEADS + hh + 1, :], causal)
                hs = slice(hh * HEAD_DIM, (hh + 1) * HEAD_DIM)
                cs = slice(hh * CHUNK, (hh + 1) * CHUNK)
                qd_ref[rs, hs] = (q * eg).astype(qd_ref.dtype)
                kd_ref[rs, hs] = (k * jnp.exp(gc[CHUNK - 1:CHUNK, :] - gc)).astype(kd_ref.dtype)
                heads.append((rs, hs, cs, q, k, v * beta, kb, kb * eg, decay))
        kks = [_mdot(kb, k, NT) for (_, _, _, _, k, _, kb, _, _) in heads]
        qks = [_mdot(q, k, NT) for (_, _, _, q, k, _, _, _, _) in heads]
        tinvs = _unit_lower_inverses([jnp.where(strict, kk * hd[8], 0.0) for kk, hd in zip(kks, heads)], eye)
        t_parts = [_split(t) for t in tinvs]
        us = [_dot3(tp, _split(hd[5]), NN) for tp, hd in zip(t_parts, heads)]
        ws = [_dot3(tp, _split(hd[7]), NN) for tp, hd in zip(t_parts, heads)]
        for hd, tinv, u, w, qk in zip(heads, tinvs, us, ws, qks):
            rs, hs, cs = hd[0], hd[1], hd[2]
            tinv_ref[rs, cs] = tinv
            u_ref[rs, hs] = u
            w_ref[rs, hs] = w.astype(w_ref.dtype)
            p_ref[rs, cs] = jnp.where(causal, qk * hd[8], 0.0).astype(p_ref.dtype)

    def spec(width):
        return pl.BlockSpec((rows, width), lambda n: (n, 0))

    hw, cw = GDN_WIDTH, GDN_HEADS * CHUNK
    return pl.pallas_call(
        body, name=name, grid=(T // rows,),
        in_specs=[spec(QKV_WIDTH), spec(128)],
        out_specs=[spec(cw), spec(hw), spec(hw), spec(hw), spec(hw), spec(cw), spec(128)],
        out_shape=[jax.ShapeDtypeStruct((T, cw), F32), jax.ShapeDtypeStruct((T, hw), F32),
                   jax.ShapeDtypeStruct((T, hw), MXU_DTYPE), jax.ShapeDtypeStruct((T, hw), MXU_DTYPE),
                   jax.ShapeDtypeStruct((T, hw), MXU_DTYPE), jax.ShapeDtypeStruct((T, cw), MXU_DTYPE),
                   jax.ShapeDtypeStruct((T, 128), F32)],
        compiler_params=_params("parallel"),
    )(qkv, bg)


def _gdn_scan_fwd(u, w, qd, kd, p, cd, name):
    T = u.shape[0]
    cb = _tile(T // CHUNK, SCAN_CHUNKS_PER_STEP)
    rows = cb * CHUNK

    def body(u_ref, w_ref, qd_ref, kd_ref, p_ref, cd_ref, o_ref, s_all_ref, vn_ref, s_ref):
        @pl.when(pl.program_id(0) == 0)
        def _():
            s_ref[...] = jnp.zeros_like(s_ref)

        hss = [slice(hh * HEAD_DIM, (hh + 1) * HEAD_DIM) for hh in range(GDN_HEADS)]
        css = [slice(hh * CHUNK, (hh + 1) * CHUNK) for hh in range(GDN_HEADS)]
        s_cur = [s_ref[hh] for hh in range(GDN_HEADS)]
        for ci in range(cb):
            rs = slice(ci * CHUNK, (ci + 1) * CHUNK)
            for hh in range(GDN_HEADS):
                s_all_ref[ci * GDN_WIDTH + hh * HEAD_DIM:ci * GDN_WIDTH + (hh + 1) * HEAD_DIM, :] = s_cur[hh]
            s_ms = [s.astype(MXU_DTYPE) for s in s_cur]
            w_s = [_dot(w_ref[rs, hs], s_m, NN) for hs, s_m in zip(hss, s_ms)]
            q_s = [_dot(qd_ref[rs, hs], s_m, NN) for hs, s_m in zip(hss, s_ms)]
            v_ms = [(u_ref[rs, hs] - ws_).astype(MXU_DTYPE) for hs, ws_ in zip(hss, w_s)]
            k_v = [_dot(kd_ref[rs, hs], v_m, TN) for hs, v_m in zip(hss, v_ms)]
            p_v = [_dot(p_ref[rs, cs], v_m, NN) for cs, v_m in zip(css, v_ms)]
            for hh in range(GDN_HEADS):
                vn_ref[rs, hss[hh]] = v_ms[hh]
                o_ref[rs, hss[hh]] = q_s[hh] + p_v[hh]
                c_dec = cd_ref[ci * CHUNK:ci * CHUNK + 1, GDN_HEADS + hh:GDN_HEADS + hh + 1]
                s_cur[hh] = s_cur[hh] * c_dec + k_v[hh]
        for hh in range(GDN_HEADS):
            s_ref[hh] = s_cur[hh]

    def spec(width):
        return pl.BlockSpec((rows, width), lambda n: (n, 0))

    hw, cw = GDN_WIDTH, GDN_HEADS * CHUNK
    return pl.pallas_call(
        body, name=name, grid=(T // rows,),
        in_specs=[spec(hw), spec(hw), spec(hw), spec(hw), spec(cw), spec(128)],
        out_specs=[spec(hw), pl.BlockSpec((cb * GDN_WIDTH, HEAD_DIM), lambda n: (n, 0)), spec(hw)],
        out_shape=[jax.ShapeDtypeStruct((T, hw), F32),
                   jax.ShapeDtypeStruct((T // CHUNK * GDN_WIDTH, HEAD_DIM), F32),
                   jax.ShapeDtypeStruct((T, hw), MXU_DTYPE)],
        scratch_shapes=[pltpu.VMEM((GDN_HEADS, HEAD_DIM, HEAD_DIM), F32)],
        compiler_params=_params("arbitrary"),
    )(u, w, qd, kd, p, cd)


def _gdn_scan_bwd(do, w, qd, kd, p, cd, s_all, vn, name):
    T = do.shape[0]
    cb = _tile(T // CHUNK, SCAN_CHUNKS_PER_STEP)
    rows = cb * CHUNK
    n_steps = T // rows

    def body(do_ref, w_ref, qd_ref, kd_ref, p_ref, cd_ref, s_all_ref, vn_ref,
             dvn_ref, dw_ref, dqd_ref, dkd_ref, dp_ref, dcd_ref, ds_ref):
        @pl.when(pl.program_id(0) == 0)
        def _():
            ds_ref[...] = jnp.zeros_like(ds_ref)

        causal, _, _ = _chunk_masks()
        lane = lax.broadcasted_iota(jnp.int32, (CHUNK, 128), 1)
        heads = range(GDN_HEADS)
        hss = [slice(hh * HEAD_DIM, (hh + 1) * HEAD_DIM) for hh in heads]
        css = [slice(hh * CHUNK, (hh + 1) * CHUNK) for hh in heads]
        ds_cur = [ds_ref[hh] for hh in heads]
        for ci in reversed(range(cb)):
            rs = slice(ci * CHUNK, (ci + 1) * CHUNK)
            ds_ms = [d.astype(MXU_DTYPE) for d in ds_cur]
            s_olds = [s_all_ref[ci * GDN_WIDTH + hh * HEAD_DIM:ci * GDN_WIDTH + (hh + 1) * HEAD_DIM, :] for hh in heads]
            s_ms = [s.astype(MXU_DTYPE) for s in s_olds]
            do_ms = [do_ref[rs, hs].astype(MXU_DTYPE) for hs in hss]
            p_do = [_dot(p_ref[rs, cs], do_m, TN) for cs, do_m in zip(css, do_ms)]
            k_ds = [_dot(kd_ref[rs, hs], ds_m, NN) for hs, ds_m in zip(hss, ds_ms)]
            q_do = [_dot(qd_ref[rs, hs], do_m, TN) for hs, do_m in zip(hss, do_ms)]
            dqds = [_dot(do_m, s_m, NT) for do_m, s_m in zip(do_ms, s_ms)]
            dkds = [_dot(vn_ref[rs, hs], ds_m, NT) for hs, ds_m in zip(hss, ds_ms)]
            dps = [_dot(do_m, vn_ref[rs, hs], NT) for hs, do_m in zip(hss, do_ms)]
            dv_news = [a + b for a, b in zip(p_do, k_ds)]
            dvn_ms = [d.astype(MXU_DTYPE) for d in dv_news]
            w_dv = [_dot(w_ref[rs, hs], dvn_m, TN) for hs, dvn_m in zip(hss, dvn_ms)]
            dws = [_dot(dvn_m, s_m, NT) for dvn_m, s_m in zip(dvn_ms, s_ms)]
            dcd_tile = jnp.zeros((CHUNK, 128), F32)
            for hh in heads:
                dvn_ref[rs, hss[hh]] = dv_news[hh]
                dw_ref[rs, hss[hh]] = -dws[hh]
                dqd_ref[rs, hss[hh]] = dqds[hh]
                dkd_ref[rs, hss[hh]] = dkds[hh]
                dp_ref[rs, css[hh]] = jnp.where(causal, dps[hh], 0.0)
                dcd = jnp.sum(jnp.sum(s_olds[hh] * ds_cur[hh], axis=1, keepdims=True), axis=0, keepdims=True)
                dcd_tile = jnp.where(lane == GDN_HEADS + hh, dcd, dcd_tile)
                c_dec = cd_ref[ci * CHUNK:ci * CHUNK + 1, GDN_HEADS + hh:GDN_HEADS + hh + 1]
                ds_cur[hh] = c_dec * ds_cur[hh] + q_do[hh] - w_dv[hh]
            dcd_ref[rs, :] = dcd_tile
        for hh in heads:
            ds_ref[hh] = ds_cur[hh]

    def spec(width):
        return pl.BlockSpec((rows, width), lambda n: (n_steps - 1 - n, 0))

    hw, cw = GDN_WIDTH, GDN_HEADS * CHUNK
    return pl.pallas_call(
        body, name=name, grid=(n_steps,),
        in_specs=[spec(hw), spec(hw), spec(hw), spec(hw), spec(cw), spec(128),
                  pl.BlockSpec((cb * GDN_WIDTH, HEAD_DIM), lambda n: (n_steps - 1 - n, 0)), spec(hw)],
        out_specs=[spec(hw), spec(hw), spec(hw), spec(hw), spec(cw), spec(128)],
        out_shape=[jax.ShapeDtypeStruct((T, hw), F32)] * 4
        + [jax.ShapeDtypeStruct((T, cw), F32), jax.ShapeDtypeStruct((T, 128), F32)],
        scratch_shapes=[pltpu.VMEM((GDN_HEADS, HEAD_DIM, HEAD_DIM), F32)],
        compiler_params=_params("arbitrary"),
    )(do, w, qd, kd, p, cd, s_all, vn)


def _gdn_chunk_bwd(qkv, bg, tinv_all, u, w, dvn, dw, dqd, dkd, dp, dcd, name):
    T = qkv.shape[0]
    cb = _tile(T // CHUNK, CHUNKS_PER_STEP)
    rows = cb * CHUNK

    def body(qkv_ref, bg_ref, tinv_ref, u_ref, w_ref, dvn_ref, dw_ref, dqd_ref, dkd_ref, dp_ref, dcd_ref,
             dqkv_ref, dbg_ref):
        masks = _chunk_masks()
        causal, strict, _ = masks
        lane = lax.broadcasted_iota(jnp.int32, (CHUNK, 128), 1)
        row = lax.broadcasted_iota(jnp.int32, (CHUNK, 128), 0)
        heads = []
        for ci in range(cb):
            rs = slice(ci * CHUNK, (ci + 1) * CHUNK)
            bgv = bg_ref[rs, :]
            gc_all = _chunk_cumsum(bgv)
            gc_rows = _row_form(gc_all)
            for hh in range(GDN_HEADS):
                qs, ks, vs = _head_slices(hh)
                q, k = qkv_ref[rs, qs], qkv_ref[rs, ks]
                beta = bgv[:, hh:hh + 1]
                gc = gc_all[:, GDN_HEADS + hh:GDN_HEADS + hh + 1]
                decay, kb, eg = _head_elementwise(k, beta, gc, gc_rows[GDN_HEADS + hh:GDN_HEADS + hh + 1, :], causal)
                heads.append(dict(ci=ci, hh=hh, rs=rs, hs=slice(hh * HEAD_DIM, (hh + 1) * HEAD_DIM),
                                  cs=slice(hh * CHUNK, (hh + 1) * CHUNK), q=q, k=k, beta=beta, gc=gc,
                                  decay=decay, kb=kb, eg=eg))
        for hd in heads:
            hd["t"] = _split(tinv_ref[hd["rs"], hd["cs"]])
        for hd in heads:
            hd["kk"] = _mdot(hd["kb"], hd["k"], NT)
            hd["qk"] = _mdot(hd["q"], hd["k"], NT)
        for hd in heads:
            hd["dvb"] = _dot3(hd["t"], _split(dvn_ref[hd["rs"], hd["hs"]]), TN)
            hd["dkbeg"] = _dot3(hd["t"], _split(dw_ref[hd["rs"], hd["hs"]]), TN)
        for hd in heads:
            rs, hs = hd["rs"], hd["hs"]
            da = -(_mdot(hd["dvb"], u_ref[rs, hs], NT) + _mdot(hd["dkbeg"], w_ref[rs, hs], NT))
            dm = jnp.where(strict, da, 0.0)
            dp_ = dp_ref[rs, hd["cs"]]
            hd["dkk"] = dm * hd["decay"]
            hd["dqk"] = dp_ * hd["decay"]
            hd["e"] = (hd["dkk"] * hd["kk"] + hd["dqk"] * hd["qk"])
        for hd in heads:
            hd["dkb"] = _mdot(hd["dkk"], hd["k"], NN)
            hd["dk"] = _mdot(hd["dkk"], hd["kb"], TN) + _mdot(hd["dqk"], hd["q"], TN)
            hd["dq"] = _mdot(hd["dqk"], hd["k"], NN)
            onehot = (lane == GDN_HEADS + hd["hh"]).astype(jnp.bfloat16)
            e_hi, e_lo = _split(hd["e"])
            hd["col_sums"] = _dot(e_lo, onehot, TN) + _dot(e_hi, onehot, TN)
        tiles = {}
        for hd in heads:
            ci, hh, rs, hs = hd["ci"], hd["hh"], hd["rs"], hd["hs"]
            qs, ks, vs = _head_slices(hh)
            q, k, beta, gc, eg, kb = hd["q"], hd["k"], hd["beta"], hd["gc"], hd["eg"], hd["kb"]
            v = qkv_ref[rs, vs]
            dqd_, dkd_ = dqd_ref[rs, hs], dkd_ref[rs, hs]
            gl = gc[CHUNK - 1:CHUNK, :]
            ek = jnp.exp(gl - gc)
            dkb = hd["dkb"] + hd["dkbeg"] * eg
            deg = jnp.sum(dqd_ * q, axis=1, keepdims=True) + jnp.sum(hd["dkbeg"] * kb, axis=1, keepdims=True)
            dek = jnp.sum(dkd_ * k, axis=1, keepdims=True)
            dcd_ = dcd_ref[ci * CHUNK:ci * CHUNK + 1, GDN_HEADS + hh:GDN_HEADS + hh + 1]
            dgl = jnp.sum(dek * ek, axis=0, keepdims=True) + dcd_ * jnp.exp(gl)
            dgc = jnp.sum(hd["e"], axis=1, keepdims=True) + deg * eg - dek * ek
            dbeta_tile, dgc_tile = tiles.get(ci, (jnp.zeros((CHUNK, 128), F32), jnp.zeros((CHUNK, 128), F32)))
            dgc_tile += jnp.where(lane == GDN_HEADS + hh, dgc, 0.0) - hd["col_sums"]
            dgc_tile += jnp.where((lane == GDN_HEADS + hh) & (row == CHUNK - 1), dgl, 0.0)
            dbeta = jnp.sum(dkb * k, axis=1, keepdims=True) + jnp.sum(hd["dvb"] * v, axis=1, keepdims=True)
            dbeta_tile += jnp.where(lane == hh, dbeta, 0.0)
            tiles[ci] = (dbeta_tile, dgc_tile)
            dqkv_ref[rs, qs] = hd["dq"] + dqd_ * eg
            dqkv_ref[rs, ks] = hd["dk"] + dkd_ * ek + dkb * beta
            dqkv_ref[rs, vs] = hd["dvb"] * beta
        for ci in range(cb):
            dbeta_tile, dgc_tile = tiles[ci]
            dbg_ref[ci * CHUNK:(ci + 1) * CHUNK, :] = dbeta_tile + _chunk_cumsum(dgc_tile, reverse=True)

    def spec(width):
        return pl.BlockSpec((rows, width), lambda n: (n, 0))

    hw, cw = GDN_WIDTH, GDN_HEADS * CHUNK
    return pl.pallas_call(
        body, name=name, grid=(T // rows,),
        in_specs=[spec(QKV_WIDTH), spec(128), spec(cw), spec(hw), spec(hw), spec(hw), spec(hw), spec(hw),
                  spec(hw), spec(cw), spec(128)],
        out_specs=[spec(QKV_WIDTH), spec(128)],
        out_shape=[jax.ShapeDtypeStruct((T, QKV_WIDTH), F32), jax.ShapeDtypeStruct((T, 128), F32)],
        compiler_params=_params("parallel"),
    )(qkv, bg, tinv_all, u, w, dvn, dw, dqd, dkd, dp, dcd)


def _pool_counts(i, tm, rows, offset):
    t = i * tm - offset + lax.broadcasted_iota(jnp.int32, (rows, 1), 0)
    return [jnp.minimum(t + 1, w).astype(F32) for w in POOL_WINDOWS]


def _window_sums(window, forward):
    sums, s, step = [], window, 1
    for _ in POOL_WINDOWS:
        s = s + _shift_rows(s, -step if forward else step)
        sums.append(s)
        step *= 2
    return sums


def _pooled(window, counts):
    sums = _window_sums(window, forward=False)
    out = []
    for gi in range(POOL_GROUPS):
        sl = slice(gi * 128, (gi + 1) * 128)
        out.append(sums[gi][HALO:, sl] / counts[gi] - window[HALO:, sl])
    return out


def _mix_post(o, proj, gdn_norm, pool_w, pool_scale, name):
    T = o.shape[0]
    tm = _tile(T, 256)
    hb = tm // HALO

    def body(o_ref, z_ref, p_ref, ph_ref, gn_ref, pw_ref, ps_ref, out_ref):
        i = pl.program_id(0)
        for hh in range(GDN_HEADS):
            sl = slice(hh * HEAD_DIM, (hh + 1) * HEAD_DIM)
            oh, zh = o_ref[:, sl], z_ref[:, sl]
            ro = lax.rsqrt(jnp.mean(oh * oh, axis=-1, keepdims=True) + EPS)
            out_ref[:, sl] = (((oh * ro) * gn_ref[...]) * (zh * _sigmoid(zh))).astype(out_ref.dtype)
        halo = jnp.where(i == 0, 0.0, ph_ref[...])
        window = jnp.concatenate([halo, p_ref[...]], axis=0)
        pooled = _pooled(window, _pool_counts(i, tm, tm, 0))
        for gi in range(POOL_GROUPS):
            pm = _mdot(pooled[gi], pw_ref[gi], NN)
            out_ref[:, GDN_WIDTH + gi * 128:GDN_WIDTH + (gi + 1) * 128] = (
                pm * ps_ref[:, gi * 128:(gi + 1) * 128]).astype(out_ref.dtype)

    return pl.pallas_call(
        body, name=name, grid=(T // tm,),
        in_specs=[pl.BlockSpec((tm, GDN_WIDTH), lambda i: (i, 0)),
                  pl.BlockSpec((tm, GDN_WIDTH), lambda i: (i, COL_Z // GDN_WIDTH)),
                  pl.BlockSpec((tm, POOL_WIDTH), lambda i: (i, COL_P // POOL_WIDTH)),
                  pl.BlockSpec((HALO, POOL_WIDTH), lambda i: (jnp.maximum(i * hb - 1, 0), COL_P // POOL_WIDTH)),
                  pl.BlockSpec((1, HEAD_DIM), lambda i: (0, 0)),
                  pl.BlockSpec((POOL_GROUPS, 128, 128), lambda i: (0, 0, 0)),
                  pl.BlockSpec((1, POOL_WIDTH), lambda i: (0, 0))],
        out_specs=pl.BlockSpec((tm, GDN_WIDTH + POOL_WIDTH), lambda i: (i, 0)),
        out_shape=jax.ShapeDtypeStruct((T, GDN_WIDTH + POOL_WIDTH), MXU_DTYPE),
        compiler_params=_params("parallel"),
    )(o, proj, proj, proj, gdn_norm, pool_w, pool_scale)


def _mix_post_bwd(dmix, o, proj, gdn_norm, pool_w, pool_scale, name):
    T = o.shape[0]
    tm = _tile(T, 256)
    hb = tm // HALO
    n_tiles = T // tm

    def body(dg_ref, dpo_ref, dpo_next_ref, o_ref, z_ref, p_ref, ph_ref, gn_ref, pw_ref, ps_ref,
             do_ref, dzp_ref, dgn_ref, dpw_ref, dps_ref):
        i = pl.program_id(0)

        @pl.when(i == 0)
        def _():
            dgn_ref[...] = jnp.zeros_like(dgn_ref)
            dpw_ref[...] = jnp.zeros_like(dpw_ref)
            dps_ref[...] = jnp.zeros_like(dps_ref)

        gn = gn_ref[...]
        dgn = jnp.zeros((1, HEAD_DIM), F32)
        for hh in range(GDN_HEADS):
            sl = slice(hh * HEAD_DIM, (hh + 1) * HEAD_DIM)
            oh, zh, dy = o_ref[:, sl], z_ref[:, sl], dg_ref[:, sl]
            ro = lax.rsqrt(jnp.mean(oh * oh, axis=-1, keepdims=True) + EPS)
            on = oh * ro
            sig = _sigmoid(zh)
            sz = zh * sig
            dzp_ref[:, sl] = (dy * (on * gn) * (sig * (1.0 + zh * (1.0 - sig)))).astype(dzp_ref.dtype)
            dgn += jnp.sum(dy * on * sz, axis=0, keepdims=True)
            don = dy * gn * sz
            do_ref[:, sl] = ro * (don - on * jnp.mean(don * on, axis=-1, keepdims=True))
        dgn_ref[...] += dgn

        halo = jnp.where(i == 0, 0.0, ph_ref[...])
        window = jnp.concatenate([halo, p_ref[...]], axis=0)
        counts = _pool_counts(i, tm, tm + HALO, 0)
        pooled = _pooled(window, [cn[:tm] for cn in counts])
        nxt = jnp.where(i == n_tiles - 1, 0.0, dpo_next_ref[...])
        dpo_w = jnp.concatenate([dpo_ref[...], nxt], axis=0)
        ps = ps_ref[...]
        dps = []
        scaled = []
        for gi in range(POOL_GROUPS):
            sl = slice(gi * 128, (gi + 1) * 128)
            dpm = dpo_w[:, sl] * ps[:, sl]
            pm = _mdot(pooled[gi], pw_ref[gi], NN)
            dps.append(jnp.sum(dpo_w[:tm, sl] * pm, axis=0, keepdims=True))
            dpw_ref[gi] += _mdot(pooled[gi], dpm[:tm], TN)
            dpooled = _mdot(dpm, pw_ref[gi], NT)
            scaled.append((dpooled, dpooled / counts[gi]))
        dps_ref[...] += jnp.concatenate(dps, axis=1)
        lead = _window_sums(jnp.concatenate([sc for _, sc in scaled], axis=1), forward=True)
        for gi in range(POOL_GROUPS):
            sl = slice(gi * 128, (gi + 1) * 128)
            dzp_ref[:, GDN_WIDTH + gi * 128:GDN_WIDTH + (gi + 1) * 128] = (
                lead[gi][:tm, sl] - scaled[gi][0][:tm]).astype(dzp_ref.dtype)

    last_halo = T // HALO - 1
    return pl.pallas_call(
        body, name=name, grid=(n_tiles,),
        in_specs=[pl.BlockSpec((tm, GDN_WIDTH), lambda i: (i, 0)),
                  pl.BlockSpec((tm, POOL_WIDTH), lambda i: (i, 1)),
                  pl.BlockSpec((HALO, POOL_WIDTH), lambda i: (jnp.minimum((i + 1) * hb, last_halo), 1)),
                  pl.BlockSpec((tm, GDN_WIDTH), lambda i: (i, 0)),
                  pl.BlockSpec((tm, GDN_WIDTH), lambda i: (i, COL_Z // GDN_WIDTH)),
                  pl.BlockSpec((tm, POOL_WIDTH), lambda i: (i, COL_P // POOL_WIDTH)),
                  pl.BlockSpec((HALO, POOL_WIDTH), lambda i: (jnp.maximum(i * hb - 1, 0), COL_P // POOL_WIDTH)),
                  pl.BlockSpec((1, HEAD_DIM), lambda i: (0, 0)),
                  pl.BlockSpec((POOL_GROUPS, 128, 128), lambda i: (0, 0, 0)),
                  pl.BlockSpec((1, POOL_WIDTH), lambda i: (0, 0))],
        out_specs=[pl.BlockSpec((tm, GDN_WIDTH), lambda i: (i, 0)),
                   pl.BlockSpec((tm, GDN_WIDTH + POOL_WIDTH), lambda i: (i, 0)),
                   pl.BlockSpec((1, HEAD_DIM), lambda i: (0, 0)),
                   pl.BlockSpec((POOL_GROUPS, 128, 128), lambda i: (0, 0, 0)),
                   pl.BlockSpec((1, POOL_WIDTH), lambda i: (0, 0))],
        out_shape=[jax.ShapeDtypeStruct((T, GDN_WIDTH), F32),
                   jax.ShapeDtypeStruct((T, GDN_WIDTH + POOL_WIDTH), MXU_DTYPE),
                   jax.ShapeDtypeStruct((1, HEAD_DIM), F32),
                   jax.ShapeDtypeStruct((POOL_GROUPS, 128, 128), F32),
                   jax.ShapeDtypeStruct((1, POOL_WIDTH), F32)],
        compiler_params=_params("arbitrary"),
    )(dmix, dmix, dmix, o, proj, proj, proj, gdn_norm, pool_w, pool_scale)


def _gdn_prep_bwd(proj, conv_w, a_log_l, dt_bias_l, dqkv, dbg, dzp, name):
    T = proj.shape[0]
    tm = _tile(T, 256)
    hb = tm // 8
    n_tiles = T // tm
    last_halo = T // 8 - 1

    def body(cur_ref, before_ref, after_ref, ba_ref, w_ref, al_ref, dtb_ref, dq_ref, dq_after_ref, dbg_ref,
             dzp_ref, dproj_ref, dw_ref, dal_ref, ddtb_ref):
        i = pl.program_id(0)

        @pl.when(i == 0)
        def _():
            dw_ref[...] = jnp.zeros_like(dw_ref)
            dal_ref[...] = jnp.zeros_like(dal_ref)
            ddtb_ref[...] = jnp.zeros_like(ddtb_ref)

        last = i == n_tiles - 1
        w = w_ref[...]
        before = jnp.where(i == 0, 0.0, before_ref[...])
        after = jnp.where(last, 0.0, after_ref[...])
        window = jnp.concatenate([before, cur_ref[...], after], axis=0)
        y = _conv_act(window, w)
        sig = _sigmoid(y)
        act = y * sig
        dq_w = jnp.concatenate([jnp.zeros((8, QKV_WIDTH), F32), dq_ref[...],
                                jnp.where(last, 0.0, dq_after_ref[...])], axis=0)
        dact = []
        for hh in range(3 * GDN_HEADS):
            sl = slice(hh * HEAD_DIM, (hh + 1) * HEAD_DIM)
            blk, dblk = act[:, sl], dq_w[:, sl]
            if hh < 2 * GDN_HEADS:
                rn = lax.rsqrt(jnp.sum(blk * blk, axis=-1, keepdims=True) + EPS)
                unit = blk * rn
                if hh < GDN_HEADS:
                    dblk = dblk * (HEAD_DIM ** -0.5)
                dblk = rn * (dblk - unit * jnp.sum(dblk * unit, axis=-1, keepdims=True))
            dact.append(dblk)
        dy = jnp.concatenate(dact, axis=1) * (sig * (1.0 + y * (1.0 - sig)))
        dx = dy * w[CONV_K - 1:CONV_K, :]
        dws = [None] * CONV_K
        dws[CONV_K - 1] = jnp.sum(dy[8:8 + tm] * window[8:8 + tm], axis=0, keepdims=True)
        for j in range(CONV_K - 1):
            s = CONV_K - 1 - j
            dx += _shift_rows(dy, -s) * w[j:j + 1, :]
            dws[j] = jnp.sum(dy[8:8 + tm] * _shift_rows(window, s)[8:8 + tm], axis=0, keepdims=True)
        dw_ref[...] += jnp.concatenate(dws, axis=0)
        dproj_ref[:, :QKV_WIDTH] = dx[8:8 + tm].astype(dproj_ref.dtype)
        dproj_ref[:, COL_Z:COL_BA] = dzp_ref[...]

        ba = ba_ref[...]
        dbg_ = dbg_ref[...]
        lane = lax.broadcasted_iota(jnp.int32, ba.shape, 1)
        beta = _sigmoid(ba)
        pre = ba + dtb_ref[...]
        neg_a = -jnp.exp(al_ref[...])
        g = neg_a * _softplus(pre)
        is_g = (lane >= GDN_HEADS) & (lane < 2 * GDN_HEADS)
        da_raw = jnp.where(is_g, dbg_ * neg_a * _sigmoid(pre), 0.0)
        dba = jnp.where(lane < GDN_HEADS, dbg_ * beta * (1.0 - beta), da_raw)
        dproj_ref[:, COL_BA:] = dba.astype(dproj_ref.dtype)
        dal_ref[...] += jnp.sum(jnp.where(is_g, dbg_ * g, 0.0), axis=0, keepdims=True)
        ddtb_ref[...] += jnp.sum(da_raw, axis=0, keepdims=True)

    lane_vec = pl.BlockSpec((1, 128), lambda i: (0, 0))
    return pl.pallas_call(
        body, name=name, grid=(n_tiles,),
        in_specs=[pl.BlockSpec((tm, QKV_WIDTH), lambda i: (i, 0)),
                  pl.BlockSpec((8, QKV_WIDTH), lambda i: (jnp.maximum(i * hb - 1, 0), 0)),
                  pl.BlockSpec((8, QKV_WIDTH), lambda i: (jnp.minimum((i + 1) * hb, last_halo), 0)),
                  pl.BlockSpec((tm, 128), lambda i: (i, COL_BA // 128)),
                  pl.BlockSpec((CONV_K, QKV_WIDTH), lambda i: (0, 0)), lane_vec, lane_vec,
                  pl.BlockSpec((tm, QKV_WIDTH), lambda i: (i, 0)),
                  pl.BlockSpec((8, QKV_WIDTH), lambda i: (jnp.minimum((i + 1) * hb, last_halo), 0)),
                  pl.BlockSpec((tm, 128), lambda i: (i, 0)),
                  pl.BlockSpec((tm, GDN_WIDTH + POOL_WIDTH), lambda i: (i, 0))],
        out_specs=[pl.BlockSpec((tm, D_IN_PAD), lambda i: (i, 0)),
                   pl.BlockSpec((CONV_K, QKV_WIDTH), lambda i: (0, 0)), lane_vec, lane_vec],
        out_shape=[jax.ShapeDtypeStruct((T, D_IN_PAD), MXU_DTYPE),
                   jax.ShapeDtypeStruct((CONV_K, QKV_WIDTH), F32),
                   jax.ShapeDtypeStruct((1, 128), F32), jax.ShapeDtypeStruct((1, 128), F32)],
        compiler_params=_params("arbitrary"),
    )(proj, proj, proj, proj, conv_w, a_log_l, dt_bias_l, dqkv, dqkv, dbg, dzp)


def _mod_part(c_all, w_ada, b_part, name):
    def body(c_ref, w_ref, b_ref, out_ref):
        cc = c_ref[...]
        out_ref[...] = _mdot(cc * _sigmoid(cc), w_ref[...], NN) + b_ref[...]

    return pl.pallas_call(
        body, name=name, out_shape=jax.ShapeDtypeStruct((c_all.shape[0], w_ada.shape[1]), F32),
        compiler_params=_params(),
    )(c_all, w_ada, b_part)


def _w_ada_grad(c_all, dmod_part, name):
    def body(c_ref, d_ref, out_ref):
        cc = c_ref[...]
        out_ref[...] = _mdot(cc * _sigmoid(cc), d_ref[...], TN)

    return pl.pallas_call(
        body, name=name, out_shape=jax.ShapeDtypeStruct((c_all.shape[1], dmod_part.shape[1]), F32),
        compiler_params=_params(),
    )(c_all, dmod_part)


def _sum_parts(parts, name):
    _, R, C = parts.shape
    tr = _tile(R, 256)

    def body(p_ref, out_ref):
        acc = p_ref[0].astype(F32)
        for s in range(1, N_DEV):
            acc += p_ref[s].astype(F32)
        out_ref[...] = acc

    return pl.pallas_call(
        body, name=name, grid=(R // tr,),
        in_specs=[pl.BlockSpec((N_DEV, tr, C), lambda i: (0, i, 0))],
        out_specs=pl.BlockSpec((tr, C), lambda i: (i, 0)),
        out_shape=jax.ShapeDtypeStruct((R, C), F32),
        compiler_params=_params("parallel"),
    )(parts)


def _adamw(w, g, m, v, name):
    R, C = w.shape
    tr = _tile(R, 256)

    def body(w_ref, g_ref, m_ref, v_ref, d_ref, mo_ref, vo_ref):
        gg = g_ref[...]
        mm = ADAM_B1 * m_ref[...] + (1.0 - ADAM_B1) * gg
        vv = ADAM_B2 * v_ref[...] + (1.0 - ADAM_B2) * (gg * gg)
        m_hat = mm / (1.0 - ADAM_B1 ** ADAM_STEP)
        v_hat = vv / (1.0 - ADAM_B2 ** ADAM_STEP)
        d_ref[...] = -ADAM_LR * (m_hat / (jnp.sqrt(v_hat) + ADAM_EPS) + ADAM_WD * w_ref[...])
        mo_ref[...] = mm
        vo_ref[...] = vv

    spec = pl.BlockSpec((tr, C), lambda i: (i, 0))
    return pl.pallas_call(
        body, name=name, grid=(R // tr,),
        in_specs=[spec] * 4, out_specs=[spec] * 3,
        out_shape=[jax.ShapeDtypeStruct((R, C), F32)] * 3,
        compiler_params=_params("parallel"),
    )(w, g, m, v)


def _swiglu_backward_weights(tag, dxo, y, gate, h, g, u, a, wd):
    dy, dgate = _gate_bwd(dxo, y, gate, 0.5, f"{tag}_gate_bwd")
    dg, du = _ffn_bwd_act(dy, wd, g, u, f"{tag}_bwd_act")
    d_wd = _matmul([(a, dy)], TN, F32, f"{tag}_dwd", tm=1408, tn=1024, tk=512)
    d_wg_t = _matmul([(dg, h)], TN, F32, f"{tag}_dwg", tm=1408, tn=1024, tk=512)
    d_wu_t = _matmul([(du, h)], TN, F32, f"{tag}_dwu", tm=1408, tn=1024, tk=512)
    return dg, du, dgate, d_wg_t, d_wu_t, d_wd


def _swiglu_backward_input(tag, dg, du, wg_t, wu_t, dep=None):
    return _matmul([(dg, wg_t), (du, wu_t)], NN, F32, f"{tag}_dh", tm=512, tn=1024, tk=1408, dep=dep)


def _rows_of(flat, lanes=1024):
    flat = flat.reshape(-1)
    n = -(-flat.shape[0] // lanes) * lanes
    return jnp.pad(flat, (0, n - flat.shape[0])).reshape(n // lanes, lanes)


def _pad_rows(a, rows):
    return jnp.pad(a, ((0, rows - a.shape[0]), (0, 0)))


def kernel(x, c, w_ada, b_ada, norm_ffn1, ffn1_gate, ffn1_up, ffn1_down, norm_mix, w_in, conv_w, a_log, dt_bias, gdn_norm, pool_w, pool_scale, w_out, norm_ffn2, ffn2_gate, ffn2_up, ffn2_down, final_norm, loss_target, m_w_ada, m_b_ada, m_norm_ffn1, m_ffn1_gate, m_ffn1_up, m_ffn1_down, m_norm_mix, m_w_in, m_conv_w, m_a_log, m_dt_bias, m_gdn_norm, m_pool_w, m_pool_scale, m_w_out, m_norm_ffn2, m_ffn2_gate, m_ffn2_up, m_ffn2_down, m_final_norm, v_w_ada, v_b_ada, v_norm_ffn1, v_ffn1_gate, v_ffn1_up, v_ffn1_down, v_norm_mix, v_w_in, v_conv_w, v_a_log, v_dt_bias, v_gdn_norm, v_pool_w, v_pool_scale, v_w_out, v_norm_ffn2, v_ffn2_gate, v_ffn2_up, v_ffn2_down, v_final_norm):
    T, D = x.shape[1], x.shape[2]
    Fs = ffn1_gate.shape[2]
    Ws = w_in.shape[2]
    Ws_pad = -(-Ws // 16) * 16
    Os = w_out.shape[1]
    Ms = w_ada.shape[2]
    Cs = conv_w.shape[2]
    me = 4 * lax.axis_index("x") + 2 * lax.axis_index("y") + lax.axis_index("c")
    x0, target = x[0], loss_target[0]

    def wire(a):
        return a.astype(WIRE_DTYPE)

    def token(started):
        return started[4][:1, :1]

    def with_own(landed, own):
        return lax.dynamic_update_slice(landed, own[None], (me, 0, 0))

    def full(landed, off, size, keep=None):
        blk = landed[:, off:off + (size if keep is None else keep), :]
        return blk.reshape(-1, D).astype(MXU_DTYPE)

    no_dep = jnp.zeros((8, 128), F32)
    w1 = jnp.concatenate([wire(ffn1_gate[0].T), wire(ffn1_up[0].T), wire(ffn1_down[0])], axis=0)
    w23 = jnp.concatenate([wire(_pad_rows(w_in[0].T, Ws_pad)), wire(w_out[0]),
                           wire(ffn2_gate[0].T), wire(ffn2_up[0].T), wire(ffn2_down[0])], axis=0)
    off23 = [0, Ws_pad, Ws_pad + Os, Ws_pad + Os + Fs, Ws_pad + Os + 2 * Fs]
    w1_sent = _exchange_start(w1, True, no_dep, "w1_start")

    small = jnp.concatenate([_pad_rows(c, 8), _pad_rows(jnp.pad(conv_w[0], ((0, 0), (0, D - Cs))), 8)], axis=0)
    got = _all_gather(small + token(w1_sent), "gather_small")
    c_all = got[:, 0, :]
    conv_full = jnp.transpose(got[:, 8:8 + CONV_K, :Cs], (1, 0, 2)).reshape(CONV_K, QKV_WIDTH)
    b_part = lax.dynamic_slice(b_ada, (0, me * Ms), (1, Ms))
    mod_parts = _all_gather(_mod_part(c_all, w_ada[0], b_part, "mod_part"), "gather_mod")
    mod_all = jnp.transpose(mod_parts, (1, 0, 2)).reshape(N_DEV, N_MOD * D)
    mod = lax.dynamic_slice(mod_all, (me, 0), (1, N_MOD * D)).reshape(N_MOD, 1, D)
    sh1, sc1, gt1, sh2, sc2, gt2, sh3, sc3, gt3 = [mod[i] for i in range(N_MOD)]

    w1_all = with_own(_exchange_wait(w1_sent, mod_all, True, "w1_wait"), w1)
    wg1_t, wu1_t, wd1 = full(w1_all, 0, Fs), full(w1_all, Fs, Fs), full(w1_all, 2 * Fs, Fs)
    w23_sent = _exchange_start(w23, True, w1_all, "w23_start")

    lane_pad = lambda a: jnp.pad(a, ((0, 0), (GDN_HEADS, 128 - 2 * GDN_HEADS)))
    a_log_l, dt_bias_l = lane_pad(a_log), lane_pad(dt_bias)
    pool_w_m = pool_w[0].astype(MXU_DTYPE)

    h1 = _norm_mod(x0, norm_ffn1, sc1 + token(w23_sent), sh1, "norm1")
    y1, g1, u1, a1 = _ffn_fwd(h1, wg1_t, wu1_t, wd1, "ffn1_fwd")
    x1, h2 = _resid_norm_mod(x0, y1, gt1, 0.5, norm_mix, sc2, sh2, "resid_norm2")
    w23_all = with_own(_exchange_wait(w23_sent, h2, True, "w23_wait"), w23)
    w_in_t = full(w23_all, off23[0], Ws_pad, Ws)
    wo = full(w23_all, off23[1], Os)
    wg2_t, wu2_t, wd2 = full(w23_all, off23[2], Fs), full(w23_all, off23[3], Fs), full(w23_all, off23[4], Fs)
    w_in_re = jnp.concatenate([w_in_t[:COL_Z + GDN_WIDTH], w_in_t[D_IN - POOL_WIDTH:],
                               w_in_t[4 * GDN_WIDTH:4 * GDN_WIDTH + 2 * GDN_HEADS],
                               jnp.zeros((128 - 2 * GDN_HEADS, D), MXU_DTYPE)], axis=0)
    proj = _matmul([(h2, w_in_re)], NT, F32, "proj_in", tm=512, tn=D_IN_PAD // 3, tk=D)
    qkv, bg = _gdn_prep(proj, conv_full, a_log_l, dt_bias_l, "gdn_prep")
    tinv, u_c, w_c, qd_c, kd_c, p_c, cd_c = _gdn_chunk_fwd(qkv, bg, "gdn_chunk_fwd")
    o, s_all, vn_c = _gdn_scan_fwd(u_c, w_c, qd_c, kd_c, p_c, cd_c, "gdn_scan_fwd")
    mix_in = _mix_post(o, proj, gdn_norm, pool_w_m, pool_scale, "mix_post")
    mixed = _matmul([(mix_in, wo)], NN, F32, "mix_out", tm=512, tn=D, tk=GDN_WIDTH + POOL_WIDTH)
    x2, h3 = _resid_norm_mod(x1, mixed, gt2, 1.0, norm_ffn2, sc3, sh3, "resid_norm3")
    y3, g3, u3, a3 = _ffn_fwd(h3, wg2_t, wu2_t, wd2, "ffn2_fwd")
    loss_row, d3, d_final = _final_loss(x2, y3, gt3, final_norm.reshape(1, D), target, "final_loss")

    dg3, du3, dgt3, d_wg2, d_wu2, d_wd2 = _swiglu_backward_weights("ffn2", d3, y3, gt3, h3, g3, u3, a3, wd2)
    dh3 = _swiglu_backward_input("ffn2", dg3, du3, wg2_t, wu2_t)
    d2, d_n3, dsc3, dsh3 = _norm_bwd(dh3, x2, norm_ffn2, sc3, d3, "norm3_bwd")
    dmixed, dgt2 = _gate_bwd(d2, mixed, gt2, 1.0, "mix_gate_bwd")
    dmix_in = _matmul([(dmixed, wo)], NT, F32, "mix_out_bwd", tm=512, tn=GDN_WIDTH + POOL_WIDTH, tk=D)
    d_wo = _matmul([(mix_in, dmixed)], TN, F32, "mix_dwo", tm=512, tn=D, tk=512)
    do, dzp, d_gn, d_pw, d_ps = _mix_post_bwd(dmix_in, o, proj, gdn_norm, pool_w_m, pool_scale, "mix_post_bwd")
    dvn, dw_c, dqd, dkd, dp_c, dcd = _gdn_scan_bwd(do, w_c, qd_c, kd_c, p_c, cd_c, s_all, vn_c, "gdn_scan_bwd")
    dqkv, dbg = _gdn_chunk_bwd(qkv, bg, tinv, u_c, w_c, dvn, dw_c, dqd, dkd, dp_c, dcd, "gdn_chunk_bwd")
    dproj, d_conv, d_al, d_dtb = _gdn_prep_bwd(proj, conv_full, a_log_l, dt_bias_l, dqkv, dbg, dzp, "gdn_prep_bwd")
    dh2 = _matmul([(dproj, w_in_re)], NN, F32, "proj_in_bwd", tm=512, tn=D, tk=D_IN_PAD // 3)
    d_win_re = _matmul([(dproj, h2)], TN, F32, "proj_in_dw", tm=D_IN_PAD // 3, tn=D, tk=512)
    d_win_t = jnp.concatenate([d_win_re[:COL_Z + GDN_WIDTH], d_win_re[COL_BA:COL_BA + 2 * GDN_HEADS],
                               d_win_re[COL_P:COL_P + POOL_WIDTH]], axis=0)
    d_win_blocks = jnp.pad(d_win_t.reshape(N_DEV, Ws, D), ((0, 0), (0, Ws_pad - Ws), (0, 0)))
    parts23 = jnp.concatenate(
        [wire(d_win_blocks), wire(d_wo.reshape(N_DEV, Os, D)), wire(d_wg2.reshape(N_DEV, Fs, D)),
         wire(d_wu2.reshape(N_DEV, Fs, D)), wire(d_wd2.reshape(N_DEV, Fs, D))], axis=1)
    own23 = lax.dynamic_index_in_dim(parts23, me, 0, keepdims=False)
    g23_sent = _exchange_start(parts23, False, no_dep, "g23_start")
    d1, d_n2, dsc2, dsh2 = _norm_bwd(dh2, x1, norm_mix, sc2 + token(g23_sent), d2, "norm2_bwd")
    dg1, du1, dgt1, d_wg1, d_wu1, d_wd1 = _swiglu_backward_weights("ffn1", d1, y1, gt1, h1, g1, u1, a1, wd1)
    parts1 = jnp.concatenate([wire(d_wg1.reshape(N_DEV, Fs, D)), wire(d_wu1.reshape(N_DEV, Fs, D)),
                              wire(d_wd1.reshape(N_DEV, Fs, D))], axis=1)
    own1 = lax.dynamic_index_in_dim(parts1, me, 0, keepdims=False)
    g1_sent = _exchange_start(parts1, False, no_dep, "g1_start")
    dh1 = _swiglu_backward_input("ffn1", dg1, du1, wg1_t, wu1_t, dep=g1_sent[4])
    grad_x, d_n1, dsc1, dsh1 = _norm_bwd(dh1, x0, norm_ffn1, sc1, d1, "norm1_bwd")

    dmod = jnp.concatenate([dsh1, dsc1, dgt1, dsh2, dsc2, dgt2, dsh3, dsc3, dgt3], axis=0)
    small_rows = [dmod.reshape(-1), d_n1[0], d_n2[0], d_n3[0], d_final[0], d_gn[0], d_ps[0],
                  d_al[0, GDN_HEADS:2 * GDN_HEADS], d_dtb[0, GDN_HEADS:2 * GDN_HEADS], loss_row[0, :1],
                  d_conv.reshape(-1), d_pw.reshape(-1)]
    lanes = 1024
    small_rows = [_rows_of(r, lanes) for r in small_rows]
    n_rows = [r.shape[0] for r in small_rows]
    row_off = [sum(n_rows[:i]) for i in range(len(n_rows))]
    total = -(-sum(n_rows) // 8) * 8
    slab = _pad_rows(jnp.concatenate(small_rows, axis=0), total)
    slab_all = _all_gather(slab, "gather_small_grads")
    summed = _sum_parts(slab_all, "sum_small_grads")

    def piece(idx, n):
        return summed[row_off[idx]:row_off[idx] + n_rows[idx]].reshape(-1)[:n]

    g_b_ada = piece(0, N_MOD * D).reshape(1, N_MOD * D)
    g_n1, g_n2, g_n3 = piece(1, D).reshape(1, D), piece(2, D).reshape(1, D), piece(3, D).reshape(1, D)
    g_final = piece(4, D)
    g_gn = piece(5, HEAD_DIM).reshape(1, HEAD_DIM)
    g_ps = piece(6, POOL_WIDTH).reshape(1, POOL_WIDTH)
    g_al = piece(7, GDN_HEADS).reshape(1, GDN_HEADS)
    g_dtb = piece(8, GDN_HEADS).reshape(1, GDN_HEADS)
    loss = piece(9, 1)[0]
    g_conv = lax.dynamic_slice(piece(10, CONV_K * QKV_WIDTH).reshape(1, CONV_K, QKV_WIDTH), (0, 0, me * Cs),
                               (1, CONV_K, Cs))
    g_pw = piece(11, POOL_GROUPS * 128 * 128).reshape(1, POOL_GROUPS, 128, 128)

    dmod_all = slab_all[:, row_off[0]:row_off[0] + n_rows[0], :].reshape(N_DEV, -1)[:, :N_MOD * D]
    g_w_ada = _w_ada_grad(c_all, lax.dynamic_slice(dmod_all, (0, me * Ms), (N_DEV, Ms)), "w_ada_grad")[None]

    big23 = _sum_parts(with_own(_exchange_wait(g23_sent, grad_x, False, "g23_wait"), own23), "sum_grads23")
    g_w_in, g_w_out = big23[:Ws].T[None], big23[off23[1]:off23[1] + Os][None]
    g_ffn2_gate, g_ffn2_up = big23[off23[2]:off23[2] + Fs].T[None], big23[off23[3]:off23[3] + Fs].T[None]
    g_ffn2_down = big23[off23[4]:off23[4] + Fs][None]

    names = ["w_ada", "b_ada", "norm_ffn1", "ffn1_gate", "ffn1_up", "ffn1_down", "norm_mix", "w_in", "conv_w",
             "a_log", "dt_bias", "gdn_norm", "pool_w", "pool_scale", "w_out", "norm_ffn2", "ffn2_gate", "ffn2_up",
             "ffn2_down", "final_norm"]
    weights = dict(zip(names, [w_ada, b_ada, norm_ffn1, ffn1_gate, ffn1_up, ffn1_down, norm_mix, w_in, conv_w,
                               a_log, dt_bias, gdn_norm, pool_w, pool_scale, w_out, norm_ffn2, ffn2_gate, ffn2_up,
                               ffn2_down, final_norm]))
    ms = dict(zip(names, [m_w_ada, m_b_ada, m_norm_ffn1, m_ffn1_gate, m_ffn1_up, m_ffn1_down, m_norm_mix, m_w_in,
                          m_conv_w, m_a_log, m_dt_bias, m_gdn_norm, m_pool_w, m_pool_scale, m_w_out, m_norm_ffn2,
                          m_ffn2_gate, m_ffn2_up, m_ffn2_down, m_final_norm]))
    vs = dict(zip(names, [v_w_ada, v_b_ada, v_norm_ffn1, v_ffn1_gate, v_ffn1_up, v_ffn1_down, v_norm_mix, v_w_in,
                          v_conv_w, v_a_log, v_dt_bias, v_gdn_norm, v_pool_w, v_pool_scale, v_w_out, v_norm_ffn2,
                          v_ffn2_gate, v_ffn2_up, v_ffn2_down, v_final_norm]))
    grads = dict(w_ada=g_w_ada, b_ada=g_b_ada, norm_ffn1=g_n1, norm_mix=g_n2, w_in=g_w_in, conv_w=g_conv,
                 a_log=g_al, dt_bias=g_dtb, gdn_norm=g_gn, pool_w=g_pw, pool_scale=g_ps, w_out=g_w_out,
                 norm_ffn2=g_n3, ffn2_gate=g_ffn2_gate, ffn2_up=g_ffn2_up, ffn2_down=g_ffn2_down, final_norm=g_final)
    delta, new_m, new_v = {}, {}, {}

    def adamw_big(n):
        shp = weights[n].shape
        two_d = lambda a: a.reshape(shp[-2], shp[-1])
        d_, m_, v_ = _adamw(two_d(weights[n]), two_d(grads[n]), two_d(ms[n]), two_d(vs[n]), f"adamw_{n}")
        delta[n], new_m[n], new_v[n] = d_.reshape(shp), m_.reshape(shp), v_.reshape(shp)

    early = ["w_ada", "w_in", "w_out", "ffn2_gate", "ffn2_up", "ffn2_down"]
    late = ["ffn1_gate", "ffn1_up", "ffn1_down"]
    for n in early:
        adamw_big(n)
    big1 = _sum_parts(with_own(_exchange_wait(g1_sent, delta["ffn2_down"], False, "g1_wait"), own1), "sum_grads1")
    grads["ffn1_gate"], grads["ffn1_up"] = big1[:Fs].T[None], big1[Fs:2 * Fs].T[None]
    grads["ffn1_down"] = big1[2 * Fs:][None]
    for n in late:
        adamw_big(n)
    small_names = [n for n in names if n not in early + late]
    pack = lambda src: jnp.concatenate([_rows_of(src[n]) for n in small_names], axis=0)
    p_rows = [_rows_of(weights[n]).shape[0] for n in small_names]
    p_total = -(-sum(p_rows) // 8) * 8
    packed = [_pad_rows(pack(src), p_total) for src in (weights, grads, ms, vs)]
    d_s, m_s, v_s = _adamw(*packed, "adamw_small")
    off = 0
    for n, r in zip(small_names, p_rows):
        shp = weights[n].shape
        size = weights[n].size
        for dst, src in ((delta, d_s), (new_m, m_s), (new_v, v_s)):
            dst[n] = src[off:off + r].reshape(-1)[:size].reshape(shp)
        off += r

    return (loss, grad_x[None], *[grads[n] for n in names], *[delta[n] for n in names],
            *[new_m[n] for n in names], *[new_v[n] for n in names])
```

```python
import functools

import jax
import jax.numpy as jnp
from jax import lax
from jax.experimental import pallas as pl
from jax.experimental.pallas import tpu as pltpu

F32 = jnp.float32
MXU_DTYPE = jnp.bfloat16
WIRE_DTYPE = jnp.bfloat16
EPS = 1e-6
N_DEV = 8
GDN_HEADS = 4
HEAD_DIM = 128
GDN_WIDTH = GDN_HEADS * HEAD_DIM
POOL_WINDOWS = (2, 4, 8, 16)
POOL_GROUPS = len(POOL_WINDOWS)
POOL_WIDTH = 512
CONV_K = 4
CHUNK = 64
QKV_WIDTH = 3 * GDN_WIDTH
D_IN = 4 * GDN_WIDTH + 2 * GDN_HEADS + POOL_WIDTH
D_IN_PAD = 4 * GDN_WIDTH + POOL_WIDTH + 128
COL_Z = QKV_WIDTH
COL_P = 4 * GDN_WIDTH
COL_BA = 4 * GDN_WIDTH + POOL_WIDTH
N_MOD = 9
HALO = 16
VMEM_LIMIT = 56 * 1024 * 1024
ADAM_LR, ADAM_B1, ADAM_B2, ADAM_EPS, ADAM_WD, ADAM_STEP = 0.001, 0.9, 0.999, 1e-08, 0.01, 10
CHUNKS_PER_STEP = 2
SCAN_CHUNKS_PER_STEP = 4

NT = (((1,), (1,)), ((), ()))
NN = (((1,), (0,)), ((), ()))
TN = (((0,), (0,)), ((), ()))


def _params(*sem):
    return pltpu.CompilerParams(dimension_semantics=tuple(sem), vmem_limit_bytes=VMEM_LIMIT)


def _dot(a, b, dims):
    return lax.dot_general(a, b, dims, preferred_element_type=F32)


def _mdot(a, b, dims):
    return _dot(a.astype(MXU_DTYPE), b.astype(MXU_DTYPE), dims)


def _split(a):
    hi = a.astype(jnp.bfloat16)
    return hi, (a - hi.astype(F32)).astype(jnp.bfloat16)


def _dot3(a, b, dims):
    (ah, al), (bh, bl) = a, b
    return (_dot(al, bh, dims) + _dot(ah, bl, dims)) + _dot(ah, bh, dims)


def _sigmoid(v):
    return 1.0 / (1.0 + jnp.exp(-v))


def _softplus(v):
    return jnp.maximum(v, 0.0) + jnp.log(1.0 + jnp.exp(-jnp.abs(v)))


def _shift_rows(v, s):
    n = v.shape[0]
    s = s % n
    return v if s == 0 else pltpu.roll(v, s, 0)


def _tile(n, want):
    t = min(n, want)
    while n % t:
        t //= 2
    return t


def _all_gather(block, name, dep=None):
    shape, dtype = block.shape, block.dtype

    def body(x_ref, *refs):
        out_ref, send_sems, recv_sems, local_sem = refs[-4:]
        x, y, c = lax.axis_index("x"), lax.axis_index("y"), lax.axis_index("c")
        me, sibling = (x, y, c), (x, y, 1 - c)
        chips = [(1 - x, y), (x, 1 - y), (1 - x, 1 - y)]

        def rows(px, py, pc):
            return out_ref.at[4 * px + 2 * py + pc]

        def copy(k, blk, to, src=None):
            return pltpu.make_async_remote_copy(
                src_ref=rows(*blk) if src is None else src, dst_ref=rows(*blk),
                send_sem=send_sems.at[k], recv_sem=recv_sems.at[k],
                device_id=to, device_id_type=pl.DeviceIdType.MESH)

        mine = pltpu.make_async_copy(x_ref, rows(*me), local_sem)
        mine.start()
        first = [copy(0, me, sibling, src=x_ref)]
        first += [copy(1 + j, me, (*chip, c), src=x_ref) for j, chip in enumerate(chips)]
        for cp in first:
            cp.start()
        passed = [copy(4 + j, (*chip, c), sibling) for j, chip in enumerate(chips)]
        for j, chip in enumerate(chips):
            copy(1 + j, (*chip, c), me).wait_recv()
            passed[j].start()
        copy(0, sibling, me).wait_recv()
        for j, chip in enumerate(chips):
            copy(4 + j, (*chip, 1 - c), me).wait_recv()
        for cp in first + passed:
            cp.wait_send()
        mine.wait()

    return pl.pallas_call(
        body, name=name,
        out_shape=jax.ShapeDtypeStruct((N_DEV,) + shape, dtype),
        in_specs=[pl.BlockSpec(memory_space=pltpu.HBM)] + [pl.BlockSpec(memory_space=pl.ANY)] * (dep is not None),
        out_specs=pl.BlockSpec(memory_space=pltpu.HBM),
        scratch_shapes=[pltpu.SemaphoreType.DMA((7,)), pltpu.SemaphoreType.DMA((7,)),
                        pltpu.SemaphoreType.DMA(())],
    )(*([block] if dep is None else [block, dep]))


_HBM = pl.BlockSpec(memory_space=pltpu.HBM)
_SEM = pl.BlockSpec(memory_space=pltpu.SEMAPHORE)
_ANY = pl.BlockSpec(memory_space=pl.ANY)
_EFFECT = pltpu.SideEffectType.DATAFLOW_SIDE_EFFECTING
_FLIPS = [(0, 0, 1), (0, 1, 0), (0, 1, 1), (1, 0, 0), (1, 0, 1), (1, 1, 0), (1, 1, 1)]


def _peers():
    x, y, c = lax.axis_index("x"), lax.axis_index("y"), lax.axis_index("c")
    return 4 * x + 2 * y + c, [(1 - x if fx else x, 1 - y if fy else y, 1 - c if fc else c)
                               for fx, fy, fc in _FLIPS]


def _exchange_start(src, gather, dep, name):
    block = src.shape if gather else src.shape[1:]
    land = (N_DEV,) + tuple(block)

    def body(src_ref, land_ref, dep_ref, send_sems, recv_sems, src_thru, land_thru, token):
        me, peers = _peers()
        for k, (px, py, pc) in enumerate(peers):
            pltpu.make_async_remote_copy(
                src_ref=src_ref if gather else src_ref.at[4 * px + 2 * py + pc], dst_ref=land_ref.at[me],
                send_sem=send_sems.at[k], recv_sem=recv_sems.at[k],
                device_id=(px, py, pc), device_id_type=pl.DeviceIdType.MESH).start()
        token[...] = jnp.zeros_like(token)

    return pl.pallas_call(
        body, name=name,
        out_shape=(pltpu.SemaphoreType.DMA((7,)), pltpu.SemaphoreType.DMA((7,)),
                   pltpu.HBM(src.shape, src.dtype), pltpu.HBM(land, src.dtype),
                   jax.ShapeDtypeStruct((8, 128), F32)),
        in_specs=(_HBM, _HBM, _ANY),
        out_specs=(_SEM, _SEM, _HBM, _HBM, pl.BlockSpec(memory_space=pltpu.VMEM)),
        input_output_aliases={0: 2, 1: 3},
        compiler_params=pltpu.CompilerParams(has_side_effects=_EFFECT),
    )(pltpu.with_memory_space_constraint(src, pltpu.HBM),
      pltpu.with_memory_space_constraint(lax.empty(land, src.dtype), pltpu.HBM), dep)


def _exchange_wait(started, after, gather, name):
    send_sems, recv_sems, src_thru, land_thru, _ = started

    def body(src_ref, land_ref, send_sems, recv_sems, after_ref, src_dead, got_ref):
        _, peers = _peers()
        for k, peer in enumerate(peers):
            copy = pltpu.make_async_remote_copy(
                src_ref=src_ref if gather else src_ref.at[0], dst_ref=land_ref.at[0],
                send_sem=send_sems.at[k], recv_sem=recv_sems.at[k],
                device_id=peer, device_id_type=pl.DeviceIdType.MESH)
            copy.wait_send()
            copy.wait_recv()

    return pl.pallas_call(
        body, name=name,
        out_shape=(pltpu.HBM(src_thru.shape, src_thru.dtype), pltpu.HBM(land_thru.shape, land_thru.dtype)),
        in_specs=(_HBM, _HBM, _SEM, _SEM, _ANY), out_specs=(_HBM, _HBM),
        input_output_aliases={0: 0, 1: 1},
        compiler_params=pltpu.CompilerParams(has_side_effects=_EFFECT),
    )(src_thru, land_thru, send_sems, recv_sems, after)[1]


def _matmul(pairs, dims, out_dtype, name, tm=512, tn=512, tk=512, dep=None):
    a0, b0 = pairs[0]
    if dims == TN:
        K, M = a0.shape
    else:
        M, K = a0.shape
    N = b0.shape[0] if dims == NT else b0.shape[1]
    tm, tn, tk = _tile(M, tm), _tile(N, tn), _tile(K, tk)
    nk = K // tk
    n_pairs = len(pairs)
    n_in = 2 * n_pairs + (dep is not None)

    def body(*refs):
        out_ref, acc_ref = refs[n_in], refs[n_in + 1]
        k = pl.program_id(2)

        @pl.when(k == 0)
        def _():
            acc_ref[...] = jnp.zeros_like(acc_ref)

        acc = acc_ref[...]
        for p in range(n_pairs):
            acc += _dot(refs[2 * p][...], refs[2 * p + 1][...], dims)
        acc_ref[...] = acc

        @pl.when(k == nk - 1)
        def _():
            out_ref[...] = acc_ref[...].astype(out_ref.dtype)

    if dims == TN:
        a_spec = pl.BlockSpec((tk, tm), lambda i, j, k: (k, i))
    else:
        a_spec = pl.BlockSpec((tm, tk), lambda i, j, k: (i, k))
    if dims == NT:
        b_spec = pl.BlockSpec((tn, tk), lambda i, j, k: (j, k))
    else:
        b_spec = pl.BlockSpec((tk, tn), lambda i, j, k: (k, j))
    args, specs = [], []
    for a, b in pairs:
        args += [a, b]
        specs += [a_spec, b_spec]
    if dep is not None:
        args.append(dep)
        specs.append(_ANY)
    return pl.pallas_call(
        body, name=name, grid=(M // tm, N // tn, nk),
        in_specs=specs, out_specs=pl.BlockSpec((tm, tn), lambda i, j, k: (i, j)),
        out_shape=jax.ShapeDtypeStruct((M, N), out_dtype),
        scratch_shapes=[pltpu.VMEM((tm, tn), F32)],
        compiler_params=_params("parallel", "parallel", "arbitrary"),
    )(*args)


def _vec_spec(d):
    return pl.BlockSpec((1, d), lambda i: (0, 0))


def _norm_mod(x, nw, scale, shift, name):
    T, D = x.shape
    tm = _tile(T, 512)

    def body(x_ref, nw_ref, sc_ref, sh_ref, h_ref):
        xf = x_ref[...]
        r = lax.rsqrt(jnp.mean(xf * xf, axis=-1, keepdims=True) + EPS)
        h_ref[...] = ((xf * r) * nw_ref[...] * (1.0 + sc_ref[...]) + sh_ref[...]).astype(h_ref.dtype)

    row = pl.BlockSpec((tm, D), lambda i: (i, 0))
    return pl.pallas_call(
        body, name=name, grid=(T // tm,),
        in_specs=[row, _vec_spec(D), _vec_spec(D), _vec_spec(D)], out_specs=row,
        out_shape=jax.ShapeDtypeStruct((T, D), MXU_DTYPE),
        compiler_params=_params("parallel"),
    )(x, nw, scale, shift)


def _resid_norm_mod(x, y, gate, coef, nw, scale, shift, name):
    T, D = x.shape
    tm = _tile(T, 512)

    def body(x_ref, y_ref, g_ref, nw_ref, sc_ref, sh_ref, xo_ref, h_ref):
        xf = x_ref[...] + (coef * g_ref[...]) * y_ref[...]
        xo_ref[...] = xf
        r = lax.rsqrt(jnp.mean(xf * xf, axis=-1, keepdims=True) + EPS)
        h_ref[...] = ((xf * r) * nw_ref[...] * (1.0 + sc_ref[...]) + sh_ref[...]).astype(h_ref.dtype)

    row = pl.BlockSpec((tm, D), lambda i: (i, 0))
    return pl.pallas_call(
        body, name=name, grid=(T // tm,),
        in_specs=[row, row, _vec_spec(D), _vec_spec(D), _vec_spec(D), _vec_spec(D)],
        out_specs=[row, row],
        out_shape=[jax.ShapeDtypeStruct((T, D), F32), jax.ShapeDtypeStruct((T, D), MXU_DTYPE)],
        compiler_params=_params("parallel"),
    )(x, y, gate, nw, scale, shift)


def _gate_bwd(dxo, y, gate, coef, name):
    T, D = dxo.shape
    tm = _tile(T, 512)

    def body(d_ref, y_ref, g_ref, dy_ref, dg_ref):
        @pl.when(pl.program_id(0) == 0)
        def _():
            dg_ref[...] = jnp.zeros_like(dg_ref)

        d = d_ref[...]
        dy_ref[...] = ((coef * g_ref[...]) * d).astype(dy_ref.dtype)
        dg_ref[...] += coef * jnp.sum(d * y_ref[...], axis=0, keepdims=True)

    row = pl.BlockSpec((tm, D), lambda i: (i, 0))
    return pl.pallas_call(
        body, name=name, grid=(T // tm,),
        in_specs=[row, row, _vec_spec(D)], out_specs=[row, _vec_spec(D)],
        out_shape=[jax.ShapeDtypeStruct((T, D), MXU_DTYPE), jax.ShapeDtypeStruct((1, D), F32)],
        compiler_params=_params("arbitrary"),
    )(dxo, y, gate)


def _norm_bwd(dh, x, nw, scale, dres, name):
    T, D = x.shape
    tm = _tile(T, 512)

    def body(dh_ref, x_ref, nw_ref, sc_ref, dr_ref, dx_ref, dnw_ref, dsc_ref, dsh_ref):
        @pl.when(pl.program_id(0) == 0)
        def _():
            dnw_ref[...] = jnp.zeros_like(dnw_ref)
            dsc_ref[...] = jnp.zeros_like(dsc_ref)
            dsh_ref[...] = jnp.zeros_like(dsh_ref)

        xf, dh_ = x_ref[...], dh_ref[...]
        r = lax.rsqrt(jnp.mean(xf * xf, axis=-1, keepdims=True) + EPS)
        xn = xf * r
        one_sc = 1.0 + sc_ref[...]
        dsh_ref[...] += jnp.sum(dh_, axis=0, keepdims=True)
        t = dh_ * xn
        dsc_ref[...] += jnp.sum(t, axis=0, keepdims=True) * nw_ref[...]
        dnw_ref[...] += jnp.sum(t, axis=0, keepdims=True) * one_sc
        dxn = dh_ * (nw_ref[...] * one_sc)
        dx_ref[...] = dr_ref[...] + r * (dxn - xn * jnp.mean(dxn * xn, axis=-1, keepdims=True))

    row = pl.BlockSpec((tm, D), lambda i: (i, 0))
    vec = _vec_spec(D)
    return pl.pallas_call(
        body, name=name, grid=(T // tm,),
        in_specs=[row, row, vec, vec, row], out_specs=[row, vec, vec, vec],
        out_shape=[jax.ShapeDtypeStruct((T, D), F32)] + [jax.ShapeDtypeStruct((1, D), F32)] * 3,
        compiler_params=_params("arbitrary"),
    )(dh, x, nw, scale, dres)


def _final_loss(x, y, gate, fw, target, name):
    T, D = x.shape
    tm = _tile(T, 512)

    def body(x_ref, y_ref, g_ref, fw_ref, t_ref, loss_ref, dx_ref, dfw_ref):
        @pl.when(pl.program_id(0) == 0)
        def _():
            loss_ref[...] = jnp.zeros_like(loss_ref)
            dfw_ref[...] = jnp.zeros_like(dfw_ref)

        xf = x_ref[...] + (0.5 * g_ref[...]) * y_ref[...]
        r = lax.rsqrt(jnp.mean(xf * xf, axis=-1, keepdims=True) + EPS)
        xn = xf * r
        err = xn * fw_ref[...] - t_ref[...]
        per_tok = jnp.mean(err * err, axis=-1, keepdims=True)
        loss_ref[...] += 0.5 * jnp.sum(per_tok, axis=0, keepdims=True)
        dy = err * (1.0 / D)
        dfw_ref[...] += jnp.sum(dy * xn, axis=0, keepdims=True)
        dxn = dy * fw_ref[...]
        dx_ref[...] = r * (dxn - xn * jnp.mean(dxn * xn, axis=-1, keepdims=True))

    row = pl.BlockSpec((tm, D), lambda i: (i, 0))
    vec = _vec_spec(D)
    return pl.pallas_call(
        body, name=name, grid=(T // tm,),
        in_specs=[row, row, vec, vec, row],
        out_specs=[pl.BlockSpec((1, 128), lambda i: (0, 0)), row, vec],
        out_shape=[jax.ShapeDtypeStruct((1, 128), F32), jax.ShapeDtypeStruct((T, D), F32),
                   jax.ShapeDtypeStruct((1, D), F32)],
        compiler_params=_params("arbitrary"),
    )(x, y, gate, fw, target)


def _ffn_fwd(h, wg_t, wu_t, wd, name):
    T, D = h.shape
    Fdim = wd.shape[0]
    tm, tf = _tile(T, 1024), _tile(Fdim, 256)
    nf = Fdim // tf

    def body(h_ref, wg_ref, wu_ref, wd_ref, y_ref, g_ref, u_ref, a_ref, acc_ref):
        k = pl.program_id(1)

        @pl.when(k == 0)
        def _():
            acc_ref[...] = jnp.zeros_like(acc_ref)

        hh = h_ref[...]
        g = _dot(hh, wg_ref[...], NT)
        u = _dot(hh, wu_ref[...], NT)
        a = ((g * _sigmoid(g)) * u).astype(a_ref.dtype)
        g_ref[...] = g
        u_ref[...] = u
        a_ref[...] = a
        acc_ref[...] += _dot(a, wd_ref[...], NN)

        @pl.when(k == nf - 1)
        def _():
            y_ref[...] = acc_ref[...]

    hrow = pl.BlockSpec((tm, D), lambda i, k: (i, 0))
    wspec = pl.BlockSpec((tf, D), lambda i, k: (k, 0))
    fspec = pl.BlockSpec((tm, tf), lambda i, k: (i, k))
    return pl.pallas_call(
        body, name=name, grid=(T // tm, nf),
        in_specs=[hrow, wspec, wspec, wspec], out_specs=[hrow, fspec, fspec, fspec],
        out_shape=[jax.ShapeDtypeStruct((T, D), F32), jax.ShapeDtypeStruct((T, Fdim), F32),
                   jax.ShapeDtypeStruct((T, Fdim), F32), jax.ShapeDtypeStruct((T, Fdim), MXU_DTYPE)],
        scratch_shapes=[pltpu.VMEM((tm, D), F32)],
        compiler_params=_params("parallel", "arbitrary"),
    )(h, wg_t, wu_t, wd)


def _ffn_bwd(dy, wd, g, u, wg_t, wu_t, name):
    T, D = dy.shape
    Fdim = wd.shape[0]
    tm, tf = _tile(T, 1024), _tile(Fdim, 256)
    nf = Fdim // tf

    def body(dy_ref, wd_ref, g_ref, u_ref, wg_ref, wu_ref, dg_ref, du_ref, dh_ref, acc_ref):
        k = pl.program_id(1)

        @pl.when(k == 0)
        def _():
            acc_ref[...] = jnp.zeros_like(acc_ref)

        da = _dot(dy_ref[...], wd_ref[...], NT)
        gg = g_ref[...]
        sig = _sigmoid(gg)
        dg = (da * u_ref[...] * (sig * (1.0 + gg * (1.0 - sig)))).astype(dg_ref.dtype)
        du = (da * (gg * sig)).astype(du_ref.dtype)
        dg_ref[...] = dg
        du_ref[...] = du
        acc_ref[...] += _dot(dg, wg_ref[...], NN) + _dot(du, wu_ref[...], NN)

        @pl.when(k == nf - 1)
        def _():
            dh_ref[...] = acc_ref[...]

    row = pl.BlockSpec((tm, D), lambda i, k: (i, 0))
    wspec = pl.BlockSpec((tf, D), lambda i, k: (k, 0))
    fspec = pl.BlockSpec((tm, tf), lambda i, k: (i, k))
    return pl.pallas_call(
        body, name=name, grid=(T // tm, nf),
        in_specs=[row, wspec, fspec, fspec, wspec, wspec],
        out_specs=[fspec, fspec, row],
        out_shape=[jax.ShapeDtypeStruct((T, Fdim), MXU_DTYPE)] * 2 + [jax.ShapeDtypeStruct((T, D), F32)],
        scratch_shapes=[pltpu.VMEM((tm, D), F32)],
        compiler_params=_params("parallel", "arbitrary"),
    )(dy, wd, g, u, wg_t, wu_t)


def _conv_act(window, w):
    y = window * w[CONV_K - 1:CONV_K, :]
    for j in range(CONV_K - 1):
        y += _shift_rows(window, CONV_K - 1 - j) * w[j:j + 1, :]
    return y


def _gdn_prep(proj, conv_w, a_log_l, dt_bias_l, name):
    T = proj.shape[0]
    tm = _tile(T, 256)
    hb = tm // 8

    def body(cur_ref, halo_ref, ba_ref, w_ref, al_ref, dtb_ref, qkv_ref, bg_ref):
        i = pl.program_id(0)
        halo = jnp.where(i == 0, 0.0, halo_ref[...])
        window = jnp.concatenate([halo, cur_ref[...]], axis=0)
        y = _conv_act(window, w_ref[...])[8:, :]
        act = y * _sigmoid(y)
        for hh in range(3 * GDN_HEADS):
            blk = act[:, hh * HEAD_DIM:(hh + 1) * HEAD_DIM]
            if hh < 2 * GDN_HEADS:
                rn = lax.rsqrt(jnp.sum(blk * blk, axis=-1, keepdims=True) + EPS)
                blk = blk * rn
                if hh < GDN_HEADS:
                    blk = blk * (HEAD_DIM ** -0.5)
            qkv_ref[:, hh * HEAD_DIM:(hh + 1) * HEAD_DIM] = blk
        ba = ba_ref[...]
        lane = lax.broadcasted_iota(jnp.int32, ba.shape, 1)
        beta = _sigmoid(ba)
        g = -jnp.exp(al_ref[...]) * _softplus(ba + dtb_ref[...])
        bg_ref[...] = jnp.where(lane < GDN_HEADS, beta, jnp.where(lane < 2 * GDN_HEADS, g, 0.0))

    return pl.pallas_call(
        body, name=name, grid=(T // tm,),
        in_specs=[pl.BlockSpec((tm, QKV_WIDTH), lambda i: (i, 0)),
                  pl.BlockSpec((8, QKV_WIDTH), lambda i: (jnp.maximum(i * hb - 1, 0), 0)),
                  pl.BlockSpec((tm, 128), lambda i: (i, COL_BA // 128)),
                  pl.BlockSpec((CONV_K, QKV_WIDTH), lambda i: (0, 0)),
                  pl.BlockSpec((1, 128), lambda i: (0, 0)), pl.BlockSpec((1, 128), lambda i: (0, 0))],
        out_specs=[pl.BlockSpec((tm, QKV_WIDTH), lambda i: (i, 0)), pl.BlockSpec((tm, 128), lambda i: (i, 0))],
        out_shape=[jax.ShapeDtypeStruct((T, QKV_WIDTH), F32), jax.ShapeDtypeStruct((T, 128), F32)],
        compiler_params=_params("parallel"),
    )(proj, proj, proj, conv_w, a_log_l, dt_bias_l)


def _chunk_cumsum(v, reverse=False):
    row = lax.broadcasted_iota(jnp.int32, v.shape, 0)
    s = 1
    while s < CHUNK:
        if reverse:
            v = v + jnp.where(row < CHUNK - s, _shift_rows(v, -s), 0.0)
        else:
            v = v + jnp.where(row >= s, _shift_rows(v, s), 0.0)
        s *= 2
    return v


def _row_form(cols):
    padded = jnp.concatenate([cols, jnp.zeros((128 - CHUNK, 128), F32)], axis=0)
    return padded.T[:, :CHUNK]


def _chunk_masks():
    ri = lax.broadcasted_iota(jnp.int32, (CHUNK, CHUNK), 0)
    ci = lax.broadcasted_iota(jnp.int32, (CHUNK, CHUNK), 1)
    return ri >= ci, ri > ci, (ri == ci).astype(F32)


def _unit_lower_inverses(ms, eye):
    rs = [eye - m for m in ms]
    ps = [_split(-m) for m in ms]
    s = 2
    while s < CHUNK:
        ps = [_split(_dot3(p, p, NN)) for p in ps]
        r_parts = [_split(r) for r in rs]
        rs = [r + _dot3(p, rp, NN) for r, p, rp in zip(rs, ps, r_parts)]
        s *= 2
    return rs


def _head_elementwise(k, beta, gc, gcr, causal):
    decay = jnp.where(causal, jnp.exp(jnp.where(causal, gc - gcr, 0.0)), 0.0)
    return decay, k * beta, jnp.exp(gc)


def _head_slices(hh):
    return (slice(hh * HEAD_DIM, (hh + 1) * HEAD_DIM),
            slice(GDN_WIDTH + hh * HEAD_DIM, GDN_WIDTH + (hh + 1) * HEAD_DIM),
            slice(2 * GDN_WIDTH + hh * HEAD_DIM, 2 * GDN_WIDTH + (hh + 1) * HEAD_DIM))


def _gdn_chunk_fwd(qkv, bg, name):
    T = qkv.shape[0]
    cb = _tile(T // CHUNK, CHUNKS_PER_STEP)
    rows = cb * CHUNK

    def body(qkv_ref, bg_ref, tinv_ref, u_ref, w_ref, qd_ref, kd_ref, p_ref, cd_ref):
        masks = _chunk_masks()
        causal, strict, eye = masks
        heads = []
        for ci in range(cb):
            rs = slice(ci * CHUNK, (ci + 1) * CHUNK)
            bgv = bg_ref[rs, :]
            gc_all = _chunk_cumsum(bgv)
            gc_rows = _row_form(gc_all)
            cd_ref[rs, :] = jnp.exp(jnp.broadcast_to(gc_all[CHUNK - 1:CHUNK, :], (CHUNK, 128)))
            for hh in range(GDN_HEADS):
                qs, ks, vs = _head_slices(hh)
                q, k, v = qkv_ref[rs, qs], qkv_ref[rs, ks], qkv_ref[rs, vs]
                beta = bgv[:, hh:hh + 1]
                gc = gc_all[:, GDN_HEADS + hh:GDN_HEADS + hh + 1]
                decay, kb, eg = _head_elementwise(k, beta, gc, gc_rows[GDN_HEADS + hh:GDN_HEADS + hh + 1, :], causal)
                hs = slice(hh * HEAD_DIM, (hh + 1) * HEAD_DIM)
                cs = slice(hh * CHUNK, (hh + 1) * CHUNK)
                qd_ref[rs, hs] = (q * eg).astype(qd_ref.dtype)
                kd_ref[rs, hs] = (k * jnp.exp(gc[CHUNK - 1:CHUNK, :] - gc)).astype(kd_ref.dtype)
                heads.append((rs, hs, cs, q, k, v * beta, kb, kb * eg, decay))
        kks = [_mdot(kb, k, NT) for (_, _, _, _, k, _, kb, _, _) in heads]
        qks = [_mdot(q, k, NT) for (_, _, _, q, k, _, _, _, _) in heads]
        tinvs = _unit_lower_inverses([jnp.where(strict, kk * hd[8], 0.0) for kk, hd in zip(kks, heads)], eye)
        t_parts = [_split(t) for t in tinvs]
        us = [_dot3(tp, _split(hd[5]), NN) for tp, hd in zip(t_parts, heads)]
        ws = [_dot3(tp, _split(hd[7]), NN) for tp, hd in zip(t_parts, heads)]
        for hd, tinv, u, w, qk in zip(heads, tinvs, us, ws, qks):
            rs, hs, cs = hd[0], hd[1], hd[2]
            tinv_ref[rs, cs] = tinv
            u_ref[rs, hs] = u
            w_ref[rs, hs] = w.astype(w_ref.dtype)
            p_ref[rs, cs] = jnp.where(causal, qk * hd[8], 0.0).astype(p_ref.dtype)

    def spec(width):
        return pl.BlockSpec((rows, width), lambda n: (n, 0))

    hw, cw = GDN_WIDTH, GDN_HEADS * CHUNK
    return pl.pallas_call(
        body, name=name, grid=(T // rows,),
        in_specs=[spec(QKV_WIDTH), spec(128)],
        out_specs=[spec(cw), spec(hw), spec(hw), spec(hw), spec(hw), spec(cw), spec(128)],
        out_shape=[jax.ShapeDtypeStruct((T, cw), F32), jax.ShapeDtypeStruct((T, hw), F32),
                   jax.ShapeDtypeStruct((T, hw), MXU_DTYPE), jax.ShapeDtypeStruct((T, hw), MXU_DTYPE),
                   jax.ShapeDtypeStruct((T, hw), MXU_DTYPE), jax.ShapeDtypeStruct((T, cw), MXU_DTYPE),
                   jax.ShapeDtypeStruct((T, 128), F32)],
        compiler_params=_params("parallel"),
    )(qkv, bg)


def _gdn_scan_fwd(u, w, qd, kd, p, cd, name):
    T = u.shape[0]
    cb = _tile(T // CHUNK, SCAN_CHUNKS_PER_STEP)
    rows = cb * CHUNK

    def body(u_ref, w_ref, qd_ref, kd_ref, p_ref, cd_ref, o_ref, s_all_ref, vn_ref, s_ref):
        @pl.when(pl.program_id(0) == 0)
        def _():
            s_ref[...] = jnp.zeros_like(s_ref)

        hss = [slice(hh * HEAD_DIM, (hh + 1) * HEAD_DIM) for hh in range(GDN_HEADS)]
        css = [slice(hh * CHUNK, (hh + 1) * CHUNK) for hh in range(GDN_HEADS)]
        s_cur = [s_ref[hh] for hh in range(GDN_HEADS)]
        for ci in range(cb):
            rs = slice(ci * CHUNK, (ci + 1) * CHUNK)
            for hh in range(GDN_HEADS):
                s_all_ref[ci * GDN_WIDTH + hh * HEAD_DIM:ci * GDN_WIDTH + (hh + 1) * HEAD_DIM, :] = s_cur[hh]
            s_ms = [s.astype(MXU_DTYPE) for s in s_cur]
            w_s = [_dot(w_ref[rs, hs], s_m, NN) for hs, s_m in zip(hss, s_ms)]
            q_s = [_dot(qd_ref[rs, hs], s_m, NN) for hs, s_m in zip(hss, s_ms)]
            v_ms = [(u_ref[rs, hs] - ws_).astype(MXU_DTYPE) for hs, ws_ in zip(hss, w_s)]
            k_v = [_dot(kd_ref[rs, hs], v_m, TN) for hs, v_m in zip(hss, v_ms)]
            p_v = [_dot(p_ref[rs, cs], v_m, NN) for cs, v_m in zip(css, v_ms)]
            for hh in range(GDN_HEADS):
                vn_ref[rs, hss[hh]] = v_ms[hh]
                o_ref[rs, hss[hh]] = q_s[hh] + p_v[hh]
                c_dec = cd_ref[ci * CHUNK:ci * CHUNK + 1, GDN_HEADS + hh:GDN_HEADS + hh + 1]
                s_cur[hh] = s_cur[hh] * c_dec + k_v[hh]
        for hh in range(GDN_HEADS):
            s_ref[hh] = s_cur[hh]

    def spec(width):
        return pl.BlockSpec((rows, width), lambda n: (n, 0))

    hw, cw = GDN_WIDTH, GDN_HEADS * CHUNK
    return pl.pallas_call(
        body, name=name, grid=(T // rows,),
        in_specs=[spec(hw), spec(hw), spec(hw), spec(hw), spec(cw), spec(128)],
        out_specs=[spec(hw), pl.BlockSpec((cb * GDN_WIDTH, HEAD_DIM), lambda n: (n, 0)), spec(hw)],
        out_shape=[jax.ShapeDtypeStruct((T, hw), F32),
                   jax.ShapeDtypeStruct((T // CHUNK * GDN_WIDTH, HEAD_DIM), F32),
                   jax.ShapeDtypeStruct((T, hw), MXU_DTYPE)],
        scratch_shapes=[pltpu.VMEM((GDN_HEADS, HEAD_DIM, HEAD_DIM), F32)],
        compiler_params=_params("arbitrary"),
    )(u, w, qd, kd, p, cd)


def _gdn_scan_bwd(do, w, qd, kd, p, cd, s_all, vn, name):
    T = do.shape[0]
    cb = _tile(T // CHUNK, SCAN_CHUNKS_PER_STEP)
    rows = cb * CHUNK
    n_steps = T // rows

    def body(do_ref, w_ref, qd_ref, kd_ref, p_ref, cd_ref, s_all_ref, vn_ref,
             dvn_ref, dw_ref, dqd_ref, dkd_ref, dp_ref, dcd_ref, ds_ref):
        @pl.when(pl.program_id(0) == 0)
        def _():
            ds_ref[...] = jnp.zeros_like(ds_ref)

        causal, _, _ = _chunk_masks()
        lane = lax.broadcasted_iota(jnp.int32, (CHUNK, 128), 1)
        heads = range(GDN_HEADS)
        hss = [slice(hh * HEAD_DIM, (hh + 1) * HEAD_DIM) for hh in heads]
        css = [slice(hh * CHUNK, (hh + 1) * CHUNK) for hh in heads]
        ds_cur = [ds_ref[hh] for hh in heads]
        for ci in reversed(range(cb)):
            rs = slice(ci * CHUNK, (ci + 1) * CHUNK)
            ds_ms = [d.astype(MXU_DTYPE) for d in ds_cur]
            s_olds = [s_all_ref[ci * GDN_WIDTH + hh * HEAD_DIM:ci * GDN_WIDTH + (hh + 1) * HEAD_DIM, :] for hh in heads]
            s_ms = [s.astype(MXU_DTYPE) for s in s_olds]
            do_ms = [do_ref[rs, hs].astype(MXU_DTYPE) for hs in hss]
            p_do = [_dot(p_ref[rs, cs], do_m, TN) for cs, do_m in zip(css, do_ms)]
            k_ds = [_dot(kd_ref[rs, hs], ds_m, NN) for hs, ds_m in zip(hss, ds_ms)]
            q_do = [_dot(qd_ref[rs, hs], do_m, TN) for hs, do_m in zip(hss, do_ms)]
            dqds = [_dot(do_m, s_m, NT) for do_m, s_m in zip(do_ms, s_ms)]
            dkds = [_dot(vn_ref[rs, hs], ds_m, NT) for hs, ds_m in zip(hss, ds_ms)]
            dps = [_dot(do_m, vn_ref[rs, hs], NT) for hs, do_m in zip(hss, do_ms)]
            dv_news = [a + b for a, b in zip(p_do, k_ds)]
            dvn_ms = [d.astype(MXU_DTYPE) for d in dv_news]
            w_dv = [_dot(w_ref[rs, hs], dvn_m, TN) for hs, dvn_m in zip(hss, dvn_ms)]
            dws = [_dot(dvn_m, s_m, NT) for dvn_m, s_m in zip(dvn_ms, s_ms)]
            dcd_tile = jnp.zeros((CHUNK, 128), F32)
            for hh in heads:
                dvn_ref[rs, hss[hh]] = dv_news[hh]
                dw_ref[rs, hss[hh]] = -dws[hh]
                dqd_ref[rs, hss[hh]] = dqds[hh]
                dkd_ref[rs, hss[hh]] = dkds[hh]
                dp_ref[rs, css[hh]] = jnp.where(causal, dps[hh], 0.0)
                dcd = jnp.sum(jnp.sum(s_olds[hh] * ds_cur[hh], axis=1, keepdims=True), axis=0, keepdims=True)
                dcd_tile = jnp.where(lane == GDN_HEADS + hh, dcd, dcd_tile)
                c_dec = cd_ref[ci * CHUNK:ci * CHUNK + 1, GDN_HEADS + hh:GDN_HEADS + hh + 1]
                ds_cur[hh] = c_dec * ds_cur[hh] + q_do[hh] - w_dv[hh]
            dcd_ref[rs, :] = dcd_tile
        for hh in heads:
            ds_ref[hh] = ds_cur[hh]

    def spec(width):
        return pl.BlockSpec((rows, width), lambda n: (n_steps - 1 - n, 0))

    hw, cw = GDN_WIDTH, GDN_HEADS * CHUNK
    return pl.pallas_call(
        body, name=name, grid=(n_steps,),
        in_specs=[spec(hw), spec(hw), spec(hw), spec(hw), spec(cw), spec(128),
                  pl.BlockSpec((cb * GDN_WIDTH, HEAD_DIM), lambda n: (n_steps - 1 - n, 0)), spec(hw)],
        out_specs=[spec(hw), spec(hw), spec(hw), spec(hw), spec(cw), spec(128)],
        out_shape=[jax.ShapeDtypeStruct((T, hw), F32)] * 4
        + [jax.ShapeDtypeStruct((T, cw), F32), jax.ShapeDtypeStruct((T, 128), F32)],
        scratch_shapes=[pltpu.VMEM((GDN_HEADS, HEAD_DIM, HEAD_DIM), F32)],
        compiler_params=_params("arbitrary"),
    )(do, w, qd, kd, p, cd, s_all, vn)


def _gdn_chunk_bwd(qkv, bg, tinv_all, u, w, dvn, dw, dqd, dkd, dp, dcd, name):
    T = qkv.shape[0]
    cb = _tile(T // CHUNK, CHUNKS_PER_STEP)
    rows = cb * CHUNK

    def body(qkv_ref, bg_ref, tinv_ref, u_ref, w_ref, dvn_ref, dw_ref, dqd_ref, dkd_ref, dp_ref, dcd_ref,
             dqkv_ref, dbg_ref):
        masks = _chunk_masks()
        causal, strict, _ = masks
        lane = lax.broadcasted_iota(jnp.int32, (CHUNK, 128), 1)
        row = lax.broadcasted_iota(jnp.int32, (CHUNK, 128), 0)
        heads = []
        for ci in range(cb):
            rs = slice(ci * CHUNK, (ci + 1) * CHUNK)
            bgv = bg_ref[rs, :]
            gc_all = _chunk_cumsum(bgv)
            gc_rows = _row_form(gc_all)
            for hh in range(GDN_HEADS):
                qs, ks, vs = _head_slices(hh)
                q, k = qkv_ref[rs, qs], qkv_ref[rs, ks]
                beta = bgv[:, hh:hh + 1]
                gc = gc_all[:, GDN_HEADS + hh:GDN_HEADS + hh + 1]
                decay, kb, eg = _head_elementwise(k, beta, gc, gc_rows[GDN_HEADS + hh:GDN_HEADS + hh + 1, :], causal)
                heads.append(dict(ci=ci, hh=hh, rs=rs, hs=slice(hh * HEAD_DIM, (hh + 1) * HEAD_DIM),
                                  cs=slice(hh * CHUNK, (hh + 1) * CHUNK), q=q, k=k, beta=beta, gc=gc,
                                  decay=decay, kb=kb, eg=eg))
        for hd in heads:
            hd["t"] = _split(tinv_ref[hd["rs"], hd["cs"]])
        for hd in heads:
            hd["kk"] = _mdot(hd["kb"], hd["k"], NT)
            hd["qk"] = _mdot(hd["q"], hd["k"], NT)
        for hd in heads:
            hd["dvb"] = _dot3(hd["t"], _split(dvn_ref[hd["rs"], hd["hs"]]), TN)
            hd["dkbeg"] = _dot3(hd["t"], _split(dw_ref[hd["rs"], hd["hs"]]), TN)
        for hd in heads:
            rs, hs = hd["rs"], hd["hs"]
            da = -(_mdot(hd["dvb"], u_ref[rs, hs], NT) + _mdot(hd["dkbeg"], w_ref[rs, hs], NT))
            dm = jnp.where(strict, da, 0.0)
            dp_ = dp_ref[rs, hd["cs"]]
            hd["dkk"] = dm * hd["decay"]
            hd["dqk"] = dp_ * hd["decay"]
            hd["e"] = (hd["dkk"] * hd["kk"] + hd["dqk"] * hd["qk"])
        for hd in heads:
            hd["dkb"] = _mdot(hd["dkk"], hd["k"], NN)
            hd["dk"] = _mdot(hd["dkk"], hd["kb"], TN) + _mdot(hd["dqk"], hd["q"], TN)
            hd["dq"] = _mdot(hd["dqk"], hd["k"], NN)
            onehot = (lane == GDN_HEADS + hd["hh"]).astype(jnp.bfloat16)
            e_hi, e_lo = _split(hd["e"])
            hd["col_sums"] = _dot(e_lo, onehot, TN) + _dot(e_hi, onehot, TN)
        tiles = {}
        for hd in heads:
            ci, hh, rs, hs = hd["ci"], hd["hh"], hd["rs"], hd["hs"]
            qs, ks, vs = _head_slices(hh)
            q, k, beta, gc, eg, kb = hd["q"], hd["k"], hd["beta"], hd["gc"], hd["eg"], hd["kb"]
            v = qkv_ref[rs, vs]
            dqd_, dkd_ = dqd_ref[rs, hs], dkd_ref[rs, hs]
            gl = gc[CHUNK - 1:CHUNK, :]
            ek = jnp.exp(gl - gc)
            dkb = hd["dkb"] + hd["dkbeg"] * eg
            deg = jnp.sum(dqd_ * q, axis=1, keepdims=True) + jnp.sum(hd["dkbeg"] * kb, axis=1, keepdims=True)
            dek = jnp.sum(dkd_ * k, axis=1, keepdims=True)
            dcd_ = dcd_ref[ci * CHUNK:ci * CHUNK + 1, GDN_HEADS + hh:GDN_HEADS + hh + 1]
            dgl = jnp.sum(dek * ek, axis=0, keepdims=True) + dcd_ * jnp.exp(gl)
            dgc = jnp.sum(hd["e"], axis=1, keepdims=True) + deg * eg - dek * ek
            dbeta_tile, dgc_tile = tiles.get(ci, (jnp.zeros((CHUNK, 128), F32), jnp.zeros((CHUNK, 128), F32)))
            dgc_tile += jnp.where(lane == GDN_HEADS + hh, dgc, 0.0) - hd["col_sums"]
            dgc_tile += jnp.where((lane == GDN_HEADS + hh) & (row == CHUNK - 1), dgl, 0.0)
            dbeta = jnp.sum(dkb * k, axis=1, keepdims=True) + jnp.sum(hd["dvb"] * v, axis=1, keepdims=True)
            dbeta_tile += jnp.where(lane == hh, dbeta, 0.0)
            tiles[ci] = (dbeta_tile, dgc_tile)
            dqkv_ref[rs, qs] = hd["dq"] + dqd_ * eg
            dqkv_ref[rs, ks] = hd["dk"] + dkd_ * ek + dkb * beta
            dqkv_ref[rs, vs] = hd["dvb"] * beta
        for ci in range(cb):
            dbeta_tile, dgc_tile = tiles[ci]
            dbg_ref[ci * CHUNK:(ci + 1) * CHUNK, :] = dbeta_tile + _chunk_cumsum(dgc_tile, reverse=True)

    def spec(width):
        return pl.BlockSpec((rows, width), lambda n: (n, 0))

    hw, cw = GDN_WIDTH, GDN_HEADS * CHUNK
    return pl.pallas_call(
        body, name=name, grid=(T // rows,),
        in_specs=[spec(QKV_WIDTH), spec(128), spec(cw), spec(hw), spec(hw), spec(hw), spec(hw), spec(hw),
                  spec(hw), spec(cw), spec(128)],
        out_specs=[spec(QKV_WIDTH), spec(128)],
        out_shape=[jax.ShapeDtypeStruct((T, QKV_WIDTH), F32), jax.ShapeDtypeStruct((T, 128), F32)],
        compiler_params=_params("parallel"),
    )(qkv, bg, tinv_all, u, w, dvn, dw, dqd, dkd, dp, dcd)


def _pool_counts(i, tm, rows, offset):
    t = i * tm - offset + lax.broadcasted_iota(jnp.int32, (rows, 1), 0)
    return [jnp.minimum(t + 1, w).astype(F32) for w in POOL_WINDOWS]


def _window_sums(window, forward):
    sums, s, step = [], window, 1
    for _ in POOL_WINDOWS:
        s = s + _shift_rows(s, -step if forward else step)
        sums.append(s)
        step *= 2
    return sums


def _pooled(window, counts):
    sums = _window_sums(window, forward=False)
    out = []
    for gi in range(POOL_GROUPS):
        sl = slice(gi * 128, (gi + 1) * 128)
        out.append(sums[gi][HALO:, sl] / counts[gi] - window[HALO:, sl])
    return out


def _mix_post(o, proj, gdn_norm, pool_w, pool_scale, name):
    T = o.shape[0]
    tm = _tile(T, 256)
    hb = tm // HALO

    def body(o_ref, z_ref, p_ref, ph_ref, gn_ref, pw_ref, ps_ref, out_ref):
        i = pl.program_id(0)
        for hh in range(GDN_HEADS):
            sl = slice(hh * HEAD_DIM, (hh + 1) * HEAD_DIM)
            oh, zh = o_ref[:, sl], z_ref[:, sl]
            ro = lax.rsqrt(jnp.mean(oh * oh, axis=-1, keepdims=True) + EPS)
            out_ref[:, sl] = (((oh * ro) * gn_ref[...]) * (zh * _sigmoid(zh))).astype(out_ref.dtype)
        halo = jnp.where(i == 0, 0.0, ph_ref[...])
        window = jnp.concatenate([halo, p_ref[...]], axis=0)
        pooled = _pooled(window, _pool_counts(i, tm, tm, 0))
        for gi in range(POOL_GROUPS):
            pm = _mdot(pooled[gi], pw_ref[gi], NN)
            out_ref[:, GDN_WIDTH + gi * 128:GDN_WIDTH + (gi + 1) * 128] = (
                pm * ps_ref[:, gi * 128:(gi + 1) * 128]).astype(out_ref.dtype)

    return pl.pallas_call(
        body, name=name, grid=(T // tm,),
        in_specs=[pl.BlockSpec((tm, GDN_WIDTH), lambda i: (i, 0)),
                  pl.BlockSpec((tm, GDN_WIDTH), lambda i: (i, COL_Z // GDN_WIDTH)),
                  pl.BlockSpec((tm, POOL_WIDTH), lambda i: (i, COL_P // POOL_WIDTH)),
                  pl.BlockSpec((HALO, POOL_WIDTH), lambda i: (jnp.maximum(i * hb - 1, 0), COL_P // POOL_WIDTH)),
                  pl.BlockSpec((1, HEAD_DIM), lambda i: (0, 0)),
                  pl.BlockSpec((POOL_GROUPS, 128, 128), lambda i: (0, 0, 0)),
                  pl.BlockSpec((1, POOL_WIDTH), lambda i: (0, 0))],
        out_specs=pl.BlockSpec((tm, GDN_WIDTH + POOL_WIDTH), lambda i: (i, 0)),
        out_shape=jax.ShapeDtypeStruct((T, GDN_WIDTH + POOL_WIDTH), MXU_DTYPE),
        compiler_params=_params("parallel"),
    )(o, proj, proj, proj, gdn_norm, pool_w, pool_scale)


def _mix_post_bwd(dmix, o, proj, gdn_norm, pool_w, pool_scale, name):
    T = o.shape[0]
    tm = _tile(T, 256)
    hb = tm // HALO
    n_tiles = T // tm

    def body(dg_ref, dpo_ref, dpo_next_ref, o_ref, z_ref, p_ref, ph_ref, gn_ref, pw_ref, ps_ref,
             do_ref, dzp_ref, dgn_ref, dpw_ref, dps_ref):
        i = pl.program_id(0)

        @pl.when(i == 0)
        def _():
            dgn_ref[...] = jnp.zeros_like(dgn_ref)
            dpw_ref[...] = jnp.zeros_like(dpw_ref)
            dps_ref[...] = jnp.zeros_like(dps_ref)

        gn = gn_ref[...]
        dgn = jnp.zeros((1, HEAD_DIM), F32)
        for hh in range(GDN_HEADS):
            sl = slice(hh * HEAD_DIM, (hh + 1) * HEAD_DIM)
            oh, zh, dy = o_ref[:, sl], z_ref[:, sl], dg_ref[:, sl]
            ro = lax.rsqrt(jnp.mean(oh * oh, axis=-1, keepdims=True) + EPS)
            on = oh * ro
            sig = _sigmoid(zh)
            sz = zh * sig
            dzp_ref[:, sl] = (dy * (on * gn) * (sig * (1.0 + zh * (1.0 - sig)))).astype(dzp_ref.dtype)
            dgn += jnp.sum(dy * on * sz, axis=0, keepdims=True)
            don = dy * gn * sz
            do_ref[:, sl] = ro * (don - on * jnp.mean(don * on, axis=-1, keepdims=True))
        dgn_ref[...] += dgn

        halo = jnp.where(i == 0, 0.0, ph_ref[...])
        window = jnp.concatenate([halo, p_ref[...]], axis=0)
        counts = _pool_counts(i, tm, tm + HALO, 0)
        pooled = _pooled(window, [cn[:tm] for cn in counts])
        nxt = jnp.where(i == n_tiles - 1, 0.0, dpo_next_ref[...])
        dpo_w = jnp.concatenate([dpo_ref[...], nxt], axis=0)
        ps = ps_ref[...]
        dps = []
        scaled = []
        for gi in range(POOL_GROUPS):
            sl = slice(gi * 128, (gi + 1) * 128)
            dpm = dpo_w[:, sl] * ps[:, sl]
            pm = _mdot(pooled[gi], pw_ref[gi], NN)
            dps.append(jnp.sum(dpo_w[:tm, sl] * pm, axis=0, keepdims=True))
            dpw_ref[gi] += _mdot(pooled[gi], dpm[:tm], TN)
            dpooled = _mdot(dpm, pw_ref[gi], NT)
            scaled.append((dpooled, dpooled / counts[gi]))
        dps_ref[...] += jnp.concatenate(dps, axis=1)
        lead = _window_sums(jnp.concatenate([sc for _, sc in scaled], axis=1), forward=True)
        for gi in range(POOL_GROUPS):
            sl = slice(gi * 128, (gi + 1) * 128)
            dzp_ref[:, GDN_WIDTH + gi * 128:GDN_WIDTH + (gi + 1) * 128] = (
                lead[gi][:tm, sl] - scaled[gi][0][:tm]).astype(dzp_ref.dtype)

    last_halo = T // HALO - 1
    return pl.pallas_call(
        body, name=name, grid=(n_tiles,),
        in_specs=[pl.BlockSpec((tm, GDN_WIDTH), lambda i: (i, 0)),
                  pl.BlockSpec((tm, POOL_WIDTH), lambda i: (i, 1)),
                  pl.BlockSpec((HALO, POOL_WIDTH), lambda i: (jnp.minimum((i + 1) * hb, last_halo), 1)),
                  pl.BlockSpec((tm, GDN_WIDTH), lambda i: (i, 0)),
                  pl.BlockSpec((tm, GDN_WIDTH), lambda i: (i, COL_Z // GDN_WIDTH)),
                  pl.BlockSpec((tm, POOL_WIDTH), lambda i: (i, COL_P // POOL_WIDTH)),
                  pl.BlockSpec((HALO, POOL_WIDTH), lambda i: (jnp.maximum(i * hb - 1, 0), COL_P // POOL_WIDTH)),
                  pl.BlockSpec((1, HEAD_DIM), lambda i: (0, 0)),
                  pl.BlockSpec((POOL_GROUPS, 128, 128), lambda i: (0, 0, 0)),
                  pl.BlockSpec((1, POOL_WIDTH), lambda i: (0, 0))],
        out_specs=[pl.BlockSpec((tm, GDN_WIDTH), lambda i: (i, 0)),
                   pl.BlockSpec((tm, GDN_WIDTH + POOL_WIDTH), lambda i: (i, 0)),
                   pl.BlockSpec((1, HEAD_DIM), lambda i: (0, 0)),
                   pl.BlockSpec((POOL_GROUPS, 128, 128), lambda i: (0, 0, 0)),
                   pl.BlockSpec((1, POOL_WIDTH), lambda i: (0, 0))],
        out_shape=[jax.ShapeDtypeStruct((T, GDN_WIDTH), F32),
                   jax.ShapeDtypeStruct((T, GDN_WIDTH + POOL_WIDTH), MXU_DTYPE),
                   jax.ShapeDtypeStruct((1, HEAD_DIM), F32),
                   jax.ShapeDtypeStruct((POOL_GROUPS, 128, 128), F32),
                   jax.ShapeDtypeStruct((1, POOL_WIDTH), F32)],
        compiler_params=_params("arbitrary"),
    )(dmix, dmix, dmix, o, proj, proj, proj, gdn_norm, pool_w, pool_scale)


def _gdn_prep_bwd(proj, conv_w, a_log_l, dt_bias_l, dqkv, dbg, dzp, name):
    T = proj.shape[0]
    tm = _tile(T, 256)
    hb = tm // 8
    n_tiles = T // tm
    last_halo = T // 8 - 1

    def body(cur_ref, before_ref, after_ref, ba_ref, w_ref, al_ref, dtb_ref, dq_ref, dq_after_ref, dbg_ref,
             dzp_ref, dproj_ref, dw_ref, dal_ref, ddtb_ref):
        i = pl.program_id(0)

        @pl.when(i == 0)
        def _():
            dw_ref[...] = jnp.zeros_like(dw_ref)
            dal_ref[...] = jnp.zeros_like(dal_ref)
            ddtb_ref[...] = jnp.zeros_like(ddtb_ref)

        last = i == n_tiles - 1
        w = w_ref[...]
        before = jnp.where(i == 0, 0.0, before_ref[...])
        after = jnp.where(last, 0.0, after_ref[...])
        window = jnp.concatenate([before, cur_ref[...], after], axis=0)
        y = _conv_act(window, w)
        sig = _sigmoid(y)
        act = y * sig
        dq_w = jnp.concatenate([jnp.zeros((8, QKV_WIDTH), F32), dq_ref[...],
                                jnp.where(last, 0.0, dq_after_ref[...])], axis=0)
        dact = []
        for hh in range(3 * GDN_HEADS):
            sl = slice(hh * HEAD_DIM, (hh + 1) * HEAD_DIM)
            blk, dblk = act[:, sl], dq_w[:, sl]
            if hh < 2 * GDN_HEADS:
                rn = lax.rsqrt(jnp.sum(blk * blk, axis=-1, keepdims=True) + EPS)
                unit = blk * rn
                if hh < GDN_HEADS:
                    dblk = dblk * (HEAD_DIM ** -0.5)
                dblk = rn * (dblk - unit * jnp.sum(dblk * unit, axis=-1, keepdims=True))
            dact.append(dblk)
        dy = jnp.concatenate(dact, axis=1) * (sig * (1.0 + y * (1.0 - sig)))
        dx = dy * w[CONV_K - 1:CONV_K, :]
        dws = [None] * CONV_K
        dws[CONV_K - 1] = jnp.sum(dy[8:8 + tm] * window[8:8 + tm], axis=0, keepdims=True)
        for j in range(CONV_K - 1):
            s = CONV_K - 1 - j
            dx += _shift_rows(dy, -s) * w[j:j + 1, :]
            dws[j] = jnp.sum(dy[8:8 + tm] * _shift_rows(window, s)[8:8 + tm], axis=0, keepdims=True)
        dw_ref[...] += jnp.concatenate(dws, axis=0)
        dproj_ref[:, :QKV_WIDTH] = dx[8:8 + tm].astype(dproj_ref.dtype)
        dproj_ref[:, COL_Z:COL_BA] = dzp_ref[...]

        ba = ba_ref[...]
        dbg_ = dbg_ref[...]
        lane = lax.broadcasted_iota(jnp.int32, ba.shape, 1)
        beta = _sigmoid(ba)
        pre = ba + dtb_ref[...]
        neg_a = -jnp.exp(al_ref[...])
        g = neg_a * _softplus(pre)
        is_g = (lane >= GDN_HEADS) & (lane < 2 * GDN_HEADS)
        da_raw = jnp.where(is_g, dbg_ * neg_a * _sigmoid(pre), 0.0)
        dba = jnp.where(lane < GDN_HEADS, dbg_ * beta * (1.0 - beta), da_raw)
        dproj_ref[:, COL_BA:] = dba.astype(dproj_ref.dtype)
        dal_ref[...] += jnp.sum(jnp.where(is_g, dbg_ * g, 0.0), axis=0, keepdims=True)
        ddtb_ref[...] += jnp.sum(da_raw, axis=0, keepdims=True)

    lane_vec = pl.BlockSpec((1, 128), lambda i: (0, 0))
    return pl.pallas_call(
        body, name=name, grid=(n_tiles,),
        in_specs=[pl.BlockSpec((tm, QKV_WIDTH), lambda i: (i, 0)),
                  pl.BlockSpec((8, QKV_WIDTH), lambda i: (jnp.maximum(i * hb - 1, 0), 0)),
                  pl.BlockSpec((8, QKV_WIDTH), lambda i: (jnp.minimum((i + 1) * hb, last_halo), 0)),
                  pl.BlockSpec((tm, 128), lambda i: (i, COL_BA // 128)),
                  pl.BlockSpec((CONV_K, QKV_WIDTH), lambda i: (0, 0)), lane_vec, lane_vec,
                  pl.BlockSpec((tm, QKV_WIDTH), lambda i: (i, 0)),
                  pl.BlockSpec((8, QKV_WIDTH), lambda i: (jnp.minimum((i + 1) * hb, last_halo), 0)),
                  pl.BlockSpec((tm, 128), lambda i: (i, 0)),
                  pl.BlockSpec((tm, GDN_WIDTH + POOL_WIDTH), lambda i: (i, 0))],
        out_specs=[pl.BlockSpec((tm, D_IN_PAD), lambda i: (i, 0)),
                   pl.BlockSpec((CONV_K, QKV_WIDTH), lambda i: (0, 0)), lane_vec, lane_vec],
        out_shape=[jax.ShapeDtypeStruct((T, D_IN_PAD), MXU_DTYPE),
                   jax.ShapeDtypeStruct((CONV_K, QKV_WIDTH), F32),
                   jax.ShapeDtypeStruct((1, 128), F32), jax.ShapeDtypeStruct((1, 128), F32)],
        compiler_params=_params("arbitrary"),
    )(proj, proj, proj, proj, conv_w, a_log_l, dt_bias_l, dqkv, dqkv, dbg, dzp)


def _mod_part(c_all, w_ada, b_part, name):
    def body(c_ref, w_ref, b_ref, out_ref):
        cc = c_ref[...]
        out_ref[...] = _mdot(cc * _sigmoid(cc), w_ref[...], NN) + b_ref[...]

    return pl.pallas_call(
        body, name=name, out_shape=jax.ShapeDtypeStruct((c_all.shape[0], w_ada.shape[1]), F32),
        compiler_params=_params(),
    )(c_all, w_ada, b_part)


def _w_ada_grad(c_all, dmod_part, name):
    def body(c_ref, d_ref, out_ref):
        cc = c_ref[...]
        out_ref[...] = _mdot(cc * _sigmoid(cc), d_ref[...], TN)

    return pl.pallas_call(
        body, name=name, out_shape=jax.ShapeDtypeStruct((c_all.shape[1], dmod_part.shape[1]), F32),
        compiler_params=_params(),
    )(c_all, dmod_part)


def _sum_parts(parts, name):
    _, R, C = parts.shape
    tr = _tile(R, 256)

    def body(p_ref, out_ref):
        acc = p_ref[0].astype(F32)
        for s in range(1, N_DEV):
            acc += p_ref[s].astype(F32)
        out_ref[...] = acc

    return pl.pallas_call(
        body, name=name, grid=(R // tr,),
        in_specs=[pl.BlockSpec((N_DEV, tr, C), lambda i: (0, i, 0))],
        out_specs=pl.BlockSpec((tr, C), lambda i: (i, 0)),
        out_shape=jax.ShapeDtypeStruct((R, C), F32),
        compiler_params=_params("parallel"),
    )(parts)


def _adamw(w, g, m, v, name):
    R, C = w.shape
    tr = _tile(R, 256)

    def body(w_ref, g_ref, m_ref, v_ref, d_ref, mo_ref, vo_ref):
        gg = g_ref[...]
        mm = ADAM_B1 * m_ref[...] + (1.0 - ADAM_B1) * gg
        vv = ADAM_B2 * v_ref[...] + (1.0 - ADAM_B2) * (gg * gg)
        m_hat = mm / (1.0 - ADAM_B1 ** ADAM_STEP)
        v_hat = vv / (1.0 - ADAM_B2 ** ADAM_STEP)
        d_ref[...] = -ADAM_LR * (m_hat / (jnp.sqrt(v_hat) + ADAM_EPS) + ADAM_WD * w_ref[...])
        mo_ref[...] = mm
        vo_ref[...] = vv

    spec = pl.BlockSpec((tr, C), lambda i: (i, 0))
    return pl.pallas_call(
        body, name=name, grid=(R // tr,),
        in_specs=[spec] * 4, out_specs=[spec] * 3,
        out_shape=[jax.ShapeDtypeStruct((R, C), F32)] * 3,
        compiler_params=_params("parallel"),
    )(w, g, m, v)


def _weight_grad(a, b, name, dep=None):
    return _matmul([(a, b)], TN, WIRE_DTYPE, name, tm=1408, tn=1024, tk=512, dep=dep)


def _rows_of(flat, lanes=1024):
    flat = flat.reshape(-1)
    n = -(-flat.shape[0] // lanes) * lanes
    return jnp.pad(flat, (0, n - flat.shape[0])).reshape(n // lanes, lanes)


def _pad_rows(a, rows):
    return jnp.pad(a, ((0, rows - a.shape[0]), (0, 0)))


def kernel(x, c, w_ada, b_ada, norm_ffn1, ffn1_gate, ffn1_up, ffn1_down, norm_mix, w_in, conv_w, a_log, dt_bias, gdn_norm, pool_w, pool_scale, w_out, norm_ffn2, ffn2_gate, ffn2_up, ffn2_down, final_norm, loss_target, m_w_ada, m_b_ada, m_norm_ffn1, m_ffn1_gate, m_ffn1_up, m_ffn1_down, m_norm_mix, m_w_in, m_conv_w, m_a_log, m_dt_bias, m_gdn_norm, m_pool_w, m_pool_scale, m_w_out, m_norm_ffn2, m_ffn2_gate, m_ffn2_up, m_ffn2_down, m_final_norm, v_w_ada, v_b_ada, v_norm_ffn1, v_ffn1_gate, v_ffn1_up, v_ffn1_down, v_norm_mix, v_w_in, v_conv_w, v_a_log, v_dt_bias, v_gdn_norm, v_pool_w, v_pool_scale, v_w_out, v_norm_ffn2, v_ffn2_gate, v_ffn2_up, v_ffn2_down, v_final_norm):
    T, D = x.shape[1], x.shape[2]
    Fs = ffn1_gate.shape[2]
    Ws = w_in.shape[2]
    Ws_pad = -(-Ws // 16) * 16
    Os = w_out.shape[1]
    Ms = w_ada.shape[2]
    Cs = conv_w.shape[2]
    me = 4 * lax.axis_index("x") + 2 * lax.axis_index("y") + lax.axis_index("c")
    x0, target = x[0], loss_target[0]

    def wire(a):
        return a.astype(WIRE_DTYPE)

    def token(started):
        return started[4][:1, :1]

    def with_own(landed, own):
        return lax.dynamic_update_slice(landed, own[None], (me, 0, 0))

    def full(landed, off, size, keep=None):
        blk = landed[:, off:off + (size if keep is None else keep), :]
        return blk.reshape(-1, D).astype(MXU_DTYPE)

    no_dep = jnp.zeros((8, 128), F32)
    small = jnp.concatenate([_pad_rows(c, 8), _pad_rows(jnp.pad(conv_w[0], ((0, 0), (0, D - Cs))), 8)], axis=0)
    got = _all_gather(small, "gather_small")
    c_all = got[:, 0, :]
    conv_full = jnp.transpose(got[:, 8:8 + CONV_K, :Cs], (1, 0, 2)).reshape(CONV_K, QKV_WIDTH)
    b_part = lax.dynamic_slice(b_ada, (0, me * Ms), (1, Ms))
    mod_parts = _all_gather(_mod_part(c_all, w_ada[0], b_part, "mod_part"), "gather_mod")
    mod_all = jnp.transpose(mod_parts, (1, 0, 2)).reshape(N_DEV, N_MOD * D)
    mod = lax.dynamic_slice(mod_all, (me, 0), (1, N_MOD * D)).reshape(N_MOD, 1, D)
    sh1, sc1, gt1, sh2, sc2, gt2, sh3, sc3, gt3 = [mod[i] for i in range(N_MOD)]

    w1 = jnp.concatenate([wire(ffn1_gate[0].T), wire(ffn1_up[0].T), wire(ffn1_down[0])], axis=0)
    w23 = jnp.concatenate([wire(_pad_rows(w_in[0].T, Ws_pad)), wire(w_out[0]),
                           wire(ffn2_gate[0].T), wire(ffn2_up[0].T), wire(ffn2_down[0])], axis=0)
    off23 = [0, Ws_pad, Ws_pad + Os, Ws_pad + Os + Fs, Ws_pad + Os + 2 * Fs]
    w1_all = _all_gather(w1, "gather_w1", dep=mod_all)
    wg1_t, wu1_t, wd1 = full(w1_all, 0, Fs), full(w1_all, Fs, Fs), full(w1_all, 2 * Fs, Fs)
    w23_sent = _exchange_start(w23, True, w1_all, "w23_start")

    lane_pad = lambda a: jnp.pad(a, ((0, 0), (GDN_HEADS, 128 - 2 * GDN_HEADS)))
    a_log_l, dt_bias_l = lane_pad(a_log), lane_pad(dt_bias)
    pool_w_m = pool_w[0].astype(MXU_DTYPE)

    h1 = _norm_mod(x0, norm_ffn1, sc1 + token(w23_sent), sh1, "norm1")
    y1, g1, u1, a1 = _ffn_fwd(h1, wg1_t, wu1_t, wd1, "ffn1_fwd")
    x1, h2 = _resid_norm_mod(x0, y1, gt1, 0.5, norm_mix, sc2, sh2, "resid_norm2")
    w23_all = with_own(_exchange_wait(w23_sent, h2, True, "w23_wait"), w23)
    w_in_t = full(w23_all, off23[0], Ws_pad, Ws)
    wo = full(w23_all, off23[1], Os)
    wg2_t, wu2_t, wd2 = full(w23_all, off23[2], Fs), full(w23_all, off23[3], Fs), full(w23_all, off23[4], Fs)
    w_in_re = jnp.concatenate([w_in_t[:COL_Z + GDN_WIDTH], w_in_t[D_IN - POOL_WIDTH:],
                               w_in_t[4 * GDN_WIDTH:4 * GDN_WIDTH + 2 * GDN_HEADS],
                               jnp.zeros((128 - 2 * GDN_HEADS, D), MXU_DTYPE)], axis=0)
    proj = _matmul([(h2, w_in_re)], NT, F32, "proj_in", tm=512, tn=D_IN_PAD // 3, tk=D)
    qkv, bg = _gdn_prep(proj, conv_full, a_log_l, dt_bias_l, "gdn_prep")
    tinv, u_c, w_c, qd_c, kd_c, p_c, cd_c = _gdn_chunk_fwd(qkv, bg, "gdn_chunk_fwd")
    o, s_all, vn_c = _gdn_scan_fwd(u_c, w_c, qd_c, kd_c, p_c, cd_c, "gdn_scan_fwd")
    mix_in = _mix_post(o, proj, gdn_norm, pool_w_m, pool_scale, "mix_post")
    mixed = _matmul([(mix_in, wo)], NN, F32, "mix_out", tm=512, tn=D, tk=GDN_WIDTH + POOL_WIDTH)
    x2, h3 = _resid_norm_mod(x1, mixed, gt2, 1.0, norm_ffn2, sc3, sh3, "resid_norm3")
    y3, g3, u3, a3 = _ffn_fwd(h3, wg2_t, wu2_t, wd2, "ffn2_fwd")
    loss_row, d3, d_final = _final_loss(x2, y3, gt3, final_norm.reshape(1, D), target, "final_loss")

    dy3, dgt3 = _gate_bwd(d3, y3, gt3, 0.5, "ffn2_gate_bwd")
    dg3, du3, dh3 = _ffn_bwd(dy3, wd2, g3, u3, wg2_t, wu2_t, "ffn2_bwd")
    d_wd2 = _weight_grad(a3, dy3, "ffn2_dwd")
    d_wg2 = _weight_grad(dg3, h3, "ffn2_dwg")
    d_wu2 = _weight_grad(du3, h3, "ffn2_dwu")
    d2, d_n3, dsc3, dsh3 = _norm_bwd(dh3, x2, norm_ffn2, sc3, d3, "norm3_bwd")
    dmixed, dgt2 = _gate_bwd(d2, mixed, gt2, 1.0, "mix_gate_bwd")
    dmix_in = _matmul([(dmixed, wo)], NT, F32, "mix_out_bwd", tm=512, tn=GDN_WIDTH + POOL_WIDTH, tk=D)
    d_wo = _matmul([(mix_in, dmixed)], TN, WIRE_DTYPE, "mix_dwo", tm=512, tn=D, tk=512)
    do, dzp, d_gn, d_pw, d_ps = _mix_post_bwd(dmix_in, o, proj, gdn_norm, pool_w_m, pool_scale, "mix_post_bwd")
    dvn, dw_c, dqd, dkd, dp_c, dcd = _gdn_scan_bwd(do, w_c, qd_c, kd_c, p_c, cd_c, s_all, vn_c, "gdn_scan_bwd")
    dqkv, dbg = _gdn_chunk_bwd(qkv, bg, tinv, u_c, w_c, dvn, dw_c, dqd, dkd, dp_c, dcd, "gdn_chunk_bwd")
    dproj, d_conv, d_al, d_dtb = _gdn_prep_bwd(proj, conv_full, a_log_l, dt_bias_l, dqkv, dbg, dzp, "gdn_prep_bwd")
    dh2 = _matmul([(dproj, w_in_re)], NN, F32, "proj_in_bwd", tm=512, tn=D, tk=D_IN_PAD // 3)
    d_win_re = _matmul([(dproj, h2)], TN, WIRE_DTYPE, "proj_in_dw", tm=D_IN_PAD // 3, tn=D, tk=512)
    d_win_t = jnp.concatenate([d_win_re[:COL_Z + GDN_WIDTH], d_win_re[COL_BA:COL_BA + 2 * GDN_HEADS],
                               d_win_re[COL_P:COL_P + POOL_WIDTH]], axis=0)
    d_win_blocks = jnp.pad(d_win_t.reshape(N_DEV, Ws, D), ((0, 0), (0, Ws_pad - Ws), (0, 0)))
    parts23 = jnp.concatenate(
        [wire(d_win_blocks), wire(d_wo.reshape(N_DEV, Os, D)), wire(d_wg2.reshape(N_DEV, Fs, D)),
         wire(d_wu2.reshape(N_DEV, Fs, D)), wire(d_wd2.reshape(N_DEV, Fs, D))], axis=1)
    own23 = lax.dynamic_index_in_dim(parts23, me, 0, keepdims=False)
    g23_sent = _exchange_start(parts23, False, no_dep, "g23_start")
    d1, d_n2, dsc2, dsh2 = _norm_bwd(dh2, x1, norm_mix, sc2 + token(g23_sent), d2, "norm2_bwd")
    dy1, dgt1 = _gate_bwd(d1, y1, gt1, 0.5, "ffn1_gate_bwd")
    dg1, du1, dh1 = _ffn_bwd(dy1, wd1, g1, u1, wg1_t, wu1_t, "ffn1_bwd")
    grad_x, d_n1, dsc1, dsh1 = _norm_bwd(dh1, x0, norm_ffn1, sc1, d1, "norm1_bwd")

    dmod = jnp.concatenate([dsh1, dsc1, dgt1, dsh2, dsc2, dgt2, dsh3, dsc3, dgt3], axis=0)
    small_rows = [dmod.reshape(-1), d_n1[0], d_n2[0], d_n3[0], d_final[0], d_gn[0], d_ps[0],
                  d_al[0, GDN_HEADS:2 * GDN_HEADS], d_dtb[0, GDN_HEADS:2 * GDN_HEADS], loss_row[0, :1],
                  d_conv.reshape(-1), d_pw.reshape(-1)]
    lanes = 1024
    small_rows = [_rows_of(r, lanes) for r in small_rows]
    n_rows = [r.shape[0] for r in small_rows]
    row_off = [sum(n_rows[:i]) for i in range(len(n_rows))]
    total = -(-sum(n_rows) // 8) * 8
    slab = _pad_rows(jnp.concatenate(small_rows, axis=0), total)
    slab_all = _all_gather(slab, "gather_small_grads")
    summed = _sum_parts(slab_all, "sum_small_grads")

    def piece(idx, n):
        return summed[row_off[idx]:row_off[idx] + n_rows[idx]].reshape(-1)[:n]

    g_b_ada = piece(0, N_MOD * D).reshape(1, N_MOD * D)
    g_n1, g_n2, g_n3 = piece(1, D).reshape(1, D), piece(2, D).reshape(1, D), piece(3, D).reshape(1, D)
    g_final = piece(4, D)
    g_gn = piece(5, HEAD_DIM).reshape(1, HEAD_DIM)
    g_ps = piece(6, POOL_WIDTH).reshape(1, POOL_WIDTH)
    g_al = piece(7, GDN_HEADS).reshape(1, GDN_HEADS)
    g_dtb = piece(8, GDN_HEADS).reshape(1, GDN_HEADS)
    loss = piece(9, 1)[0]
    g_conv = lax.dynamic_slice(piece(10, CONV_K * QKV_WIDTH).reshape(1, CONV_K, QKV_WIDTH), (0, 0, me * Cs),
                               (1, CONV_K, Cs))
    g_pw = piece(11, POOL_GROUPS * 128 * 128).reshape(1, POOL_GROUPS, 128, 128)

    dmod_all = slab_all[:, row_off[0]:row_off[0] + n_rows[0], :].reshape(N_DEV, -1)[:, :N_MOD * D]
    g_w_ada = _w_ada_grad(c_all, lax.dynamic_slice(dmod_all, (0, me * Ms), (N_DEV, Ms)), "w_ada_grad")[None]

    def send_ffn1(a, b, which, dep):
        parts = _weight_grad(a, b, f"ffn1_{which}", dep=dep).reshape(N_DEV, Fs, D)
        own = lax.dynamic_index_in_dim(parts, me, 0, keepdims=False)
        return _exchange_start(parts, False, no_dep, f"g1_{which}_start"), own

    g1_wg, own_wg = send_ffn1(dg1, h1, "dwg", summed)
    g1_wu, own_wu = send_ffn1(du1, h1, "dwu", g1_wg[4])
    g1_wd, own_wd = send_ffn1(a1, dy1, "dwd", g1_wu[4])

    big23 = _sum_parts(with_own(_exchange_wait(g23_sent, g1_wd[4], False, "g23_wait"), own23), "sum_grads23")
    g_w_in, g_w_out = big23[:Ws].T[None], big23[off23[1]:off23[1] + Os][None]
    g_ffn2_gate, g_ffn2_up = big23[off23[2]:off23[2] + Fs].T[None], big23[off23[3]:off23[3] + Fs].T[None]
    g_ffn2_down = big23[off23[4]:off23[4] + Fs][None]

    names = ["w_ada", "b_ada", "norm_ffn1", "ffn1_gate", "ffn1_up", "ffn1_down", "norm_mix", "w_in", "conv_w",
             "a_log", "dt_bias", "gdn_norm", "pool_w", "pool_scale", "w_out", "norm_ffn2", "ffn2_gate", "ffn2_up",
             "ffn2_down", "final_norm"]
    weights = dict(zip(names, [w_ada, b_ada, norm_ffn1, ffn1_gate, ffn1_up, ffn1_down, norm_mix, w_in, conv_w,
                               a_log, dt_bias, gdn_norm, pool_w, pool_scale, w_out, norm_ffn2, ffn2_gate, ffn2_up,
                               ffn2_down, final_norm]))
    ms = dict(zip(names, [m_w_ada, m_b_ada, m_norm_ffn1, m_ffn1_gate, m_ffn1_up, m_ffn1_down, m_norm_mix, m_w_in,
                          m_conv_w, m_a_log, m_dt_bias, m_gdn_norm, m_pool_w, m_pool_scale, m_w_out, m_norm_ffn2,
                          m_ffn2_gate, m_ffn2_up, m_ffn2_down, m_final_norm]))
    vs = dict(zip(names, [v_w_ada, v_b_ada, v_norm_ffn1, v_ffn1_gate, v_ffn1_up, v_ffn1_down, v_norm_mix, v_w_in,
                          v_conv_w, v_a_log, v_dt_bias, v_gdn_norm, v_pool_w, v_pool_scale, v_w_out, v_norm_ffn2,
                          v_ffn2_gate, v_ffn2_up, v_ffn2_down, v_final_norm]))
    grads = dict(w_ada=g_w_ada, b_ada=g_b_ada, norm_ffn1=g_n1, norm_mix=g_n2, w_in=g_w_in, conv_w=g_conv,
                 a_log=g_al, dt_bias=g_dtb, gdn_norm=g_gn, pool_w=g_pw, pool_scale=g_ps, w_out=g_w_out,
                 norm_ffn2=g_n3, ffn2_gate=g_ffn2_gate, ffn2_up=g_ffn2_up, ffn2_down=g_ffn2_down, final_norm=g_final)
    delta, new_m, new_v = {}, {}, {}

    def adamw_big(n):
        shp = weights[n].shape
        two_d = lambda a: a.reshape(shp[-2], shp[-1])
        d_, m_, v_ = _adamw(two_d(weights[n]), two_d(grads[n]), two_d(ms[n]), two_d(vs[n]), f"adamw_{n}")
        delta[n], new_m[n], new_v[n] = d_.reshape(shp), m_.reshape(shp), v_.reshape(shp)

    early = ["w_ada", "w_in", "w_out", "ffn2_gate", "ffn2_up", "ffn2_down"]
    late = ["ffn1_gate", "ffn1_up", "ffn1_down"]
    for n in early:
        adamw_big(n)
    done = sum(delta[n].reshape(-1)[:1] for n in early).reshape(1, 1)

    def arrived(started, own, which):
        return _sum_parts(with_own(_exchange_wait(started, done, False, f"g1_{which}_wait"), own), f"sum_{which}")

    grads["ffn1_gate"] = arrived(g1_wg, own_wg, "dwg").T[None]
    grads["ffn1_up"] = arrived(g1_wu, own_wu, "dwu").T[None]
    grads["ffn1_down"] = arrived(g1_wd, own_wd, "dwd")[None]
    for n in late:
        adamw_big(n)
    small_names = [n for n in names if n not in early + late]
    pack = lambda src: jnp.concatenate([_rows_of(src[n]) for n in small_names], axis=0)
    p_rows = [_rows_of(weights[n]).shape[0] for n in small_names]
    p_total = -(-sum(p_rows) // 8) * 8
    packed = [_pad_rows(pack(src), p_total) for src in (weights, grads, ms, vs)]
    d_s, m_s, v_s = _adamw(*packed, "adamw_small")
    off = 0
    for n, r in zip(small_names, p_rows):
        shp = weights[n].shape
        size = weights[n].size
        for dst, src in ((delta, d_s), (new_m, m_s), (new_v, v_s)):
            dst[n] = src[off:off + r].reshape(-1)[:size].reshape(shp)
        off += r

    return (loss, grad_x[None], *[grads[n] for n in names], *[delta[n] for n in names],
            *[new_m[n] for n in names], *[new_v[n] for n in names])
```

```python
import functools

import jax
import jax.numpy as jnp
from jax import lax
from jax.experimental import pallas as pl
from jax.experimental.pallas import tpu as pltpu

F32 = jnp.float32
MXU_DTYPE = jnp.bfloat16
WIRE_DTYPE = jnp.bfloat16
EPS = 1e-6
N_DEV = 8
GDN_HEADS = 4
HEAD_DIM = 128
GDN_WIDTH = GDN_HEADS * HEAD_DIM
POOL_WINDOWS = (2, 4, 8, 16)
POOL_GROUPS = len(POOL_WINDOWS)
POOL_WIDTH = 512
CONV_K = 4
CHUNK = 64
QKV_WIDTH = 3 * GDN_WIDTH
D_IN = 4 * GDN_WIDTH + 2 * GDN_HEADS + POOL_WIDTH
D_IN_PAD = 4 * GDN_WIDTH + POOL_WIDTH + 128
COL_Z = QKV_WIDTH
COL_P = 4 * GDN_WIDTH
COL_BA = 4 * GDN_WIDTH + POOL_WIDTH
N_MOD = 9
HALO = 16
VMEM_LIMIT = 56 * 1024 * 1024
ADAM_LR, ADAM_B1, ADAM_B2, ADAM_EPS, ADAM_WD, ADAM_STEP = 0.001, 0.9, 0.999, 1e-08, 0.01, 10
FFN_TOKEN_TILE = 512
FFN_HIDDEN_TILE = 1408
CHUNKS_PER_STEP = 4
SCAN_CHUNKS_PER_STEP = 4

NT = (((1,), (1,)), ((), ()))
NN = (((1,), (0,)), ((), ()))
TN = (((0,), (0,)), ((), ()))


def _params(*sem):
    return pltpu.CompilerParams(dimension_semantics=tuple(sem), vmem_limit_bytes=VMEM_LIMIT)


def _dot(a, b, dims):
    return lax.dot_general(a, b, dims, preferred_element_type=F32)


def _mdot(a, b, dims):
    return _dot(a.astype(MXU_DTYPE), b.astype(MXU_DTYPE), dims)


def _split(a):
    hi = a.astype(jnp.bfloat16)
    return hi, (a - hi.astype(F32)).astype(jnp.bfloat16)


def _dot3(a, b, dims):
    (ah, al), (bh, bl) = a, b
    return (_dot(al, bh, dims) + _dot(ah, bl, dims)) + _dot(ah, bh, dims)


def _sigmoid(v):
    return 1.0 / (1.0 + jnp.exp(-v))


def _softplus(v):
    return jnp.maximum(v, 0.0) + jnp.log(1.0 + jnp.exp(-jnp.abs(v)))


def _shift_rows(v, s):
    n = v.shape[0]
    s = s % n
    return v if s == 0 else pltpu.roll(v, s, 0)


def _tile(n, want):
    t = min(n, want)
    while n % t:
        t //= 2
    return t


def _all_gather(block, name, dep=None):
    shape, dtype = block.shape, block.dtype

    def body(x_ref, *refs):
        out_ref, send_sems, recv_sems, local_sem = refs[-4:]
        x, y, c = lax.axis_index("x"), lax.axis_index("y"), lax.axis_index("c")
        me, sibling = (x, y, c), (x, y, 1 - c)
        chips = [(1 - x, y), (x, 1 - y), (1 - x, 1 - y)]

        def rows(px, py, pc):
            return out_ref.at[4 * px + 2 * py + pc]

        def copy(k, blk, to, src=None):
            return pltpu.make_async_remote_copy(
                src_ref=rows(*blk) if src is None else src, dst_ref=rows(*blk),
                send_sem=send_sems.at[k], recv_sem=recv_sems.at[k],
                device_id=to, device_id_type=pl.DeviceIdType.MESH)

        mine = pltpu.make_async_copy(x_ref, rows(*me), local_sem)
        mine.start()
        first = [copy(0, me, sibling, src=x_ref)]
        first += [copy(1 + j, me, (*chip, c), src=x_ref) for j, chip in enumerate(chips)]
        for cp in first:
            cp.start()
        passed = [copy(4 + j, (*chip, c), sibling) for j, chip in enumerate(chips)]
        for j, chip in enumerate(chips):
            copy(1 + j, (*chip, c), me).wait_recv()
            passed[j].start()
        copy(0, sibling, me).wait_recv()
        for j, chip in enumerate(chips):
            copy(4 + j, (*chip, 1 - c), me).wait_recv()
        for cp in first + passed:
            cp.wait_send()
        mine.wait()

    return pl.pallas_call(
        body, name=name,
        out_shape=jax.ShapeDtypeStruct((N_DEV,) + shape, dtype),
        in_specs=[pl.BlockSpec(memory_space=pltpu.HBM)] + [pl.BlockSpec(memory_space=pl.ANY)] * (dep is not None),
        out_specs=pl.BlockSpec(memory_space=pltpu.HBM),
        scratch_shapes=[pltpu.SemaphoreType.DMA((7,)), pltpu.SemaphoreType.DMA((7,)),
                        pltpu.SemaphoreType.DMA(())],
    )(*([block] if dep is None else [block, dep]))


_HBM = pl.BlockSpec(memory_space=pltpu.HBM)
_SEM = pl.BlockSpec(memory_space=pltpu.SEMAPHORE)
_ANY = pl.BlockSpec(memory_space=pl.ANY)
_EFFECT = pltpu.SideEffectType.DATAFLOW_SIDE_EFFECTING
_FLIPS = [(0, 0, 1), (0, 1, 0), (0, 1, 1), (1, 0, 0), (1, 0, 1), (1, 1, 0), (1, 1, 1)]


def _peers():
    x, y, c = lax.axis_index("x"), lax.axis_index("y"), lax.axis_index("c")
    return 4 * x + 2 * y + c, [(1 - x if fx else x, 1 - y if fy else y, 1 - c if fc else c)
                               for fx, fy, fc in _FLIPS]


def _exchange_start(src, gather, dep, name):
    block = src.shape if gather else src.shape[1:]
    land = (N_DEV,) + tuple(block)

    def body(src_ref, land_ref, dep_ref, send_sems, recv_sems, src_thru, land_thru, token):
        me, peers = _peers()
        for k, (px, py, pc) in enumerate(peers):
            pltpu.make_async_remote_copy(
                src_ref=src_ref if gather else src_ref.at[4 * px + 2 * py + pc], dst_ref=land_ref.at[me],
                send_sem=send_sems.at[k], recv_sem=recv_sems.at[k],
                device_id=(px, py, pc), device_id_type=pl.DeviceIdType.MESH).start()
        token[...] = jnp.zeros_like(token)

    return pl.pallas_call(
        body, name=name,
        out_shape=(pltpu.SemaphoreType.DMA((7,)), pltpu.SemaphoreType.DMA((7,)),
                   pltpu.HBM(src.shape, src.dtype), pltpu.HBM(land, src.dtype),
                   jax.ShapeDtypeStruct((8, 128), F32)),
        in_specs=(_HBM, _HBM, _ANY),
        out_specs=(_SEM, _SEM, _HBM, _HBM, pl.BlockSpec(memory_space=pltpu.VMEM)),
        input_output_aliases={0: 2, 1: 3},
        compiler_params=pltpu.CompilerParams(has_side_effects=_EFFECT),
    )(pltpu.with_memory_space_constraint(src, pltpu.HBM),
      pltpu.with_memory_space_constraint(lax.empty(land, src.dtype), pltpu.HBM), dep)


def _exchange_wait(started, after, gather, name):
    send_sems, recv_sems, src_thru, land_thru, _ = started

    def body(src_ref, land_ref, send_sems, recv_sems, after_ref, src_dead, got_ref):
        _, peers = _peers()
        for k, peer in enumerate(peers):
            copy = pltpu.make_async_remote_copy(
                src_ref=src_ref if gather else src_ref.at[0], dst_ref=land_ref.at[0],
                send_sem=send_sems.at[k], recv_sem=recv_sems.at[k],
                device_id=peer, device_id_type=pl.DeviceIdType.MESH)
            copy.wait_send()
            copy.wait_recv()

    return pl.pallas_call(
        body, name=name,
        out_shape=(pltpu.HBM(src_thru.shape, src_thru.dtype), pltpu.HBM(land_thru.shape, land_thru.dtype)),
        in_specs=(_HBM, _HBM, _SEM, _SEM, _ANY), out_specs=(_HBM, _HBM),
        input_output_aliases={0: 0, 1: 1},
        compiler_params=pltpu.CompilerParams(has_side_effects=_EFFECT),
    )(src_thru, land_thru, send_sems, recv_sems, after)[1]


def _matmul(pairs, dims, out_dtype, name, tm=512, tn=512, tk=512, dep=None):
    a0, b0 = pairs[0]
    if dims == TN:
        K, M = a0.shape
    else:
        M, K = a0.shape
    N = b0.shape[0] if dims == NT else b0.shape[1]
    tm, tn, tk = _tile(M, tm), _tile(N, tn), _tile(K, tk)
    nk = K // tk
    n_pairs = len(pairs)
    n_in = 2 * n_pairs + (dep is not None)

    def body(*refs):
        out_ref = refs[n_in]

        def product():
            total = _dot(refs[0][...], refs[1][...], dims)
            for p in range(1, n_pairs):
                total += _dot(refs[2 * p][...], refs[2 * p + 1][...], dims)
            return total

        if nk == 1:
            out_ref[...] = product().astype(out_ref.dtype)
            return
        acc_ref = refs[n_in + 1]
        k = pl.program_id(2)

        @pl.when(k == 0)
        def _():
            acc_ref[...] = product()

        @pl.when((k > 0) & (k < nk - 1))
        def _():
            acc_ref[...] += product()

        @pl.when(k == nk - 1)
        def _():
            out_ref[...] = (acc_ref[...] + product()).astype(out_ref.dtype)

    if dims == TN:
        a_spec = pl.BlockSpec((tk, tm), lambda i, j, k: (k, i))
    else:
        a_spec = pl.BlockSpec((tm, tk), lambda i, j, k: (i, k))
    if dims == NT:
        b_spec = pl.BlockSpec((tn, tk), lambda i, j, k: (j, k))
    else:
        b_spec = pl.BlockSpec((tk, tn), lambda i, j, k: (k, j))
    args, specs = [], []
    for a, b in pairs:
        args += [a, b]
        specs += [a_spec, b_spec]
    if dep is not None:
        args.append(dep)
        specs.append(_ANY)
    return pl.pallas_call(
        body, name=name, grid=(M // tm, N // tn, nk),
        in_specs=specs, out_specs=pl.BlockSpec((tm, tn), lambda i, j, k: (i, j)),
        out_shape=jax.ShapeDtypeStruct((M, N), out_dtype),
        scratch_shapes=[pltpu.VMEM((tm, tn), F32)] * (nk > 1),
        compiler_params=_params("parallel", "parallel", "arbitrary"),
    )(*args)


def _vec_spec(d):
    return pl.BlockSpec((1, d), lambda i: (0, 0))


def _norm_mod(x, nw, scale, shift, name):
    T, D = x.shape
    tm = _tile(T, 512)

    def body(x_ref, nw_ref, sc_ref, sh_ref, h_ref):
        xf = x_ref[...]
        r = lax.rsqrt(jnp.mean(xf * xf, axis=-1, keepdims=True) + EPS)
        h_ref[...] = ((xf * r) * nw_ref[...] * (1.0 + sc_ref[...]) + sh_ref[...]).astype(h_ref.dtype)

    row = pl.BlockSpec((tm, D), lambda i: (i, 0))
    return pl.pallas_call(
        body, name=name, grid=(T // tm,),
        in_specs=[row, _vec_spec(D), _vec_spec(D), _vec_spec(D)], out_specs=row,
        out_shape=jax.ShapeDtypeStruct((T, D), MXU_DTYPE),
        compiler_params=_params("parallel"),
    )(x, nw, scale, shift)


def _resid_norm_mod(x, y, gate, coef, nw, scale, shift, name):
    T, D = x.shape
    tm = _tile(T, 512)

    def body(x_ref, y_ref, g_ref, nw_ref, sc_ref, sh_ref, xo_ref, h_ref):
        xf = x_ref[...] + (coef * g_ref[...]) * y_ref[...]
        xo_ref[...] = xf
        r = lax.rsqrt(jnp.mean(xf * xf, axis=-1, keepdims=True) + EPS)
        h_ref[...] = ((xf * r) * nw_ref[...] * (1.0 + sc_ref[...]) + sh_ref[...]).astype(h_ref.dtype)

    row = pl.BlockSpec((tm, D), lambda i: (i, 0))
    return pl.pallas_call(
        body, name=name, grid=(T // tm,),
        in_specs=[row, row, _vec_spec(D), _vec_spec(D), _vec_spec(D), _vec_spec(D)],
        out_specs=[row, row],
        out_shape=[jax.ShapeDtypeStruct((T, D), F32), jax.ShapeDtypeStruct((T, D), MXU_DTYPE)],
        compiler_params=_params("parallel"),
    )(x, y, gate, nw, scale, shift)


def _gate_bwd(dxo, y, gate, coef, name):
    T, D = dxo.shape
    tm = _tile(T, 512)

    def body(d_ref, y_ref, g_ref, dy_ref, dg_ref):
        @pl.when(pl.program_id(0) == 0)
        def _():
            dg_ref[...] = jnp.zeros_like(dg_ref)

        d = d_ref[...]
        dy_ref[...] = ((coef * g_ref[...]) * d).astype(dy_ref.dtype)
        dg_ref[...] += coef * jnp.sum(d * y_ref[...], axis=0, keepdims=True)

    row = pl.BlockSpec((tm, D), lambda i: (i, 0))
    return pl.pallas_call(
        body, name=name, grid=(T // tm,),
        in_specs=[row, row, _vec_spec(D)], out_specs=[row, _vec_spec(D)],
        out_shape=[jax.ShapeDtypeStruct((T, D), MXU_DTYPE), jax.ShapeDtypeStruct((1, D), F32)],
        compiler_params=_params("arbitrary"),
    )(dxo, y, gate)


def _norm_bwd(dh, x, nw, scale, dres, name):
    T, D = x.shape
    tm = _tile(T, 512)

    def body(dh_ref, x_ref, nw_ref, sc_ref, dr_ref, dx_ref, dnw_ref, dsc_ref, dsh_ref):
        @pl.when(pl.program_id(0) == 0)
        def _():
            dnw_ref[...] = jnp.zeros_like(dnw_ref)
            dsc_ref[...] = jnp.zeros_like(dsc_ref)
            dsh_ref[...] = jnp.zeros_like(dsh_ref)

        xf, dh_ = x_ref[...], dh_ref[...]
        r = lax.rsqrt(jnp.mean(xf * xf, axis=-1, keepdims=True) + EPS)
        xn = xf * r
        one_sc = 1.0 + sc_ref[...]
        dsh_ref[...] += jnp.sum(dh_, axis=0, keepdims=True)
        t = dh_ * xn
        dsc_ref[...] += jnp.sum(t, axis=0, keepdims=True) * nw_ref[...]
        dnw_ref[...] += jnp.sum(t, axis=0, keepdims=True) * one_sc
        dxn = dh_ * (nw_ref[...] * one_sc)
        dx_ref[...] = dr_ref[...] + r * (dxn - xn * jnp.mean(dxn * xn, axis=-1, keepdims=True))

    row = pl.BlockSpec((tm, D), lambda i: (i, 0))
    vec = _vec_spec(D)
    return pl.pallas_call(
        body, name=name, grid=(T // tm,),
        in_specs=[row, row, vec, vec, row], out_specs=[row, vec, vec, vec],
        out_shape=[jax.ShapeDtypeStruct((T, D), F32)] + [jax.ShapeDtypeStruct((1, D), F32)] * 3,
        compiler_params=_params("arbitrary"),
    )(dh, x, nw, scale, dres)


def _final_loss(x, y, gate, fw, target, name):
    T, D = x.shape
    tm = _tile(T, 512)

    def body(x_ref, y_ref, g_ref, fw_ref, t_ref, loss_ref, dx_ref, dfw_ref):
        @pl.when(pl.program_id(0) == 0)
        def _():
            loss_ref[...] = jnp.zeros_like(loss_ref)
            dfw_ref[...] = jnp.zeros_like(dfw_ref)

        xf = x_ref[...] + (0.5 * g_ref[...]) * y_ref[...]
        r = lax.rsqrt(jnp.mean(xf * xf, axis=-1, keepdims=True) + EPS)
        xn = xf * r
        err = xn * fw_ref[...] - t_ref[...]
        per_tok = jnp.mean(err * err, axis=-1, keepdims=True)
        loss_ref[...] += 0.5 * jnp.sum(per_tok, axis=0, keepdims=True)
        dy = err * (1.0 / D)
        dfw_ref[...] += jnp.sum(dy * xn, axis=0, keepdims=True)
        dxn = dy * fw_ref[...]
        dx_ref[...] = r * (dxn - xn * jnp.mean(dxn * xn, axis=-1, keepdims=True))

    row = pl.BlockSpec((tm, D), lambda i: (i, 0))
    vec = _vec_spec(D)
    return pl.pallas_call(
        body, name=name, grid=(T // tm,),
        in_specs=[row, row, vec, vec, row],
        out_specs=[pl.BlockSpec((1, 128), lambda i: (0, 0)), row, vec],
        out_shape=[jax.ShapeDtypeStruct((1, 128), F32), jax.ShapeDtypeStruct((T, D), F32),
                   jax.ShapeDtypeStruct((1, D), F32)],
        compiler_params=_params("arbitrary"),
    )(x, y, gate, fw, target)


def _ffn_fwd(h, wg_t, wu_t, wd, name):
    T, D = h.shape
    Fdim = wd.shape[0]
    tm, tf = _tile(T, FFN_TOKEN_TILE), _tile(Fdim, FFN_HIDDEN_TILE)
    nf = Fdim // tf

    def body(h_ref, wg_ref, wu_ref, wd_ref, y_ref, g_ref, u_ref, a_ref, acc_ref):
        k = pl.program_id(1)

        @pl.when(k == 0)
        def _():
            acc_ref[...] = jnp.zeros_like(acc_ref)

        hh = h_ref[...]
        g = _dot(hh, wg_ref[...], NT)
        u = _dot(hh, wu_ref[...], NT)
        a = ((g * _sigmoid(g)) * u).astype(a_ref.dtype)
        g_ref[...] = g
        u_ref[...] = u
        a_ref[...] = a
        acc_ref[...] += _dot(a, wd_ref[...], NN)

        @pl.when(k == nf - 1)
        def _():
            y_ref[...] = acc_ref[...]

    hrow = pl.BlockSpec((tm, D), lambda i, k: (i, 0))
    wspec = pl.BlockSpec((tf, D), lambda i, k: (k, 0))
    fspec = pl.BlockSpec((tm, tf), lambda i, k: (i, k))
    return pl.pallas_call(
        body, name=name, grid=(T // tm, nf),
        in_specs=[hrow, wspec, wspec, wspec], out_specs=[hrow, fspec, fspec, fspec],
        out_shape=[jax.ShapeDtypeStruct((T, D), F32), jax.ShapeDtypeStruct((T, Fdim), F32),
                   jax.ShapeDtypeStruct((T, Fdim), F32), jax.ShapeDtypeStruct((T, Fdim), MXU_DTYPE)],
        scratch_shapes=[pltpu.VMEM((tm, D), F32)],
        compiler_params=_params("parallel", "arbitrary"),
    )(h, wg_t, wu_t, wd)


def _ffn_bwd(dy, wd, g, u, wg_t, wu_t, name):
    T, D = dy.shape
    Fdim = wd.shape[0]
    tm, tf = _tile(T, FFN_TOKEN_TILE), _tile(Fdim, FFN_HIDDEN_TILE)
    nf = Fdim // tf

    def body(dy_ref, wd_ref, g_ref, u_ref, wg_ref, wu_ref, dg_ref, du_ref, dh_ref, acc_ref):
        k = pl.program_id(1)

        @pl.when(k == 0)
        def _():
            acc_ref[...] = jnp.zeros_like(acc_ref)

        da = _dot(dy_ref[...], wd_ref[...], NT)
        gg = g_ref[...]
        sig = _sigmoid(gg)
        dg = (da * u_ref[...] * (sig * (1.0 + gg * (1.0 - sig)))).astype(dg_ref.dtype)
        du = (da * (gg * sig)).astype(du_ref.dtype)
        dg_ref[...] = dg
        du_ref[...] = du
        acc_ref[...] += _dot(dg, wg_ref[...], NN) + _dot(du, wu_ref[...], NN)

        @pl.when(k == nf - 1)
        def _():
            dh_ref[...] = acc_ref[...]

    row = pl.BlockSpec((tm, D), lambda i, k: (i, 0))
    wspec = pl.BlockSpec((tf, D), lambda i, k: (k, 0))
    fspec = pl.BlockSpec((tm, tf), lambda i, k: (i, k))
    return pl.pallas_call(
        body, name=name, grid=(T // tm, nf),
        in_specs=[row, wspec, fspec, fspec, wspec, wspec],
        out_specs=[fspec, fspec, row],
        out_shape=[jax.ShapeDtypeStruct((T, Fdim), MXU_DTYPE)] * 2 + [jax.ShapeDtypeStruct((T, D), F32)],
        scratch_shapes=[pltpu.VMEM((tm, D), F32)],
        compiler_params=_params("parallel", "arbitrary"),
    )(dy, wd, g, u, wg_t, wu_t)


def _conv_act(window, w):
    y = window * w[CONV_K - 1:CONV_K, :]
    for j in range(CONV_K - 1):
        y += _shift_rows(window, CONV_K - 1 - j) * w[j:j + 1, :]
    return y


def _gdn_prep(proj, conv_w, a_log_l, dt_bias_l, name):
    T = proj.shape[0]
    tm = _tile(T, 256)
    hb = tm // 8

    def body(cur_ref, halo_ref, ba_ref, w_ref, al_ref, dtb_ref, qkv_ref, bg_ref):
        i = pl.program_id(0)
        halo = jnp.where(i == 0, 0.0, halo_ref[...])
        window = jnp.concatenate([halo, cur_ref[...]], axis=0)
        y = _conv_act(window, w_ref[...])[8:, :]
        act = y * _sigmoid(y)
        for hh in range(3 * GDN_HEADS):
            blk = act[:, hh * HEAD_DIM:(hh + 1) * HEAD_DIM]
            if hh < 2 * GDN_HEADS:
                rn = lax.rsqrt(jnp.sum(blk * blk, axis=-1, keepdims=True) + EPS)
                blk = blk * rn
                if hh < GDN_HEADS:
                    blk = blk * (HEAD_DIM ** -0.5)
            qkv_ref[:, hh * HEAD_DIM:(hh + 1) * HEAD_DIM] = blk
        ba = ba_ref[...]
        lane = lax.broadcasted_iota(jnp.int32, ba.shape, 1)
        beta = _sigmoid(ba)
        g = -jnp.exp(al_ref[...]) * _softplus(ba + dtb_ref[...])
        bg_ref[...] = jnp.where(lane < GDN_HEADS, beta, jnp.where(lane < 2 * GDN_HEADS, g, 0.0))

    return pl.pallas_call(
        body, name=name, grid=(T // tm,),
        in_specs=[pl.BlockSpec((tm, QKV_WIDTH), lambda i: (i, 0)),
                  pl.BlockSpec((8, QKV_WIDTH), lambda i: (jnp.maximum(i * hb - 1, 0), 0)),
                  pl.BlockSpec((tm, 128), lambda i: (i, COL_BA // 128)),
                  pl.BlockSpec((CONV_K, QKV_WIDTH), lambda i: (0, 0)),
                  pl.BlockSpec((1, 128), lambda i: (0, 0)), pl.BlockSpec((1, 128), lambda i: (0, 0))],
        out_specs=[pl.BlockSpec((tm, QKV_WIDTH), lambda i: (i, 0)), pl.BlockSpec((tm, 128), lambda i: (i, 0))],
        out_shape=[jax.ShapeDtypeStruct((T, QKV_WIDTH), F32), jax.ShapeDtypeStruct((T, 128), F32)],
        compiler_params=_params("parallel"),
    )(proj, proj, proj, conv_w, a_log_l, dt_bias_l)


def _chunk_cumsum(v, reverse=False):
    row = lax.broadcasted_iota(jnp.int32, v.shape, 0)
    s = 1
    while s < CHUNK:
        if reverse:
            v = v + jnp.where(row < CHUNK - s, _shift_rows(v, -s), 0.0)
        else:
            v = v + jnp.where(row >= s, _shift_rows(v, s), 0.0)
        s *= 2
    return v


def _row_form(cols):
    padded = jnp.concatenate([cols, jnp.zeros((128 - CHUNK, 128), F32)], axis=0)
    return padded.T[:, :CHUNK]


def _chunk_masks():
    ri = lax.broadcasted_iota(jnp.int32, (CHUNK, CHUNK), 0)
    ci = lax.broadcasted_iota(jnp.int32, (CHUNK, CHUNK), 1)
    return ri >= ci, ri > ci, (ri == ci).astype(F32)


def _unit_lower_inverses(ms, eye):
    rs = [eye - m for m in ms]
    ps = [_split(-m) for m in ms]
    s = 2
    while s < CHUNK:
        ps = [_split(_dot3(p, p, NN)) for p in ps]
        r_parts = [_split(r) for r in rs]
        rs = [r + _dot3(p, rp, NN) for r, p, rp in zip(rs, ps, r_parts)]
        s *= 2
    return rs


def _head_elementwise(k, beta, gc, gcr, causal):
    decay = jnp.where(causal, jnp.exp(jnp.where(causal, gc - gcr, 0.0)), 0.0)
    return decay, k * beta, jnp.exp(gc)


def _head_slices(hh):
    return (slice(hh * HEAD_DIM, (hh + 1) * HEAD_DIM),
            slice(GDN_WIDTH + hh * HEAD_DIM, GDN_WIDTH + (hh + 1) * HEAD_DIM),
            slice(2 * GDN_WIDTH + hh * HEAD_DIM, 2 * GDN_WIDTH + (hh + 1) * HEAD_DIM))


def _gdn_chunk_fwd(qkv, bg, name):
    T = qkv.shape[0]
    cb = _tile(T // CHUNK, CHUNKS_PER_STEP)
    rows = cb * CHUNK

    def body(qkv_ref, bg_ref, tinv_ref, u_ref, w_ref, qd_ref, kd_ref, p_ref, cd_ref):
        masks = _chunk_masks()
        causal, strict, eye = masks
        heads = []
        for ci in range(cb):
            rs = slice(ci * CHUNK, (ci + 1) * CHUNK)
            bgv = bg_ref[rs, :]
            gc_all = _chunk_cumsum(bgv)
            gc_rows = _row_form(gc_all)
            cd_ref[rs, :] = jnp.exp(jnp.broadcast_to(gc_all[CHUNK - 1:CHUNK, :], (CHUNK, 128)))
            for hh in range(GDN_HEADS):
                qs, ks, vs = _head_slices(hh)
                q, k, v = qkv_ref[rs, qs], qkv_ref[rs, ks], qkv_ref[rs, vs]
                beta = bgv[:, hh:hh + 1]
                gc = gc_all[:, GDN_HEADS + hh:GDN_HEADS + hh + 1]
                decay, kb, eg = _head_elementwise(k, beta, gc, gc_rows[GDN_HEADS + hh:GDN_HEADS + hh + 1, :], causal)
                hs = slice(hh * HEAD_DIM, (hh + 1) * HEAD_DIM)
                cs = slice(hh * CHUNK, (hh + 1) * CHUNK)
                qd_ref[rs, hs] = (q * eg).astype(qd_ref.dtype)
                kd_ref[rs, hs] = (k * jnp.exp(gc[CHUNK - 1:CHUNK, :] - gc)).astype(kd_ref.dtype)
                heads.append((rs, hs, cs, q, k, v * beta, kb, kb * eg, decay))
        kks = [_mdot(kb, k, NT) for (_, _, _, _, k, _, kb, _, _) in heads]
        qks = [_mdot(q, k, NT) for (_, _, _, q, k, _, _, _, _) in heads]
        tinvs = _unit_lower_inverses([jnp.where(strict, kk * hd[8], 0.0) for kk, hd in zip(kks, heads)], eye)
        t_parts = [_split(t) for t in tinvs]
        us = [_dot3(tp, _split(hd[5]), NN) for tp, hd in zip(t_parts, heads)]
        ws = [_dot3(tp, _split(hd[7]), NN) for tp, hd in zip(t_parts, heads)]
        for hd, tinv, u, w, qk in zip(heads, tinvs, us, ws, qks):
            rs, hs, cs = hd[0], hd[1], hd[2]
            tinv_ref[rs, cs] = tinv
            u_ref[rs, hs] = u
            w_ref[rs, hs] = w.astype(w_ref.dtype)
            p_ref[rs, cs] = jnp.where(causal, qk * hd[8], 0.0).astype(p_ref.dtype)

    def spec(width):
        return pl.BlockSpec((rows, width), lambda n: (n, 0))

    hw, cw = GDN_WIDTH, GDN_HEADS * CHUNK
    return pl.pallas_call(
        body, name=name, grid=(T // rows,),
        in_specs=[spec(QKV_WIDTH), spec(128)],
        out_specs=[spec(cw), spec(hw), spec(hw), spec(hw), spec(hw), spec(cw), spec(128)],
        out_shape=[jax.ShapeDtypeStruct((T, cw), F32), jax.ShapeDtypeStruct((T, hw), F32),
                   jax.ShapeDtypeStruct((T, hw), MXU_DTYPE), jax.ShapeDtypeStruct((T, hw), MXU_DTYPE),
                   jax.ShapeDtypeStruct((T, hw), MXU_DTYPE), jax.ShapeDtypeStruct((T, cw), MXU_DTYPE),
                   jax.ShapeDtypeStruct((T, 128), F32)],
        compiler_params=_params("parallel"),
    )(qkv, bg)


def _gdn_scan_fwd(u, w, qd, kd, p, cd, name):
    T = u.shape[0]
    cb = _tile(T // CHUNK, SCAN_CHUNKS_PER_STEP)
    rows = cb * CHUNK

    def body(u_ref, w_ref, qd_ref, kd_ref, p_ref, cd_ref, o_ref, s_all_ref, vn_ref, s_ref):
        @pl.when(pl.program_id(0) == 0)
        def _():
            s_ref[...] = jnp.zeros_like(s_ref)

        hss = [slice(hh * HEAD_DIM, (hh + 1) * HEAD_DIM) for hh in range(GDN_HEADS)]
        css = [slice(hh * CHUNK, (hh + 1) * CHUNK) for hh in range(GDN_HEADS)]
        s_cur = [s_ref[hh] for hh in range(GDN_HEADS)]
        for ci in range(cb):
            rs = slice(ci * CHUNK, (ci + 1) * CHUNK)
            for hh in range(GDN_HEADS):
                s_all_ref[ci * GDN_WIDTH + hh * HEAD_DIM:ci * GDN_WIDTH + (hh + 1) * HEAD_DIM, :] = s_cur[hh]
            s_ms = [s.astype(MXU_DTYPE) for s in s_cur]
            w_s = [_dot(w_ref[rs, hs], s_m, NN) for hs, s_m in zip(hss, s_ms)]
            q_s = [_dot(qd_ref[rs, hs], s_m, NN) for hs, s_m in zip(hss, s_ms)]
            v_ms = [(u_ref[rs, hs] - ws_).astype(MXU_DTYPE) for hs, ws_ in zip(hss, w_s)]
            k_v = [_dot(kd_ref[rs, hs], v_m, TN) for hs, v_m in zip(hss, v_ms)]
            p_v = [_dot(p_ref[rs, cs], v_m, NN) for cs, v_m in zip(css, v_ms)]
            for hh in range(GDN_HEADS):
                vn_ref[rs, hss[hh]] = v_ms[hh]
                o_ref[rs, hss[hh]] = q_s[hh] + p_v[hh]
                c_dec = cd_ref[ci * CHUNK:ci * CHUNK + 1, GDN_HEADS + hh:GDN_HEADS + hh + 1]
                s_cur[hh] = s_cur[hh] * c_dec + k_v[hh]
        for hh in range(GDN_HEADS):
            s_ref[hh] = s_cur[hh]

    def spec(width):
        return pl.BlockSpec((rows, width), lambda n: (n, 0))

    hw, cw = GDN_WIDTH, GDN_HEADS * CHUNK
    return pl.pallas_call(
        body, name=name, grid=(T // rows,),
        in_specs=[spec(hw), spec(hw), spec(hw), spec(hw), spec(cw), spec(128)],
        out_specs=[spec(hw), pl.BlockSpec((cb * GDN_WIDTH, HEAD_DIM), lambda n: (n, 0)), spec(hw)],
        out_shape=[jax.ShapeDtypeStruct((T, hw), F32),
                   jax.ShapeDtypeStruct((T // CHUNK * GDN_WIDTH, HEAD_DIM), F32),
                   jax.ShapeDtypeStruct((T, hw), MXU_DTYPE)],
        scratch_shapes=[pltpu.VMEM((GDN_HEADS, HEAD_DIM, HEAD_DIM), F32)],
        compiler_params=_params("arbitrary"),
    )(u, w, qd, kd, p, cd)


def _gdn_scan_bwd(do, w, qd, kd, p, cd, s_all, vn, name):
    T = do.shape[0]
    cb = _tile(T // CHUNK, SCAN_CHUNKS_PER_STEP)
    rows = cb * CHUNK
    n_steps = T // rows

    def body(do_ref, w_ref, qd_ref, kd_ref, p_ref, cd_ref, s_all_ref, vn_ref,
             dvn_ref, dw_ref, dqd_ref, dkd_ref, dp_ref, dcd_ref, ds_ref):
        @pl.when(pl.program_id(0) == 0)
        def _():
            ds_ref[...] = jnp.zeros_like(ds_ref)

        causal, _, _ = _chunk_masks()
        lane = lax.broadcasted_iota(jnp.int32, (CHUNK, 128), 1)
        heads = range(GDN_HEADS)
        hss = [slice(hh * HEAD_DIM, (hh + 1) * HEAD_DIM) for hh in heads]
        css = [slice(hh * CHUNK, (hh + 1) * CHUNK) for hh in heads]
        ds_cur = [ds_ref[hh] for hh in heads]
        for ci in reversed(range(cb)):
            rs = slice(ci * CHUNK, (ci + 1) * CHUNK)
            ds_ms = [d.astype(MXU_DTYPE) for d in ds_cur]
            s_olds = [s_all_ref[ci * GDN_WIDTH + hh * HEAD_DIM:ci * GDN_WIDTH + (hh + 1) * HEAD_DIM, :] for hh in heads]
            s_ms = [s.astype(MXU_DTYPE) for s in s_olds]
            do_ms = [do_ref[rs, hs].astype(MXU_DTYPE) for hs in hss]
            p_do = [_dot(p_ref[rs, cs], do_m, TN) for cs, do_m in zip(css, do_ms)]
            k_ds = [_dot(kd_ref[rs, hs], ds_m, NN) for hs, ds_m in zip(hss, ds_ms)]
            q_do = [_dot(qd_ref[rs, hs], do_m, TN) for hs, do_m in zip(hss, do_ms)]
            dqds = [_dot(do_m, s_m, NT) for do_m, s_m in zip(do_ms, s_ms)]
            dkds = [_dot(vn_ref[rs, hs], ds_m, NT) for hs, ds_m in zip(hss, ds_ms)]
            dps = [_dot(do_m, vn_ref[rs, hs], NT) for hs, do_m in zip(hss, do_ms)]
            dv_news = [a + b for a, b in zip(p_do, k_ds)]
            dvn_ms = [d.astype(MXU_DTYPE) for d in dv_news]
            w_dv = [_dot(w_ref[rs, hs], dvn_m, TN) for hs, dvn_m in zip(hss, dvn_ms)]
            dws = [_dot(dvn_m, s_m, NT) for dvn_m, s_m in zip(dvn_ms, s_ms)]
            dcd_tile = jnp.zeros((CHUNK, 128), F32)
            for hh in heads:
                dvn_ref[rs, hss[hh]] = dv_news[hh]
                dw_ref[rs, hss[hh]] = -dws[hh]
                dqd_ref[rs, hss[hh]] = dqds[hh]
                dkd_ref[rs, hss[hh]] = dkds[hh]
                dp_ref[rs, css[hh]] = jnp.where(causal, dps[hh], 0.0)
                dcd = jnp.sum(jnp.sum(s_olds[hh] * ds_cur[hh], axis=1, keepdims=True), axis=0, keepdims=True)
                dcd_tile = jnp.where(lane == GDN_HEADS + hh, dcd, dcd_tile)
                c_dec = cd_ref[ci * CHUNK:ci * CHUNK + 1, GDN_HEADS + hh:GDN_HEADS + hh + 1]
                ds_cur[hh] = c_dec * ds_cur[hh] + q_do[hh] - w_dv[hh]
            dcd_ref[rs, :] = dcd_tile
        for hh in heads:
            ds_ref[hh] = ds_cur[hh]

    def spec(width):
        return pl.BlockSpec((rows, width), lambda n: (n_steps - 1 - n, 0))

    hw, cw = GDN_WIDTH, GDN_HEADS * CHUNK
    return pl.pallas_call(
        body, name=name, grid=(n_steps,),
        in_specs=[spec(hw), spec(hw), spec(hw), spec(hw), spec(cw), spec(128),
                  pl.BlockSpec((cb * GDN_WIDTH, HEAD_DIM), lambda n: (n_steps - 1 - n, 0)), spec(hw)],
        out_specs=[spec(hw), spec(hw), spec(hw), spec(hw), spec(cw), spec(128)],
        out_shape=[jax.ShapeDtypeStruct((T, hw), F32)] * 4
        + [jax.ShapeDtypeStruct((T, cw), F32), jax.ShapeDtypeStruct((T, 128), F32)],
        scratch_shapes=[pltpu.VMEM((GDN_HEADS, HEAD_DIM, HEAD_DIM), F32)],
        compiler_params=_params("arbitrary"),
    )(do, w, qd, kd, p, cd, s_all, vn)


def _gdn_chunk_bwd(qkv, bg, tinv_all, u, w, dvn, dw, dqd, dkd, dp, dcd, name):
    T = qkv.shape[0]
    cb = _tile(T // CHUNK, CHUNKS_PER_STEP)
    rows = cb * CHUNK

    def body(qkv_ref, bg_ref, tinv_ref, u_ref, w_ref, dvn_ref, dw_ref, dqd_ref, dkd_ref, dp_ref, dcd_ref,
             dqkv_ref, dbg_ref):
        masks = _chunk_masks()
        causal, strict, _ = masks
        lane = lax.broadcasted_iota(jnp.int32, (CHUNK, 128), 1)
        row = lax.broadcasted_iota(jnp.int32, (CHUNK, 128), 0)
        heads = []
        for ci in range(cb):
            rs = slice(ci * CHUNK, (ci + 1) * CHUNK)
            bgv = bg_ref[rs, :]
            gc_all = _chunk_cumsum(bgv)
            gc_rows = _row_form(gc_all)
            for hh in range(GDN_HEADS):
                qs, ks, vs = _head_slices(hh)
                q, k = qkv_ref[rs, qs], qkv_ref[rs, ks]
                beta = bgv[:, hh:hh + 1]
                gc = gc_all[:, GDN_HEADS + hh:GDN_HEADS + hh + 1]
                decay, kb, eg = _head_elementwise(k, beta, gc, gc_rows[GDN_HEADS + hh:GDN_HEADS + hh + 1, :], causal)
                heads.append(dict(ci=ci, hh=hh, rs=rs, hs=slice(hh * HEAD_DIM, (hh + 1) * HEAD_DIM),
                                  cs=slice(hh * CHUNK, (hh + 1) * CHUNK), q=q, k=k, beta=beta, gc=gc,
                                  decay=decay, kb=kb, eg=eg))
        for hd in heads:
            hd["t"] = _split(tinv_ref[hd["rs"], hd["cs"]])
        for hd in heads:
            hd["kk"] = _mdot(hd["kb"], hd["k"], NT)
            hd["qk"] = _mdot(hd["q"], hd["k"], NT)
        for hd in heads:
            hd["dvb"] = _dot3(hd["t"], _split(dvn_ref[hd["rs"], hd["hs"]]), TN)
            hd["dkbeg"] = _dot3(hd["t"], _split(dw_ref[hd["rs"], hd["hs"]]), TN)
        for hd in heads:
            rs, hs = hd["rs"], hd["hs"]
            da = -(_mdot(hd["dvb"], u_ref[rs, hs], NT) + _mdot(hd["dkbeg"], w_ref[rs, hs], NT))
            dm = jnp.where(strict, da, 0.0)
            dp_ = dp_ref[rs, hd["cs"]]
            hd["dkk"] = dm * hd["decay"]
            hd["dqk"] = dp_ * hd["decay"]
            hd["e"] = (hd["dkk"] * hd["kk"] + hd["dqk"] * hd["qk"])
        for hd in heads:
            hd["dkb"] = _mdot(hd["dkk"], hd["k"], NN)
            hd["dk"] = _mdot(hd["dkk"], hd["kb"], TN) + _mdot(hd["dqk"], hd["q"], TN)
            hd["dq"] = _mdot(hd["dqk"], hd["k"], NN)
            onehot = (lane == GDN_HEADS + hd["hh"]).astype(jnp.bfloat16)
            e_hi, e_lo = _split(hd["e"])
            hd["col_sums"] = _dot(e_lo, onehot, TN) + _dot(e_hi, onehot, TN)
        tiles = {}
        for hd in heads:
            ci, hh, rs, hs = hd["ci"], hd["hh"], hd["rs"], hd["hs"]
            qs, ks, vs = _head_slices(hh)
            q, k, beta, gc, eg, kb = hd["q"], hd["k"], hd["beta"], hd["gc"], hd["eg"], hd["kb"]
            v = qkv_ref[rs, vs]
            dqd_, dkd_ = dqd_ref[rs, hs], dkd_ref[rs, hs]
            gl = gc[CHUNK - 1:CHUNK, :]
            ek = jnp.exp(gl - gc)
            dkb = hd["dkb"] + hd["dkbeg"] * eg
            deg = jnp.sum(dqd_ * q, axis=1, keepdims=True) + jnp.sum(hd["dkbeg"] * kb, axis=1, keepdims=True)
            dek = jnp.sum(dkd_ * k, axis=1, keepdims=True)
            dcd_ = dcd_ref[ci * CHUNK:ci * CHUNK + 1, GDN_HEADS + hh:GDN_HEADS + hh + 1]
            dgl = jnp.sum(dek * ek, axis=0, keepdims=True) + dcd_ * jnp.exp(gl)
            dgc = jnp.sum(hd["e"], axis=1, keepdims=True) + deg * eg - dek * ek
            dbeta_tile, dgc_tile = tiles.get(ci, (jnp.zeros((CHUNK, 128), F32), jnp.zeros((CHUNK, 128), F32)))
            dgc_tile += jnp.where(lane == GDN_HEADS + hh, dgc, 0.0) - hd["col_sums"]
            dgc_tile += jnp.where((lane == GDN_HEADS + hh) & (row == CHUNK - 1), dgl, 0.0)
            dbeta = jnp.sum(dkb * k, axis=1, keepdims=True) + jnp.sum(hd["dvb"] * v, axis=1, keepdims=True)
            dbeta_tile += jnp.where(lane == hh, dbeta, 0.0)
            tiles[ci] = (dbeta_tile, dgc_tile)
            dqkv_ref[rs, qs] = hd["dq"] + dqd_ * eg
            dqkv_ref[rs, ks] = hd["dk"] + dkd_ * ek + dkb * beta
            dqkv_ref[rs, vs] = hd["dvb"] * beta
        for ci in range(cb):
            dbeta_tile, dgc_tile = tiles[ci]
            dbg_ref[ci * CHUNK:(ci + 1) * CHUNK, :] = dbeta_tile + _chunk_cumsum(dgc_tile, reverse=True)

    def spec(width):
        return pl.BlockSpec((rows, width), lambda n: (n, 0))

    hw, cw = GDN_WIDTH, GDN_HEADS * CHUNK
    return pl.pallas_call(
        body, name=name, grid=(T // rows,),
        in_specs=[spec(QKV_WIDTH), spec(128), spec(cw), spec(hw), spec(hw), spec(hw), spec(hw), spec(hw),
                  spec(hw), spec(cw), spec(128)],
        out_specs=[spec(QKV_WIDTH), spec(128)],
        out_shape=[jax.ShapeDtypeStruct((T, QKV_WIDTH), F32), jax.ShapeDtypeStruct((T, 128), F32)],
        compiler_params=_params("parallel"),
    )(qkv, bg, tinv_all, u, w, dvn, dw, dqd, dkd, dp, dcd)


def _pool_counts(i, tm, rows, offset):
    t = i * tm - offset + lax.broadcasted_iota(jnp.int32, (rows, 1), 0)
    return [jnp.minimum(t + 1, w).astype(F32) for w in POOL_WINDOWS]


def _window_sums(window, forward):
    sums, s, step = [], window, 1
    for _ in POOL_WINDOWS:
        s = s + _shift_rows(s, -step if forward else step)
        sums.append(s)
        step *= 2
    return sums


def _pooled(window, counts):
    sums = _window_sums(window, forward=False)
    out = []
    for gi in range(POOL_GROUPS):
        sl = slice(gi * 128, (gi + 1) * 128)
        out.append(sums[gi][HALO:, sl] / counts[gi] - window[HALO:, sl])
    return out


def _mix_post(o, proj, gdn_norm, pool_w, pool_scale, name):
    T = o.shape[0]
    tm = _tile(T, 256)
    hb = tm // HALO

    def body(o_ref, z_ref, p_ref, ph_ref, gn_ref, pw_ref, ps_ref, out_ref):
        i = pl.program_id(0)
        for hh in range(GDN_HEADS):
            sl = slice(hh * HEAD_DIM, (hh + 1) * HEAD_DIM)
            oh, zh = o_ref[:, sl], z_ref[:, sl]
            ro = lax.rsqrt(jnp.mean(oh * oh, axis=-1, keepdims=True) + EPS)
            out_ref[:, sl] = (((oh * ro) * gn_ref[...]) * (zh * _sigmoid(zh))).astype(out_ref.dtype)
        halo = jnp.where(i == 0, 0.0, ph_ref[...])
        window = jnp.concatenate([halo, p_ref[...]], axis=0)
        pooled = _pooled(window, _pool_counts(i, tm, tm, 0))
        for gi in range(POOL_GROUPS):
            pm = _mdot(pooled[gi], pw_ref[gi], NN)
            out_ref[:, GDN_WIDTH + gi * 128:GDN_WIDTH + (gi + 1) * 128] = (
                pm * ps_ref[:, gi * 128:(gi + 1) * 128]).astype(out_ref.dtype)

    return pl.pallas_call(
        body, name=name, grid=(T // tm,),
        in_specs=[pl.BlockSpec((tm, GDN_WIDTH), lambda i: (i, 0)),
                  pl.BlockSpec((tm, GDN_WIDTH), lambda i: (i, COL_Z // GDN_WIDTH)),
                  pl.BlockSpec((tm, POOL_WIDTH), lambda i: (i, COL_P // POOL_WIDTH)),
                  pl.BlockSpec((HALO, POOL_WIDTH), lambda i: (jnp.maximum(i * hb - 1, 0), COL_P // POOL_WIDTH)),
                  pl.BlockSpec((1, HEAD_DIM), lambda i: (0, 0)),
                  pl.BlockSpec((POOL_GROUPS, 128, 128), lambda i: (0, 0, 0)),
                  pl.BlockSpec((1, POOL_WIDTH), lambda i: (0, 0))],
        out_specs=pl.BlockSpec((tm, GDN_WIDTH + POOL_WIDTH), lambda i: (i, 0)),
        out_shape=jax.ShapeDtypeStruct((T, GDN_WIDTH + POOL_WIDTH), MXU_DTYPE),
        compiler_params=_params("parallel"),
    )(o, proj, proj, proj, gdn_norm, pool_w, pool_scale)


def _mix_post_bwd(dmix, o, proj, gdn_norm, pool_w, pool_scale, name):
    T = o.shape[0]
    tm = _tile(T, 256)
    hb = tm // HALO
    n_tiles = T // tm

    def body(dg_ref, dpo_ref, dpo_next_ref, o_ref, z_ref, p_ref, ph_ref, gn_ref, pw_ref, ps_ref,
             do_ref, dzp_ref, dgn_ref, dpw_ref, dps_ref):
        i = pl.program_id(0)

        @pl.when(i == 0)
        def _():
            dgn_ref[...] = jnp.zeros_like(dgn_ref)
            dpw_ref[...] = jnp.zeros_like(dpw_ref)
            dps_ref[...] = jnp.zeros_like(dps_ref)

        gn = gn_ref[...]
        dgn = jnp.zeros((1, HEAD_DIM), F32)
        for hh in range(GDN_HEADS):
            sl = slice(hh * HEAD_DIM, (hh + 1) * HEAD_DIM)
            oh, zh, dy = o_ref[:, sl], z_ref[:, sl], dg_ref[:, sl]
            ro = lax.rsqrt(jnp.mean(oh * oh, axis=-1, keepdims=True) + EPS)
            on = oh * ro
            sig = _sigmoid(zh)
            sz = zh * sig
            dzp_ref[:, sl] = (dy * (on * gn) * (sig * (1.0 + zh * (1.0 - sig)))).astype(dzp_ref.dtype)
            dgn += jnp.sum(dy * on * sz, axis=0, keepdims=True)
            don = dy * gn * sz
            do_ref[:, sl] = ro * (don - on * jnp.mean(don * on, axis=-1, keepdims=True))
        dgn_ref[...] += dgn

        halo = jnp.where(i == 0, 0.0, ph_ref[...])
        window = jnp.concatenate([halo, p_ref[...]], axis=0)
        counts = _pool_counts(i, tm, tm + HALO, 0)
        pooled = _pooled(window, [cn[:tm] for cn in counts])
        nxt = jnp.where(i == n_tiles - 1, 0.0, dpo_next_ref[...])
        dpo_w = jnp.concatenate([dpo_ref[...], nxt], axis=0)
        ps = ps_ref[...]
        dps = []
        scaled = []
        for gi in range(POOL_GROUPS):
            sl = slice(gi * 128, (gi + 1) * 128)
            dpm = dpo_w[:, sl] * ps[:, sl]
            pm = _mdot(pooled[gi], pw_ref[gi], NN)
            dps.append(jnp.sum(dpo_w[:tm, sl] * pm, axis=0, keepdims=True))
            dpw_ref[gi] += _mdot(pooled[gi], dpm[:tm], TN)
            dpooled = _mdot(dpm, pw_ref[gi], NT)
            scaled.append((dpooled, dpooled / counts[gi]))
        dps_ref[...] += jnp.concatenate(dps, axis=1)
        lead = _window_sums(jnp.concatenate([sc for _, sc in scaled], axis=1), forward=True)
        for gi in range(POOL_GROUPS):
            sl = slice(gi * 128, (gi + 1) * 128)
            dzp_ref[:, GDN_WIDTH + gi * 128:GDN_WIDTH + (gi + 1) * 128] = (
                lead[gi][:tm, sl] - scaled[gi][0][:tm]).astype(dzp_ref.dtype)

    last_halo = T // HALO - 1
    return pl.pallas_call(
        body, name=name, grid=(n_tiles,),
        in_specs=[pl.BlockSpec((tm, GDN_WIDTH), lambda i: (i, 0)),
                  pl.BlockSpec((tm, POOL_WIDTH), lambda i: (i, 1)),
                  pl.BlockSpec((HALO, POOL_WIDTH), lambda i: (jnp.minimum((i + 1) * hb, last_halo), 1)),
                  pl.BlockSpec((tm, GDN_WIDTH), lambda i: (i, 0)),
                  pl.BlockSpec((tm, GDN_WIDTH), lambda i: (i, COL_Z // GDN_WIDTH)),
                  pl.BlockSpec((tm, POOL_WIDTH), lambda i: (i, COL_P // POOL_WIDTH)),
                  pl.BlockSpec((HALO, POOL_WIDTH), lambda i: (jnp.maximum(i * hb - 1, 0), COL_P // POOL_WIDTH)),
                  pl.BlockSpec((1, HEAD_DIM), lambda i: (0, 0)),
                  pl.BlockSpec((POOL_GROUPS, 128, 128), lambda i: (0, 0, 0)),
                  pl.BlockSpec((1, POOL_WIDTH), lambda i: (0, 0))],
        out_specs=[pl.BlockSpec((tm, GDN_WIDTH), lambda i: (i, 0)),
                   pl.BlockSpec((tm, GDN_WIDTH + POOL_WIDTH), lambda i: (i, 0)),
                   pl.BlockSpec((1, HEAD_DIM), lambda i: (0, 0)),
                   pl.BlockSpec((POOL_GROUPS, 128, 128), lambda i: (0, 0, 0)),
                   pl.BlockSpec((1, POOL_WIDTH), lambda i: (0, 0))],
        out_shape=[jax.ShapeDtypeStruct((T, GDN_WIDTH), F32),
                   jax.ShapeDtypeStruct((T, GDN_WIDTH + POOL_WIDTH), MXU_DTYPE),
                   jax.ShapeDtypeStruct((1, HEAD_DIM), F32),
                   jax.ShapeDtypeStruct((POOL_GROUPS, 128, 128), F32),
                   jax.ShapeDtypeStruct((1, POOL_WIDTH), F32)],
        compiler_params=_params("arbitrary"),
    )(dmix, dmix, dmix, o, proj, proj, proj, gdn_norm, pool_w, pool_scale)


def _gdn_prep_bwd(proj, conv_w, a_log_l, dt_bias_l, dqkv, dbg, dzp, name):
    T = proj.shape[0]
    tm = _tile(T, 256)
    hb = tm // 8
    n_tiles = T // tm
    last_halo = T // 8 - 1

    def body(cur_ref, before_ref, after_ref, ba_ref, w_ref, al_ref, dtb_ref, dq_ref, dq_after_ref, dbg_ref,
             dzp_ref, dproj_ref, dw_ref, dal_ref, ddtb_ref):
        i = pl.program_id(0)

        @pl.when(i == 0)
        def _():
            dw_ref[...] = jnp.zeros_like(dw_ref)
            dal_ref[...] = jnp.zeros_like(dal_ref)
            ddtb_ref[...] = jnp.zeros_like(ddtb_ref)

        last = i == n_tiles - 1
        w = w_ref[...]
        before = jnp.where(i == 0, 0.0, before_ref[...])
        after = jnp.where(last, 0.0, after_ref[...])
        window = jnp.concatenate([before, cur_ref[...], after], axis=0)
        y = _conv_act(window, w)
        sig = _sigmoid(y)
        act = y * sig
        dq_w = jnp.concatenate([jnp.zeros((8, QKV_WIDTH), F32), dq_ref[...],
                                jnp.where(last, 0.0, dq_after_ref[...])], axis=0)
        dact = []
        for hh in range(3 * GDN_HEADS):
            sl = slice(hh * HEAD_DIM, (hh + 1) * HEAD_DIM)
            blk, dblk = act[:, sl], dq_w[:, sl]
            if hh < 2 * GDN_HEADS:
                rn = lax.rsqrt(jnp.sum(blk * blk, axis=-1, keepdims=True) + EPS)
                unit = blk * rn
                if hh < GDN_HEADS:
                    dblk = dblk * (HEAD_DIM ** -0.5)
                dblk = rn * (dblk - unit * jnp.sum(dblk * unit, axis=-1, keepdims=True))
            dact.append(dblk)
        dy = jnp.concatenate(dact, axis=1) * (sig * (1.0 + y * (1.0 - sig)))
        dx = dy * w[CONV_K - 1:CONV_K, :]
        dws = [None] * CONV_K
        dws[CONV_K - 1] = jnp.sum(dy[8:8 + tm] * window[8:8 + tm], axis=0, keepdims=True)
        for j in range(CONV_K - 1):
            s = CONV_K - 1 - j
            dx += _shift_rows(dy, -s) * w[j:j + 1, :]
            dws[j] = jnp.sum(dy[8:8 + tm] * _shift_rows(window, s)[8:8 + tm], axis=0, keepdims=True)
        dw_ref[...] += jnp.concatenate(dws, axis=0)
        dproj_ref[:, :QKV_WIDTH] = dx[8:8 + tm].astype(dproj_ref.dtype)
        dproj_ref[:, COL_Z:COL_BA] = dzp_ref[...]

        ba = ba_ref[...]
        dbg_ = dbg_ref[...]
        lane = lax.broadcasted_iota(jnp.int32, ba.shape, 1)
        beta = _sigmoid(ba)
        pre = ba + dtb_ref[...]
        neg_a = -jnp.exp(al_ref[...])
        g = neg_a * _softplus(pre)
        is_g = (lane >= GDN_HEADS) & (lane < 2 * GDN_HEADS)
        da_raw = jnp.where(is_g, dbg_ * neg_a * _sigmoid(pre), 0.0)
        dba = jnp.where(lane < GDN_HEADS, dbg_ * beta * (1.0 - beta), da_raw)
        dproj_ref[:, COL_BA:] = dba.astype(dproj_ref.dtype)
        dal_ref[...] += jnp.sum(jnp.where(is_g, dbg_ * g, 0.0), axis=0, keepdims=True)
        ddtb_ref[...] += jnp.sum(da_raw, axis=0, keepdims=True)

    lane_vec = pl.BlockSpec((1, 128), lambda i: (0, 0))
    return pl.pallas_call(
        body, name=name, grid=(n_tiles,),
        in_specs=[pl.BlockSpec((tm, QKV_WIDTH), lambda i: (i, 0)),
                  pl.BlockSpec((8, QKV_WIDTH), lambda i: (jnp.maximum(i * hb - 1, 0), 0)),
                  pl.BlockSpec((8, QKV_WIDTH), lambda i: (jnp.minimum((i + 1) * hb, last_halo), 0)),
                  pl.BlockSpec((tm, 128), lambda i: (i, COL_BA // 128)),
                  pl.BlockSpec((CONV_K, QKV_WIDTH), lambda i: (0, 0)), lane_vec, lane_vec,
                  pl.BlockSpec((tm, QKV_WIDTH), lambda i: (i, 0)),
                  pl.BlockSpec((8, QKV_WIDTH), lambda i: (jnp.minimum((i + 1) * hb, last_halo), 0)),
                  pl.BlockSpec((tm, 128), lambda i: (i, 0)),
                  pl.BlockSpec((tm, GDN_WIDTH + POOL_WIDTH), lambda i: (i, 0))],
        out_specs=[pl.BlockSpec((tm, D_IN_PAD), lambda i: (i, 0)),
                   pl.BlockSpec((CONV_K, QKV_WIDTH), lambda i: (0, 0)), lane_vec, lane_vec],
        out_shape=[jax.ShapeDtypeStruct((T, D_IN_PAD), MXU_DTYPE),
                   jax.ShapeDtypeStruct((CONV_K, QKV_WIDTH), F32),
                   jax.ShapeDtypeStruct((1, 128), F32), jax.ShapeDtypeStruct((1, 128), F32)],
        compiler_params=_params("arbitrary"),
    )(proj, proj, proj, proj, conv_w, a_log_l, dt_bias_l, dqkv, dqkv, dbg, dzp)


def _mod_part(c_all, w_ada, b_part, name):
    def body(c_ref, w_ref, b_ref, out_ref):
        cc = c_ref[...]
        out_ref[...] = _mdot(cc * _sigmoid(cc), w_ref[...], NN) + b_ref[...]

    return pl.pallas_call(
        body, name=name, out_shape=jax.ShapeDtypeStruct((c_all.shape[0], w_ada.shape[1]), F32),
        compiler_params=_params(),
    )(c_all, w_ada, b_part)


def _w_ada_grad(c_all, dmod_part, name):
    def body(c_ref, d_ref, out_ref):
        cc = c_ref[...]
        out_ref[...] = _mdot(cc * _sigmoid(cc), d_ref[...], TN)

    return pl.pallas_call(
        body, name=name, out_shape=jax.ShapeDtypeStruct((c_all.shape[1], dmod_part.shape[1]), F32),
        compiler_params=_params(),
    )(c_all, dmod_part)


def _sum_parts(parts, name):
    _, R, C = parts.shape
    tr = _tile(R, 256)

    def body(p_ref, out_ref):
        acc = p_ref[0].astype(F32)
        for s in range(1, N_DEV):
            acc += p_ref[s].astype(F32)
        out_ref[...] = acc

    return pl.pallas_call(
        body, name=name, grid=(R // tr,),
        in_specs=[pl.BlockSpec((N_DEV, tr, C), lambda i: (0, i, 0))],
        out_specs=pl.BlockSpec((tr, C), lambda i: (i, 0)),
        out_shape=jax.ShapeDtypeStruct((R, C), F32),
        compiler_params=_params("parallel"),
    )(parts)


def _adamw(w, g, m, v, name):
    R, C = w.shape
    tr = _tile(R, 256)

    def body(w_ref, g_ref, m_ref, v_ref, d_ref, mo_ref, vo_ref):
        gg = g_ref[...]
        mm = ADAM_B1 * m_ref[...] + (1.0 - ADAM_B1) * gg
        vv = ADAM_B2 * v_ref[...] + (1.0 - ADAM_B2) * (gg * gg)
        m_hat = mm / (1.0 - ADAM_B1 ** ADAM_STEP)
        v_hat = vv / (1.0 - ADAM_B2 ** ADAM_STEP)
        d_ref[...] = -ADAM_LR * (m_hat / (jnp.sqrt(v_hat) + ADAM_EPS) + ADAM_WD * w_ref[...])
        mo_ref[...] = mm
        vo_ref[...] = vv

    spec = pl.BlockSpec((tr, C), lambda i: (i, 0))
    return pl.pallas_call(
        body, name=name, grid=(R // tr,),
        in_specs=[spec] * 4, out_specs=[spec] * 3,
        out_shape=[jax.ShapeDtypeStruct((R, C), F32)] * 3,
        compiler_params=_params("parallel"),
    )(w, g, m, v)


def _weight_grad(a, b, name, dep=None):
    return _matmul([(a, b)], TN, WIRE_DTYPE, name, tm=1408, tn=1024, tk=1024, dep=dep)


def _rows_of(flat, lanes=1024):
    flat = flat.reshape(-1)
    n = -(-flat.shape[0] // lanes) * lanes
    return jnp.pad(flat, (0, n - flat.shape[0])).reshape(n // lanes, lanes)


def _pad_rows(a, rows):
    return jnp.pad(a, ((0, rows - a.shape[0]), (0, 0)))


def kernel(x, c, w_ada, b_ada, norm_ffn1, ffn1_gate, ffn1_up, ffn1_down, norm_mix, w_in, conv_w, a_log, dt_bias, gdn_norm, pool_w, pool_scale, w_out, norm_ffn2, ffn2_gate, ffn2_up, ffn2_down, final_norm, loss_target, m_w_ada, m_b_ada, m_norm_ffn1, m_ffn1_gate, m_ffn1_up, m_ffn1_down, m_norm_mix, m_w_in, m_conv_w, m_a_log, m_dt_bias, m_gdn_norm, m_pool_w, m_pool_scale, m_w_out, m_norm_ffn2, m_ffn2_gate, m_ffn2_up, m_ffn2_down, m_final_norm, v_w_ada, v_b_ada, v_norm_ffn1, v_ffn1_gate, v_ffn1_up, v_ffn1_down, v_norm_mix, v_w_in, v_conv_w, v_a_log, v_dt_bias, v_gdn_norm, v_pool_w, v_pool_scale, v_w_out, v_norm_ffn2, v_ffn2_gate, v_ffn2_up, v_ffn2_down, v_final_norm):
    T, D = x.shape[1], x.shape[2]
    Fs = ffn1_gate.shape[2]
    Ws = w_in.shape[2]
    Ws_pad = -(-Ws // 16) * 16
    Os = w_out.shape[1]
    Ms = w_ada.shape[2]
    Cs = conv_w.shape[2]
    me = 4 * lax.axis_index("x") + 2 * lax.axis_index("y") + lax.axis_index("c")
    x0, target = x[0], loss_target[0]

    def wire(a):
        return a.astype(WIRE_DTYPE)

    def token(started):
        return started[4][:1, :1]

    def with_own(landed, own):
        return lax.dynamic_update_slice(landed, own[None], (me, 0, 0))

    def full(landed, off, size, keep=None):
        blk = landed[:, off:off + (size if keep is None else keep), :]
        return blk.reshape(-1, D).astype(MXU_DTYPE)

    no_dep = jnp.zeros((8, 128), F32)
    small = jnp.concatenate([_pad_rows(c, 8), _pad_rows(jnp.pad(conv_w[0], ((0, 0), (0, D - Cs))), 8)], axis=0)
    got = _all_gather(small, "gather_small")
    c_all = got[:, 0, :]
    conv_full = jnp.transpose(got[:, 8:8 + CONV_K, :Cs], (1, 0, 2)).reshape(CONV_K, QKV_WIDTH)
    b_part = lax.dynamic_slice(b_ada, (0, me * Ms), (1, Ms))
    mod_parts = _all_gather(_mod_part(c_all, w_ada[0], b_part, "mod_part"), "gather_mod")
    mod_all = jnp.transpose(mod_parts, (1, 0, 2)).reshape(N_DEV, N_MOD * D)
    mod = lax.dynamic_slice(mod_all, (me, 0), (1, N_MOD * D)).reshape(N_MOD, 1, D)
    sh1, sc1, gt1, sh2, sc2, gt2, sh3, sc3, gt3 = [mod[i] for i in range(N_MOD)]

    w1 = jnp.concatenate([wire(ffn1_gate[0].T), wire(ffn1_up[0].T), wire(ffn1_down[0])], axis=0)
    w23 = jnp.concatenate([wire(_pad_rows(w_in[0].T, Ws_pad)), wire(w_out[0]),
                           wire(ffn2_gate[0].T), wire(ffn2_up[0].T), wire(ffn2_down[0])], axis=0)
    off23 = [0, Ws_pad, Ws_pad + Os, Ws_pad + Os + Fs, Ws_pad + Os + 2 * Fs]
    w1_all = _all_gather(w1, "gather_w1", dep=mod_all)
    wg1_t, wu1_t, wd1 = full(w1_all, 0, Fs), full(w1_all, Fs, Fs), full(w1_all, 2 * Fs, Fs)
    w23_sent = _exchange_start(w23, True, w1_all, "w23_start")

    lane_pad = lambda a: jnp.pad(a, ((0, 0), (GDN_HEADS, 128 - 2 * GDN_HEADS)))
    a_log_l, dt_bias_l = lane_pad(a_log), lane_pad(dt_bias)
    pool_w_m = pool_w[0].astype(MXU_DTYPE)

    h1 = _norm_mod(x0, norm_ffn1, sc1 + token(w23_sent), sh1, "norm1")
    y1, g1, u1, a1 = _ffn_fwd(h1, wg1_t, wu1_t, wd1, "ffn1_fwd")
    x1, h2 = _resid_norm_mod(x0, y1, gt1, 0.5, norm_mix, sc2, sh2, "resid_norm2")
    w23_all = with_own(_exchange_wait(w23_sent, h2, True, "w23_wait"), w23)
    w_in_t = full(w23_all, off23[0], Ws_pad, Ws)
    wo = full(w23_all, off23[1], Os)
    wg2_t, wu2_t, wd2 = full(w23_all, off23[2], Fs), full(w23_all, off23[3], Fs), full(w23_all, off23[4], Fs)
    w_in_re = jnp.concatenate([w_in_t[:COL_Z + GDN_WIDTH], w_in_t[D_IN - POOL_WIDTH:],
                               w_in_t[4 * GDN_WIDTH:4 * GDN_WIDTH + 2 * GDN_HEADS],
                               jnp.zeros((128 - 2 * GDN_HEADS, D), MXU_DTYPE)], axis=0)
    proj = _matmul([(h2, w_in_re)], NT, F32, "proj_in", tm=512, tn=D_IN_PAD, tk=D)
    qkv, bg = _gdn_prep(proj, conv_full, a_log_l, dt_bias_l, "gdn_prep")
    tinv, u_c, w_c, qd_c, kd_c, p_c, cd_c = _gdn_chunk_fwd(qkv, bg, "gdn_chunk_fwd")
    o, s_all, vn_c = _gdn_scan_fwd(u_c, w_c, qd_c, kd_c, p_c, cd_c, "gdn_scan_fwd")
    mix_in = _mix_post(o, proj, gdn_norm, pool_w_m, pool_scale, "mix_post")
    mixed = _matmul([(mix_in, wo)], NN, F32, "mix_out", tm=512, tn=D, tk=GDN_WIDTH + POOL_WIDTH)
    x2, h3 = _resid_norm_mod(x1, mixed, gt2, 1.0, norm_ffn2, sc3, sh3, "resid_norm3")
    y3, g3, u3, a3 = _ffn_fwd(h3, wg2_t, wu2_t, wd2, "ffn2_fwd")
    loss_row, d3, d_final = _final_loss(x2, y3, gt3, final_norm.reshape(1, D), target, "final_loss")

    dy3, dgt3 = _gate_bwd(d3, y3, gt3, 0.5, "ffn2_gate_bwd")
    dg3, du3, dh3 = _ffn_bwd(dy3, wd2, g3, u3, wg2_t, wu2_t, "ffn2_bwd")
    d_wd2 = _weight_grad(a3, dy3, "ffn2_dwd")
    d_wg2 = _weight_grad(dg3, h3, "ffn2_dwg")
    d_wu2 = _weight_grad(du3, h3, "ffn2_dwu")
    d2, d_n3, dsc3, dsh3 = _norm_bwd(dh3, x2, norm_ffn2, sc3, d3, "norm3_bwd")
    dmixed, dgt2 = _gate_bwd(d2, mixed, gt2, 1.0, "mix_gate_bwd")
    dmix_in = _matmul([(dmixed, wo)], NT, F32, "mix_out_bwd", tm=512, tn=GDN_WIDTH + POOL_WIDTH, tk=D)
    d_wo = _matmul([(mix_in, dmixed)], TN, WIRE_DTYPE, "mix_dwo", tm=GDN_WIDTH + POOL_WIDTH, tn=D, tk=1024)
    do, dzp, d_gn, d_pw, d_ps = _mix_post_bwd(dmix_in, o, proj, gdn_norm, pool_w_m, pool_scale, "mix_post_bwd")
    dvn, dw_c, dqd, dkd, dp_c, dcd = _gdn_scan_bwd(do, w_c, qd_c, kd_c, p_c, cd_c, s_all, vn_c, "gdn_scan_bwd")
    dqkv, dbg = _gdn_chunk_bwd(qkv, bg, tinv, u_c, w_c, dvn, dw_c, dqd, dkd, dp_c, dcd, "gdn_chunk_bwd")
    dproj, d_conv, d_al, d_dtb = _gdn_prep_bwd(proj, conv_full, a_log_l, dt_bias_l, dqkv, dbg, dzp, "gdn_prep_bwd")
    dh2 = _matmul([(dproj, w_in_re)], NN, F32, "proj_in_bwd", tm=512, tn=D, tk=D_IN_PAD)
    d_win_re = _matmul([(dproj, h2)], TN, WIRE_DTYPE, "proj_in_dw", tm=D_IN_PAD, tn=D, tk=1024)
    d_win_t = jnp.concatenate([d_win_re[:COL_Z + GDN_WIDTH], d_win_re[COL_BA:COL_BA + 2 * GDN_HEADS],
                               d_win_re[COL_P:COL_P + POOL_WIDTH]], axis=0)
    d_win_blocks = jnp.pad(d_win_t.reshape(N_DEV, Ws, D), ((0, 0), (0, Ws_pad - Ws), (0, 0)))
    parts23 = jnp.concatenate(
        [wire(d_win_blocks), wire(d_wo.reshape(N_DEV, Os, D)), wire(d_wg2.reshape(N_DEV, Fs, D)),
         wire(d_wu2.reshape(N_DEV, Fs, D)), wire(d_wd2.reshape(N_DEV, Fs, D))], axis=1)
    own23 = lax.dynamic_index_in_dim(parts23, me, 0, keepdims=False)
    g23_sent = _exchange_start(parts23, False, no_dep, "g23_start")
    d1, d_n2, dsc2, dsh2 = _norm_bwd(dh2, x1, norm_mix, sc2 + token(g23_sent), d2, "norm2_bwd")
    dy1, dgt1 = _gate_bwd(d1, y1, gt1, 0.5, "ffn1_gate_bwd")
    dg1, du1, dh1 = _ffn_bwd(dy1, wd1, g1, u1, wg1_t, wu1_t, "ffn1_bwd")
    grad_x, d_n1, dsc1, dsh1 = _norm_bwd(dh1, x0, norm_ffn1, sc1, d1, "norm1_bwd")

    dmod = jnp.concatenate([dsh1, dsc1, dgt1, dsh2, dsc2, dgt2, dsh3, dsc3, dgt3], axis=0)
    small_rows = [dmod.reshape(-1), d_n1[0], d_n2[0], d_n3[0], d_final[0], d_gn[0], d_ps[0],
                  d_al[0, GDN_HEADS:2 * GDN_HEADS], d_dtb[0, GDN_HEADS:2 * GDN_HEADS], loss_row[0, :1],
                  d_conv.reshape(-1), d_pw.reshape(-1)]
    lanes = 1024
    small_rows = [_rows_of(r, lanes) for r in small_rows]
    n_rows = [r.shape[0] for r in small_rows]
    row_off = [sum(n_rows[:i]) for i in range(len(n_rows))]
    total = -(-sum(n_rows) // 8) * 8
    slab = _pad_rows(jnp.concatenate(small_rows, axis=0), total)
    slab_all = _all_gather(slab, "gather_small_grads")
    summed = _sum_parts(slab_all, "sum_small_grads")

    def piece(idx, n):
        return summed[row_off[idx]:row_off[idx] + n_rows[idx]].reshape(-1)[:n]

    g_b_ada = piece(0, N_MOD * D).reshape(1, N_MOD * D)
    g_n1, g_n2, g_n3 = piece(1, D).reshape(1, D), piece(2, D).reshape(1, D), piece(3, D).reshape(1, D)
    g_final = piece(4, D)
    g_gn = piece(5, HEAD_DIM).reshape(1, HEAD_DIM)
    g_ps = piece(6, POOL_WIDTH).reshape(1, POOL_WIDTH)
    g_al = piece(7, GDN_HEADS).reshape(1, GDN_HEADS)
    g_dtb = piece(8, GDN_HEADS).reshape(1, GDN_HEADS)
    loss = piece(9, 1)[0]
    g_conv = lax.dynamic_slice(piece(10, CONV_K * QKV_WIDTH).reshape(1, CONV_K, QKV_WIDTH), (0, 0, me * Cs),
                               (1, CONV_K, Cs))
    g_pw = piece(11, POOL_GROUPS * 128 * 128).reshape(1, POOL_GROUPS, 128, 128)

    dmod_all = slab_all[:, row_off[0]:row_off[0] + n_rows[0], :].reshape(N_DEV, -1)[:, :N_MOD * D]
    g_w_ada = _w_ada_grad(c_all, lax.dynamic_slice(dmod_all, (0, me * Ms), (N_DEV, Ms)), "w_ada_grad")[None]

    def send_ffn1(a, b, which, dep):
        parts = _weight_grad(a, b, f"ffn1_{which}", dep=dep).reshape(N_DEV, Fs, D)
        own = lax.dynamic_index_in_dim(parts, me, 0, keepdims=False)
        return _exchange_start(parts, False, no_dep, f"g1_{which}_start"), own

    g1_wg, own_wg = send_ffn1(dg1, h1, "dwg", summed)
    g1_wu, own_wu = send_ffn1(du1, h1, "dwu", g1_wg[4])
    g1_wd, own_wd = send_ffn1(a1, dy1, "dwd", g1_wu[4])

    big23 = _sum_parts(with_own(_exchange_wait(g23_sent, g1_wd[4], False, "g23_wait"), own23), "sum_grads23")
    g_w_in, g_w_out = big23[:Ws].T[None], big23[off23[1]:off23[1] + Os][None]
    g_ffn2_gate, g_ffn2_up = big23[off23[2]:off23[2] + Fs].T[None], big23[off23[3]:off23[3] + Fs].T[None]
    g_ffn2_down = big23[off23[4]:off23[4] + Fs][None]

    names = ["w_ada", "b_ada", "norm_ffn1", "ffn1_gate", "ffn1_up", "ffn1_down", "norm_mix", "w_in", "conv_w",
             "a_log", "dt_bias", "gdn_norm", "pool_w", "pool_scale", "w_out", "norm_ffn2", "ffn2_gate", "ffn2_up",
             "ffn2_down", "final_norm"]
    weights = dict(zip(names, [w_ada, b_ada, norm_ffn1, ffn1_gate, ffn1_up, ffn1_down, norm_mix, w_in, conv_w,
                               a_log, dt_bias, gdn_norm, pool_w, pool_scale, w_out, norm_ffn2, ffn2_gate, ffn2_up,
                               ffn2_down, final_norm]))
    ms = dict(zip(names, [m_w_ada, m_b_ada, m_norm_ffn1, m_ffn1_gate, m_ffn1_up, m_ffn1_down, m_norm_mix, m_w_in,
                          m_conv_w, m_a_log, m_dt_bias, m_gdn_norm, m_pool_w, m_pool_scale, m_w_out, m_norm_ffn2,
                          m_ffn2_gate, m_ffn2_up, m_ffn2_down, m_final_norm]))
    vs = dict(zip(names, [v_w_ada, v_b_ada, v_norm_ffn1, v_ffn1_gate, v_ffn1_up, v_ffn1_down, v_norm_mix, v_w_in,
                          v_conv_w, v_a_log, v_dt_bias, v_gdn_norm, v_pool_w, v_pool_scale, v_w_out, v_norm_ffn2,
                          v_ffn2_gate, v_ffn2_up, v_ffn2_down, v_final_norm]))
    grads = dict(w_ada=g_w_ada, b_ada=g_b_ada, norm_ffn1=g_n1, norm_mix=g_n2, w_in=g_w_in, conv_w=g_conv,
                 a_log=g_al, dt_bias=g_dtb, gdn_norm=g_gn, pool_w=g_pw, pool_scale=g_ps, w_out=g_w_out,
                 norm_ffn2=g_n3, ffn2_gate=g_ffn2_gate, ffn2_up=g_ffn2_up, ffn2_down=g_ffn2_down, final_norm=g_final)
    delta, new_m, new_v = {}, {}, {}

    def adamw_big(n):
        shp = weights[n].shape
        two_d = lambda a: a.reshape(shp[-2], shp[-1])
        d_, m_, v_ = _adamw(two_d(weights[n]), two_d(grads[n]), two_d(ms[n]), two_d(vs[n]), f"adamw_{n}")
        delta[n], new_m[n], new_v[n] = d_.reshape(shp), m_.reshape(shp), v_.reshape(shp)

    early = ["w_ada", "w_in", "w_out", "ffn2_gate", "ffn2_up", "ffn2_down"]
    late = ["ffn1_gate", "ffn1_up", "ffn1_down"]
    for n in early:
        adamw_big(n)
    done = sum(delta[n].reshape(-1)[:1] for n in early).reshape(1, 1)

    def arrived(started, own, which):
        return _sum_parts(with_own(_exchange_wait(started, done, False, f"g1_{which}_wait"), own), f"sum_{which}")

    grads["ffn1_gate"] = arrived(g1_wg, own_wg, "dwg").T[None]
    grads["ffn1_up"] = arrived(g1_wu, own_wu, "dwu").T[None]
    grads["ffn1_down"] = arrived(g1_wd, own_wd, "dwd")[None]
    for n in late:
        adamw_big(n)
    small_names = [n for n in names if n not in early + late]
    pack = lambda src: jnp.concatenate([_rows_of(src[n]) for n in small_names], axis=0)
    p_rows = [_rows_of(weights[n]).shape[0] for n in small_names]
    p_total = -(-sum(p_rows) // 8) * 8
    packed = [_pad_rows(pack(src), p_total) for src in (weights, grads, ms, vs)]
    d_s, m_s, v_s = _adamw(*packed, "adamw_small")
    off = 0
    for n, r in zip(small_names, p_rows):
        shp = weights[n].shape
        size = weights[n].size
        for dst, src in ((delta, d_s), (new_m, m_s), (new_v, v_s)):
            dst[n] = src[off:off + r].reshape(-1)[:size].reshape(shp)
        off += r

    return (loss, grad_x[None], *[grads[n] for n in names], *[delta[n] for n in names],
            *[new_m[n] for n in names], *[new_v[n] for n in names])
```

```python
import functools

import jax
import jax.numpy as jnp
from jax import lax
from jax.experimental import pallas as pl
from jax.experimental.pallas import tpu as pltpu

F32 = jnp.float32
MXU_DTYPE = jnp.bfloat16
WIRE_DTYPE = jnp.bfloat16
EPS = 1e-6
N_DEV = 8
GDN_HEADS = 4
HEAD_DIM = 128
GDN_WIDTH = GDN_HEADS * HEAD_DIM
POOL_WINDOWS = (2, 4, 8, 16)
POOL_GROUPS = len(POOL_WINDOWS)
POOL_WIDTH = 512
CONV_K = 4
CHUNK = 64
QKV_WIDTH = 3 * GDN_WIDTH
D_IN = 4 * GDN_WIDTH + 2 * GDN_HEADS + POOL_WIDTH
D_IN_PAD = 4 * GDN_WIDTH + POOL_WIDTH + 128
COL_Z = QKV_WIDTH
COL_P = 4 * GDN_WIDTH
COL_BA = 4 * GDN_WIDTH + POOL_WIDTH
N_MOD = 9
HALO = 16
VMEM_LIMIT = 56 * 1024 * 1024
ADAM_LR, ADAM_B1, ADAM_B2, ADAM_EPS, ADAM_WD, ADAM_STEP = 0.001, 0.9, 0.999, 1e-08, 0.01, 10
FFN_TOKEN_TILE = 512
FFN_HIDDEN_TILE = 1408
CHUNKS_PER_STEP = 4
SCAN_CHUNKS_PER_STEP = 4

NT = (((1,), (1,)), ((), ()))
NN = (((1,), (0,)), ((), ()))
TN = (((0,), (0,)), ((), ()))


def _params(*sem):
    return pltpu.CompilerParams(dimension_semantics=tuple(sem), vmem_limit_bytes=VMEM_LIMIT)


def _dot(a, b, dims):
    return lax.dot_general(a, b, dims, preferred_element_type=F32)


def _mdot(a, b, dims):
    return _dot(a.astype(MXU_DTYPE), b.astype(MXU_DTYPE), dims)


def _split(a):
    hi = a.astype(jnp.bfloat16)
    return hi, (a - hi.astype(F32)).astype(jnp.bfloat16)


def _dot3(a, b, dims):
    (ah, al), (bh, bl) = a, b
    return (_dot(al, bh, dims) + _dot(ah, bl, dims)) + _dot(ah, bh, dims)


def _sigmoid(v):
    return 1.0 / (1.0 + jnp.exp(-v))


def _softplus(v):
    return jnp.maximum(v, 0.0) + jnp.log(1.0 + jnp.exp(-jnp.abs(v)))


def _shift_rows(v, s):
    n = v.shape[0]
    s = s % n
    return v if s == 0 else pltpu.roll(v, s, 0)


def _tile(n, want):
    t = min(n, want)
    while n % t:
        t //= 2
    return t


def _all_gather(block, name, dep=None):
    shape, dtype = block.shape, block.dtype

    def body(x_ref, *refs):
        out_ref, send_sems, recv_sems, local_sem = refs[-4:]
        x, y, c = lax.axis_index("x"), lax.axis_index("y"), lax.axis_index("c")
        me, sibling = (x, y, c), (x, y, 1 - c)
        chips = [(1 - x, y), (x, 1 - y), (1 - x, 1 - y)]

        def rows(px, py, pc):
            return out_ref.at[4 * px + 2 * py + pc]

        def copy(k, blk, to, src=None):
            return pltpu.make_async_remote_copy(
                src_ref=rows(*blk) if src is None else src, dst_ref=rows(*blk),
                send_sem=send_sems.at[k], recv_sem=recv_sems.at[k],
                device_id=to, device_id_type=pl.DeviceIdType.MESH)

        mine = pltpu.make_async_copy(x_ref, rows(*me), local_sem)
        mine.start()
        first = [copy(0, me, sibling, src=x_ref)]
        first += [copy(1 + j, me, (*chip, c), src=x_ref) for j, chip in enumerate(chips)]
        for cp in first:
            cp.start()
        passed = [copy(4 + j, (*chip, c), sibling) for j, chip in enumerate(chips)]
        for j, chip in enumerate(chips):
            copy(1 + j, (*chip, c), me).wait_recv()
            passed[j].start()
        copy(0, sibling, me).wait_recv()
        for j, chip in enumerate(chips):
            copy(4 + j, (*chip, 1 - c), me).wait_recv()
        for cp in first + passed:
            cp.wait_send()
        mine.wait()

    return pl.pallas_call(
        body, name=name,
        out_shape=jax.ShapeDtypeStruct((N_DEV,) + shape, dtype),
        in_specs=[pl.BlockSpec(memory_space=pltpu.HBM)] + [pl.BlockSpec(memory_space=pl.ANY)] * (dep is not None),
        out_specs=pl.BlockSpec(memory_space=pltpu.HBM),
        scratch_shapes=[pltpu.SemaphoreType.DMA((7,)), pltpu.SemaphoreType.DMA((7,)),
                        pltpu.SemaphoreType.DMA(())],
    )(*([block] if dep is None else [block, dep]))


_HBM = pl.BlockSpec(memory_space=pltpu.HBM)
_SEM = pl.BlockSpec(memory_space=pltpu.SEMAPHORE)
_ANY = pl.BlockSpec(memory_space=pl.ANY)
_EFFECT = pltpu.SideEffectType.DATAFLOW_SIDE_EFFECTING
_FLIPS = [(0, 0, 1), (0, 1, 0), (0, 1, 1), (1, 0, 0), (1, 0, 1), (1, 1, 0), (1, 1, 1)]


def _peers():
    x, y, c = lax.axis_index("x"), lax.axis_index("y"), lax.axis_index("c")
    return 4 * x + 2 * y + c, [(1 - x if fx else x, 1 - y if fy else y, 1 - c if fc else c)
                               for fx, fy, fc in _FLIPS]


def _exchange_start(src, gather, dep, name):
    block = src.shape if gather else src.shape[1:]
    land = (N_DEV,) + tuple(block)

    def body(src_ref, land_ref, dep_ref, send_sems, recv_sems, src_thru, land_thru, token):
        me, peers = _peers()
        for k, (px, py, pc) in enumerate(peers):
            pltpu.make_async_remote_copy(
                src_ref=src_ref if gather else src_ref.at[4 * px + 2 * py + pc], dst_ref=land_ref.at[me],
                send_sem=send_sems.at[k], recv_sem=recv_sems.at[k],
                device_id=(px, py, pc), device_id_type=pl.DeviceIdType.MESH).start()
        token[...] = jnp.zeros_like(token)

    return pl.pallas_call(
        body, name=name,
        out_shape=(pltpu.SemaphoreType.DMA((7,)), pltpu.SemaphoreType.DMA((7,)),
                   pltpu.HBM(src.shape, src.dtype), pltpu.HBM(land, src.dtype),
                   jax.ShapeDtypeStruct((8, 128), F32)),
        in_specs=(_HBM, _HBM, _ANY),
        out_specs=(_SEM, _SEM, _HBM, _HBM, pl.BlockSpec(memory_space=pltpu.VMEM)),
        input_output_aliases={0: 2, 1: 3},
        compiler_params=pltpu.CompilerParams(has_side_effects=_EFFECT),
    )(pltpu.with_memory_space_constraint(src, pltpu.HBM),
      pltpu.with_memory_space_constraint(lax.empty(land, src.dtype), pltpu.HBM), dep)


def _exchange_wait(started, after, gather, name):
    send_sems, recv_sems, src_thru, land_thru, _ = started

    def body(src_ref, land_ref, send_sems, recv_sems, after_ref, src_dead, got_ref):
        _, peers = _peers()
        for k, peer in enumerate(peers):
            copy = pltpu.make_async_remote_copy(
                src_ref=src_ref if gather else src_ref.at[0], dst_ref=land_ref.at[0],
                send_sem=send_sems.at[k], recv_sem=recv_sems.at[k],
                device_id=peer, device_id_type=pl.DeviceIdType.MESH)
            copy.wait_send()
            copy.wait_recv()

    return pl.pallas_call(
        body, name=name,
        out_shape=(pltpu.HBM(src_thru.shape, src_thru.dtype), pltpu.HBM(land_thru.shape, land_thru.dtype)),
        in_specs=(_HBM, _HBM, _SEM, _SEM, _ANY), out_specs=(_HBM, _HBM),
        input_output_aliases={0: 0, 1: 1},
        compiler_params=pltpu.CompilerParams(has_side_effects=_EFFECT),
    )(src_thru, land_thru, send_sems, recv_sems, after)[1]


def _matmul(pairs, dims, out_dtype, name, tm=512, tn=512, tk=512, dep=None):
    a0, b0 = pairs[0]
    if dims == TN:
        K, M = a0.shape
    else:
        M, K = a0.shape
    N = b0.shape[0] if dims == NT else b0.shape[1]
    tm, tn, tk = _tile(M, tm), _tile(N, tn), _tile(K, tk)
    nk = K // tk
    n_pairs = len(pairs)
    n_in = 2 * n_pairs + (dep is not None)

    def body(*refs):
        out_ref = refs[n_in]

        def product():
            total = _dot(refs[0][...], refs[1][...], dims)
            for p in range(1, n_pairs):
                total += _dot(refs[2 * p][...], refs[2 * p + 1][...], dims)
            return total

        if nk == 1:
            out_ref[...] = product().astype(out_ref.dtype)
            return
        acc_ref = refs[n_in + 1]
        k = pl.program_id(2)

        @pl.when(k == 0)
        def _():
            acc_ref[...] = product()

        @pl.when((k > 0) & (k < nk - 1))
        def _():
            acc_ref[...] += product()

        @pl.when(k == nk - 1)
        def _():
            out_ref[...] = (acc_ref[...] + product()).astype(out_ref.dtype)

    if dims == TN:
        a_spec = pl.BlockSpec((tk, tm), lambda i, j, k: (k, i))
    else:
        a_spec = pl.BlockSpec((tm, tk), lambda i, j, k: (i, k))
    if dims == NT:
        b_spec = pl.BlockSpec((tn, tk), lambda i, j, k: (j, k))
    else:
        b_spec = pl.BlockSpec((tk, tn), lambda i, j, k: (k, j))
    args, specs = [], []
    for a, b in pairs:
        args += [a, b]
        specs += [a_spec, b_spec]
    if dep is not None:
        args.append(dep)
        specs.append(_ANY)
    return pl.pallas_call(
        body, name=name, grid=(M // tm, N // tn, nk),
        in_specs=specs, out_specs=pl.BlockSpec((tm, tn), lambda i, j, k: (i, j)),
        out_shape=jax.ShapeDtypeStruct((M, N), out_dtype),
        scratch_shapes=[pltpu.VMEM((tm, tn), F32)] * (nk > 1),
        compiler_params=_params("parallel", "parallel", "arbitrary"),
    )(*args)


def _vec_spec(d):
    return pl.BlockSpec((1, d), lambda i: (0, 0))


def _norm_mod(x, nw, scale, shift, name):
    T, D = x.shape
    tm = _tile(T, 512)

    def body(x_ref, nw_ref, sc_ref, sh_ref, h_ref):
        xf = x_ref[...]
        r = lax.rsqrt(jnp.mean(xf * xf, axis=-1, keepdims=True) + EPS)
        h_ref[...] = ((xf * r) * nw_ref[...] * (1.0 + sc_ref[...]) + sh_ref[...]).astype(h_ref.dtype)

    row = pl.BlockSpec((tm, D), lambda i: (i, 0))
    return pl.pallas_call(
        body, name=name, grid=(T // tm,),
        in_specs=[row, _vec_spec(D), _vec_spec(D), _vec_spec(D)], out_specs=row,
        out_shape=jax.ShapeDtypeStruct((T, D), MXU_DTYPE),
        compiler_params=_params("parallel"),
    )(x, nw, scale, shift)


def _resid_norm_mod(x, y, gate, coef, nw, scale, shift, name):
    T, D = x.shape
    tm = _tile(T, 512)

    def body(x_ref, y_ref, g_ref, nw_ref, sc_ref, sh_ref, xo_ref, h_ref):
        xf = x_ref[...] + (coef * g_ref[...]) * y_ref[...]
        xo_ref[...] = xf
        r = lax.rsqrt(jnp.mean(xf * xf, axis=-1, keepdims=True) + EPS)
        h_ref[...] = ((xf * r) * nw_ref[...] * (1.0 + sc_ref[...]) + sh_ref[...]).astype(h_ref.dtype)

    row = pl.BlockSpec((tm, D), lambda i: (i, 0))
    return pl.pallas_call(
        body, name=name, grid=(T // tm,),
        in_specs=[row, row, _vec_spec(D), _vec_spec(D), _vec_spec(D), _vec_spec(D)],
        out_specs=[row, row],
        out_shape=[jax.ShapeDtypeStruct((T, D), F32), jax.ShapeDtypeStruct((T, D), MXU_DTYPE)],
        compiler_params=_params("parallel"),
    )(x, y, gate, nw, scale, shift)


def _norm_bwd(dh, x, nw, scale, dres, name, produced_by=None):
    T, D = x.shape
    tm = _tile(T, 512)
    n_in = 5 if produced_by is None else 7

    def body(*refs):
        dh_ref, x_ref, nw_ref, sc_ref, dr_ref = refs[:5]
        dx_ref, dnw_ref, dsc_ref, dsh_ref = refs[n_in:n_in + 4]

        @pl.when(pl.program_id(0) == 0)
        def _():
            dnw_ref[...] = jnp.zeros_like(dnw_ref)
            dsc_ref[...] = jnp.zeros_like(dsc_ref)
            dsh_ref[...] = jnp.zeros_like(dsh_ref)
            if produced_by is not None:
                refs[n_in + 5][...] = jnp.zeros_like(refs[n_in + 5])

        xf, dh_ = x_ref[...], dh_ref[...]
        r = lax.rsqrt(jnp.mean(xf * xf, axis=-1, keepdims=True) + EPS)
        xn = xf * r
        one_sc = 1.0 + sc_ref[...]
        dsh_ref[...] += jnp.sum(dh_, axis=0, keepdims=True)
        t = dh_ * xn
        dsc_ref[...] += jnp.sum(t, axis=0, keepdims=True) * nw_ref[...]
        dnw_ref[...] += jnp.sum(t, axis=0, keepdims=True) * one_sc
        dxn = dh_ * (nw_ref[...] * one_sc)
        dx = dr_ref[...] + r * (dxn - xn * jnp.mean(dxn * xn, axis=-1, keepdims=True))
        dx_ref[...] = dx
        if produced_by is not None:
            y_ref, g_ref, dy_ref, dg_ref = refs[5], refs[6], refs[n_in + 4], refs[n_in + 5]
            dy_ref[...] = ((produced_by[2] * g_ref[...]) * dx).astype(dy_ref.dtype)
            dg_ref[...] += produced_by[2] * jnp.sum(dx * y_ref[...], axis=0, keepdims=True)

    row = pl.BlockSpec((tm, D), lambda i: (i, 0))
    vec = _vec_spec(D)
    vec_out = jax.ShapeDtypeStruct((1, D), F32)
    args, in_specs = [dh, x, nw, scale, dres], [row, row, vec, vec, row]
    out_specs, out_shape = [row, vec, vec, vec], [jax.ShapeDtypeStruct((T, D), F32), vec_out, vec_out, vec_out]
    if produced_by is not None:
        args += [produced_by[0], produced_by[1]]
        in_specs += [row, vec]
        out_specs += [row, vec]
        out_shape += [jax.ShapeDtypeStruct((T, D), MXU_DTYPE), vec_out]
    return pl.pallas_call(
        body, name=name, grid=(T // tm,),
        in_specs=in_specs, out_specs=out_specs, out_shape=out_shape,
        compiler_params=_params("arbitrary"),
    )(*args)


def _final_loss(x, y, gate, fw, target, name):
    T, D = x.shape
    tm = _tile(T, 512)

    def body(x_ref, y_ref, g_ref, fw_ref, t_ref, loss_ref, dx_ref, dfw_ref, dy_ref, dg_ref):
        @pl.when(pl.program_id(0) == 0)
        def _():
            loss_ref[...] = jnp.zeros_like(loss_ref)
            dfw_ref[...] = jnp.zeros_like(dfw_ref)
            dg_ref[...] = jnp.zeros_like(dg_ref)

        yy = y_ref[...]
        xf = x_ref[...] + (0.5 * g_ref[...]) * yy
        r = lax.rsqrt(jnp.mean(xf * xf, axis=-1, keepdims=True) + EPS)
        xn = xf * r
        err = xn * fw_ref[...] - t_ref[...]
        per_tok = jnp.mean(err * err, axis=-1, keepdims=True)
        loss_ref[...] += 0.5 * jnp.sum(per_tok, axis=0, keepdims=True)
        dy = err * (1.0 / D)
        dfw_ref[...] += jnp.sum(dy * xn, axis=0, keepdims=True)
        dxn = dy * fw_ref[...]
        dx = r * (dxn - xn * jnp.mean(dxn * xn, axis=-1, keepdims=True))
        dx_ref[...] = dx
        dy_ref[...] = ((0.5 * g_ref[...]) * dx).astype(dy_ref.dtype)
        dg_ref[...] += 0.5 * jnp.sum(dx * yy, axis=0, keepdims=True)

    row = pl.BlockSpec((tm, D), lambda i: (i, 0))
    vec = _vec_spec(D)
    return pl.pallas_call(
        body, name=name, grid=(T // tm,),
        in_specs=[row, row, vec, vec, row],
        out_specs=[pl.BlockSpec((1, 128), lambda i: (0, 0)), row, vec, row, vec],
        out_shape=[jax.ShapeDtypeStruct((1, 128), F32), jax.ShapeDtypeStruct((T, D), F32),
                   jax.ShapeDtypeStruct((1, D), F32), jax.ShapeDtypeStruct((T, D), MXU_DTYPE),
                   jax.ShapeDtypeStruct((1, D), F32)],
        compiler_params=_params("arbitrary"),
    )(x, y, gate, fw, target)


def _resident(shape):
    return pl.BlockSpec(shape, lambda i: (0,) * len(shape), pipeline_mode=pl.Buffered(1))


def _ffn_fwd(h, wg_t, wu_t, wd, name):
    T, D = h.shape
    Fdim = wd.shape[0]
    tm, tf = _tile(T, FFN_TOKEN_TILE), _tile(Fdim, FFN_HIDDEN_TILE)

    def body(h_ref, wg_ref, wu_ref, wd_ref, y_ref, g_ref, u_ref, a_ref):
        hh = h_ref[...]
        y = None
        for k in range(Fdim // tf):
            ks = slice(k * tf, (k + 1) * tf)
            g = _dot(hh, wg_ref[ks, :], NT)
            u = _dot(hh, wu_ref[ks, :], NT)
            a = ((g * _sigmoid(g)) * u).astype(a_ref.dtype)
            g_ref[:, ks] = g.astype(g_ref.dtype)
            u_ref[:, ks] = u.astype(u_ref.dtype)
            a_ref[:, ks] = a
            part = _dot(a, wd_ref[ks, :], NN)
            y = part if y is None else y + part
        y_ref[...] = y

    hrow = pl.BlockSpec((tm, D), lambda i: (i, 0))
    frow = pl.BlockSpec((tm, Fdim), lambda i: (i, 0))
    return pl.pallas_call(
        body, name=name, grid=(T // tm,),
        in_specs=[hrow, _resident((Fdim, D)), _resident((Fdim, D)), _resident((Fdim, D))],
        out_specs=[hrow, frow, frow, frow],
        out_shape=[jax.ShapeDtypeStruct((T, D), F32)] + [jax.ShapeDtypeStruct((T, Fdim), MXU_DTYPE)] * 3,
        compiler_params=_params("parallel"),
    )(h, wg_t, wu_t, wd)


def _ffn_bwd(dy, wd, g, u, wg_t, wu_t, name):
    T, D = dy.shape
    Fdim = wd.shape[0]
    tm, tf = _tile(T, FFN_TOKEN_TILE), _tile(Fdim, FFN_HIDDEN_TILE)

    def body(dy_ref, wd_ref, g_ref, u_ref, wg_ref, wu_ref, dg_ref, du_ref, dh_ref):
        dyy = dy_ref[...]
        dh = None
        for k in range(Fdim // tf):
            ks = slice(k * tf, (k + 1) * tf)
            da = _dot(dyy, wd_ref[ks, :], NT)
            gg = g_ref[:, ks].astype(F32)
            sig = _sigmoid(gg)
            dg = (da * u_ref[:, ks].astype(F32) * (sig * (1.0 + gg * (1.0 - sig)))).astype(dg_ref.dtype)
            du = (da * (gg * sig)).astype(du_ref.dtype)
            dg_ref[:, ks] = dg
            du_ref[:, ks] = du
            part = _dot(dg, wg_ref[ks, :], NN) + _dot(du, wu_ref[ks, :], NN)
            dh = part if dh is None else dh + part
        dh_ref[...] = dh

    row = pl.BlockSpec((tm, D), lambda i: (i, 0))
    frow = pl.BlockSpec((tm, Fdim), lambda i: (i, 0))
    wres = _resident((Fdim, D))
    return pl.pallas_call(
        body, name=name, grid=(T // tm,),
        in_specs=[row, wres, frow, frow, wres, wres],
        out_specs=[frow, frow, row],
        out_shape=[jax.ShapeDtypeStruct((T, Fdim), MXU_DTYPE)] * 2 + [jax.ShapeDtypeStruct((T, D), F32)],
        compiler_params=_params("parallel"),
    )(dy, wd, g, u, wg_t, wu_t)


def _conv_act(window, w):
    y = window * w[CONV_K - 1:CONV_K, :]
    for j in range(CONV_K - 1):
        y += _shift_rows(window, CONV_K - 1 - j) * w[j:j + 1, :]
    return y


def _gdn_prep(proj, conv_w, a_log_l, dt_bias_l, name):
    T = proj.shape[0]
    tm = _tile(T, 256)
    hb = tm // 8

    def body(cur_ref, halo_ref, ba_ref, w_ref, al_ref, dtb_ref, qkv_ref, bg_ref):
        i = pl.program_id(0)
        halo = jnp.where(i == 0, 0.0, halo_ref[...])
        window = jnp.concatenate([halo, cur_ref[...]], axis=0)
        y = _conv_act(window, w_ref[...])[8:, :]
        act = y * _sigmoid(y)
        for hh in range(3 * GDN_HEADS):
            blk = act[:, hh * HEAD_DIM:(hh + 1) * HEAD_DIM]
            if hh < 2 * GDN_HEADS:
                rn = lax.rsqrt(jnp.sum(blk * blk, axis=-1, keepdims=True) + EPS)
                blk = blk * rn
                if hh < GDN_HEADS:
                    blk = blk * (HEAD_DIM ** -0.5)
            qkv_ref[:, hh * HEAD_DIM:(hh + 1) * HEAD_DIM] = blk
        ba = ba_ref[...]
        lane = lax.broadcasted_iota(jnp.int32, ba.shape, 1)
        beta = _sigmoid(ba)
        g = -jnp.exp(al_ref[...]) * _softplus(ba + dtb_ref[...])
        bg_ref[...] = jnp.where(lane < GDN_HEADS, beta, jnp.where(lane < 2 * GDN_HEADS, g, 0.0))

    return pl.pallas_call(
        body, name=name, grid=(T // tm,),
        in_specs=[pl.BlockSpec((tm, QKV_WIDTH), lambda i: (i, 0)),
                  pl.BlockSpec((8, QKV_WIDTH), lambda i: (jnp.maximum(i * hb - 1, 0), 0)),
                  pl.BlockSpec((tm, 128), lambda i: (i, COL_BA // 128)),
                  pl.BlockSpec((CONV_K, QKV_WIDTH), lambda i: (0, 0)),
                  pl.BlockSpec((1, 128), lambda i: (0, 0)), pl.BlockSpec((1, 128), lambda i: (0, 0))],
        out_specs=[pl.BlockSpec((tm, QKV_WIDTH), lambda i: (i, 0)), pl.BlockSpec((tm, 128), lambda i: (i, 0))],
        out_shape=[jax.ShapeDtypeStruct((T, QKV_WIDTH), F32), jax.ShapeDtypeStruct((T, 128), F32)],
        compiler_params=_params("parallel"),
    )(proj, proj, proj, conv_w, a_log_l, dt_bias_l)


def _chunk_cumsum(v, reverse=False):
    row = lax.broadcasted_iota(jnp.int32, v.shape, 0)
    s = 1
    while s < CHUNK:
        if reverse:
            v = v + jnp.where(row < CHUNK - s, _shift_rows(v, -s), 0.0)
        else:
            v = v + jnp.where(row >= s, _shift_rows(v, s), 0.0)
        s *= 2
    return v


def _row_form(cols):
    padded = jnp.concatenate([cols, jnp.zeros((128 - CHUNK, 128), F32)], axis=0)
    return padded.T[:, :CHUNK]


def _chunk_masks():
    ri = lax.broadcasted_iota(jnp.int32, (CHUNK, CHUNK), 0)
    ci = lax.broadcasted_iota(jnp.int32, (CHUNK, CHUNK), 1)
    return ri >= ci, ri > ci, (ri == ci).astype(F32)


def _unit_lower_inverses(ms, eye):
    rs = [eye - m for m in ms]
    ps = [_split(-m) for m in ms]
    s = 2
    while s < CHUNK:
        ps = [_split(_dot3(p, p, NN)) for p in ps]
        r_parts = [_split(r) for r in rs]
        rs = [r + _dot3(p, rp, NN) for r, p, rp in zip(rs, ps, r_parts)]
        s *= 2
    return rs


def _head_elementwise(k, beta, gc, gcr, causal):
    decay = jnp.where(causal, jnp.exp(jnp.where(causal, gc - gcr, 0.0)), 0.0)
    return decay, k * beta, jnp.exp(gc)


def _head_slices(hh):
    return (slice(hh * HEAD_DIM, (hh + 1) * HEAD_DIM),
            slice(GDN_WIDTH + hh * HEAD_DIM, GDN_WIDTH + (hh + 1) * HEAD_DIM),
            slice(2 * GDN_WIDTH + hh * HEAD_DIM, 2 * GDN_WIDTH + (hh + 1) * HEAD_DIM))


def _gdn_chunk_fwd(qkv, bg, name):
    T = qkv.shape[0]
    cb = _tile(T // CHUNK, CHUNKS_PER_STEP)
    rows = cb * CHUNK

    def body(qkv_ref, bg_ref, tinv_ref, u_ref, w_ref, qd_ref, kd_ref, p_ref, cd_ref):
        masks = _chunk_masks()
        causal, strict, eye = masks
        heads = []
        for ci in range(cb):
            rs = slice(ci * CHUNK, (ci + 1) * CHUNK)
            bgv = bg_ref[rs, :]
            gc_all = _chunk_cumsum(bgv)
            gc_rows = _row_form(gc_all)
            cd_ref[rs, :] = jnp.exp(jnp.broadcast_to(gc_all[CHUNK - 1:CHUNK, :], (CHUNK, 128)))
            for hh in range(GDN_HEADS):
                qs, ks, vs = _head_slices(hh)
                q, k, v = qkv_ref[rs, qs], qkv_ref[rs, ks], qkv_ref[rs, vs]
                beta = bgv[:, hh:hh + 1]
                gc = gc_all[:, GDN_HEADS + hh:GDN_HEADS + hh + 1]
                decay, kb, eg = _head_elementwise(k, beta, gc, gc_rows[GDN_HEADS + hh:GDN_HEADS + hh + 1, :], causal)
                hs = slice(hh * HEAD_DIM, (hh + 1) * HEAD_DIM)
                cs = slice(hh * CHUNK, (hh + 1) * CHUNK)
                qd_ref[rs, hs] = (q * eg).astype(qd_ref.dtype)
                kd_ref[rs, hs] = (k * jnp.exp(gc[CHUNK - 1:CHUNK, :] - gc)).astype(kd_ref.dtype)
                heads.append((rs, hs, cs, q, k, v * beta, kb, kb * eg, decay))
        kks = [_mdot(kb, k, NT) for (_, _, _, _, k, _, kb, _, _) in heads]
        qks = [_mdot(q, k, NT) for (_, _, _, q, k, _, _, _, _) in heads]
        tinvs = _unit_lower_inverses([jnp.where(strict, kk * hd[8], 0.0) for kk, hd in zip(kks, heads)], eye)
        t_parts = [_split(t) for t in tinvs]
        us = [_dot3(tp, _split(hd[5]), NN) for tp, hd in zip(t_parts, heads)]
        ws = [_dot3(tp, _split(hd[7]), NN) for tp, hd in zip(t_parts, heads)]
        for hd, tinv, u, w, qk in zip(heads, tinvs, us, ws, qks):
            rs, hs, cs = hd[0], hd[1], hd[2]
            tinv_ref[rs, cs] = tinv
            u_ref[rs, hs] = u
            w_ref[rs, hs] = w.astype(w_ref.dtype)
            p_ref[rs, cs] = jnp.where(causal, qk * hd[8], 0.0).astype(p_ref.dtype)

    def spec(width):
        return pl.BlockSpec((rows, width), lambda n: (n, 0))

    hw, cw = GDN_WIDTH, GDN_HEADS * CHUNK
    return pl.pallas_call(
        body, name=name, grid=(T // rows,),
        in_specs=[spec(QKV_WIDTH), spec(128)],
        out_specs=[spec(cw), spec(hw), spec(hw), spec(hw), spec(hw), spec(cw), spec(128)],
        out_shape=[jax.ShapeDtypeStruct((T, cw), F32), jax.ShapeDtypeStruct((T, hw), F32),
                   jax.ShapeDtypeStruct((T, hw), MXU_DTYPE), jax.ShapeDtypeStruct((T, hw), MXU_DTYPE),
                   jax.ShapeDtypeStruct((T, hw), MXU_DTYPE), jax.ShapeDtypeStruct((T, cw), MXU_DTYPE),
                   jax.ShapeDtypeStruct((T, 128), F32)],
        compiler_params=_params("parallel"),
    )(qkv, bg)


def _gdn_scan_fwd(u, w, qd, kd, p, cd, name):
    T = u.shape[0]
    cb = _tile(T // CHUNK, SCAN_CHUNKS_PER_STEP)
    rows = cb * CHUNK

    def body(u_ref, w_ref, qd_ref, kd_ref, p_ref, cd_ref, o_ref, s_all_ref, vn_ref, s_ref):
        @pl.when(pl.program_id(0) == 0)
        def _():
            s_ref[...] = jnp.zeros_like(s_ref)

        hss = [slice(hh * HEAD_DIM, (hh + 1) * HEAD_DIM) for hh in range(GDN_HEADS)]
        css = [slice(hh * CHUNK, (hh + 1) * CHUNK) for hh in range(GDN_HEADS)]
        s_cur = [s_ref[hh] for hh in range(GDN_HEADS)]
        for ci in range(cb):
            rs = slice(ci * CHUNK, (ci + 1) * CHUNK)
            for hh in range(GDN_HEADS):
                s_all_ref[ci * GDN_WIDTH + hh * HEAD_DIM:ci * GDN_WIDTH + (hh + 1) * HEAD_DIM, :] = s_cur[hh]
            s_ms = [s.astype(MXU_DTYPE) for s in s_cur]
            w_s = [_dot(w_ref[rs, hs], s_m, NN) for hs, s_m in zip(hss, s_ms)]
            q_s = [_dot(qd_ref[rs, hs], s_m, NN) for hs, s_m in zip(hss, s_ms)]
            v_ms = [(u_ref[rs, hs] - ws_).astype(MXU_DTYPE) for hs, ws_ in zip(hss, w_s)]
            k_v = [_dot(kd_ref[rs, hs], v_m, TN) for hs, v_m in zip(hss, v_ms)]
            p_v = [_dot(p_ref[rs, cs], v_m, NN) for cs, v_m in zip(css, v_ms)]
            for hh in range(GDN_HEADS):
                vn_ref[rs, hss[hh]] = v_ms[hh]
                o_ref[rs, hss[hh]] = q_s[hh] + p_v[hh]
                c_dec = cd_ref[ci * CHUNK:ci * CHUNK + 1, GDN_HEADS + hh:GDN_HEADS + hh + 1]
                s_cur[hh] = s_cur[hh] * c_dec + k_v[hh]
        for hh in range(GDN_HEADS):
            s_ref[hh] = s_cur[hh]

    def spec(width):
        return pl.BlockSpec((rows, width), lambda n: (n, 0))

    hw, cw = GDN_WIDTH, GDN_HEADS * CHUNK
    return pl.pallas_call(
        body, name=name, grid=(T // rows,),
        in_specs=[spec(hw), spec(hw), spec(hw), spec(hw), spec(cw), spec(128)],
        out_specs=[spec(hw), pl.BlockSpec((cb * GDN_WIDTH, HEAD_DIM), lambda n: (n, 0)), spec(hw)],
        out_shape=[jax.ShapeDtypeStruct((T, hw), F32),
                   jax.ShapeDtypeStruct((T // CHUNK * GDN_WIDTH, HEAD_DIM), F32),
                   jax.ShapeDtypeStruct((T, hw), MXU_DTYPE)],
        scratch_shapes=[pltpu.VMEM((GDN_HEADS, HEAD_DIM, HEAD_DIM), F32)],
        compiler_params=_params("arbitrary"),
    )(u, w, qd, kd, p, cd)


def _gdn_scan_bwd(do, w, qd, kd, p, cd, s_all, vn, name):
    T = do.shape[0]
    cb = _tile(T // CHUNK, SCAN_CHUNKS_PER_STEP)
    rows = cb * CHUNK
    n_steps = T // rows

    def body(do_ref, w_ref, qd_ref, kd_ref, p_ref, cd_ref, s_all_ref, vn_ref,
             dvn_ref, dw_ref, dqd_ref, dkd_ref, dp_ref, dcd_ref, ds_ref):
        @pl.when(pl.program_id(0) == 0)
        def _():
            ds_ref[...] = jnp.zeros_like(ds_ref)

        causal, _, _ = _chunk_masks()
        lane = lax.broadcasted_iota(jnp.int32, (CHUNK, 128), 1)
        heads = range(GDN_HEADS)
        hss = [slice(hh * HEAD_DIM, (hh + 1) * HEAD_DIM) for hh in heads]
        css = [slice(hh * CHUNK, (hh + 1) * CHUNK) for hh in heads]
        ds_cur = [ds_ref[hh] for hh in heads]
        for ci in reversed(range(cb)):
            rs = slice(ci * CHUNK, (ci + 1) * CHUNK)
            ds_ms = [d.astype(MXU_DTYPE) for d in ds_cur]
            s_olds = [s_all_ref[ci * GDN_WIDTH + hh * HEAD_DIM:ci * GDN_WIDTH + (hh + 1) * HEAD_DIM, :] for hh in heads]
            s_ms = [s.astype(MXU_DTYPE) for s in s_olds]
            do_ms = [do_ref[rs, hs].astype(MXU_DTYPE) for hs in hss]
            p_do = [_dot(p_ref[rs, cs], do_m, TN) for cs, do_m in zip(css, do_ms)]
            k_ds = [_dot(kd_ref[rs, hs], ds_m, NN) for hs, ds_m in zip(hss, ds_ms)]
            q_do = [_dot(qd_ref[rs, hs], do_m, TN) for hs, do_m in zip(hss, do_ms)]
            dqds = [_dot(do_m, s_m, NT) for do_m, s_m in zip(do_ms, s_ms)]
            dkds = [_dot(vn_ref[rs, hs], ds_m, NT) for hs, ds_m in zip(hss, ds_ms)]
            dps = [_dot(do_m, vn_ref[rs, hs], NT) for hs, do_m in zip(hss, do_ms)]
            dv_news = [a + b for a, b in zip(p_do, k_ds)]
            dvn_ms = [d.astype(MXU_DTYPE) for d in dv_news]
            w_dv = [_dot(w_ref[rs, hs], dvn_m, TN) for hs, dvn_m in zip(hss, dvn_ms)]
            dws = [_dot(dvn_m, s_m, NT) for dvn_m, s_m in zip(dvn_ms, s_ms)]
            dcd_tile = jnp.zeros((CHUNK, 128), F32)
            for hh in heads:
                dvn_ref[rs, hss[hh]] = dv_news[hh]
                dw_ref[rs, hss[hh]] = -dws[hh]
                dqd_ref[rs, hss[hh]] = dqds[hh]
                dkd_ref[rs, hss[hh]] = dkds[hh]
                dp_ref[rs, css[hh]] = jnp.where(causal, dps[hh], 0.0)
                dcd = jnp.sum(jnp.sum(s_olds[hh] * ds_cur[hh], axis=1, keepdims=True), axis=0, keepdims=True)
                dcd_tile = jnp.where(lane == GDN_HEADS + hh, dcd, dcd_tile)
                c_dec = cd_ref[ci * CHUNK:ci * CHUNK + 1, GDN_HEADS + hh:GDN_HEADS + hh + 1]
                ds_cur[hh] = c_dec * ds_cur[hh] + q_do[hh] - w_dv[hh]
            dcd_ref[rs, :] = dcd_tile
        for hh in heads:
            ds_ref[hh] = ds_cur[hh]

    def spec(width):
        return pl.BlockSpec((rows, width), lambda n: (n_steps - 1 - n, 0))

    hw, cw = GDN_WIDTH, GDN_HEADS * CHUNK
    return pl.pallas_call(
        body, name=name, grid=(n_steps,),
        in_specs=[spec(hw), spec(hw), spec(hw), spec(hw), spec(cw), spec(128),
                  pl.BlockSpec((cb * GDN_WIDTH, HEAD_DIM), lambda n: (n_steps - 1 - n, 0)), spec(hw)],
        out_specs=[spec(hw), spec(hw), spec(hw), spec(hw), spec(cw), spec(128)],
        out_shape=[jax.ShapeDtypeStruct((T, hw), F32)] * 4
        + [jax.ShapeDtypeStruct((T, cw), F32), jax.ShapeDtypeStruct((T, 128), F32)],
        scratch_shapes=[pltpu.VMEM((GDN_HEADS, HEAD_DIM, HEAD_DIM), F32)],
        compiler_params=_params("arbitrary"),
    )(do, w, qd, kd, p, cd, s_all, vn)


def _gdn_chunk_bwd(qkv, bg, tinv_all, u, w, dvn, dw, dqd, dkd, dp, dcd, name):
    T = qkv.shape[0]
    cb = _tile(T // CHUNK, CHUNKS_PER_STEP)
    rows = cb * CHUNK

    def body(qkv_ref, bg_ref, tinv_ref, u_ref, w_ref, dvn_ref, dw_ref, dqd_ref, dkd_ref, dp_ref, dcd_ref,
             dqkv_ref, dbg_ref):
        masks = _chunk_masks()
        causal, strict, _ = masks
        lane = lax.broadcasted_iota(jnp.int32, (CHUNK, 128), 1)
        row = lax.broadcasted_iota(jnp.int32, (CHUNK, 128), 0)
        heads = []
        for ci in range(cb):
            rs = slice(ci * CHUNK, (ci + 1) * CHUNK)
            bgv = bg_ref[rs, :]
            gc_all = _chunk_cumsum(bgv)
            gc_rows = _row_form(gc_all)
            for hh in range(GDN_HEADS):
                qs, ks, vs = _head_slices(hh)
                q, k = qkv_ref[rs, qs], qkv_ref[rs, ks]
                beta = bgv[:, hh:hh + 1]
                gc = gc_all[:, GDN_HEADS + hh:GDN_HEADS + hh + 1]
                decay, kb, eg = _head_elementwise(k, beta, gc, gc_rows[GDN_HEADS + hh:GDN_HEADS + hh + 1, :], causal)
                heads.append(dict(ci=ci, hh=hh, rs=rs, hs=slice(hh * HEAD_DIM, (hh + 1) * HEAD_DIM),
                                  cs=slice(hh * CHUNK, (hh + 1) * CHUNK), q=q, k=k, beta=beta, gc=gc,
                                  decay=decay, kb=kb, eg=eg))
        for hd in heads:
            hd["t"] = _split(tinv_ref[hd["rs"], hd["cs"]])
        for hd in heads:
            hd["kk"] = _mdot(hd["kb"], hd["k"], NT)
            hd["qk"] = _mdot(hd["q"], hd["k"], NT)
        for hd in heads:
            hd["dvb"] = _dot3(hd["t"], _split(dvn_ref[hd["rs"], hd["hs"]]), TN)
            hd["dkbeg"] = _dot3(hd["t"], _split(dw_ref[hd["rs"], hd["hs"]]), TN)
        for hd in heads:
            rs, hs = hd["rs"], hd["hs"]
            da = -(_mdot(hd["dvb"], u_ref[rs, hs], NT) + _mdot(hd["dkbeg"], w_ref[rs, hs], NT))
            dm = jnp.where(strict, da, 0.0)
            dp_ = dp_ref[rs, hd["cs"]]
            hd["dkk"] = dm * hd["decay"]
            hd["dqk"] = dp_ * hd["decay"]
            hd["e"] = (hd["dkk"] * hd["kk"] + hd["dqk"] * hd["qk"])
        for hd in heads:
            hd["dkb"] = _mdot(hd["dkk"], hd["k"], NN)
            hd["dk"] = _mdot(hd["dkk"], hd["kb"], TN) + _mdot(hd["dqk"], hd["q"], TN)
            hd["dq"] = _mdot(hd["dqk"], hd["k"], NN)
            onehot = (lane == GDN_HEADS + hd["hh"]).astype(jnp.bfloat16)
            e_hi, e_lo = _split(hd["e"])
            hd["col_sums"] = _dot(e_lo, onehot, TN) + _dot(e_hi, onehot, TN)
        tiles = {}
        for hd in heads:
            ci, hh, rs, hs = hd["ci"], hd["hh"], hd["rs"], hd["hs"]
            qs, ks, vs = _head_slices(hh)
            q, k, beta, gc, eg, kb = hd["q"], hd["k"], hd["beta"], hd["gc"], hd["eg"], hd["kb"]
            v = qkv_ref[rs, vs]
            dqd_, dkd_ = dqd_ref[rs, hs], dkd_ref[rs, hs]
            gl = gc[CHUNK - 1:CHUNK, :]
            ek = jnp.exp(gl - gc)
            dkb = hd["dkb"] + hd["dkbeg"] * eg
            deg = jnp.sum(dqd_ * q, axis=1, keepdims=True) + jnp.sum(hd["dkbeg"] * kb, axis=1, keepdims=True)
            dek = jnp.sum(dkd_ * k, axis=1, keepdims=True)
            dcd_ = dcd_ref[ci * CHUNK:ci * CHUNK + 1, GDN_HEADS + hh:GDN_HEADS + hh + 1]
            dgl = jnp.sum(dek * ek, axis=0, keepdims=True) + dcd_ * jnp.exp(gl)
            dgc = jnp.sum(hd["e"], axis=1, keepdims=True) + deg * eg - dek * ek
            dbeta_tile, dgc_tile = tiles.get(ci, (jnp.zeros((CHUNK, 128), F32), jnp.zeros((CHUNK, 128), F32)))
            dgc_tile += jnp.where(lane == GDN_HEADS + hh, dgc, 0.0) - hd["col_sums"]
            dgc_tile += jnp.where((lane == GDN_HEADS + hh) & (row == CHUNK - 1), dgl, 0.0)
            dbeta = jnp.sum(dkb * k, axis=1, keepdims=True) + jnp.sum(hd["dvb"] * v, axis=1, keepdims=True)
            dbeta_tile += jnp.where(lane == hh, dbeta, 0.0)
            tiles[ci] = (dbeta_tile, dgc_tile)
            dqkv_ref[rs, qs] = hd["dq"] + dqd_ * eg
            dqkv_ref[rs, ks] = hd["dk"] + dkd_ * ek + dkb * beta
            dqkv_ref[rs, vs] = hd["dvb"] * beta
        for ci in range(cb):
            dbeta_tile, dgc_tile = tiles[ci]
            dbg_ref[ci * CHUNK:(ci + 1) * CHUNK, :] = dbeta_tile + _chunk_cumsum(dgc_tile, reverse=True)

    def spec(width):
        return pl.BlockSpec((rows, width), lambda n: (n, 0))

    hw, cw = GDN_WIDTH, GDN_HEADS * CHUNK
    return pl.pallas_call(
        body, name=name, grid=(T // rows,),
        in_specs=[spec(QKV_WIDTH), spec(128), spec(cw), spec(hw), spec(hw), spec(hw), spec(hw), spec(hw),
                  spec(hw), spec(cw), spec(128)],
        out_specs=[spec(QKV_WIDTH), spec(128)],
        out_shape=[jax.ShapeDtypeStruct((T, QKV_WIDTH), F32), jax.ShapeDtypeStruct((T, 128), F32)],
        compiler_params=_params("parallel"),
    )(qkv, bg, tinv_all, u, w, dvn, dw, dqd, dkd, dp, dcd)


def _pool_counts(i, tm, rows, offset):
    t = i * tm - offset + lax.broadcasted_iota(jnp.int32, (rows, 1), 0)
    return [jnp.minimum(t + 1, w).astype(F32) for w in POOL_WINDOWS]


def _window_sums(window, forward):
    sums, s, step = [], window, 1
    for _ in POOL_WINDOWS:
        s = s + _shift_rows(s, -step if forward else step)
        sums.append(s)
        step *= 2
    return sums


def _pooled(window, counts):
    sums = _window_sums(window, forward=False)
    out = []
    for gi in range(POOL_GROUPS):
        sl = slice(gi * 128, (gi + 1) * 128)
        out.append(sums[gi][HALO:, sl] / counts[gi] - window[HALO:, sl])
    return out


def _mix_post(o, proj, gdn_norm, pool_w, pool_scale, name):
    T = o.shape[0]
    tm = _tile(T, 256)
    hb = tm // HALO

    def body(o_ref, z_ref, p_ref, ph_ref, gn_ref, pw_ref, ps_ref, out_ref):
        i = pl.program_id(0)
        for hh in range(GDN_HEADS):
            sl = slice(hh * HEAD_DIM, (hh + 1) * HEAD_DIM)
            oh, zh = o_ref[:, sl], z_ref[:, sl]
            ro = lax.rsqrt(jnp.mean(oh * oh, axis=-1, keepdims=True) + EPS)
            out_ref[:, sl] = (((oh * ro) * gn_ref[...]) * (zh * _sigmoid(zh))).astype(out_ref.dtype)
        halo = jnp.where(i == 0, 0.0, ph_ref[...])
        window = jnp.concatenate([halo, p_ref[...]], axis=0)
        pooled = _pooled(window, _pool_counts(i, tm, tm, 0))
        for gi in range(POOL_GROUPS):
            pm = _mdot(pooled[gi], pw_ref[gi], NN)
            out_ref[:, GDN_WIDTH + gi * 128:GDN_WIDTH + (gi + 1) * 128] = (
                pm * ps_ref[:, gi * 128:(gi + 1) * 128]).astype(out_ref.dtype)

    return pl.pallas_call(
        body, name=name, grid=(T // tm,),
        in_specs=[pl.BlockSpec((tm, GDN_WIDTH), lambda i: (i, 0)),
                  pl.BlockSpec((tm, GDN_WIDTH), lambda i: (i, COL_Z // GDN_WIDTH)),
                  pl.BlockSpec((tm, POOL_WIDTH), lambda i: (i, COL_P // POOL_WIDTH)),
                  pl.BlockSpec((HALO, POOL_WIDTH), lambda i: (jnp.maximum(i * hb - 1, 0), COL_P // POOL_WIDTH)),
                  pl.BlockSpec((1, HEAD_DIM), lambda i: (0, 0)),
                  pl.BlockSpec((POOL_GROUPS, 128, 128), lambda i: (0, 0, 0)),
                  pl.BlockSpec((1, POOL_WIDTH), lambda i: (0, 0))],
        out_specs=pl.BlockSpec((tm, GDN_WIDTH + POOL_WIDTH), lambda i: (i, 0)),
        out_shape=jax.ShapeDtypeStruct((T, GDN_WIDTH + POOL_WIDTH), MXU_DTYPE),
        compiler_params=_params("parallel"),
    )(o, proj, proj, proj, gdn_norm, pool_w, pool_scale)


def _mix_post_bwd(dmix, o, proj, gdn_norm, pool_w, pool_scale, name):
    T = o.shape[0]
    tm = _tile(T, 256)
    hb = tm // HALO
    n_tiles = T // tm

    def body(dg_ref, dpo_ref, dpo_next_ref, o_ref, z_ref, p_ref, ph_ref, gn_ref, pw_ref, ps_ref,
             do_ref, dzp_ref, dgn_ref, dpw_ref, dps_ref):
        i = pl.program_id(0)

        @pl.when(i == 0)
        def _():
            dgn_ref[...] = jnp.zeros_like(dgn_ref)
            dpw_ref[...] = jnp.zeros_like(dpw_ref)
            dps_ref[...] = jnp.zeros_like(dps_ref)

        gn = gn_ref[...]
        dgn = jnp.zeros((1, HEAD_DIM), F32)
        for hh in range(GDN_HEADS):
            sl = slice(hh * HEAD_DIM, (hh + 1) * HEAD_DIM)
            oh, zh, dy = o_ref[:, sl], z_ref[:, sl], dg_ref[:, sl]
            ro = lax.rsqrt(jnp.mean(oh * oh, axis=-1, keepdims=True) + EPS)
            on = oh * ro
            sig = _sigmoid(zh)
            sz = zh * sig
            dzp_ref[:, sl] = (dy * (on * gn) * (sig * (1.0 + zh * (1.0 - sig)))).astype(dzp_ref.dtype)
            dgn += jnp.sum(dy * on * sz, axis=0, keepdims=True)
            don = dy * gn * sz
            do_ref[:, sl] = ro * (don - on * jnp.mean(don * on, axis=-1, keepdims=True))
        dgn_ref[...] += dgn

        halo = jnp.where(i == 0, 0.0, ph_ref[...])
        window = jnp.concatenate([halo, p_ref[...]], axis=0)
        counts = _pool_counts(i, tm, tm + HALO, 0)
        pooled = _pooled(window, [cn[:tm] for cn in counts])
        nxt = jnp.where(i == n_tiles - 1, 0.0, dpo_next_ref[...])
        dpo_w = jnp.concatenate([dpo_ref[...], nxt], axis=0)
        ps = ps_ref[...]
        dps = []
        scaled = []
        for gi in range(POOL_GROUPS):
            sl = slice(gi * 128, (gi + 1) * 128)
            dpm = dpo_w[:, sl] * ps[:, sl]
            pm = _mdot(pooled[gi], pw_ref[gi], NN)
            dps.append(jnp.sum(dpo_w[:tm, sl] * pm, axis=0, keepdims=True))
            dpw_ref[gi] += _mdot(pooled[gi], dpm[:tm], TN)
            dpooled = _mdot(dpm, pw_ref[gi], NT)
            scaled.append((dpooled, dpooled / counts[gi]))
        dps_ref[...] += jnp.concatenate(dps, axis=1)
        lead = _window_sums(jnp.concatenate([sc for _, sc in scaled], axis=1), forward=True)
        for gi in range(POOL_GROUPS):
            sl = slice(gi * 128, (gi + 1) * 128)
            dzp_ref[:, GDN_WIDTH + gi * 128:GDN_WIDTH + (gi + 1) * 128] = (
                lead[gi][:tm, sl] - scaled[gi][0][:tm]).astype(dzp_ref.dtype)

    last_halo = T // HALO - 1
    return pl.pallas_call(
        body, name=name, grid=(n_tiles,),
        in_specs=[pl.BlockSpec((tm, GDN_WIDTH), lambda i: (i, 0)),
                  pl.BlockSpec((tm, POOL_WIDTH), lambda i: (i, 1)),
                  pl.BlockSpec((HALO, POOL_WIDTH), lambda i: (jnp.minimum((i + 1) * hb, last_halo), 1)),
                  pl.BlockSpec((tm, GDN_WIDTH), lambda i: (i, 0)),
                  pl.BlockSpec((tm, GDN_WIDTH), lambda i: (i, COL_Z // GDN_WIDTH)),
                  pl.BlockSpec((tm, POOL_WIDTH), lambda i: (i, COL_P // POOL_WIDTH)),
                  pl.BlockSpec((HALO, POOL_WIDTH), lambda i: (jnp.maximum(i * hb - 1, 0), COL_P // POOL_WIDTH)),
                  pl.BlockSpec((1, HEAD_DIM), lambda i: (0, 0)),
                  pl.BlockSpec((POOL_GROUPS, 128, 128), lambda i: (0, 0, 0)),
                  pl.BlockSpec((1, POOL_WIDTH), lambda i: (0, 0))],
        out_specs=[pl.BlockSpec((tm, GDN_WIDTH), lambda i: (i, 0)),
                   pl.BlockSpec((tm, GDN_WIDTH + POOL_WIDTH), lambda i: (i, 0)),
                   pl.BlockSpec((1, HEAD_DIM), lambda i: (0, 0)),
                   pl.BlockSpec((POOL_GROUPS, 128, 128), lambda i: (0, 0, 0)),
                   pl.BlockSpec((1, POOL_WIDTH), lambda i: (0, 0))],
        out_shape=[jax.ShapeDtypeStruct((T, GDN_WIDTH), F32),
                   jax.ShapeDtypeStruct((T, GDN_WIDTH + POOL_WIDTH), MXU_DTYPE),
                   jax.ShapeDtypeStruct((1, HEAD_DIM), F32),
                   jax.ShapeDtypeStruct((POOL_GROUPS, 128, 128), F32),
                   jax.ShapeDtypeStruct((1, POOL_WIDTH), F32)],
        compiler_params=_params("arbitrary"),
    )(dmix, dmix, dmix, o, proj, proj, proj, gdn_norm, pool_w, pool_scale)


def _gdn_prep_bwd(proj, conv_w, a_log_l, dt_bias_l, dqkv, dbg, dzp, name):
    T = proj.shape[0]
    tm = _tile(T, 256)
    hb = tm // 8
    n_tiles = T // tm
    last_halo = T // 8 - 1

    def body(cur_ref, before_ref, after_ref, ba_ref, w_ref, al_ref, dtb_ref, dq_ref, dq_after_ref, dbg_ref,
             dzp_ref, dproj_ref, dw_ref, dal_ref, ddtb_ref):
        i = pl.program_id(0)

        @pl.when(i == 0)
        def _():
            dw_ref[...] = jnp.zeros_like(dw_ref)
            dal_ref[...] = jnp.zeros_like(dal_ref)
            ddtb_ref[...] = jnp.zeros_like(ddtb_ref)

        last = i == n_tiles - 1
        w = w_ref[...]
        before = jnp.where(i == 0, 0.0, before_ref[...])
        after = jnp.where(last, 0.0, after_ref[...])
        window = jnp.concatenate([before, cur_ref[...], after], axis=0)
        y = _conv_act(window, w)
        sig = _sigmoid(y)
        act = y * sig
        dq_w = jnp.concatenate([jnp.zeros((8, QKV_WIDTH), F32), dq_ref[...],
                                jnp.where(last, 0.0, dq_after_ref[...])], axis=0)
        dact = []
        for hh in range(3 * GDN_HEADS):
            sl = slice(hh * HEAD_DIM, (hh + 1) * HEAD_DIM)
            blk, dblk = act[:, sl], dq_w[:, sl]
            if hh < 2 * GDN_HEADS:
                rn = lax.rsqrt(jnp.sum(blk * blk, axis=-1, keepdims=True) + EPS)
                unit = blk * rn
                if hh < GDN_HEADS:
                    dblk = dblk * (HEAD_DIM ** -0.5)
                dblk = rn * (dblk - unit * jnp.sum(dblk * unit, axis=-1, keepdims=True))
            dact.append(dblk)
        dy = jnp.concatenate(dact, axis=1) * (sig * (1.0 + y * (1.0 - sig)))
        dx = dy * w[CONV_K - 1:CONV_K, :]
        dws = [None] * CONV_K
        dws[CONV_K - 1] = jnp.sum(dy[8:8 + tm] * window[8:8 + tm], axis=0, keepdims=True)
        for j in range(CONV_K - 1):
            s = CONV_K - 1 - j
            dx += _shift_rows(dy, -s) * w[j:j + 1, :]
            dws[j] = jnp.sum(dy[8:8 + tm] * _shift_rows(window, s)[8:8 + tm], axis=0, keepdims=True)
        dw_ref[...] += jnp.concatenate(dws, axis=0)
        dproj_ref[:, :QKV_WIDTH] = dx[8:8 + tm].astype(dproj_ref.dtype)
        dproj_ref[:, COL_Z:COL_BA] = dzp_ref[...]

        ba = ba_ref[...]
        dbg_ = dbg_ref[...]
        lane = lax.broadcasted_iota(jnp.int32, ba.shape, 1)
        beta = _sigmoid(ba)
        pre = ba + dtb_ref[...]
        neg_a = -jnp.exp(al_ref[...])
        g = neg_a * _softplus(pre)
        is_g = (lane >= GDN_HEADS) & (lane < 2 * GDN_HEADS)
        da_raw = jnp.where(is_g, dbg_ * neg_a * _sigmoid(pre), 0.0)
        dba = jnp.where(lane < GDN_HEADS, dbg_ * beta * (1.0 - beta), da_raw)
        dproj_ref[:, COL_BA:] = dba.astype(dproj_ref.dtype)
        dal_ref[...] += jnp.sum(jnp.where(is_g, dbg_ * g, 0.0), axis=0, keepdims=True)
        ddtb_ref[...] += jnp.sum(da_raw, axis=0, keepdims=True)

    lane_vec = pl.BlockSpec((1, 128), lambda i: (0, 0))
    return pl.pallas_call(
        body, name=name, grid=(n_tiles,),
        in_specs=[pl.BlockSpec((tm, QKV_WIDTH), lambda i: (i, 0)),
                  pl.BlockSpec((8, QKV_WIDTH), lambda i: (jnp.maximum(i * hb - 1, 0), 0)),
                  pl.BlockSpec((8, QKV_WIDTH), lambda i: (jnp.minimum((i + 1) * hb, last_halo), 0)),
                  pl.BlockSpec((tm, 128), lambda i: (i, COL_BA // 128)),
                  pl.BlockSpec((CONV_K, QKV_WIDTH), lambda i: (0, 0)), lane_vec, lane_vec,
                  pl.BlockSpec((tm, QKV_WIDTH), lambda i: (i, 0)),
                  pl.BlockSpec((8, QKV_WIDTH), lambda i: (jnp.minimum((i + 1) * hb, last_halo), 0)),
                  pl.BlockSpec((tm, 128), lambda i: (i, 0)),
                  pl.BlockSpec((tm, GDN_WIDTH + POOL_WIDTH), lambda i: (i, 0))],
        out_specs=[pl.BlockSpec((tm, D_IN_PAD), lambda i: (i, 0)),
                   pl.BlockSpec((CONV_K, QKV_WIDTH), lambda i: (0, 0)), lane_vec, lane_vec],
        out_shape=[jax.ShapeDtypeStruct((T, D_IN_PAD), MXU_DTYPE),
                   jax.ShapeDtypeStruct((CONV_K, QKV_WIDTH), F32),
                   jax.ShapeDtypeStruct((1, 128), F32), jax.ShapeDtypeStruct((1, 128), F32)],
        compiler_params=_params("arbitrary"),
    )(proj, proj, proj, proj, conv_w, a_log_l, dt_bias_l, dqkv, dqkv, dbg, dzp)


def _mod_part(c_all, w_ada, b_part, name):
    def body(c_ref, w_ref, b_ref, out_ref):
        cc = c_ref[...]
        out_ref[...] = _mdot(cc * _sigmoid(cc), w_ref[...], NN) + b_ref[...]

    return pl.pallas_call(
        body, name=name, out_shape=jax.ShapeDtypeStruct((c_all.shape[0], w_ada.shape[1]), F32),
        compiler_params=_params(),
    )(c_all, w_ada, b_part)


def _w_ada_grad(c_all, dmod_part, name):
    def body(c_ref, d_ref, out_ref):
        cc = c_ref[...]
        out_ref[...] = _mdot(cc * _sigmoid(cc), d_ref[...], TN)

    return pl.pallas_call(
        body, name=name, out_shape=jax.ShapeDtypeStruct((c_all.shape[1], dmod_part.shape[1]), F32),
        compiler_params=_params(),
    )(c_all, dmod_part)


def _sum_parts(parts, name):
    _, R, C = parts.shape
    tr = max([t for t in range(16, min(R, 512) + 1, 16) if R % t == 0], default=R)

    def body(p_ref, out_ref):
        acc = p_ref[0].astype(F32)
        for s in range(1, N_DEV):
            acc += p_ref[s].astype(F32)
        out_ref[...] = acc

    return pl.pallas_call(
        body, name=name, grid=(R // tr,),
        in_specs=[pl.BlockSpec((N_DEV, tr, C), lambda i: (0, i, 0))],
        out_specs=pl.BlockSpec((tr, C), lambda i: (i, 0)),
        out_shape=jax.ShapeDtypeStruct((R, C), F32),
        compiler_params=_params("parallel"),
    )(parts)


def _adamw(w, g, m, v, name):
    R, C = w.shape
    tr = max([t for t in range(8, min(R, 512) + 1, 8) if R % t == 0], default=R)

    def body(w_ref, g_ref, m_ref, v_ref, d_ref, mo_ref, vo_ref):
        gg = g_ref[...]
        mm = ADAM_B1 * m_ref[...] + (1.0 - ADAM_B1) * gg
        vv = ADAM_B2 * v_ref[...] + (1.0 - ADAM_B2) * (gg * gg)
        m_hat = mm / (1.0 - ADAM_B1 ** ADAM_STEP)
        v_hat = vv / (1.0 - ADAM_B2 ** ADAM_STEP)
        d_ref[...] = -ADAM_LR * (m_hat / (jnp.sqrt(v_hat) + ADAM_EPS) + ADAM_WD * w_ref[...])
        mo_ref[...] = mm
        vo_ref[...] = vv

    spec = pl.BlockSpec((tr, C), lambda i: (i, 0))
    return pl.pallas_call(
        body, name=name, grid=(R // tr,),
        in_specs=[spec] * 4, out_specs=[spec] * 3,
        out_shape=[jax.ShapeDtypeStruct((R, C), F32)] * 3,
        compiler_params=_params("parallel"),
    )(w, g, m, v)


def _weight_grad(a, b, name, dep=None):
    return _matmul([(a, b)], TN, WIRE_DTYPE, name, tm=1408, tn=1024, tk=1024, dep=dep)


def _rows_of(flat, lanes=1024):
    flat = flat.reshape(-1)
    n = -(-flat.shape[0] // lanes) * lanes
    return jnp.pad(flat, (0, n - flat.shape[0])).reshape(n // lanes, lanes)


def _pad_rows(a, rows):
    return jnp.pad(a, ((0, rows - a.shape[0]), (0, 0)))


def kernel(x, c, w_ada, b_ada, norm_ffn1, ffn1_gate, ffn1_up, ffn1_down, norm_mix, w_in, conv_w, a_log, dt_bias, gdn_norm, pool_w, pool_scale, w_out, norm_ffn2, ffn2_gate, ffn2_up, ffn2_down, final_norm, loss_target, m_w_ada, m_b_ada, m_norm_ffn1, m_ffn1_gate, m_ffn1_up, m_ffn1_down, m_norm_mix, m_w_in, m_conv_w, m_a_log, m_dt_bias, m_gdn_norm, m_pool_w, m_pool_scale, m_w_out, m_norm_ffn2, m_ffn2_gate, m_ffn2_up, m_ffn2_down, m_final_norm, v_w_ada, v_b_ada, v_norm_ffn1, v_ffn1_gate, v_ffn1_up, v_ffn1_down, v_norm_mix, v_w_in, v_conv_w, v_a_log, v_dt_bias, v_gdn_norm, v_pool_w, v_pool_scale, v_w_out, v_norm_ffn2, v_ffn2_gate, v_ffn2_up, v_ffn2_down, v_final_norm):
    T, D = x.shape[1], x.shape[2]
    Fs = ffn1_gate.shape[2]
    Ws = w_in.shape[2]
    Ws_pad = -(-Ws // 16) * 16
    Os = w_out.shape[1]
    Ms = w_ada.shape[2]
    Cs = conv_w.shape[2]
    me = 4 * lax.axis_index("x") + 2 * lax.axis_index("y") + lax.axis_index("c")
    x0, target = x[0], loss_target[0]

    def wire(a):
        return a.astype(WIRE_DTYPE)

    def token(started):
        return started[4][:1, :1]

    def with_own(landed, own):
        return lax.dynamic_update_slice(landed, own[None], (me, 0, 0))

    def full(landed, off, size, keep=None):
        blk = landed[:, off:off + (size if keep is None else keep), :]
        return blk.reshape(-1, D).astype(MXU_DTYPE)

    no_dep = jnp.zeros((8, 128), F32)
    small = jnp.concatenate([_pad_rows(c, 8), _pad_rows(jnp.pad(conv_w[0], ((0, 0), (0, D - Cs))), 8)], axis=0)
    got = _all_gather(small, "gather_small")
    c_all = got[:, 0, :]
    conv_full = jnp.transpose(got[:, 8:8 + CONV_K, :Cs], (1, 0, 2)).reshape(CONV_K, QKV_WIDTH)
    b_part = lax.dynamic_slice(b_ada, (0, me * Ms), (1, Ms))
    mod_parts = _all_gather(_mod_part(c_all, w_ada[0], b_part, "mod_part"), "gather_mod")
    mod_all = jnp.transpose(mod_parts, (1, 0, 2)).reshape(N_DEV, N_MOD * D)
    mod = lax.dynamic_slice(mod_all, (me, 0), (1, N_MOD * D)).reshape(N_MOD, 1, D)
    sh1, sc1, gt1, sh2, sc2, gt2, sh3, sc3, gt3 = [mod[i] for i in range(N_MOD)]

    w1 = jnp.concatenate([wire(ffn1_gate[0].T), wire(ffn1_up[0].T), wire(ffn1_down[0])], axis=0)
    w23 = jnp.concatenate([wire(_pad_rows(w_in[0].T, Ws_pad)), wire(w_out[0]),
                           wire(ffn2_gate[0].T), wire(ffn2_up[0].T), wire(ffn2_down[0])], axis=0)
    off23 = [0, Ws_pad, Ws_pad + Os, Ws_pad + Os + Fs, Ws_pad + Os + 2 * Fs]
    w1_all = _all_gather(w1, "gather_w1", dep=mod_all)
    wg1_t, wu1_t, wd1 = full(w1_all, 0, Fs), full(w1_all, Fs, Fs), full(w1_all, 2 * Fs, Fs)
    w23_sent = _exchange_start(w23, True, w1_all, "w23_start")

    lane_pad = lambda a: jnp.pad(a, ((0, 0), (GDN_HEADS, 128 - 2 * GDN_HEADS)))
    a_log_l, dt_bias_l = lane_pad(a_log), lane_pad(dt_bias)
    pool_w_m = pool_w[0].astype(MXU_DTYPE)

    h1 = _norm_mod(x0, norm_ffn1, sc1 + token(w23_sent), sh1, "norm1")
    y1, g1, u1, a1 = _ffn_fwd(h1, wg1_t, wu1_t, wd1, "ffn1_fwd")
    x1, h2 = _resid_norm_mod(x0, y1, gt1, 0.5, norm_mix, sc2, sh2, "resid_norm2")
    w23_all = with_own(_exchange_wait(w23_sent, h2, True, "w23_wait"), w23)
    w_in_t = full(w23_all, off23[0], Ws_pad, Ws)
    wo = full(w23_all, off23[1], Os)
    wg2_t, wu2_t, wd2 = full(w23_all, off23[2], Fs), full(w23_all, off23[3], Fs), full(w23_all, off23[4], Fs)
    w_in_re = jnp.concatenate([w_in_t[:COL_Z + GDN_WIDTH], w_in_t[D_IN - POOL_WIDTH:],
                               w_in_t[4 * GDN_WIDTH:4 * GDN_WIDTH + 2 * GDN_HEADS],
                               jnp.zeros((128 - 2 * GDN_HEADS, D), MXU_DTYPE)], axis=0)
    proj = _matmul([(h2, w_in_re)], NT, F32, "proj_in", tm=512, tn=D_IN_PAD, tk=D)
    qkv, bg = _gdn_prep(proj, conv_full, a_log_l, dt_bias_l, "gdn_prep")
    tinv, u_c, w_c, qd_c, kd_c, p_c, cd_c = _gdn_chunk_fwd(qkv, bg, "gdn_chunk_fwd")
    o, s_all, vn_c = _gdn_scan_fwd(u_c, w_c, qd_c, kd_c, p_c, cd_c, "gdn_scan_fwd")
    mix_in = _mix_post(o, proj, gdn_norm, pool_w_m, pool_scale, "mix_post")
    mixed = _matmul([(mix_in, wo)], NN, F32, "mix_out", tm=512, tn=D, tk=GDN_WIDTH + POOL_WIDTH)
    x2, h3 = _resid_norm_mod(x1, mixed, gt2, 1.0, norm_ffn2, sc3, sh3, "resid_norm3")
    y3, g3, u3, a3 = _ffn_fwd(h3, wg2_t, wu2_t, wd2, "ffn2_fwd")
    loss_row, d3, d_final, dy3, dgt3 = _final_loss(x2, y3, gt3, final_norm.reshape(1, D), target, "final_loss")

    dg3, du3, dh3 = _ffn_bwd(dy3, wd2, g3, u3, wg2_t, wu2_t, "ffn2_bwd")
    d_wd2 = _weight_grad(a3, dy3, "ffn2_dwd")
    d_wg2 = _weight_grad(dg3, h3, "ffn2_dwg")
    d_wu2 = _weight_grad(du3, h3, "ffn2_dwu")
    d2, d_n3, dsc3, dsh3, dmixed, dgt2 = _norm_bwd(dh3, x2, norm_ffn2, sc3, d3, "norm3_bwd",
                                                   produced_by=(mixed, gt2, 1.0))
    dmix_in = _matmul([(dmixed, wo)], NT, F32, "mix_out_bwd", tm=512, tn=GDN_WIDTH + POOL_WIDTH, tk=D)
    d_wo = _matmul([(mix_in, dmixed)], TN, WIRE_DTYPE, "mix_dwo", tm=GDN_WIDTH + POOL_WIDTH, tn=D, tk=1024)
    do, dzp, d_gn, d_pw, d_ps = _mix_post_bwd(dmix_in, o, proj, gdn_norm, pool_w_m, pool_scale, "mix_post_bwd")
    dvn, dw_c, dqd, dkd, dp_c, dcd = _gdn_scan_bwd(do, w_c, qd_c, kd_c, p_c, cd_c, s_all, vn_c, "gdn_scan_bwd")
    dqkv, dbg = _gdn_chunk_bwd(qkv, bg, tinv, u_c, w_c, dvn, dw_c, dqd, dkd, dp_c, dcd, "gdn_chunk_bwd")
    dproj, d_conv, d_al, d_dtb = _gdn_prep_bwd(proj, conv_full, a_log_l, dt_bias_l, dqkv, dbg, dzp, "gdn_prep_bwd")
    dh2 = _matmul([(dproj, w_in_re)], NN, F32, "proj_in_bwd", tm=512, tn=D, tk=D_IN_PAD)
    d_win_re = _matmul([(dproj, h2)], TN, WIRE_DTYPE, "proj_in_dw", tm=D_IN_PAD, tn=D, tk=1024)
    d_win_t = jnp.concatenate([d_win_re[:COL_Z + GDN_WIDTH], d_win_re[COL_BA:COL_BA + 2 * GDN_HEADS],
                               d_win_re[COL_P:COL_P + POOL_WIDTH]], axis=0)
    d_win_blocks = jnp.pad(d_win_t.reshape(N_DEV, Ws, D), ((0, 0), (0, Ws_pad - Ws), (0, 0)))
    parts23 = jnp.concatenate(
        [wire(d_win_blocks), wire(d_wo.reshape(N_DEV, Os, D)), wire(d_wg2.reshape(N_DEV, Fs, D)),
         wire(d_wu2.reshape(N_DEV, Fs, D)), wire(d_wd2.reshape(N_DEV, Fs, D))], axis=1)
    own23 = lax.dynamic_index_in_dim(parts23, me, 0, keepdims=False)
    g23_sent = _exchange_start(parts23, False, no_dep, "g23_start")
    d1, d_n2, dsc2, dsh2, dy1, dgt1 = _norm_bwd(dh2, x1, norm_mix, sc2 + token(g23_sent), d2, "norm2_bwd",
                                                produced_by=(y1, gt1, 0.5))
    dg1, du1, dh1 = _ffn_bwd(dy1, wd1, g1, u1, wg1_t, wu1_t, "ffn1_bwd")
    grad_x, d_n1, dsc1, dsh1 = _norm_bwd(dh1, x0, norm_ffn1, sc1, d1, "norm1_bwd")

    dmod = jnp.concatenate([dsh1, dsc1, dgt1, dsh2, dsc2, dgt2, dsh3, dsc3, dgt3], axis=0)
    small_rows = [dmod.reshape(-1), d_n1[0], d_n2[0], d_n3[0], d_final[0], d_gn[0], d_ps[0],
                  d_al[0, GDN_HEADS:2 * GDN_HEADS], d_dtb[0, GDN_HEADS:2 * GDN_HEADS], loss_row[0, :1],
                  d_conv.reshape(-1), d_pw.reshape(-1)]
    lanes = 1024
    small_rows = [_rows_of(r, lanes) for r in small_rows]
    n_rows = [r.shape[0] for r in small_rows]
    row_off = [sum(n_rows[:i]) for i in range(len(n_rows))]
    total = -(-sum(n_rows) // 8) * 8
    slab = _pad_rows(jnp.concatenate(small_rows, axis=0), total)
    slab_all = _all_gather(slab, "gather_small_grads")
    summed = _sum_parts(slab_all, "sum_small_grads")

    def piece(idx, n):
        return summed[row_off[idx]:row_off[idx] + n_rows[idx]].reshape(-1)[:n]

    g_b_ada = piece(0, N_MOD * D).reshape(1, N_MOD * D)
    g_n1, g_n2, g_n3 = piece(1, D).reshape(1, D), piece(2, D).reshape(1, D), piece(3, D).reshape(1, D)
    g_final = piece(4, D)
    g_gn = piece(5, HEAD_DIM).reshape(1, HEAD_DIM)
    g_ps = piece(6, POOL_WIDTH).reshape(1, POOL_WIDTH)
    g_al = piece(7, GDN_HEADS).reshape(1, GDN_HEADS)
    g_dtb = piece(8, GDN_HEADS).reshape(1, GDN_HEADS)
    loss = piece(9, 1)[0]
    g_conv = lax.dynamic_slice(piece(10, CONV_K * QKV_WIDTH).reshape(1, CONV_K, QKV_WIDTH), (0, 0, me * Cs),
                               (1, CONV_K, Cs))
    g_pw = piece(11, POOL_GROUPS * 128 * 128).reshape(1, POOL_GROUPS, 128, 128)

    dmod_all = slab_all[:, row_off[0]:row_off[0] + n_rows[0], :].reshape(N_DEV, -1)[:, :N_MOD * D]
    g_w_ada = _w_ada_grad(c_all, lax.dynamic_slice(dmod_all, (0, me * Ms), (N_DEV, Ms)), "w_ada_grad")[None]

    def send_ffn1(a, b, which, dep):
        parts = _weight_grad(a, b, f"ffn1_{which}", dep=dep).reshape(N_DEV, Fs, D)
        own = lax.dynamic_index_in_dim(parts, me, 0, keepdims=False)
        return _exchange_start(parts, False, no_dep, f"g1_{which}_start"), own

    g1_wg, own_wg = send_ffn1(dg1, h1, "dwg", summed)
    g1_wu, own_wu = send_ffn1(du1, h1, "dwu", g1_wg[4])
    g1_wd, own_wd = send_ffn1(a1, dy1, "dwd", g1_wu[4])

    big23 = _sum_parts(with_own(_exchange_wait(g23_sent, g1_wd[4], False, "g23_wait"), own23), "sum_grads23")
    g_w_in, g_w_out = big23[:Ws].T[None], big23[off23[1]:off23[1] + Os][None]
    g_ffn2_gate, g_ffn2_up = big23[off23[2]:off23[2] + Fs].T[None], big23[off23[3]:off23[3] + Fs].T[None]
    g_ffn2_down = big23[off23[4]:off23[4] + Fs][None]

    names = ["w_ada", "b_ada", "norm_ffn1", "ffn1_gate", "ffn1_up", "ffn1_down", "norm_mix", "w_in", "conv_w",
             "a_log", "dt_bias", "gdn_norm", "pool_w", "pool_scale", "w_out", "norm_ffn2", "ffn2_gate", "ffn2_up",
             "ffn2_down", "final_norm"]
    weights = dict(zip(names, [w_ada, b_ada, norm_ffn1, ffn1_gate, ffn1_up, ffn1_down, norm_mix, w_in, conv_w,
                               a_log, dt_bias, gdn_norm, pool_w, pool_scale, w_out, norm_ffn2, ffn2_gate, ffn2_up,
                               ffn2_down, final_norm]))
    ms = dict(zip(names, [m_w_ada, m_b_ada, m_norm_ffn1, m_ffn1_gate, m_ffn1_up, m_ffn1_down, m_norm_mix, m_w_in,
                          m_conv_w, m_a_log, m_dt_bias, m_gdn_norm, m_pool_w, m_pool_scale, m_w_out, m_norm_ffn2,
                          m_ffn2_gate, m_ffn2_up, m_ffn2_down, m_final_norm]))
    vs = dict(zip(names, [v_w_ada, v_b_ada, v_norm_ffn1, v_ffn1_gate, v_ffn1_up, v_ffn1_down, v_norm_mix, v_w_in,
                          v_conv_w, v_a_log, v_dt_bias, v_gdn_norm, v_pool_w, v_pool_scale, v_w_out, v_norm_ffn2,
                          v_ffn2_gate, v_ffn2_up, v_ffn2_down, v_final_norm]))
    grads = dict(w_ada=g_w_ada, b_ada=g_b_ada, norm_ffn1=g_n1, norm_mix=g_n2, w_in=g_w_in, conv_w=g_conv,
                 a_log=g_al, dt_bias=g_dtb, gdn_norm=g_gn, pool_w=g_pw, pool_scale=g_ps, w_out=g_w_out,
                 norm_ffn2=g_n3, ffn2_gate=g_ffn2_gate, ffn2_up=g_ffn2_up, ffn2_down=g_ffn2_down, final_norm=g_final)
    delta, new_m, new_v = {}, {}, {}

    def adamw_big(n):
        shp = weights[n].shape
        two_d = lambda a: a.reshape(shp[-2], shp[-1])
        d_, m_, v_ = _adamw(two_d(weights[n]), two_d(grads[n]), two_d(ms[n]), two_d(vs[n]), f"adamw_{n}")
        delta[n], new_m[n], new_v[n] = d_.reshape(shp), m_.reshape(shp), v_.reshape(shp)

    early = ["w_ada", "w_in", "w_out", "ffn2_gate", "ffn2_up", "ffn2_down"]
    late = ["ffn1_gate", "ffn1_up", "ffn1_down"]
    for n in early:
        adamw_big(n)
    done = sum(delta[n].reshape(-1)[:1] for n in early).reshape(1, 1)

    def arrived(started, own, which):
        return _sum_parts(with_own(_exchange_wait(started, done, False, f"g1_{which}_wait"), own), f"sum_{which}")

    grads["ffn1_gate"] = arrived(g1_wg, own_wg, "dwg").T[None]
    grads["ffn1_up"] = arrived(g1_wu, own_wu, "dwu").T[None]
    grads["ffn1_down"] = arrived(g1_wd, own_wd, "dwd")[None]
    for n in late:
        adamw_big(n)
    small_names = [n for n in names if n not in early + late]
    pack = lambda src: jnp.concatenate([_rows_of(src[n]) for n in small_names], axis=0)
    p_rows = [_rows_of(weights[n]).shape[0] for n in small_names]
    p_total = -(-sum(p_rows) // 8) * 8
    packed = [_pad_rows(pack(src), p_total) for src in (weights, grads, ms, vs)]
    d_s, m_s, v_s = _adamw(*packed, "adamw_small")
    off = 0
    for n, r in zip(small_names, p_rows):
        shp = weights[n].shape
        size = weights[n].size
        for dst, src in ((delta, d_s), (new_m, m_s), (new_v, v_s)):
            dst[n] = src[off:off + r].reshape(-1)[:size].reshape(shp)
        off += r

    return (loss, grad_x[None], *[grads[n] for n in names], *[delta[n] for n in names],
            *[new_m[n] for n in names], *[new_v[n] for n in names])
```

```python
import functools

import jax
import jax.numpy as jnp
from jax import lax
from jax.experimental import pallas as pl
from jax.experimental.pallas import tpu as pltpu

F32 = jnp.float32
MXU_DTYPE = jnp.bfloat16
WIRE_DTYPE = jnp.bfloat16
EPS = 1e-6
N_DEV = 8
GDN_HEADS = 4
HEAD_DIM = 128
GDN_WIDTH = GDN_HEADS * HEAD_DIM
POOL_WINDOWS = (2, 4, 8, 16)
POOL_GROUPS = len(POOL_WINDOWS)
POOL_WIDTH = 512
CONV_K = 4
CHUNK = 64
QKV_WIDTH = 3 * GDN_WIDTH
D_IN = 4 * GDN_WIDTH + 2 * GDN_HEADS + POOL_WIDTH
D_IN_PAD = 4 * GDN_WIDTH + POOL_WIDTH + 128
COL_Z = QKV_WIDTH
COL_P = 4 * GDN_WIDTH
COL_BA = 4 * GDN_WIDTH + POOL_WIDTH
N_MOD = 9
HALO = 16
VMEM_LIMIT = 56 * 1024 * 1024
ADAM_LR, ADAM_B1, ADAM_B2, ADAM_EPS, ADAM_WD, ADAM_STEP = 0.001, 0.9, 0.999, 1e-08, 0.01, 10
FFN_TOKEN_TILE = 512
FFN_HIDDEN_TILE = 1408
CHUNKS_PER_STEP = 4
SCAN_CHUNKS_PER_STEP = 4

NT = (((1,), (1,)), ((), ()))
NN = (((1,), (0,)), ((), ()))
TN = (((0,), (0,)), ((), ()))


def _params(*sem):
    return pltpu.CompilerParams(dimension_semantics=tuple(sem), vmem_limit_bytes=VMEM_LIMIT)


def _dot(a, b, dims):
    return lax.dot_general(a, b, dims, preferred_element_type=F32)


def _mdot(a, b, dims):
    return _dot(a.astype(MXU_DTYPE), b.astype(MXU_DTYPE), dims)


def _split(a):
    hi = a.astype(jnp.bfloat16)
    return hi, (a - hi.astype(F32)).astype(jnp.bfloat16)


def _dot3(a, b, dims):
    (ah, al), (bh, bl) = a, b
    return (_dot(al, bh, dims) + _dot(ah, bl, dims)) + _dot(ah, bh, dims)


def _sigmoid(v):
    return 1.0 / (1.0 + jnp.exp(-v))


def _softplus(v):
    return jnp.maximum(v, 0.0) + jnp.log(1.0 + jnp.exp(-jnp.abs(v)))


def _shift_rows(v, s):
    n = v.shape[0]
    s = s % n
    return v if s == 0 else pltpu.roll(v, s, 0)


def _tile(n, want):
    t = min(n, want)
    while n % t:
        t //= 2
    return t


def _all_gather(block, name, dep=None):
    shape, dtype = block.shape, block.dtype

    def body(x_ref, *refs):
        out_ref, send_sems, recv_sems, local_sem = refs[-4:]
        x, y, c = lax.axis_index("x"), lax.axis_index("y"), lax.axis_index("c")
        me, sibling = (x, y, c), (x, y, 1 - c)
        chips = [(1 - x, y), (x, 1 - y), (1 - x, 1 - y)]

        def rows(px, py, pc):
            return out_ref.at[4 * px + 2 * py + pc]

        def copy(k, blk, to, src=None):
            return pltpu.make_async_remote_copy(
                src_ref=rows(*blk) if src is None else src, dst_ref=rows(*blk),
                send_sem=send_sems.at[k], recv_sem=recv_sems.at[k],
                device_id=to, device_id_type=pl.DeviceIdType.MESH)

        mine = pltpu.make_async_copy(x_ref, rows(*me), local_sem)
        mine.start()
        first = [copy(0, me, sibling, src=x_ref)]
        first += [copy(1 + j, me, (*chip, c), src=x_ref) for j, chip in enumerate(chips)]
        for cp in first:
            cp.start()
        passed = [copy(4 + j, (*chip, c), sibling) for j, chip in enumerate(chips)]
        for j, chip in enumerate(chips):
            copy(1 + j, (*chip, c), me).wait_recv()
            passed[j].start()
        copy(0, sibling, me).wait_recv()
        for j, chip in enumerate(chips):
            copy(4 + j, (*chip, 1 - c), me).wait_recv()
        for cp in first + passed:
            cp.wait_send()
        mine.wait()

    return pl.pallas_call(
        body, name=name,
        out_shape=jax.ShapeDtypeStruct((N_DEV,) + shape, dtype),
        in_specs=[pl.BlockSpec(memory_space=pltpu.HBM)] + [pl.BlockSpec(memory_space=pl.ANY)] * (dep is not None),
        out_specs=pl.BlockSpec(memory_space=pltpu.HBM),
        scratch_shapes=[pltpu.SemaphoreType.DMA((7,)), pltpu.SemaphoreType.DMA((7,)),
                        pltpu.SemaphoreType.DMA(())],
    )(*([block] if dep is None else [block, dep]))


_HBM = pl.BlockSpec(memory_space=pltpu.HBM)
_SEM = pl.BlockSpec(memory_space=pltpu.SEMAPHORE)
_ANY = pl.BlockSpec(memory_space=pl.ANY)
_EFFECT = pltpu.SideEffectType.DATAFLOW_SIDE_EFFECTING
_FLIPS = [(0, 0, 1), (0, 1, 0), (0, 1, 1), (1, 0, 0), (1, 0, 1), (1, 1, 0), (1, 1, 1)]


def _peers():
    x, y, c = lax.axis_index("x"), lax.axis_index("y"), lax.axis_index("c")
    return 4 * x + 2 * y + c, [(1 - x if fx else x, 1 - y if fy else y, 1 - c if fc else c)
                               for fx, fy, fc in _FLIPS]


def _exchange_start(src, gather, dep, name):
    block = src.shape if gather else src.shape[1:]
    land = (N_DEV,) + tuple(block)

    def body(src_ref, land_ref, dep_ref, send_sems, recv_sems, land_thru, token):
        me, peers = _peers()
        for k, (px, py, pc) in enumerate(peers):
            pltpu.make_async_remote_copy(
                src_ref=src_ref if gather else src_ref.at[4 * px + 2 * py + pc], dst_ref=land_ref.at[me],
                send_sem=send_sems.at[k], recv_sem=recv_sems.at[k],
                device_id=(px, py, pc), device_id_type=pl.DeviceIdType.MESH).start()
        token[...] = jnp.zeros_like(token)

    src = pltpu.with_memory_space_constraint(src, pltpu.HBM)
    send_sems, recv_sems, land_thru, token = pl.pallas_call(
        body, name=name,
        out_shape=(pltpu.SemaphoreType.DMA((7,)), pltpu.SemaphoreType.DMA((7,)),
                   pltpu.HBM(land, src.dtype), jax.ShapeDtypeStruct((8, 128), F32)),
        in_specs=(_HBM, _HBM, _ANY),
        out_specs=(_SEM, _SEM, _HBM, pl.BlockSpec(memory_space=pltpu.VMEM)),
        input_output_aliases={1: 2},
        compiler_params=pltpu.CompilerParams(has_side_effects=_EFFECT),
    )(src, pltpu.with_memory_space_constraint(lax.empty(land, src.dtype), pltpu.HBM), dep)
    return send_sems, recv_sems, src, land_thru, token


def _exchange_wait(started, after, gather, name):
    send_sems, recv_sems, src, land_thru, _ = started

    def body(src_ref, land_ref, send_sems, recv_sems, after_ref, got_ref):
        _, peers = _peers()
        for k, peer in enumerate(peers):
            copy = pltpu.make_async_remote_copy(
                src_ref=src_ref if gather else src_ref.at[0], dst_ref=land_ref.at[0],
                send_sem=send_sems.at[k], recv_sem=recv_sems.at[k],
                device_id=peer, device_id_type=pl.DeviceIdType.MESH)
            copy.wait_send()
            copy.wait_recv()

    return pl.pallas_call(
        body, name=name,
        out_shape=pltpu.HBM(land_thru.shape, land_thru.dtype),
        in_specs=(_HBM, _HBM, _SEM, _SEM, _ANY), out_specs=_HBM,
        input_output_aliases={1: 0},
        compiler_params=pltpu.CompilerParams(has_side_effects=_EFFECT),
    )(src, land_thru, send_sems, recv_sems, after)


def _matmul(pairs, dims, out_dtype, name, tm=512, tn=512, tk=512, dep=None):
    a0, b0 = pairs[0]
    if dims == TN:
        K, M = a0.shape
    else:
        M, K = a0.shape
    N = b0.shape[0] if dims == NT else b0.shape[1]
    tm, tn, tk = _tile(M, tm), _tile(N, tn), _tile(K, tk)
    nk = K // tk
    n_pairs = len(pairs)
    n_in = 2 * n_pairs + (dep is not None)

    def body(*refs):
        out_ref = refs[n_in]

        def product():
            total = _dot(refs[0][...], refs[1][...], dims)
            for p in range(1, n_pairs):
                total += _dot(refs[2 * p][...], refs[2 * p + 1][...], dims)
            return total

        if nk == 1:
            out_ref[...] = product().astype(out_ref.dtype)
            return
        acc_ref = refs[n_in + 1]
        k = pl.program_id(2)

        @pl.when(k == 0)
        def _():
            acc_ref[...] = product()

        @pl.when((k > 0) & (k < nk - 1))
        def _():
            acc_ref[...] += product()

        @pl.when(k == nk - 1)
        def _():
            out_ref[...] = (acc_ref[...] + product()).astype(out_ref.dtype)

    if dims == TN:
        a_spec = pl.BlockSpec((tk, tm), lambda i, j, k: (k, i))
    else:
        a_spec = pl.BlockSpec((tm, tk), lambda i, j, k: (i, k))
    if dims == NT:
        b_spec = pl.BlockSpec((tn, tk), lambda i, j, k: (j, k))
    else:
        b_spec = pl.BlockSpec((tk, tn), lambda i, j, k: (k, j))
    args, specs = [], []
    for a, b in pairs:
        args += [a, b]
        specs += [a_spec, b_spec]
    if dep is not None:
        args.append(dep)
        specs.append(_ANY)
    return pl.pallas_call(
        body, name=name, grid=(M // tm, N // tn, nk),
        in_specs=specs, out_specs=pl.BlockSpec((tm, tn), lambda i, j, k: (i, j)),
        out_shape=jax.ShapeDtypeStruct((M, N), out_dtype),
        scratch_shapes=[pltpu.VMEM((tm, tn), F32)] * (nk > 1),
        compiler_params=_params("parallel", "parallel", "arbitrary"),
    )(*args)


def _vec_spec(d):
    return pl.BlockSpec((1, d), lambda i: (0, 0))


def _norm_mod(x, nw, scale, shift, name):
    T, D = x.shape
    tm = _tile(T, 512)

    def body(x_ref, nw_ref, sc_ref, sh_ref, h_ref):
        xf = x_ref[...]
        r = lax.rsqrt(jnp.mean(xf * xf, axis=-1, keepdims=True) + EPS)
        h_ref[...] = ((xf * r) * nw_ref[...] * (1.0 + sc_ref[...]) + sh_ref[...]).astype(h_ref.dtype)

    row = pl.BlockSpec((tm, D), lambda i: (i, 0))
    return pl.pallas_call(
        body, name=name, grid=(T // tm,),
        in_specs=[row, _vec_spec(D), _vec_spec(D), _vec_spec(D)], out_specs=row,
        out_shape=jax.ShapeDtypeStruct((T, D), MXU_DTYPE),
        compiler_params=_params("parallel"),
    )(x, nw, scale, shift)


def _resid_norm_mod(x, y, gate, coef, nw, scale, shift, name):
    T, D = x.shape
    tm = _tile(T, 512)

    def body(x_ref, y_ref, g_ref, nw_ref, sc_ref, sh_ref, xo_ref, h_ref):
        xf = x_ref[...] + (coef * g_ref[...]) * y_ref[...]
        xo_ref[...] = xf
        r = lax.rsqrt(jnp.mean(xf * xf, axis=-1, keepdims=True) + EPS)
        h_ref[...] = ((xf * r) * nw_ref[...] * (1.0 + sc_ref[...]) + sh_ref[...]).astype(h_ref.dtype)

    row = pl.BlockSpec((tm, D), lambda i: (i, 0))
    return pl.pallas_call(
        body, name=name, grid=(T // tm,),
        in_specs=[row, row, _vec_spec(D), _vec_spec(D), _vec_spec(D), _vec_spec(D)],
        out_specs=[row, row],
        out_shape=[jax.ShapeDtypeStruct((T, D), F32), jax.ShapeDtypeStruct((T, D), MXU_DTYPE)],
        compiler_params=_params("parallel"),
    )(x, y, gate, nw, scale, shift)


def _matmul_resid_norm_mod(a, b, x, gate, coef, nw, scale, shift, name):
    T, K = a.shape
    D = b.shape[1]
    tm = _tile(T, 512)

    def body(a_ref, b_ref, x_ref, g_ref, nw_ref, sc_ref, sh_ref, y_ref, xo_ref, h_ref):
        y = _dot(a_ref[...], b_ref[...], NN)
        y_ref[...] = y
        xf = x_ref[...] + (coef * g_ref[...]) * y
        xo_ref[...] = xf
        r = lax.rsqrt(jnp.mean(xf * xf, axis=-1, keepdims=True) + EPS)
        h_ref[...] = ((xf * r) * nw_ref[...] * (1.0 + sc_ref[...]) + sh_ref[...]).astype(h_ref.dtype)

    row = pl.BlockSpec((tm, D), lambda i: (i, 0))
    vec = _vec_spec(D)
    return pl.pallas_call(
        body, name=name, grid=(T // tm,),
        in_specs=[pl.BlockSpec((tm, K), lambda i: (i, 0)), _resident((K, D)), row, vec, vec, vec, vec],
        out_specs=[row, row, row],
        out_shape=[jax.ShapeDtypeStruct((T, D), F32), jax.ShapeDtypeStruct((T, D), F32),
                   jax.ShapeDtypeStruct((T, D), MXU_DTYPE)],
        compiler_params=_params("parallel"),
    )(a, b, x, gate, nw, scale, shift)


def _norm_bwd(dh, x, nw, scale, dres, name, produced_by=None):
    T, D = x.shape
    tm = _tile(T, 512)

    def body(*refs):
        if isinstance(dh, tuple):
            dh_value = _dot(refs[0][...], refs[1][...], NN)
            refs = refs[1:]
        else:
            dh_value = refs[0][...]
        _, x_ref, nw_ref, sc_ref, dr_ref = refs[:5]
        n_in = 5 if produced_by is None else 7
        dx_ref, dnw_ref, dsc_ref, dsh_ref = refs[n_in:n_in + 4]

        @pl.when(pl.program_id(0) == 0)
        def _():
            dnw_ref[...] = jnp.zeros_like(dnw_ref)
            dsc_ref[...] = jnp.zeros_like(dsc_ref)
            dsh_ref[...] = jnp.zeros_like(dsh_ref)
            if produced_by is not None:
                refs[n_in + 5][...] = jnp.zeros_like(refs[n_in + 5])

        xf, dh_ = x_ref[...], dh_value
        r = lax.rsqrt(jnp.mean(xf * xf, axis=-1, keepdims=True) + EPS)
        xn = xf * r
        one_sc = 1.0 + sc_ref[...]
        dsh_ref[...] += jnp.sum(dh_, axis=0, keepdims=True)
        t = dh_ * xn
        dsc_ref[...] += jnp.sum(t, axis=0, keepdims=True) * nw_ref[...]
        dnw_ref[...] += jnp.sum(t, axis=0, keepdims=True) * one_sc
        dxn = dh_ * (nw_ref[...] * one_sc)
        dx = dr_ref[...] + r * (dxn - xn * jnp.mean(dxn * xn, axis=-1, keepdims=True))
        dx_ref[...] = dx
        if produced_by is not None:
            y_ref, g_ref, dy_ref, dg_ref = refs[5], refs[6], refs[n_in + 4], refs[n_in + 5]
            dy_ref[...] = ((produced_by[2] * g_ref[...]) * dx).astype(dy_ref.dtype)
            dg_ref[...] += produced_by[2] * jnp.sum(dx * y_ref[...], axis=0, keepdims=True)

    row = pl.BlockSpec((tm, D), lambda i: (i, 0))
    vec = _vec_spec(D)
    vec_out = jax.ShapeDtypeStruct((1, D), F32)
    if isinstance(dh, tuple):
        k_dim = dh[0].shape[1]
        args, in_specs = [dh[0], dh[1]], [pl.BlockSpec((tm, k_dim), lambda i: (i, 0)), _resident((k_dim, D))]
    else:
        args, in_specs = [dh], [row]
    args, in_specs = args + [x, nw, scale, dres], in_specs + [row, vec, vec, row]
    out_specs, out_shape = [row, vec, vec, vec], [jax.ShapeDtypeStruct((T, D), F32), vec_out, vec_out, vec_out]
    if produced_by is not None:
        args += [produced_by[0], produced_by[1]]
        in_specs += [row, vec]
        out_specs += [row, vec]
        out_shape += [jax.ShapeDtypeStruct((T, D), MXU_DTYPE), vec_out]
    return pl.pallas_call(
        body, name=name, grid=(T // tm,),
        in_specs=in_specs, out_specs=out_specs, out_shape=out_shape,
        compiler_params=_params("arbitrary"),
    )(*args)


def _final_loss(x, y, gate, fw, target, name):
    T, D = x.shape
    tm = _tile(T, 512)

    def body(x_ref, y_ref, g_ref, fw_ref, t_ref, loss_ref, dx_ref, dfw_ref, dy_ref, dg_ref):
        @pl.when(pl.program_id(0) == 0)
        def _():
            loss_ref[...] = jnp.zeros_like(loss_ref)
            dfw_ref[...] = jnp.zeros_like(dfw_ref)
            dg_ref[...] = jnp.zeros_like(dg_ref)

        yy = y_ref[...]
        xf = x_ref[...] + (0.5 * g_ref[...]) * yy
        r = lax.rsqrt(jnp.mean(xf * xf, axis=-1, keepdims=True) + EPS)
        xn = xf * r
        err = xn * fw_ref[...] - t_ref[...]
        per_tok = jnp.mean(err * err, axis=-1, keepdims=True)
        loss_ref[...] += 0.5 * jnp.sum(per_tok, axis=0, keepdims=True)
        dy = err * (1.0 / D)
        dfw_ref[...] += jnp.sum(dy * xn, axis=0, keepdims=True)
        dxn = dy * fw_ref[...]
        dx = r * (dxn - xn * jnp.mean(dxn * xn, axis=-1, keepdims=True))
        dx_ref[...] = dx
        dy_ref[...] = ((0.5 * g_ref[...]) * dx).astype(dy_ref.dtype)
        dg_ref[...] += 0.5 * jnp.sum(dx * yy, axis=0, keepdims=True)

    row = pl.BlockSpec((tm, D), lambda i: (i, 0))
    vec = _vec_spec(D)
    return pl.pallas_call(
        body, name=name, grid=(T // tm,),
        in_specs=[row, row, vec, vec, row],
        out_specs=[pl.BlockSpec((1, 128), lambda i: (0, 0)), row, vec, row, vec],
        out_shape=[jax.ShapeDtypeStruct((1, 128), F32), jax.ShapeDtypeStruct((T, D), F32),
                   jax.ShapeDtypeStruct((1, D), F32), jax.ShapeDtypeStruct((T, D), MXU_DTYPE),
                   jax.ShapeDtypeStruct((1, D), F32)],
        compiler_params=_params("arbitrary"),
    )(x, y, gate, fw, target)


def _resident(shape):
    return pl.BlockSpec(shape, lambda i: (0,) * len(shape), pipeline_mode=pl.Buffered(1))


def _ffn_fwd(h, wg_t, wu_t, wd, name):
    T, D = h.shape
    Fdim = wd.shape[0]
    tm, tf = _tile(T, FFN_TOKEN_TILE), _tile(Fdim, FFN_HIDDEN_TILE)

    def body(h_ref, wg_ref, wu_ref, wd_ref, y_ref, g_ref, u_ref, a_ref):
        hh = h_ref[...]
        y = None
        for k in range(Fdim // tf):
            ks = slice(k * tf, (k + 1) * tf)
            g = _dot(hh, wg_ref[ks, :], NT)
            u = _dot(hh, wu_ref[ks, :], NT)
            a = ((g * _sigmoid(g)) * u).astype(a_ref.dtype)
            g_ref[:, ks] = g.astype(g_ref.dtype)
            u_ref[:, ks] = u.astype(u_ref.dtype)
            a_ref[:, ks] = a
            part = _dot(a, wd_ref[ks, :], NN)
            y = part if y is None else y + part
        y_ref[...] = y

    hrow = pl.BlockSpec((tm, D), lambda i: (i, 0))
    frow = pl.BlockSpec((tm, Fdim), lambda i: (i, 0))
    return pl.pallas_call(
        body, name=name, grid=(T // tm,),
        in_specs=[hrow, _resident((Fdim, D)), _resident((Fdim, D)), _resident((Fdim, D))],
        out_specs=[hrow, frow, frow, frow],
        out_shape=[jax.ShapeDtypeStruct((T, D), F32)] + [jax.ShapeDtypeStruct((T, Fdim), MXU_DTYPE)] * 3,
        compiler_params=_params("parallel"),
    )(h, wg_t, wu_t, wd)


def _ffn_bwd(dy, wd, g, u, wg_t, wu_t, name):
    T, D = dy.shape
    Fdim = wd.shape[0]
    tm, tf = _tile(T, FFN_TOKEN_TILE), _tile(Fdim, FFN_HIDDEN_TILE)

    def body(dy_ref, wd_ref, g_ref, u_ref, wg_ref, wu_ref, dg_ref, du_ref, dh_ref):
        dyy = dy_ref[...]
        dh = None
        for k in range(Fdim // tf):
            ks = slice(k * tf, (k + 1) * tf)
            da = _dot(dyy, wd_ref[ks, :], NT)
            gg = g_ref[:, ks].astype(F32)
            sig = _sigmoid(gg)
            dg = (da * u_ref[:, ks].astype(F32) * (sig * (1.0 + gg * (1.0 - sig)))).astype(dg_ref.dtype)
            du = (da * (gg * sig)).astype(du_ref.dtype)
            dg_ref[:, ks] = dg
            du_ref[:, ks] = du
            part = _dot(dg, wg_ref[ks, :], NN) + _dot(du, wu_ref[ks, :], NN)
            dh = part if dh is None else dh + part
        dh_ref[...] = dh

    row = pl.BlockSpec((tm, D), lambda i: (i, 0))
    frow = pl.BlockSpec((tm, Fdim), lambda i: (i, 0))
    wres = _resident((Fdim, D))
    return pl.pallas_call(
        body, name=name, grid=(T // tm,),
        in_specs=[row, wres, frow, frow, wres, wres],
        out_specs=[frow, frow, row],
        out_shape=[jax.ShapeDtypeStruct((T, Fdim), MXU_DTYPE)] * 2 + [jax.ShapeDtypeStruct((T, D), F32)],
        compiler_params=_params("parallel"),
    )(dy, wd, g, u, wg_t, wu_t)


def _conv_act(window, w):
    y = window * w[CONV_K - 1:CONV_K, :]
    for j in range(CONV_K - 1):
        y += _shift_rows(window, CONV_K - 1 - j) * w[j:j + 1, :]
    return y


def _gdn_prep(proj, conv_w, a_log_l, dt_bias_l, name):
    T = proj.shape[0]
    tm = _tile(T, 256)
    hb = tm // 8

    def body(cur_ref, halo_ref, ba_ref, w_ref, al_ref, dtb_ref, qkv_ref, bg_ref):
        i = pl.program_id(0)
        halo = jnp.where(i == 0, 0.0, halo_ref[...])
        window = jnp.concatenate([halo, cur_ref[...]], axis=0)
        y = _conv_act(window, w_ref[...])[8:, :]
        act = y * _sigmoid(y)
        for hh in range(3 * GDN_HEADS):
            blk = act[:, hh * HEAD_DIM:(hh + 1) * HEAD_DIM]
            if hh < 2 * GDN_HEADS:
                rn = lax.rsqrt(jnp.sum(blk * blk, axis=-1, keepdims=True) + EPS)
                blk = blk * rn
                if hh < GDN_HEADS:
                    blk = blk * (HEAD_DIM ** -0.5)
            qkv_ref[:, hh * HEAD_DIM:(hh + 1) * HEAD_DIM] = blk
        ba = ba_ref[...]
        lane = lax.broadcasted_iota(jnp.int32, ba.shape, 1)
        beta = _sigmoid(ba)
        g = -jnp.exp(al_ref[...]) * _softplus(ba + dtb_ref[...])
        bg_ref[...] = jnp.where(lane < GDN_HEADS, beta, jnp.where(lane < 2 * GDN_HEADS, g, 0.0))

    return pl.pallas_call(
        body, name=name, grid=(T // tm,),
        in_specs=[pl.BlockSpec((tm, QKV_WIDTH), lambda i: (i, 0)),
                  pl.BlockSpec((8, QKV_WIDTH), lambda i: (jnp.maximum(i * hb - 1, 0), 0)),
                  pl.BlockSpec((tm, 128), lambda i: (i, COL_BA // 128)),
                  pl.BlockSpec((CONV_K, QKV_WIDTH), lambda i: (0, 0)),
                  pl.BlockSpec((1, 128), lambda i: (0, 0)), pl.BlockSpec((1, 128), lambda i: (0, 0))],
        out_specs=[pl.BlockSpec((tm, QKV_WIDTH), lambda i: (i, 0)), pl.BlockSpec((tm, 128), lambda i: (i, 0))],
        out_shape=[jax.ShapeDtypeStruct((T, QKV_WIDTH), F32), jax.ShapeDtypeStruct((T, 128), F32)],
        compiler_params=_params("parallel"),
    )(proj, proj, proj, conv_w, a_log_l, dt_bias_l)


def _chunk_cumsum(v, reverse=False):
    row = lax.broadcasted_iota(jnp.int32, v.shape, 0)
    s = 1
    while s < CHUNK:
        if reverse:
            v = v + jnp.where(row < CHUNK - s, _shift_rows(v, -s), 0.0)
        else:
            v = v + jnp.where(row >= s, _shift_rows(v, s), 0.0)
        s *= 2
    return v


def _row_form(cols):
    padded = jnp.concatenate([cols, jnp.zeros((128 - CHUNK, 128), F32)], axis=0)
    return padded.T[:, :CHUNK]


def _chunk_masks():
    ri = lax.broadcasted_iota(jnp.int32, (CHUNK, CHUNK), 0)
    ci = lax.broadcasted_iota(jnp.int32, (CHUNK, CHUNK), 1)
    return ri >= ci, ri > ci, (ri == ci).astype(F32)


def _unit_lower_inverses(ms, eye):
    rs = [eye - m for m in ms]
    ps = [_split(-m) for m in ms]
    s = 2
    while s < CHUNK:
        ps = [_split(_dot3(p, p, NN)) for p in ps]
        r_parts = [_split(r) for r in rs]
        rs = [r + _dot3(p, rp, NN) for r, p, rp in zip(rs, ps, r_parts)]
        s *= 2
    return rs


def _head_elementwise(k, beta, gc, gcr, causal):
    decay = jnp.where(causal, jnp.exp(jnp.where(causal, gc - gcr, 0.0)), 0.0)
    return decay, k * beta, jnp.exp(gc)


def _head_slices(hh):
    return (slice(hh * HEAD_DIM, (hh + 1) * HEAD_DIM),
            slice(GDN_WIDTH + hh * HEAD_DIM, GDN_WIDTH + (hh + 1) * HEAD_DIM),
            slice(2 * GDN_WIDTH + hh * HEAD_DIM, 2 * GDN_WIDTH + (hh + 1) * HEAD_DIM))


def _gdn_chunk_fwd(qkv, bg, name):
    T = qkv.shape[0]
    cb = _tile(T // CHUNK, CHUNKS_PER_STEP)
    rows = cb * CHUNK

    def body(qkv_ref, bg_ref, tinv_ref, u_ref, w_ref, qd_ref, kd_ref, p_ref, cd_ref):
        masks = _chunk_masks()
        causal, strict, eye = masks
        heads = []
        for ci in range(cb):
            rs = slice(ci * CHUNK, (ci + 1) * CHUNK)
            bgv = bg_ref[rs, :]
            gc_all = _chunk_cumsum(bgv)
            gc_rows = _row_form(gc_all)
            cd_ref[rs, :] = jnp.exp(jnp.broadcast_to(gc_all[CHUNK - 1:CHUNK, :], (CHUNK, 128)))
            for hh in range(GDN_HEADS):
                qs, ks, vs = _head_slices(hh)
                q, k, v = qkv_ref[rs, qs], qkv_ref[rs, ks], qkv_ref[rs, vs]
                beta = bgv[:, hh:hh + 1]
                gc = gc_all[:, GDN_HEADS + hh:GDN_HEADS + hh + 1]
                decay, kb, eg = _head_elementwise(k, beta, gc, gc_rows[GDN_HEADS + hh:GDN_HEADS + hh + 1, :], causal)
                hs = slice(hh * HEAD_DIM, (hh + 1) * HEAD_DIM)
                cs = slice(hh * CHUNK, (hh + 1) * CHUNK)
                qd_ref[rs, hs] = (q * eg).astype(qd_ref.dtype)
                kd_ref[rs, hs] = (k * jnp.exp(gc[CHUNK - 1:CHUNK, :] - gc)).astype(kd_ref.dtype)
                heads.append((rs, hs, cs, q, k, v * beta, kb, kb * eg, decay))
        kks = [_mdot(kb, k, NT) for (_, _, _, _, k, _, kb, _, _) in heads]
        qks = [_mdot(q, k, NT) for (_, _, _, q, k, _, _, _, _) in heads]
        tinvs = _unit_lower_inverses([jnp.where(strict, kk * hd[8], 0.0) for kk, hd in zip(kks, heads)], eye)
        t_parts = [_split(t) for t in tinvs]
        us = [_dot3(tp, _split(hd[5]), NN) for tp, hd in zip(t_parts, heads)]
        ws = [_dot3(tp, _split(hd[7]), NN) for tp, hd in zip(t_parts, heads)]
        for hd, tinv, u, w, qk in zip(heads, tinvs, us, ws, qks):
            rs, hs, cs = hd[0], hd[1], hd[2]
            tinv_ref[rs, cs] = tinv
            u_ref[rs, hs] = u
            w_ref[rs, hs] = w.astype(w_ref.dtype)
            p_ref[rs, cs] = jnp.where(causal, qk * hd[8], 0.0).astype(p_ref.dtype)

    def spec(width):
        return pl.BlockSpec((rows, width), lambda n: (n, 0))

    hw, cw = GDN_WIDTH, GDN_HEADS * CHUNK
    return pl.pallas_call(
        body, name=name, grid=(T // rows,),
        in_specs=[spec(QKV_WIDTH), spec(128)],
        out_specs=[spec(cw), spec(hw), spec(hw), spec(hw), spec(hw), spec(cw), spec(128)],
        out_shape=[jax.ShapeDtypeStruct((T, cw), F32), jax.ShapeDtypeStruct((T, hw), F32),
                   jax.ShapeDtypeStruct((T, hw), MXU_DTYPE), jax.ShapeDtypeStruct((T, hw), MXU_DTYPE),
                   jax.ShapeDtypeStruct((T, hw), MXU_DTYPE), jax.ShapeDtypeStruct((T, cw), MXU_DTYPE),
                   jax.ShapeDtypeStruct((T, 128), F32)],
        compiler_params=_params("parallel"),
    )(qkv, bg)


def _gdn_scan_fwd(u, w, qd, kd, p, cd, name):
    T = u.shape[0]
    cb = _tile(T // CHUNK, SCAN_CHUNKS_PER_STEP)
    rows = cb * CHUNK

    def body(u_ref, w_ref, qd_ref, kd_ref, p_ref, cd_ref, o_ref, s_all_ref, vn_ref, s_ref):
        @pl.when(pl.program_id(0) == 0)
        def _():
            s_ref[...] = jnp.zeros_like(s_ref)

        hss = [slice(hh * HEAD_DIM, (hh + 1) * HEAD_DIM) for hh in range(GDN_HEADS)]
        css = [slice(hh * CHUNK, (hh + 1) * CHUNK) for hh in range(GDN_HEADS)]
        s_cur = [s_ref[hh] for hh in range(GDN_HEADS)]
        for ci in range(cb):
            rs = slice(ci * CHUNK, (ci + 1) * CHUNK)
            for hh in range(GDN_HEADS):
                s_all_ref[ci * GDN_WIDTH + hh * HEAD_DIM:ci * GDN_WIDTH + (hh + 1) * HEAD_DIM, :] = s_cur[hh]
            s_ms = [s.astype(MXU_DTYPE) for s in s_cur]
            w_s = [_dot(w_ref[rs, hs], s_m, NN) for hs, s_m in zip(hss, s_ms)]
            q_s = [_dot(qd_ref[rs, hs], s_m, NN) for hs, s_m in zip(hss, s_ms)]
            v_ms = [(u_ref[rs, hs] - ws_).astype(MXU_DTYPE) for hs, ws_ in zip(hss, w_s)]
            k_v = [_dot(kd_ref[rs, hs], v_m, TN) for hs, v_m in zip(hss, v_ms)]
            p_v = [_dot(p_ref[rs, cs], v_m, NN) for cs, v_m in zip(css, v_ms)]
            for hh in range(GDN_HEADS):
                vn_ref[rs, hss[hh]] = v_ms[hh]
                o_ref[rs, hss[hh]] = q_s[hh] + p_v[hh]
                c_dec = cd_ref[ci * CHUNK:ci * CHUNK + 1, GDN_HEADS + hh:GDN_HEADS + hh + 1]
                s_cur[hh] = s_cur[hh] * c_dec + k_v[hh]
        for hh in range(GDN_HEADS):
            s_ref[hh] = s_cur[hh]

    def spec(width):
        return pl.BlockSpec((rows, width), lambda n: (n, 0))

    hw, cw = GDN_WIDTH, GDN_HEADS * CHUNK
    return pl.pallas_call(
        body, name=name, grid=(T // rows,),
        in_specs=[spec(hw), spec(hw), spec(hw), spec(hw), spec(cw), spec(128)],
        out_specs=[spec(hw), pl.BlockSpec((cb * GDN_WIDTH, HEAD_DIM), lambda n: (n, 0)), spec(hw)],
        out_shape=[jax.ShapeDtypeStruct((T, hw), F32),
                   jax.ShapeDtypeStruct((T // CHUNK * GDN_WIDTH, HEAD_DIM), F32),
                   jax.ShapeDtypeStruct((T, hw), MXU_DTYPE)],
        scratch_shapes=[pltpu.VMEM((GDN_HEADS, HEAD_DIM, HEAD_DIM), F32)],
        compiler_params=_params("arbitrary"),
    )(u, w, qd, kd, p, cd)


def _gdn_scan_bwd(do, w, qd, kd, p, cd, s_all, vn, name):
    T = do.shape[0]
    cb = _tile(T // CHUNK, SCAN_CHUNKS_PER_STEP)
    rows = cb * CHUNK
    n_steps = T // rows

    def body(do_ref, w_ref, qd_ref, kd_ref, p_ref, cd_ref, s_all_ref, vn_ref,
             dvn_ref, dw_ref, dqd_ref, dkd_ref, dp_ref, dcd_ref, ds_ref):
        @pl.when(pl.program_id(0) == 0)
        def _():
            ds_ref[...] = jnp.zeros_like(ds_ref)

        causal, _, _ = _chunk_masks()
        lane = lax.broadcasted_iota(jnp.int32, (CHUNK, 128), 1)
        heads = range(GDN_HEADS)
        hss = [slice(hh * HEAD_DIM, (hh + 1) * HEAD_DIM) for hh in heads]
        css = [slice(hh * CHUNK, (hh + 1) * CHUNK) for hh in heads]
        ds_cur = [ds_ref[hh] for hh in heads]
        for ci in reversed(range(cb)):
            rs = slice(ci * CHUNK, (ci + 1) * CHUNK)
            ds_ms = [d.astype(MXU_DTYPE) for d in ds_cur]
            s_olds = [s_all_ref[ci * GDN_WIDTH + hh * HEAD_DIM:ci * GDN_WIDTH + (hh + 1) * HEAD_DIM, :] for hh in heads]
            s_ms = [s.astype(MXU_DTYPE) for s in s_olds]
            do_ms = [do_ref[rs, hs].astype(MXU_DTYPE) for hs in hss]
            p_do = [_dot(p_ref[rs, cs], do_m, TN) for cs, do_m in zip(css, do_ms)]
            k_ds = [_dot(kd_ref[rs, hs], ds_m, NN) for hs, ds_m in zip(hss, ds_ms)]
            q_do = [_dot(qd_ref[rs, hs], do_m, TN) for hs, do_m in zip(hss, do_ms)]
            dqds = [_dot(do_m, s_m, NT) for do_m, s_m in zip(do_ms, s_ms)]
            dkds = [_dot(vn_ref[rs, hs], ds_m, NT) for hs, ds_m in zip(hss, ds_ms)]
            dps = [_dot(do_m, vn_ref[rs, hs], NT) for hs, do_m in zip(hss, do_ms)]
            dv_news = [a + b for a, b in zip(p_do, k_ds)]
            dvn_ms = [d.astype(MXU_DTYPE) for d in dv_news]
            w_dv = [_dot(w_ref[rs, hs], dvn_m, TN) for hs, dvn_m in zip(hss, dvn_ms)]
            dws = [_dot(dvn_m, s_m, NT) for dvn_m, s_m in zip(dvn_ms, s_ms)]
            dcd_tile = jnp.zeros((CHUNK, 128), F32)
            for hh in heads:
                dvn_ref[rs, hss[hh]] = dv_news[hh]
                dw_ref[rs, hss[hh]] = -dws[hh]
                dqd_ref[rs, hss[hh]] = dqds[hh]
                dkd_ref[rs, hss[hh]] = dkds[hh]
                dp_ref[rs, css[hh]] = jnp.where(causal, dps[hh], 0.0)
                dcd = jnp.sum(jnp.sum(s_olds[hh] * ds_cur[hh], axis=1, keepdims=True), axis=0, keepdims=True)
                dcd_tile = jnp.where(lane == GDN_HEADS + hh, dcd, dcd_tile)
                c_dec = cd_ref[ci * CHUNK:ci * CHUNK + 1, GDN_HEADS + hh:GDN_HEADS + hh + 1]
                ds_cur[hh] = c_dec * ds_cur[hh] + q_do[hh] - w_dv[hh]
            dcd_ref[rs, :] = dcd_tile
        for hh in heads:
            ds_ref[hh] = ds_cur[hh]

    def spec(width):
        return pl.BlockSpec((rows, width), lambda n: (n_steps - 1 - n, 0))

    hw, cw = GDN_WIDTH, GDN_HEADS * CHUNK
    return pl.pallas_call(
        body, name=name, grid=(n_steps,),
        in_specs=[spec(hw), spec(hw), spec(hw), spec(hw), spec(cw), spec(128),
                  pl.BlockSpec((cb * GDN_WIDTH, HEAD_DIM), lambda n: (n_steps - 1 - n, 0)), spec(hw)],
        out_specs=[spec(hw), spec(hw), spec(hw), spec(hw), spec(cw), spec(128)],
        out_shape=[jax.ShapeDtypeStruct((T, hw), F32)] * 4
        + [jax.ShapeDtypeStruct((T, cw), F32), jax.ShapeDtypeStruct((T, 128), F32)],
        scratch_shapes=[pltpu.VMEM((GDN_HEADS, HEAD_DIM, HEAD_DIM), F32)],
        compiler_params=_params("arbitrary"),
    )(do, w, qd, kd, p, cd, s_all, vn)


def _gdn_chunk_bwd(qkv, bg, tinv_all, u, w, dvn, dw, dqd, dkd, dp, dcd, name):
    T = qkv.shape[0]
    cb = _tile(T // CHUNK, CHUNKS_PER_STEP)
    rows = cb * CHUNK

    def body(qkv_ref, bg_ref, tinv_ref, u_ref, w_ref, dvn_ref, dw_ref, dqd_ref, dkd_ref, dp_ref, dcd_ref,
             dqkv_ref, dbg_ref):
        masks = _chunk_masks()
        causal, strict, _ = masks
        lane = lax.broadcasted_iota(jnp.int32, (CHUNK, 128), 1)
        row = lax.broadcasted_iota(jnp.int32, (CHUNK, 128), 0)
        heads = []
        for ci in range(cb):
            rs = slice(ci * CHUNK, (ci + 1) * CHUNK)
            bgv = bg_ref[rs, :]
            gc_all = _chunk_cumsum(bgv)
            gc_rows = _row_form(gc_all)
            for hh in range(GDN_HEADS):
                qs, ks, vs = _head_slices(hh)
                q, k = qkv_ref[rs, qs], qkv_ref[rs, ks]
                beta = bgv[:, hh:hh + 1]
                gc = gc_all[:, GDN_HEADS + hh:GDN_HEADS + hh + 1]
                decay, kb, eg = _head_elementwise(k, beta, gc, gc_rows[GDN_HEADS + hh:GDN_HEADS + hh + 1, :], causal)
                heads.append(dict(ci=ci, hh=hh, rs=rs, hs=slice(hh * HEAD_DIM, (hh + 1) * HEAD_DIM),
                                  cs=slice(hh * CHUNK, (hh + 1) * CHUNK), q=q, k=k, beta=beta, gc=gc,
                                  decay=decay, kb=kb, eg=eg))
        for hd in heads:
            hd["t"] = _split(tinv_ref[hd["rs"], hd["cs"]])
        for hd in heads:
            hd["kk"] = _mdot(hd["kb"], hd["k"], NT)
            hd["qk"] = _mdot(hd["q"], hd["k"], NT)
        for hd in heads:
            hd["dvb"] = _dot3(hd["t"], _split(dvn_ref[hd["rs"], hd["hs"]]), TN)
            hd["dkbeg"] = _dot3(hd["t"], _split(dw_ref[hd["rs"], hd["hs"]]), TN)
        for hd in heads:
            rs, hs = hd["rs"], hd["hs"]
            da = -(_mdot(hd["dvb"], u_ref[rs, hs], NT) + _mdot(hd["dkbeg"], w_ref[rs, hs], NT))
            dm = jnp.where(strict, da, 0.0)
            dp_ = dp_ref[rs, hd["cs"]]
            hd["dkk"] = dm * hd["decay"]
            hd["dqk"] = dp_ * hd["decay"]
            hd["e"] = (hd["dkk"] * hd["kk"] + hd["dqk"] * hd["qk"])
        for hd in heads:
            hd["dkb"] = _mdot(hd["dkk"], hd["k"], NN)
            hd["dk"] = _mdot(hd["dkk"], hd["kb"], TN) + _mdot(hd["dqk"], hd["q"], TN)
            hd["dq"] = _mdot(hd["dqk"], hd["k"], NN)
            onehot = (lane == GDN_HEADS + hd["hh"]).astype(jnp.bfloat16)
            e_hi, e_lo = _split(hd["e"])
            hd["col_sums"] = _dot(e_lo, onehot, TN) + _dot(e_hi, onehot, TN)
        tiles = {}
        for hd in heads:
            ci, hh, rs, hs = hd["ci"], hd["hh"], hd["rs"], hd["hs"]
            qs, ks, vs = _head_slices(hh)
            q, k, beta, gc, eg, kb = hd["q"], hd["k"], hd["beta"], hd["gc"], hd["eg"], hd["kb"]
            v = qkv_ref[rs, vs]
            dqd_, dkd_ = dqd_ref[rs, hs], dkd_ref[rs, hs]
            gl = gc[CHUNK - 1:CHUNK, :]
            ek = jnp.exp(gl - gc)
            dkb = hd["dkb"] + hd["dkbeg"] * eg
            deg = jnp.sum(dqd_ * q, axis=1, keepdims=True) + jnp.sum(hd["dkbeg"] * kb, axis=1, keepdims=True)
            dek = jnp.sum(dkd_ * k, axis=1, keepdims=True)
            dcd_ = dcd_ref[ci * CHUNK:ci * CHUNK + 1, GDN_HEADS + hh:GDN_HEADS + hh + 1]
            dgl = jnp.sum(dek * ek, axis=0, keepdims=True) + dcd_ * jnp.exp(gl)
            dgc = jnp.sum(hd["e"], axis=1, keepdims=True) + deg * eg - dek * ek
            dbeta_tile, dgc_tile = tiles.get(ci, (jnp.zeros((CHUNK, 128), F32), jnp.zeros((CHUNK, 128), F32)))
            dgc_tile += jnp.where(lane == GDN_HEADS + hh, dgc, 0.0) - hd["col_sums"]
            dgc_tile += jnp.where((lane == GDN_HEADS + hh) & (row == CHUNK - 1), dgl, 0.0)
            dbeta = jnp.sum(dkb * k, axis=1, keepdims=True) + jnp.sum(hd["dvb"] * v, axis=1, keepdims=True)
            dbeta_tile += jnp.where(lane == hh, dbeta, 0.0)
            tiles[ci] = (dbeta_tile, dgc_tile)
            dqkv_ref[rs, qs] = hd["dq"] + dqd_ * eg
            dqkv_ref[rs, ks] = hd["dk"] + dkd_ * ek + dkb * beta
            dqkv_ref[rs, vs] = hd["dvb"] * beta
        for ci in range(cb):
            dbeta_tile, dgc_tile = tiles[ci]
            dbg_ref[ci * CHUNK:(ci + 1) * CHUNK, :] = dbeta_tile + _chunk_cumsum(dgc_tile, reverse=True)

    def spec(width):
        return pl.BlockSpec((rows, width), lambda n: (n, 0))

    hw, cw = GDN_WIDTH, GDN_HEADS * CHUNK
    return pl.pallas_call(
        body, name=name, grid=(T // rows,),
        in_specs=[spec(QKV_WIDTH), spec(128), spec(cw), spec(hw), spec(hw), spec(hw), spec(hw), spec(hw),
                  spec(hw), spec(cw), spec(128)],
        out_specs=[spec(QKV_WIDTH), spec(128)],
        out_shape=[jax.ShapeDtypeStruct((T, QKV_WIDTH), F32), jax.ShapeDtypeStruct((T, 128), F32)],
        compiler_params=_params("parallel"),
    )(qkv, bg, tinv_all, u, w, dvn, dw, dqd, dkd, dp, dcd)


def _pool_counts(i, tm, rows, offset):
    t = i * tm - offset + lax.broadcasted_iota(jnp.int32, (rows, 1), 0)
    return [jnp.minimum(t + 1, w).astype(F32) for w in POOL_WINDOWS]


def _window_sums(window, forward):
    sums, s, step = [], window, 1
    for _ in POOL_WINDOWS:
        s = s + _shift_rows(s, -step if forward else step)
        sums.append(s)
        step *= 2
    return sums


def _pooled(window, counts):
    sums = _window_sums(window, forward=False)
    out = []
    for gi in range(POOL_GROUPS):
        sl = slice(gi * 128, (gi + 1) * 128)
        out.append(sums[gi][HALO:, sl] / counts[gi] - window[HALO:, sl])
    return out


def _mix_post(o, proj, gdn_norm, pool_w, pool_scale, name):
    T = o.shape[0]
    tm = _tile(T, 256)
    hb = tm // HALO

    def body(o_ref, z_ref, p_ref, ph_ref, gn_ref, pw_ref, ps_ref, out_ref):
        i = pl.program_id(0)
        for hh in range(GDN_HEADS):
            sl = slice(hh * HEAD_DIM, (hh + 1) * HEAD_DIM)
            oh, zh = o_ref[:, sl], z_ref[:, sl]
            ro = lax.rsqrt(jnp.mean(oh * oh, axis=-1, keepdims=True) + EPS)
            out_ref[:, sl] = (((oh * ro) * gn_ref[...]) * (zh * _sigmoid(zh))).astype(out_ref.dtype)
        halo = jnp.where(i == 0, 0.0, ph_ref[...])
        window = jnp.concatenate([halo, p_ref[...]], axis=0)
        pooled = _pooled(window, _pool_counts(i, tm, tm, 0))
        for gi in range(POOL_GROUPS):
            pm = _mdot(pooled[gi], pw_ref[gi], NN)
            out_ref[:, GDN_WIDTH + gi * 128:GDN_WIDTH + (gi + 1) * 128] = (
                pm * ps_ref[:, gi * 128:(gi + 1) * 128]).astype(out_ref.dtype)

    return pl.pallas_call(
        body, name=name, grid=(T // tm,),
        in_specs=[pl.BlockSpec((tm, GDN_WIDTH), lambda i: (i, 0)),
                  pl.BlockSpec((tm, GDN_WIDTH), lambda i: (i, COL_Z // GDN_WIDTH)),
                  pl.BlockSpec((tm, POOL_WIDTH), lambda i: (i, COL_P // POOL_WIDTH)),
                  pl.BlockSpec((HALO, POOL_WIDTH), lambda i: (jnp.maximum(i * hb - 1, 0), COL_P // POOL_WIDTH)),
                  pl.BlockSpec((1, HEAD_DIM), lambda i: (0, 0)),
                  pl.BlockSpec((POOL_GROUPS, 128, 128), lambda i: (0, 0, 0)),
                  pl.BlockSpec((1, POOL_WIDTH), lambda i: (0, 0))],
        out_specs=pl.BlockSpec((tm, GDN_WIDTH + POOL_WIDTH), lambda i: (i, 0)),
        out_shape=jax.ShapeDtypeStruct((T, GDN_WIDTH + POOL_WIDTH), MXU_DTYPE),
        compiler_params=_params("parallel"),
    )(o, proj, proj, proj, gdn_norm, pool_w, pool_scale)


def _mix_post_bwd(dmix, o, proj, gdn_norm, pool_w, pool_scale, name):
    T = o.shape[0]
    tm = _tile(T, 256)
    hb = tm // HALO
    n_tiles = T // tm

    def body(dg_ref, dpo_ref, dpo_next_ref, o_ref, z_ref, p_ref, ph_ref, gn_ref, pw_ref, ps_ref,
             do_ref, dzp_ref, dgn_ref, dpw_ref, dps_ref):
        i = pl.program_id(0)

        @pl.when(i == 0)
        def _():
            dgn_ref[...] = jnp.zeros_like(dgn_ref)
            dpw_ref[...] = jnp.zeros_like(dpw_ref)
            dps_ref[...] = jnp.zeros_like(dps_ref)

        gn = gn_ref[...]
        dgn = jnp.zeros((1, HEAD_DIM), F32)
        for hh in range(GDN_HEADS):
            sl = slice(hh * HEAD_DIM, (hh + 1) * HEAD_DIM)
            oh, zh, dy = o_ref[:, sl], z_ref[:, sl], dg_ref[:, sl]
            ro = lax.rsqrt(jnp.mean(oh * oh, axis=-1, keepdims=True) + EPS)
            on = oh * ro
            sig = _sigmoid(zh)
            sz = zh * sig
            dzp_ref[:, sl] = (dy * (on * gn) * (sig * (1.0 + zh * (1.0 - sig)))).astype(dzp_ref.dtype)
            dgn += jnp.sum(dy * on * sz, axis=0, keepdims=True)
            don = dy * gn * sz
            do_ref[:, sl] = ro * (don - on * jnp.mean(don * on, axis=-1, keepdims=True))
        dgn_ref[...] += dgn

        halo = jnp.where(i == 0, 0.0, ph_ref[...])
        window = jnp.concatenate([halo, p_ref[...]], axis=0)
        counts = _pool_counts(i, tm, tm + HALO, 0)
        pooled = _pooled(window, [cn[:tm] for cn in counts])
        nxt = jnp.where(i == n_tiles - 1, 0.0, dpo_next_ref[...])
        dpo_w = jnp.concatenate([dpo_ref[...], nxt], axis=0)
        ps = ps_ref[...]
        dps = []
        scaled = []
        for gi in range(POOL_GROUPS):
            sl = slice(gi * 128, (gi + 1) * 128)
            dpm = dpo_w[:, sl] * ps[:, sl]
            pm = _mdot(pooled[gi], pw_ref[gi], NN)
            dps.append(jnp.sum(dpo_w[:tm, sl] * pm, axis=0, keepdims=True))
            dpw_ref[gi] += _mdot(pooled[gi], dpm[:tm], TN)
            dpooled = _mdot(dpm, pw_ref[gi], NT)
            scaled.append((dpooled, dpooled / counts[gi]))
        dps_ref[...] += jnp.concatenate(dps, axis=1)
        lead = _window_sums(jnp.concatenate([sc for _, sc in scaled], axis=1), forward=True)
        for gi in range(POOL_GROUPS):
            sl = slice(gi * 128, (gi + 1) * 128)
            dzp_ref[:, GDN_WIDTH + gi * 128:GDN_WIDTH + (gi + 1) * 128] = (
                lead[gi][:tm, sl] - scaled[gi][0][:tm]).astype(dzp_ref.dtype)

    last_halo = T // HALO - 1
    return pl.pallas_call(
        body, name=name, grid=(n_tiles,),
        in_specs=[pl.BlockSpec((tm, GDN_WIDTH), lambda i: (i, 0)),
                  pl.BlockSpec((tm, POOL_WIDTH), lambda i: (i, 1)),
                  pl.BlockSpec((HALO, POOL_WIDTH), lambda i: (jnp.minimum((i + 1) * hb, last_halo), 1)),
                  pl.BlockSpec((tm, GDN_WIDTH), lambda i: (i, 0)),
                  pl.BlockSpec((tm, GDN_WIDTH), lambda i: (i, COL_Z // GDN_WIDTH)),
                  pl.BlockSpec((tm, POOL_WIDTH), lambda i: (i, COL_P // POOL_WIDTH)),
                  pl.BlockSpec((HALO, POOL_WIDTH), lambda i: (jnp.maximum(i * hb - 1, 0), COL_P // POOL_WIDTH)),
                  pl.BlockSpec((1, HEAD_DIM), lambda i: (0, 0)),
                  pl.BlockSpec((POOL_GROUPS, 128, 128), lambda i: (0, 0, 0)),
                  pl.BlockSpec((1, POOL_WIDTH), lambda i: (0, 0))],
        out_specs=[pl.BlockSpec((tm, GDN_WIDTH), lambda i: (i, 0)),
                   pl.BlockSpec((tm, GDN_WIDTH + POOL_WIDTH), lambda i: (i, 0)),
                   pl.BlockSpec((1, HEAD_DIM), lambda i: (0, 0)),
                   pl.BlockSpec((POOL_GROUPS, 128, 128), lambda i: (0, 0, 0)),
                   pl.BlockSpec((1, POOL_WIDTH), lambda i: (0, 0))],
        out_shape=[jax.ShapeDtypeStruct((T, GDN_WIDTH), F32),
                   jax.ShapeDtypeStruct((T, GDN_WIDTH + POOL_WIDTH), MXU_DTYPE),
                   jax.ShapeDtypeStruct((1, HEAD_DIM), F32),
                   jax.ShapeDtypeStruct((POOL_GROUPS, 128, 128), F32),
                   jax.ShapeDtypeStruct((1, POOL_WIDTH), F32)],
        compiler_params=_params("arbitrary"),
    )(dmix, dmix, dmix, o, proj, proj, proj, gdn_norm, pool_w, pool_scale)


def _gdn_prep_bwd(proj, conv_w, a_log_l, dt_bias_l, dqkv, dbg, dzp, name):
    T = proj.shape[0]
    tm = _tile(T, 256)
    hb = tm // 8
    n_tiles = T // tm
    last_halo = T // 8 - 1

    def body(cur_ref, before_ref, after_ref, ba_ref, w_ref, al_ref, dtb_ref, dq_ref, dq_after_ref, dbg_ref,
             dzp_ref, dproj_ref, dw_ref, dal_ref, ddtb_ref):
        i = pl.program_id(0)

        @pl.when(i == 0)
        def _():
            dw_ref[...] = jnp.zeros_like(dw_ref)
            dal_ref[...] = jnp.zeros_like(dal_ref)
            ddtb_ref[...] = jnp.zeros_like(ddtb_ref)

        last = i == n_tiles - 1
        w = w_ref[...]
        before = jnp.where(i == 0, 0.0, before_ref[...])
        after = jnp.where(last, 0.0, after_ref[...])
        window = jnp.concatenate([before, cur_ref[...], after], axis=0)
        y = _conv_act(window, w)
        sig = _sigmoid(y)
        act = y * sig
        dq_w = jnp.concatenate([jnp.zeros((8, QKV_WIDTH), F32), dq_ref[...],
                                jnp.where(last, 0.0, dq_after_ref[...])], axis=0)
        dact = []
        for hh in range(3 * GDN_HEADS):
            sl = slice(hh * HEAD_DIM, (hh + 1) * HEAD_DIM)
            blk, dblk = act[:, sl], dq_w[:, sl]
            if hh < 2 * GDN_HEADS:
                rn = lax.rsqrt(jnp.sum(blk * blk, axis=-1, keepdims=True) + EPS)
                unit = blk * rn
                if hh < GDN_HEADS:
                    dblk = dblk * (HEAD_DIM ** -0.5)
                dblk = rn * (dblk - unit * jnp.sum(dblk * unit, axis=-1, keepdims=True))
            dact.append(dblk)
        dy = jnp.concatenate(dact, axis=1) * (sig * (1.0 + y * (1.0 - sig)))
        dx = dy * w[CONV_K - 1:CONV_K, :]
        dws = [None] * CONV_K
        dws[CONV_K - 1] = jnp.sum(dy[8:8 + tm] * window[8:8 + tm], axis=0, keepdims=True)
        for j in range(CONV_K - 1):
            s = CONV_K - 1 - j
            dx += _shift_rows(dy, -s) * w[j:j + 1, :]
            dws[j] = jnp.sum(dy[8:8 + tm] * _shift_rows(window, s)[8:8 + tm], axis=0, keepdims=True)
        dw_ref[...] += jnp.concatenate(dws, axis=0)
        dproj_ref[:, :QKV_WIDTH] = dx[8:8 + tm].astype(dproj_ref.dtype)
        dproj_ref[:, COL_Z:COL_BA] = dzp_ref[...]

        ba = ba_ref[...]
        dbg_ = dbg_ref[...]
        lane = lax.broadcasted_iota(jnp.int32, ba.shape, 1)
        beta = _sigmoid(ba)
        pre = ba + dtb_ref[...]
        neg_a = -jnp.exp(al_ref[...])
        g = neg_a * _softplus(pre)
        is_g = (lane >= GDN_HEADS) & (lane < 2 * GDN_HEADS)
        da_raw = jnp.where(is_g, dbg_ * neg_a * _sigmoid(pre), 0.0)
        dba = jnp.where(lane < GDN_HEADS, dbg_ * beta * (1.0 - beta), da_raw)
        dproj_ref[:, COL_BA:] = dba.astype(dproj_ref.dtype)
        dal_ref[...] += jnp.sum(jnp.where(is_g, dbg_ * g, 0.0), axis=0, keepdims=True)
        ddtb_ref[...] += jnp.sum(da_raw, axis=0, keepdims=True)

    lane_vec = pl.BlockSpec((1, 128), lambda i: (0, 0))
    return pl.pallas_call(
        body, name=name, grid=(n_tiles,),
        in_specs=[pl.BlockSpec((tm, QKV_WIDTH), lambda i: (i, 0)),
                  pl.BlockSpec((8, QKV_WIDTH), lambda i: (jnp.maximum(i * hb - 1, 0), 0)),
                  pl.BlockSpec((8, QKV_WIDTH), lambda i: (jnp.minimum((i + 1) * hb, last_halo), 0)),
                  pl.BlockSpec((tm, 128), lambda i: (i, COL_BA // 128)),
                  pl.BlockSpec((CONV_K, QKV_WIDTH), lambda i: (0, 0)), lane_vec, lane_vec,
                  pl.BlockSpec((tm, QKV_WIDTH), lambda i: (i, 0)),
                  pl.BlockSpec((8, QKV_WIDTH), lambda i: (jnp.minimum((i + 1) * hb, last_halo), 0)),
                  pl.BlockSpec((tm, 128), lambda i: (i, 0)),
                  pl.BlockSpec((tm, GDN_WIDTH + POOL_WIDTH), lambda i: (i, 0))],
        out_specs=[pl.BlockSpec((tm, D_IN_PAD), lambda i: (i, 0)),
                   pl.BlockSpec((CONV_K, QKV_WIDTH), lambda i: (0, 0)), lane_vec, lane_vec],
        out_shape=[jax.ShapeDtypeStruct((T, D_IN_PAD), MXU_DTYPE),
                   jax.ShapeDtypeStruct((CONV_K, QKV_WIDTH), F32),
                   jax.ShapeDtypeStruct((1, 128), F32), jax.ShapeDtypeStruct((1, 128), F32)],
        compiler_params=_params("arbitrary"),
    )(proj, proj, proj, proj, conv_w, a_log_l, dt_bias_l, dqkv, dqkv, dbg, dzp)


def _mod_part(c_all, w_ada, b_part, name):
    def body(c_ref, w_ref, b_ref, out_ref):
        cc = c_ref[...]
        out_ref[...] = _mdot(cc * _sigmoid(cc), w_ref[...], NN) + b_ref[...]

    return pl.pallas_call(
        body, name=name, out_shape=jax.ShapeDtypeStruct((c_all.shape[0], w_ada.shape[1]), F32),
        compiler_params=_params(),
    )(c_all, w_ada, b_part)


def _w_ada_grad(c_all, dmod_part, name):
    def body(c_ref, d_ref, out_ref):
        cc = c_ref[...]
        out_ref[...] = _mdot(cc * _sigmoid(cc), d_ref[...], TN)

    return pl.pallas_call(
        body, name=name, out_shape=jax.ShapeDtypeStruct((c_all.shape[1], dmod_part.shape[1]), F32),
        compiler_params=_params(),
    )(c_all, dmod_part)


def _sum_parts(parts, name):
    _, R, C = parts.shape
    tr = max([t for t in range(16, min(R, 512) + 1, 16) if R % t == 0], default=R)

    def body(p_ref, out_ref):
        acc = p_ref[0].astype(F32)
        for s in range(1, N_DEV):
            acc += p_ref[s].astype(F32)
        out_ref[...] = acc

    return pl.pallas_call(
        body, name=name, grid=(R // tr,),
        in_specs=[pl.BlockSpec((N_DEV, tr, C), lambda i: (0, i, 0))],
        out_specs=pl.BlockSpec((tr, C), lambda i: (i, 0)),
        out_shape=jax.ShapeDtypeStruct((R, C), F32),
        compiler_params=_params("parallel"),
    )(parts)


def _adamw(w, g, m, v, name):
    R, C = w.shape
    tr = max([t for t in range(8, min(R, 512) + 1, 8) if R % t == 0], default=R)

    def body(w_ref, g_ref, m_ref, v_ref, d_ref, mo_ref, vo_ref):
        gg = g_ref[...]
        mm = ADAM_B1 * m_ref[...] + (1.0 - ADAM_B1) * gg
        vv = ADAM_B2 * v_ref[...] + (1.0 - ADAM_B2) * (gg * gg)
        m_hat = mm / (1.0 - ADAM_B1 ** ADAM_STEP)
        v_hat = vv / (1.0 - ADAM_B2 ** ADAM_STEP)
        d_ref[...] = -ADAM_LR * (m_hat / (jnp.sqrt(v_hat) + ADAM_EPS) + ADAM_WD * w_ref[...])
        mo_ref[...] = mm
        vo_ref[...] = vv

    spec = pl.BlockSpec((tr, C), lambda i: (i, 0))
    return pl.pallas_call(
        body, name=name, grid=(R // tr,),
        in_specs=[spec] * 4, out_specs=[spec] * 3,
        out_shape=[jax.ShapeDtypeStruct((R, C), F32)] * 3,
        compiler_params=_params("parallel"),
    )(w, g, m, v)


def _weight_grad(a, b, name, dep=None):
    return _matmul([(a, b)], TN, WIRE_DTYPE, name, tm=1408, tn=1024, tk=1024, dep=dep)


def _rows_of(flat, lanes=1024):
    flat = flat.reshape(-1)
    n = -(-flat.shape[0] // lanes) * lanes
    return jnp.pad(flat, (0, n - flat.shape[0])).reshape(n // lanes, lanes)


def _pad_rows(a, rows):
    return jnp.pad(a, ((0, rows - a.shape[0]), (0, 0)))


def kernel(x, c, w_ada, b_ada, norm_ffn1, ffn1_gate, ffn1_up, ffn1_down, norm_mix, w_in, conv_w, a_log, dt_bias, gdn_norm, pool_w, pool_scale, w_out, norm_ffn2, ffn2_gate, ffn2_up, ffn2_down, final_norm, loss_target, m_w_ada, m_b_ada, m_norm_ffn1, m_ffn1_gate, m_ffn1_up, m_ffn1_down, m_norm_mix, m_w_in, m_conv_w, m_a_log, m_dt_bias, m_gdn_norm, m_pool_w, m_pool_scale, m_w_out, m_norm_ffn2, m_ffn2_gate, m_ffn2_up, m_ffn2_down, m_final_norm, v_w_ada, v_b_ada, v_norm_ffn1, v_ffn1_gate, v_ffn1_up, v_ffn1_down, v_norm_mix, v_w_in, v_conv_w, v_a_log, v_dt_bias, v_gdn_norm, v_pool_w, v_pool_scale, v_w_out, v_norm_ffn2, v_ffn2_gate, v_ffn2_up, v_ffn2_down, v_final_norm):
    T, D = x.shape[1], x.shape[2]
    Fs = ffn1_gate.shape[2]
    Ws = w_in.shape[2]
    Ws_pad = -(-Ws // 16) * 16
    Os = w_out.shape[1]
    Ms = w_ada.shape[2]
    Cs = conv_w.shape[2]
    me = 4 * lax.axis_index("x") + 2 * lax.axis_index("y") + lax.axis_index("c")
    x0, target = x[0], loss_target[0]

    def wire(a):
        return a.astype(WIRE_DTYPE)

    def token(started):
        return started[4][:1, :1]

    def with_own(landed, own):
        return lax.dynamic_update_slice(landed, own[None], (me, 0, 0))

    def full(landed, off, size, keep=None):
        blk = landed[:, off:off + (size if keep is None else keep), :]
        return blk.reshape(-1, D).astype(MXU_DTYPE)

    no_dep = jnp.zeros((8, 128), F32)
    small = jnp.concatenate([_pad_rows(c, 8), _pad_rows(jnp.pad(conv_w[0], ((0, 0), (0, D - Cs))), 8)], axis=0)
    got = _all_gather(small, "gather_small")
    c_all = got[:, 0, :]
    conv_full = jnp.transpose(got[:, 8:8 + CONV_K, :Cs], (1, 0, 2)).reshape(CONV_K, QKV_WIDTH)
    b_part = lax.dynamic_slice(b_ada, (0, me * Ms), (1, Ms))
    mod_parts = _all_gather(_mod_part(c_all, w_ada[0], b_part, "mod_part"), "gather_mod")
    mod_all = jnp.transpose(mod_parts, (1, 0, 2)).reshape(N_DEV, N_MOD * D)
    mod = lax.dynamic_slice(mod_all, (me, 0), (1, N_MOD * D)).reshape(N_MOD, 1, D)
    sh1, sc1, gt1, sh2, sc2, gt2, sh3, sc3, gt3 = [mod[i] for i in range(N_MOD)]

    w1 = jnp.concatenate([wire(ffn1_gate[0].T), wire(ffn1_up[0].T), wire(ffn1_down[0])], axis=0)
    w23 = jnp.concatenate([wire(_pad_rows(w_in[0].T, Ws_pad)), wire(w_out[0]),
                           wire(ffn2_gate[0].T), wire(ffn2_up[0].T), wire(ffn2_down[0])], axis=0)
    off23 = [0, Ws_pad, Ws_pad + Os, Ws_pad + Os + Fs, Ws_pad + Os + 2 * Fs]
    w1_all = _all_gather(w1, "gather_w1", dep=mod_all)
    wg1_t, wu1_t, wd1 = full(w1_all, 0, Fs), full(w1_all, Fs, Fs), full(w1_all, 2 * Fs, Fs)
    w23_sent = _exchange_start(w23, True, w1_all, "w23_start")

    lane_pad = lambda a: jnp.pad(a, ((0, 0), (GDN_HEADS, 128 - 2 * GDN_HEADS)))
    a_log_l, dt_bias_l = lane_pad(a_log), lane_pad(dt_bias)
    pool_w_m = pool_w[0].astype(MXU_DTYPE)

    h1 = _norm_mod(x0, norm_ffn1, sc1 + token(w23_sent), sh1, "norm1")
    y1, g1, u1, a1 = _ffn_fwd(h1, wg1_t, wu1_t, wd1, "ffn1_fwd")
    x1, h2 = _resid_norm_mod(x0, y1, gt1, 0.5, norm_mix, sc2, sh2, "resid_norm2")
    w23_all = with_own(_exchange_wait(w23_sent, h2, True, "w23_wait"), w23)
    w_in_t = full(w23_all, off23[0], Ws_pad, Ws)
    wo = full(w23_all, off23[1], Os)
    wg2_t, wu2_t, wd2 = full(w23_all, off23[2], Fs), full(w23_all, off23[3], Fs), full(w23_all, off23[4], Fs)
    w_in_re = jnp.concatenate([w_in_t[:COL_Z + GDN_WIDTH], w_in_t[D_IN - POOL_WIDTH:],
                               w_in_t[4 * GDN_WIDTH:4 * GDN_WIDTH + 2 * GDN_HEADS],
                               jnp.zeros((128 - 2 * GDN_HEADS, D), MXU_DTYPE)], axis=0)
    proj = _matmul([(h2, w_in_re)], NT, F32, "proj_in", tm=512, tn=D_IN_PAD, tk=D)
    qkv, bg = _gdn_prep(proj, conv_full, a_log_l, dt_bias_l, "gdn_prep")
    tinv, u_c, w_c, qd_c, kd_c, p_c, cd_c = _gdn_chunk_fwd(qkv, bg, "gdn_chunk_fwd")
    o, s_all, vn_c = _gdn_scan_fwd(u_c, w_c, qd_c, kd_c, p_c, cd_c, "gdn_scan_fwd")
    mix_in = _mix_post(o, proj, gdn_norm, pool_w_m, pool_scale, "mix_post")
    mixed, x2, h3 = _matmul_resid_norm_mod(mix_in, wo, x1, gt2, 1.0, norm_ffn2, sc3, sh3, "mix_out")
    y3, g3, u3, a3 = _ffn_fwd(h3, wg2_t, wu2_t, wd2, "ffn2_fwd")
    loss_row, d3, d_final, dy3, dgt3 = _final_loss(x2, y3, gt3, final_norm.reshape(1, D), target, "final_loss")

    dg3, du3, dh3 = _ffn_bwd(dy3, wd2, g3, u3, wg2_t, wu2_t, "ffn2_bwd")
    d_wd2 = _weight_grad(a3, dy3, "ffn2_dwd")
    d_wg2 = _weight_grad(dg3, h3, "ffn2_dwg")
    d_wu2 = _weight_grad(du3, h3, "ffn2_dwu")
    d2, d_n3, dsc3, dsh3, dmixed, dgt2 = _norm_bwd(dh3, x2, norm_ffn2, sc3, d3, "norm3_bwd",
                                                   produced_by=(mixed, gt2, 1.0))
    dmix_in = _matmul([(dmixed, wo)], NT, F32, "mix_out_bwd", tm=512, tn=GDN_WIDTH + POOL_WIDTH, tk=D)
    d_wo = _matmul([(mix_in, dmixed)], TN, WIRE_DTYPE, "mix_dwo", tm=GDN_WIDTH + POOL_WIDTH, tn=D, tk=1024)
    do, dzp, d_gn, d_pw, d_ps = _mix_post_bwd(dmix_in, o, proj, gdn_norm, pool_w_m, pool_scale, "mix_post_bwd")
    dvn, dw_c, dqd, dkd, dp_c, dcd = _gdn_scan_bwd(do, w_c, qd_c, kd_c, p_c, cd_c, s_all, vn_c, "gdn_scan_bwd")
    dqkv, dbg = _gdn_chunk_bwd(qkv, bg, tinv, u_c, w_c, dvn, dw_c, dqd, dkd, dp_c, dcd, "gdn_chunk_bwd")
    dproj, d_conv, d_al, d_dtb = _gdn_prep_bwd(proj, conv_full, a_log_l, dt_bias_l, dqkv, dbg, dzp, "gdn_prep_bwd")
    d_win_re = _matmul([(dproj, h2)], TN, WIRE_DTYPE, "proj_in_dw", tm=D_IN_PAD, tn=D, tk=1024)
    d_win_t = jnp.concatenate([d_win_re[:COL_Z + GDN_WIDTH], d_win_re[COL_BA:COL_BA + 2 * GDN_HEADS],
                               d_win_re[COL_P:COL_P + POOL_WIDTH]], axis=0)
    d_win_blocks = jnp.pad(d_win_t.reshape(N_DEV, Ws, D), ((0, 0), (0, Ws_pad - Ws), (0, 0)))
    parts23 = jnp.concatenate(
        [wire(d_win_blocks), wire(d_wo.reshape(N_DEV, Os, D)), wire(d_wg2.reshape(N_DEV, Fs, D)),
         wire(d_wu2.reshape(N_DEV, Fs, D)), wire(d_wd2.reshape(N_DEV, Fs, D))], axis=1)
    own23 = lax.dynamic_index_in_dim(parts23, me, 0, keepdims=False)
    g23_sent = _exchange_start(parts23, False, no_dep, "g23_start")
    d1, d_n2, dsc2, dsh2, dy1, dgt1 = _norm_bwd((dproj, w_in_re), x1, norm_mix, sc2 + token(g23_sent), d2,
                                                "proj_in_bwd_norm2_bwd", produced_by=(y1, gt1, 0.5))
    dg1, du1, dh1 = _ffn_bwd(dy1, wd1, g1, u1, wg1_t, wu1_t, "ffn1_bwd")
    grad_x, d_n1, dsc1, dsh1 = _norm_bwd(dh1, x0, norm_ffn1, sc1, d1, "norm1_bwd")

    dmod = jnp.concatenate([dsh1, dsc1, dgt1, dsh2, dsc2, dgt2, dsh3, dsc3, dgt3], axis=0)
    small_rows = [dmod.reshape(-1), d_n1[0], d_n2[0], d_n3[0], d_final[0], d_gn[0], d_ps[0],
                  d_al[0, GDN_HEADS:2 * GDN_HEADS], d_dtb[0, GDN_HEADS:2 * GDN_HEADS], loss_row[0, :1],
                  d_conv.reshape(-1), d_pw.reshape(-1)]
    lanes = 1024
    small_rows = [_rows_of(r, lanes) for r in small_rows]
    n_rows = [r.shape[0] for r in small_rows]
    row_off = [sum(n_rows[:i]) for i in range(len(n_rows))]
    total = -(-sum(n_rows) // 8) * 8
    slab = _pad_rows(jnp.concatenate(small_rows, axis=0), total)
    slab_all = _all_gather(slab, "gather_small_grads")
    summed = _sum_parts(slab_all, "sum_small_grads")

    def piece(idx, n):
        return summed[row_off[idx]:row_off[idx] + n_rows[idx]].reshape(-1)[:n]

    g_b_ada = piece(0, N_MOD * D).reshape(1, N_MOD * D)
    g_n1, g_n2, g_n3 = piece(1, D).reshape(1, D), piece(2, D).reshape(1, D), piece(3, D).reshape(1, D)
    g_final = piece(4, D)
    g_gn = piece(5, HEAD_DIM).reshape(1, HEAD_DIM)
    g_ps = piece(6, POOL_WIDTH).reshape(1, POOL_WIDTH)
    g_al = piece(7, GDN_HEADS).reshape(1, GDN_HEADS)
    g_dtb = piece(8, GDN_HEADS).reshape(1, GDN_HEADS)
    loss = piece(9, 1)[0]
    g_conv = lax.dynamic_slice(piece(10, CONV_K * QKV_WIDTH).reshape(1, CONV_K, QKV_WIDTH), (0, 0, me * Cs),
                               (1, CONV_K, Cs))
    g_pw = piece(11, POOL_GROUPS * 128 * 128).reshape(1, POOL_GROUPS, 128, 128)

    dmod_all = slab_all[:, row_off[0]:row_off[0] + n_rows[0], :].reshape(N_DEV, -1)[:, :N_MOD * D]
    g_w_ada = _w_ada_grad(c_all, lax.dynamic_slice(dmod_all, (0, me * Ms), (N_DEV, Ms)), "w_ada_grad")[None]

    def send_ffn1(a, b, which, dep):
        parts = _weight_grad(a, b, f"ffn1_{which}", dep=dep).reshape(N_DEV, Fs, D)
        own = lax.dynamic_index_in_dim(parts, me, 0, keepdims=False)
        return _exchange_start(parts, False, no_dep, f"g1_{which}_start"), own

    g1_wg, own_wg = send_ffn1(dg1, h1, "dwg", summed)
    g1_wu, own_wu = send_ffn1(du1, h1, "dwu", g1_wg[4])
    g1_wd, own_wd = send_ffn1(a1, dy1, "dwd", g1_wu[4])

    big23 = _sum_parts(with_own(_exchange_wait(g23_sent, g1_wd[4], False, "g23_wait"), own23), "sum_grads23")
    g_rows = dict(w_in=big23[:Ws], w_out=big23[off23[1]:off23[1] + Os],
                  ffn2_gate=big23[off23[2]:off23[2] + Fs], ffn2_up=big23[off23[3]:off23[3] + Fs],
                  ffn2_down=big23[off23[4]:off23[4] + Fs])
    column_sharded = ("w_in", "ffn1_gate", "ffn1_up", "ffn2_gate", "ffn2_up")

    names = ["w_ada", "b_ada", "norm_ffn1", "ffn1_gate", "ffn1_up", "ffn1_down", "norm_mix", "w_in", "conv_w",
             "a_log", "dt_bias", "gdn_norm", "pool_w", "pool_scale", "w_out", "norm_ffn2", "ffn2_gate", "ffn2_up",
             "ffn2_down", "final_norm"]
    weights = dict(zip(names, [w_ada, b_ada, norm_ffn1, ffn1_gate, ffn1_up, ffn1_down, norm_mix, w_in, conv_w,
                               a_log, dt_bias, gdn_norm, pool_w, pool_scale, w_out, norm_ffn2, ffn2_gate, ffn2_up,
                               ffn2_down, final_norm]))
    ms = dict(zip(names, [m_w_ada, m_b_ada, m_norm_ffn1, m_ffn1_gate, m_ffn1_up, m_ffn1_down, m_norm_mix, m_w_in,
                          m_conv_w, m_a_log, m_dt_bias, m_gdn_norm, m_pool_w, m_pool_scale, m_w_out, m_norm_ffn2,
                          m_ffn2_gate, m_ffn2_up, m_ffn2_down, m_final_norm]))
    vs = dict(zip(names, [v_w_ada, v_b_ada, v_norm_ffn1, v_ffn1_gate, v_ffn1_up, v_ffn1_down, v_norm_mix, v_w_in,
                          v_conv_w, v_a_log, v_dt_bias, v_gdn_norm, v_pool_w, v_pool_scale, v_w_out, v_norm_ffn2,
                          v_ffn2_gate, v_ffn2_up, v_ffn2_down, v_final_norm]))
    grads = dict(w_ada=g_w_ada, b_ada=g_b_ada, norm_ffn1=g_n1, norm_mix=g_n2, conv_w=g_conv,
                 a_log=g_al, dt_bias=g_dtb, gdn_norm=g_gn, pool_w=g_pw, pool_scale=g_ps,
                 norm_ffn2=g_n3, final_norm=g_final)
    delta, new_m, new_v = {}, {}, {}

    def adamw_big(n):
        if n in column_sharded:
            view, back = (lambda a: a[0].T), (lambda a: a.T[None])
        else:
            view, back = (lambda a: a[0]), (lambda a: a[None])
        g = g_rows[n] if n in g_rows else view(grads[n])
        d_, m_, v_ = _adamw(view(weights[n]), g, view(ms[n]), view(vs[n]), f"adamw_{n}")
        grads[n], delta[n], new_m[n], new_v[n] = back(g), back(d_), back(m_), back(v_)

    early = ["w_ada", "w_in", "w_out", "ffn2_gate", "ffn2_up", "ffn2_down"]
    late = ["ffn1_gate", "ffn1_up", "ffn1_down"]
    for n in early:
        adamw_big(n)
    done = sum(delta[n][0, :1, :1] for n in early)

    def arrived(started, own, which):
        return _sum_parts(with_own(_exchange_wait(started, done, False, f"g1_{which}_wait"), own), f"sum_{which}")

    g_rows["ffn1_gate"] = arrived(g1_wg, own_wg, "dwg")
    g_rows["ffn1_up"] = arrived(g1_wu, own_wu, "dwu")
    g_rows["ffn1_down"] = arrived(g1_wd, own_wd, "dwd")
    for n in late:
        adamw_big(n)
    small_names = [n for n in names if n not in early + late]
    pack = lambda src: jnp.concatenate([_rows_of(src[n]) for n in small_names], axis=0)
    p_rows = [_rows_of(weights[n]).shape[0] for n in small_names]
    p_total = -(-sum(p_rows) // 8) * 8
    packed = [_pad_rows(pack(src), p_total) for src in (weights, grads, ms, vs)]
    d_s, m_s, v_s = _adamw(*packed, "adamw_small")
    off = 0
    for n, r in zip(small_names, p_rows):
        shp = weights[n].shape
        size = weights[n].size
        for dst, src in ((delta, d_s), (new_m, m_s), (new_v, v_s)):
            dst[n] = src[off:off + r].reshape(-1)[:size].reshape(shp)
        off += r

    return (loss, grad_x[None], *[grads[n] for n in names], *[delta[n] for n in names],
            *[new_m[n] for n in names], *[new_v[n] for n in names])
```

```python
import functools

import jax
import jax.numpy as jnp
from jax import lax
from jax.experimental import pallas as pl
from jax.experimental.pallas import tpu as pltpu

F32 = jnp.float32
MXU_DTYPE = jnp.bfloat16
WIRE_DTYPE = jnp.bfloat16
EPS = 1e-6
N_DEV = 8
GDN_HEADS = 4
HEAD_DIM = 128
GDN_WIDTH = GDN_HEADS * HEAD_DIM
POOL_WINDOWS = (2, 4, 8, 16)
POOL_GROUPS = len(POOL_WINDOWS)
POOL_WIDTH = 512
CONV_K = 4
CHUNK = 64
QKV_WIDTH = 3 * GDN_WIDTH
D_IN = 4 * GDN_WIDTH + 2 * GDN_HEADS + POOL_WIDTH
D_IN_PAD = 4 * GDN_WIDTH + POOL_WIDTH + 128
COL_Z = QKV_WIDTH
COL_P = 4 * GDN_WIDTH
COL_BA = 4 * GDN_WIDTH + POOL_WIDTH
N_MOD = 9
HALO = 16
VMEM_LIMIT = 56 * 1024 * 1024
ADAM_LR, ADAM_B1, ADAM_B2, ADAM_EPS, ADAM_WD, ADAM_STEP = 0.001, 0.9, 0.999, 1e-08, 0.01, 10
FFN_TOKEN_TILE = 256
FFN_HIDDEN_TILE = 1408
CHUNKS_PER_STEP = 4
SCAN_CHUNKS_PER_STEP = 4

NT = (((1,), (1,)), ((), ()))
NN = (((1,), (0,)), ((), ()))
TN = (((0,), (0,)), ((), ()))


def _params(*sem):
    return pltpu.CompilerParams(dimension_semantics=tuple(sem), vmem_limit_bytes=VMEM_LIMIT)


def _dot(a, b, dims):
    return lax.dot_general(a, b, dims, preferred_element_type=F32)


def _mdot(a, b, dims):
    return _dot(a.astype(MXU_DTYPE), b.astype(MXU_DTYPE), dims)


def _split(a):
    hi = a.astype(jnp.bfloat16)
    return hi, (a - hi.astype(F32)).astype(jnp.bfloat16)


def _dot3(a, b, dims):
    (ah, al), (bh, bl) = a, b
    return (_dot(al, bh, dims) + _dot(ah, bl, dims)) + _dot(ah, bh, dims)


def _sigmoid(v):
    return 1.0 / (1.0 + jnp.exp(-v))


def _softplus(v):
    return jnp.maximum(v, 0.0) + jnp.log(1.0 + jnp.exp(-jnp.abs(v)))


def _shift_rows(v, s):
    n = v.shape[0]
    s = s % n
    return v if s == 0 else pltpu.roll(v, s, 0)


def _tile(n, want):
    t = min(n, want)
    while n % t:
        t //= 2
    return t


def _all_gather(block, name, dep=None):
    shape, dtype = block.shape, block.dtype

    def body(x_ref, *refs):
        out_ref, send_sems, recv_sems, local_sem = refs[-4:]
        x, y, c = lax.axis_index("x"), lax.axis_index("y"), lax.axis_index("c")
        me, sibling = (x, y, c), (x, y, 1 - c)
        chips = [(1 - x, y), (x, 1 - y), (1 - x, 1 - y)]

        def rows(px, py, pc):
            return out_ref.at[4 * px + 2 * py + pc]

        def copy(k, blk, to, src=None):
            return pltpu.make_async_remote_copy(
                src_ref=rows(*blk) if src is None else src, dst_ref=rows(*blk),
                send_sem=send_sems.at[k], recv_sem=recv_sems.at[k],
                device_id=to, device_id_type=pl.DeviceIdType.MESH)

        mine = pltpu.make_async_copy(x_ref, rows(*me), local_sem)
        mine.start()
        first = [copy(0, me, sibling, src=x_ref)]
        first += [copy(1 + j, me, (*chip, c), src=x_ref) for j, chip in enumerate(chips)]
        for cp in first:
            cp.start()
        passed = [copy(4 + j, (*chip, c), sibling) for j, chip in enumerate(chips)]
        for j, chip in enumerate(chips):
            copy(1 + j, (*chip, c), me).wait_recv()
            passed[j].start()
        copy(0, sibling, me).wait_recv()
        for j, chip in enumerate(chips):
            copy(4 + j, (*chip, 1 - c), me).wait_recv()
        for cp in first + passed:
            cp.wait_send()
        mine.wait()

    return pl.pallas_call(
        body, name=name,
        out_shape=jax.ShapeDtypeStruct((N_DEV,) + shape, dtype),
        in_specs=[pl.BlockSpec(memory_space=pltpu.HBM)] + [pl.BlockSpec(memory_space=pl.ANY)] * (dep is not None),
        out_specs=pl.BlockSpec(memory_space=pltpu.HBM),
        scratch_shapes=[pltpu.SemaphoreType.DMA((7,)), pltpu.SemaphoreType.DMA((7,)),
                        pltpu.SemaphoreType.DMA(())],
    )(*([block] if dep is None else [block, dep]))


_HBM = pl.BlockSpec(memory_space=pltpu.HBM)
_SEM = pl.BlockSpec(memory_space=pltpu.SEMAPHORE)
_ANY = pl.BlockSpec(memory_space=pl.ANY)
_EFFECT = pltpu.SideEffectType.DATAFLOW_SIDE_EFFECTING
_FLIPS = [(0, 0, 1), (0, 1, 0), (0, 1, 1), (1, 0, 0), (1, 0, 1), (1, 1, 0), (1, 1, 1)]


def _peers():
    x, y, c = lax.axis_index("x"), lax.axis_index("y"), lax.axis_index("c")
    return 4 * x + 2 * y + c, [(1 - x if fx else x, 1 - y if fy else y, 1 - c if fc else c)
                               for fx, fy, fc in _FLIPS]


def _exchange_start(src, gather, dep, name):
    block = src.shape if gather else src.shape[1:]
    land = (N_DEV,) + tuple(block)

    def body(src_ref, land_ref, dep_ref, send_sems, recv_sems, land_thru, token):
        me, peers = _peers()
        for k, (px, py, pc) in enumerate(peers):
            pltpu.make_async_remote_copy(
                src_ref=src_ref if gather else src_ref.at[4 * px + 2 * py + pc], dst_ref=land_ref.at[me],
                send_sem=send_sems.at[k], recv_sem=recv_sems.at[k],
                device_id=(px, py, pc), device_id_type=pl.DeviceIdType.MESH).start()
        token[...] = jnp.zeros_like(token)

    src = pltpu.with_memory_space_constraint(src, pltpu.HBM)
    send_sems, recv_sems, land_thru, token = pl.pallas_call(
        body, name=name,
        out_shape=(pltpu.SemaphoreType.DMA((7,)), pltpu.SemaphoreType.DMA((7,)),
                   pltpu.HBM(land, src.dtype), jax.ShapeDtypeStruct((8, 128), F32)),
        in_specs=(_HBM, _HBM, _ANY),
        out_specs=(_SEM, _SEM, _HBM, pl.BlockSpec(memory_space=pltpu.VMEM)),
        input_output_aliases={1: 2},
        compiler_params=pltpu.CompilerParams(has_side_effects=_EFFECT),
    )(src, pltpu.with_memory_space_constraint(lax.empty(land, src.dtype), pltpu.HBM), dep)
    return send_sems, recv_sems, src, land_thru, token


def _exchange_wait(started, after, gather, name):
    send_sems, recv_sems, src, land_thru, _ = started

    def body(src_ref, land_ref, send_sems, recv_sems, after_ref, got_ref):
        _, peers = _peers()
        for k, peer in enumerate(peers):
            copy = pltpu.make_async_remote_copy(
                src_ref=src_ref if gather else src_ref.at[0], dst_ref=land_ref.at[0],
                send_sem=send_sems.at[k], recv_sem=recv_sems.at[k],
                device_id=peer, device_id_type=pl.DeviceIdType.MESH)
            copy.wait_send()
            copy.wait_recv()

    return pl.pallas_call(
        body, name=name,
        out_shape=pltpu.HBM(land_thru.shape, land_thru.dtype),
        in_specs=(_HBM, _HBM, _SEM, _SEM, _ANY), out_specs=_HBM,
        input_output_aliases={1: 0},
        compiler_params=pltpu.CompilerParams(has_side_effects=_EFFECT),
    )(src, land_thru, send_sems, recv_sems, after)


def _matmul(pairs, dims, out_dtype, name, tm=512, tn=512, tk=512, dep=None):
    a0, b0 = pairs[0]
    if dims == TN:
        K, M = a0.shape
    else:
        M, K = a0.shape
    N = b0.shape[0] if dims == NT else b0.shape[1]
    tm, tn, tk = _tile(M, tm), _tile(N, tn), _tile(K, tk)
    nk = K // tk
    n_pairs = len(pairs)
    n_in = 2 * n_pairs + (dep is not None)

    def body(*refs):
        out_ref = refs[n_in]

        def product():
            total = _dot(refs[0][...], refs[1][...], dims)
            for p in range(1, n_pairs):
                total += _dot(refs[2 * p][...], refs[2 * p + 1][...], dims)
            return total

        if nk == 1:
            out_ref[...] = product().astype(out_ref.dtype)
            return
        acc_ref = refs[n_in + 1]
        k = pl.program_id(2)

        @pl.when(k == 0)
        def _():
            acc_ref[...] = product()

        @pl.when((k > 0) & (k < nk - 1))
        def _():
            acc_ref[...] += product()

        @pl.when(k == nk - 1)
        def _():
            out_ref[...] = (acc_ref[...] + product()).astype(out_ref.dtype)

    if dims == TN:
        a_spec = pl.BlockSpec((tk, tm), lambda i, j, k: (k, i))
    else:
        a_spec = pl.BlockSpec((tm, tk), lambda i, j, k: (i, k))
    if dims == NT:
        b_spec = pl.BlockSpec((tn, tk), lambda i, j, k: (j, k))
    else:
        b_spec = pl.BlockSpec((tk, tn), lambda i, j, k: (k, j))
    args, specs = [], []
    for a, b in pairs:
        args += [a, b]
        specs += [a_spec, b_spec]
    if dep is not None:
        args.append(dep)
        specs.append(_ANY)
    return pl.pallas_call(
        body, name=name, grid=(M // tm, N // tn, nk),
        in_specs=specs, out_specs=pl.BlockSpec((tm, tn), lambda i, j, k: (i, j)),
        out_shape=jax.ShapeDtypeStruct((M, N), out_dtype),
        scratch_shapes=[pltpu.VMEM((tm, tn), F32)] * (nk > 1),
        compiler_params=_params("parallel", "parallel", "arbitrary"),
    )(*args)


def _vec_spec(d):
    return pl.BlockSpec((1, d), lambda i: (0, 0))


def _matmul_resid_norm_mod(a, b, x, gate, coef, nw, scale, shift, name):
    T, K = a.shape
    D = b.shape[1]
    tm = _tile(T, 512)

    def body(a_ref, b_ref, x_ref, g_ref, nw_ref, sc_ref, sh_ref, y_ref, xo_ref, h_ref):
        y = _dot(a_ref[...], b_ref[...], NN)
        y_ref[...] = y
        xf = x_ref[...] + (coef * g_ref[...]) * y
        xo_ref[...] = xf
        r = lax.rsqrt(jnp.mean(xf * xf, axis=-1, keepdims=True) + EPS)
        h_ref[...] = ((xf * r) * nw_ref[...] * (1.0 + sc_ref[...]) + sh_ref[...]).astype(h_ref.dtype)

    row = pl.BlockSpec((tm, D), lambda i: (i, 0))
    vec = _vec_spec(D)
    return pl.pallas_call(
        body, name=name, grid=(T // tm,),
        in_specs=[pl.BlockSpec((tm, K), lambda i: (i, 0)), _resident((K, D)), row, vec, vec, vec, vec],
        out_specs=[row, row, row],
        out_shape=[jax.ShapeDtypeStruct((T, D), F32), jax.ShapeDtypeStruct((T, D), F32),
                   jax.ShapeDtypeStruct((T, D), MXU_DTYPE)],
        compiler_params=_params("parallel"),
    )(a, b, x, gate, nw, scale, shift)


def _norm_bwd(dh, x, nw, scale, dres, name, produced_by=None):
    T, D = x.shape
    tm = _tile(T, 512)

    def body(*refs):
        if isinstance(dh, tuple):
            dh_value = _dot(refs[0][...], refs[1][...], NN)
            refs = refs[1:]
        else:
            dh_value = refs[0][...]
        _, x_ref, nw_ref, sc_ref, dr_ref = refs[:5]
        n_in = 5 if produced_by is None else 7
        dx_ref, dnw_ref, dsc_ref, dsh_ref = refs[n_in:n_in + 4]

        @pl.when(pl.program_id(0) == 0)
        def _():
            dnw_ref[...] = jnp.zeros_like(dnw_ref)
            dsc_ref[...] = jnp.zeros_like(dsc_ref)
            dsh_ref[...] = jnp.zeros_like(dsh_ref)
            if produced_by is not None:
                refs[n_in + 5][...] = jnp.zeros_like(refs[n_in + 5])

        xf, dh_ = x_ref[...], dh_value
        r = lax.rsqrt(jnp.mean(xf * xf, axis=-1, keepdims=True) + EPS)
        xn = xf * r
        one_sc = 1.0 + sc_ref[...]
        dsh_ref[...] += jnp.sum(dh_, axis=0, keepdims=True)
        t = dh_ * xn
        dsc_ref[...] += jnp.sum(t, axis=0, keepdims=True) * nw_ref[...]
        dnw_ref[...] += jnp.sum(t, axis=0, keepdims=True) * one_sc
        dxn = dh_ * (nw_ref[...] * one_sc)
        dx = dr_ref[...] + r * (dxn - xn * jnp.mean(dxn * xn, axis=-1, keepdims=True))
        dx_ref[...] = dx
        if produced_by is not None:
            y_ref, g_ref, dy_ref, dg_ref = refs[5], refs[6], refs[n_in + 4], refs[n_in + 5]
            dy_ref[...] = ((produced_by[2] * g_ref[...]) * dx).astype(dy_ref.dtype)
            dg_ref[...] += produced_by[2] * jnp.sum(dx * y_ref[...], axis=0, keepdims=True)

    row = pl.BlockSpec((tm, D), lambda i: (i, 0))
    vec = _vec_spec(D)
    vec_out = jax.ShapeDtypeStruct((1, D), F32)
    if isinstance(dh, tuple):
        k_dim = dh[0].shape[1]
        args, in_specs = [dh[0], dh[1]], [pl.BlockSpec((tm, k_dim), lambda i: (i, 0)), _resident((k_dim, D))]
    else:
        args, in_specs = [dh], [row]
    args, in_specs = args + [x, nw, scale, dres], in_specs + [row, vec, vec, row]
    out_specs, out_shape = [row, vec, vec, vec], [jax.ShapeDtypeStruct((T, D), F32), vec_out, vec_out, vec_out]
    if produced_by is not None:
        args += [produced_by[0], produced_by[1]]
        in_specs += [row, vec]
        out_specs += [row, vec]
        out_shape += [jax.ShapeDtypeStruct((T, D), MXU_DTYPE), vec_out]
    return pl.pallas_call(
        body, name=name, grid=(T // tm,),
        in_specs=in_specs, out_specs=out_specs, out_shape=out_shape,
        compiler_params=_params("arbitrary"),
    )(*args)


def _rms(xf):
    r = lax.rsqrt(jnp.mean(xf * xf, axis=-1, keepdims=True) + EPS)
    return r, xf * r


def _norm_bwd_math(dh, xf, nw, sc):
    r, xn = _rms(xf)
    t = dh * xn
    dxn = dh * (nw * (1.0 + sc))
    dx = r * (dxn - xn * jnp.mean(dxn * xn, axis=-1, keepdims=True))
    return dx, jnp.sum(dh, axis=0, keepdims=True), jnp.sum(t, axis=0, keepdims=True)


def _swiglu_fwd(h, wg_t, wu_t, wd, name, norm_in=None, resid_out=None, loss_out=None):
    T, D = h.shape
    Fdim = wd.shape[0]
    tm, tf = _tile(T, FFN_TOKEN_TILE), _tile(Fdim, FFN_HIDDEN_TILE)
    row = pl.BlockSpec((tm, D), lambda i: (i, 0))
    frow = pl.BlockSpec((tm, Fdim), lambda i: (i, 0))
    vec, wres = _vec_spec(D), _resident((Fdim, D))
    f32_row, mxu_row = jax.ShapeDtypeStruct((T, D), F32), jax.ShapeDtypeStruct((T, D), MXU_DTYPE)
    vec_out = jax.ShapeDtypeStruct((1, D), F32)
    args, in_specs = [h, wg_t, wu_t, wd], [row, wres, wres, wres]
    out_shape, out_specs = [jax.ShapeDtypeStruct((T, Fdim), MXU_DTYPE)] * 3, [frow] * 3
    if norm_in is not None:
        args += list(norm_in)
        in_specs += [vec] * 3
        out_shape, out_specs = out_shape + [mxu_row], out_specs + [row]
    if resid_out is not None:
        x, gate, coef, nw, sc, sh = resid_out
        args += [x, gate, nw, sc, sh]
        in_specs += [row, vec, vec, vec, vec]
        out_shape, out_specs = out_shape + [f32_row, f32_row, mxu_row], out_specs + [row, row, row]
    if loss_out is not None:
        args += list(loss_out)
        in_specs += [row, vec, vec, row]
        out_shape += [jax.ShapeDtypeStruct((1, 128), F32), f32_row, vec_out, mxu_row, vec_out]
        out_specs += [pl.BlockSpec((1, 128), lambda i: (0, 0)), row, vec, row, vec]

    def body(*refs):
        it = iter(refs)
        h_ref, wg_ref, wu_ref, wd_ref = next(it), next(it), next(it), next(it)
        norm_refs = [next(it) for _ in range(3)] if norm_in is not None else None
        resid_refs = [next(it) for _ in range(5)] if resid_out is not None else None
        loss_refs = [next(it) for _ in range(4)] if loss_out is not None else None
        g_ref, u_ref, a_ref = next(it), next(it), next(it)
        if norm_in is not None:
            nw_ref, sc_ref, sh_ref = norm_refs
            _, xn = _rms(h_ref[...])
            hh = (xn * nw_ref[...] * (1.0 + sc_ref[...]) + sh_ref[...]).astype(MXU_DTYPE)
            next(it)[...] = hh
        else:
            hh = h_ref[...]
        y = None
        for k in range(Fdim // tf):
            ks = slice(k * tf, (k + 1) * tf)
            g = _dot(hh, wg_ref[ks, :], NT)
            u = _dot(hh, wu_ref[ks, :], NT)
            a = ((g * _sigmoid(g)) * u).astype(a_ref.dtype)
            g_ref[:, ks] = g.astype(g_ref.dtype)
            u_ref[:, ks] = u.astype(u_ref.dtype)
            a_ref[:, ks] = a
            part = _dot(a, wd_ref[ks, :], NN)
            y = part if y is None else y + part
        if resid_out is not None:
            x_ref, gt_ref, nw_ref, sc_ref, sh_ref = resid_refs
            y_ref, xo_ref, hn_ref = next(it), next(it), next(it)
            y_ref[...] = y
            xf = x_ref[...] + (resid_out[2] * gt_ref[...]) * y
            xo_ref[...] = xf
            _, xn = _rms(xf)
            hn_ref[...] = (xn * nw_ref[...] * (1.0 + sc_ref[...]) + sh_ref[...]).astype(hn_ref.dtype)
        if loss_out is not None:
            x_ref, gt_ref, fw_ref, t_ref = loss_refs
            loss_ref, dx_ref, dfw_ref, dy_ref, dg_ref = [next(it) for _ in range(5)]

            @pl.when(pl.program_id(0) == 0)
            def _():
                loss_ref[...] = jnp.zeros_like(loss_ref)
                dfw_ref[...] = jnp.zeros_like(dfw_ref)
                dg_ref[...] = jnp.zeros_like(dg_ref)

            r, xn = _rms(x_ref[...] + (0.5 * gt_ref[...]) * y)
            err = xn * fw_ref[...] - t_ref[...]
            per_tok = jnp.mean(err * err, axis=-1, keepdims=True)
            loss_ref[...] += 0.5 * jnp.sum(per_tok, axis=0, keepdims=True)
            d_out = err * (1.0 / D)
            dfw_ref[...] += jnp.sum(d_out * xn, axis=0, keepdims=True)
            dxn = d_out * fw_ref[...]
            dx = r * (dxn - xn * jnp.mean(dxn * xn, axis=-1, keepdims=True))
            dx_ref[...] = dx
            dy_ref[...] = ((0.5 * gt_ref[...]) * dx).astype(dy_ref.dtype)
            dg_ref[...] += 0.5 * jnp.sum(dx * y, axis=0, keepdims=True)

    return pl.pallas_call(
        body, name=name, grid=(T // tm,), in_specs=in_specs, out_specs=out_specs, out_shape=out_shape,
        compiler_params=_params("arbitrary" if loss_out is not None else "parallel"),
    )(*args)


def _swiglu_bwd(dy, wd, g, u, wg_t, wu_t, name, norm_in, dres, produced_by=None):
    T, D = dy.shape
    Fdim = wd.shape[0]
    tm, tf = _tile(T, FFN_TOKEN_TILE), _tile(Fdim, FFN_HIDDEN_TILE)
    row = pl.BlockSpec((tm, D), lambda i: (i, 0))
    frow = pl.BlockSpec((tm, Fdim), lambda i: (i, 0))
    vec, wres = _vec_spec(D), _resident((Fdim, D))
    vec_out = jax.ShapeDtypeStruct((1, D), F32)
    args, in_specs = [dy, wd, g, u, wg_t, wu_t, *norm_in, dres], [row, wres, frow, frow, wres, wres, row, vec, vec, row]
    out_shape = [jax.ShapeDtypeStruct((T, Fdim), MXU_DTYPE)] * 2 + [jax.ShapeDtypeStruct((T, D), F32)] + [vec_out] * 3
    out_specs = [frow, frow, row, vec, vec, vec]
    if produced_by is not None:
        args += [produced_by[0], produced_by[1]]
        in_specs += [row, vec]
        out_shape += [jax.ShapeDtypeStruct((T, D), MXU_DTYPE), vec_out]
        out_specs += [row, vec]

    def body(*refs):
        it = iter(refs)
        dy_ref, wd_ref, g_ref, u_ref, wg_ref, wu_ref, x_ref, nw_ref, sc_ref, dr_ref = [next(it) for _ in range(10)]
        prev_refs = [next(it), next(it)] if produced_by is not None else None
        dg_ref, du_ref, dx_ref, dnw_ref, dsc_ref, dsh_ref = [next(it) for _ in range(6)]
        prev_out = [next(it), next(it)] if produced_by is not None else None

        @pl.when(pl.program_id(0) == 0)
        def _():
            dnw_ref[...] = jnp.zeros_like(dnw_ref)
            dsc_ref[...] = jnp.zeros_like(dsc_ref)
            dsh_ref[...] = jnp.zeros_like(dsh_ref)
            if produced_by is not None:
                prev_out[1][...] = jnp.zeros_like(prev_out[1])

        dyy = dy_ref[...]
        dh = None
        for k in range(Fdim // tf):
            ks = slice(k * tf, (k + 1) * tf)
            da = _dot(dyy, wd_ref[ks, :], NT)
            gg = g_ref[:, ks].astype(F32)
            sig = _sigmoid(gg)
            dg = (da * u_ref[:, ks].astype(F32) * (sig * (1.0 + gg * (1.0 - sig)))).astype(dg_ref.dtype)
            du = (da * (gg * sig)).astype(du_ref.dtype)
            dg_ref[:, ks] = dg
            du_ref[:, ks] = du
            part = _dot(dg, wg_ref[ks, :], NN) + _dot(du, wu_ref[ks, :], NN)
            dh = part if dh is None else dh + part
        dx_norm, dsh_row, t_row = _norm_bwd_math(dh, x_ref[...], nw_ref[...], sc_ref[...])
        dsh_ref[...] += dsh_row
        dsc_ref[...] += t_row * nw_ref[...]
        dnw_ref[...] += t_row * (1.0 + sc_ref[...])
        dx = dr_ref[...] + dx_norm
        dx_ref[...] = dx
        if produced_by is not None:
            prev_out[0][...] = ((produced_by[2] * prev_refs[1][...]) * dx).astype(prev_out[0].dtype)
            prev_out[1][...] += produced_by[2] * jnp.sum(dx * prev_refs[0][...], axis=0, keepdims=True)

    return pl.pallas_call(
        body, name=name, grid=(T // tm,), in_specs=in_specs, out_specs=out_specs, out_shape=out_shape,
        compiler_params=_params("arbitrary"),
    )(*args)


def _resident(shape):
    return pl.BlockSpec(shape, lambda i: (0,) * len(shape), pipeline_mode=pl.Buffered(1))


def _conv_act(window, w):
    y = window * w[CONV_K - 1:CONV_K, :]
    for j in range(CONV_K - 1):
        y += _shift_rows(window, CONV_K - 1 - j) * w[j:j + 1, :]
    return y


def _gdn_prep(proj, conv_w, a_log_l, dt_bias_l, name):
    T = proj.shape[0]
    tm = _tile(T, 256)
    hb = tm // 8

    def body(cur_ref, halo_ref, ba_ref, w_ref, al_ref, dtb_ref, qkv_ref, bg_ref):
        i = pl.program_id(0)
        halo = jnp.where(i == 0, 0.0, halo_ref[...])
        window = jnp.concatenate([halo, cur_ref[...]], axis=0)
        y = _conv_act(window, w_ref[...])[8:, :]
        act = y * _sigmoid(y)
        for hh in range(3 * GDN_HEADS):
            blk = act[:, hh * HEAD_DIM:(hh + 1) * HEAD_DIM]
            if hh < 2 * GDN_HEADS:
                rn = lax.rsqrt(jnp.sum(blk * blk, axis=-1, keepdims=True) + EPS)
                blk = blk * rn
                if hh < GDN_HEADS:
                    blk = blk * (HEAD_DIM ** -0.5)
            qkv_ref[:, hh * HEAD_DIM:(hh + 1) * HEAD_DIM] = blk
        ba = ba_ref[...]
        lane = lax.broadcasted_iota(jnp.int32, ba.shape, 1)
        beta = _sigmoid(ba)
        g = -jnp.exp(al_ref[...]) * _softplus(ba + dtb_ref[...])
        bg_ref[...] = jnp.where(lane < GDN_HEADS, beta, jnp.where(lane < 2 * GDN_HEADS, g, 0.0))

    return pl.pallas_call(
        body, name=name, grid=(T // tm,),
        in_specs=[pl.BlockSpec((tm, QKV_WIDTH), lambda i: (i, 0)),
                  pl.BlockSpec((8, QKV_WIDTH), lambda i: (jnp.maximum(i * hb - 1, 0), 0)),
                  pl.BlockSpec((tm, 128), lambda i: (i, COL_BA // 128)),
                  pl.BlockSpec((CONV_K, QKV_WIDTH), lambda i: (0, 0)),
                  pl.BlockSpec((1, 128), lambda i: (0, 0)), pl.BlockSpec((1, 128), lambda i: (0, 0))],
        out_specs=[pl.BlockSpec((tm, QKV_WIDTH), lambda i: (i, 0)), pl.BlockSpec((tm, 128), lambda i: (i, 0))],
        out_shape=[jax.ShapeDtypeStruct((T, QKV_WIDTH), F32), jax.ShapeDtypeStruct((T, 128), F32)],
        compiler_params=_params("parallel"),
    )(proj, proj, proj, conv_w, a_log_l, dt_bias_l)


def _chunk_cumsum(v, reverse=False):
    row = lax.broadcasted_iota(jnp.int32, v.shape, 0)
    s = 1
    while s < CHUNK:
        if reverse:
            v = v + jnp.where(row < CHUNK - s, _shift_rows(v, -s), 0.0)
        else:
            v = v + jnp.where(row >= s, _shift_rows(v, s), 0.0)
        s *= 2
    return v


def _row_form(cols):
    padded = jnp.concatenate([cols, jnp.zeros((128 - CHUNK, 128), F32)], axis=0)
    return padded.T[:, :CHUNK]


def _chunk_masks():
    ri = lax.broadcasted_iota(jnp.int32, (CHUNK, CHUNK), 0)
    ci = lax.broadcasted_iota(jnp.int32, (CHUNK, CHUNK), 1)
    return ri >= ci, ri > ci, (ri == ci).astype(F32)


def _unit_lower_inverses(ms, eye):
    rs = [eye - m for m in ms]
    ps = [_split(-m) for m in ms]
    s = 2
    while s < CHUNK:
        ps = [_split(_dot3(p, p, NN)) for p in ps]
        r_parts = [_split(r) for r in rs]
        rs = [r + _dot3(p, rp, NN) for r, p, rp in zip(rs, ps, r_parts)]
        s *= 2
    return rs


def _head_elementwise(k, beta, gc, gcr, causal):
    decay = jnp.where(causal, jnp.exp(jnp.where(causal, gc - gcr, 0.0)), 0.0)
    return decay, k * beta, jnp.exp(gc)


def _head_slices(hh):
    return (slice(hh * HEAD_DIM, (hh + 1) * HEAD_DIM),
            slice(GDN_WIDTH + hh * HEAD_DIM, GDN_WIDTH + (hh + 1) * HEAD_DIM),
            slice(2 * GDN_WIDTH + hh * HEAD_DIM, 2 * GDN_WIDTH + (hh + 1) * HEAD_DIM))


def _gdn_chunk_fwd(qkv, bg, name):
    T = qkv.shape[0]
    cb = _tile(T // CHUNK, CHUNKS_PER_STEP)
    rows = cb * CHUNK

    def body(qkv_ref, bg_ref, tinv_ref, u_ref, w_ref, qd_ref, kd_ref, p_ref, cd_ref):
        masks = _chunk_masks()
        causal, strict, eye = masks
        heads = []
        for ci in range(cb):
            rs = slice(ci * CHUNK, (ci + 1) * CHUNK)
            bgv = bg_ref[rs, :]
            gc_all = _chunk_cumsum(bgv)
            gc_rows = _row_form(gc_all)
            cd_ref[rs, :] = jnp.exp(jnp.broadcast_to(gc_all[CHUNK - 1:CHUNK, :], (CHUNK, 128)))
            for hh in range(GDN_HEADS):
                qs, ks, vs = _head_slices(hh)
                q, k, v = qkv_ref[rs, qs], qkv_ref[rs, ks], qkv_ref[rs, vs]
                beta = bgv[:, hh:hh + 1]
                gc = gc_all[:, GDN_HEADS + hh:GDN_HEADS + hh + 1]
                decay, kb, eg = _head_elementwise(k, beta, gc, gc_rows[GDN_HEADS + hh:GDN_HEADS + hh + 1, :], causal)
                hs = slice(hh * HEAD_DIM, (hh + 1) * HEAD_DIM)
                cs = slice(hh * CHUNK, (hh + 1) * CHUNK)
                qd_ref[rs, hs] = (q * eg).astype(qd_ref.dtype)
                kd_ref[rs, hs] = (k * jnp.exp(gc[CHUNK - 1:CHUNK, :] - gc)).astype(kd_ref.dtype)
                heads.append((rs, hs, cs, q, k, v * beta, kb, kb * eg, decay))
        kks = [_mdot(kb, k, NT) for (_, _, _, _, k, _, kb, _, _) in heads]
        qks = [_mdot(q, k, NT) for (_, _, _, q, k, _, _, _, _) in heads]
        tinvs = _unit_lower_inverses([jnp.where(strict, kk * hd[8], 0.0) for kk, hd in zip(kks, heads)], eye)
        t_parts = [_split(t) for t in tinvs]
        us = [_dot3(tp, _split(hd[5]), NN) for tp, hd in zip(t_parts, heads)]
        ws = [_dot3(tp, _split(hd[7]), NN) for tp, hd in zip(t_parts, heads)]
        for hd, tinv, u, w, qk in zip(heads, tinvs, us, ws, qks):
            rs, hs, cs = hd[0], hd[1], hd[2]
            tinv_ref[rs, cs] = tinv
            u_ref[rs, hs] = u
            w_ref[rs, hs] = w.astype(w_ref.dtype)
            p_ref[rs, cs] = jnp.where(causal, qk * hd[8], 0.0).astype(p_ref.dtype)

    def spec(width):
        return pl.BlockSpec((rows, width), lambda n: (n, 0))

    hw, cw = GDN_WIDTH, GDN_HEADS * CHUNK
    return pl.pallas_call(
        body, name=name, grid=(T // rows,),
        in_specs=[spec(QKV_WIDTH), spec(128)],
        out_specs=[spec(cw), spec(hw), spec(hw), spec(hw), spec(hw), spec(cw), spec(128)],
        out_shape=[jax.ShapeDtypeStruct((T, cw), F32), jax.ShapeDtypeStruct((T, hw), F32),
                   jax.ShapeDtypeStruct((T, hw), MXU_DTYPE), jax.ShapeDtypeStruct((T, hw), MXU_DTYPE),
                   jax.ShapeDtypeStruct((T, hw), MXU_DTYPE), jax.ShapeDtypeStruct((T, cw), MXU_DTYPE),
                   jax.ShapeDtypeStruct((T, 128), F32)],
        compiler_params=_params("parallel"),
    )(qkv, bg)


def _gdn_scan_fwd(u, w, qd, kd, p, cd, name):
    T = u.shape[0]
    cb = _tile(T // CHUNK, SCAN_CHUNKS_PER_STEP)
    rows = cb * CHUNK

    def body(u_ref, w_ref, qd_ref, kd_ref, p_ref, cd_ref, o_ref, s_all_ref, vn_ref, s_ref):
        @pl.when(pl.program_id(0) == 0)
        def _():
            s_ref[...] = jnp.zeros_like(s_ref)

        hss = [slice(hh * HEAD_DIM, (hh + 1) * HEAD_DIM) for hh in range(GDN_HEADS)]
        css = [slice(hh * CHUNK, (hh + 1) * CHUNK) for hh in range(GDN_HEADS)]
        s_cur = [s_ref[hh] for hh in range(GDN_HEADS)]
        for ci in range(cb):
            rs = slice(ci * CHUNK, (ci + 1) * CHUNK)
            for hh in range(GDN_HEADS):
                s_all_ref[ci * GDN_WIDTH + hh * HEAD_DIM:ci * GDN_WIDTH + (hh + 1) * HEAD_DIM, :] = s_cur[hh]
            s_ms = [s.astype(MXU_DTYPE) for s in s_cur]
            w_s = [_dot(w_ref[rs, hs], s_m, NN) for hs, s_m in zip(hss, s_ms)]
            q_s = [_dot(qd_ref[rs, hs], s_m, NN) for hs, s_m in zip(hss, s_ms)]
            v_ms = [(u_ref[rs, hs] - ws_).astype(MXU_DTYPE) for hs, ws_ in zip(hss, w_s)]
            k_v = [_dot(kd_ref[rs, hs], v_m, TN) for hs, v_m in zip(hss, v_ms)]
            p_v = [_dot(p_ref[rs, cs], v_m, NN) for cs, v_m in zip(css, v_ms)]
            for hh in range(GDN_HEADS):
                vn_ref[rs, hss[hh]] = v_ms[hh]
                o_ref[rs, hss[hh]] = q_s[hh] + p_v[hh]
                c_dec = cd_ref[ci * CHUNK:ci * CHUNK + 1, GDN_HEADS + hh:GDN_HEADS + hh + 1]
                s_cur[hh] = s_cur[hh] * c_dec + k_v[hh]
        for hh in range(GDN_HEADS):
            s_ref[hh] = s_cur[hh]

    def spec(width):
        return pl.BlockSpec((rows, width), lambda n: (n, 0))

    hw, cw = GDN_WIDTH, GDN_HEADS * CHUNK
    return pl.pallas_call(
        body, name=name, grid=(T // rows,),
        in_specs=[spec(hw), spec(hw), spec(hw), spec(hw), spec(cw), spec(128)],
        out_specs=[spec(hw), pl.BlockSpec((cb * GDN_WIDTH, HEAD_DIM), lambda n: (n, 0)), spec(hw)],
        out_shape=[jax.ShapeDtypeStruct((T, hw), F32),
                   jax.ShapeDtypeStruct((T // CHUNK * GDN_WIDTH, HEAD_DIM), F32),
                   jax.ShapeDtypeStruct((T, hw), MXU_DTYPE)],
        scratch_shapes=[pltpu.VMEM((GDN_HEADS, HEAD_DIM, HEAD_DIM), F32)],
        compiler_params=_params("arbitrary"),
    )(u, w, qd, kd, p, cd)


def _gdn_scan_bwd(do, w, qd, kd, p, cd, s_all, vn, name):
    T = do.shape[0]
    cb = _tile(T // CHUNK, SCAN_CHUNKS_PER_STEP)
    rows = cb * CHUNK
    n_steps = T // rows

    def body(do_ref, w_ref, qd_ref, kd_ref, p_ref, cd_ref, s_all_ref, vn_ref,
             dvn_ref, dw_ref, dqd_ref, dkd_ref, dp_ref, dcd_ref, ds_ref):
        @pl.when(pl.program_id(0) == 0)
        def _():
            ds_ref[...] = jnp.zeros_like(ds_ref)

        causal, _, _ = _chunk_masks()
        lane = lax.broadcasted_iota(jnp.int32, (CHUNK, 128), 1)
        heads = range(GDN_HEADS)
        hss = [slice(hh * HEAD_DIM, (hh + 1) * HEAD_DIM) for hh in heads]
        css = [slice(hh * CHUNK, (hh + 1) * CHUNK) for hh in heads]
        ds_cur = [ds_ref[hh] for hh in heads]
        for ci in reversed(range(cb)):
            rs = slice(ci * CHUNK, (ci + 1) * CHUNK)
            ds_ms = [d.astype(MXU_DTYPE) for d in ds_cur]
            s_olds = [s_all_ref[ci * GDN_WIDTH + hh * HEAD_DIM:ci * GDN_WIDTH + (hh + 1) * HEAD_DIM, :] for hh in heads]
            s_ms = [s.astype(MXU_DTYPE) for s in s_olds]
            do_ms = [do_ref[rs, hs].astype(MXU_DTYPE) for hs in hss]
            p_do = [_dot(p_ref[rs, cs], do_m, TN) for cs, do_m in zip(css, do_ms)]
            k_ds = [_dot(kd_ref[rs, hs], ds_m, NN) for hs, ds_m in zip(hss, ds_ms)]
            q_do = [_dot(qd_ref[rs, hs], do_m, TN) for hs, do_m in zip(hss, do_ms)]
            dqds = [_dot(do_m, s_m, NT) for do_m, s_m in zip(do_ms, s_ms)]
            dkds = [_dot(vn_ref[rs, hs], ds_m, NT) for hs, ds_m in zip(hss, ds_ms)]
            dps = [_dot(do_m, vn_ref[rs, hs], NT) for hs, do_m in zip(hss, do_ms)]
            dv_news = [a + b for a, b in zip(p_do, k_ds)]
            dvn_ms = [d.astype(MXU_DTYPE) for d in dv_news]
            w_dv = [_dot(w_ref[rs, hs], dvn_m, TN) for hs, dvn_m in zip(hss, dvn_ms)]
            dws = [_dot(dvn_m, s_m, NT) for dvn_m, s_m in zip(dvn_ms, s_ms)]
            dcd_tile = jnp.zeros((CHUNK, 128), F32)
            for hh in heads:
                dvn_ref[rs, hss[hh]] = dv_news[hh]
                dw_ref[rs, hss[hh]] = -dws[hh]
                dqd_ref[rs, hss[hh]] = dqds[hh]
                dkd_ref[rs, hss[hh]] = dkds[hh]
                dp_ref[rs, css[hh]] = jnp.where(causal, dps[hh], 0.0)
                dcd = jnp.sum(jnp.sum(s_olds[hh] * ds_cur[hh], axis=1, keepdims=True), axis=0, keepdims=True)
                dcd_tile = jnp.where(lane == GDN_HEADS + hh, dcd, dcd_tile)
                c_dec = cd_ref[ci * CHUNK:ci * CHUNK + 1, GDN_HEADS + hh:GDN_HEADS + hh + 1]
                ds_cur[hh] = c_dec * ds_cur[hh] + q_do[hh] - w_dv[hh]
            dcd_ref[rs, :] = dcd_tile
        for hh in heads:
            ds_ref[hh] = ds_cur[hh]

    def spec(width):
        return pl.BlockSpec((rows, width), lambda n: (n_steps - 1 - n, 0))

    hw, cw = GDN_WIDTH, GDN_HEADS * CHUNK
    return pl.pallas_call(
        body, name=name, grid=(n_steps,),
        in_specs=[spec(hw), spec(hw), spec(hw), spec(hw), spec(cw), spec(128),
                  pl.BlockSpec((cb * GDN_WIDTH, HEAD_DIM), lambda n: (n_steps - 1 - n, 0)), spec(hw)],
        out_specs=[spec(hw), spec(hw), spec(hw), spec(hw), spec(cw), spec(128)],
        out_shape=[jax.ShapeDtypeStruct((T, hw), F32)] * 4
        + [jax.ShapeDtypeStruct((T, cw), F32), jax.ShapeDtypeStruct((T, 128), F32)],
        scratch_shapes=[pltpu.VMEM((GDN_HEADS, HEAD_DIM, HEAD_DIM), F32)],
        compiler_params=_params("arbitrary"),
    )(do, w, qd, kd, p, cd, s_all, vn)


def _gdn_chunk_bwd(qkv, bg, tinv_all, u, w, dvn, dw, dqd, dkd, dp, dcd, name):
    T = qkv.shape[0]
    cb = _tile(T // CHUNK, CHUNKS_PER_STEP)
    rows = cb * CHUNK

    def body(qkv_ref, bg_ref, tinv_ref, u_ref, w_ref, dvn_ref, dw_ref, dqd_ref, dkd_ref, dp_ref, dcd_ref,
             dqkv_ref, dbg_ref):
        masks = _chunk_masks()
        causal, strict, _ = masks
        lane = lax.broadcasted_iota(jnp.int32, (CHUNK, 128), 1)
        row = lax.broadcasted_iota(jnp.int32, (CHUNK, 128), 0)
        heads = []
        for ci in range(cb):
            rs = slice(ci * CHUNK, (ci + 1) * CHUNK)
            bgv = bg_ref[rs, :]
            gc_all = _chunk_cumsum(bgv)
            gc_rows = _row_form(gc_all)
            for hh in range(GDN_HEADS):
                qs, ks, vs = _head_slices(hh)
                q, k = qkv_ref[rs, qs], qkv_ref[rs, ks]
                beta = bgv[:, hh:hh + 1]
                gc = gc_all[:, GDN_HEADS + hh:GDN_HEADS + hh + 1]
                decay, kb, eg = _head_elementwise(k, beta, gc, gc_rows[GDN_HEADS + hh:GDN_HEADS + hh + 1, :], causal)
                heads.append(dict(ci=ci, hh=hh, rs=rs, hs=slice(hh * HEAD_DIM, (hh + 1) * HEAD_DIM),
                                  cs=slice(hh * CHUNK, (hh + 1) * CHUNK), q=q, k=k, beta=beta, gc=gc,
                                  decay=decay, kb=kb, eg=eg))
        for hd in heads:
            hd["t"] = _split(tinv_ref[hd["rs"], hd["cs"]])
        for hd in heads:
            hd["kk"] = _mdot(hd["kb"], hd["k"], NT)
            hd["qk"] = _mdot(hd["q"], hd["k"], NT)
        for hd in heads:
            hd["dvb"] = _dot3(hd["t"], _split(dvn_ref[hd["rs"], hd["hs"]]), TN)
            hd["dkbeg"] = _dot3(hd["t"], _split(dw_ref[hd["rs"], hd["hs"]]), TN)
        for hd in heads:
            rs, hs = hd["rs"], hd["hs"]
            da = -(_mdot(hd["dvb"], u_ref[rs, hs], NT) + _mdot(hd["dkbeg"], w_ref[rs, hs], NT))
            dm = jnp.where(strict, da, 0.0)
            dp_ = dp_ref[rs, hd["cs"]]
            hd["dkk"] = dm * hd["decay"]
            hd["dqk"] = dp_ * hd["decay"]
            hd["e"] = (hd["dkk"] * hd["kk"] + hd["dqk"] * hd["qk"])
        for hd in heads:
            hd["dkb"] = _mdot(hd["dkk"], hd["k"], NN)
            hd["dk"] = _mdot(hd["dkk"], hd["kb"], TN) + _mdot(hd["dqk"], hd["q"], TN)
            hd["dq"] = _mdot(hd["dqk"], hd["k"], NN)
            onehot = (lane == GDN_HEADS + hd["hh"]).astype(jnp.bfloat16)
            e_hi, e_lo = _split(hd["e"])
            hd["col_sums"] = _dot(e_lo, onehot, TN) + _dot(e_hi, onehot, TN)
        tiles = {}
        for hd in heads:
            ci, hh, rs, hs = hd["ci"], hd["hh"], hd["rs"], hd["hs"]
            qs, ks, vs = _head_slices(hh)
            q, k, beta, gc, eg, kb = hd["q"], hd["k"], hd["beta"], hd["gc"], hd["eg"], hd["kb"]
            v = qkv_ref[rs, vs]
            dqd_, dkd_ = dqd_ref[rs, hs], dkd_ref[rs, hs]
            gl = gc[CHUNK - 1:CHUNK, :]
            ek = jnp.exp(gl - gc)
            dkb = hd["dkb"] + hd["dkbeg"] * eg
            deg = jnp.sum(dqd_ * q, axis=1, keepdims=True) + jnp.sum(hd["dkbeg"] * kb, axis=1, keepdims=True)
            dek = jnp.sum(dkd_ * k, axis=1, keepdims=True)
            dcd_ = dcd_ref[ci * CHUNK:ci * CHUNK + 1, GDN_HEADS + hh:GDN_HEADS + hh + 1]
            dgl = jnp.sum(dek * ek, axis=0, keepdims=True) + dcd_ * jnp.exp(gl)
            dgc = jnp.sum(hd["e"], axis=1, keepdims=True) + deg * eg - dek * ek
            dbeta_tile, dgc_tile = tiles.get(ci, (jnp.zeros((CHUNK, 128), F32), jnp.zeros((CHUNK, 128), F32)))
            dgc_tile += jnp.where(lane == GDN_HEADS + hh, dgc, 0.0) - hd["col_sums"]
            dgc_tile += jnp.where((lane == GDN_HEADS + hh) & (row == CHUNK - 1), dgl, 0.0)
            dbeta = jnp.sum(dkb * k, axis=1, keepdims=True) + jnp.sum(hd["dvb"] * v, axis=1, keepdims=True)
            dbeta_tile += jnp.where(lane == hh, dbeta, 0.0)
            tiles[ci] = (dbeta_tile, dgc_tile)
            dqkv_ref[rs, qs] = hd["dq"] + dqd_ * eg
            dqkv_ref[rs, ks] = hd["dk"] + dkd_ * ek + dkb * beta
            dqkv_ref[rs, vs] = hd["dvb"] * beta
        for ci in range(cb):
            dbeta_tile, dgc_tile = tiles[ci]
            dbg_ref[ci * CHUNK:(ci + 1) * CHUNK, :] = dbeta_tile + _chunk_cumsum(dgc_tile, reverse=True)

    def spec(width):
        return pl.BlockSpec((rows, width), lambda n: (n, 0))

    hw, cw = GDN_WIDTH, GDN_HEADS * CHUNK
    return pl.pallas_call(
        body, name=name, grid=(T // rows,),
        in_specs=[spec(QKV_WIDTH), spec(128), spec(cw), spec(hw), spec(hw), spec(hw), spec(hw), spec(hw),
                  spec(hw), spec(cw), spec(128)],
        out_specs=[spec(QKV_WIDTH), spec(128)],
        out_shape=[jax.ShapeDtypeStruct((T, QKV_WIDTH), F32), jax.ShapeDtypeStruct((T, 128), F32)],
        compiler_params=_params("parallel"),
    )(qkv, bg, tinv_all, u, w, dvn, dw, dqd, dkd, dp, dcd)


def _pool_counts(i, tm, rows, offset):
    t = i * tm - offset + lax.broadcasted_iota(jnp.int32, (rows, 1), 0)
    return [jnp.minimum(t + 1, w).astype(F32) for w in POOL_WINDOWS]


def _window_sums(window, forward):
    sums, s, step = [], window, 1
    for _ in POOL_WINDOWS:
        s = s + _shift_rows(s, -step if forward else step)
        sums.append(s)
        step *= 2
    return sums


def _pooled(window, counts):
    sums = _window_sums(window, forward=False)
    out = []
    for gi in range(POOL_GROUPS):
        sl = slice(gi * 128, (gi + 1) * 128)
        out.append(sums[gi][HALO:, sl] / counts[gi] - window[HALO:, sl])
    return out


def _mix_post(o, proj, gdn_norm, pool_w, pool_scale, name):
    T = o.shape[0]
    tm = _tile(T, 256)
    hb = tm // HALO

    def body(o_ref, z_ref, p_ref, ph_ref, gn_ref, pw_ref, ps_ref, out_ref):
        i = pl.program_id(0)
        for hh in range(GDN_HEADS):
            sl = slice(hh * HEAD_DIM, (hh + 1) * HEAD_DIM)
            oh, zh = o_ref[:, sl], z_ref[:, sl]
            ro = lax.rsqrt(jnp.mean(oh * oh, axis=-1, keepdims=True) + EPS)
            out_ref[:, sl] = (((oh * ro) * gn_ref[...]) * (zh * _sigmoid(zh))).astype(out_ref.dtype)
        halo = jnp.where(i == 0, 0.0, ph_ref[...])
        window = jnp.concatenate([halo, p_ref[...]], axis=0)
        pooled = _pooled(window, _pool_counts(i, tm, tm, 0))
        for gi in range(POOL_GROUPS):
            pm = _mdot(pooled[gi], pw_ref[gi], NN)
            out_ref[:, GDN_WIDTH + gi * 128:GDN_WIDTH + (gi + 1) * 128] = (
                pm * ps_ref[:, gi * 128:(gi + 1) * 128]).astype(out_ref.dtype)

    return pl.pallas_call(
        body, name=name, grid=(T // tm,),
        in_specs=[pl.BlockSpec((tm, GDN_WIDTH), lambda i: (i, 0)),
                  pl.BlockSpec((tm, GDN_WIDTH), lambda i: (i, COL_Z // GDN_WIDTH)),
                  pl.BlockSpec((tm, POOL_WIDTH), lambda i: (i, COL_P // POOL_WIDTH)),
                  pl.BlockSpec((HALO, POOL_WIDTH), lambda i: (jnp.maximum(i * hb - 1, 0), COL_P // POOL_WIDTH)),
                  pl.BlockSpec((1, HEAD_DIM), lambda i: (0, 0)),
                  pl.BlockSpec((POOL_GROUPS, 128, 128), lambda i: (0, 0, 0)),
                  pl.BlockSpec((1, POOL_WIDTH), lambda i: (0, 0))],
        out_specs=pl.BlockSpec((tm, GDN_WIDTH + POOL_WIDTH), lambda i: (i, 0)),
        out_shape=jax.ShapeDtypeStruct((T, GDN_WIDTH + POOL_WIDTH), MXU_DTYPE),
        compiler_params=_params("parallel"),
    )(o, proj, proj, proj, gdn_norm, pool_w, pool_scale)


def _mix_post_bwd(dmix, o, proj, gdn_norm, pool_w, pool_scale, name):
    T = o.shape[0]
    tm = _tile(T, 256)
    hb = tm // HALO
    n_tiles = T // tm

    def body(dg_ref, dpo_ref, dpo_next_ref, o_ref, z_ref, p_ref, ph_ref, gn_ref, pw_ref, ps_ref,
             do_ref, dzp_ref, dgn_ref, dpw_ref, dps_ref):
        i = pl.program_id(0)

        @pl.when(i == 0)
        def _():
            dgn_ref[...] = jnp.zeros_like(dgn_ref)
            dpw_ref[...] = jnp.zeros_like(dpw_ref)
            dps_ref[...] = jnp.zeros_like(dps_ref)

        gn = gn_ref[...]
        dgn = jnp.zeros((1, HEAD_DIM), F32)
        for hh in range(GDN_HEADS):
            sl = slice(hh * HEAD_DIM, (hh + 1) * HEAD_DIM)
            oh, zh, dy = o_ref[:, sl], z_ref[:, sl], dg_ref[:, sl]
            ro = lax.rsqrt(jnp.mean(oh * oh, axis=-1, keepdims=True) + EPS)
            on = oh * ro
            sig = _sigmoid(zh)
            sz = zh * sig
            dzp_ref[:, sl] = (dy * (on * gn) * (sig * (1.0 + zh * (1.0 - sig)))).astype(dzp_ref.dtype)
            dgn += jnp.sum(dy * on * sz, axis=0, keepdims=True)
            don = dy * gn * sz
            do_ref[:, sl] = ro * (don - on * jnp.mean(don * on, axis=-1, keepdims=True))
        dgn_ref[...] += dgn

        halo = jnp.where(i == 0, 0.0, ph_ref[...])
        window = jnp.concatenate([halo, p_ref[...]], axis=0)
        counts = _pool_counts(i, tm, tm + HALO, 0)
        pooled = _pooled(window, [cn[:tm] for cn in counts])
        nxt = jnp.where(i == n_tiles - 1, 0.0, dpo_next_ref[...])
        dpo_w = jnp.concatenate([dpo_ref[...], nxt], axis=0)
        ps = ps_ref[...]
        dps = []
        scaled = []
        for gi in range(POOL_GROUPS):
            sl = slice(gi * 128, (gi + 1) * 128)
            dpm = dpo_w[:, sl] * ps[:, sl]
            pm = _mdot(pooled[gi], pw_ref[gi], NN)
            dps.append(jnp.sum(dpo_w[:tm, sl] * pm, axis=0, keepdims=True))
            dpw_ref[gi] += _mdot(pooled[gi], dpm[:tm], TN)
            dpooled = _mdot(dpm, pw_ref[gi], NT)
            scaled.append((dpooled, dpooled / counts[gi]))
        dps_ref[...] += jnp.concatenate(dps, axis=1)
        lead = _window_sums(jnp.concatenate([sc for _, sc in scaled], axis=1), forward=True)
        for gi in range(POOL_GROUPS):
            sl = slice(gi * 128, (gi + 1) * 128)
            dzp_ref[:, GDN_WIDTH + gi * 128:GDN_WIDTH + (gi + 1) * 128] = (
                lead[gi][:tm, sl] - scaled[gi][0][:tm]).astype(dzp_ref.dtype)

    last_halo = T // HALO - 1
    return pl.pallas_call(
        body, name=name, grid=(n_tiles,),
        in_specs=[pl.BlockSpec((tm, GDN_WIDTH), lambda i: (i, 0)),
                  pl.BlockSpec((tm, POOL_WIDTH), lambda i: (i, 1)),
                  pl.BlockSpec((HALO, POOL_WIDTH), lambda i: (jnp.minimum((i + 1) * hb, last_halo), 1)),
                  pl.BlockSpec((tm, GDN_WIDTH), lambda i: (i, 0)),
                  pl.BlockSpec((tm, GDN_WIDTH), lambda i: (i, COL_Z // GDN_WIDTH)),
                  pl.BlockSpec((tm, POOL_WIDTH), lambda i: (i, COL_P // POOL_WIDTH)),
                  pl.BlockSpec((HALO, POOL_WIDTH), lambda i: (jnp.maximum(i * hb - 1, 0), COL_P // POOL_WIDTH)),
                  pl.BlockSpec((1, HEAD_DIM), lambda i: (0, 0)),
                  pl.BlockSpec((POOL_GROUPS, 128, 128), lambda i: (0, 0, 0)),
                  pl.BlockSpec((1, POOL_WIDTH), lambda i: (0, 0))],
        out_specs=[pl.BlockSpec((tm, GDN_WIDTH), lambda i: (i, 0)),
                   pl.BlockSpec((tm, GDN_WIDTH + POOL_WIDTH), lambda i: (i, 0)),
                   pl.BlockSpec((1, HEAD_DIM), lambda i: (0, 0)),
                   pl.BlockSpec((POOL_GROUPS, 128, 128), lambda i: (0, 0, 0)),
                   pl.BlockSpec((1, POOL_WIDTH), lambda i: (0, 0))],
        out_shape=[jax.ShapeDtypeStruct((T, GDN_WIDTH), F32),
                   jax.ShapeDtypeStruct((T, GDN_WIDTH + POOL_WIDTH), MXU_DTYPE),
                   jax.ShapeDtypeStruct((1, HEAD_DIM), F32),
                   jax.ShapeDtypeStruct((POOL_GROUPS, 128, 128), F32),
                   jax.ShapeDtypeStruct((1, POOL_WIDTH), F32)],
        compiler_params=_params("arbitrary"),
    )(dmix, dmix, dmix, o, proj, proj, proj, gdn_norm, pool_w, pool_scale)


def _gdn_prep_bwd(proj, conv_w, a_log_l, dt_bias_l, dqkv, dbg, dzp, name):
    T = proj.shape[0]
    tm = _tile(T, 256)
    hb = tm // 8
    n_tiles = T // tm
    last_halo = T // 8 - 1

    def body(cur_ref, before_ref, after_ref, ba_ref, w_ref, al_ref, dtb_ref, dq_ref, dq_after_ref, dbg_ref,
             dzp_ref, dproj_ref, dw_ref, dal_ref, ddtb_ref):
        i = pl.program_id(0)

        @pl.when(i == 0)
        def _():
            dw_ref[...] = jnp.zeros_like(dw_ref)
            dal_ref[...] = jnp.zeros_like(dal_ref)
            ddtb_ref[...] = jnp.zeros_like(ddtb_ref)

        last = i == n_tiles - 1
        w = w_ref[...]
        before = jnp.where(i == 0, 0.0, before_ref[...])
        after = jnp.where(last, 0.0, after_ref[...])
        window = jnp.concatenate([before, cur_ref[...], after], axis=0)
        y = _conv_act(window, w)
        sig = _sigmoid(y)
        act = y * sig
        dq_w = jnp.concatenate([jnp.zeros((8, QKV_WIDTH), F32), dq_ref[...],
                                jnp.where(last, 0.0, dq_after_ref[...])], axis=0)
        dact = []
        for hh in range(3 * GDN_HEADS):
            sl = slice(hh * HEAD_DIM, (hh + 1) * HEAD_DIM)
            blk, dblk = act[:, sl], dq_w[:, sl]
            if hh < 2 * GDN_HEADS:
                rn = lax.rsqrt(jnp.sum(blk * blk, axis=-1, keepdims=True) + EPS)
                unit = blk * rn
                if hh < GDN_HEADS:
                    dblk = dblk * (HEAD_DIM ** -0.5)
                dblk = rn * (dblk - unit * jnp.sum(dblk * unit, axis=-1, keepdims=True))
            dact.append(dblk)
        dy = jnp.concatenate(dact, axis=1) * (sig * (1.0 + y * (1.0 - sig)))
        dx = dy * w[CONV_K - 1:CONV_K, :]
        dws = [None] * CONV_K
        dws[CONV_K - 1] = jnp.sum(dy[8:8 + tm] * window[8:8 + tm], axis=0, keepdims=True)
        for j in range(CONV_K - 1):
            s = CONV_K - 1 - j
            dx += _shift_rows(dy, -s) * w[j:j + 1, :]
            dws[j] = jnp.sum(dy[8:8 + tm] * _shift_rows(window, s)[8:8 + tm], axis=0, keepdims=True)
        dw_ref[...] += jnp.concatenate(dws, axis=0)
        dproj_ref[:, :QKV_WIDTH] = dx[8:8 + tm].astype(dproj_ref.dtype)
        dproj_ref[:, COL_Z:COL_BA] = dzp_ref[...]

        ba = ba_ref[...]
        dbg_ = dbg_ref[...]
        lane = lax.broadcasted_iota(jnp.int32, ba.shape, 1)
        beta = _sigmoid(ba)
        pre = ba + dtb_ref[...]
        neg_a = -jnp.exp(al_ref[...])
        g = neg_a * _softplus(pre)
        is_g = (lane >= GDN_HEADS) & (lane < 2 * GDN_HEADS)
        da_raw = jnp.where(is_g, dbg_ * neg_a * _sigmoid(pre), 0.0)
        dba = jnp.where(lane < GDN_HEADS, dbg_ * beta * (1.0 - beta), da_raw)
        dproj_ref[:, COL_BA:] = dba.astype(dproj_ref.dtype)
        dal_ref[...] += jnp.sum(jnp.where(is_g, dbg_ * g, 0.0), axis=0, keepdims=True)
        ddtb_ref[...] += jnp.sum(da_raw, axis=0, keepdims=True)

    lane_vec = pl.BlockSpec((1, 128), lambda i: (0, 0))
    return pl.pallas_call(
        body, name=name, grid=(n_tiles,),
        in_specs=[pl.BlockSpec((tm, QKV_WIDTH), lambda i: (i, 0)),
                  pl.BlockSpec((8, QKV_WIDTH), lambda i: (jnp.maximum(i * hb - 1, 0), 0)),
                  pl.BlockSpec((8, QKV_WIDTH), lambda i: (jnp.minimum((i + 1) * hb, last_halo), 0)),
                  pl.BlockSpec((tm, 128), lambda i: (i, COL_BA // 128)),
                  pl.BlockSpec((CONV_K, QKV_WIDTH), lambda i: (0, 0)), lane_vec, lane_vec,
                  pl.BlockSpec((tm, QKV_WIDTH), lambda i: (i, 0)),
                  pl.BlockSpec((8, QKV_WIDTH), lambda i: (jnp.minimum((i + 1) * hb, last_halo), 0)),
                  pl.BlockSpec((tm, 128), lambda i: (i, 0)),
                  pl.BlockSpec((tm, GDN_WIDTH + POOL_WIDTH), lambda i: (i, 0))],
        out_specs=[pl.BlockSpec((tm, D_IN_PAD), lambda i: (i, 0)),
                   pl.BlockSpec((CONV_K, QKV_WIDTH), lambda i: (0, 0)), lane_vec, lane_vec],
        out_shape=[jax.ShapeDtypeStruct((T, D_IN_PAD), MXU_DTYPE),
                   jax.ShapeDtypeStruct((CONV_K, QKV_WIDTH), F32),
                   jax.ShapeDtypeStruct((1, 128), F32), jax.ShapeDtypeStruct((1, 128), F32)],
        compiler_params=_params("arbitrary"),
    )(proj, proj, proj, proj, conv_w, a_log_l, dt_bias_l, dqkv, dqkv, dbg, dzp)


def _mod_part(c_all, w_ada, b_part, name):
    def body(c_ref, w_ref, b_ref, out_ref):
        cc = c_ref[...]
        out_ref[...] = _mdot(cc * _sigmoid(cc), w_ref[...], NN) + b_ref[...]

    return pl.pallas_call(
        body, name=name, out_shape=jax.ShapeDtypeStruct((c_all.shape[0], w_ada.shape[1]), F32),
        compiler_params=_params(),
    )(c_all, w_ada, b_part)


def _w_ada_grad(c_all, dmod_part, name):
    def body(c_ref, d_ref, out_ref):
        cc = c_ref[...]
        out_ref[...] = _mdot(cc * _sigmoid(cc), d_ref[...], TN)

    return pl.pallas_call(
        body, name=name, out_shape=jax.ShapeDtypeStruct((c_all.shape[1], dmod_part.shape[1]), F32),
        compiler_params=_params(),
    )(c_all, dmod_part)


def _sum_parts(parts, name):
    _, R, C = parts.shape
    tr = max([t for t in range(16, min(R, 512) + 1, 16) if R % t == 0], default=R)

    def body(p_ref, out_ref):
        acc = p_ref[0].astype(F32)
        for s in range(1, N_DEV):
            acc += p_ref[s].astype(F32)
        out_ref[...] = acc

    return pl.pallas_call(
        body, name=name, grid=(R // tr,),
        in_specs=[pl.BlockSpec((N_DEV, tr, C), lambda i: (0, i, 0))],
        out_specs=pl.BlockSpec((tr, C), lambda i: (i, 0)),
        out_shape=jax.ShapeDtypeStruct((R, C), F32),
        compiler_params=_params("parallel"),
    )(parts)


def _adamw(w, g, m, v, name):
    R, C = w.shape
    tr = max([t for t in range(8, min(R, 512) + 1, 8) if R % t == 0], default=R)

    def body(w_ref, g_ref, m_ref, v_ref, d_ref, mo_ref, vo_ref):
        gg = g_ref[...]
        mm = ADAM_B1 * m_ref[...] + (1.0 - ADAM_B1) * gg
        vv = ADAM_B2 * v_ref[...] + (1.0 - ADAM_B2) * (gg * gg)
        m_hat = mm / (1.0 - ADAM_B1 ** ADAM_STEP)
        v_hat = vv / (1.0 - ADAM_B2 ** ADAM_STEP)
        d_ref[...] = -ADAM_LR * (m_hat / (jnp.sqrt(v_hat) + ADAM_EPS) + ADAM_WD * w_ref[...])
        mo_ref[...] = mm
        vo_ref[...] = vv

    spec = pl.BlockSpec((tr, C), lambda i: (i, 0))
    return pl.pallas_call(
        body, name=name, grid=(R // tr,),
        in_specs=[spec] * 4, out_specs=[spec] * 3,
        out_shape=[jax.ShapeDtypeStruct((R, C), F32)] * 3,
        compiler_params=_params("parallel"),
    )(w, g, m, v)


def _weight_grad(a, b, name, dep=None):
    return _matmul([(a, b)], TN, WIRE_DTYPE, name, tm=1408, tn=1024, tk=1024, dep=dep)


def _rows_of(flat, lanes=1024):
    flat = flat.reshape(-1)
    n = -(-flat.shape[0] // lanes) * lanes
    return jnp.pad(flat, (0, n - flat.shape[0])).reshape(n // lanes, lanes)


def _pad_rows(a, rows):
    return jnp.pad(a, ((0, rows - a.shape[0]), (0, 0)))


def kernel(x, c, w_ada, b_ada, norm_ffn1, ffn1_gate, ffn1_up, ffn1_down, norm_mix, w_in, conv_w, a_log, dt_bias, gdn_norm, pool_w, pool_scale, w_out, norm_ffn2, ffn2_gate, ffn2_up, ffn2_down, final_norm, loss_target, m_w_ada, m_b_ada, m_norm_ffn1, m_ffn1_gate, m_ffn1_up, m_ffn1_down, m_norm_mix, m_w_in, m_conv_w, m_a_log, m_dt_bias, m_gdn_norm, m_pool_w, m_pool_scale, m_w_out, m_norm_ffn2, m_ffn2_gate, m_ffn2_up, m_ffn2_down, m_final_norm, v_w_ada, v_b_ada, v_norm_ffn1, v_ffn1_gate, v_ffn1_up, v_ffn1_down, v_norm_mix, v_w_in, v_conv_w, v_a_log, v_dt_bias, v_gdn_norm, v_pool_w, v_pool_scale, v_w_out, v_norm_ffn2, v_ffn2_gate, v_ffn2_up, v_ffn2_down, v_final_norm):
    T, D = x.shape[1], x.shape[2]
    Fs = ffn1_gate.shape[2]
    Ws = w_in.shape[2]
    Ws_pad = -(-Ws // 16) * 16
    Os = w_out.shape[1]
    Ms = w_ada.shape[2]
    Cs = conv_w.shape[2]
    me = 4 * lax.axis_index("x") + 2 * lax.axis_index("y") + lax.axis_index("c")
    x0, target = x[0], loss_target[0]

    def wire(a):
        return a.astype(WIRE_DTYPE)

    def token(started):
        return started[4][:1, :1]

    def with_own(landed, own):
        return lax.dynamic_update_slice(landed, own[None], (me, 0, 0))

    def full(landed, off, size, keep=None):
        blk = landed[:, off:off + (size if keep is None else keep), :]
        return blk.reshape(-1, D).astype(MXU_DTYPE)

    no_dep = jnp.zeros((8, 128), F32)
    small = jnp.concatenate([_pad_rows(c, 8), _pad_rows(jnp.pad(conv_w[0], ((0, 0), (0, D - Cs))), 8)], axis=0)
    got = _all_gather(small, "gather_small")
    c_all = got[:, 0, :]
    conv_full = jnp.transpose(got[:, 8:8 + CONV_K, :Cs], (1, 0, 2)).reshape(CONV_K, QKV_WIDTH)
    b_part = lax.dynamic_slice(b_ada, (0, me * Ms), (1, Ms))
    mod_parts = _all_gather(_mod_part(c_all, w_ada[0], b_part, "mod_part"), "gather_mod")
    mod_all = jnp.transpose(mod_parts, (1, 0, 2)).reshape(N_DEV, N_MOD * D)
    mod = lax.dynamic_slice(mod_all, (me, 0), (1, N_MOD * D)).reshape(N_MOD, 1, D)
    sh1, sc1, gt1, sh2, sc2, gt2, sh3, sc3, gt3 = [mod[i] for i in range(N_MOD)]

    w1 = jnp.concatenate([wire(ffn1_gate[0].T), wire(ffn1_up[0].T), wire(ffn1_down[0])], axis=0)
    w23 = jnp.concatenate([wire(_pad_rows(w_in[0].T, Ws_pad)), wire(w_out[0]),
                           wire(ffn2_gate[0].T), wire(ffn2_up[0].T), wire(ffn2_down[0])], axis=0)
    off23 = [0, Ws_pad, Ws_pad + Os, Ws_pad + Os + Fs, Ws_pad + Os + 2 * Fs]
    w1_all = _all_gather(w1, "gather_w1", dep=mod_all)
    wg1_t, wu1_t, wd1 = full(w1_all, 0, Fs), full(w1_all, Fs, Fs), full(w1_all, 2 * Fs, Fs)
    w23_sent = _exchange_start(w23, True, w1_all, "w23_start")

    lane_pad = lambda a: jnp.pad(a, ((0, 0), (GDN_HEADS, 128 - 2 * GDN_HEADS)))
    a_log_l, dt_bias_l = lane_pad(a_log), lane_pad(dt_bias)
    pool_w_m = pool_w[0].astype(MXU_DTYPE)

    g1, u1, a1, h1, y1, x1, h2 = _swiglu_fwd(
        x0, wg1_t, wu1_t, wd1, "ffn1_fwd", norm_in=(norm_ffn1, sc1 + token(w23_sent), sh1),
        resid_out=(x0, gt1, 0.5, norm_mix, sc2, sh2))
    w23_all = with_own(_exchange_wait(w23_sent, h2, True, "w23_wait"), w23)
    w_in_t = full(w23_all, off23[0], Ws_pad, Ws)
    wo = full(w23_all, off23[1], Os)
    wg2_t, wu2_t, wd2 = full(w23_all, off23[2], Fs), full(w23_all, off23[3], Fs), full(w23_all, off23[4], Fs)
    w_in_re = jnp.concatenate([w_in_t[:COL_Z + GDN_WIDTH], w_in_t[D_IN - POOL_WIDTH:],
                               w_in_t[4 * GDN_WIDTH:4 * GDN_WIDTH + 2 * GDN_HEADS],
                               jnp.zeros((128 - 2 * GDN_HEADS, D), MXU_DTYPE)], axis=0)
    proj = _matmul([(h2, w_in_re)], NT, F32, "proj_in", tm=512, tn=D_IN_PAD, tk=D)
    qkv, bg = _gdn_prep(proj, conv_full, a_log_l, dt_bias_l, "gdn_prep")
    tinv, u_c, w_c, qd_c, kd_c, p_c, cd_c = _gdn_chunk_fwd(qkv, bg, "gdn_chunk_fwd")
    o, s_all, vn_c = _gdn_scan_fwd(u_c, w_c, qd_c, kd_c, p_c, cd_c, "gdn_scan_fwd")
    mix_in = _mix_post(o, proj, gdn_norm, pool_w_m, pool_scale, "mix_post")
    mixed, x2, h3 = _matmul_resid_norm_mod(mix_in, wo, x1, gt2, 1.0, norm_ffn2, sc3, sh3, "mix_out")
    g3, u3, a3, loss_row, d3, d_final, dy3, dgt3 = _swiglu_fwd(
        h3, wg2_t, wu2_t, wd2, "ffn2_fwd_loss", loss_out=(x2, gt3, final_norm.reshape(1, D), target))

    dg3, du3, d2, d_n3, dsc3, dsh3, dmixed, dgt2 = _swiglu_bwd(
        dy3, wd2, g3, u3, wg2_t, wu2_t, "ffn2_bwd_norm3_bwd", (x2, norm_ffn2, sc3), d3,
        produced_by=(mixed, gt2, 1.0))
    d_wd2 = _weight_grad(a3, dy3, "ffn2_dwd")
    d_wg2 = _weight_grad(dg3, h3, "ffn2_dwg")
    d_wu2 = _weight_grad(du3, h3, "ffn2_dwu")
    dmix_in = _matmul([(dmixed, wo)], NT, F32, "mix_out_bwd", tm=512, tn=GDN_WIDTH + POOL_WIDTH, tk=D)
    d_wo = _matmul([(mix_in, dmixed)], TN, WIRE_DTYPE, "mix_dwo", tm=GDN_WIDTH + POOL_WIDTH, tn=D, tk=1024)
    do, dzp, d_gn, d_pw, d_ps = _mix_post_bwd(dmix_in, o, proj, gdn_norm, pool_w_m, pool_scale, "mix_post_bwd")
    dvn, dw_c, dqd, dkd, dp_c, dcd = _gdn_scan_bwd(do, w_c, qd_c, kd_c, p_c, cd_c, s_all, vn_c, "gdn_scan_bwd")
    dqkv, dbg = _gdn_chunk_bwd(qkv, bg, tinv, u_c, w_c, dvn, dw_c, dqd, dkd, dp_c, dcd, "gdn_chunk_bwd")
    dproj, d_conv, d_al, d_dtb = _gdn_prep_bwd(proj, conv_full, a_log_l, dt_bias_l, dqkv, dbg, dzp, "gdn_prep_bwd")
    d_win_re = _matmul([(dproj, h2)], TN, WIRE_DTYPE, "proj_in_dw", tm=D_IN_PAD, tn=D, tk=1024)
    d_win_t = jnp.concatenate([d_win_re[:COL_Z + GDN_WIDTH], d_win_re[COL_BA:COL_BA + 2 * GDN_HEADS],
                               d_win_re[COL_P:COL_P + POOL_WIDTH]], axis=0)
    d_win_blocks = jnp.pad(d_win_t.reshape(N_DEV, Ws, D), ((0, 0), (0, Ws_pad - Ws), (0, 0)))
    parts23 = jnp.concatenate(
        [wire(d_win_blocks), wire(d_wo.reshape(N_DEV, Os, D)), wire(d_wg2.reshape(N_DEV, Fs, D)),
         wire(d_wu2.reshape(N_DEV, Fs, D)), wire(d_wd2.reshape(N_DEV, Fs, D))], axis=1)
    own23 = lax.dynamic_index_in_dim(parts23, me, 0, keepdims=False)
    g23_sent = _exchange_start(parts23, False, no_dep, "g23_start")
    d1, d_n2, dsc2, dsh2, dy1, dgt1 = _norm_bwd((dproj, w_in_re), x1, norm_mix, sc2 + token(g23_sent), d2,
                                                "proj_in_bwd_norm2_bwd", produced_by=(y1, gt1, 0.5))
    dg1, du1, grad_x, d_n1, dsc1, dsh1 = _swiglu_bwd(
        dy1, wd1, g1, u1, wg1_t, wu1_t, "ffn1_bwd_norm1_bwd", (x0, norm_ffn1, sc1), d1)

    dmod = jnp.concatenate([dsh1, dsc1, dgt1, dsh2, dsc2, dgt2, dsh3, dsc3, dgt3], axis=0)
    small_rows = [dmod.reshape(-1), d_n1[0], d_n2[0], d_n3[0], d_final[0], d_gn[0], d_ps[0],
                  d_al[0, GDN_HEADS:2 * GDN_HEADS], d_dtb[0, GDN_HEADS:2 * GDN_HEADS], loss_row[0, :1],
                  d_conv.reshape(-1), d_pw.reshape(-1)]
    lanes = 1024
    small_rows = [_rows_of(r, lanes) for r in small_rows]
    n_rows = [r.shape[0] for r in small_rows]
    row_off = [sum(n_rows[:i]) for i in range(len(n_rows))]
    total = -(-sum(n_rows) // 8) * 8
    slab = _pad_rows(jnp.concatenate(small_rows, axis=0), total)
    slab_all = _all_gather(slab, "gather_small_grads")
    summed = _sum_parts(slab_all, "sum_small_grads")

    def piece(idx, n):
        return summed[row_off[idx]:row_off[idx] + n_rows[idx]].reshape(-1)[:n]

    g_b_ada = piece(0, N_MOD * D).reshape(1, N_MOD * D)
    g_n1, g_n2, g_n3 = piece(1, D).reshape(1, D), piece(2, D).reshape(1, D), piece(3, D).reshape(1, D)
    g_final = piece(4, D)
    g_gn = piece(5, HEAD_DIM).reshape(1, HEAD_DIM)
    g_ps = piece(6, POOL_WIDTH).reshape(1, POOL_WIDTH)
    g_al = piece(7, GDN_HEADS).reshape(1, GDN_HEADS)
    g_dtb = piece(8, GDN_HEADS).reshape(1, GDN_HEADS)
    loss = piece(9, 1)[0]
    g_conv = lax.dynamic_slice(piece(10, CONV_K * QKV_WIDTH).reshape(1, CONV_K, QKV_WIDTH), (0, 0, me * Cs),
                               (1, CONV_K, Cs))
    g_pw = piece(11, POOL_GROUPS * 128 * 128).reshape(1, POOL_GROUPS, 128, 128)

    dmod_all = slab_all[:, row_off[0]:row_off[0] + n_rows[0], :].reshape(N_DEV, -1)[:, :N_MOD * D]
    g_w_ada = _w_ada_grad(c_all, lax.dynamic_slice(dmod_all, (0, me * Ms), (N_DEV, Ms)), "w_ada_grad")[None]

    def send_ffn1(a, b, which, dep):
        parts = _weight_grad(a, b, f"ffn1_{which}", dep=dep).reshape(N_DEV, Fs, D)
        own = lax.dynamic_index_in_dim(parts, me, 0, keepdims=False)
        return _exchange_start(parts, False, no_dep, f"g1_{which}_start"), own

    g1_wg, own_wg = send_ffn1(dg1, h1, "dwg", summed)
    g1_wu, own_wu = send_ffn1(du1, h1, "dwu", g1_wg[4])
    g1_wd, own_wd = send_ffn1(a1, dy1, "dwd", g1_wu[4])

    big23 = _sum_parts(with_own(_exchange_wait(g23_sent, g1_wd[4], False, "g23_wait"), own23), "sum_grads23")
    g_rows = dict(w_in=big23[:Ws], w_out=big23[off23[1]:off23[1] + Os],
                  ffn2_gate=big23[off23[2]:off23[2] + Fs], ffn2_up=big23[off23[3]:off23[3] + Fs],
                  ffn2_down=big23[off23[4]:off23[4] + Fs])
    column_sharded = ("w_in", "ffn1_gate", "ffn1_up", "ffn2_gate", "ffn2_up")

    names = ["w_ada", "b_ada", "norm_ffn1", "ffn1_gate", "ffn1_up", "ffn1_down", "norm_mix", "w_in", "conv_w",
             "a_log", "dt_bias", "gdn_norm", "pool_w", "pool_scale", "w_out", "norm_ffn2", "ffn2_gate", "ffn2_up",
             "ffn2_down", "final_norm"]
    weights = dict(zip(names, [w_ada, b_ada, norm_ffn1, ffn1_gate, ffn1_up, ffn1_down, norm_mix, w_in, conv_w,
                               a_log, dt_bias, gdn_norm, pool_w, pool_scale, w_out, norm_ffn2, ffn2_gate, ffn2_up,
                               ffn2_down, final_norm]))
    ms = dict(zip(names, [m_w_ada, m_b_ada, m_norm_ffn1, m_ffn1_gate, m_ffn1_up, m_ffn1_down, m_norm_mix, m_w_in,
                          m_conv_w, m_a_log, m_dt_bias, m_gdn_norm, m_pool_w, m_pool_scale, m_w_out, m_norm_ffn2,
                          m_ffn2_gate, m_ffn2_up, m_ffn2_down, m_final_norm]))
    vs = dict(zip(names, [v_w_ada, v_b_ada, v_norm_ffn1, v_ffn1_gate, v_ffn1_up, v_ffn1_down, v_norm_mix, v_w_in,
                          v_conv_w, v_a_log, v_dt_bias, v_gdn_norm, v_pool_w, v_pool_scale, v_w_out, v_norm_ffn2,
                          v_ffn2_gate, v_ffn2_up, v_ffn2_down, v_final_norm]))
    grads = dict(w_ada=g_w_ada, b_ada=g_b_ada, norm_ffn1=g_n1, norm_mix=g_n2, conv_w=g_conv,
                 a_log=g_al, dt_bias=g_dtb, gdn_norm=g_gn, pool_w=g_pw, pool_scale=g_ps,
                 norm_ffn2=g_n3, final_norm=g_final)
    delta, new_m, new_v = {}, {}, {}

    def adamw_big(n):
        if n in column_sharded:
            view, back = (lambda a: a[0].T), (lambda a: a.T[None])
        else:
            view, back = (lambda a: a[0]), (lambda a: a[None])
        g = g_rows[n] if n in g_rows else view(grads[n])
        d_, m_, v_ = _adamw(view(weights[n]), g, view(ms[n]), view(vs[n]), f"adamw_{n}")
        grads[n], delta[n], new_m[n], new_v[n] = back(g), back(d_), back(m_), back(v_)

    early = ["w_ada", "w_in", "w_out", "ffn2_gate", "ffn2_up", "ffn2_down"]
    late = ["ffn1_gate", "ffn1_up", "ffn1_down"]
    for n in early:
        adamw_big(n)
    done = sum(delta[n][0, :1, :1] for n in early)

    def arrived(started, own, which):
        return _sum_parts(with_own(_exchange_wait(started, done, False, f"g1_{which}_wait"), own), f"sum_{which}")

    g_rows["ffn1_gate"] = arrived(g1_wg, own_wg, "dwg")
    g_rows["ffn1_up"] = arrived(g1_wu, own_wu, "dwu")
    g_rows["ffn1_down"] = arrived(g1_wd, own_wd, "dwd")
    for n in late:
        adamw_big(n)
    small_names = [n for n in names if n not in early + late]
    pack = lambda src: jnp.concatenate([_rows_of(src[n]) for n in small_names], axis=0)
    p_rows = [_rows_of(weights[n]).shape[0] for n in small_names]
    p_total = -(-sum(p_rows) // 8) * 8
    packed = [_pad_rows(pack(src), p_total) for src in (weights, grads, ms, vs)]
    d_s, m_s, v_s = _adamw(*packed, "adamw_small")
    off = 0
    for n, r in zip(small_names, p_rows):
        shp = weights[n].shape
        size = weights[n].size
        for dst, src in ((delta, d_s), (new_m, m_s), (new_v, v_s)):
            dst[n] = src[off:off + r].reshape(-1)[:size].reshape(shp)
        off += r

    return (loss, grad_x[None], *[grads[n] for n in names], *[delta[n] for n in names],
            *[new_m[n] for n in names], *[new_v[n] for n in names])
```

```python
import functools

import jax
import jax.numpy as jnp
from jax import lax
from jax.experimental import pallas as pl
from jax.experimental.pallas import tpu as pltpu

F32 = jnp.float32
MXU_DTYPE = jnp.bfloat16
WIRE_DTYPE = jnp.bfloat16
EPS = 1e-6
N_DEV = 8
GDN_HEADS = 4
HEAD_DIM = 128
GDN_WIDTH = GDN_HEADS * HEAD_DIM
POOL_WINDOWS = (2, 4, 8, 16)
POOL_GROUPS = len(POOL_WINDOWS)
POOL_WIDTH = 512
CONV_K = 4
CHUNK = 64
QKV_WIDTH = 3 * GDN_WIDTH
D_IN = 4 * GDN_WIDTH + 2 * GDN_HEADS + POOL_WIDTH
D_IN_PAD = 4 * GDN_WIDTH + POOL_WIDTH + 128
COL_Z = QKV_WIDTH
COL_P = 4 * GDN_WIDTH
COL_BA = 4 * GDN_WIDTH + POOL_WIDTH
N_MOD = 9
HALO = 16
VMEM_LIMIT = 56 * 1024 * 1024
ADAM_LR, ADAM_B1, ADAM_B2, ADAM_EPS, ADAM_WD, ADAM_STEP = 0.001, 0.9, 0.999, 1e-08, 0.01, 10
FFN_TOKEN_TILE = 256
FFN_HIDDEN_TILE = 1408
CHUNKS_PER_STEP = 4
SCAN_CHUNKS_PER_STEP = 4

NT = (((1,), (1,)), ((), ()))
NN = (((1,), (0,)), ((), ()))
TN = (((0,), (0,)), ((), ()))


def _params(*sem):
    return pltpu.CompilerParams(dimension_semantics=tuple(sem), vmem_limit_bytes=VMEM_LIMIT)


def _dot(a, b, dims):
    return lax.dot_general(a, b, dims, preferred_element_type=F32)


def _mdot(a, b, dims):
    return _dot(a.astype(MXU_DTYPE), b.astype(MXU_DTYPE), dims)


def _split(a):
    hi = a.astype(jnp.bfloat16)
    return hi, (a - hi.astype(F32)).astype(jnp.bfloat16)


def _dot3(a, b, dims):
    (ah, al), (bh, bl) = a, b
    return (_dot(al, bh, dims) + _dot(ah, bl, dims)) + _dot(ah, bh, dims)


def _sigmoid(v):
    return 1.0 / (1.0 + jnp.exp(-v))


def _softplus(v):
    return jnp.maximum(v, 0.0) + jnp.log(1.0 + jnp.exp(-jnp.abs(v)))


def _shift_rows(v, s):
    n = v.shape[0]
    s = s % n
    return v if s == 0 else pltpu.roll(v, s, 0)


def _tile(n, want):
    t = min(n, want)
    while n % t:
        t //= 2
    return t


def _all_gather(blocks, name, dep=None):
    n = len(blocks)

    def body(*refs):
        x_refs, out_refs = refs[:n], refs[-3 - n:-3]
        send_sems, recv_sems, local_sems = refs[-3:]
        x, y, c = lax.axis_index("x"), lax.axis_index("y"), lax.axis_index("c")
        me, sibling = (x, y, c), (x, y, 1 - c)
        chips = [(1 - x, y), (x, 1 - y), (1 - x, 1 - y)]

        def copy(a, k, blk, to, own=False):
            rows = out_refs[a].at[4 * blk[0] + 2 * blk[1] + blk[2]]
            return pltpu.make_async_remote_copy(
                src_ref=x_refs[a] if own else rows, dst_ref=rows,
                send_sem=send_sems.at[7 * a + k], recv_sem=recv_sems.at[7 * a + k],
                device_id=to, device_id_type=pl.DeviceIdType.MESH)

        mine = [pltpu.make_async_copy(x_refs[a], out_refs[a].at[4 * x + 2 * y + c], local_sems.at[a])
                for a in range(n)]
        for cp in mine:
            cp.start()
        sent = []
        for a in range(n):
            sent.append(copy(a, 0, me, sibling, own=True))
            sent += [copy(a, 1 + j, me, (*chip, c), own=True) for j, chip in enumerate(chips)]
        for cp in sent:
            cp.start()
        for a in range(n):
            for j, chip in enumerate(chips):
                copy(a, 1 + j, (*chip, c), me).wait_recv()
                passed = copy(a, 4 + j, (*chip, c), sibling)
                passed.start()
                sent.append(passed)
        for a in range(n):
            copy(a, 0, sibling, me).wait_recv()
            for j, chip in enumerate(chips):
                copy(a, 4 + j, (*chip, 1 - c), me).wait_recv()
        for cp in sent:
            cp.wait_send()
        for cp in mine:
            cp.wait()

    hbm = pl.BlockSpec(memory_space=pltpu.HBM)
    return pl.pallas_call(
        body, name=name,
        out_shape=[jax.ShapeDtypeStruct((N_DEV,) + b.shape, b.dtype) for b in blocks],
        in_specs=[hbm] * n + [pl.BlockSpec(memory_space=pl.ANY)] * (dep is not None),
        out_specs=[hbm] * n,
        scratch_shapes=[pltpu.SemaphoreType.DMA((7 * n,)), pltpu.SemaphoreType.DMA((7 * n,)),
                        pltpu.SemaphoreType.DMA((n,))],
    )(*(list(blocks) + ([] if dep is None else [dep])))


_HBM = pl.BlockSpec(memory_space=pltpu.HBM)
_SEM = pl.BlockSpec(memory_space=pltpu.SEMAPHORE)
_ANY = pl.BlockSpec(memory_space=pl.ANY)
_EFFECT = pltpu.SideEffectType.DATAFLOW_SIDE_EFFECTING
_FLIPS = [(0, 0, 1), (0, 1, 0), (0, 1, 1), (1, 0, 0), (1, 0, 1), (1, 1, 0), (1, 1, 1)]


def _peers():
    x, y, c = lax.axis_index("x"), lax.axis_index("y"), lax.axis_index("c")
    return 4 * x + 2 * y + c, [(1 - x if fx else x, 1 - y if fy else y, 1 - c if fc else c)
                               for fx, fy, fc in _FLIPS]


def _exchange_start(srcs, gather, dep, name):
    n = len(srcs)
    lands = [(N_DEV,) + tuple(s.shape if gather else s.shape[1:]) for s in srcs]

    def body(*refs):
        src_refs, land_refs = refs[:n], refs[n:2 * n]
        send_sems, recv_sems = refs[2 * n + 1], refs[2 * n + 2]
        token = refs[-1]
        me, peers = _peers()
        for a in range(n):
            for k, (px, py, pc) in enumerate(peers):
                pltpu.make_async_remote_copy(
                    src_ref=src_refs[a] if gather else src_refs[a].at[4 * px + 2 * py + pc],
                    dst_ref=land_refs[a].at[me],
                    send_sem=send_sems.at[7 * a + k], recv_sem=recv_sems.at[7 * a + k],
                    device_id=(px, py, pc), device_id_type=pl.DeviceIdType.MESH).start()
        token[...] = jnp.zeros_like(token)

    srcs = [pltpu.with_memory_space_constraint(s, pltpu.HBM) for s in srcs]
    empties = [pltpu.with_memory_space_constraint(lax.empty(shape, s.dtype), pltpu.HBM)
               for shape, s in zip(lands, srcs)]
    out = pl.pallas_call(
        body, name=name,
        out_shape=(pltpu.SemaphoreType.DMA((7 * n,)), pltpu.SemaphoreType.DMA((7 * n,)),
                   *[pltpu.HBM(shape, s.dtype) for shape, s in zip(lands, srcs)],
                   jax.ShapeDtypeStruct((8, 128), F32)),
        in_specs=(*[_HBM] * (2 * n), _ANY),
        out_specs=(_SEM, _SEM, *[_HBM] * n, pl.BlockSpec(memory_space=pltpu.VMEM)),
        input_output_aliases={n + a: 2 + a for a in range(n)},
        compiler_params=pltpu.CompilerParams(has_side_effects=_EFFECT),
    )(*srcs, *empties, dep)
    return out[0], out[1], srcs, list(out[2:2 + n]), out[-1]


def _exchange_wait(started, after, gather, name):
    send_sems, recv_sems, srcs, lands, _ = started
    n = len(srcs)

    def body(*refs):
        src_refs, land_refs = refs[:n], refs[n:2 * n]
        send_sems, recv_sems = refs[2 * n], refs[2 * n + 1]
        _, peers = _peers()
        for a in range(n):
            for k, peer in enumerate(peers):
                copy = pltpu.make_async_remote_copy(
                    src_ref=src_refs[a] if gather else src_refs[a].at[0], dst_ref=land_refs[a].at[0],
                    send_sem=send_sems.at[7 * a + k], recv_sem=recv_sems.at[7 * a + k],
                    device_id=peer, device_id_type=pl.DeviceIdType.MESH)
                copy.wait_send()
                copy.wait_recv()

    out = pl.pallas_call(
        body, name=name,
        out_shape=[pltpu.HBM(z.shape, z.dtype) for z in lands],
        in_specs=(*[_HBM] * (2 * n), _SEM, _SEM, _ANY), out_specs=[_HBM] * n,
        input_output_aliases={n + a: a for a in range(n)},
        compiler_params=pltpu.CompilerParams(has_side_effects=_EFFECT),
    )(*srcs, *lands, send_sems, recv_sems, after)
    return list(out)


def _matmul(pairs, dims, out_dtype, name, tm=512, tn=512, tk=512, dep=None):
    a0, b0 = pairs[0]
    if dims == TN:
        K, M = a0.shape
    else:
        M, K = a0.shape
    N = b0.shape[0] if dims == NT else b0.shape[1]
    tm, tn, tk = _tile(M, tm), _tile(N, tn), _tile(K, tk)
    nk = K // tk
    n_pairs = len(pairs)
    n_in = 2 * n_pairs + (dep is not None)

    def body(*refs):
        out_ref = refs[n_in]

        def product():
            total = _dot(refs[0][...], refs[1][...], dims)
            for p in range(1, n_pairs):
                total += _dot(refs[2 * p][...], refs[2 * p + 1][...], dims)
            return total

        if nk == 1:
            out_ref[...] = product().astype(out_ref.dtype)
            return
        acc_ref = refs[n_in + 1]
        k = pl.program_id(2)

        @pl.when(k == 0)
        def _():
            acc_ref[...] = product()

        @pl.when((k > 0) & (k < nk - 1))
        def _():
            acc_ref[...] += product()

        @pl.when(k == nk - 1)
        def _():
            out_ref[...] = (acc_ref[...] + product()).astype(out_ref.dtype)

    if dims == TN:
        a_spec = pl.BlockSpec((tk, tm), lambda i, j, k: (k, i))
    else:
        a_spec = pl.BlockSpec((tm, tk), lambda i, j, k: (i, k))
    if dims == NT:
        b_spec = pl.BlockSpec((tn, tk), lambda i, j, k: (j, k))
    else:
        b_spec = pl.BlockSpec((tk, tn), lambda i, j, k: (k, j))
    args, specs = [], []
    for a, b in pairs:
        args += [a, b]
        specs += [a_spec, b_spec]
    if dep is not None:
        args.append(dep)
        specs.append(_ANY)
    return pl.pallas_call(
        body, name=name, grid=(M // tm, N // tn, nk),
        in_specs=specs, out_specs=pl.BlockSpec((tm, tn), lambda i, j, k: (i, j)),
        out_shape=jax.ShapeDtypeStruct((M, N), out_dtype),
        scratch_shapes=[pltpu.VMEM((tm, tn), F32)] * (nk > 1),
        compiler_params=_params("parallel", "parallel", "arbitrary"),
    )(*args)


def _vec_spec(d):
    return pl.BlockSpec((1, d), lambda i: (0, 0))


def _matmul_resid_norm_mod(a, b, x, gate, coef, nw, scale, shift, name):
    T, K = a.shape
    D = b.shape[1]
    tm = _tile(T, 512)

    def body(a_ref, b_ref, x_ref, g_ref, nw_ref, sc_ref, sh_ref, y_ref, xo_ref, h_ref):
        y = _dot(a_ref[...], b_ref[...], NN)
        y_ref[...] = y
        xf = x_ref[...] + (coef * g_ref[...]) * y
        xo_ref[...] = xf
        r = lax.rsqrt(jnp.mean(xf * xf, axis=-1, keepdims=True) + EPS)
        h_ref[...] = ((xf * r) * nw_ref[...] * (1.0 + sc_ref[...]) + sh_ref[...]).astype(h_ref.dtype)

    row = pl.BlockSpec((tm, D), lambda i: (i, 0))
    vec = _vec_spec(D)
    return pl.pallas_call(
        body, name=name, grid=(T // tm,),
        in_specs=[pl.BlockSpec((tm, K), lambda i: (i, 0)), _resident((K, D)), row, vec, vec, vec, vec],
        out_specs=[row, row, row],
        out_shape=[jax.ShapeDtypeStruct((T, D), F32), jax.ShapeDtypeStruct((T, D), F32),
                   jax.ShapeDtypeStruct((T, D), MXU_DTYPE)],
        compiler_params=_params("parallel"),
    )(a, b, x, gate, nw, scale, shift)


def _norm_bwd(dh, x, nw, scale, dres, name, produced_by=None):
    T, D = x.shape
    tm = _tile(T, 512)

    def body(*refs):
        if isinstance(dh, tuple):
            dh_value = _dot(refs[0][...], refs[1][...], NN)
            refs = refs[1:]
        else:
            dh_value = refs[0][...]
        _, x_ref, nw_ref, sc_ref, dr_ref = refs[:5]
        n_in = 5 if produced_by is None else 7
        dx_ref, dnw_ref, dsc_ref, dsh_ref = refs[n_in:n_in + 4]

        @pl.when(pl.program_id(0) == 0)
        def _():
            dnw_ref[...] = jnp.zeros_like(dnw_ref)
            dsc_ref[...] = jnp.zeros_like(dsc_ref)
            dsh_ref[...] = jnp.zeros_like(dsh_ref)
            if produced_by is not None:
                refs[n_in + 5][...] = jnp.zeros_like(refs[n_in + 5])

        xf, dh_ = x_ref[...], dh_value
        r = lax.rsqrt(jnp.mean(xf * xf, axis=-1, keepdims=True) + EPS)
        xn = xf * r
        one_sc = 1.0 + sc_ref[...]
        dsh_ref[...] += jnp.sum(dh_, axis=0, keepdims=True)
        t = dh_ * xn
        dsc_ref[...] += jnp.sum(t, axis=0, keepdims=True) * nw_ref[...]
        dnw_ref[...] += jnp.sum(t, axis=0, keepdims=True) * one_sc
        dxn = dh_ * (nw_ref[...] * one_sc)
        dx = dr_ref[...] + r * (dxn - xn * jnp.mean(dxn * xn, axis=-1, keepdims=True))
        dx_ref[...] = dx
        if produced_by is not None:
            y_ref, g_ref, dy_ref, dg_ref = refs[5], refs[6], refs[n_in + 4], refs[n_in + 5]
            dy_ref[...] = ((produced_by[2] * g_ref[...]) * dx).astype(dy_ref.dtype)
            dg_ref[...] += produced_by[2] * jnp.sum(dx * y_ref[...], axis=0, keepdims=True)

    row = pl.BlockSpec((tm, D), lambda i: (i, 0))
    vec = _vec_spec(D)
    vec_out = jax.ShapeDtypeStruct((1, D), F32)
    if isinstance(dh, tuple):
        k_dim = dh[0].shape[1]
        args, in_specs = [dh[0], dh[1]], [pl.BlockSpec((tm, k_dim), lambda i: (i, 0)), _resident((k_dim, D))]
    else:
        args, in_specs = [dh], [row]
    args, in_specs = args + [x, nw, scale, dres], in_specs + [row, vec, vec, row]
    out_specs, out_shape = [row, vec, vec, vec], [jax.ShapeDtypeStruct((T, D), F32), vec_out, vec_out, vec_out]
    if produced_by is not None:
        args += [produced_by[0], produced_by[1]]
        in_specs += [row, vec]
        out_specs += [row, vec]
        out_shape += [jax.ShapeDtypeStruct((T, D), MXU_DTYPE), vec_out]
    return pl.pallas_call(
        body, name=name, grid=(T // tm,),
        in_specs=in_specs, out_specs=out_specs, out_shape=out_shape,
        compiler_params=_params("arbitrary"),
    )(*args)


def _rms(xf):
    r = lax.rsqrt(jnp.mean(xf * xf, axis=-1, keepdims=True) + EPS)
    return r, xf * r


def _norm_bwd_math(dh, xf, nw, sc):
    r, xn = _rms(xf)
    t = dh * xn
    dxn = dh * (nw * (1.0 + sc))
    dx = r * (dxn - xn * jnp.mean(dxn * xn, axis=-1, keepdims=True))
    return dx, jnp.sum(dh, axis=0, keepdims=True), jnp.sum(t, axis=0, keepdims=True)


def _swiglu_fwd(h, wg_t, wu_t, wd, name, norm_in=None, resid_out=None, loss_out=None):
    T, D = h.shape
    Fdim = wd.shape[0]
    tm, tf = _tile(T, FFN_TOKEN_TILE), _tile(Fdim, FFN_HIDDEN_TILE)
    row = pl.BlockSpec((tm, D), lambda i: (i, 0))
    frow = pl.BlockSpec((tm, Fdim), lambda i: (i, 0))
    vec, wres = _vec_spec(D), _resident((Fdim, D))
    f32_row, mxu_row = jax.ShapeDtypeStruct((T, D), F32), jax.ShapeDtypeStruct((T, D), MXU_DTYPE)
    vec_out = jax.ShapeDtypeStruct((1, D), F32)
    args, in_specs = [h, wg_t, wu_t, wd], [row, wres, wres, wres]
    out_shape, out_specs = [jax.ShapeDtypeStruct((T, Fdim), MXU_DTYPE)] * 3, [frow] * 3
    if norm_in is not None:
        args += list(norm_in)
        in_specs += [vec] * 3
        out_shape, out_specs = out_shape + [mxu_row], out_specs + [row]
    if resid_out is not None:
        x, gate, coef, nw, sc, sh = resid_out
        args += [x, gate, nw, sc, sh]
        in_specs += [row, vec, vec, vec, vec]
        out_shape, out_specs = out_shape + [f32_row, f32_row, mxu_row], out_specs + [row, row, row]
    if loss_out is not None:
        args += list(loss_out)
        in_specs += [row, vec, vec, row]
        out_shape += [jax.ShapeDtypeStruct((1, 128), F32), f32_row, vec_out, mxu_row, vec_out]
        out_specs += [pl.BlockSpec((1, 128), lambda i: (0, 0)), row, vec, row, vec]

    def body(*refs):
        it = iter(refs)
        h_ref, wg_ref, wu_ref, wd_ref = next(it), next(it), next(it), next(it)
        norm_refs = [next(it) for _ in range(3)] if norm_in is not None else None
        resid_refs = [next(it) for _ in range(5)] if resid_out is not None else None
        loss_refs = [next(it) for _ in range(4)] if loss_out is not None else None
        g_ref, u_ref, a_ref = next(it), next(it), next(it)
        if norm_in is not None:
            nw_ref, sc_ref, sh_ref = norm_refs
            _, xn = _rms(h_ref[...])
            hh = (xn * nw_ref[...] * (1.0 + sc_ref[...]) + sh_ref[...]).astype(MXU_DTYPE)
            next(it)[...] = hh
        else:
            hh = h_ref[...]
        y = None
        for k in range(Fdim // tf):
            ks = slice(k * tf, (k + 1) * tf)
            g = _dot(hh, wg_ref[ks, :], NT)
            u = _dot(hh, wu_ref[ks, :], NT)
            a = ((g * _sigmoid(g)) * u).astype(a_ref.dtype)
            g_ref[:, ks] = g.astype(g_ref.dtype)
            u_ref[:, ks] = u.astype(u_ref.dtype)
            a_ref[:, ks] = a
            part = _dot(a, wd_ref[ks, :], NN)
            y = part if y is None else y + part
        if resid_out is not None:
            x_ref, gt_ref, nw_ref, sc_ref, sh_ref = resid_refs
            y_ref, xo_ref, hn_ref = next(it), next(it), next(it)
            y_ref[...] = y
            xf = x_ref[...] + (resid_out[2] * gt_ref[...]) * y
            xo_ref[...] = xf
            _, xn = _rms(xf)
            hn_ref[...] = (xn * nw_ref[...] * (1.0 + sc_ref[...]) + sh_ref[...]).astype(hn_ref.dtype)
        if loss_out is not None:
            x_ref, gt_ref, fw_ref, t_ref = loss_refs
            loss_ref, dx_ref, dfw_ref, dy_ref, dg_ref = [next(it) for _ in range(5)]

            @pl.when(pl.program_id(0) == 0)
            def _():
                loss_ref[...] = jnp.zeros_like(loss_ref)
                dfw_ref[...] = jnp.zeros_like(dfw_ref)
                dg_ref[...] = jnp.zeros_like(dg_ref)

            r, xn = _rms(x_ref[...] + (0.5 * gt_ref[...]) * y)
            err = xn * fw_ref[...] - t_ref[...]
            per_tok = jnp.mean(err * err, axis=-1, keepdims=True)
            loss_ref[...] += 0.5 * jnp.sum(per_tok, axis=0, keepdims=True)
            d_out = err * (1.0 / D)
            dfw_ref[...] += jnp.sum(d_out * xn, axis=0, keepdims=True)
            dxn = d_out * fw_ref[...]
            dx = r * (dxn - xn * jnp.mean(dxn * xn, axis=-1, keepdims=True))
            dx_ref[...] = dx
            dy_ref[...] = ((0.5 * gt_ref[...]) * dx).astype(dy_ref.dtype)
            dg_ref[...] += 0.5 * jnp.sum(dx * y, axis=0, keepdims=True)

    return pl.pallas_call(
        body, name=name, grid=(T // tm,), in_specs=in_specs, out_specs=out_specs, out_shape=out_shape,
        compiler_params=_params("arbitrary" if loss_out is not None else "parallel"),
    )(*args)


def _swiglu_bwd(dy, wd, g, u, wg_t, wu_t, name, norm_in, dres, produced_by=None):
    T, D = dy.shape
    Fdim = wd.shape[0]
    tm, tf = _tile(T, FFN_TOKEN_TILE), _tile(Fdim, FFN_HIDDEN_TILE)
    row = pl.BlockSpec((tm, D), lambda i: (i, 0))
    frow = pl.BlockSpec((tm, Fdim), lambda i: (i, 0))
    vec, wres = _vec_spec(D), _resident((Fdim, D))
    vec_out = jax.ShapeDtypeStruct((1, D), F32)
    args, in_specs = [dy, wd, g, u, wg_t, wu_t, *norm_in, dres], [row, wres, frow, frow, wres, wres, row, vec, vec, row]
    out_shape = [jax.ShapeDtypeStruct((T, Fdim), MXU_DTYPE)] * 2 + [jax.ShapeDtypeStruct((T, D), F32)] + [vec_out] * 3
    out_specs = [frow, frow, row, vec, vec, vec]
    if produced_by is not None:
        args += [produced_by[0], produced_by[1]]
        in_specs += [row, vec]
        out_shape += [jax.ShapeDtypeStruct((T, D), MXU_DTYPE), vec_out]
        out_specs += [row, vec]

    def body(*refs):
        it = iter(refs)
        dy_ref, wd_ref, g_ref, u_ref, wg_ref, wu_ref, x_ref, nw_ref, sc_ref, dr_ref = [next(it) for _ in range(10)]
        prev_refs = [next(it), next(it)] if produced_by is not None else None
        dg_ref, du_ref, dx_ref, dnw_ref, dsc_ref, dsh_ref = [next(it) for _ in range(6)]
        prev_out = [next(it), next(it)] if produced_by is not None else None

        @pl.when(pl.program_id(0) == 0)
        def _():
            dnw_ref[...] = jnp.zeros_like(dnw_ref)
            dsc_ref[...] = jnp.zeros_like(dsc_ref)
            dsh_ref[...] = jnp.zeros_like(dsh_ref)
            if produced_by is not None:
                prev_out[1][...] = jnp.zeros_like(prev_out[1])

        dyy = dy_ref[...]
        dh = None
        for k in range(Fdim // tf):
            ks = slice(k * tf, (k + 1) * tf)
            da = _dot(dyy, wd_ref[ks, :], NT)
            gg = g_ref[:, ks].astype(F32)
            sig = _sigmoid(gg)
            dg = (da * u_ref[:, ks].astype(F32) * (sig * (1.0 + gg * (1.0 - sig)))).astype(dg_ref.dtype)
            du = (da * (gg * sig)).astype(du_ref.dtype)
            dg_ref[:, ks] = dg
            du_ref[:, ks] = du
            part = _dot(dg, wg_ref[ks, :], NN) + _dot(du, wu_ref[ks, :], NN)
            dh = part if dh is None else dh + part
        dx_norm, dsh_row, t_row = _norm_bwd_math(dh, x_ref[...], nw_ref[...], sc_ref[...])
        dsh_ref[...] += dsh_row
        dsc_ref[...] += t_row * nw_ref[...]
        dnw_ref[...] += t_row * (1.0 + sc_ref[...])
        dx = dr_ref[...] + dx_norm
        dx_ref[...] = dx
        if produced_by is not None:
            prev_out[0][...] = ((produced_by[2] * prev_refs[1][...]) * dx).astype(prev_out[0].dtype)
            prev_out[1][...] += produced_by[2] * jnp.sum(dx * prev_refs[0][...], axis=0, keepdims=True)

    return pl.pallas_call(
        body, name=name, grid=(T // tm,), in_specs=in_specs, out_specs=out_specs, out_shape=out_shape,
        compiler_params=_params("arbitrary"),
    )(*args)


def _resident(shape):
    return pl.BlockSpec(shape, lambda i: (0,) * len(shape), pipeline_mode=pl.Buffered(1))


def _conv_act(window, w):
    y = window * w[CONV_K - 1:CONV_K, :]
    for j in range(CONV_K - 1):
        y += _shift_rows(window, CONV_K - 1 - j) * w[j:j + 1, :]
    return y


def _gdn_prep(proj, conv_w, a_log_l, dt_bias_l, name):
    T = proj.shape[0]
    tm = _tile(T, 256)
    hb = tm // 8

    def body(cur_ref, halo_ref, ba_ref, w_ref, al_ref, dtb_ref, qkv_ref, bg_ref):
        i = pl.program_id(0)
        halo = jnp.where(i == 0, 0.0, halo_ref[...])
        window = jnp.concatenate([halo, cur_ref[...]], axis=0)
        y = _conv_act(window, w_ref[...])[8:, :]
        act = y * _sigmoid(y)
        for hh in range(3 * GDN_HEADS):
            blk = act[:, hh * HEAD_DIM:(hh + 1) * HEAD_DIM]
            if hh < 2 * GDN_HEADS:
                rn = lax.rsqrt(jnp.sum(blk * blk, axis=-1, keepdims=True) + EPS)
                blk = blk * rn
                if hh < GDN_HEADS:
                    blk = blk * (HEAD_DIM ** -0.5)
            qkv_ref[:, hh * HEAD_DIM:(hh + 1) * HEAD_DIM] = blk
        ba = ba_ref[...]
        lane = lax.broadcasted_iota(jnp.int32, ba.shape, 1)
        beta = _sigmoid(ba)
        g = -jnp.exp(al_ref[...]) * _softplus(ba + dtb_ref[...])
        bg_ref[...] = jnp.where(lane < GDN_HEADS, beta, jnp.where(lane < 2 * GDN_HEADS, g, 0.0))

    return pl.pallas_call(
        body, name=name, grid=(T // tm,),
        in_specs=[pl.BlockSpec((tm, QKV_WIDTH), lambda i: (i, 0)),
                  pl.BlockSpec((8, QKV_WIDTH), lambda i: (jnp.maximum(i * hb - 1, 0), 0)),
                  pl.BlockSpec((tm, 128), lambda i: (i, COL_BA // 128)),
                  pl.BlockSpec((CONV_K, QKV_WIDTH), lambda i: (0, 0)),
                  pl.BlockSpec((1, 128), lambda i: (0, 0)), pl.BlockSpec((1, 128), lambda i: (0, 0))],
        out_specs=[pl.BlockSpec((tm, QKV_WIDTH), lambda i: (i, 0)), pl.BlockSpec((tm, 128), lambda i: (i, 0))],
        out_shape=[jax.ShapeDtypeStruct((T, QKV_WIDTH), F32), jax.ShapeDtypeStruct((T, 128), F32)],
        compiler_params=_params("parallel"),
    )(proj, proj, proj, conv_w, a_log_l, dt_bias_l)


def _chunk_cumsum(v, reverse=False):
    row = lax.broadcasted_iota(jnp.int32, v.shape, 0)
    s = 1
    while s < CHUNK:
        if reverse:
            v = v + jnp.where(row < CHUNK - s, _shift_rows(v, -s), 0.0)
        else:
            v = v + jnp.where(row >= s, _shift_rows(v, s), 0.0)
        s *= 2
    return v


def _row_form(cols):
    padded = jnp.concatenate([cols, jnp.zeros((128 - CHUNK, 128), F32)], axis=0)
    return padded.T[:, :CHUNK]


def _chunk_masks():
    ri = lax.broadcasted_iota(jnp.int32, (CHUNK, CHUNK), 0)
    ci = lax.broadcasted_iota(jnp.int32, (CHUNK, CHUNK), 1)
    return ri >= ci, ri > ci, (ri == ci).astype(F32)


def _unit_lower_inverses(ms, eye):
    rs = [eye - m for m in ms]
    ps = [_split(-m) for m in ms]
    s = 2
    while s < CHUNK:
        ps = [_split(_dot3(p, p, NN)) for p in ps]
        r_parts = [_split(r) for r in rs]
        rs = [r + _dot3(p, rp, NN) for r, p, rp in zip(rs, ps, r_parts)]
        s *= 2
    return rs


def _head_elementwise(k, beta, gc, gcr, causal):
    decay = jnp.where(causal, jnp.exp(jnp.where(causal, gc - gcr, 0.0)), 0.0)
    return decay, k * beta, jnp.exp(gc)


def _head_slices(hh):
    return (slice(hh * HEAD_DIM, (hh + 1) * HEAD_DIM),
            slice(GDN_WIDTH + hh * HEAD_DIM, GDN_WIDTH + (hh + 1) * HEAD_DIM),
            slice(2 * GDN_WIDTH + hh * HEAD_DIM, 2 * GDN_WIDTH + (hh + 1) * HEAD_DIM))


def _gdn_chunk_fwd(qkv, bg, name):
    T = qkv.shape[0]
    cb = _tile(T // CHUNK, CHUNKS_PER_STEP)
    rows = cb * CHUNK

    def body(qkv_ref, bg_ref, tinv_ref, u_ref, w_ref, qd_ref, kd_ref, p_ref, cd_ref):
        masks = _chunk_masks()
        causal, strict, eye = masks
        heads = []
        for ci in range(cb):
            rs = slice(ci * CHUNK, (ci + 1) * CHUNK)
            bgv = bg_ref[rs, :]
            gc_all = _chunk_cumsum(bgv)
            gc_rows = _row_form(gc_all)
            cd_ref[rs, :] = jnp.exp(jnp.broadcast_to(gc_all[CHUNK - 1:CHUNK, :], (CHUNK, 128)))
            for hh in range(GDN_HEADS):
                qs, ks, vs = _head_slices(hh)
                q, k, v = qkv_ref[rs, qs], qkv_ref[rs, ks], qkv_ref[rs, vs]
                beta = bgv[:, hh:hh + 1]
                gc = gc_all[:, GDN_HEADS + hh:GDN_HEADS + hh + 1]
                decay, kb, eg = _head_elementwise(k, beta, gc, gc_rows[GDN_HEADS + hh:GDN_HEADS + hh + 1, :], causal)
                hs = slice(hh * HEAD_DIM, (hh + 1) * HEAD_DIM)
                cs = slice(hh * CHUNK, (hh + 1) * CHUNK)
                qd_ref[rs, hs] = (q * eg).astype(qd_ref.dtype)
                kd_ref[rs, hs] = (k * jnp.exp(gc[CHUNK - 1:CHUNK, :] - gc)).astype(kd_ref.dtype)
                heads.append((rs, hs, cs, q, k, v * beta, kb, kb * eg, decay))
        kks = [_mdot(kb, k, NT) for (_, _, _, _, k, _, kb, _, _) in heads]
        qks = [_mdot(q, k, NT) for (_, _, _, q, k, _, _, _, _) in heads]
        tinvs = _unit_lower_inverses([jnp.where(strict, kk * hd[8], 0.0) for kk, hd in zip(kks, heads)], eye)
        t_parts = [_split(t) for t in tinvs]
        us = [_dot3(tp, _split(hd[5]), NN) for tp, hd in zip(t_parts, heads)]
        ws = [_dot3(tp, _split(hd[7]), NN) for tp, hd in zip(t_parts, heads)]
        for hd, tinv, u, w, qk in zip(heads, tinvs, us, ws, qks):
            rs, hs, cs = hd[0], hd[1], hd[2]
            tinv_ref[rs, cs] = tinv
            u_ref[rs, hs] = u
            w_ref[rs, hs] = w.astype(w_ref.dtype)
            p_ref[rs, cs] = jnp.where(causal, qk * hd[8], 0.0).astype(p_ref.dtype)

    def spec(width):
        return pl.BlockSpec((rows, width), lambda n: (n, 0))

    hw, cw = GDN_WIDTH, GDN_HEADS * CHUNK
    return pl.pallas_call(
        body, name=name, grid=(T // rows,),
        in_specs=[spec(QKV_WIDTH), spec(128)],
        out_specs=[spec(cw), spec(hw), spec(hw), spec(hw), spec(hw), spec(cw), spec(128)],
        out_shape=[jax.ShapeDtypeStruct((T, cw), F32), jax.ShapeDtypeStruct((T, hw), F32),
                   jax.ShapeDtypeStruct((T, hw), MXU_DTYPE), jax.ShapeDtypeStruct((T, hw), MXU_DTYPE),
                   jax.ShapeDtypeStruct((T, hw), MXU_DTYPE), jax.ShapeDtypeStruct((T, cw), MXU_DTYPE),
                   jax.ShapeDtypeStruct((T, 128), F32)],
        compiler_params=_params("parallel"),
    )(qkv, bg)


def _gdn_scan_fwd(u, w, qd, kd, p, cd, name):
    T = u.shape[0]
    cb = _tile(T // CHUNK, SCAN_CHUNKS_PER_STEP)
    rows = cb * CHUNK

    def body(u_ref, w_ref, qd_ref, kd_ref, p_ref, cd_ref, o_ref, s_all_ref, vn_ref, s_ref):
        @pl.when(pl.program_id(0) == 0)
        def _():
            s_ref[...] = jnp.zeros_like(s_ref)

        hss = [slice(hh * HEAD_DIM, (hh + 1) * HEAD_DIM) for hh in range(GDN_HEADS)]
        css = [slice(hh * CHUNK, (hh + 1) * CHUNK) for hh in range(GDN_HEADS)]
        s_cur = [s_ref[hh] for hh in range(GDN_HEADS)]
        for ci in range(cb):
            rs = slice(ci * CHUNK, (ci + 1) * CHUNK)
            for hh in range(GDN_HEADS):
                s_all_ref[ci * GDN_WIDTH + hh * HEAD_DIM:ci * GDN_WIDTH + (hh + 1) * HEAD_DIM, :] = s_cur[hh]
            s_ms = [s.astype(MXU_DTYPE) for s in s_cur]
            w_s = [_dot(w_ref[rs, hs], s_m, NN) for hs, s_m in zip(hss, s_ms)]
            q_s = [_dot(qd_ref[rs, hs], s_m, NN) for hs, s_m in zip(hss, s_ms)]
            v_ms = [(u_ref[rs, hs] - ws_).astype(MXU_DTYPE) for hs, ws_ in zip(hss, w_s)]
            k_v = [_dot(kd_ref[rs, hs], v_m, TN) for hs, v_m in zip(hss, v_ms)]
            p_v = [_dot(p_ref[rs, cs], v_m, NN) for cs, v_m in zip(css, v_ms)]
            for hh in range(GDN_HEADS):
                vn_ref[rs, hss[hh]] = v_ms[hh]
                o_ref[rs, hss[hh]] = q_s[hh] + p_v[hh]
                c_dec = cd_ref[ci * CHUNK:ci * CHUNK + 1, GDN_HEADS + hh:GDN_HEADS + hh + 1]
                s_cur[hh] = s_cur[hh] * c_dec + k_v[hh]
        for hh in range(GDN_HEADS):
            s_ref[hh] = s_cur[hh]

    def spec(width):
        return pl.BlockSpec((rows, width), lambda n: (n, 0))

    hw, cw = GDN_WIDTH, GDN_HEADS * CHUNK
    return pl.pallas_call(
        body, name=name, grid=(T // rows,),
        in_specs=[spec(hw), spec(hw), spec(hw), spec(hw), spec(cw), spec(128)],
        out_specs=[spec(hw), pl.BlockSpec((cb * GDN_WIDTH, HEAD_DIM), lambda n: (n, 0)), spec(hw)],
        out_shape=[jax.ShapeDtypeStruct((T, hw), F32),
                   jax.ShapeDtypeStruct((T // CHUNK * GDN_WIDTH, HEAD_DIM), F32),
                   jax.ShapeDtypeStruct((T, hw), MXU_DTYPE)],
        scratch_shapes=[pltpu.VMEM((GDN_HEADS, HEAD_DIM, HEAD_DIM), F32)],
        compiler_params=_params("arbitrary"),
    )(u, w, qd, kd, p, cd)


def _gdn_scan_bwd(do, w, qd, kd, p, cd, s_all, vn, name):
    T = do.shape[0]
    cb = _tile(T // CHUNK, SCAN_CHUNKS_PER_STEP)
    rows = cb * CHUNK
    n_steps = T // rows

    def body(do_ref, w_ref, qd_ref, kd_ref, p_ref, cd_ref, s_all_ref, vn_ref,
             dvn_ref, dw_ref, dqd_ref, dkd_ref, dp_ref, dcd_ref, ds_ref):
        @pl.when(pl.program_id(0) == 0)
        def _():
            ds_ref[...] = jnp.zeros_like(ds_ref)

        causal, _, _ = _chunk_masks()
        lane = lax.broadcasted_iota(jnp.int32, (CHUNK, 128), 1)
        heads = range(GDN_HEADS)
        hss = [slice(hh * HEAD_DIM, (hh + 1) * HEAD_DIM) for hh in heads]
        css = [slice(hh * CHUNK, (hh + 1) * CHUNK) for hh in heads]
        ds_cur = [ds_ref[hh] for hh in heads]
        for ci in reversed(range(cb)):
            rs = slice(ci * CHUNK, (ci + 1) * CHUNK)
            ds_ms = [d.astype(MXU_DTYPE) for d in ds_cur]
            s_olds = [s_all_ref[ci * GDN_WIDTH + hh * HEAD_DIM:ci * GDN_WIDTH + (hh + 1) * HEAD_DIM, :] for hh in heads]
            s_ms = [s.astype(MXU_DTYPE) for s in s_olds]
            do_ms = [do_ref[rs, hs].astype(MXU_DTYPE) for hs in hss]
            p_do = [_dot(p_ref[rs, cs], do_m, TN) for cs, do_m in zip(css, do_ms)]
            k_ds = [_dot(kd_ref[rs, hs], ds_m, NN) for hs, ds_m in zip(hss, ds_ms)]
            q_do = [_dot(qd_ref[rs, hs], do_m, TN) for hs, do_m in zip(hss, do_ms)]
            dqds = [_dot(do_m, s_m, NT) for do_m, s_m in zip(do_ms, s_ms)]
            dkds = [_dot(vn_ref[rs, hs], ds_m, NT) for hs, ds_m in zip(hss, ds_ms)]
            dps = [_dot(do_m, vn_ref[rs, hs], NT) for hs, do_m in zip(hss, do_ms)]
            dv_news = [a + b for a, b in zip(p_do, k_ds)]
            dvn_ms = [d.astype(MXU_DTYPE) for d in dv_news]
            w_dv = [_dot(w_ref[rs, hs], dvn_m, TN) for hs, dvn_m in zip(hss, dvn_ms)]
            dws = [_dot(dvn_m, s_m, NT) for dvn_m, s_m in zip(dvn_ms, s_ms)]
            dcd_tile = jnp.zeros((CHUNK, 128), F32)
            for hh in heads:
                dvn_ref[rs, hss[hh]] = dv_news[hh]
                dw_ref[rs, hss[hh]] = -dws[hh]
                dqd_ref[rs, hss[hh]] = dqds[hh]
                dkd_ref[rs, hss[hh]] = dkds[hh]
                dp_ref[rs, css[hh]] = jnp.where(causal, dps[hh], 0.0)
                dcd = jnp.sum(jnp.sum(s_olds[hh] * ds_cur[hh], axis=1, keepdims=True), axis=0, keepdims=True)
                dcd_tile = jnp.where(lane == GDN_HEADS + hh, dcd, dcd_tile)
                c_dec = cd_ref[ci * CHUNK:ci * CHUNK + 1, GDN_HEADS + hh:GDN_HEADS + hh + 1]
                ds_cur[hh] = c_dec * ds_cur[hh] + q_do[hh] - w_dv[hh]
            dcd_ref[rs, :] = dcd_tile
        for hh in heads:
            ds_ref[hh] = ds_cur[hh]

    def spec(width):
        return pl.BlockSpec((rows, width), lambda n: (n_steps - 1 - n, 0))

    hw, cw = GDN_WIDTH, GDN_HEADS * CHUNK
    return pl.pallas_call(
        body, name=name, grid=(n_steps,),
        in_specs=[spec(hw), spec(hw), spec(hw), spec(hw), spec(cw), spec(128),
                  pl.BlockSpec((cb * GDN_WIDTH, HEAD_DIM), lambda n: (n_steps - 1 - n, 0)), spec(hw)],
        out_specs=[spec(hw), spec(hw), spec(hw), spec(hw), spec(cw), spec(128)],
        out_shape=[jax.ShapeDtypeStruct((T, hw), F32)] * 4
        + [jax.ShapeDtypeStruct((T, cw), F32), jax.ShapeDtypeStruct((T, 128), F32)],
        scratch_shapes=[pltpu.VMEM((GDN_HEADS, HEAD_DIM, HEAD_DIM), F32)],
        compiler_params=_params("arbitrary"),
    )(do, w, qd, kd, p, cd, s_all, vn)


def _gdn_chunk_bwd(qkv, bg, tinv_all, u, w, dvn, dw, dqd, dkd, dp, dcd, name):
    T = qkv.shape[0]
    cb = _tile(T // CHUNK, CHUNKS_PER_STEP)
    rows = cb * CHUNK

    def body(qkv_ref, bg_ref, tinv_ref, u_ref, w_ref, dvn_ref, dw_ref, dqd_ref, dkd_ref, dp_ref, dcd_ref,
             dqkv_ref, dbg_ref):
        masks = _chunk_masks()
        causal, strict, _ = masks
        lane = lax.broadcasted_iota(jnp.int32, (CHUNK, 128), 1)
        row = lax.broadcasted_iota(jnp.int32, (CHUNK, 128), 0)
        heads = []
        for ci in range(cb):
            rs = slice(ci * CHUNK, (ci + 1) * CHUNK)
            bgv = bg_ref[rs, :]
            gc_all = _chunk_cumsum(bgv)
            gc_rows = _row_form(gc_all)
            for hh in range(GDN_HEADS):
                qs, ks, vs = _head_slices(hh)
                q, k = qkv_ref[rs, qs], qkv_ref[rs, ks]
                beta = bgv[:, hh:hh + 1]
                gc = gc_all[:, GDN_HEADS + hh:GDN_HEADS + hh + 1]
                decay, kb, eg = _head_elementwise(k, beta, gc, gc_rows[GDN_HEADS + hh:GDN_HEADS + hh + 1, :], causal)
                heads.append(dict(ci=ci, hh=hh, rs=rs, hs=slice(hh * HEAD_DIM, (hh + 1) * HEAD_DIM),
                                  cs=slice(hh * CHUNK, (hh + 1) * CHUNK), q=q, k=k, beta=beta, gc=gc,
                                  decay=decay, kb=kb, eg=eg))
        for hd in heads:
            hd["t"] = _split(tinv_ref[hd["rs"], hd["cs"]])
        for hd in heads:
            hd["kk"] = _mdot(hd["kb"], hd["k"], NT)
            hd["qk"] = _mdot(hd["q"], hd["k"], NT)
        for hd in heads:
            hd["dvb"] = _dot3(hd["t"], _split(dvn_ref[hd["rs"], hd["hs"]]), TN)
            hd["dkbeg"] = _dot3(hd["t"], _split(dw_ref[hd["rs"], hd["hs"]]), TN)
        for hd in heads:
            rs, hs = hd["rs"], hd["hs"]
            da = -(_mdot(hd["dvb"], u_ref[rs, hs], NT) + _mdot(hd["dkbeg"], w_ref[rs, hs], NT))
            dm = jnp.where(strict, da, 0.0)
            dp_ = dp_ref[rs, hd["cs"]]
            hd["dkk"] = dm * hd["decay"]
            hd["dqk"] = dp_ * hd["decay"]
            hd["e"] = (hd["dkk"] * hd["kk"] + hd["dqk"] * hd["qk"])
        for hd in heads:
            hd["dkb"] = _mdot(hd["dkk"], hd["k"], NN)
            hd["dk"] = _mdot(hd["dkk"], hd["kb"], TN) + _mdot(hd["dqk"], hd["q"], TN)
            hd["dq"] = _mdot(hd["dqk"], hd["k"], NN)
            onehot = (lane == GDN_HEADS + hd["hh"]).astype(jnp.bfloat16)
            e_hi, e_lo = _split(hd["e"])
            hd["col_sums"] = _dot(e_lo, onehot, TN) + _dot(e_hi, onehot, TN)
        tiles = {}
        for hd in heads:
            ci, hh, rs, hs = hd["ci"], hd["hh"], hd["rs"], hd["hs"]
            qs, ks, vs = _head_slices(hh)
            q, k, beta, gc, eg, kb = hd["q"], hd["k"], hd["beta"], hd["gc"], hd["eg"], hd["kb"]
            v = qkv_ref[rs, vs]
            dqd_, dkd_ = dqd_ref[rs, hs], dkd_ref[rs, hs]
            gl = gc[CHUNK - 1:CHUNK, :]
            ek = jnp.exp(gl - gc)
            dkb = hd["dkb"] + hd["dkbeg"] * eg
            deg = jnp.sum(dqd_ * q, axis=1, keepdims=True) + jnp.sum(hd["dkbeg"] * kb, axis=1, keepdims=True)
            dek = jnp.sum(dkd_ * k, axis=1, keepdims=True)
            dcd_ = dcd_ref[ci * CHUNK:ci * CHUNK + 1, GDN_HEADS + hh:GDN_HEADS + hh + 1]
            dgl = jnp.sum(dek * ek, axis=0, keepdims=True) + dcd_ * jnp.exp(gl)
            dgc = jnp.sum(hd["e"], axis=1, keepdims=True) + deg * eg - dek * ek
            dbeta_tile, dgc_tile = tiles.get(ci, (jnp.zeros((CHUNK, 128), F32), jnp.zeros((CHUNK, 128), F32)))
            dgc_tile += jnp.where(lane == GDN_HEADS + hh, dgc, 0.0) - hd["col_sums"]
            dgc_tile += jnp.where((lane == GDN_HEADS + hh) & (row == CHUNK - 1), dgl, 0.0)
            dbeta = jnp.sum(dkb * k, axis=1, keepdims=True) + jnp.sum(hd["dvb"] * v, axis=1, keepdims=True)
            dbeta_tile += jnp.where(lane == hh, dbeta, 0.0)
            tiles[ci] = (dbeta_tile, dgc_tile)
            dqkv_ref[rs, qs] = hd["dq"] + dqd_ * eg
            dqkv_ref[rs, ks] = hd["dk"] + dkd_ * ek + dkb * beta
            dqkv_ref[rs, vs] = hd["dvb"] * beta
        for ci in range(cb):
            dbeta_tile, dgc_tile = tiles[ci]
            dbg_ref[ci * CHUNK:(ci + 1) * CHUNK, :] = dbeta_tile + _chunk_cumsum(dgc_tile, reverse=True)

    def spec(width):
        return pl.BlockSpec((rows, width), lambda n: (n, 0))

    hw, cw = GDN_WIDTH, GDN_HEADS * CHUNK
    return pl.pallas_call(
        body, name=name, grid=(T // rows,),
        in_specs=[spec(QKV_WIDTH), spec(128), spec(cw), spec(hw), spec(hw), spec(hw), spec(hw), spec(hw),
                  spec(hw), spec(cw), spec(128)],
        out_specs=[spec(QKV_WIDTH), spec(128)],
        out_shape=[jax.ShapeDtypeStruct((T, QKV_WIDTH), F32), jax.ShapeDtypeStruct((T, 128), F32)],
        compiler_params=_params("parallel"),
    )(qkv, bg, tinv_all, u, w, dvn, dw, dqd, dkd, dp, dcd)


def _pool_counts(i, tm, rows, offset):
    t = i * tm - offset + lax.broadcasted_iota(jnp.int32, (rows, 1), 0)
    return [jnp.minimum(t + 1, w).astype(F32) for w in POOL_WINDOWS]


def _window_sums(window, forward):
    sums, s, step = [], window, 1
    for _ in POOL_WINDOWS:
        s = s + _shift_rows(s, -step if forward else step)
        sums.append(s)
        step *= 2
    return sums


def _pooled(window, counts):
    sums = _window_sums(window, forward=False)
    out = []
    for gi in range(POOL_GROUPS):
        sl = slice(gi * 128, (gi + 1) * 128)
        out.append(sums[gi][HALO:, sl] / counts[gi] - window[HALO:, sl])
    return out


def _mix_post(o, proj, gdn_norm, pool_w, pool_scale, name):
    T = o.shape[0]
    tm = _tile(T, 256)
    hb = tm // HALO

    def body(o_ref, z_ref, p_ref, ph_ref, gn_ref, pw_ref, ps_ref, out_ref):
        i = pl.program_id(0)
        for hh in range(GDN_HEADS):
            sl = slice(hh * HEAD_DIM, (hh + 1) * HEAD_DIM)
            oh, zh = o_ref[:, sl], z_ref[:, sl]
            ro = lax.rsqrt(jnp.mean(oh * oh, axis=-1, keepdims=True) + EPS)
            out_ref[:, sl] = (((oh * ro) * gn_ref[...]) * (zh * _sigmoid(zh))).astype(out_ref.dtype)
        halo = jnp.where(i == 0, 0.0, ph_ref[...])
        window = jnp.concatenate([halo, p_ref[...]], axis=0)
        pooled = _pooled(window, _pool_counts(i, tm, tm, 0))
        for gi in range(POOL_GROUPS):
            pm = _mdot(pooled[gi], pw_ref[gi], NN)
            out_ref[:, GDN_WIDTH + gi * 128:GDN_WIDTH + (gi + 1) * 128] = (
                pm * ps_ref[:, gi * 128:(gi + 1) * 128]).astype(out_ref.dtype)

    return pl.pallas_call(
        body, name=name, grid=(T // tm,),
        in_specs=[pl.BlockSpec((tm, GDN_WIDTH), lambda i: (i, 0)),
                  pl.BlockSpec((tm, GDN_WIDTH), lambda i: (i, COL_Z // GDN_WIDTH)),
                  pl.BlockSpec((tm, POOL_WIDTH), lambda i: (i, COL_P // POOL_WIDTH)),
                  pl.BlockSpec((HALO, POOL_WIDTH), lambda i: (jnp.maximum(i * hb - 1, 0), COL_P // POOL_WIDTH)),
                  pl.BlockSpec((1, HEAD_DIM), lambda i: (0, 0)),
                  pl.BlockSpec((POOL_GROUPS, 128, 128), lambda i: (0, 0, 0)),
                  pl.BlockSpec((1, POOL_WIDTH), lambda i: (0, 0))],
        out_specs=pl.BlockSpec((tm, GDN_WIDTH + POOL_WIDTH), lambda i: (i, 0)),
        out_shape=jax.ShapeDtypeStruct((T, GDN_WIDTH + POOL_WIDTH), MXU_DTYPE),
        compiler_params=_params("parallel"),
    )(o, proj, proj, proj, gdn_norm, pool_w, pool_scale)


def _mix_post_bwd(dmix, o, proj, gdn_norm, pool_w, pool_scale, name):
    T = o.shape[0]
    tm = _tile(T, 256)
    hb = tm // HALO
    n_tiles = T // tm

    def body(dg_ref, dpo_ref, dpo_next_ref, o_ref, z_ref, p_ref, ph_ref, gn_ref, pw_ref, ps_ref,
             do_ref, dzp_ref, dgn_ref, dpw_ref, dps_ref):
        i = pl.program_id(0)

        @pl.when(i == 0)
        def _():
            dgn_ref[...] = jnp.zeros_like(dgn_ref)
            dpw_ref[...] = jnp.zeros_like(dpw_ref)
            dps_ref[...] = jnp.zeros_like(dps_ref)

        gn = gn_ref[...]
        dgn = jnp.zeros((1, HEAD_DIM), F32)
        for hh in range(GDN_HEADS):
            sl = slice(hh * HEAD_DIM, (hh + 1) * HEAD_DIM)
            oh, zh, dy = o_ref[:, sl], z_ref[:, sl], dg_ref[:, sl]
            ro = lax.rsqrt(jnp.mean(oh * oh, axis=-1, keepdims=True) + EPS)
            on = oh * ro
            sig = _sigmoid(zh)
            sz = zh * sig
            dzp_ref[:, sl] = (dy * (on * gn) * (sig * (1.0 + zh * (1.0 - sig)))).astype(dzp_ref.dtype)
            dgn += jnp.sum(dy * on * sz, axis=0, keepdims=True)
            don = dy * gn * sz
            do_ref[:, sl] = ro * (don - on * jnp.mean(don * on, axis=-1, keepdims=True))
        dgn_ref[...] += dgn

        halo = jnp.where(i == 0, 0.0, ph_ref[...])
        window = jnp.concatenate([halo, p_ref[...]], axis=0)
        counts = _pool_counts(i, tm, tm + HALO, 0)
        pooled = _pooled(window, [cn[:tm] for cn in counts])
        nxt = jnp.where(i == n_tiles - 1, 0.0, dpo_next_ref[...])
        dpo_w = jnp.concatenate([dpo_ref[...], nxt], axis=0)
        ps = ps_ref[...]
        dps = []
        scaled = []
        for gi in range(POOL_GROUPS):
            sl = slice(gi * 128, (gi + 1) * 128)
            dpm = dpo_w[:, sl] * ps[:, sl]
            pm = _mdot(pooled[gi], pw_ref[gi], NN)
            dps.append(jnp.sum(dpo_w[:tm, sl] * pm, axis=0, keepdims=True))
            dpw_ref[gi] += _mdot(pooled[gi], dpm[:tm], TN)
            dpooled = _mdot(dpm, pw_ref[gi], NT)
            scaled.append((dpooled, dpooled / counts[gi]))
        dps_ref[...] += jnp.concatenate(dps, axis=1)
        lead = _window_sums(jnp.concatenate([sc for _, sc in scaled], axis=1), forward=True)
        for gi in range(POOL_GROUPS):
            sl = slice(gi * 128, (gi + 1) * 128)
            dzp_ref[:, GDN_WIDTH + gi * 128:GDN_WIDTH + (gi + 1) * 128] = (
                lead[gi][:tm, sl] - scaled[gi][0][:tm]).astype(dzp_ref.dtype)

    last_halo = T // HALO - 1
    return pl.pallas_call(
        body, name=name, grid=(n_tiles,),
        in_specs=[pl.BlockSpec((tm, GDN_WIDTH), lambda i: (i, 0)),
                  pl.BlockSpec((tm, POOL_WIDTH), lambda i: (i, 1)),
                  pl.BlockSpec((HALO, POOL_WIDTH), lambda i: (jnp.minimum((i + 1) * hb, last_halo), 1)),
                  pl.BlockSpec((tm, GDN_WIDTH), lambda i: (i, 0)),
                  pl.BlockSpec((tm, GDN_WIDTH), lambda i: (i, COL_Z // GDN_WIDTH)),
                  pl.BlockSpec((tm, POOL_WIDTH), lambda i: (i, COL_P // POOL_WIDTH)),
                  pl.BlockSpec((HALO, POOL_WIDTH), lambda i: (jnp.maximum(i * hb - 1, 0), COL_P // POOL_WIDTH)),
                  pl.BlockSpec((1, HEAD_DIM), lambda i: (0, 0)),
                  pl.BlockSpec((POOL_GROUPS, 128, 128), lambda i: (0, 0, 0)),
                  pl.BlockSpec((1, POOL_WIDTH), lambda i: (0, 0))],
        out_specs=[pl.BlockSpec((tm, GDN_WIDTH), lambda i: (i, 0)),
                   pl.BlockSpec((tm, GDN_WIDTH + POOL_WIDTH), lambda i: (i, 0)),
                   pl.BlockSpec((1, HEAD_DIM), lambda i: (0, 0)),
                   pl.BlockSpec((POOL_GROUPS, 128, 128), lambda i: (0, 0, 0)),
                   pl.BlockSpec((1, POOL_WIDTH), lambda i: (0, 0))],
        out_shape=[jax.ShapeDtypeStruct((T, GDN_WIDTH), F32),
                   jax.ShapeDtypeStruct((T, GDN_WIDTH + POOL_WIDTH), MXU_DTYPE),
                   jax.ShapeDtypeStruct((1, HEAD_DIM), F32),
                   jax.ShapeDtypeStruct((POOL_GROUPS, 128, 128), F32),
                   jax.ShapeDtypeStruct((1, POOL_WIDTH), F32)],
        compiler_params=_params("arbitrary"),
    )(dmix, dmix, dmix, o, proj, proj, proj, gdn_norm, pool_w, pool_scale)


def _gdn_prep_bwd(proj, conv_w, a_log_l, dt_bias_l, dqkv, dbg, dzp, name):
    T = proj.shape[0]
    tm = _tile(T, 256)
    hb = tm // 8
    n_tiles = T // tm
    last_halo = T // 8 - 1

    def body(cur_ref, before_ref, after_ref, ba_ref, w_ref, al_ref, dtb_ref, dq_ref, dq_after_ref, dbg_ref,
             dzp_ref, dproj_ref, dw_ref, dal_ref, ddtb_ref):
        i = pl.program_id(0)

        @pl.when(i == 0)
        def _():
            dw_ref[...] = jnp.zeros_like(dw_ref)
            dal_ref[...] = jnp.zeros_like(dal_ref)
            ddtb_ref[...] = jnp.zeros_like(ddtb_ref)

        last = i == n_tiles - 1
        w = w_ref[...]
        before = jnp.where(i == 0, 0.0, before_ref[...])
        after = jnp.where(last, 0.0, after_ref[...])
        window = jnp.concatenate([before, cur_ref[...], after], axis=0)
        y = _conv_act(window, w)
        sig = _sigmoid(y)
        act = y * sig
        dq_w = jnp.concatenate([jnp.zeros((8, QKV_WIDTH), F32), dq_ref[...],
                                jnp.where(last, 0.0, dq_after_ref[...])], axis=0)
        dact = []
        for hh in range(3 * GDN_HEADS):
            sl = slice(hh * HEAD_DIM, (hh + 1) * HEAD_DIM)
            blk, dblk = act[:, sl], dq_w[:, sl]
            if hh < 2 * GDN_HEADS:
                rn = lax.rsqrt(jnp.sum(blk * blk, axis=-1, keepdims=True) + EPS)
                unit = blk * rn
                if hh < GDN_HEADS:
                    dblk = dblk * (HEAD_DIM ** -0.5)
                dblk = rn * (dblk - unit * jnp.sum(dblk * unit, axis=-1, keepdims=True))
            dact.append(dblk)
        dy = jnp.concatenate(dact, axis=1) * (sig * (1.0 + y * (1.0 - sig)))
        dx = dy * w[CONV_K - 1:CONV_K, :]
        dws = [None] * CONV_K
        dws[CONV_K - 1] = jnp.sum(dy[8:8 + tm] * window[8:8 + tm], axis=0, keepdims=True)
        for j in range(CONV_K - 1):
            s = CONV_K - 1 - j
            dx += _shift_rows(dy, -s) * w[j:j + 1, :]
            dws[j] = jnp.sum(dy[8:8 + tm] * _shift_rows(window, s)[8:8 + tm], axis=0, keepdims=True)
        dw_ref[...] += jnp.concatenate(dws, axis=0)
        dproj_ref[:, :QKV_WIDTH] = dx[8:8 + tm].astype(dproj_ref.dtype)
        dproj_ref[:, COL_Z:COL_BA] = dzp_ref[...]

        ba = ba_ref[...]
        dbg_ = dbg_ref[...]
        lane = lax.broadcasted_iota(jnp.int32, ba.shape, 1)
        beta = _sigmoid(ba)
        pre = ba + dtb_ref[...]
        neg_a = -jnp.exp(al_ref[...])
        g = neg_a * _softplus(pre)
        is_g = (lane >= GDN_HEADS) & (lane < 2 * GDN_HEADS)
        da_raw = jnp.where(is_g, dbg_ * neg_a * _sigmoid(pre), 0.0)
        dba = jnp.where(lane < GDN_HEADS, dbg_ * beta * (1.0 - beta), da_raw)
        dproj_ref[:, COL_BA:] = dba.astype(dproj_ref.dtype)
        dal_ref[...] += jnp.sum(jnp.where(is_g, dbg_ * g, 0.0), axis=0, keepdims=True)
        ddtb_ref[...] += jnp.sum(da_raw, axis=0, keepdims=True)

    lane_vec = pl.BlockSpec((1, 128), lambda i: (0, 0))
    return pl.pallas_call(
        body, name=name, grid=(n_tiles,),
        in_specs=[pl.BlockSpec((tm, QKV_WIDTH), lambda i: (i, 0)),
                  pl.BlockSpec((8, QKV_WIDTH), lambda i: (jnp.maximum(i * hb - 1, 0), 0)),
                  pl.BlockSpec((8, QKV_WIDTH), lambda i: (jnp.minimum((i + 1) * hb, last_halo), 0)),
                  pl.BlockSpec((tm, 128), lambda i: (i, COL_BA // 128)),
                  pl.BlockSpec((CONV_K, QKV_WIDTH), lambda i: (0, 0)), lane_vec, lane_vec,
                  pl.BlockSpec((tm, QKV_WIDTH), lambda i: (i, 0)),
                  pl.BlockSpec((8, QKV_WIDTH), lambda i: (jnp.minimum((i + 1) * hb, last_halo), 0)),
                  pl.BlockSpec((tm, 128), lambda i: (i, 0)),
                  pl.BlockSpec((tm, GDN_WIDTH + POOL_WIDTH), lambda i: (i, 0))],
        out_specs=[pl.BlockSpec((tm, D_IN_PAD), lambda i: (i, 0)),
                   pl.BlockSpec((CONV_K, QKV_WIDTH), lambda i: (0, 0)), lane_vec, lane_vec],
        out_shape=[jax.ShapeDtypeStruct((T, D_IN_PAD), MXU_DTYPE),
                   jax.ShapeDtypeStruct((CONV_K, QKV_WIDTH), F32),
                   jax.ShapeDtypeStruct((1, 128), F32), jax.ShapeDtypeStruct((1, 128), F32)],
        compiler_params=_params("arbitrary"),
    )(proj, proj, proj, proj, conv_w, a_log_l, dt_bias_l, dqkv, dqkv, dbg, dzp)


def _mod_part(c_all, w_ada, b_part, name):
    def body(c_ref, w_ref, b_ref, out_ref):
        cc = c_ref[...]
        out_ref[...] = _mdot(cc * _sigmoid(cc), w_ref[...], NN) + b_ref[...]

    return pl.pallas_call(
        body, name=name, out_shape=jax.ShapeDtypeStruct((c_all.shape[0], w_ada.shape[1]), F32),
        compiler_params=_params(),
    )(c_all, w_ada, b_part)


def _w_ada_grad(c_all, dmod_part, name):
    def body(c_ref, d_ref, out_ref):
        cc = c_ref[...]
        out_ref[...] = _mdot(cc * _sigmoid(cc), d_ref[...], TN)

    return pl.pallas_call(
        body, name=name, out_shape=jax.ShapeDtypeStruct((c_all.shape[1], dmod_part.shape[1]), F32),
        compiler_params=_params(),
    )(c_all, dmod_part)


def _sum_parts(parts, name):
    _, R, C = parts.shape
    tr = max([t for t in range(16, min(R, 512) + 1, 16) if R % t == 0], default=R)

    def body(p_ref, out_ref):
        acc = p_ref[0].astype(F32)
        for s in range(1, N_DEV):
            acc += p_ref[s].astype(F32)
        out_ref[...] = acc

    return pl.pallas_call(
        body, name=name, grid=(R // tr,),
        in_specs=[pl.BlockSpec((N_DEV, tr, C), lambda i: (0, i, 0))],
        out_specs=pl.BlockSpec((tr, C), lambda i: (i, 0)),
        out_shape=jax.ShapeDtypeStruct((R, C), F32),
        compiler_params=_params("parallel"),
    )(parts)


def _adamw(w, g, m, v, name):
    R, C = w.shape
    tr = max([t for t in range(8, min(R, 512) + 1, 8) if R % t == 0], default=R)

    def body(w_ref, g_ref, m_ref, v_ref, d_ref, mo_ref, vo_ref):
        gg = g_ref[...]
        mm = ADAM_B1 * m_ref[...] + (1.0 - ADAM_B1) * gg
        vv = ADAM_B2 * v_ref[...] + (1.0 - ADAM_B2) * (gg * gg)
        m_hat = mm / (1.0 - ADAM_B1 ** ADAM_STEP)
        v_hat = vv / (1.0 - ADAM_B2 ** ADAM_STEP)
        d_ref[...] = -ADAM_LR * (m_hat / (jnp.sqrt(v_hat) + ADAM_EPS) + ADAM_WD * w_ref[...])
        mo_ref[...] = mm
        vo_ref[...] = vv

    spec = pl.BlockSpec((tr, C), lambda i: (i, 0))
    return pl.pallas_call(
        body, name=name, grid=(R // tr,),
        in_specs=[spec] * 4, out_specs=[spec] * 3,
        out_shape=[jax.ShapeDtypeStruct((R, C), F32)] * 3,
        compiler_params=_params("parallel"),
    )(w, g, m, v)


def _weight_grad(a, b, name, dep=None):
    return _matmul([(a, b)], TN, WIRE_DTYPE, name, tm=1408, tn=1024, tk=1024, dep=dep)


def _rows_of(flat, lanes=1024):
    flat = flat.reshape(-1)
    n = -(-flat.shape[0] // lanes) * lanes
    return jnp.pad(flat, (0, n - flat.shape[0])).reshape(n // lanes, lanes)


def _pad_rows(a, rows):
    return jnp.pad(a, ((0, rows - a.shape[0]), (0, 0)))


def kernel(x, c, w_ada, b_ada, norm_ffn1, ffn1_gate, ffn1_up, ffn1_down, norm_mix, w_in, conv_w, a_log, dt_bias, gdn_norm, pool_w, pool_scale, w_out, norm_ffn2, ffn2_gate, ffn2_up, ffn2_down, final_norm, loss_target, m_w_ada, m_b_ada, m_norm_ffn1, m_ffn1_gate, m_ffn1_up, m_ffn1_down, m_norm_mix, m_w_in, m_conv_w, m_a_log, m_dt_bias, m_gdn_norm, m_pool_w, m_pool_scale, m_w_out, m_norm_ffn2, m_ffn2_gate, m_ffn2_up, m_ffn2_down, m_final_norm, v_w_ada, v_b_ada, v_norm_ffn1, v_ffn1_gate, v_ffn1_up, v_ffn1_down, v_norm_mix, v_w_in, v_conv_w, v_a_log, v_dt_bias, v_gdn_norm, v_pool_w, v_pool_scale, v_w_out, v_norm_ffn2, v_ffn2_gate, v_ffn2_up, v_ffn2_down, v_final_norm):
    T, D = x.shape[1], x.shape[2]
    Fs = ffn1_gate.shape[2]
    Ws = w_in.shape[2]
    Ws_pad = -(-Ws // 16) * 16
    Os = w_out.shape[1]
    Ms = w_ada.shape[2]
    Cs = conv_w.shape[2]
    me = 4 * lax.axis_index("x") + 2 * lax.axis_index("y") + lax.axis_index("c")
    x0, target = x[0], loss_target[0]

    def wire(a):
        return a.astype(WIRE_DTYPE)

    def token(started):
        return started[4][:1, :1]

    def with_own(landed, own):
        return lax.dynamic_update_slice(landed, own[None], (me, 0, 0))

    def full(landed):
        return landed.reshape(-1, D).astype(MXU_DTYPE)

    no_dep = jnp.zeros((8, 128), F32)
    small = jnp.concatenate([_pad_rows(c, 8), _pad_rows(jnp.pad(conv_w[0], ((0, 0), (0, D - Cs))), 8)], axis=0)
    got, = _all_gather([small], "gather_small")
    c_all = got[:, 0, :]
    conv_full = jnp.transpose(got[:, 8:8 + CONV_K, :Cs], (1, 0, 2)).reshape(CONV_K, QKV_WIDTH)
    b_part = lax.dynamic_slice(b_ada, (0, me * Ms), (1, Ms))
    mod_parts, = _all_gather([_mod_part(c_all, w_ada[0], b_part, "mod_part")], "gather_mod")
    mod_all = jnp.transpose(mod_parts, (1, 0, 2)).reshape(N_DEV, N_MOD * D)
    mod = lax.dynamic_slice(mod_all, (me, 0), (1, N_MOD * D)).reshape(N_MOD, 1, D)
    sh1, sc1, gt1, sh2, sc2, gt2, sh3, sc3, gt3 = [mod[i] for i in range(N_MOD)]

    w1 = [wire(ffn1_gate[0].T), wire(ffn1_up[0].T), wire(ffn1_down[0])]
    w2 = [wire(_pad_rows(w_in[0].T, Ws_pad)), wire(w_out[0])]
    w3 = [wire(ffn2_gate[0].T), wire(ffn2_up[0].T), wire(ffn2_down[0])]
    off23 = [0, Ws_pad, Ws_pad + Os, Ws_pad + Os + Fs, Ws_pad + Os + 2 * Fs]
    w1_all = _all_gather(w1, "gather_w1", dep=mod_all)
    wg1_t, wu1_t, wd1 = [full(w) for w in w1_all]
    w2_sent = _exchange_start(w2, True, w1_all[0], "w2_start")
    w3_sent = _exchange_start(w3, True, w2_sent[4], "w3_start")

    lane_pad = lambda a: jnp.pad(a, ((0, 0), (GDN_HEADS, 128 - 2 * GDN_HEADS)))
    a_log_l, dt_bias_l = lane_pad(a_log), lane_pad(dt_bias)
    pool_w_m = pool_w[0].astype(MXU_DTYPE)

    g1, u1, a1, h1, y1, x1, h2 = _swiglu_fwd(
        x0, wg1_t, wu1_t, wd1, "ffn1_fwd", norm_in=(norm_ffn1, sc1 + token(w3_sent), sh1),
        resid_out=(x0, gt1, 0.5, norm_mix, sc2, sh2))
    w_in_all, wo_all = [with_own(z, own) for z, own in zip(_exchange_wait(w2_sent, h2, True, "w2_wait"), w2)]
    w_in_t = w_in_all[:, :Ws, :].reshape(-1, D).astype(MXU_DTYPE)
    wo = full(wo_all)
    w_in_re = jnp.concatenate([w_in_t[:COL_Z + GDN_WIDTH], w_in_t[D_IN - POOL_WIDTH:],
                               w_in_t[4 * GDN_WIDTH:4 * GDN_WIDTH + 2 * GDN_HEADS],
                               jnp.zeros((128 - 2 * GDN_HEADS, D), MXU_DTYPE)], axis=0)
    proj = _matmul([(h2, w_in_re)], NT, F32, "proj_in", tm=512, tn=D_IN_PAD, tk=D)
    qkv, bg = _gdn_prep(proj, conv_full, a_log_l, dt_bias_l, "gdn_prep")
    tinv, u_c, w_c, qd_c, kd_c, p_c, cd_c = _gdn_chunk_fwd(qkv, bg, "gdn_chunk_fwd")
    o, s_all, vn_c = _gdn_scan_fwd(u_c, w_c, qd_c, kd_c, p_c, cd_c, "gdn_scan_fwd")
    mix_in = _mix_post(o, proj, gdn_norm, pool_w_m, pool_scale, "mix_post")
    mixed, x2, h3 = _matmul_resid_norm_mod(mix_in, wo, x1, gt2, 1.0, norm_ffn2, sc3, sh3, "mix_out")
    wg2_t, wu2_t, wd2 = [full(with_own(z, own))
                         for z, own in zip(_exchange_wait(w3_sent, h3, True, "w3_wait"), w3)]
    g3, u3, a3, loss_row, d3, d_final, dy3, dgt3 = _swiglu_fwd(
        h3, wg2_t, wu2_t, wd2, "ffn2_fwd_loss", loss_out=(x2, gt3, final_norm.reshape(1, D), target))

    dg3, du3, d2, d_n3, dsc3, dsh3, dmixed, dgt2 = _swiglu_bwd(
        dy3, wd2, g3, u3, wg2_t, wu2_t, "ffn2_bwd_norm3_bwd", (x2, norm_ffn2, sc3), d3,
        produced_by=(mixed, gt2, 1.0))
    d_wd2 = _weight_grad(a3, dy3, "ffn2_dwd")
    d_wg2 = _weight_grad(dg3, h3, "ffn2_dwg")
    d_wu2 = _weight_grad(du3, h3, "ffn2_dwu")
    dmix_in = _matmul([(dmixed, wo)], NT, F32, "mix_out_bwd", tm=512, tn=GDN_WIDTH + POOL_WIDTH, tk=D)
    d_wo = _matmul([(mix_in, dmixed)], TN, WIRE_DTYPE, "mix_dwo", tm=GDN_WIDTH + POOL_WIDTH, tn=D, tk=1024)
    do, dzp, d_gn, d_pw, d_ps = _mix_post_bwd(dmix_in, o, proj, gdn_norm, pool_w_m, pool_scale, "mix_post_bwd")
    dvn, dw_c, dqd, dkd, dp_c, dcd = _gdn_scan_bwd(do, w_c, qd_c, kd_c, p_c, cd_c, s_all, vn_c, "gdn_scan_bwd")
    dqkv, dbg = _gdn_chunk_bwd(qkv, bg, tinv, u_c, w_c, dvn, dw_c, dqd, dkd, dp_c, dcd, "gdn_chunk_bwd")
    dproj, d_conv, d_al, d_dtb = _gdn_prep_bwd(proj, conv_full, a_log_l, dt_bias_l, dqkv, dbg, dzp, "gdn_prep_bwd")
    d_win_re = _matmul([(dproj, h2)], TN, WIRE_DTYPE, "proj_in_dw", tm=D_IN_PAD, tn=D, tk=1024)
    d_win_t = jnp.concatenate([d_win_re[:COL_Z + GDN_WIDTH], d_win_re[COL_BA:COL_BA + 2 * GDN_HEADS],
                               d_win_re[COL_P:COL_P + POOL_WIDTH]], axis=0)
    d_win_blocks = jnp.pad(d_win_t.reshape(N_DEV, Ws, D), ((0, 0), (0, Ws_pad - Ws), (0, 0)))
    parts23 = jnp.concatenate(
        [wire(d_win_blocks), wire(d_wo.reshape(N_DEV, Os, D)), wire(d_wg2.reshape(N_DEV, Fs, D)),
         wire(d_wu2.reshape(N_DEV, Fs, D)), wire(d_wd2.reshape(N_DEV, Fs, D))], axis=1)
    own23 = lax.dynamic_index_in_dim(parts23, me, 0, keepdims=False)
    g23_sent = _exchange_start([parts23], False, no_dep, "g23_start")
    d1, d_n2, dsc2, dsh2, dy1, dgt1 = _norm_bwd((dproj, w_in_re), x1, norm_mix, sc2 + token(g23_sent), d2,
                                                "proj_in_bwd_norm2_bwd", produced_by=(y1, gt1, 0.5))
    dg1, du1, grad_x, d_n1, dsc1, dsh1 = _swiglu_bwd(
        dy1, wd1, g1, u1, wg1_t, wu1_t, "ffn1_bwd_norm1_bwd", (x0, norm_ffn1, sc1), d1)

    dmod = jnp.concatenate([dsh1, dsc1, dgt1, dsh2, dsc2, dgt2, dsh3, dsc3, dgt3], axis=0)
    small_rows = [dmod.reshape(-1), d_n1[0], d_n2[0], d_n3[0], d_final[0], d_gn[0], d_ps[0],
                  d_al[0, GDN_HEADS:2 * GDN_HEADS], d_dtb[0, GDN_HEADS:2 * GDN_HEADS], loss_row[0, :1],
                  d_conv.reshape(-1), d_pw.reshape(-1)]
    lanes = 1024
    small_rows = [_rows_of(r, lanes) for r in small_rows]
    n_rows = [r.shape[0] for r in small_rows]
    row_off = [sum(n_rows[:i]) for i in range(len(n_rows))]
    total = -(-sum(n_rows) // 8) * 8
    slab = _pad_rows(jnp.concatenate(small_rows, axis=0), total)
    slab_all, = _all_gather([slab], "gather_small_grads")
    summed = _sum_parts(slab_all, "sum_small_grads")

    def piece(idx, n):
        return summed[row_off[idx]:row_off[idx] + n_rows[idx]].reshape(-1)[:n]

    g_b_ada = piece(0, N_MOD * D).reshape(1, N_MOD * D)
    g_n1, g_n2, g_n3 = piece(1, D).reshape(1, D), piece(2, D).reshape(1, D), piece(3, D).reshape(1, D)
    g_final = piece(4, D)
    g_gn = piece(5, HEAD_DIM).reshape(1, HEAD_DIM)
    g_ps = piece(6, POOL_WIDTH).reshape(1, POOL_WIDTH)
    g_al = piece(7, GDN_HEADS).reshape(1, GDN_HEADS)
    g_dtb = piece(8, GDN_HEADS).reshape(1, GDN_HEADS)
    loss = piece(9, 1)[0]
    g_conv = lax.dynamic_slice(piece(10, CONV_K * QKV_WIDTH).reshape(1, CONV_K, QKV_WIDTH), (0, 0, me * Cs),
                               (1, CONV_K, Cs))
    g_pw = piece(11, POOL_GROUPS * 128 * 128).reshape(1, POOL_GROUPS, 128, 128)

    dmod_all = slab_all[:, row_off[0]:row_off[0] + n_rows[0], :].reshape(N_DEV, -1)[:, :N_MOD * D]
    g_w_ada = _w_ada_grad(c_all, lax.dynamic_slice(dmod_all, (0, me * Ms), (N_DEV, Ms)), "w_ada_grad")[None]

    def send_ffn1(a, b, which, dep):
        parts = _weight_grad(a, b, f"ffn1_{which}", dep=dep).reshape(N_DEV, Fs, D)
        own = lax.dynamic_index_in_dim(parts, me, 0, keepdims=False)
        return _exchange_start([parts], False, no_dep, f"g1_{which}_start"), own

    g1_wg, own_wg = send_ffn1(dg1, h1, "dwg", summed)
    g1_wu, own_wu = send_ffn1(du1, h1, "dwu", g1_wg[4])
    g1_wd, own_wd = send_ffn1(a1, dy1, "dwd", g1_wu[4])

    big23 = _sum_parts(with_own(_exchange_wait(g23_sent, g1_wd[4], False, "g23_wait")[0], own23), "sum_grads23")
    g_rows = dict(w_in=big23[:Ws], w_out=big23[off23[1]:off23[1] + Os],
                  ffn2_gate=big23[off23[2]:off23[2] + Fs], ffn2_up=big23[off23[3]:off23[3] + Fs],
                  ffn2_down=big23[off23[4]:off23[4] + Fs])
    column_sharded = ("w_in", "ffn1_gate", "ffn1_up", "ffn2_gate", "ffn2_up")

    names = ["w_ada", "b_ada", "norm_ffn1", "ffn1_gate", "ffn1_up", "ffn1_down", "norm_mix", "w_in", "conv_w",
             "a_log", "dt_bias", "gdn_norm", "pool_w", "pool_scale", "w_out", "norm_ffn2", "ffn2_gate", "ffn2_up",
             "ffn2_down", "final_norm"]
    weights = dict(zip(names, [w_ada, b_ada, norm_ffn1, ffn1_gate, ffn1_up, ffn1_down, norm_mix, w_in, conv_w,
                               a_log, dt_bias, gdn_norm, pool_w, pool_scale, w_out, norm_ffn2, ffn2_gate, ffn2_up,
                               ffn2_down, final_norm]))
    ms = dict(zip(names, [m_w_ada, m_b_ada, m_norm_ffn1, m_ffn1_gate, m_ffn1_up, m_ffn1_down, m_norm_mix, m_w_in,
                          m_conv_w, m_a_log, m_dt_bias, m_gdn_norm, m_pool_w, m_pool_scale, m_w_out, m_norm_ffn2,
                          m_ffn2_gate, m_ffn2_up, m_ffn2_down, m_final_norm]))
    vs = dict(zip(names, [v_w_ada, v_b_ada, v_norm_ffn1, v_ffn1_gate, v_ffn1_up, v_ffn1_down, v_norm_mix, v_w_in,
                          v_conv_w, v_a_log, v_dt_bias, v_gdn_norm, v_pool_w, v_pool_scale, v_w_out, v_norm_ffn2,
                          v_ffn2_gate, v_ffn2_up, v_ffn2_down, v_final_norm]))
    grads = dict(w_ada=g_w_ada, b_ada=g_b_ada, norm_ffn1=g_n1, norm_mix=g_n2, conv_w=g_conv,
                 a_log=g_al, dt_bias=g_dtb, gdn_norm=g_gn, pool_w=g_pw, pool_scale=g_ps,
                 norm_ffn2=g_n3, final_norm=g_final)
    delta, new_m, new_v = {}, {}, {}

    def adamw_big(n):
        if n in column_sharded:
            view, back = (lambda a: a[0].T), (lambda a: a.T[None])
        else:
            view, back = (lambda a: a[0]), (lambda a: a[None])
        g = g_rows[n] if n in g_rows else view(grads[n])
        d_, m_, v_ = _adamw(view(weights[n]), g, view(ms[n]), view(vs[n]), f"adamw_{n}")
        grads[n], delta[n], new_m[n], new_v[n] = back(g), back(d_), back(m_), back(v_)

    early = ["w_ada", "w_in", "w_out", "ffn2_gate", "ffn2_up", "ffn2_down"]
    late = ["ffn1_gate", "ffn1_up", "ffn1_down"]
    for n in early:
        adamw_big(n)
    done = sum(delta[n][0, :1, :1] for n in early)

    def arrived(started, own, which):
        landed, = _exchange_wait(started, done, False, f"g1_{which}_wait")
        return _sum_parts(with_own(landed, own), f"sum_{which}")

    g_rows["ffn1_gate"] = arrived(g1_wg, own_wg, "dwg")
    g_rows["ffn1_up"] = arrived(g1_wu, own_wu, "dwu")
    g_rows["ffn1_down"] = arrived(g1_wd, own_wd, "dwd")
    for n in late:
        adamw_big(n)
    small_names = [n for n in names if n not in early + late]
    pack = lambda src: jnp.concatenate([_rows_of(src[n]) for n in small_names], axis=0)
    p_rows = [_rows_of(weights[n]).shape[0] for n in small_names]
    p_total = -(-sum(p_rows) // 8) * 8
    packed = [_pad_rows(pack(src), p_total) for src in (weights, grads, ms, vs)]
    d_s, m_s, v_s = _adamw(*packed, "adamw_small")
    off = 0
    for n, r in zip(small_names, p_rows):
        shp = weights[n].shape
        size = weights[n].size
        for dst, src in ((delta, d_s), (new_m, m_s), (new_v, v_s)):
            dst[n] = src[off:off + r].reshape(-1)[:size].reshape(shp)
        off += r

    return (loss, grad_x[None], *[grads[n] for n in names], *[delta[n] for n in names],
            *[new_m[n] for n in names], *[new_v[n] for n in names])
```

```python
import functools

import jax
import jax.numpy as jnp
from jax import lax
from jax.experimental import pallas as pl
from jax.experimental.pallas import tpu as pltpu

F32 = jnp.float32
MXU_DTYPE = jnp.bfloat16
WIRE_DTYPE = jnp.bfloat16
EPS = 1e-6
N_DEV = 8
GDN_HEADS = 4
HEAD_DIM = 128
GDN_WIDTH = GDN_HEADS * HEAD_DIM
POOL_WINDOWS = (2, 4, 8, 16)
POOL_GROUPS = len(POOL_WINDOWS)
POOL_WIDTH = 512
CONV_K = 4
CHUNK = 64
QKV_WIDTH = 3 * GDN_WIDTH
D_IN = 4 * GDN_WIDTH + 2 * GDN_HEADS + POOL_WIDTH
D_IN_PAD = 4 * GDN_WIDTH + POOL_WIDTH + 128
COL_Z = QKV_WIDTH
COL_P = 4 * GDN_WIDTH
COL_BA = 4 * GDN_WIDTH + POOL_WIDTH
N_MOD = 9
HALO = 16
VMEM_LIMIT = 56 * 1024 * 1024
ADAM_LR, ADAM_B1, ADAM_B2, ADAM_EPS, ADAM_WD, ADAM_STEP = 0.001, 0.9, 0.999, 1e-08, 0.01, 10
FFN_TOKEN_TILE = 256
FFN_HIDDEN_TILE = 1408
CHUNKS_PER_STEP = 4
SCAN_CHUNKS_PER_STEP = 4

NT = (((1,), (1,)), ((), ()))
NN = (((1,), (0,)), ((), ()))
TN = (((0,), (0,)), ((), ()))


def _params(*sem):
    return pltpu.CompilerParams(dimension_semantics=tuple(sem), vmem_limit_bytes=VMEM_LIMIT)


def _dot(a, b, dims):
    return lax.dot_general(a, b, dims, preferred_element_type=F32)


def _mdot(a, b, dims):
    return _dot(a.astype(MXU_DTYPE), b.astype(MXU_DTYPE), dims)


def _split(a):
    hi = a.astype(jnp.bfloat16)
    return hi, (a - hi.astype(F32)).astype(jnp.bfloat16)


def _dot3(a, b, dims):
    (ah, al), (bh, bl) = a, b
    return (_dot(al, bh, dims) + _dot(ah, bl, dims)) + _dot(ah, bh, dims)


def _sigmoid(v):
    return 0.5 * jnp.tanh(0.5 * v) + 0.5


def _softplus(v):
    return jnp.maximum(v, 0.0) + jnp.log(1.0 + jnp.exp(-jnp.abs(v)))


def _shift_rows(v, s):
    n = v.shape[0]
    s = s % n
    return v if s == 0 else pltpu.roll(v, s, 0)


def _tile(n, want):
    t = min(n, want)
    while n % t:
        t //= 2
    return t


def _all_gather(blocks, name, dep=None):
    n = len(blocks)

    def body(*refs):
        x_refs, out_refs = refs[:n], refs[-3 - n:-3]
        send_sems, recv_sems, local_sems = refs[-3:]
        x, y, c = lax.axis_index("x"), lax.axis_index("y"), lax.axis_index("c")
        me, sibling = (x, y, c), (x, y, 1 - c)
        chips = [(1 - x, y), (x, 1 - y), (1 - x, 1 - y)]

        def copy(a, k, blk, to, own=False):
            rows = out_refs[a].at[4 * blk[0] + 2 * blk[1] + blk[2]]
            return pltpu.make_async_remote_copy(
                src_ref=x_refs[a] if own else rows, dst_ref=rows,
                send_sem=send_sems.at[7 * a + k], recv_sem=recv_sems.at[7 * a + k],
                device_id=to, device_id_type=pl.DeviceIdType.MESH)

        mine = [pltpu.make_async_copy(x_refs[a], out_refs[a].at[4 * x + 2 * y + c], local_sems.at[a])
                for a in range(n)]
        for cp in mine:
            cp.start()
        sent = []
        for a in range(n):
            sent.append(copy(a, 0, me, sibling, own=True))
            sent += [copy(a, 1 + j, me, (*chip, c), own=True) for j, chip in enumerate(chips)]
        for cp in sent:
            cp.start()
        for a in range(n):
            for j, chip in enumerate(chips):
                copy(a, 1 + j, (*chip, c), me).wait_recv()
                passed = copy(a, 4 + j, (*chip, c), sibling)
                passed.start()
                sent.append(passed)
        for a in range(n):
            copy(a, 0, sibling, me).wait_recv()
            for j, chip in enumerate(chips):
                copy(a, 4 + j, (*chip, 1 - c), me).wait_recv()
        for cp in sent:
            cp.wait_send()
        for cp in mine:
            cp.wait()

    hbm = pl.BlockSpec(memory_space=pltpu.HBM)
    return pl.pallas_call(
        body, name=name,
        out_shape=[jax.ShapeDtypeStruct((N_DEV,) + b.shape, b.dtype) for b in blocks],
        in_specs=[hbm] * n + [pl.BlockSpec(memory_space=pl.ANY)] * (dep is not None),
        out_specs=[hbm] * n,
        scratch_shapes=[pltpu.SemaphoreType.DMA((7 * n,)), pltpu.SemaphoreType.DMA((7 * n,)),
                        pltpu.SemaphoreType.DMA((n,))],
    )(*(list(blocks) + ([] if dep is None else [dep])))


_HBM = pl.BlockSpec(memory_space=pltpu.HBM)
_SEM = pl.BlockSpec(memory_space=pltpu.SEMAPHORE)
_ANY = pl.BlockSpec(memory_space=pl.ANY)
_EFFECT = pltpu.SideEffectType.DATAFLOW_SIDE_EFFECTING
_FLIPS = [(0, 0, 1), (0, 1, 0), (0, 1, 1), (1, 0, 0), (1, 0, 1), (1, 1, 0), (1, 1, 1)]


def _peers():
    x, y, c = lax.axis_index("x"), lax.axis_index("y"), lax.axis_index("c")
    return 4 * x + 2 * y + c, [(1 - x if fx else x, 1 - y if fy else y, 1 - c if fc else c)
                               for fx, fy, fc in _FLIPS]


def _exchange_start(srcs, gather, dep, name):
    n = len(srcs)
    lands = [(N_DEV,) + tuple(s.shape if gather else s.shape[1:]) for s in srcs]

    def body(*refs):
        src_refs, land_refs = refs[:n], refs[n:2 * n]
        send_sems, recv_sems = refs[2 * n + 1], refs[2 * n + 2]
        token = refs[-1]
        me, peers = _peers()
        for a in range(n):
            for k, (px, py, pc) in enumerate(peers):
                pltpu.make_async_remote_copy(
                    src_ref=src_refs[a] if gather else src_refs[a].at[4 * px + 2 * py + pc],
                    dst_ref=land_refs[a].at[me],
                    send_sem=send_sems.at[7 * a + k], recv_sem=recv_sems.at[7 * a + k],
                    device_id=(px, py, pc), device_id_type=pl.DeviceIdType.MESH).start()
        token[...] = jnp.zeros_like(token)

    srcs = [pltpu.with_memory_space_constraint(s, pltpu.HBM) for s in srcs]
    empties = [pltpu.with_memory_space_constraint(lax.empty(shape, s.dtype), pltpu.HBM)
               for shape, s in zip(lands, srcs)]
    out = pl.pallas_call(
        body, name=name,
        out_shape=(pltpu.SemaphoreType.DMA((7 * n,)), pltpu.SemaphoreType.DMA((7 * n,)),
                   *[pltpu.HBM(shape, s.dtype) for shape, s in zip(lands, srcs)],
                   jax.ShapeDtypeStruct((8, 128), F32)),
        in_specs=(*[_HBM] * (2 * n), _ANY),
        out_specs=(_SEM, _SEM, *[_HBM] * n, pl.BlockSpec(memory_space=pltpu.VMEM)),
        input_output_aliases={n + a: 2 + a for a in range(n)},
        compiler_params=pltpu.CompilerParams(has_side_effects=_EFFECT),
    )(*srcs, *empties, dep)
    return out[0], out[1], srcs, list(out[2:2 + n]), out[-1]


def _exchange_wait(started, after, gather, name):
    send_sems, recv_sems, srcs, lands, _ = started
    n = len(srcs)

    def body(*refs):
        src_refs, land_refs = refs[:n], refs[n:2 * n]
        send_sems, recv_sems = refs[2 * n], refs[2 * n + 1]
        _, peers = _peers()
        for a in range(n):
            for k, peer in enumerate(peers):
                copy = pltpu.make_async_remote_copy(
                    src_ref=src_refs[a] if gather else src_refs[a].at[0], dst_ref=land_refs[a].at[0],
                    send_sem=send_sems.at[7 * a + k], recv_sem=recv_sems.at[7 * a + k],
                    device_id=peer, device_id_type=pl.DeviceIdType.MESH)
                copy.wait_send()
                copy.wait_recv()

    out = pl.pallas_call(
        body, name=name,
        out_shape=[pltpu.HBM(z.shape, z.dtype) for z in lands],
        in_specs=(*[_HBM] * (2 * n), _SEM, _SEM, _ANY), out_specs=[_HBM] * n,
        input_output_aliases={n + a: a for a in range(n)},
        compiler_params=pltpu.CompilerParams(has_side_effects=_EFFECT),
    )(*srcs, *lands, send_sems, recv_sems, after)
    return list(out)


def _matmul(pairs, dims, out_dtype, name, tm=512, tn=512, tk=512, dep=None):
    a0, b0 = pairs[0]
    if dims == TN:
        K, M = a0.shape
    else:
        M, K = a0.shape
    N = b0.shape[0] if dims == NT else b0.shape[1]
    tm, tn, tk = _tile(M, tm), _tile(N, tn), _tile(K, tk)
    nk = K // tk
    n_pairs = len(pairs)
    n_in = 2 * n_pairs + (dep is not None)

    def body(*refs):
        out_ref = refs[n_in]

        def product():
            total = _dot(refs[0][...], refs[1][...], dims)
            for p in range(1, n_pairs):
                total += _dot(refs[2 * p][...], refs[2 * p + 1][...], dims)
            return total

        if nk == 1:
            out_ref[...] = product().astype(out_ref.dtype)
            return
        acc_ref = refs[n_in + 1]
        k = pl.program_id(2)

        @pl.when(k == 0)
        def _():
            acc_ref[...] = product()

        @pl.when((k > 0) & (k < nk - 1))
        def _():
            acc_ref[...] += product()

        @pl.when(k == nk - 1)
        def _():
            out_ref[...] = (acc_ref[...] + product()).astype(out_ref.dtype)

    if dims == TN:
        a_spec = pl.BlockSpec((tk, tm), lambda i, j, k: (k, i))
    else:
        a_spec = pl.BlockSpec((tm, tk), lambda i, j, k: (i, k))
    if dims == NT:
        b_spec = pl.BlockSpec((tn, tk), lambda i, j, k: (j, k))
    else:
        b_spec = pl.BlockSpec((tk, tn), lambda i, j, k: (k, j))
    args, specs = [], []
    for a, b in pairs:
        args += [a, b]
        specs += [a_spec, b_spec]
    if dep is not None:
        args.append(dep)
        specs.append(_ANY)
    return pl.pallas_call(
        body, name=name, grid=(M // tm, N // tn, nk),
        in_specs=specs, out_specs=pl.BlockSpec((tm, tn), lambda i, j, k: (i, j)),
        out_shape=jax.ShapeDtypeStruct((M, N), out_dtype),
        scratch_shapes=[pltpu.VMEM((tm, tn), F32)] * (nk > 1),
        compiler_params=_params("parallel", "parallel", "arbitrary"),
    )(*args)


def _vec_spec(d):
    return pl.BlockSpec((1, d), lambda i: (0, 0))


def _matmul_resid_norm_mod(a, b, x, gate, coef, nw, scale, shift, name):
    T, K = a.shape
    D = b.shape[1]
    tm = _tile(T, 512)

    def body(a_ref, b_ref, x_ref, g_ref, nw_ref, sc_ref, sh_ref, y_ref, xo_ref, h_ref):
        y = _dot(a_ref[...], b_ref[...], NN)
        y_ref[...] = y
        xf = x_ref[...] + (coef * g_ref[...]) * y
        xo_ref[...] = xf
        r = lax.rsqrt(jnp.mean(xf * xf, axis=-1, keepdims=True) + EPS)
        h_ref[...] = ((xf * r) * nw_ref[...] * (1.0 + sc_ref[...]) + sh_ref[...]).astype(h_ref.dtype)

    row = pl.BlockSpec((tm, D), lambda i: (i, 0))
    vec = _vec_spec(D)
    return pl.pallas_call(
        body, name=name, grid=(T // tm,),
        in_specs=[pl.BlockSpec((tm, K), lambda i: (i, 0)), _resident((K, D)), row, vec, vec, vec, vec],
        out_specs=[row, row, row],
        out_shape=[jax.ShapeDtypeStruct((T, D), F32), jax.ShapeDtypeStruct((T, D), F32),
                   jax.ShapeDtypeStruct((T, D), MXU_DTYPE)],
        compiler_params=_params("parallel"),
    )(a, b, x, gate, nw, scale, shift)


def _norm_bwd(dh, x, nw, scale, dres, name, produced_by=None):
    T, D = x.shape
    tm = _tile(T, 512)

    def body(*refs):
        if isinstance(dh, tuple):
            dh_value = _dot(refs[0][...], refs[1][...], NN)
            refs = refs[1:]
        else:
            dh_value = refs[0][...]
        _, x_ref, nw_ref, sc_ref, dr_ref = refs[:5]
        n_in = 5 if produced_by is None else 7
        dx_ref, dnw_ref, dsc_ref, dsh_ref = refs[n_in:n_in + 4]

        @pl.when(pl.program_id(0) == 0)
        def _():
            dnw_ref[...] = jnp.zeros_like(dnw_ref)
            dsc_ref[...] = jnp.zeros_like(dsc_ref)
            dsh_ref[...] = jnp.zeros_like(dsh_ref)
            if produced_by is not None:
                refs[n_in + 5][...] = jnp.zeros_like(refs[n_in + 5])

        xf, dh_ = x_ref[...], dh_value
        r = lax.rsqrt(jnp.mean(xf * xf, axis=-1, keepdims=True) + EPS)
        xn = xf * r
        one_sc = 1.0 + sc_ref[...]
        dsh_ref[...] += jnp.sum(dh_, axis=0, keepdims=True)
        t = dh_ * xn
        dsc_ref[...] += jnp.sum(t, axis=0, keepdims=True) * nw_ref[...]
        dnw_ref[...] += jnp.sum(t, axis=0, keepdims=True) * one_sc
        dxn = dh_ * (nw_ref[...] * one_sc)
        dx = dr_ref[...] + r * (dxn - xn * jnp.mean(dxn * xn, axis=-1, keepdims=True))
        dx_ref[...] = dx
        if produced_by is not None:
            y_ref, g_ref, dy_ref, dg_ref = refs[5], refs[6], refs[n_in + 4], refs[n_in + 5]
            dy_ref[...] = ((produced_by[2] * g_ref[...]) * dx).astype(dy_ref.dtype)
            dg_ref[...] += produced_by[2] * jnp.sum(dx * y_ref[...], axis=0, keepdims=True)

    row = pl.BlockSpec((tm, D), lambda i: (i, 0))
    vec = _vec_spec(D)
    vec_out = jax.ShapeDtypeStruct((1, D), F32)
    if isinstance(dh, tuple):
        k_dim = dh[0].shape[1]
        args, in_specs = [dh[0], dh[1]], [pl.BlockSpec((tm, k_dim), lambda i: (i, 0)), _resident((k_dim, D))]
    else:
        args, in_specs = [dh], [row]
    args, in_specs = args + [x, nw, scale, dres], in_specs + [row, vec, vec, row]
    out_specs, out_shape = [row, vec, vec, vec], [jax.ShapeDtypeStruct((T, D), F32), vec_out, vec_out, vec_out]
    if produced_by is not None:
        args += [produced_by[0], produced_by[1]]
        in_specs += [row, vec]
        out_specs += [row, vec]
        out_shape += [jax.ShapeDtypeStruct((T, D), MXU_DTYPE), vec_out]
    return pl.pallas_call(
        body, name=name, grid=(T // tm,),
        in_specs=in_specs, out_specs=out_specs, out_shape=out_shape,
        compiler_params=_params("arbitrary"),
    )(*args)


def _rms(xf):
    r = lax.rsqrt(jnp.mean(xf * xf, axis=-1, keepdims=True) + EPS)
    return r, xf * r


def _norm_bwd_math(dh, xf, nw, sc):
    r, xn = _rms(xf)
    t = dh * xn
    dxn = dh * (nw * (1.0 + sc))
    dx = r * (dxn - xn * jnp.mean(dxn * xn, axis=-1, keepdims=True))
    return dx, jnp.sum(dh, axis=0, keepdims=True), jnp.sum(t, axis=0, keepdims=True)


def _swiglu_fwd(h, wg_t, wu_t, wd, name, norm_in=None, resid_out=None, loss_out=None):
    T, D = h.shape
    Fdim = wd.shape[0]
    tm, tf = _tile(T, FFN_TOKEN_TILE), _tile(Fdim, FFN_HIDDEN_TILE)
    row = pl.BlockSpec((tm, D), lambda i: (i, 0))
    frow = pl.BlockSpec((tm, Fdim), lambda i: (i, 0))
    vec, wres = _vec_spec(D), _resident((Fdim, D))
    f32_row, mxu_row = jax.ShapeDtypeStruct((T, D), F32), jax.ShapeDtypeStruct((T, D), MXU_DTYPE)
    vec_out = jax.ShapeDtypeStruct((1, D), F32)
    args, in_specs = [h, wg_t, wu_t, wd], [row, wres, wres, wres]
    out_shape, out_specs = [jax.ShapeDtypeStruct((T, Fdim), MXU_DTYPE)] * 3, [frow] * 3
    if norm_in is not None:
        args += list(norm_in)
        in_specs += [vec] * 3
        out_shape, out_specs = out_shape + [mxu_row], out_specs + [row]
    if resid_out is not None:
        x, gate, coef, nw, sc, sh = resid_out
        args += [x, gate, nw, sc, sh]
        in_specs += [row, vec, vec, vec, vec]
        out_shape, out_specs = out_shape + [f32_row, f32_row, mxu_row], out_specs + [row, row, row]
    if loss_out is not None:
        args += list(loss_out)
        in_specs += [row, vec, vec, row]
        out_shape += [jax.ShapeDtypeStruct((1, 128), F32), f32_row, vec_out, mxu_row, vec_out]
        out_specs += [pl.BlockSpec((1, 128), lambda i: (0, 0)), row, vec, row, vec]

    def body(*refs):
        it = iter(refs)
        h_ref, wg_ref, wu_ref, wd_ref = next(it), next(it), next(it), next(it)
        norm_refs = [next(it) for _ in range(3)] if norm_in is not None else None
        resid_refs = [next(it) for _ in range(5)] if resid_out is not None else None
        loss_refs = [next(it) for _ in range(4)] if loss_out is not None else None
        g_ref, u_ref, a_ref = next(it), next(it), next(it)
        if norm_in is not None:
            nw_ref, sc_ref, sh_ref = norm_refs
            _, xn = _rms(h_ref[...])
            hh = (xn * nw_ref[...] * (1.0 + sc_ref[...]) + sh_ref[...]).astype(MXU_DTYPE)
            next(it)[...] = hh
        else:
            hh = h_ref[...]
        y = None
        for k in range(Fdim // tf):
            ks = slice(k * tf, (k + 1) * tf)
            g = _dot(hh, wg_ref[ks, :], NT)
            u = _dot(hh, wu_ref[ks, :], NT)
            a = ((g * _sigmoid(g)) * u).astype(a_ref.dtype)
            g_ref[:, ks] = g.astype(g_ref.dtype)
            u_ref[:, ks] = u.astype(u_ref.dtype)
            a_ref[:, ks] = a
            part = _dot(a, wd_ref[ks, :], NN)
            y = part if y is None else y + part
        if resid_out is not None:
            x_ref, gt_ref, nw_ref, sc_ref, sh_ref = resid_refs
            y_ref, xo_ref, hn_ref = next(it), next(it), next(it)
            y_ref[...] = y
            xf = x_ref[...] + (resid_out[2] * gt_ref[...]) * y
            xo_ref[...] = xf
            _, xn = _rms(xf)
            hn_ref[...] = (xn * nw_ref[...] * (1.0 + sc_ref[...]) + sh_ref[...]).astype(hn_ref.dtype)
        if loss_out is not None:
            x_ref, gt_ref, fw_ref, t_ref = loss_refs
            loss_ref, dx_ref, dfw_ref, dy_ref, dg_ref = [next(it) for _ in range(5)]

            @pl.when(pl.program_id(0) == 0)
            def _():
                loss_ref[...] = jnp.zeros_like(loss_ref)
                dfw_ref[...] = jnp.zeros_like(dfw_ref)
                dg_ref[...] = jnp.zeros_like(dg_ref)

            r, xn = _rms(x_ref[...] + (0.5 * gt_ref[...]) * y)
            err = xn * fw_ref[...] - t_ref[...]
            per_tok = jnp.mean(err * err, axis=-1, keepdims=True)
            loss_ref[...] += 0.5 * jnp.sum(per_tok, axis=0, keepdims=True)
            d_out = err * (1.0 / D)
            dfw_ref[...] += jnp.sum(d_out * xn, axis=0, keepdims=True)
            dxn = d_out * fw_ref[...]
            dx = r * (dxn - xn * jnp.mean(dxn * xn, axis=-1, keepdims=True))
            dx_ref[...] = dx
            dy_ref[...] = ((0.5 * gt_ref[...]) * dx).astype(dy_ref.dtype)
            dg_ref[...] += 0.5 * jnp.sum(dx * y, axis=0, keepdims=True)

    return pl.pallas_call(
        body, name=name, grid=(T // tm,), in_specs=in_specs, out_specs=out_specs, out_shape=out_shape,
        compiler_params=_params("arbitrary" if loss_out is not None else "parallel"),
    )(*args)


def _swiglu_bwd(dy, wd, g, u, wg_t, wu_t, name, norm_in, dres, produced_by=None):
    T, D = dy.shape
    Fdim = wd.shape[0]
    tm, tf = _tile(T, FFN_TOKEN_TILE), _tile(Fdim, FFN_HIDDEN_TILE)
    row = pl.BlockSpec((tm, D), lambda i: (i, 0))
    frow = pl.BlockSpec((tm, Fdim), lambda i: (i, 0))
    vec, wres = _vec_spec(D), _resident((Fdim, D))
    vec_out = jax.ShapeDtypeStruct((1, D), F32)
    args, in_specs = [dy, wd, g, u, wg_t, wu_t, *norm_in, dres], [row, wres, frow, frow, wres, wres, row, vec, vec, row]
    out_shape = [jax.ShapeDtypeStruct((T, Fdim), MXU_DTYPE)] * 2 + [jax.ShapeDtypeStruct((T, D), F32)] + [vec_out] * 3
    out_specs = [frow, frow, row, vec, vec, vec]
    if produced_by is not None:
        args += [produced_by[0], produced_by[1]]
        in_specs += [row, vec]
        out_shape += [jax.ShapeDtypeStruct((T, D), MXU_DTYPE), vec_out]
        out_specs += [row, vec]

    def body(*refs):
        it = iter(refs)
        dy_ref, wd_ref, g_ref, u_ref, wg_ref, wu_ref, x_ref, nw_ref, sc_ref, dr_ref = [next(it) for _ in range(10)]
        prev_refs = [next(it), next(it)] if produced_by is not None else None
        dg_ref, du_ref, dx_ref, dnw_ref, dsc_ref, dsh_ref = [next(it) for _ in range(6)]
        prev_out = [next(it), next(it)] if produced_by is not None else None

        @pl.when(pl.program_id(0) == 0)
        def _():
            dnw_ref[...] = jnp.zeros_like(dnw_ref)
            dsc_ref[...] = jnp.zeros_like(dsc_ref)
            dsh_ref[...] = jnp.zeros_like(dsh_ref)
            if produced_by is not None:
                prev_out[1][...] = jnp.zeros_like(prev_out[1])

        dyy = dy_ref[...]
        dh = None
        for k in range(Fdim // tf):
            ks = slice(k * tf, (k + 1) * tf)
            da = _dot(dyy, wd_ref[ks, :], NT)
            gg = g_ref[:, ks].astype(F32)
            sig = _sigmoid(gg)
            dg = (da * u_ref[:, ks].astype(F32) * (sig * (1.0 + gg * (1.0 - sig)))).astype(dg_ref.dtype)
            du = (da * (gg * sig)).astype(du_ref.dtype)
            dg_ref[:, ks] = dg
            du_ref[:, ks] = du
            part = _dot(dg, wg_ref[ks, :], NN) + _dot(du, wu_ref[ks, :], NN)
            dh = part if dh is None else dh + part
        dx_norm, dsh_row, t_row = _norm_bwd_math(dh, x_ref[...], nw_ref[...], sc_ref[...])
        dsh_ref[...] += dsh_row
        dsc_ref[...] += t_row * nw_ref[...]
        dnw_ref[...] += t_row * (1.0 + sc_ref[...])
        dx = dr_ref[...] + dx_norm
        dx_ref[...] = dx
        if produced_by is not None:
            prev_out[0][...] = ((produced_by[2] * prev_refs[1][...]) * dx).astype(prev_out[0].dtype)
            prev_out[1][...] += produced_by[2] * jnp.sum(dx * prev_refs[0][...], axis=0, keepdims=True)

    return pl.pallas_call(
        body, name=name, grid=(T // tm,), in_specs=in_specs, out_specs=out_specs, out_shape=out_shape,
        compiler_params=_params("arbitrary"),
    )(*args)


def _resident(shape):
    return pl.BlockSpec(shape, lambda i: (0,) * len(shape), pipeline_mode=pl.Buffered(1))


def _conv_act(window, w):
    y = window * w[CONV_K - 1:CONV_K, :]
    for j in range(CONV_K - 1):
        y += _shift_rows(window, CONV_K - 1 - j) * w[j:j + 1, :]
    return y


def _gdn_prep(proj, conv_w, a_log_l, dt_bias_l, name):
    T = proj.shape[0]
    tm = _tile(T, 256)
    hb = tm // 8

    def body(cur_ref, halo_ref, ba_ref, w_ref, al_ref, dtb_ref, qkv_ref, bg_ref):
        i = pl.program_id(0)
        halo = jnp.where(i == 0, 0.0, halo_ref[...])
        window = jnp.concatenate([halo, cur_ref[...]], axis=0)
        y = _conv_act(window, w_ref[...])[8:, :]
        act = y * _sigmoid(y)
        for hh in range(3 * GDN_HEADS):
            blk = act[:, hh * HEAD_DIM:(hh + 1) * HEAD_DIM]
            if hh < 2 * GDN_HEADS:
                rn = lax.rsqrt(jnp.sum(blk * blk, axis=-1, keepdims=True) + EPS)
                blk = blk * rn
                if hh < GDN_HEADS:
                    blk = blk * (HEAD_DIM ** -0.5)
            qkv_ref[:, hh * HEAD_DIM:(hh + 1) * HEAD_DIM] = blk
        ba = ba_ref[...]
        lane = lax.broadcasted_iota(jnp.int32, ba.shape, 1)
        beta = _sigmoid(ba)
        g = -jnp.exp(al_ref[...]) * _softplus(ba + dtb_ref[...])
        bg_ref[...] = jnp.where(lane < GDN_HEADS, beta, jnp.where(lane < 2 * GDN_HEADS, g, 0.0))

    return pl.pallas_call(
        body, name=name, grid=(T // tm,),
        in_specs=[pl.BlockSpec((tm, QKV_WIDTH), lambda i: (i, 0)),
                  pl.BlockSpec((8, QKV_WIDTH), lambda i: (jnp.maximum(i * hb - 1, 0), 0)),
                  pl.BlockSpec((tm, 128), lambda i: (i, COL_BA // 128)),
                  pl.BlockSpec((CONV_K, QKV_WIDTH), lambda i: (0, 0)),
                  pl.BlockSpec((1, 128), lambda i: (0, 0)), pl.BlockSpec((1, 128), lambda i: (0, 0))],
        out_specs=[pl.BlockSpec((tm, QKV_WIDTH), lambda i: (i, 0)), pl.BlockSpec((tm, 128), lambda i: (i, 0))],
        out_shape=[jax.ShapeDtypeStruct((T, QKV_WIDTH), F32), jax.ShapeDtypeStruct((T, 128), F32)],
        compiler_params=_params("parallel"),
    )(proj, proj, proj, conv_w, a_log_l, dt_bias_l)


def _chunk_cumsum(v, reverse=False):
    row = lax.broadcasted_iota(jnp.int32, v.shape, 0)
    s = 1
    while s < CHUNK:
        if reverse:
            v = v + jnp.where(row < CHUNK - s, _shift_rows(v, -s), 0.0)
        else:
            v = v + jnp.where(row >= s, _shift_rows(v, s), 0.0)
        s *= 2
    return v


def _row_form(cols):
    padded = jnp.concatenate([cols, jnp.zeros((128 - CHUNK, 128), F32)], axis=0)
    return padded.T[:, :CHUNK]


def _chunk_masks():
    ri = lax.broadcasted_iota(jnp.int32, (CHUNK, CHUNK), 0)
    ci = lax.broadcasted_iota(jnp.int32, (CHUNK, CHUNK), 1)
    return ri >= ci, ri > ci, (ri == ci).astype(F32)


def _unit_lower_inverses(ms, eye):
    rs = [eye - m for m in ms]
    ps = [_split(-m) for m in ms]
    s = 2
    while s < CHUNK:
        ps = [_split(_dot3(p, p, NN)) for p in ps]
        r_parts = [_split(r) for r in rs]
        rs = [r + _dot3(p, rp, NN) for r, p, rp in zip(rs, ps, r_parts)]
        s *= 2
    return rs


def _head_elementwise(k, beta, gc, gcr, causal):
    decay = jnp.where(causal, jnp.exp(jnp.where(causal, gc - gcr, 0.0)), 0.0)
    return decay, k * beta, jnp.exp(gc)


def _head_slices(hh):
    return (slice(hh * HEAD_DIM, (hh + 1) * HEAD_DIM),
            slice(GDN_WIDTH + hh * HEAD_DIM, GDN_WIDTH + (hh + 1) * HEAD_DIM),
            slice(2 * GDN_WIDTH + hh * HEAD_DIM, 2 * GDN_WIDTH + (hh + 1) * HEAD_DIM))


def _gdn_chunk_fwd(qkv, bg, name):
    T = qkv.shape[0]
    cb = _tile(T // CHUNK, CHUNKS_PER_STEP)
    rows = cb * CHUNK

    def body(qkv_ref, bg_ref, tinv_ref, u_ref, w_ref, qd_ref, kd_ref, p_ref, cd_ref):
        masks = _chunk_masks()
        causal, strict, eye = masks
        heads = []
        for ci in range(cb):
            rs = slice(ci * CHUNK, (ci + 1) * CHUNK)
            bgv = bg_ref[rs, :]
            gc_all = _chunk_cumsum(bgv)
            gc_rows = _row_form(gc_all)
            cd_ref[rs, :] = jnp.exp(jnp.broadcast_to(gc_all[CHUNK - 1:CHUNK, :], (CHUNK, 128)))
            for hh in range(GDN_HEADS):
                qs, ks, vs = _head_slices(hh)
                q, k, v = qkv_ref[rs, qs], qkv_ref[rs, ks], qkv_ref[rs, vs]
                beta = bgv[:, hh:hh + 1]
                gc = gc_all[:, GDN_HEADS + hh:GDN_HEADS + hh + 1]
                decay, kb, eg = _head_elementwise(k, beta, gc, gc_rows[GDN_HEADS + hh:GDN_HEADS + hh + 1, :], causal)
                hs = slice(hh * HEAD_DIM, (hh + 1) * HEAD_DIM)
                cs = slice(hh * CHUNK, (hh + 1) * CHUNK)
                qd_ref[rs, hs] = (q * eg).astype(qd_ref.dtype)
                kd_ref[rs, hs] = (k * jnp.exp(gc[CHUNK - 1:CHUNK, :] - gc)).astype(kd_ref.dtype)
                heads.append((rs, hs, cs, q, k, v * beta, kb, kb * eg, decay))
        kks = [_mdot(kb, k, NT) for (_, _, _, _, k, _, kb, _, _) in heads]
        qks = [_mdot(q, k, NT) for (_, _, _, q, k, _, _, _, _) in heads]
        tinvs = _unit_lower_inverses([jnp.where(strict, kk * hd[8], 0.0) for kk, hd in zip(kks, heads)], eye)
        t_parts = [_split(t) for t in tinvs]
        us = [_dot3(tp, _split(hd[5]), NN) for tp, hd in zip(t_parts, heads)]
        ws = [_dot3(tp, _split(hd[7]), NN) for tp, hd in zip(t_parts, heads)]
        for hd, tinv, u, w, qk in zip(heads, tinvs, us, ws, qks):
            rs, hs, cs = hd[0], hd[1], hd[2]
            tinv_ref[rs, cs] = tinv
            u_ref[rs, hs] = u
            w_ref[rs, hs] = w.astype(w_ref.dtype)
            p_ref[rs, cs] = jnp.where(causal, qk * hd[8], 0.0).astype(p_ref.dtype)

    def spec(width):
        return pl.BlockSpec((rows, width), lambda n: (n, 0))

    hw, cw = GDN_WIDTH, GDN_HEADS * CHUNK
    return pl.pallas_call(
        body, name=name, grid=(T // rows,),
        in_specs=[spec(QKV_WIDTH), spec(128)],
        out_specs=[spec(cw), spec(hw), spec(hw), spec(hw), spec(hw), spec(cw), spec(128)],
        out_shape=[jax.ShapeDtypeStruct((T, cw), F32), jax.ShapeDtypeStruct((T, hw), F32),
                   jax.ShapeDtypeStruct((T, hw), MXU_DTYPE), jax.ShapeDtypeStruct((T, hw), MXU_DTYPE),
                   jax.ShapeDtypeStruct((T, hw), MXU_DTYPE), jax.ShapeDtypeStruct((T, cw), MXU_DTYPE),
                   jax.ShapeDtypeStruct((T, 128), F32)],
        compiler_params=_params("parallel"),
    )(qkv, bg)


def _gdn_scan_fwd(u, w, qd, kd, p, cd, name):
    T = u.shape[0]
    cb = _tile(T // CHUNK, SCAN_CHUNKS_PER_STEP)
    rows = cb * CHUNK

    def body(u_ref, w_ref, qd_ref, kd_ref, p_ref, cd_ref, o_ref, s_all_ref, vn_ref, s_ref):
        @pl.when(pl.program_id(0) == 0)
        def _():
            s_ref[...] = jnp.zeros_like(s_ref)

        hss = [slice(hh * HEAD_DIM, (hh + 1) * HEAD_DIM) for hh in range(GDN_HEADS)]
        css = [slice(hh * CHUNK, (hh + 1) * CHUNK) for hh in range(GDN_HEADS)]
        s_cur = [s_ref[hh] for hh in range(GDN_HEADS)]
        for ci in range(cb):
            rs = slice(ci * CHUNK, (ci + 1) * CHUNK)
            for hh in range(GDN_HEADS):
                s_all_ref[ci * GDN_WIDTH + hh * HEAD_DIM:ci * GDN_WIDTH + (hh + 1) * HEAD_DIM, :] = s_cur[hh]
            s_ms = [s.astype(MXU_DTYPE) for s in s_cur]
            w_s = [_dot(w_ref[rs, hs], s_m, NN) for hs, s_m in zip(hss, s_ms)]
            q_s = [_dot(qd_ref[rs, hs], s_m, NN) for hs, s_m in zip(hss, s_ms)]
            v_ms = [(u_ref[rs, hs] - ws_).astype(MXU_DTYPE) for hs, ws_ in zip(hss, w_s)]
            k_v = [_dot(kd_ref[rs, hs], v_m, TN) for hs, v_m in zip(hss, v_ms)]
            p_v = [_dot(p_ref[rs, cs], v_m, NN) for cs, v_m in zip(css, v_ms)]
            for hh in range(GDN_HEADS):
                vn_ref[rs, hss[hh]] = v_ms[hh]
                o_ref[rs, hss[hh]] = q_s[hh] + p_v[hh]
                c_dec = cd_ref[ci * CHUNK:ci * CHUNK + 1, GDN_HEADS + hh:GDN_HEADS + hh + 1]
                s_cur[hh] = s_cur[hh] * c_dec + k_v[hh]
        for hh in range(GDN_HEADS):
            s_ref[hh] = s_cur[hh]

    def spec(width):
        return pl.BlockSpec((rows, width), lambda n: (n, 0))

    hw, cw = GDN_WIDTH, GDN_HEADS * CHUNK
    return pl.pallas_call(
        body, name=name, grid=(T // rows,),
        in_specs=[spec(hw), spec(hw), spec(hw), spec(hw), spec(cw), spec(128)],
        out_specs=[spec(hw), pl.BlockSpec((cb * GDN_WIDTH, HEAD_DIM), lambda n: (n, 0)), spec(hw)],
        out_shape=[jax.ShapeDtypeStruct((T, hw), F32),
                   jax.ShapeDtypeStruct((T // CHUNK * GDN_WIDTH, HEAD_DIM), F32),
                   jax.ShapeDtypeStruct((T, hw), MXU_DTYPE)],
        scratch_shapes=[pltpu.VMEM((GDN_HEADS, HEAD_DIM, HEAD_DIM), F32)],
        compiler_params=_params("arbitrary"),
    )(u, w, qd, kd, p, cd)


def _gdn_scan_bwd(do, w, qd, kd, p, cd, s_all, vn, name):
    T = do.shape[0]
    cb = _tile(T // CHUNK, SCAN_CHUNKS_PER_STEP)
    rows = cb * CHUNK
    n_steps = T // rows

    def body(do_ref, w_ref, qd_ref, kd_ref, p_ref, cd_ref, s_all_ref, vn_ref,
             dvn_ref, dw_ref, dqd_ref, dkd_ref, dp_ref, dcd_ref, ds_ref):
        @pl.when(pl.program_id(0) == 0)
        def _():
            ds_ref[...] = jnp.zeros_like(ds_ref)

        causal, _, _ = _chunk_masks()
        lane = lax.broadcasted_iota(jnp.int32, (CHUNK, 128), 1)
        heads = range(GDN_HEADS)
        hss = [slice(hh * HEAD_DIM, (hh + 1) * HEAD_DIM) for hh in heads]
        css = [slice(hh * CHUNK, (hh + 1) * CHUNK) for hh in heads]
        ds_cur = [ds_ref[hh] for hh in heads]
        for ci in reversed(range(cb)):
            rs = slice(ci * CHUNK, (ci + 1) * CHUNK)
            ds_ms = [d.astype(MXU_DTYPE) for d in ds_cur]
            s_olds = [s_all_ref[ci * GDN_WIDTH + hh * HEAD_DIM:ci * GDN_WIDTH + (hh + 1) * HEAD_DIM, :] for hh in heads]
            s_ms = [s.astype(MXU_DTYPE) for s in s_olds]
            do_ms = [do_ref[rs, hs].astype(MXU_DTYPE) for hs in hss]
            p_do = [_dot(p_ref[rs, cs], do_m, TN) for cs, do_m in zip(css, do_ms)]
            k_ds = [_dot(kd_ref[rs, hs], ds_m, NN) for hs, ds_m in zip(hss, ds_ms)]
            q_do = [_dot(qd_ref[rs, hs], do_m, TN) for hs, do_m in zip(hss, do_ms)]
            dqds = [_dot(do_m, s_m, NT) for do_m, s_m in zip(do_ms, s_ms)]
            dkds = [_dot(vn_ref[rs, hs], ds_m, NT) for hs, ds_m in zip(hss, ds_ms)]
            dps = [_dot(do_m, vn_ref[rs, hs], NT) for hs, do_m in zip(hss, do_ms)]
            dv_news = [a + b for a, b in zip(p_do, k_ds)]
            dvn_ms = [d.astype(MXU_DTYPE) for d in dv_news]
            w_dv = [_dot(w_ref[rs, hs], dvn_m, TN) for hs, dvn_m in zip(hss, dvn_ms)]
            dws = [_dot(dvn_m, s_m, NT) for dvn_m, s_m in zip(dvn_ms, s_ms)]
            dcd_tile = jnp.zeros((CHUNK, 128), F32)
            for hh in heads:
                dvn_ref[rs, hss[hh]] = dv_news[hh]
                dw_ref[rs, hss[hh]] = -dws[hh]
                dqd_ref[rs, hss[hh]] = dqds[hh]
                dkd_ref[rs, hss[hh]] = dkds[hh]
                dp_ref[rs, css[hh]] = jnp.where(causal, dps[hh], 0.0)
                dcd = jnp.sum(jnp.sum(s_olds[hh] * ds_cur[hh], axis=1, keepdims=True), axis=0, keepdims=True)
                dcd_tile = jnp.where(lane == GDN_HEADS + hh, dcd, dcd_tile)
                c_dec = cd_ref[ci * CHUNK:ci * CHUNK + 1, GDN_HEADS + hh:GDN_HEADS + hh + 1]
                ds_cur[hh] = c_dec * ds_cur[hh] + q_do[hh] - w_dv[hh]
            dcd_ref[rs, :] = dcd_tile
        for hh in heads:
            ds_ref[hh] = ds_cur[hh]

    def spec(width):
        return pl.BlockSpec((rows, width), lambda n: (n_steps - 1 - n, 0))

    hw, cw = GDN_WIDTH, GDN_HEADS * CHUNK
    return pl.pallas_call(
        body, name=name, grid=(n_steps,),
        in_specs=[spec(hw), spec(hw), spec(hw), spec(hw), spec(cw), spec(128),
                  pl.BlockSpec((cb * GDN_WIDTH, HEAD_DIM), lambda n: (n_steps - 1 - n, 0)), spec(hw)],
        out_specs=[spec(hw), spec(hw), spec(hw), spec(hw), spec(cw), spec(128)],
        out_shape=[jax.ShapeDtypeStruct((T, hw), F32)] * 4
        + [jax.ShapeDtypeStruct((T, cw), F32), jax.ShapeDtypeStruct((T, 128), F32)],
        scratch_shapes=[pltpu.VMEM((GDN_HEADS, HEAD_DIM, HEAD_DIM), F32)],
        compiler_params=_params("arbitrary"),
    )(do, w, qd, kd, p, cd, s_all, vn)


def _gdn_chunk_bwd(qkv, bg, tinv_all, u, w, dvn, dw, dqd, dkd, dp, dcd, name):
    T = qkv.shape[0]
    cb = _tile(T // CHUNK, CHUNKS_PER_STEP)
    rows = cb * CHUNK

    def body(qkv_ref, bg_ref, tinv_ref, u_ref, w_ref, dvn_ref, dw_ref, dqd_ref, dkd_ref, dp_ref, dcd_ref,
             dqkv_ref, dbg_ref):
        masks = _chunk_masks()
        causal, strict, _ = masks
        lane = lax.broadcasted_iota(jnp.int32, (CHUNK, 128), 1)
        row = lax.broadcasted_iota(jnp.int32, (CHUNK, 128), 0)
        heads = []
        for ci in range(cb):
            rs = slice(ci * CHUNK, (ci + 1) * CHUNK)
            bgv = bg_ref[rs, :]
            gc_all = _chunk_cumsum(bgv)
            gc_rows = _row_form(gc_all)
            for hh in range(GDN_HEADS):
                qs, ks, vs = _head_slices(hh)
                q, k = qkv_ref[rs, qs], qkv_ref[rs, ks]
                beta = bgv[:, hh:hh + 1]
                gc = gc_all[:, GDN_HEADS + hh:GDN_HEADS + hh + 1]
                decay, kb, eg = _head_elementwise(k, beta, gc, gc_rows[GDN_HEADS + hh:GDN_HEADS + hh + 1, :], causal)
                heads.append(dict(ci=ci, hh=hh, rs=rs, hs=slice(hh * HEAD_DIM, (hh + 1) * HEAD_DIM),
                                  cs=slice(hh * CHUNK, (hh + 1) * CHUNK), q=q, k=k, beta=beta, gc=gc,
                                  decay=decay, kb=kb, eg=eg))
        for hd in heads:
            hd["t"] = _split(tinv_ref[hd["rs"], hd["cs"]])
        for hd in heads:
            hd["kk"] = _mdot(hd["kb"], hd["k"], NT)
            hd["qk"] = _mdot(hd["q"], hd["k"], NT)
        for hd in heads:
            hd["dvb"] = _dot3(hd["t"], _split(dvn_ref[hd["rs"], hd["hs"]]), TN)
            hd["dkbeg"] = _dot3(hd["t"], _split(dw_ref[hd["rs"], hd["hs"]]), TN)
        for hd in heads:
            rs, hs = hd["rs"], hd["hs"]
            da = -(_mdot(hd["dvb"], u_ref[rs, hs], NT) + _mdot(hd["dkbeg"], w_ref[rs, hs], NT))
            dm = jnp.where(strict, da, 0.0)
            dp_ = dp_ref[rs, hd["cs"]]
            hd["dkk"] = dm * hd["decay"]
            hd["dqk"] = dp_ * hd["decay"]
            hd["e"] = (hd["dkk"] * hd["kk"] + hd["dqk"] * hd["qk"])
        for hd in heads:
            hd["dkb"] = _mdot(hd["dkk"], hd["k"], NN)
            hd["dk"] = _mdot(hd["dkk"], hd["kb"], TN) + _mdot(hd["dqk"], hd["q"], TN)
            hd["dq"] = _mdot(hd["dqk"], hd["k"], NN)
            onehot = (lane == GDN_HEADS + hd["hh"]).astype(jnp.bfloat16)
            e_hi, e_lo = _split(hd["e"])
            hd["col_sums"] = _dot(e_lo, onehot, TN) + _dot(e_hi, onehot, TN)
        tiles = {}
        for hd in heads:
            ci, hh, rs, hs = hd["ci"], hd["hh"], hd["rs"], hd["hs"]
            qs, ks, vs = _head_slices(hh)
            q, k, beta, gc, eg, kb = hd["q"], hd["k"], hd["beta"], hd["gc"], hd["eg"], hd["kb"]
            v = qkv_ref[rs, vs]
            dqd_, dkd_ = dqd_ref[rs, hs], dkd_ref[rs, hs]
            gl = gc[CHUNK - 1:CHUNK, :]
            ek = jnp.exp(gl - gc)
            dkb = hd["dkb"] + hd["dkbeg"] * eg
            deg = jnp.sum(dqd_ * q, axis=1, keepdims=True) + jnp.sum(hd["dkbeg"] * kb, axis=1, keepdims=True)
            dek = jnp.sum(dkd_ * k, axis=1, keepdims=True)
            dcd_ = dcd_ref[ci * CHUNK:ci * CHUNK + 1, GDN_HEADS + hh:GDN_HEADS + hh + 1]
            dgl = jnp.sum(dek * ek, axis=0, keepdims=True) + dcd_ * jnp.exp(gl)
            dgc = jnp.sum(hd["e"], axis=1, keepdims=True) + deg * eg - dek * ek
            dbeta_tile, dgc_tile = tiles.get(ci, (jnp.zeros((CHUNK, 128), F32), jnp.zeros((CHUNK, 128), F32)))
            dgc_tile += jnp.where(lane == GDN_HEADS + hh, dgc, 0.0) - hd["col_sums"]
            dgc_tile += jnp.where((lane == GDN_HEADS + hh) & (row == CHUNK - 1), dgl, 0.0)
            dbeta = jnp.sum(dkb * k, axis=1, keepdims=True) + jnp.sum(hd["dvb"] * v, axis=1, keepdims=True)
            dbeta_tile += jnp.where(lane == hh, dbeta, 0.0)
            tiles[ci] = (dbeta_tile, dgc_tile)
            dqkv_ref[rs, qs] = hd["dq"] + dqd_ * eg
            dqkv_ref[rs, ks] = hd["dk"] + dkd_ * ek + dkb * beta
            dqkv_ref[rs, vs] = hd["dvb"] * beta
        for ci in range(cb):
            dbeta_tile, dgc_tile = tiles[ci]
            dbg_ref[ci * CHUNK:(ci + 1) * CHUNK, :] = dbeta_tile + _chunk_cumsum(dgc_tile, reverse=True)

    def spec(width):
        return pl.BlockSpec((rows, width), lambda n: (n, 0))

    hw, cw = GDN_WIDTH, GDN_HEADS * CHUNK
    return pl.pallas_call(
        body, name=name, grid=(T // rows,),
        in_specs=[spec(QKV_WIDTH), spec(128), spec(cw), spec(hw), spec(hw), spec(hw), spec(hw), spec(hw),
                  spec(hw), spec(cw), spec(128)],
        out_specs=[spec(QKV_WIDTH), spec(128)],
        out_shape=[jax.ShapeDtypeStruct((T, QKV_WIDTH), F32), jax.ShapeDtypeStruct((T, 128), F32)],
        compiler_params=_params("parallel"),
    )(qkv, bg, tinv_all, u, w, dvn, dw, dqd, dkd, dp, dcd)


def _pool_counts(i, tm, rows, offset):
    t = i * tm - offset + lax.broadcasted_iota(jnp.int32, (rows, 1), 0)
    return [jnp.minimum(t + 1, w).astype(F32) for w in POOL_WINDOWS]


def _window_sums(window, forward):
    sums, s, step = [], window, 1
    for _ in POOL_WINDOWS:
        s = s + _shift_rows(s, -step if forward else step)
        sums.append(s)
        step *= 2
    return sums


def _pooled(window, counts):
    sums = _window_sums(window, forward=False)
    out = []
    for gi in range(POOL_GROUPS):
        sl = slice(gi * 128, (gi + 1) * 128)
        out.append(sums[gi][HALO:, sl] / counts[gi] - window[HALO:, sl])
    return out


def _mix_post(o, proj, gdn_norm, pool_w, pool_scale, name):
    T = o.shape[0]
    tm = _tile(T, 256)
    hb = tm // HALO

    def body(o_ref, z_ref, p_ref, ph_ref, gn_ref, pw_ref, ps_ref, out_ref):
        i = pl.program_id(0)
        for hh in range(GDN_HEADS):
            sl = slice(hh * HEAD_DIM, (hh + 1) * HEAD_DIM)
            oh, zh = o_ref[:, sl], z_ref[:, sl]
            ro = lax.rsqrt(jnp.mean(oh * oh, axis=-1, keepdims=True) + EPS)
            out_ref[:, sl] = (((oh * ro) * gn_ref[...]) * (zh * _sigmoid(zh))).astype(out_ref.dtype)
        halo = jnp.where(i == 0, 0.0, ph_ref[...])
        window = jnp.concatenate([halo, p_ref[...]], axis=0)
        pooled = _pooled(window, _pool_counts(i, tm, tm, 0))
        for gi in range(POOL_GROUPS):
            pm = _mdot(pooled[gi], pw_ref[gi], NN)
            out_ref[:, GDN_WIDTH + gi * 128:GDN_WIDTH + (gi + 1) * 128] = (
                pm * ps_ref[:, gi * 128:(gi + 1) * 128]).astype(out_ref.dtype)

    return pl.pallas_call(
        body, name=name, grid=(T // tm,),
        in_specs=[pl.BlockSpec((tm, GDN_WIDTH), lambda i: (i, 0)),
                  pl.BlockSpec((tm, GDN_WIDTH), lambda i: (i, COL_Z // GDN_WIDTH)),
                  pl.BlockSpec((tm, POOL_WIDTH), lambda i: (i, COL_P // POOL_WIDTH)),
                  pl.BlockSpec((HALO, POOL_WIDTH), lambda i: (jnp.maximum(i * hb - 1, 0), COL_P // POOL_WIDTH)),
                  pl.BlockSpec((1, HEAD_DIM), lambda i: (0, 0)),
                  pl.BlockSpec((POOL_GROUPS, 128, 128), lambda i: (0, 0, 0)),
                  pl.BlockSpec((1, POOL_WIDTH), lambda i: (0, 0))],
        out_specs=pl.BlockSpec((tm, GDN_WIDTH + POOL_WIDTH), lambda i: (i, 0)),
        out_shape=jax.ShapeDtypeStruct((T, GDN_WIDTH + POOL_WIDTH), MXU_DTYPE),
        compiler_params=_params("parallel"),
    )(o, proj, proj, proj, gdn_norm, pool_w, pool_scale)


def _mix_post_bwd(dmix, o, proj, gdn_norm, pool_w, pool_scale, name):
    T = o.shape[0]
    tm = _tile(T, 256)
    hb = tm // HALO
    n_tiles = T // tm

    def body(dg_ref, dpo_ref, dpo_next_ref, o_ref, z_ref, p_ref, ph_ref, gn_ref, pw_ref, ps_ref,
             do_ref, dzp_ref, dgn_ref, dpw_ref, dps_ref):
        i = pl.program_id(0)

        @pl.when(i == 0)
        def _():
            dgn_ref[...] = jnp.zeros_like(dgn_ref)
            dpw_ref[...] = jnp.zeros_like(dpw_ref)
            dps_ref[...] = jnp.zeros_like(dps_ref)

        gn = gn_ref[...]
        dgn = jnp.zeros((1, HEAD_DIM), F32)
        for hh in range(GDN_HEADS):
            sl = slice(hh * HEAD_DIM, (hh + 1) * HEAD_DIM)
            oh, zh, dy = o_ref[:, sl], z_ref[:, sl], dg_ref[:, sl]
            ro = lax.rsqrt(jnp.mean(oh * oh, axis=-1, keepdims=True) + EPS)
            on = oh * ro
            sig = _sigmoid(zh)
            sz = zh * sig
            dzp_ref[:, sl] = (dy * (on * gn) * (sig * (1.0 + zh * (1.0 - sig)))).astype(dzp_ref.dtype)
            dgn += jnp.sum(dy * on * sz, axis=0, keepdims=True)
            don = dy * gn * sz
            do_ref[:, sl] = ro * (don - on * jnp.mean(don * on, axis=-1, keepdims=True))
        dgn_ref[...] += dgn

        halo = jnp.where(i == 0, 0.0, ph_ref[...])
        window = jnp.concatenate([halo, p_ref[...]], axis=0)
        counts = _pool_counts(i, tm, tm + HALO, 0)
        pooled = _pooled(window, [cn[:tm] for cn in counts])
        nxt = jnp.where(i == n_tiles - 1, 0.0, dpo_next_ref[...])
        dpo_w = jnp.concatenate([dpo_ref[...], nxt], axis=0)
        ps = ps_ref[...]
        dps = []
        scaled = []
        for gi in range(POOL_GROUPS):
            sl = slice(gi * 128, (gi + 1) * 128)
            dpm = dpo_w[:, sl] * ps[:, sl]
            pm = _mdot(pooled[gi], pw_ref[gi], NN)
            dps.append(jnp.sum(dpo_w[:tm, sl] * pm, axis=0, keepdims=True))
            dpw_ref[gi] += _mdot(pooled[gi], dpm[:tm], TN)
            dpooled = _mdot(dpm, pw_ref[gi], NT)
            scaled.append((dpooled, dpooled / counts[gi]))
        dps_ref[...] += jnp.concatenate(dps, axis=1)
        lead = _window_sums(jnp.concatenate([sc for _, sc in scaled], axis=1), forward=True)
        for gi in range(POOL_GROUPS):
            sl = slice(gi * 128, (gi + 1) * 128)
            dzp_ref[:, GDN_WIDTH + gi * 128:GDN_WIDTH + (gi + 1) * 128] = (
                lead[gi][:tm, sl] - scaled[gi][0][:tm]).astype(dzp_ref.dtype)

    last_halo = T // HALO - 1
    return pl.pallas_call(
        body, name=name, grid=(n_tiles,),
        in_specs=[pl.BlockSpec((tm, GDN_WIDTH), lambda i: (i, 0)),
                  pl.BlockSpec((tm, POOL_WIDTH), lambda i: (i, 1)),
                  pl.BlockSpec((HALO, POOL_WIDTH), lambda i: (jnp.minimum((i + 1) * hb, last_halo), 1)),
                  pl.BlockSpec((tm, GDN_WIDTH), lambda i: (i, 0)),
                  pl.BlockSpec((tm, GDN_WIDTH), lambda i: (i, COL_Z // GDN_WIDTH)),
                  pl.BlockSpec((tm, POOL_WIDTH), lambda i: (i, COL_P // POOL_WIDTH)),
                  pl.BlockSpec((HALO, POOL_WIDTH), lambda i: (jnp.maximum(i * hb - 1, 0), COL_P // POOL_WIDTH)),
                  pl.BlockSpec((1, HEAD_DIM), lambda i: (0, 0)),
                  pl.BlockSpec((POOL_GROUPS, 128, 128), lambda i: (0, 0, 0)),
                  pl.BlockSpec((1, POOL_WIDTH), lambda i: (0, 0))],
        out_specs=[pl.BlockSpec((tm, GDN_WIDTH), lambda i: (i, 0)),
                   pl.BlockSpec((tm, GDN_WIDTH + POOL_WIDTH), lambda i: (i, 0)),
                   pl.BlockSpec((1, HEAD_DIM), lambda i: (0, 0)),
                   pl.BlockSpec((POOL_GROUPS, 128, 128), lambda i: (0, 0, 0)),
                   pl.BlockSpec((1, POOL_WIDTH), lambda i: (0, 0))],
        out_shape=[jax.ShapeDtypeStruct((T, GDN_WIDTH), F32),
                   jax.ShapeDtypeStruct((T, GDN_WIDTH + POOL_WIDTH), MXU_DTYPE),
                   jax.ShapeDtypeStruct((1, HEAD_DIM), F32),
                   jax.ShapeDtypeStruct((POOL_GROUPS, 128, 128), F32),
                   jax.ShapeDtypeStruct((1, POOL_WIDTH), F32)],
        compiler_params=_params("arbitrary"),
    )(dmix, dmix, dmix, o, proj, proj, proj, gdn_norm, pool_w, pool_scale)


def _gdn_prep_bwd(proj, conv_w, a_log_l, dt_bias_l, dqkv, dbg, dzp, name):
    T = proj.shape[0]
    tm = _tile(T, 256)
    hb = tm // 8
    n_tiles = T // tm
    last_halo = T // 8 - 1

    def body(cur_ref, before_ref, after_ref, ba_ref, w_ref, al_ref, dtb_ref, dq_ref, dq_after_ref, dbg_ref,
             dzp_ref, dproj_ref, dw_ref, dal_ref, ddtb_ref):
        i = pl.program_id(0)

        @pl.when(i == 0)
        def _():
            dw_ref[...] = jnp.zeros_like(dw_ref)
            dal_ref[...] = jnp.zeros_like(dal_ref)
            ddtb_ref[...] = jnp.zeros_like(ddtb_ref)

        last = i == n_tiles - 1
        w = w_ref[...]
        before = jnp.where(i == 0, 0.0, before_ref[...])
        after = jnp.where(last, 0.0, after_ref[...])
        window = jnp.concatenate([before, cur_ref[...], after], axis=0)
        y = _conv_act(window, w)
        sig = _sigmoid(y)
        act = y * sig
        dq_w = jnp.concatenate([jnp.zeros((8, QKV_WIDTH), F32), dq_ref[...],
                                jnp.where(last, 0.0, dq_after_ref[...])], axis=0)
        dact = []
        for hh in range(3 * GDN_HEADS):
            sl = slice(hh * HEAD_DIM, (hh + 1) * HEAD_DIM)
            blk, dblk = act[:, sl], dq_w[:, sl]
            if hh < 2 * GDN_HEADS:
                rn = lax.rsqrt(jnp.sum(blk * blk, axis=-1, keepdims=True) + EPS)
                unit = blk * rn
                if hh < GDN_HEADS:
                    dblk = dblk * (HEAD_DIM ** -0.5)
                dblk = rn * (dblk - unit * jnp.sum(dblk * unit, axis=-1, keepdims=True))
            dact.append(dblk)
        dy = jnp.concatenate(dact, axis=1) * (sig * (1.0 + y * (1.0 - sig)))
        dx = dy * w[CONV_K - 1:CONV_K, :]
        dws = [None] * CONV_K
        dws[CONV_K - 1] = jnp.sum(dy[8:8 + tm] * window[8:8 + tm], axis=0, keepdims=True)
        for j in range(CONV_K - 1):
            s = CONV_K - 1 - j
            dx += _shift_rows(dy, -s) * w[j:j + 1, :]
            dws[j] = jnp.sum(dy[8:8 + tm] * _shift_rows(window, s)[8:8 + tm], axis=0, keepdims=True)
        dw_ref[...] += jnp.concatenate(dws, axis=0)
        dproj_ref[:, :QKV_WIDTH] = dx[8:8 + tm].astype(dproj_ref.dtype)
        dproj_ref[:, COL_Z:COL_BA] = dzp_ref[...]

        ba = ba_ref[...]
        dbg_ = dbg_ref[...]
        lane = lax.broadcasted_iota(jnp.int32, ba.shape, 1)
        beta = _sigmoid(ba)
        pre = ba + dtb_ref[...]
        neg_a = -jnp.exp(al_ref[...])
        g = neg_a * _softplus(pre)
        is_g = (lane >= GDN_HEADS) & (lane < 2 * GDN_HEADS)
        da_raw = jnp.where(is_g, dbg_ * neg_a * _sigmoid(pre), 0.0)
        dba = jnp.where(lane < GDN_HEADS, dbg_ * beta * (1.0 - beta), da_raw)
        dproj_ref[:, COL_BA:] = dba.astype(dproj_ref.dtype)
        dal_ref[...] += jnp.sum(jnp.where(is_g, dbg_ * g, 0.0), axis=0, keepdims=True)
        ddtb_ref[...] += jnp.sum(da_raw, axis=0, keepdims=True)

    lane_vec = pl.BlockSpec((1, 128), lambda i: (0, 0))
    return pl.pallas_call(
        body, name=name, grid=(n_tiles,),
        in_specs=[pl.BlockSpec((tm, QKV_WIDTH), lambda i: (i, 0)),
                  pl.BlockSpec((8, QKV_WIDTH), lambda i: (jnp.maximum(i * hb - 1, 0), 0)),
                  pl.BlockSpec((8, QKV_WIDTH), lambda i: (jnp.minimum((i + 1) * hb, last_halo), 0)),
                  pl.BlockSpec((tm, 128), lambda i: (i, COL_BA // 128)),
                  pl.BlockSpec((CONV_K, QKV_WIDTH), lambda i: (0, 0)), lane_vec, lane_vec,
                  pl.BlockSpec((tm, QKV_WIDTH), lambda i: (i, 0)),
                  pl.BlockSpec((8, QKV_WIDTH), lambda i: (jnp.minimum((i + 1) * hb, last_halo), 0)),
                  pl.BlockSpec((tm, 128), lambda i: (i, 0)),
                  pl.BlockSpec((tm, GDN_WIDTH + POOL_WIDTH), lambda i: (i, 0))],
        out_specs=[pl.BlockSpec((tm, D_IN_PAD), lambda i: (i, 0)),
                   pl.BlockSpec((CONV_K, QKV_WIDTH), lambda i: (0, 0)), lane_vec, lane_vec],
        out_shape=[jax.ShapeDtypeStruct((T, D_IN_PAD), MXU_DTYPE),
                   jax.ShapeDtypeStruct((CONV_K, QKV_WIDTH), F32),
                   jax.ShapeDtypeStruct((1, 128), F32), jax.ShapeDtypeStruct((1, 128), F32)],
        compiler_params=_params("arbitrary"),
    )(proj, proj, proj, proj, conv_w, a_log_l, dt_bias_l, dqkv, dqkv, dbg, dzp)


def _mod_part(c_all, w_ada, b_part, name):
    def body(c_ref, w_ref, b_ref, out_ref):
        cc = c_ref[...]
        out_ref[...] = _mdot(cc * _sigmoid(cc), w_ref[...], NN) + b_ref[...]

    return pl.pallas_call(
        body, name=name, out_shape=jax.ShapeDtypeStruct((c_all.shape[0], w_ada.shape[1]), F32),
        compiler_params=_params(),
    )(c_all, w_ada, b_part)


def _w_ada_grad(c_all, dmod_part, name):
    def body(c_ref, d_ref, out_ref):
        cc = c_ref[...]
        out_ref[...] = _mdot(cc * _sigmoid(cc), d_ref[...], TN)

    return pl.pallas_call(
        body, name=name, out_shape=jax.ShapeDtypeStruct((c_all.shape[1], dmod_part.shape[1]), F32),
        compiler_params=_params(),
    )(c_all, dmod_part)


def _sum_parts(parts, name):
    _, R, C = parts.shape
    tr = max([t for t in range(16, min(R, 512) + 1, 16) if R % t == 0], default=R)

    def body(p_ref, out_ref):
        acc = p_ref[0].astype(F32)
        for s in range(1, N_DEV):
            acc += p_ref[s].astype(F32)
        out_ref[...] = acc

    return pl.pallas_call(
        body, name=name, grid=(R // tr,),
        in_specs=[pl.BlockSpec((N_DEV, tr, C), lambda i: (0, i, 0))],
        out_specs=pl.BlockSpec((tr, C), lambda i: (i, 0)),
        out_shape=jax.ShapeDtypeStruct((R, C), F32),
        compiler_params=_params("parallel"),
    )(parts)


def _adamw_math(w, g, m, v):
    mm = ADAM_B1 * m + (1.0 - ADAM_B1) * g
    vv = ADAM_B2 * v + (1.0 - ADAM_B2) * (g * g)
    m_hat = mm / (1.0 - ADAM_B1 ** ADAM_STEP)
    v_hat = vv / (1.0 - ADAM_B2 ** ADAM_STEP)
    return -ADAM_LR * (m_hat / (jnp.sqrt(v_hat) + ADAM_EPS) + ADAM_WD * w), mm, vv


def _adamw_small(ws, gs, ms, vs, name):
    n = len(ws)

    def body(*refs):
        for i in range(n):
            d, mm, vv = _adamw_math(*[refs[k * n + i][...] for k in range(4)])
            refs[4 * n + i][...] = d
            refs[5 * n + i][...] = mm
            refs[6 * n + i][...] = vv

    out = pl.pallas_call(
        body, name=name, out_shape=[jax.ShapeDtypeStruct(w.shape, F32) for w in ws] * 3,
        compiler_params=_params(),
    )(*ws, *gs, *ms, *vs)
    return out[:n], out[n:2 * n], out[2 * n:]


def _adamw(w, g, m, v, name):
    R, C = w.shape
    tr = max([t for t in range(8, min(R, 512) + 1, 8) if R % t == 0], default=R)

    def body(w_ref, g_ref, m_ref, v_ref, d_ref, mo_ref, vo_ref):
        d_ref[...], mo_ref[...], vo_ref[...] = _adamw_math(w_ref[...], g_ref[...], m_ref[...], v_ref[...])

    spec = pl.BlockSpec((tr, C), lambda i: (i, 0))
    return pl.pallas_call(
        body, name=name, grid=(R // tr,),
        in_specs=[spec] * 4, out_specs=[spec] * 3,
        out_shape=[jax.ShapeDtypeStruct((R, C), F32)] * 3,
        compiler_params=_params("parallel"),
    )(w, g, m, v)


def _weight_grad(a, b, name, dep=None):
    return _matmul([(a, b)], TN, WIRE_DTYPE, name, tm=1408, tn=1024, tk=1024, dep=dep)


def _rows_of(flat, lanes=1024):
    flat = flat.reshape(-1)
    n = -(-flat.shape[0] // lanes) * lanes
    return jnp.pad(flat, (0, n - flat.shape[0])).reshape(n // lanes, lanes)


def _pad_rows(a, rows):
    return jnp.pad(a, ((0, rows - a.shape[0]), (0, 0)))


def kernel(x, c, w_ada, b_ada, norm_ffn1, ffn1_gate, ffn1_up, ffn1_down, norm_mix, w_in, conv_w, a_log, dt_bias, gdn_norm, pool_w, pool_scale, w_out, norm_ffn2, ffn2_gate, ffn2_up, ffn2_down, final_norm, loss_target, m_w_ada, m_b_ada, m_norm_ffn1, m_ffn1_gate, m_ffn1_up, m_ffn1_down, m_norm_mix, m_w_in, m_conv_w, m_a_log, m_dt_bias, m_gdn_norm, m_pool_w, m_pool_scale, m_w_out, m_norm_ffn2, m_ffn2_gate, m_ffn2_up, m_ffn2_down, m_final_norm, v_w_ada, v_b_ada, v_norm_ffn1, v_ffn1_gate, v_ffn1_up, v_ffn1_down, v_norm_mix, v_w_in, v_conv_w, v_a_log, v_dt_bias, v_gdn_norm, v_pool_w, v_pool_scale, v_w_out, v_norm_ffn2, v_ffn2_gate, v_ffn2_up, v_ffn2_down, v_final_norm):
    T, D = x.shape[1], x.shape[2]
    Fs = ffn1_gate.shape[2]
    Ws = w_in.shape[2]
    Ws_pad = -(-Ws // 16) * 16
    Os = w_out.shape[1]
    Ms = w_ada.shape[2]
    Cs = conv_w.shape[2]
    me = 4 * lax.axis_index("x") + 2 * lax.axis_index("y") + lax.axis_index("c")
    x0, target = x[0], loss_target[0]

    def wire(a):
        return a.astype(WIRE_DTYPE)

    def token(started):
        return started[4][:1, :1]

    def with_own(landed, own):
        return lax.dynamic_update_slice(landed, own[None], (me, 0, 0))

    def full(landed):
        return landed.reshape(-1, D).astype(MXU_DTYPE)

    no_dep = jnp.zeros((8, 128), F32)
    small = jnp.concatenate([_pad_rows(c, 8), _pad_rows(jnp.pad(conv_w[0], ((0, 0), (0, D - Cs))), 8)], axis=0)
    got, = _all_gather([small], "gather_small")
    c_all = got[:, 0, :]
    conv_full = jnp.transpose(got[:, 8:8 + CONV_K, :Cs], (1, 0, 2)).reshape(CONV_K, QKV_WIDTH)
    b_part = lax.dynamic_slice(b_ada, (0, me * Ms), (1, Ms))
    mod_parts, = _all_gather([_mod_part(c_all, w_ada[0], b_part, "mod_part")], "gather_mod")
    mod_all = jnp.transpose(mod_parts, (1, 0, 2)).reshape(N_DEV, N_MOD * D)
    mod = lax.dynamic_slice(mod_all, (me, 0), (1, N_MOD * D)).reshape(N_MOD, 1, D)
    sh1, sc1, gt1, sh2, sc2, gt2, sh3, sc3, gt3 = [mod[i] for i in range(N_MOD)]

    w1 = [wire(ffn1_gate[0].T), wire(ffn1_up[0].T), wire(ffn1_down[0])]
    w2 = [wire(_pad_rows(w_in[0].T, Ws_pad)), wire(w_out[0])]
    w3 = [wire(ffn2_gate[0].T), wire(ffn2_up[0].T), wire(ffn2_down[0])]
    off23 = [0, Ws_pad, Ws_pad + Os, Ws_pad + Os + Fs, Ws_pad + Os + 2 * Fs]
    w1_all = _all_gather(w1, "gather_w1", dep=mod_all)
    wg1_t, wu1_t, wd1 = [full(w) for w in w1_all]
    w2_sent = _exchange_start(w2, True, w1_all[0], "w2_start")
    w3_sent = _exchange_start(w3, True, w2_sent[4], "w3_start")

    lane_pad = lambda a: jnp.pad(a, ((0, 0), (GDN_HEADS, 128 - 2 * GDN_HEADS)))
    a_log_l, dt_bias_l = lane_pad(a_log), lane_pad(dt_bias)
    pool_w_m = pool_w[0].astype(MXU_DTYPE)

    g1, u1, a1, h1, y1, x1, h2 = _swiglu_fwd(
        x0, wg1_t, wu1_t, wd1, "ffn1_fwd", norm_in=(norm_ffn1, sc1 + token(w3_sent), sh1),
        resid_out=(x0, gt1, 0.5, norm_mix, sc2, sh2))
    w_in_all, wo_all = [with_own(z, own) for z, own in zip(_exchange_wait(w2_sent, h2, True, "w2_wait"), w2)]
    w_in_t = w_in_all[:, :Ws, :].reshape(-1, D).astype(MXU_DTYPE)
    wo = full(wo_all)
    w_in_re = jnp.concatenate([w_in_t[:COL_Z + GDN_WIDTH], w_in_t[D_IN - POOL_WIDTH:],
                               w_in_t[4 * GDN_WIDTH:4 * GDN_WIDTH + 2 * GDN_HEADS],
                               jnp.zeros((128 - 2 * GDN_HEADS, D), MXU_DTYPE)], axis=0)
    proj = _matmul([(h2, w_in_re)], NT, F32, "proj_in", tm=512, tn=D_IN_PAD, tk=D)
    qkv, bg = _gdn_prep(proj, conv_full, a_log_l, dt_bias_l, "gdn_prep")
    tinv, u_c, w_c, qd_c, kd_c, p_c, cd_c = _gdn_chunk_fwd(qkv, bg, "gdn_chunk_fwd")
    o, s_all, vn_c = _gdn_scan_fwd(u_c, w_c, qd_c, kd_c, p_c, cd_c, "gdn_scan_fwd")
    mix_in = _mix_post(o, proj, gdn_norm, pool_w_m, pool_scale, "mix_post")
    mixed, x2, h3 = _matmul_resid_norm_mod(mix_in, wo, x1, gt2, 1.0, norm_ffn2, sc3, sh3, "mix_out")
    wg2_t, wu2_t, wd2 = [full(with_own(z, own))
                         for z, own in zip(_exchange_wait(w3_sent, h3, True, "w3_wait"), w3)]
    g3, u3, a3, loss_row, d3, d_final, dy3, dgt3 = _swiglu_fwd(
        h3, wg2_t, wu2_t, wd2, "ffn2_fwd_loss", loss_out=(x2, gt3, final_norm.reshape(1, D), target))

    dg3, du3, d2, d_n3, dsc3, dsh3, dmixed, dgt2 = _swiglu_bwd(
        dy3, wd2, g3, u3, wg2_t, wu2_t, "ffn2_bwd_norm3_bwd", (x2, norm_ffn2, sc3), d3,
        produced_by=(mixed, gt2, 1.0))
    d_wd2 = _weight_grad(a3, dy3, "ffn2_dwd")
    d_wg2 = _weight_grad(dg3, h3, "ffn2_dwg")
    d_wu2 = _weight_grad(du3, h3, "ffn2_dwu")
    dmix_in = _matmul([(dmixed, wo)], NT, F32, "mix_out_bwd", tm=512, tn=GDN_WIDTH + POOL_WIDTH, tk=D)
    d_wo = _matmul([(mix_in, dmixed)], TN, WIRE_DTYPE, "mix_dwo", tm=GDN_WIDTH + POOL_WIDTH, tn=D, tk=1024)
    do, dzp, d_gn, d_pw, d_ps = _mix_post_bwd(dmix_in, o, proj, gdn_norm, pool_w_m, pool_scale, "mix_post_bwd")
    dvn, dw_c, dqd, dkd, dp_c, dcd = _gdn_scan_bwd(do, w_c, qd_c, kd_c, p_c, cd_c, s_all, vn_c, "gdn_scan_bwd")
    dqkv, dbg = _gdn_chunk_bwd(qkv, bg, tinv, u_c, w_c, dvn, dw_c, dqd, dkd, dp_c, dcd, "gdn_chunk_bwd")
    dproj, d_conv, d_al, d_dtb = _gdn_prep_bwd(proj, conv_full, a_log_l, dt_bias_l, dqkv, dbg, dzp, "gdn_prep_bwd")
    d_win_re = _matmul([(dproj, h2)], TN, WIRE_DTYPE, "proj_in_dw", tm=D_IN_PAD, tn=D, tk=1024)
    d_win_t = jnp.concatenate([d_win_re[:COL_Z + GDN_WIDTH], d_win_re[COL_BA:COL_BA + 2 * GDN_HEADS],
                               d_win_re[COL_P:COL_P + POOL_WIDTH]], axis=0)
    d_win_blocks = jnp.pad(d_win_t.reshape(N_DEV, Ws, D), ((0, 0), (0, Ws_pad - Ws), (0, 0)))
    parts23 = jnp.concatenate(
        [wire(d_win_blocks), wire(d_wo.reshape(N_DEV, Os, D)), wire(d_wg2.reshape(N_DEV, Fs, D)),
         wire(d_wu2.reshape(N_DEV, Fs, D)), wire(d_wd2.reshape(N_DEV, Fs, D))], axis=1)
    own23 = lax.dynamic_index_in_dim(parts23, me, 0, keepdims=False)
    g23_sent = _exchange_start([parts23], False, no_dep, "g23_start")
    d1, d_n2, dsc2, dsh2, dy1, dgt1 = _norm_bwd((dproj, w_in_re), x1, norm_mix, sc2 + token(g23_sent), d2,
                                                "proj_in_bwd_norm2_bwd", produced_by=(y1, gt1, 0.5))
    dg1, du1, grad_x, d_n1, dsc1, dsh1 = _swiglu_bwd(
        dy1, wd1, g1, u1, wg1_t, wu1_t, "ffn1_bwd_norm1_bwd", (x0, norm_ffn1, sc1), d1)

    dmod = jnp.concatenate([dsh1, dsc1, dgt1, dsh2, dsc2, dgt2, dsh3, dsc3, dgt3], axis=0)
    small_rows = [dmod.reshape(-1), d_n1[0], d_n2[0], d_n3[0], d_final[0], d_gn[0], d_ps[0],
                  d_al[0, GDN_HEADS:2 * GDN_HEADS], d_dtb[0, GDN_HEADS:2 * GDN_HEADS], loss_row[0, :1],
                  d_conv.reshape(-1), d_pw.reshape(-1)]
    lanes = 1024
    small_rows = [_rows_of(r, lanes) for r in small_rows]
    n_rows = [r.shape[0] for r in small_rows]
    row_off = [sum(n_rows[:i]) for i in range(len(n_rows))]
    total = -(-sum(n_rows) // 8) * 8
    slab = _pad_rows(jnp.concatenate(small_rows, axis=0), total)
    slab_all, = _all_gather([slab], "gather_small_grads")
    summed = _sum_parts(slab_all, "sum_small_grads")

    def piece(idx, n):
        return summed[row_off[idx]:row_off[idx] + n_rows[idx]].reshape(-1)[:n]

    g_b_ada = piece(0, N_MOD * D).reshape(1, N_MOD * D)
    g_n1, g_n2, g_n3 = piece(1, D).reshape(1, D), piece(2, D).reshape(1, D), piece(3, D).reshape(1, D)
    g_final = piece(4, D)
    g_gn = piece(5, HEAD_DIM).reshape(1, HEAD_DIM)
    g_ps = piece(6, POOL_WIDTH).reshape(1, POOL_WIDTH)
    g_al = piece(7, GDN_HEADS).reshape(1, GDN_HEADS)
    g_dtb = piece(8, GDN_HEADS).reshape(1, GDN_HEADS)
    loss = piece(9, 1)[0]
    g_conv = lax.dynamic_slice(piece(10, CONV_K * QKV_WIDTH).reshape(1, CONV_K, QKV_WIDTH), (0, 0, me * Cs),
                               (1, CONV_K, Cs))
    g_pw = piece(11, POOL_GROUPS * 128 * 128).reshape(1, POOL_GROUPS, 128, 128)

    dmod_all = slab_all[:, row_off[0]:row_off[0] + n_rows[0], :].reshape(N_DEV, -1)[:, :N_MOD * D]
    g_w_ada = _w_ada_grad(c_all, lax.dynamic_slice(dmod_all, (0, me * Ms), (N_DEV, Ms)), "w_ada_grad")[None]

    def send_ffn1(a, b, which, dep):
        parts = _weight_grad(a, b, f"ffn1_{which}", dep=dep).reshape(N_DEV, Fs, D)
        own = lax.dynamic_index_in_dim(parts, me, 0, keepdims=False)
        return _exchange_start([parts], False, no_dep, f"g1_{which}_start"), own

    g1_wg, own_wg = send_ffn1(dg1, h1, "dwg", summed)
    g1_wu, own_wu = send_ffn1(du1, h1, "dwu", g1_wg[4])
    g1_wd, own_wd = send_ffn1(a1, dy1, "dwd", g1_wu[4])

    big23 = _sum_parts(with_own(_exchange_wait(g23_sent, g1_wd[4], False, "g23_wait")[0], own23), "sum_grads23")
    g_rows = dict(w_in=big23[:Ws], w_out=big23[off23[1]:off23[1] + Os],
                  ffn2_gate=big23[off23[2]:off23[2] + Fs], ffn2_up=big23[off23[3]:off23[3] + Fs],
                  ffn2_down=big23[off23[4]:off23[4] + Fs])
    column_sharded = ("w_in", "ffn1_gate", "ffn1_up", "ffn2_gate", "ffn2_up")

    names = ["w_ada", "b_ada", "norm_ffn1", "ffn1_gate", "ffn1_up", "ffn1_down", "norm_mix", "w_in", "conv_w",
             "a_log", "dt_bias", "gdn_norm", "pool_w", "pool_scale", "w_out", "norm_ffn2", "ffn2_gate", "ffn2_up",
             "ffn2_down", "final_norm"]
    weights = dict(zip(names, [w_ada, b_ada, norm_ffn1, ffn1_gate, ffn1_up, ffn1_down, norm_mix, w_in, conv_w,
                               a_log, dt_bias, gdn_norm, pool_w, pool_scale, w_out, norm_ffn2, ffn2_gate, ffn2_up,
                               ffn2_down, final_norm]))
    ms = dict(zip(names, [m_w_ada, m_b_ada, m_norm_ffn1, m_ffn1_gate, m_ffn1_up, m_ffn1_down, m_norm_mix, m_w_in,
                          m_conv_w, m_a_log, m_dt_bias, m_gdn_norm, m_pool_w, m_pool_scale, m_w_out, m_norm_ffn2,
                          m_ffn2_gate, m_ffn2_up, m_ffn2_down, m_final_norm]))
    vs = dict(zip(names, [v_w_ada, v_b_ada, v_norm_ffn1, v_ffn1_gate, v_ffn1_up, v_ffn1_down, v_norm_mix, v_w_in,
                          v_conv_w, v_a_log, v_dt_bias, v_gdn_norm, v_pool_w, v_pool_scale, v_w_out, v_norm_ffn2,
                          v_ffn2_gate, v_ffn2_up, v_ffn2_down, v_final_norm]))
    grads = dict(w_ada=g_w_ada, b_ada=g_b_ada, norm_ffn1=g_n1, norm_mix=g_n2, conv_w=g_conv,
                 a_log=g_al, dt_bias=g_dtb, gdn_norm=g_gn, pool_w=g_pw, pool_scale=g_ps,
                 norm_ffn2=g_n3, final_norm=g_final)
    delta, new_m, new_v = {}, {}, {}

    def adamw_big(n):
        if n in column_sharded:
            view, back = (lambda a: a[0].T), (lambda a: a.T[None])
        else:
            view, back = (lambda a: a[0]), (lambda a: a[None])
        g = g_rows[n] if n in g_rows else view(grads[n])
        d_, m_, v_ = _adamw(view(weights[n]), g, view(ms[n]), view(vs[n]), f"adamw_{n}")
        grads[n], delta[n], new_m[n], new_v[n] = back(g), back(d_), back(m_), back(v_)

    early = ["w_ada", "w_in", "w_out", "ffn2_gate", "ffn2_up", "ffn2_down"]
    late = ["ffn1_gate", "ffn1_up", "ffn1_down"]
    for n in early:
        adamw_big(n)
    done = sum(delta[n][0, :1, :1] for n in early)

    def arrived(started, own, which):
        landed, = _exchange_wait(started, done, False, f"g1_{which}_wait")
        return _sum_parts(with_own(landed, own), f"sum_{which}")

    g_rows["ffn1_gate"] = arrived(g1_wg, own_wg, "dwg")
    g_rows["ffn1_up"] = arrived(g1_wu, own_wu, "dwu")
    g_rows["ffn1_down"] = arrived(g1_wd, own_wd, "dwd")
    for n in late:
        adamw_big(n)
    small_names = [n for n in names if n not in early + late]
    two_d = lambda a: a.reshape(-1, a.shape[-1])
    small_out = _adamw_small(*[[two_d(src[n]) for n in small_names] for src in (weights, grads, ms, vs)],
                             "adamw_small")
    for dst, outs in zip((delta, new_m, new_v), small_out):
        for n, a in zip(small_names, outs):
            dst[n] = a.reshape(weights[n].shape)

    return (loss, grad_x[None], *[grads[n] for n in names], *[delta[n] for n in names],
            *[new_m[n] for n in names], *[new_v[n] for n in names])
```

```python
import functools

import jax
import jax.numpy as jnp
from jax import lax
from jax.experimental import pallas as pl
from jax.experimental.pallas import tpu as pltpu

F32 = jnp.float32
MXU_DTYPE = jnp.bfloat16
WIRE_DTYPE = jnp.bfloat16
EPS = 1e-6
N_DEV = 8
GDN_HEADS = 4
HEAD_DIM = 128
GDN_WIDTH = GDN_HEADS * HEAD_DIM
POOL_WINDOWS = (2, 4, 8, 16)
POOL_GROUPS = len(POOL_WINDOWS)
POOL_WIDTH = 512
CONV_K = 4
CHUNK = 64
QKV_WIDTH = 3 * GDN_WIDTH
D_IN = 4 * GDN_WIDTH + 2 * GDN_HEADS + POOL_WIDTH
D_IN_PAD = 4 * GDN_WIDTH + POOL_WIDTH + 128
COL_Z = QKV_WIDTH
COL_P = 4 * GDN_WIDTH
COL_BA = 4 * GDN_WIDTH + POOL_WIDTH
N_MOD = 9
HALO = 16
VMEM_LIMIT = 56 * 1024 * 1024
ADAM_LR, ADAM_B1, ADAM_B2, ADAM_EPS, ADAM_WD, ADAM_STEP = 0.001, 0.9, 0.999, 1e-08, 0.01, 10
FFN_TOKEN_TILE = 256
FFN_HIDDEN_TILE = 1408
CHUNKS_PER_STEP = 4
SCAN_CHUNKS_PER_STEP = 8

NT = (((1,), (1,)), ((), ()))
NN = (((1,), (0,)), ((), ()))
TN = (((0,), (0,)), ((), ()))


def _params(*sem):
    return pltpu.CompilerParams(dimension_semantics=tuple(sem), vmem_limit_bytes=VMEM_LIMIT)


def _dot(a, b, dims):
    return lax.dot_general(a, b, dims, preferred_element_type=F32)


def _mdot(a, b, dims):
    return _dot(a.astype(MXU_DTYPE), b.astype(MXU_DTYPE), dims)


def _split(a):
    hi = a.astype(jnp.bfloat16)
    return hi, (a - hi.astype(F32)).astype(jnp.bfloat16)


def _dot3(a, b, dims):
    (ah, al), (bh, bl) = a, b
    return (_dot(al, bh, dims) + _dot(ah, bl, dims)) + _dot(ah, bh, dims)


def _sigmoid(v):
    return 0.5 * jnp.tanh(0.5 * v) + 0.5


def _softplus(v):
    return jnp.maximum(v, 0.0) + jnp.log(1.0 + jnp.exp(-jnp.abs(v)))


def _shift_rows(v, s):
    n = v.shape[0]
    s = s % n
    return v if s == 0 else pltpu.roll(v, s, 0)


def _tile(n, want):
    t = min(n, want)
    while n % t:
        t //= 2
    return t


def _all_gather(blocks, name, dep=None):
    n = len(blocks)

    def body(*refs):
        x_refs, out_refs = refs[:n], refs[-3 - n:-3]
        send_sems, recv_sems, local_sems = refs[-3:]
        x, y, c = lax.axis_index("x"), lax.axis_index("y"), lax.axis_index("c")
        me, sibling = (x, y, c), (x, y, 1 - c)
        chips = [(1 - x, y), (x, 1 - y), (1 - x, 1 - y)]

        def copy(a, k, blk, to, own=False):
            rows = out_refs[a].at[4 * blk[0] + 2 * blk[1] + blk[2]]
            return pltpu.make_async_remote_copy(
                src_ref=x_refs[a] if own else rows, dst_ref=rows,
                send_sem=send_sems.at[7 * a + k], recv_sem=recv_sems.at[7 * a + k],
                device_id=to, device_id_type=pl.DeviceIdType.MESH)

        mine = [pltpu.make_async_copy(x_refs[a], out_refs[a].at[4 * x + 2 * y + c], local_sems.at[a])
                for a in range(n)]
        for cp in mine:
            cp.start()
        sent = []
        for a in range(n):
            sent.append(copy(a, 0, me, sibling, own=True))
            sent += [copy(a, 1 + j, me, (*chip, c), own=True) for j, chip in enumerate(chips)]
        for cp in sent:
            cp.start()
        for a in range(n):
            for j, chip in enumerate(chips):
                copy(a, 1 + j, (*chip, c), me).wait_recv()
                passed = copy(a, 4 + j, (*chip, c), sibling)
                passed.start()
                sent.append(passed)
        for a in range(n):
            copy(a, 0, sibling, me).wait_recv()
            for j, chip in enumerate(chips):
                copy(a, 4 + j, (*chip, 1 - c), me).wait_recv()
        for cp in sent:
            cp.wait_send()
        for cp in mine:
            cp.wait()

    hbm = pl.BlockSpec(memory_space=pltpu.HBM)
    return pl.pallas_call(
        body, name=name,
        out_shape=[jax.ShapeDtypeStruct((N_DEV,) + b.shape, b.dtype) for b in blocks],
        in_specs=[hbm] * n + [pl.BlockSpec(memory_space=pl.ANY)] * (dep is not None),
        out_specs=[hbm] * n,
        scratch_shapes=[pltpu.SemaphoreType.DMA((7 * n,)), pltpu.SemaphoreType.DMA((7 * n,)),
                        pltpu.SemaphoreType.DMA((n,))],
    )(*(list(blocks) + ([] if dep is None else [dep])))


_HBM = pl.BlockSpec(memory_space=pltpu.HBM)
_SEM = pl.BlockSpec(memory_space=pltpu.SEMAPHORE)
_ANY = pl.BlockSpec(memory_space=pl.ANY)
_EFFECT = pltpu.SideEffectType.DATAFLOW_SIDE_EFFECTING
_FLIPS = [(0, 0, 1), (0, 1, 0), (0, 1, 1), (1, 0, 0), (1, 0, 1), (1, 1, 0), (1, 1, 1)]


def _peers():
    x, y, c = lax.axis_index("x"), lax.axis_index("y"), lax.axis_index("c")
    return 4 * x + 2 * y + c, [(1 - x if fx else x, 1 - y if fy else y, 1 - c if fc else c)
                               for fx, fy, fc in _FLIPS]


def _exchange_start(srcs, gather, dep, name):
    n = len(srcs)
    lands = [(N_DEV,) + tuple(s.shape if gather else s.shape[1:]) for s in srcs]

    def body(*refs):
        src_refs, land_refs = refs[:n], refs[n:2 * n]
        send_sems, recv_sems = refs[2 * n + 1], refs[2 * n + 2]
        token = refs[-1]
        me, peers = _peers()
        for a in range(n):
            for k, (px, py, pc) in enumerate(peers):
                pltpu.make_async_remote_copy(
                    src_ref=src_refs[a] if gather else src_refs[a].at[4 * px + 2 * py + pc],
                    dst_ref=land_refs[a].at[me],
                    send_sem=send_sems.at[7 * a + k], recv_sem=recv_sems.at[7 * a + k],
                    device_id=(px, py, pc), device_id_type=pl.DeviceIdType.MESH).start()
        token[...] = jnp.zeros_like(token)

    srcs = [pltpu.with_memory_space_constraint(s, pltpu.HBM) for s in srcs]
    empties = [pltpu.with_memory_space_constraint(lax.empty(shape, s.dtype), pltpu.HBM)
               for shape, s in zip(lands, srcs)]
    out = pl.pallas_call(
        body, name=name,
        out_shape=(pltpu.SemaphoreType.DMA((7 * n,)), pltpu.SemaphoreType.DMA((7 * n,)),
                   *[pltpu.HBM(shape, s.dtype) for shape, s in zip(lands, srcs)],
                   jax.ShapeDtypeStruct((8, 128), F32)),
        in_specs=(*[_HBM] * (2 * n), _ANY),
        out_specs=(_SEM, _SEM, *[_HBM] * n, pl.BlockSpec(memory_space=pltpu.VMEM)),
        input_output_aliases={n + a: 2 + a for a in range(n)},
        compiler_params=pltpu.CompilerParams(has_side_effects=_EFFECT),
    )(*srcs, *empties, dep)
    return out[0], out[1], srcs, list(out[2:2 + n]), out[-1]


def _exchange_wait(started, after, gather, name):
    send_sems, recv_sems, srcs, lands, _ = started
    n = len(srcs)

    def body(*refs):
        src_refs, land_refs = refs[:n], refs[n:2 * n]
        send_sems, recv_sems = refs[2 * n], refs[2 * n + 1]
        _, peers = _peers()
        for a in range(n):
            for k, peer in enumerate(peers):
                copy = pltpu.make_async_remote_copy(
                    src_ref=src_refs[a] if gather else src_refs[a].at[0], dst_ref=land_refs[a].at[0],
                    send_sem=send_sems.at[7 * a + k], recv_sem=recv_sems.at[7 * a + k],
                    device_id=peer, device_id_type=pl.DeviceIdType.MESH)
                copy.wait_send()
                copy.wait_recv()

    out = pl.pallas_call(
        body, name=name,
        out_shape=[pltpu.HBM(z.shape, z.dtype) for z in lands],
        in_specs=(*[_HBM] * (2 * n), _SEM, _SEM, _ANY), out_specs=[_HBM] * n,
        input_output_aliases={n + a: a for a in range(n)},
        compiler_params=pltpu.CompilerParams(has_side_effects=_EFFECT),
    )(*srcs, *lands, send_sems, recv_sems, after)
    return list(out)


def _matmul(pairs, dims, out_dtype, name, tm=512, tn=512, tk=512, dep=None):
    a0, b0 = pairs[0]
    if dims == TN:
        K, M = a0.shape
    else:
        M, K = a0.shape
    N = b0.shape[0] if dims == NT else b0.shape[1]
    tm, tn, tk = _tile(M, tm), _tile(N, tn), _tile(K, tk)
    nk = K // tk
    n_pairs = len(pairs)
    n_in = 2 * n_pairs + (dep is not None)

    def body(*refs):
        out_ref = refs[n_in]

        def product():
            total = _dot(refs[0][...], refs[1][...], dims)
            for p in range(1, n_pairs):
                total += _dot(refs[2 * p][...], refs[2 * p + 1][...], dims)
            return total

        if nk == 1:
            out_ref[...] = product().astype(out_ref.dtype)
            return
        acc_ref = refs[n_in + 1]
        k = pl.program_id(2)

        @pl.when(k == 0)
        def _():
            acc_ref[...] = product()

        @pl.when((k > 0) & (k < nk - 1))
        def _():
            acc_ref[...] += product()

        @pl.when(k == nk - 1)
        def _():
            out_ref[...] = (acc_ref[...] + product()).astype(out_ref.dtype)

    if dims == TN:
        a_spec = pl.BlockSpec((tk, tm), lambda i, j, k: (k, i))
    else:
        a_spec = pl.BlockSpec((tm, tk), lambda i, j, k: (i, k))
    if dims == NT:
        b_spec = pl.BlockSpec((tn, tk), lambda i, j, k: (j, k))
    else:
        b_spec = pl.BlockSpec((tk, tn), lambda i, j, k: (k, j))
    args, specs = [], []
    for a, b in pairs:
        args += [a, b]
        specs += [a_spec, b_spec]
    if dep is not None:
        args.append(dep)
        specs.append(_ANY)
    return pl.pallas_call(
        body, name=name, grid=(M // tm, N // tn, nk),
        in_specs=specs, out_specs=pl.BlockSpec((tm, tn), lambda i, j, k: (i, j)),
        out_shape=jax.ShapeDtypeStruct((M, N), out_dtype),
        scratch_shapes=[pltpu.VMEM((tm, tn), F32)] * (nk > 1),
        compiler_params=_params("parallel", "parallel", "arbitrary"),
    )(*args)


def _vec_spec(d):
    return pl.BlockSpec((1, d), lambda i: (0, 0))


def _matmul_resid_norm_mod(a, b, x, gate, coef, nw, scale, shift, name):
    T, K = a.shape
    D = b.shape[1]
    tm = _tile(T, 512)

    def body(a_ref, b_ref, x_ref, g_ref, nw_ref, sc_ref, sh_ref, y_ref, xo_ref, h_ref):
        y = _dot(a_ref[...], b_ref[...], NN)
        y_ref[...] = y
        xf = x_ref[...] + (coef * g_ref[...]) * y
        xo_ref[...] = xf
        r = lax.rsqrt(jnp.mean(xf * xf, axis=-1, keepdims=True) + EPS)
        h_ref[...] = ((xf * r) * nw_ref[...] * (1.0 + sc_ref[...]) + sh_ref[...]).astype(h_ref.dtype)

    row = pl.BlockSpec((tm, D), lambda i: (i, 0))
    vec = _vec_spec(D)
    return pl.pallas_call(
        body, name=name, grid=(T // tm,),
        in_specs=[pl.BlockSpec((tm, K), lambda i: (i, 0)), _resident((K, D)), row, vec, vec, vec, vec],
        out_specs=[row, row, row],
        out_shape=[jax.ShapeDtypeStruct((T, D), F32), jax.ShapeDtypeStruct((T, D), F32),
                   jax.ShapeDtypeStruct((T, D), MXU_DTYPE)],
        compiler_params=_params("parallel"),
    )(a, b, x, gate, nw, scale, shift)


def _norm_bwd(dh, x, nw, scale, dres, name, produced_by=None):
    T, D = x.shape
    tm = _tile(T, 512)

    def body(*refs):
        if isinstance(dh, tuple):
            dh_value = _dot(refs[0][...], refs[1][...], NN)
            refs = refs[1:]
        else:
            dh_value = refs[0][...]
        _, x_ref, nw_ref, sc_ref, dr_ref = refs[:5]
        n_in = 5 if produced_by is None else 7
        dx_ref, dnw_ref, dsc_ref, dsh_ref = refs[n_in:n_in + 4]

        @pl.when(pl.program_id(0) == 0)
        def _():
            dnw_ref[...] = jnp.zeros_like(dnw_ref)
            dsc_ref[...] = jnp.zeros_like(dsc_ref)
            dsh_ref[...] = jnp.zeros_like(dsh_ref)
            if produced_by is not None:
                refs[n_in + 5][...] = jnp.zeros_like(refs[n_in + 5])

        xf, dh_ = x_ref[...], dh_value
        r = lax.rsqrt(jnp.mean(xf * xf, axis=-1, keepdims=True) + EPS)
        xn = xf * r
        one_sc = 1.0 + sc_ref[...]
        dsh_ref[...] += jnp.sum(dh_, axis=0, keepdims=True)
        t = dh_ * xn
        dsc_ref[...] += jnp.sum(t, axis=0, keepdims=True) * nw_ref[...]
        dnw_ref[...] += jnp.sum(t, axis=0, keepdims=True) * one_sc
        dxn = dh_ * (nw_ref[...] * one_sc)
        dx = dr_ref[...] + r * (dxn - xn * jnp.mean(dxn * xn, axis=-1, keepdims=True))
        dx_ref[...] = dx
        if produced_by is not None:
            y_ref, g_ref, dy_ref, dg_ref = refs[5], refs[6], refs[n_in + 4], refs[n_in + 5]
            dy_ref[...] = ((produced_by[2] * g_ref[...]) * dx).astype(dy_ref.dtype)
            dg_ref[...] += produced_by[2] * jnp.sum(dx * y_ref[...], axis=0, keepdims=True)

    row = pl.BlockSpec((tm, D), lambda i: (i, 0))
    vec = _vec_spec(D)
    vec_out = jax.ShapeDtypeStruct((1, D), F32)
    if isinstance(dh, tuple):
        k_dim = dh[0].shape[1]
        args, in_specs = [dh[0], dh[1]], [pl.BlockSpec((tm, k_dim), lambda i: (i, 0)), _resident((k_dim, D))]
    else:
        args, in_specs = [dh], [row]
    args, in_specs = args + [x, nw, scale, dres], in_specs + [row, vec, vec, row]
    out_specs, out_shape = [row, vec, vec, vec], [jax.ShapeDtypeStruct((T, D), F32), vec_out, vec_out, vec_out]
    if produced_by is not None:
        args += [produced_by[0], produced_by[1]]
        in_specs += [row, vec]
        out_specs += [row, vec]
        out_shape += [jax.ShapeDtypeStruct((T, D), MXU_DTYPE), vec_out]
    return pl.pallas_call(
        body, name=name, grid=(T // tm,),
        in_specs=in_specs, out_specs=out_specs, out_shape=out_shape,
        compiler_params=_params("arbitrary"),
    )(*args)


def _rms(xf):
    r = lax.rsqrt(jnp.mean(xf * xf, axis=-1, keepdims=True) + EPS)
    return r, xf * r


def _norm_bwd_math(dh, xf, nw, sc):
    r, xn = _rms(xf)
    t = dh * xn
    dxn = dh * (nw * (1.0 + sc))
    dx = r * (dxn - xn * jnp.mean(dxn * xn, axis=-1, keepdims=True))
    return dx, jnp.sum(dh, axis=0, keepdims=True), jnp.sum(t, axis=0, keepdims=True)


def _swiglu_fwd(h, wg_t, wu_t, wd, name, norm_in=None, resid_out=None, loss_out=None):
    T, D = h.shape
    Fdim = wd.shape[0]
    tm, tf = _tile(T, FFN_TOKEN_TILE), _tile(Fdim, FFN_HIDDEN_TILE)
    row = pl.BlockSpec((tm, D), lambda i: (i, 0))
    frow = pl.BlockSpec((tm, Fdim), lambda i: (i, 0))
    vec, wres = _vec_spec(D), _resident((Fdim, D))
    f32_row, mxu_row = jax.ShapeDtypeStruct((T, D), F32), jax.ShapeDtypeStruct((T, D), MXU_DTYPE)
    vec_out = jax.ShapeDtypeStruct((1, D), F32)
    args, in_specs = [h, wg_t, wu_t, wd], [row, wres, wres, wres]
    out_shape, out_specs = [jax.ShapeDtypeStruct((T, Fdim), MXU_DTYPE)] * 3, [frow] * 3
    if norm_in is not None:
        args += list(norm_in)
        in_specs += [vec] * 3
        out_shape, out_specs = out_shape + [mxu_row], out_specs + [row]
    if resid_out is not None:
        x, gate, coef, nw, sc, sh = resid_out
        args += [x, gate, nw, sc, sh]
        in_specs += [row, vec, vec, vec, vec]
        out_shape, out_specs = out_shape + [f32_row, f32_row, mxu_row], out_specs + [row, row, row]
    if loss_out is not None:
        args += list(loss_out)
        in_specs += [row, vec, vec, row]
        out_shape += [jax.ShapeDtypeStruct((1, 128), F32), f32_row, vec_out, mxu_row, vec_out]
        out_specs += [pl.BlockSpec((1, 128), lambda i: (0, 0)), row, vec, row, vec]

    def body(*refs):
        it = iter(refs)
        h_ref, wg_ref, wu_ref, wd_ref = next(it), next(it), next(it), next(it)
        norm_refs = [next(it) for _ in range(3)] if norm_in is not None else None
        resid_refs = [next(it) for _ in range(5)] if resid_out is not None else None
        loss_refs = [next(it) for _ in range(4)] if loss_out is not None else None
        g_ref, u_ref, a_ref = next(it), next(it), next(it)
        if norm_in is not None:
            nw_ref, sc_ref, sh_ref = norm_refs
            _, xn = _rms(h_ref[...])
            hh = (xn * nw_ref[...] * (1.0 + sc_ref[...]) + sh_ref[...]).astype(MXU_DTYPE)
            next(it)[...] = hh
        else:
            hh = h_ref[...]
        y = None
        for k in range(Fdim // tf):
            ks = slice(k * tf, (k + 1) * tf)
            g = _dot(hh, wg_ref[ks, :], NT)
            u = _dot(hh, wu_ref[ks, :], NT)
            a = ((g * _sigmoid(g)) * u).astype(a_ref.dtype)
            g_ref[:, ks] = g.astype(g_ref.dtype)
            u_ref[:, ks] = u.astype(u_ref.dtype)
            a_ref[:, ks] = a
            part = _dot(a, wd_ref[ks, :], NN)
            y = part if y is None else y + part
        if resid_out is not None:
            x_ref, gt_ref, nw_ref, sc_ref, sh_ref = resid_refs
            y_ref, xo_ref, hn_ref = next(it), next(it), next(it)
            y_ref[...] = y
            xf = x_ref[...] + (resid_out[2] * gt_ref[...]) * y
            xo_ref[...] = xf
            _, xn = _rms(xf)
            hn_ref[...] = (xn * nw_ref[...] * (1.0 + sc_ref[...]) + sh_ref[...]).astype(hn_ref.dtype)
        if loss_out is not None:
            x_ref, gt_ref, fw_ref, t_ref = loss_refs
            loss_ref, dx_ref, dfw_ref, dy_ref, dg_ref = [next(it) for _ in range(5)]

            @pl.when(pl.program_id(0) == 0)
            def _():
                loss_ref[...] = jnp.zeros_like(loss_ref)
                dfw_ref[...] = jnp.zeros_like(dfw_ref)
                dg_ref[...] = jnp.zeros_like(dg_ref)

            r, xn = _rms(x_ref[...] + (0.5 * gt_ref[...]) * y)
            err = xn * fw_ref[...] - t_ref[...]
            per_tok = jnp.mean(err * err, axis=-1, keepdims=True)
            loss_ref[...] += 0.5 * jnp.sum(per_tok, axis=0, keepdims=True)
            d_out = err * (1.0 / D)
            dfw_ref[...] += jnp.sum(d_out * xn, axis=0, keepdims=True)
            dxn = d_out * fw_ref[...]
            dx = r * (dxn - xn * jnp.mean(dxn * xn, axis=-1, keepdims=True))
            dx_ref[...] = dx
            dy_ref[...] = ((0.5 * gt_ref[...]) * dx).astype(dy_ref.dtype)
            dg_ref[...] += 0.5 * jnp.sum(dx * y, axis=0, keepdims=True)

    return pl.pallas_call(
        body, name=name, grid=(T // tm,), in_specs=in_specs, out_specs=out_specs, out_shape=out_shape,
        compiler_params=_params("arbitrary" if loss_out is not None else "parallel"),
    )(*args)


def _swiglu_bwd(dy, wd, g, u, wg_t, wu_t, name, norm_in, dres, produced_by=None):
    T, D = dy.shape
    Fdim = wd.shape[0]
    tm, tf = _tile(T, FFN_TOKEN_TILE), _tile(Fdim, FFN_HIDDEN_TILE)
    row = pl.BlockSpec((tm, D), lambda i: (i, 0))
    frow = pl.BlockSpec((tm, Fdim), lambda i: (i, 0))
    vec, wres = _vec_spec(D), _resident((Fdim, D))
    vec_out = jax.ShapeDtypeStruct((1, D), F32)
    args, in_specs = [dy, wd, g, u, wg_t, wu_t, *norm_in, dres], [row, wres, frow, frow, wres, wres, row, vec, vec, row]
    out_shape = [jax.ShapeDtypeStruct((T, Fdim), MXU_DTYPE)] * 2 + [jax.ShapeDtypeStruct((T, D), F32)] + [vec_out] * 3
    out_specs = [frow, frow, row, vec, vec, vec]
    if produced_by is not None:
        args += [produced_by[0], produced_by[1]]
        in_specs += [row, vec]
        out_shape += [jax.ShapeDtypeStruct((T, D), MXU_DTYPE), vec_out]
        out_specs += [row, vec]

    def body(*refs):
        it = iter(refs)
        dy_ref, wd_ref, g_ref, u_ref, wg_ref, wu_ref, x_ref, nw_ref, sc_ref, dr_ref = [next(it) for _ in range(10)]
        prev_refs = [next(it), next(it)] if produced_by is not None else None
        dg_ref, du_ref, dx_ref, dnw_ref, dsc_ref, dsh_ref = [next(it) for _ in range(6)]
        prev_out = [next(it), next(it)] if produced_by is not None else None

        @pl.when(pl.program_id(0) == 0)
        def _():
            dnw_ref[...] = jnp.zeros_like(dnw_ref)
            dsc_ref[...] = jnp.zeros_like(dsc_ref)
            dsh_ref[...] = jnp.zeros_like(dsh_ref)
            if produced_by is not None:
                prev_out[1][...] = jnp.zeros_like(prev_out[1])

        dyy = dy_ref[...]
        dh = None
        for k in range(Fdim // tf):
            ks = slice(k * tf, (k + 1) * tf)
            da = _dot(dyy, wd_ref[ks, :], NT)
            gg = g_ref[:, ks].astype(F32)
            sig = _sigmoid(gg)
            dg = (da * u_ref[:, ks].astype(F32) * (sig * (1.0 + gg * (1.0 - sig)))).astype(dg_ref.dtype)
            du = (da * (gg * sig)).astype(du_ref.dtype)
            dg_ref[:, ks] = dg
            du_ref[:, ks] = du
            part = _dot(dg, wg_ref[ks, :], NN) + _dot(du, wu_ref[ks, :], NN)
            dh = part if dh is None else dh + part
        dx_norm, dsh_row, t_row = _norm_bwd_math(dh, x_ref[...], nw_ref[...], sc_ref[...])
        dsh_ref[...] += dsh_row
        dsc_ref[...] += t_row * nw_ref[...]
        dnw_ref[...] += t_row * (1.0 + sc_ref[...])
        dx = dr_ref[...] + dx_norm
        dx_ref[...] = dx
        if produced_by is not None:
            prev_out[0][...] = ((produced_by[2] * prev_refs[1][...]) * dx).astype(prev_out[0].dtype)
            prev_out[1][...] += produced_by[2] * jnp.sum(dx * prev_refs[0][...], axis=0, keepdims=True)

    return pl.pallas_call(
        body, name=name, grid=(T // tm,), in_specs=in_specs, out_specs=out_specs, out_shape=out_shape,
        compiler_params=_params("arbitrary"),
    )(*args)


def _resident(shape):
    return pl.BlockSpec(shape, lambda i: (0,) * len(shape), pipeline_mode=pl.Buffered(1))


def _conv_act(window, w):
    y = window * w[CONV_K - 1:CONV_K, :]
    for j in range(CONV_K - 1):
        y += _shift_rows(window, CONV_K - 1 - j) * w[j:j + 1, :]
    return y


def _gdn_prep(proj, conv_w, a_log_l, dt_bias_l, name):
    T = proj.shape[0]
    tm = _tile(T, 256)
    hb = tm // 8

    def body(cur_ref, halo_ref, ba_ref, w_ref, al_ref, dtb_ref, qkv_ref, bg_ref):
        i = pl.program_id(0)
        halo = jnp.where(i == 0, 0.0, halo_ref[...])
        window = jnp.concatenate([halo, cur_ref[...]], axis=0)
        y = _conv_act(window, w_ref[...])[8:, :]
        act = y * _sigmoid(y)
        for hh in range(3 * GDN_HEADS):
            blk = act[:, hh * HEAD_DIM:(hh + 1) * HEAD_DIM]
            if hh < 2 * GDN_HEADS:
                rn = lax.rsqrt(jnp.sum(blk * blk, axis=-1, keepdims=True) + EPS)
                blk = blk * rn
                if hh < GDN_HEADS:
                    blk = blk * (HEAD_DIM ** -0.5)
            qkv_ref[:, hh * HEAD_DIM:(hh + 1) * HEAD_DIM] = blk
        ba = ba_ref[...]
        lane = lax.broadcasted_iota(jnp.int32, ba.shape, 1)
        beta = _sigmoid(ba)
        g = -jnp.exp(al_ref[...]) * _softplus(ba + dtb_ref[...])
        bg_ref[...] = jnp.where(lane < GDN_HEADS, beta, jnp.where(lane < 2 * GDN_HEADS, g, 0.0))

    return pl.pallas_call(
        body, name=name, grid=(T // tm,),
        in_specs=[pl.BlockSpec((tm, QKV_WIDTH), lambda i: (i, 0)),
                  pl.BlockSpec((8, QKV_WIDTH), lambda i: (jnp.maximum(i * hb - 1, 0), 0)),
                  pl.BlockSpec((tm, 128), lambda i: (i, COL_BA // 128)),
                  pl.BlockSpec((CONV_K, QKV_WIDTH), lambda i: (0, 0)),
                  pl.BlockSpec((1, 128), lambda i: (0, 0)), pl.BlockSpec((1, 128), lambda i: (0, 0))],
        out_specs=[pl.BlockSpec((tm, QKV_WIDTH), lambda i: (i, 0)), pl.BlockSpec((tm, 128), lambda i: (i, 0))],
        out_shape=[jax.ShapeDtypeStruct((T, QKV_WIDTH), F32), jax.ShapeDtypeStruct((T, 128), F32)],
        compiler_params=_params("parallel"),
    )(proj, proj, proj, conv_w, a_log_l, dt_bias_l)


def _chunk_cumsum(v, reverse=False):
    row = lax.broadcasted_iota(jnp.int32, v.shape, 0)
    s = 1
    while s < CHUNK:
        if reverse:
            v = v + jnp.where(row < CHUNK - s, _shift_rows(v, -s), 0.0)
        else:
            v = v + jnp.where(row >= s, _shift_rows(v, s), 0.0)
        s *= 2
    return v


def _row_form(cols):
    padded = jnp.concatenate([cols, jnp.zeros((128 - CHUNK, 128), F32)], axis=0)
    return padded.T[:, :CHUNK]


def _chunk_masks():
    ri = lax.broadcasted_iota(jnp.int32, (CHUNK, CHUNK), 0)
    ci = lax.broadcasted_iota(jnp.int32, (CHUNK, CHUNK), 1)
    return ri >= ci, ri > ci, (ri == ci).astype(F32)


def _unit_lower_inverses(ms, eye):
    rs = [eye - m for m in ms]
    ps = [_split(-m) for m in ms]
    s = 2
    while s < CHUNK:
        ps = [_split(_dot3(p, p, NN)) for p in ps]
        r_parts = [_split(r) for r in rs]
        rs = [r + _dot3(p, rp, NN) for r, p, rp in zip(rs, ps, r_parts)]
        s *= 2
    return rs


def _head_elementwise(k, beta, gc, gcr, causal):
    decay = jnp.where(causal, jnp.exp(jnp.where(causal, gc - gcr, 0.0)), 0.0)
    return decay, k * beta, jnp.exp(gc)


def _head_slices(hh):
    return (slice(hh * HEAD_DIM, (hh + 1) * HEAD_DIM),
            slice(GDN_WIDTH + hh * HEAD_DIM, GDN_WIDTH + (hh + 1) * HEAD_DIM),
            slice(2 * GDN_WIDTH + hh * HEAD_DIM, 2 * GDN_WIDTH + (hh + 1) * HEAD_DIM))


def _gdn_chunk_fwd(qkv, bg, name):
    T = qkv.shape[0]
    cb = _tile(T // CHUNK, CHUNKS_PER_STEP)
    rows = cb * CHUNK

    def body(qkv_ref, bg_ref, tinv_ref, u_ref, w_ref, qd_ref, kd_ref, p_ref, cd_ref):
        masks = _chunk_masks()
        causal, strict, eye = masks
        heads = []
        for ci in range(cb):
            rs = slice(ci * CHUNK, (ci + 1) * CHUNK)
            bgv = bg_ref[rs, :]
            gc_all = _chunk_cumsum(bgv)
            gc_rows = _row_form(gc_all)
            cd_ref[rs, :] = jnp.exp(jnp.broadcast_to(gc_all[CHUNK - 1:CHUNK, :], (CHUNK, 128)))
            for hh in range(GDN_HEADS):
                qs, ks, vs = _head_slices(hh)
                q, k, v = qkv_ref[rs, qs], qkv_ref[rs, ks], qkv_ref[rs, vs]
                beta = bgv[:, hh:hh + 1]
                gc = gc_all[:, GDN_HEADS + hh:GDN_HEADS + hh + 1]
                decay, kb, eg = _head_elementwise(k, beta, gc, gc_rows[GDN_HEADS + hh:GDN_HEADS + hh + 1, :], causal)
                hs = slice(hh * HEAD_DIM, (hh + 1) * HEAD_DIM)
                cs = slice(hh * CHUNK, (hh + 1) * CHUNK)
                qd_ref[rs, hs] = (q * eg).astype(qd_ref.dtype)
                kd_ref[rs, hs] = (k * jnp.exp(gc[CHUNK - 1:CHUNK, :] - gc)).astype(kd_ref.dtype)
                heads.append((rs, hs, cs, q, k, v * beta, kb, kb * eg, decay))
        kks = [_mdot(kb, k, NT) for (_, _, _, _, k, _, kb, _, _) in heads]
        qks = [_mdot(q, k, NT) for (_, _, _, q, k, _, _, _, _) in heads]
        tinvs = _unit_lower_inverses([jnp.where(strict, kk * hd[8], 0.0) for kk, hd in zip(kks, heads)], eye)
        t_parts = [_split(t) for t in tinvs]
        us = [_dot3(tp, _split(hd[5]), NN) for tp, hd in zip(t_parts, heads)]
        ws = [_dot3(tp, _split(hd[7]), NN) for tp, hd in zip(t_parts, heads)]
        for hd, tinv, u, w, qk in zip(heads, tinvs, us, ws, qks):
            rs, hs, cs = hd[0], hd[1], hd[2]
            tinv_ref[rs, cs] = tinv
            u_ref[rs, hs] = u
            w_ref[rs, hs] = w.astype(w_ref.dtype)
            p_ref[rs, cs] = jnp.where(causal, qk * hd[8], 0.0).astype(p_ref.dtype)

    def spec(width):
        return pl.BlockSpec((rows, width), lambda n: (n, 0))

    hw, cw = GDN_WIDTH, GDN_HEADS * CHUNK
    return pl.pallas_call(
        body, name=name, grid=(T // rows,),
        in_specs=[spec(QKV_WIDTH), spec(128)],
        out_specs=[spec(cw), spec(hw), spec(hw), spec(hw), spec(hw), spec(cw), spec(128)],
        out_shape=[jax.ShapeDtypeStruct((T, cw), F32), jax.ShapeDtypeStruct((T, hw), F32),
                   jax.ShapeDtypeStruct((T, hw), MXU_DTYPE), jax.ShapeDtypeStruct((T, hw), MXU_DTYPE),
                   jax.ShapeDtypeStruct((T, hw), MXU_DTYPE), jax.ShapeDtypeStruct((T, cw), MXU_DTYPE),
                   jax.ShapeDtypeStruct((T, 128), F32)],
        compiler_params=_params("parallel"),
    )(qkv, bg)


def _gdn_scan_fwd(u, w, qd, kd, p, cd, name):
    T = u.shape[0]
    cb = _tile(T // CHUNK, SCAN_CHUNKS_PER_STEP)
    rows = cb * CHUNK

    def body(u_ref, w_ref, qd_ref, kd_ref, p_ref, cd_ref, o_ref, s_all_ref, vn_ref, s_ref):
        @pl.when(pl.program_id(0) == 0)
        def _():
            s_ref[...] = jnp.zeros_like(s_ref)

        hss = [slice(hh * HEAD_DIM, (hh + 1) * HEAD_DIM) for hh in range(GDN_HEADS)]
        css = [slice(hh * CHUNK, (hh + 1) * CHUNK) for hh in range(GDN_HEADS)]
        s_cur = [s_ref[hh] for hh in range(GDN_HEADS)]
        for ci in range(cb):
            rs = slice(ci * CHUNK, (ci + 1) * CHUNK)
            for hh in range(GDN_HEADS):
                s_all_ref[ci * GDN_WIDTH + hh * HEAD_DIM:ci * GDN_WIDTH + (hh + 1) * HEAD_DIM, :] = s_cur[hh]
            s_ms = [s.astype(MXU_DTYPE) for s in s_cur]
            w_s = [_dot(w_ref[rs, hs], s_m, NN) for hs, s_m in zip(hss, s_ms)]
            q_s = [_dot(qd_ref[rs, hs], s_m, NN) for hs, s_m in zip(hss, s_ms)]
            v_ms = [(u_ref[rs, hs] - ws_).astype(MXU_DTYPE) for hs, ws_ in zip(hss, w_s)]
            k_v = [_dot(kd_ref[rs, hs], v_m, TN) for hs, v_m in zip(hss, v_ms)]
            p_v = [_dot(p_ref[rs, cs], v_m, NN) for cs, v_m in zip(css, v_ms)]
            for hh in range(GDN_HEADS):
                vn_ref[rs, hss[hh]] = v_ms[hh]
                o_ref[rs, hss[hh]] = q_s[hh] + p_v[hh]
                c_dec = cd_ref[ci * CHUNK:ci * CHUNK + 1, GDN_HEADS + hh:GDN_HEADS + hh + 1]
                s_cur[hh] = s_cur[hh] * c_dec + k_v[hh]
        for hh in range(GDN_HEADS):
            s_ref[hh] = s_cur[hh]

    def spec(width):
        return pl.BlockSpec((rows, width), lambda n: (n, 0))

    hw, cw = GDN_WIDTH, GDN_HEADS * CHUNK
    return pl.pallas_call(
        body, name=name, grid=(T // rows,),
        in_specs=[spec(hw), spec(hw), spec(hw), spec(hw), spec(cw), spec(128)],
        out_specs=[spec(hw), pl.BlockSpec((cb * GDN_WIDTH, HEAD_DIM), lambda n: (n, 0)), spec(hw)],
        out_shape=[jax.ShapeDtypeStruct((T, hw), F32),
                   jax.ShapeDtypeStruct((T // CHUNK * GDN_WIDTH, HEAD_DIM), F32),
                   jax.ShapeDtypeStruct((T, hw), MXU_DTYPE)],
        scratch_shapes=[pltpu.VMEM((GDN_HEADS, HEAD_DIM, HEAD_DIM), F32)],
        compiler_params=_params("arbitrary"),
    )(u, w, qd, kd, p, cd)


def _gdn_scan_bwd(do, w, qd, kd, p, cd, s_all, vn, name):
    T = do.shape[0]
    cb = _tile(T // CHUNK, SCAN_CHUNKS_PER_STEP)
    rows = cb * CHUNK
    n_steps = T // rows

    def body(do_ref, w_ref, qd_ref, kd_ref, p_ref, cd_ref, s_all_ref, vn_ref,
             dvn_ref, dw_ref, dqd_ref, dkd_ref, dp_ref, dcd_ref, ds_ref):
        @pl.when(pl.program_id(0) == 0)
        def _():
            ds_ref[...] = jnp.zeros_like(ds_ref)

        causal, _, _ = _chunk_masks()
        lane = lax.broadcasted_iota(jnp.int32, (CHUNK, 128), 1)
        heads = range(GDN_HEADS)
        hss = [slice(hh * HEAD_DIM, (hh + 1) * HEAD_DIM) for hh in heads]
        css = [slice(hh * CHUNK, (hh + 1) * CHUNK) for hh in heads]
        ds_cur = [ds_ref[hh] for hh in heads]
        for ci in reversed(range(cb)):
            rs = slice(ci * CHUNK, (ci + 1) * CHUNK)
            ds_ms = [d.astype(MXU_DTYPE) for d in ds_cur]
            s_olds = [s_all_ref[ci * GDN_WIDTH + hh * HEAD_DIM:ci * GDN_WIDTH + (hh + 1) * HEAD_DIM, :] for hh in heads]
            s_ms = [s.astype(MXU_DTYPE) for s in s_olds]
            do_ms = [do_ref[rs, hs].astype(MXU_DTYPE) for hs in hss]
            p_do = [_dot(p_ref[rs, cs], do_m, TN) for cs, do_m in zip(css, do_ms)]
            k_ds = [_dot(kd_ref[rs, hs], ds_m, NN) for hs, ds_m in zip(hss, ds_ms)]
            q_do = [_dot(qd_ref[rs, hs], do_m, TN) for hs, do_m in zip(hss, do_ms)]
            dqds = [_dot(do_m, s_m, NT) for do_m, s_m in zip(do_ms, s_ms)]
            dkds = [_dot(vn_ref[rs, hs], ds_m, NT) for hs, ds_m in zip(hss, ds_ms)]
            dps = [_dot(do_m, vn_ref[rs, hs], NT) for hs, do_m in zip(hss, do_ms)]
            dv_news = [a + b for a, b in zip(p_do, k_ds)]
            dvn_ms = [d.astype(MXU_DTYPE) for d in dv_news]
            w_dv = [_dot(w_ref[rs, hs], dvn_m, TN) for hs, dvn_m in zip(hss, dvn_ms)]
            dws = [_dot(dvn_m, s_m, NT) for dvn_m, s_m in zip(dvn_ms, s_ms)]
            dcd_tile = jnp.zeros((CHUNK, 128), F32)
            for hh in heads:
                dvn_ref[rs, hss[hh]] = dv_news[hh]
                dw_ref[rs, hss[hh]] = -dws[hh]
                dqd_ref[rs, hss[hh]] = dqds[hh]
                dkd_ref[rs, hss[hh]] = dkds[hh]
                dp_ref[rs, css[hh]] = jnp.where(causal, dps[hh], 0.0)
                dcd = jnp.sum(jnp.sum(s_olds[hh] * ds_cur[hh], axis=1, keepdims=True), axis=0, keepdims=True)
                dcd_tile = jnp.where(lane == GDN_HEADS + hh, dcd, dcd_tile)
                c_dec = cd_ref[ci * CHUNK:ci * CHUNK + 1, GDN_HEADS + hh:GDN_HEADS + hh + 1]
                ds_cur[hh] = c_dec * ds_cur[hh] + q_do[hh] - w_dv[hh]
            dcd_ref[rs, :] = dcd_tile
        for hh in heads:
            ds_ref[hh] = ds_cur[hh]

    def spec(width):
        return pl.BlockSpec((rows, width), lambda n: (n_steps - 1 - n, 0))

    hw, cw = GDN_WIDTH, GDN_HEADS * CHUNK
    return pl.pallas_call(
        body, name=name, grid=(n_steps,),
        in_specs=[spec(hw), spec(hw), spec(hw), spec(hw), spec(cw), spec(128),
                  pl.BlockSpec((cb * GDN_WIDTH, HEAD_DIM), lambda n: (n_steps - 1 - n, 0)), spec(hw)],
        out_specs=[spec(hw), spec(hw), spec(hw), spec(hw), spec(cw), spec(128)],
        out_shape=[jax.ShapeDtypeStruct((T, hw), F32)] * 4
        + [jax.ShapeDtypeStruct((T, cw), F32), jax.ShapeDtypeStruct((T, 128), F32)],
        scratch_shapes=[pltpu.VMEM((GDN_HEADS, HEAD_DIM, HEAD_DIM), F32)],
        compiler_params=_params("arbitrary"),
    )(do, w, qd, kd, p, cd, s_all, vn)


def _gdn_chunk_bwd(qkv, bg, tinv_all, u, w, dvn, dw, dqd, dkd, dp, dcd, name):
    T = qkv.shape[0]
    cb = _tile(T // CHUNK, CHUNKS_PER_STEP)
    rows = cb * CHUNK

    def body(qkv_ref, bg_ref, tinv_ref, u_ref, w_ref, dvn_ref, dw_ref, dqd_ref, dkd_ref, dp_ref, dcd_ref,
             dqkv_ref, dbg_ref):
        masks = _chunk_masks()
        causal, strict, _ = masks
        lane = lax.broadcasted_iota(jnp.int32, (CHUNK, 128), 1)
        row = lax.broadcasted_iota(jnp.int32, (CHUNK, 128), 0)
        heads = []
        for ci in range(cb):
            rs = slice(ci * CHUNK, (ci + 1) * CHUNK)
            bgv = bg_ref[rs, :]
            gc_all = _chunk_cumsum(bgv)
            gc_rows = _row_form(gc_all)
            for hh in range(GDN_HEADS):
                qs, ks, vs = _head_slices(hh)
                q, k = qkv_ref[rs, qs], qkv_ref[rs, ks]
                beta = bgv[:, hh:hh + 1]
                gc = gc_all[:, GDN_HEADS + hh:GDN_HEADS + hh + 1]
                decay, kb, eg = _head_elementwise(k, beta, gc, gc_rows[GDN_HEADS + hh:GDN_HEADS + hh + 1, :], causal)
                heads.append(dict(ci=ci, hh=hh, rs=rs, hs=slice(hh * HEAD_DIM, (hh + 1) * HEAD_DIM),
                                  cs=slice(hh * CHUNK, (hh + 1) * CHUNK), q=q, k=k, beta=beta, gc=gc,
                                  decay=decay, kb=kb, eg=eg))
        for hd in heads:
            hd["t"] = _split(tinv_ref[hd["rs"], hd["cs"]])
        for hd in heads:
            hd["kk"] = _mdot(hd["kb"], hd["k"], NT)
            hd["qk"] = _mdot(hd["q"], hd["k"], NT)
        for hd in heads:
            hd["dvb"] = _dot3(hd["t"], _split(dvn_ref[hd["rs"], hd["hs"]]), TN)
            hd["dkbeg"] = _dot3(hd["t"], _split(dw_ref[hd["rs"], hd["hs"]]), TN)
        for hd in heads:
            rs, hs = hd["rs"], hd["hs"]
            da = -(_mdot(hd["dvb"], u_ref[rs, hs], NT) + _mdot(hd["dkbeg"], w_ref[rs, hs], NT))
            dm = jnp.where(strict, da, 0.0)
            dp_ = dp_ref[rs, hd["cs"]]
            hd["dkk"] = dm * hd["decay"]
            hd["dqk"] = dp_ * hd["decay"]
            hd["e"] = (hd["dkk"] * hd["kk"] + hd["dqk"] * hd["qk"])
        for hd in heads:
            hd["dkb"] = _mdot(hd["dkk"], hd["k"], NN)
            hd["dk"] = _mdot(hd["dkk"], hd["kb"], TN) + _mdot(hd["dqk"], hd["q"], TN)
            hd["dq"] = _mdot(hd["dqk"], hd["k"], NN)
            onehot = (lane == GDN_HEADS + hd["hh"]).astype(jnp.bfloat16)
            e_hi, e_lo = _split(hd["e"])
            hd["col_sums"] = _dot(e_lo, onehot, TN) + _dot(e_hi, onehot, TN)
        tiles = {}
        for hd in heads:
            ci, hh, rs, hs = hd["ci"], hd["hh"], hd["rs"], hd["hs"]
            qs, ks, vs = _head_slices(hh)
            q, k, beta, gc, eg, kb = hd["q"], hd["k"], hd["beta"], hd["gc"], hd["eg"], hd["kb"]
            v = qkv_ref[rs, vs]
            dqd_, dkd_ = dqd_ref[rs, hs], dkd_ref[rs, hs]
            gl = gc[CHUNK - 1:CHUNK, :]
            ek = jnp.exp(gl - gc)
            dkb = hd["dkb"] + hd["dkbeg"] * eg
            deg = jnp.sum(dqd_ * q, axis=1, keepdims=True) + jnp.sum(hd["dkbeg"] * kb, axis=1, keepdims=True)
            dek = jnp.sum(dkd_ * k, axis=1, keepdims=True)
            dcd_ = dcd_ref[ci * CHUNK:ci * CHUNK + 1, GDN_HEADS + hh:GDN_HEADS + hh + 1]
            dgl = jnp.sum(dek * ek, axis=0, keepdims=True) + dcd_ * jnp.exp(gl)
            dgc = jnp.sum(hd["e"], axis=1, keepdims=True) + deg * eg - dek * ek
            dbeta_tile, dgc_tile = tiles.get(ci, (jnp.zeros((CHUNK, 128), F32), jnp.zeros((CHUNK, 128), F32)))
            dgc_tile += jnp.where(lane == GDN_HEADS + hh, dgc, 0.0) - hd["col_sums"]
            dgc_tile += jnp.where((lane == GDN_HEADS + hh) & (row == CHUNK - 1), dgl, 0.0)
            dbeta = jnp.sum(dkb * k, axis=1, keepdims=True) + jnp.sum(hd["dvb"] * v, axis=1, keepdims=True)
            dbeta_tile += jnp.where(lane == hh, dbeta, 0.0)
            tiles[ci] = (dbeta_tile, dgc_tile)
            dqkv_ref[rs, qs] = hd["dq"] + dqd_ * eg
            dqkv_ref[rs, ks] = hd["dk"] + dkd_ * ek + dkb * beta
            dqkv_ref[rs, vs] = hd["dvb"] * beta
        for ci in range(cb):
            dbeta_tile, dgc_tile = tiles[ci]
            dbg_ref[ci * CHUNK:(ci + 1) * CHUNK, :] = dbeta_tile + _chunk_cumsum(dgc_tile, reverse=True)

    def spec(width):
        return pl.BlockSpec((rows, width), lambda n: (n, 0))

    hw, cw = GDN_WIDTH, GDN_HEADS * CHUNK
    return pl.pallas_call(
        body, name=name, grid=(T // rows,),
        in_specs=[spec(QKV_WIDTH), spec(128), spec(cw), spec(hw), spec(hw), spec(hw), spec(hw), spec(hw),
                  spec(hw), spec(cw), spec(128)],
        out_specs=[spec(QKV_WIDTH), spec(128)],
        out_shape=[jax.ShapeDtypeStruct((T, QKV_WIDTH), F32), jax.ShapeDtypeStruct((T, 128), F32)],
        compiler_params=_params("parallel"),
    )(qkv, bg, tinv_all, u, w, dvn, dw, dqd, dkd, dp, dcd)


def _pool_counts(i, tm, rows, offset):
    t = i * tm - offset + lax.broadcasted_iota(jnp.int32, (rows, 1), 0)
    return [jnp.minimum(t + 1, w).astype(F32) for w in POOL_WINDOWS]


def _window_sums(window, forward):
    sums, s, step = [], window, 1
    for _ in POOL_WINDOWS:
        s = s + _shift_rows(s, -step if forward else step)
        sums.append(s)
        step *= 2
    return sums


def _pooled(window, counts):
    sums = _window_sums(window, forward=False)
    out = []
    for gi in range(POOL_GROUPS):
        sl = slice(gi * 128, (gi + 1) * 128)
        out.append(sums[gi][HALO:, sl] / counts[gi] - window[HALO:, sl])
    return out


def _mix_post(o, proj, gdn_norm, pool_w, pool_scale, name):
    T = o.shape[0]
    tm = _tile(T, 256)
    hb = tm // HALO

    def body(o_ref, z_ref, p_ref, ph_ref, gn_ref, pw_ref, ps_ref, out_ref):
        i = pl.program_id(0)
        for hh in range(GDN_HEADS):
            sl = slice(hh * HEAD_DIM, (hh + 1) * HEAD_DIM)
            oh, zh = o_ref[:, sl], z_ref[:, sl]
            ro = lax.rsqrt(jnp.mean(oh * oh, axis=-1, keepdims=True) + EPS)
            out_ref[:, sl] = (((oh * ro) * gn_ref[...]) * (zh * _sigmoid(zh))).astype(out_ref.dtype)
        halo = jnp.where(i == 0, 0.0, ph_ref[...])
        window = jnp.concatenate([halo, p_ref[...]], axis=0)
        pooled = _pooled(window, _pool_counts(i, tm, tm, 0))
        for gi in range(POOL_GROUPS):
            pm = _mdot(pooled[gi], pw_ref[gi], NN)
            out_ref[:, GDN_WIDTH + gi * 128:GDN_WIDTH + (gi + 1) * 128] = (
                pm * ps_ref[:, gi * 128:(gi + 1) * 128]).astype(out_ref.dtype)

    return pl.pallas_call(
        body, name=name, grid=(T // tm,),
        in_specs=[pl.BlockSpec((tm, GDN_WIDTH), lambda i: (i, 0)),
                  pl.BlockSpec((tm, GDN_WIDTH), lambda i: (i, COL_Z // GDN_WIDTH)),
                  pl.BlockSpec((tm, POOL_WIDTH), lambda i: (i, COL_P // POOL_WIDTH)),
                  pl.BlockSpec((HALO, POOL_WIDTH), lambda i: (jnp.maximum(i * hb - 1, 0), COL_P // POOL_WIDTH)),
                  pl.BlockSpec((1, HEAD_DIM), lambda i: (0, 0)),
                  pl.BlockSpec((POOL_GROUPS, 128, 128), lambda i: (0, 0, 0)),
                  pl.BlockSpec((1, POOL_WIDTH), lambda i: (0, 0))],
        out_specs=pl.BlockSpec((tm, GDN_WIDTH + POOL_WIDTH), lambda i: (i, 0)),
        out_shape=jax.ShapeDtypeStruct((T, GDN_WIDTH + POOL_WIDTH), MXU_DTYPE),
        compiler_params=_params("parallel"),
    )(o, proj, proj, proj, gdn_norm, pool_w, pool_scale)


def _mix_post_bwd(dmix, o, proj, gdn_norm, pool_w, pool_scale, name):
    T = o.shape[0]
    tm = _tile(T, 256)
    hb = tm // HALO
    n_tiles = T // tm

    def body(dg_ref, dpo_ref, dpo_next_ref, o_ref, z_ref, p_ref, ph_ref, gn_ref, pw_ref, ps_ref,
             do_ref, dzp_ref, dgn_ref, dpw_ref, dps_ref):
        i = pl.program_id(0)

        @pl.when(i == 0)
        def _():
            dgn_ref[...] = jnp.zeros_like(dgn_ref)
            dpw_ref[...] = jnp.zeros_like(dpw_ref)
            dps_ref[...] = jnp.zeros_like(dps_ref)

        gn = gn_ref[...]
        dgn = jnp.zeros((1, HEAD_DIM), F32)
        for hh in range(GDN_HEADS):
            sl = slice(hh * HEAD_DIM, (hh + 1) * HEAD_DIM)
            oh, zh, dy = o_ref[:, sl], z_ref[:, sl], dg_ref[:, sl]
            ro = lax.rsqrt(jnp.mean(oh * oh, axis=-1, keepdims=True) + EPS)
            on = oh * ro
            sig = _sigmoid(zh)
            sz = zh * sig
            dzp_ref[:, sl] = (dy * (on * gn) * (sig * (1.0 + zh * (1.0 - sig)))).astype(dzp_ref.dtype)
            dgn += jnp.sum(dy * on * sz, axis=0, keepdims=True)
            don = dy * gn * sz
            do_ref[:, sl] = ro * (don - on * jnp.mean(don * on, axis=-1, keepdims=True))
        dgn_ref[...] += dgn

        halo = jnp.where(i == 0, 0.0, ph_ref[...])
        window = jnp.concatenate([halo, p_ref[...]], axis=0)
        counts = _pool_counts(i, tm, tm + HALO, 0)
        pooled = _pooled(window, [cn[:tm] for cn in counts])
        nxt = jnp.where(i == n_tiles - 1, 0.0, dpo_next_ref[...])
        dpo_w = jnp.concatenate([dpo_ref[...], nxt], axis=0)
        ps = ps_ref[...]
        dps = []
        scaled = []
        for gi in range(POOL_GROUPS):
            sl = slice(gi * 128, (gi + 1) * 128)
            dpm = dpo_w[:, sl] * ps[:, sl]
            pm = _mdot(pooled[gi], pw_ref[gi], NN)
            dps.append(jnp.sum(dpo_w[:tm, sl] * pm, axis=0, keepdims=True))
            dpw_ref[gi] += _mdot(pooled[gi], dpm[:tm], TN)
            dpooled = _mdot(dpm, pw_ref[gi], NT)
            scaled.append((dpooled, dpooled / counts[gi]))
        dps_ref[...] += jnp.concatenate(dps, axis=1)
        lead = _window_sums(jnp.concatenate([sc for _, sc in scaled], axis=1), forward=True)
        for gi in range(POOL_GROUPS):
            sl = slice(gi * 128, (gi + 1) * 128)
            dzp_ref[:, GDN_WIDTH + gi * 128:GDN_WIDTH + (gi + 1) * 128] = (
                lead[gi][:tm, sl] - scaled[gi][0][:tm]).astype(dzp_ref.dtype)

    last_halo = T // HALO - 1
    return pl.pallas_call(
        body, name=name, grid=(n_tiles,),
        in_specs=[pl.BlockSpec((tm, GDN_WIDTH), lambda i: (i, 0)),
                  pl.BlockSpec((tm, POOL_WIDTH), lambda i: (i, 1)),
                  pl.BlockSpec((HALO, POOL_WIDTH), lambda i: (jnp.minimum((i + 1) * hb, last_halo), 1)),
                  pl.BlockSpec((tm, GDN_WIDTH), lambda i: (i, 0)),
                  pl.BlockSpec((tm, GDN_WIDTH), lambda i: (i, COL_Z // GDN_WIDTH)),
                  pl.BlockSpec((tm, POOL_WIDTH), lambda i: (i, COL_P // POOL_WIDTH)),
                  pl.BlockSpec((HALO, POOL_WIDTH), lambda i: (jnp.maximum(i * hb - 1, 0), COL_P // POOL_WIDTH)),
                  pl.BlockSpec((1, HEAD_DIM), lambda i: (0, 0)),
                  pl.BlockSpec((POOL_GROUPS, 128, 128), lambda i: (0, 0, 0)),
                  pl.BlockSpec((1, POOL_WIDTH), lambda i: (0, 0))],
        out_specs=[pl.BlockSpec((tm, GDN_WIDTH), lambda i: (i, 0)),
                   pl.BlockSpec((tm, GDN_WIDTH + POOL_WIDTH), lambda i: (i, 0)),
                   pl.BlockSpec((1, HEAD_DIM), lambda i: (0, 0)),
                   pl.BlockSpec((POOL_GROUPS, 128, 128), lambda i: (0, 0, 0)),
                   pl.BlockSpec((1, POOL_WIDTH), lambda i: (0, 0))],
        out_shape=[jax.ShapeDtypeStruct((T, GDN_WIDTH), F32),
                   jax.ShapeDtypeStruct((T, GDN_WIDTH + POOL_WIDTH), MXU_DTYPE),
                   jax.ShapeDtypeStruct((1, HEAD_DIM), F32),
                   jax.ShapeDtypeStruct((POOL_GROUPS, 128, 128), F32),
                   jax.ShapeDtypeStruct((1, POOL_WIDTH), F32)],
        compiler_params=_params("arbitrary"),
    )(dmix, dmix, dmix, o, proj, proj, proj, gdn_norm, pool_w, pool_scale)


def _gdn_prep_bwd(proj, conv_w, a_log_l, dt_bias_l, dqkv, dbg, dzp, name):
    T = proj.shape[0]
    tm = _tile(T, 256)
    hb = tm // 8
    n_tiles = T // tm
    last_halo = T // 8 - 1

    def body(cur_ref, before_ref, after_ref, ba_ref, w_ref, al_ref, dtb_ref, dq_ref, dq_after_ref, dbg_ref,
             dzp_ref, dproj_ref, dw_ref, dal_ref, ddtb_ref):
        i = pl.program_id(0)

        @pl.when(i == 0)
        def _():
            dw_ref[...] = jnp.zeros_like(dw_ref)
            dal_ref[...] = jnp.zeros_like(dal_ref)
            ddtb_ref[...] = jnp.zeros_like(ddtb_ref)

        last = i == n_tiles - 1
        w = w_ref[...]
        before = jnp.where(i == 0, 0.0, before_ref[...])
        after = jnp.where(last, 0.0, after_ref[...])
        window = jnp.concatenate([before, cur_ref[...], after], axis=0)
        y = _conv_act(window, w)
        sig = _sigmoid(y)
        act = y * sig
        dq_w = jnp.concatenate([jnp.zeros((8, QKV_WIDTH), F32), dq_ref[...],
                                jnp.where(last, 0.0, dq_after_ref[...])], axis=0)
        dact = []
        for hh in range(3 * GDN_HEADS):
            sl = slice(hh * HEAD_DIM, (hh + 1) * HEAD_DIM)
            blk, dblk = act[:, sl], dq_w[:, sl]
            if hh < 2 * GDN_HEADS:
                rn = lax.rsqrt(jnp.sum(blk * blk, axis=-1, keepdims=True) + EPS)
                unit = blk * rn
                if hh < GDN_HEADS:
                    dblk = dblk * (HEAD_DIM ** -0.5)
                dblk = rn * (dblk - unit * jnp.sum(dblk * unit, axis=-1, keepdims=True))
            dact.append(dblk)
        dy = jnp.concatenate(dact, axis=1) * (sig * (1.0 + y * (1.0 - sig)))
        dx = dy * w[CONV_K - 1:CONV_K, :]
        dws = [None] * CONV_K
        dws[CONV_K - 1] = jnp.sum(dy[8:8 + tm] * window[8:8 + tm], axis=0, keepdims=True)
        for j in range(CONV_K - 1):
            s = CONV_K - 1 - j
            dx += _shift_rows(dy, -s) * w[j:j + 1, :]
            dws[j] = jnp.sum(dy[8:8 + tm] * _shift_rows(window, s)[8:8 + tm], axis=0, keepdims=True)
        dw_ref[...] += jnp.concatenate(dws, axis=0)
        dproj_ref[:, :QKV_WIDTH] = dx[8:8 + tm].astype(dproj_ref.dtype)
        dproj_ref[:, COL_Z:COL_BA] = dzp_ref[...]

        ba = ba_ref[...]
        dbg_ = dbg_ref[...]
        lane = lax.broadcasted_iota(jnp.int32, ba.shape, 1)
        beta = _sigmoid(ba)
        pre = ba + dtb_ref[...]
        neg_a = -jnp.exp(al_ref[...])
        g = neg_a * _softplus(pre)
        is_g = (lane >= GDN_HEADS) & (lane < 2 * GDN_HEADS)
        da_raw = jnp.where(is_g, dbg_ * neg_a * _sigmoid(pre), 0.0)
        dba = jnp.where(lane < GDN_HEADS, dbg_ * beta * (1.0 - beta), da_raw)
        dproj_ref[:, COL_BA:] = dba.astype(dproj_ref.dtype)
        dal_ref[...] += jnp.sum(jnp.where(is_g, dbg_ * g, 0.0), axis=0, keepdims=True)
        ddtb_ref[...] += jnp.sum(da_raw, axis=0, keepdims=True)

    lane_vec = pl.BlockSpec((1, 128), lambda i: (0, 0))
    return pl.pallas_call(
        body, name=name, grid=(n_tiles,),
        in_specs=[pl.BlockSpec((tm, QKV_WIDTH), lambda i: (i, 0)),
                  pl.BlockSpec((8, QKV_WIDTH), lambda i: (jnp.maximum(i * hb - 1, 0), 0)),
                  pl.BlockSpec((8, QKV_WIDTH), lambda i: (jnp.minimum((i + 1) * hb, last_halo), 0)),
                  pl.BlockSpec((tm, 128), lambda i: (i, COL_BA // 128)),
                  pl.BlockSpec((CONV_K, QKV_WIDTH), lambda i: (0, 0)), lane_vec, lane_vec,
                  pl.BlockSpec((tm, QKV_WIDTH), lambda i: (i, 0)),
                  pl.BlockSpec((8, QKV_WIDTH), lambda i: (jnp.minimum((i + 1) * hb, last_halo), 0)),
                  pl.BlockSpec((tm, 128), lambda i: (i, 0)),
                  pl.BlockSpec((tm, GDN_WIDTH + POOL_WIDTH), lambda i: (i, 0))],
        out_specs=[pl.BlockSpec((tm, D_IN_PAD), lambda i: (i, 0)),
                   pl.BlockSpec((CONV_K, QKV_WIDTH), lambda i: (0, 0)), lane_vec, lane_vec],
        out_shape=[jax.ShapeDtypeStruct((T, D_IN_PAD), MXU_DTYPE),
                   jax.ShapeDtypeStruct((CONV_K, QKV_WIDTH), F32),
                   jax.ShapeDtypeStruct((1, 128), F32), jax.ShapeDtypeStruct((1, 128), F32)],
        compiler_params=_params("arbitrary"),
    )(proj, proj, proj, proj, conv_w, a_log_l, dt_bias_l, dqkv, dqkv, dbg, dzp)


def _mod_part(c_all, w_ada, b_part, name):
    def body(c_ref, w_ref, b_ref, out_ref):
        cc = c_ref[...]
        out_ref[...] = _mdot(cc * _sigmoid(cc), w_ref[...], NN) + b_ref[...]

    return pl.pallas_call(
        body, name=name, out_shape=jax.ShapeDtypeStruct((c_all.shape[0], w_ada.shape[1]), F32),
        compiler_params=_params(),
    )(c_all, w_ada, b_part)


def _w_ada_grad(c_all, dmod_part, name):
    def body(c_ref, d_ref, out_ref):
        cc = c_ref[...]
        out_ref[...] = _mdot(cc * _sigmoid(cc), d_ref[...], TN)

    return pl.pallas_call(
        body, name=name, out_shape=jax.ShapeDtypeStruct((c_all.shape[1], dmod_part.shape[1]), F32),
        compiler_params=_params(),
    )(c_all, dmod_part)


def _sum_parts(parts, name):
    _, R, C = parts.shape
    tr = max([t for t in range(16, min(R, 512) + 1, 16) if R % t == 0], default=R)

    def body(p_ref, out_ref):
        acc = p_ref[0].astype(F32)
        for s in range(1, N_DEV):
            acc += p_ref[s].astype(F32)
        out_ref[...] = acc

    return pl.pallas_call(
        body, name=name, grid=(R // tr,),
        in_specs=[pl.BlockSpec((N_DEV, tr, C), lambda i: (0, i, 0))],
        out_specs=pl.BlockSpec((tr, C), lambda i: (i, 0)),
        out_shape=jax.ShapeDtypeStruct((R, C), F32),
        compiler_params=_params("parallel"),
    )(parts)


def _adamw_math(w, g, m, v):
    mm = ADAM_B1 * m + (1.0 - ADAM_B1) * g
    vv = ADAM_B2 * v + (1.0 - ADAM_B2) * (g * g)
    m_hat = mm / (1.0 - ADAM_B1 ** ADAM_STEP)
    v_hat = vv / (1.0 - ADAM_B2 ** ADAM_STEP)
    return -ADAM_LR * (m_hat / (jnp.sqrt(v_hat) + ADAM_EPS) + ADAM_WD * w), mm, vv


def _adamw_small(ws, gs, ms, vs, name):
    n = len(ws)

    def body(*refs):
        for i in range(n):
            d, mm, vv = _adamw_math(*[refs[k * n + i][...] for k in range(4)])
            refs[4 * n + i][...] = d
            refs[5 * n + i][...] = mm
            refs[6 * n + i][...] = vv

    out = pl.pallas_call(
        body, name=name, out_shape=[jax.ShapeDtypeStruct(w.shape, F32) for w in ws] * 3,
        compiler_params=_params(),
    )(*ws, *gs, *ms, *vs)
    return out[:n], out[n:2 * n], out[2 * n:]


def _adamw(w, g, m, v, name):
    R, C = w.shape
    tr = max([t for t in range(8, min(R, 512) + 1, 8) if R % t == 0], default=R)

    def body(w_ref, g_ref, m_ref, v_ref, d_ref, mo_ref, vo_ref):
        d_ref[...], mo_ref[...], vo_ref[...] = _adamw_math(w_ref[...], g_ref[...], m_ref[...], v_ref[...])

    spec = pl.BlockSpec((tr, C), lambda i: (i, 0))
    return pl.pallas_call(
        body, name=name, grid=(R // tr,),
        in_specs=[spec] * 4, out_specs=[spec] * 3,
        out_shape=[jax.ShapeDtypeStruct((R, C), F32)] * 3,
        compiler_params=_params("parallel"),
    )(w, g, m, v)


def _weight_grad(a, b, name, dep=None):
    return _matmul([(a, b)], TN, WIRE_DTYPE, name, tm=1408, tn=1024, tk=1024, dep=dep)


def _rows_of(flat, lanes=1024):
    flat = flat.reshape(-1)
    n = -(-flat.shape[0] // lanes) * lanes
    return jnp.pad(flat, (0, n - flat.shape[0])).reshape(n // lanes, lanes)


def _pad_rows(a, rows):
    return jnp.pad(a, ((0, rows - a.shape[0]), (0, 0)))


def kernel(x, c, w_ada, b_ada, norm_ffn1, ffn1_gate, ffn1_up, ffn1_down, norm_mix, w_in, conv_w, a_log, dt_bias, gdn_norm, pool_w, pool_scale, w_out, norm_ffn2, ffn2_gate, ffn2_up, ffn2_down, final_norm, loss_target, m_w_ada, m_b_ada, m_norm_ffn1, m_ffn1_gate, m_ffn1_up, m_ffn1_down, m_norm_mix, m_w_in, m_conv_w, m_a_log, m_dt_bias, m_gdn_norm, m_pool_w, m_pool_scale, m_w_out, m_norm_ffn2, m_ffn2_gate, m_ffn2_up, m_ffn2_down, m_final_norm, v_w_ada, v_b_ada, v_norm_ffn1, v_ffn1_gate, v_ffn1_up, v_ffn1_down, v_norm_mix, v_w_in, v_conv_w, v_a_log, v_dt_bias, v_gdn_norm, v_pool_w, v_pool_scale, v_w_out, v_norm_ffn2, v_ffn2_gate, v_ffn2_up, v_ffn2_down, v_final_norm):
    T, D = x.shape[1], x.shape[2]
    Fs = ffn1_gate.shape[2]
    Ws = w_in.shape[2]
    Ws_pad = -(-Ws // 16) * 16
    Os = w_out.shape[1]
    Ms = w_ada.shape[2]
    Cs = conv_w.shape[2]
    me = 4 * lax.axis_index("x") + 2 * lax.axis_index("y") + lax.axis_index("c")
    x0, target = x[0], loss_target[0]

    def wire(a):
        return a.astype(WIRE_DTYPE)

    def token(started):
        return started[4][:1, :1]

    def with_own(landed, own):
        return lax.dynamic_update_slice(landed, own[None], (me, 0, 0))

    def full(landed):
        return landed.reshape(-1, D).astype(MXU_DTYPE)

    no_dep = jnp.zeros((8, 128), F32)
    small = jnp.concatenate([_pad_rows(c, 8), _pad_rows(jnp.pad(conv_w[0], ((0, 0), (0, D - Cs))), 8)], axis=0)
    got, = _all_gather([small], "gather_small")
    c_all = got[:, 0, :]
    conv_full = jnp.transpose(got[:, 8:8 + CONV_K, :Cs], (1, 0, 2)).reshape(CONV_K, QKV_WIDTH)
    b_part = lax.dynamic_slice(b_ada, (0, me * Ms), (1, Ms))
    mod_part = _mod_part(c_all, w_ada[0], b_part, "mod_part")

    w1 = [wire(ffn1_gate[0].T), wire(ffn1_up[0].T), wire(ffn1_down[0])]
    w2 = [wire(_pad_rows(w_in[0].T, Ws_pad)), wire(w_out[0])]
    w3 = [wire(ffn2_gate[0].T), wire(ffn2_up[0].T), wire(ffn2_down[0])]
    off23 = [0, Ws_pad, Ws_pad + Os, Ws_pad + Os + Fs, Ws_pad + Os + 2 * Fs]
    mod_parts, *w1_all = _all_gather([mod_part] + w1, "gather_mod_w1")
    mod_all = jnp.transpose(mod_parts, (1, 0, 2)).reshape(N_DEV, N_MOD * D)
    mod = lax.dynamic_slice(mod_all, (me, 0), (1, N_MOD * D)).reshape(N_MOD, 1, D)
    sh1, sc1, gt1, sh2, sc2, gt2, sh3, sc3, gt3 = [mod[i] for i in range(N_MOD)]
    wg1_t, wu1_t, wd1 = [full(w) for w in w1_all]
    w2_sent = _exchange_start(w2, True, w1_all[0], "w2_start")
    w3_sent = _exchange_start(w3, True, w2_sent[4], "w3_start")

    lane_pad = lambda a: jnp.pad(a, ((0, 0), (GDN_HEADS, 128 - 2 * GDN_HEADS)))
    a_log_l, dt_bias_l = lane_pad(a_log), lane_pad(dt_bias)
    pool_w_m = pool_w[0].astype(MXU_DTYPE)

    g1, u1, a1, h1, y1, x1, h2 = _swiglu_fwd(
        x0, wg1_t, wu1_t, wd1, "ffn1_fwd", norm_in=(norm_ffn1, sc1 + token(w3_sent), sh1),
        resid_out=(x0, gt1, 0.5, norm_mix, sc2, sh2))
    w_in_all, wo_all = [with_own(z, own) for z, own in zip(_exchange_wait(w2_sent, h2, True, "w2_wait"), w2)]
    w_in_t = w_in_all[:, :Ws, :].reshape(-1, D).astype(MXU_DTYPE)
    wo = full(wo_all)
    w_in_re = jnp.concatenate([w_in_t[:COL_Z + GDN_WIDTH], w_in_t[D_IN - POOL_WIDTH:],
                               w_in_t[4 * GDN_WIDTH:4 * GDN_WIDTH + 2 * GDN_HEADS],
                               jnp.zeros((128 - 2 * GDN_HEADS, D), MXU_DTYPE)], axis=0)
    proj = _matmul([(h2, w_in_re)], NT, F32, "proj_in", tm=512, tn=D_IN_PAD, tk=D)
    qkv, bg = _gdn_prep(proj, conv_full, a_log_l, dt_bias_l, "gdn_prep")
    tinv, u_c, w_c, qd_c, kd_c, p_c, cd_c = _gdn_chunk_fwd(qkv, bg, "gdn_chunk_fwd")
    o, s_all, vn_c = _gdn_scan_fwd(u_c, w_c, qd_c, kd_c, p_c, cd_c, "gdn_scan_fwd")
    mix_in = _mix_post(o, proj, gdn_norm, pool_w_m, pool_scale, "mix_post")
    mixed, x2, h3 = _matmul_resid_norm_mod(mix_in, wo, x1, gt2, 1.0, norm_ffn2, sc3, sh3, "mix_out")
    wg2_t, wu2_t, wd2 = [full(with_own(z, own))
                         for z, own in zip(_exchange_wait(w3_sent, h3, True, "w3_wait"), w3)]
    g3, u3, a3, loss_row, d3, d_final, dy3, dgt3 = _swiglu_fwd(
        h3, wg2_t, wu2_t, wd2, "ffn2_fwd_loss", loss_out=(x2, gt3, final_norm.reshape(1, D), target))

    dg3, du3, d2, d_n3, dsc3, dsh3, dmixed, dgt2 = _swiglu_bwd(
        dy3, wd2, g3, u3, wg2_t, wu2_t, "ffn2_bwd_norm3_bwd", (x2, norm_ffn2, sc3), d3,
        produced_by=(mixed, gt2, 1.0))
    d_wd2 = _weight_grad(a3, dy3, "ffn2_dwd")
    d_wg2 = _weight_grad(dg3, h3, "ffn2_dwg")
    d_wu2 = _weight_grad(du3, h3, "ffn2_dwu")
    dmix_in = _matmul([(dmixed, wo)], NT, F32, "mix_out_bwd", tm=512, tn=GDN_WIDTH + POOL_WIDTH, tk=D)
    d_wo = _matmul([(mix_in, dmixed)], TN, WIRE_DTYPE, "mix_dwo", tm=GDN_WIDTH + POOL_WIDTH, tn=D, tk=1024)
    do, dzp, d_gn, d_pw, d_ps = _mix_post_bwd(dmix_in, o, proj, gdn_norm, pool_w_m, pool_scale, "mix_post_bwd")
    dvn, dw_c, dqd, dkd, dp_c, dcd = _gdn_scan_bwd(do, w_c, qd_c, kd_c, p_c, cd_c, s_all, vn_c, "gdn_scan_bwd")
    dqkv, dbg = _gdn_chunk_bwd(qkv, bg, tinv, u_c, w_c, dvn, dw_c, dqd, dkd, dp_c, dcd, "gdn_chunk_bwd")
    dproj, d_conv, d_al, d_dtb = _gdn_prep_bwd(proj, conv_full, a_log_l, dt_bias_l, dqkv, dbg, dzp, "gdn_prep_bwd")
    d_win_re = _matmul([(dproj, h2)], TN, WIRE_DTYPE, "proj_in_dw", tm=D_IN_PAD, tn=D, tk=1024)
    d_win_t = jnp.concatenate([d_win_re[:COL_Z + GDN_WIDTH], d_win_re[COL_BA:COL_BA + 2 * GDN_HEADS],
                               d_win_re[COL_P:COL_P + POOL_WIDTH]], axis=0)
    d_win_blocks = jnp.pad(d_win_t.reshape(N_DEV, Ws, D), ((0, 0), (0, Ws_pad - Ws), (0, 0)))
    parts23 = jnp.concatenate(
        [wire(d_win_blocks), wire(d_wo.reshape(N_DEV, Os, D)), wire(d_wg2.reshape(N_DEV, Fs, D)),
         wire(d_wu2.reshape(N_DEV, Fs, D)), wire(d_wd2.reshape(N_DEV, Fs, D))], axis=1)
    own23 = lax.dynamic_index_in_dim(parts23, me, 0, keepdims=False)
    g23_sent = _exchange_start([parts23], False, no_dep, "g23_start")
    d1, d_n2, dsc2, dsh2, dy1, dgt1 = _norm_bwd((dproj, w_in_re), x1, norm_mix, sc2 + token(g23_sent), d2,
                                                "proj_in_bwd_norm2_bwd", produced_by=(y1, gt1, 0.5))
    dg1, du1, grad_x, d_n1, dsc1, dsh1 = _swiglu_bwd(
        dy1, wd1, g1, u1, wg1_t, wu1_t, "ffn1_bwd_norm1_bwd", (x0, norm_ffn1, sc1), d1)

    dmod = jnp.concatenate([dsh1, dsc1, dgt1, dsh2, dsc2, dgt2, dsh3, dsc3, dgt3], axis=0)
    small_rows = [dmod.reshape(-1), d_n1[0], d_n2[0], d_n3[0], d_final[0], d_gn[0], d_ps[0],
                  d_al[0, GDN_HEADS:2 * GDN_HEADS], d_dtb[0, GDN_HEADS:2 * GDN_HEADS], loss_row[0, :1],
                  d_conv.reshape(-1), d_pw.reshape(-1)]
    lanes = 1024
    small_rows = [_rows_of(r, lanes) for r in small_rows]
    n_rows = [r.shape[0] for r in small_rows]
    row_off = [sum(n_rows[:i]) for i in range(len(n_rows))]
    total = -(-sum(n_rows) // 8) * 8
    slab = _pad_rows(jnp.concatenate(small_rows, axis=0), total)
    slab_sent = _exchange_start([slab], True, no_dep, "small_grads_start")

    def send_ffn1(a, b, which, dep):
        parts = _weight_grad(a, b, f"ffn1_{which}", dep=dep).reshape(N_DEV, Fs, D)
        own = lax.dynamic_index_in_dim(parts, me, 0, keepdims=False)
        return _exchange_start([parts], False, no_dep, f"g1_{which}_start"), own

    g1_wg, own_wg = send_ffn1(dg1, h1, "dwg", slab_sent[4])
    g1_wu, own_wu = send_ffn1(du1, h1, "dwu", g1_wg[4])
    g1_wd, own_wd = send_ffn1(a1, dy1, "dwd", g1_wu[4])

    slab_all = with_own(_exchange_wait(slab_sent, g1_wg[4], True, "small_grads_wait")[0], slab)
    summed = _sum_parts(slab_all, "sum_small_grads")

    def piece(idx, n):
        return summed[row_off[idx]:row_off[idx] + n_rows[idx]].reshape(-1)[:n]

    g_b_ada = piece(0, N_MOD * D).reshape(1, N_MOD * D)
    g_n1, g_n2, g_n3 = piece(1, D).reshape(1, D), piece(2, D).reshape(1, D), piece(3, D).reshape(1, D)
    g_final = piece(4, D)
    g_gn = piece(5, HEAD_DIM).reshape(1, HEAD_DIM)
    g_ps = piece(6, POOL_WIDTH).reshape(1, POOL_WIDTH)
    g_al = piece(7, GDN_HEADS).reshape(1, GDN_HEADS)
    g_dtb = piece(8, GDN_HEADS).reshape(1, GDN_HEADS)
    loss = piece(9, 1)[0]
    g_conv = lax.dynamic_slice(piece(10, CONV_K * QKV_WIDTH).reshape(1, CONV_K, QKV_WIDTH), (0, 0, me * Cs),
                               (1, CONV_K, Cs))
    g_pw = piece(11, POOL_GROUPS * 128 * 128).reshape(1, POOL_GROUPS, 128, 128)

    dmod_all = slab_all[:, row_off[0]:row_off[0] + n_rows[0], :].reshape(N_DEV, -1)[:, :N_MOD * D]
    g_w_ada = _w_ada_grad(c_all, lax.dynamic_slice(dmod_all, (0, me * Ms), (N_DEV, Ms)), "w_ada_grad")[None]

    big23 = _sum_parts(with_own(_exchange_wait(g23_sent, g1_wd[4], False, "g23_wait")[0], own23), "sum_grads23")
    g_rows = dict(w_in=big23[:Ws], w_out=big23[off23[1]:off23[1] + Os],
                  ffn2_gate=big23[off23[2]:off23[2] + Fs], ffn2_up=big23[off23[3]:off23[3] + Fs],
                  ffn2_down=big23[off23[4]:off23[4] + Fs])
    column_sharded = ("w_in", "ffn1_gate", "ffn1_up", "ffn2_gate", "ffn2_up")

    names = ["w_ada", "b_ada", "norm_ffn1", "ffn1_gate", "ffn1_up", "ffn1_down", "norm_mix", "w_in", "conv_w",
             "a_log", "dt_bias", "gdn_norm", "pool_w", "pool_scale", "w_out", "norm_ffn2", "ffn2_gate", "ffn2_up",
             "ffn2_down", "final_norm"]
    weights = dict(zip(names, [w_ada, b_ada, norm_ffn1, ffn1_gate, ffn1_up, ffn1_down, norm_mix, w_in, conv_w,
                               a_log, dt_bias, gdn_norm, pool_w, pool_scale, w_out, norm_ffn2, ffn2_gate, ffn2_up,
                               ffn2_down, final_norm]))
    ms = dict(zip(names, [m_w_ada, m_b_ada, m_norm_ffn1, m_ffn1_gate, m_ffn1_up, m_ffn1_down, m_norm_mix, m_w_in,
                          m_conv_w, m_a_log, m_dt_bias, m_gdn_norm, m_pool_w, m_pool_scale, m_w_out, m_norm_ffn2,
                          m_ffn2_gate, m_ffn2_up, m_ffn2_down, m_final_norm]))
    vs = dict(zip(names, [v_w_ada, v_b_ada, v_norm_ffn1, v_ffn1_gate, v_ffn1_up, v_ffn1_down, v_norm_mix, v_w_in,
                          v_conv_w, v_a_log, v_dt_bias, v_gdn_norm, v_pool_w, v_pool_scale, v_w_out, v_norm_ffn2,
                          v_ffn2_gate, v_ffn2_up, v_ffn2_down, v_final_norm]))
    grads = dict(w_ada=g_w_ada, b_ada=g_b_ada, norm_ffn1=g_n1, norm_mix=g_n2, conv_w=g_conv,
                 a_log=g_al, dt_bias=g_dtb, gdn_norm=g_gn, pool_w=g_pw, pool_scale=g_ps,
                 norm_ffn2=g_n3, final_norm=g_final)
    delta, new_m, new_v = {}, {}, {}

    def adamw_big(n):
        if n in column_sharded:
            view, back = (lambda a: a[0].T), (lambda a: a.T[None])
        else:
            view, back = (lambda a: a[0]), (lambda a: a[None])
        g = g_rows[n] if n in g_rows else view(grads[n])
        d_, m_, v_ = _adamw(view(weights[n]), g, view(ms[n]), view(vs[n]), f"adamw_{n}")
        grads[n], delta[n], new_m[n], new_v[n] = back(g), back(d_), back(m_), back(v_)

    early = ["w_ada", "w_in", "w_out", "ffn2_gate", "ffn2_up", "ffn2_down"]
    late = ["ffn1_gate", "ffn1_up", "ffn1_down"]
    for n in early:
        adamw_big(n)
    done = sum(delta[n][0, :1, :1] for n in early)

    def arrived(started, own, which):
        landed, = _exchange_wait(started, done, False, f"g1_{which}_wait")
        return _sum_parts(with_own(landed, own), f"sum_{which}")

    g_rows["ffn1_gate"] = arrived(g1_wg, own_wg, "dwg")
    g_rows["ffn1_up"] = arrived(g1_wu, own_wu, "dwu")
    g_rows["ffn1_down"] = arrived(g1_wd, own_wd, "dwd")
    for n in late:
        adamw_big(n)
    small_names = [n for n in names if n not in early + late]
    two_d = lambda a: a.reshape(-1, a.shape[-1])
    small_out = _adamw_small(*[[two_d(src[n]) for n in small_names] for src in (weights, grads, ms, vs)],
                             "adamw_small")
    for dst, outs in zip((delta, new_m, new_v), small_out):
        for n, a in zip(small_names, outs):
            dst[n] = a.reshape(weights[n].shape)

    return (loss, grad_x[None], *[grads[n] for n in names], *[delta[n] for n in names],
            *[new_m[n] for n in names], *[new_v[n] for n in names])
```

```python
import functools

import jax
import jax.numpy as jnp
from jax import lax
from jax.experimental import pallas as pl
from jax.experimental.pallas import tpu as pltpu

F32 = jnp.float32
MXU_DTYPE = jnp.bfloat16
WIRE_DTYPE = jnp.bfloat16
EPS = 1e-6
N_DEV = 8
GDN_HEADS = 4
HEAD_DIM = 128
GDN_WIDTH = GDN_HEADS * HEAD_DIM
POOL_WINDOWS = (2, 4, 8, 16)
POOL_GROUPS = len(POOL_WINDOWS)
POOL_WIDTH = 512
CONV_K = 4
CHUNK = 64
QKV_WIDTH = 3 * GDN_WIDTH
D_IN = 4 * GDN_WIDTH + 2 * GDN_HEADS + POOL_WIDTH
D_IN_PAD = 4 * GDN_WIDTH + POOL_WIDTH + 128
COL_Z = QKV_WIDTH
COL_P = 4 * GDN_WIDTH
COL_BA = 4 * GDN_WIDTH + POOL_WIDTH
N_MOD = 9
HALO = 16
VMEM_LIMIT = 56 * 1024 * 1024
ADAM_LR, ADAM_B1, ADAM_B2, ADAM_EPS, ADAM_WD, ADAM_STEP = 0.001, 0.9, 0.999, 1e-08, 0.01, 10
FFN_TOKEN_TILE = 256
FFN_HIDDEN_TILE = 1408
CHUNKS_PER_STEP = 4
SCAN_CHUNKS_PER_STEP = 8

NT = (((1,), (1,)), ((), ()))
NN = (((1,), (0,)), ((), ()))
TN = (((0,), (0,)), ((), ()))


def _params(*sem):
    return pltpu.CompilerParams(dimension_semantics=tuple(sem), vmem_limit_bytes=VMEM_LIMIT)


def _dot(a, b, dims):
    return lax.dot_general(a, b, dims, preferred_element_type=F32)


def _mdot(a, b, dims):
    return _dot(a.astype(MXU_DTYPE), b.astype(MXU_DTYPE), dims)


def _split(a):
    hi = a.astype(jnp.bfloat16)
    return hi, (a - hi.astype(F32)).astype(jnp.bfloat16)


def _dot3(a, b, dims):
    (ah, al), (bh, bl) = a, b
    return (_dot(al, bh, dims) + _dot(ah, bl, dims)) + _dot(ah, bh, dims)


def _sigmoid(v):
    return 0.5 * jnp.tanh(0.5 * v) + 0.5


def _softplus(v):
    return jnp.maximum(v, 0.0) + jnp.log(1.0 + jnp.exp(-jnp.abs(v)))


def _shift_rows(v, s):
    n = v.shape[0]
    s = s % n
    return v if s == 0 else pltpu.roll(v, s, 0)


def _tile(n, want):
    t = min(n, want)
    while n % t:
        t //= 2
    return t


def _all_gather(blocks, name, dep=None):
    n = len(blocks)

    def body(*refs):
        x_refs, out_refs = refs[:n], refs[-3 - n:-3]
        send_sems, recv_sems, local_sems = refs[-3:]
        x, y, c = lax.axis_index("x"), lax.axis_index("y"), lax.axis_index("c")
        me, sibling = (x, y, c), (x, y, 1 - c)
        chips = [(1 - x, y), (x, 1 - y), (1 - x, 1 - y)]

        def copy(a, k, blk, to, own=False):
            rows = out_refs[a].at[4 * blk[0] + 2 * blk[1] + blk[2]]
            return pltpu.make_async_remote_copy(
                src_ref=x_refs[a] if own else rows, dst_ref=rows,
                send_sem=send_sems.at[7 * a + k], recv_sem=recv_sems.at[7 * a + k],
                device_id=to, device_id_type=pl.DeviceIdType.MESH)

        mine = [pltpu.make_async_copy(x_refs[a], out_refs[a].at[4 * x + 2 * y + c], local_sems.at[a])
                for a in range(n)]
        for cp in mine:
            cp.start()
        sent = []
        for a in range(n):
            sent.append(copy(a, 0, me, sibling, own=True))
            sent += [copy(a, 1 + j, me, (*chip, c), own=True) for j, chip in enumerate(chips)]
        for cp in sent:
            cp.start()
        for a in range(n):
            for j, chip in enumerate(chips):
                copy(a, 1 + j, (*chip, c), me).wait_recv()
                passed = copy(a, 4 + j, (*chip, c), sibling)
                passed.start()
                sent.append(passed)
        for a in range(n):
            copy(a, 0, sibling, me).wait_recv()
            for j, chip in enumerate(chips):
                copy(a, 4 + j, (*chip, 1 - c), me).wait_recv()
        for cp in sent:
            cp.wait_send()
        for cp in mine:
            cp.wait()

    hbm = pl.BlockSpec(memory_space=pltpu.HBM)
    return pl.pallas_call(
        body, name=name,
        out_shape=[jax.ShapeDtypeStruct((N_DEV,) + b.shape, b.dtype) for b in blocks],
        in_specs=[hbm] * n + [pl.BlockSpec(memory_space=pl.ANY)] * (dep is not None),
        out_specs=[hbm] * n,
        scratch_shapes=[pltpu.SemaphoreType.DMA((7 * n,)), pltpu.SemaphoreType.DMA((7 * n,)),
                        pltpu.SemaphoreType.DMA((n,))],
    )(*(list(blocks) + ([] if dep is None else [dep])))


_HBM = pl.BlockSpec(memory_space=pltpu.HBM)
_SEM = pl.BlockSpec(memory_space=pltpu.SEMAPHORE)
_ANY = pl.BlockSpec(memory_space=pl.ANY)
_EFFECT = pltpu.SideEffectType.DATAFLOW_SIDE_EFFECTING
_FLIPS = [(0, 0, 1), (0, 1, 0), (0, 1, 1), (1, 0, 0), (1, 0, 1), (1, 1, 0), (1, 1, 1)]


def _peers():
    x, y, c = lax.axis_index("x"), lax.axis_index("y"), lax.axis_index("c")
    return 4 * x + 2 * y + c, [(1 - x if fx else x, 1 - y if fy else y, 1 - c if fc else c)
                               for fx, fy, fc in _FLIPS]


def _exchange_start(srcs, gather, dep, name):
    n = len(srcs)
    lands = [(N_DEV,) + tuple(s.shape if gather else s.shape[1:]) for s in srcs]

    def body(*refs):
        src_refs, land_refs = refs[:n], refs[n:2 * n]
        send_sems, recv_sems = refs[2 * n + 1], refs[2 * n + 2]
        token = refs[-1]
        me, peers = _peers()
        for a in range(n):
            for k, (px, py, pc) in enumerate(peers):
                pltpu.make_async_remote_copy(
                    src_ref=src_refs[a] if gather else src_refs[a].at[4 * px + 2 * py + pc],
                    dst_ref=land_refs[a].at[me],
                    send_sem=send_sems.at[7 * a + k], recv_sem=recv_sems.at[7 * a + k],
                    device_id=(px, py, pc), device_id_type=pl.DeviceIdType.MESH).start()
        token[...] = jnp.zeros_like(token)

    srcs = [pltpu.with_memory_space_constraint(s, pltpu.HBM) for s in srcs]
    empties = [pltpu.with_memory_space_constraint(lax.empty(shape, s.dtype), pltpu.HBM)
               for shape, s in zip(lands, srcs)]
    out = pl.pallas_call(
        body, name=name,
        out_shape=(pltpu.SemaphoreType.DMA((7 * n,)), pltpu.SemaphoreType.DMA((7 * n,)),
                   *[pltpu.HBM(shape, s.dtype) for shape, s in zip(lands, srcs)],
                   jax.ShapeDtypeStruct((8, 128), F32)),
        in_specs=(*[_HBM] * (2 * n), _ANY),
        out_specs=(_SEM, _SEM, *[_HBM] * n, pl.BlockSpec(memory_space=pltpu.VMEM)),
        input_output_aliases={n + a: 2 + a for a in range(n)},
        compiler_params=pltpu.CompilerParams(has_side_effects=_EFFECT),
    )(*srcs, *empties, dep)
    return out[0], out[1], srcs, list(out[2:2 + n]), out[-1]


def _exchange_wait(started, after, gather, name):
    send_sems, recv_sems, srcs, lands, _ = started
    n = len(srcs)

    def body(*refs):
        src_refs, land_refs = refs[:n], refs[n:2 * n]
        send_sems, recv_sems = refs[2 * n], refs[2 * n + 1]
        _, peers = _peers()
        for a in range(n):
            for k, peer in enumerate(peers):
                copy = pltpu.make_async_remote_copy(
                    src_ref=src_refs[a] if gather else src_refs[a].at[0], dst_ref=land_refs[a].at[0],
                    send_sem=send_sems.at[7 * a + k], recv_sem=recv_sems.at[7 * a + k],
                    device_id=peer, device_id_type=pl.DeviceIdType.MESH)
                copy.wait_send()
                copy.wait_recv()

    out = pl.pallas_call(
        body, name=name,
        out_shape=[pltpu.HBM(z.shape, z.dtype) for z in lands],
        in_specs=(*[_HBM] * (2 * n), _SEM, _SEM, _ANY), out_specs=[_HBM] * n,
        input_output_aliases={n + a: a for a in range(n)},
        compiler_params=pltpu.CompilerParams(has_side_effects=_EFFECT),
    )(*srcs, *lands, send_sems, recv_sems, after)
    return list(out)


def _matmul(pairs, dims, out_dtype, name, tm=512, tn=512, tk=512, dep=None):
    a0, b0 = pairs[0]
    if dims == TN:
        K, M = a0.shape
    else:
        M, K = a0.shape
    N = b0.shape[0] if dims == NT else b0.shape[1]
    tm, tn, tk = _tile(M, tm), _tile(N, tn), _tile(K, tk)
    nk = K // tk
    n_pairs = len(pairs)
    n_in = 2 * n_pairs + (dep is not None)

    def body(*refs):
        out_ref = refs[n_in]

        def product():
            total = _dot(refs[0][...], refs[1][...], dims)
            for p in range(1, n_pairs):
                total += _dot(refs[2 * p][...], refs[2 * p + 1][...], dims)
            return total

        if nk == 1:
            out_ref[...] = product().astype(out_ref.dtype)
            return
        acc_ref = refs[n_in + 1]
        k = pl.program_id(2)

        @pl.when(k == 0)
        def _():
            acc_ref[...] = product()

        @pl.when((k > 0) & (k < nk - 1))
        def _():
            acc_ref[...] += product()

        @pl.when(k == nk - 1)
        def _():
            out_ref[...] = (acc_ref[...] + product()).astype(out_ref.dtype)

    if dims == TN:
        a_spec = pl.BlockSpec((tk, tm), lambda i, j, k: (k, i))
    else:
        a_spec = pl.BlockSpec((tm, tk), lambda i, j, k: (i, k))
    if dims == NT:
        b_spec = pl.BlockSpec((tn, tk), lambda i, j, k: (j, k))
    else:
        b_spec = pl.BlockSpec((tk, tn), lambda i, j, k: (k, j))
    args, specs = [], []
    for a, b in pairs:
        args += [a, b]
        specs += [a_spec, b_spec]
    if dep is not None:
        args.append(dep)
        specs.append(_ANY)
    return pl.pallas_call(
        body, name=name, grid=(M // tm, N // tn, nk),
        in_specs=specs, out_specs=pl.BlockSpec((tm, tn), lambda i, j, k: (i, j)),
        out_shape=jax.ShapeDtypeStruct((M, N), out_dtype),
        scratch_shapes=[pltpu.VMEM((tm, tn), F32)] * (nk > 1),
        compiler_params=_params("parallel", "parallel", "arbitrary"),
    )(*args)


def _vec_spec(d):
    return pl.BlockSpec((1, d), lambda i: (0, 0))


def _matmul_resid_norm_mod(a, b, x, gate, coef, nw, scale, shift, name):
    T, K = a.shape
    D = b.shape[1]
    tm = _tile(T, 512)

    def body(a_ref, b_ref, x_ref, g_ref, nw_ref, sc_ref, sh_ref, y_ref, xo_ref, h_ref):
        y = _dot(a_ref[...], b_ref[...], NN)
        y_ref[...] = y
        xf = x_ref[...] + (coef * g_ref[...]) * y
        xo_ref[...] = xf
        r = lax.rsqrt(jnp.mean(xf * xf, axis=-1, keepdims=True) + EPS)
        h_ref[...] = ((xf * r) * nw_ref[...] * (1.0 + sc_ref[...]) + sh_ref[...]).astype(h_ref.dtype)

    row = pl.BlockSpec((tm, D), lambda i: (i, 0))
    vec = _vec_spec(D)
    return pl.pallas_call(
        body, name=name, grid=(T // tm,),
        in_specs=[pl.BlockSpec((tm, K), lambda i: (i, 0)), _resident((K, D)), row, vec, vec, vec, vec],
        out_specs=[row, row, row],
        out_shape=[jax.ShapeDtypeStruct((T, D), F32), jax.ShapeDtypeStruct((T, D), F32),
                   jax.ShapeDtypeStruct((T, D), MXU_DTYPE)],
        compiler_params=_params("parallel"),
    )(a, b, x, gate, nw, scale, shift)


def _norm_bwd(dh, x, nw, scale, dres, name, produced_by=None):
    T, D = x.shape
    tm = _tile(T, 512)

    def body(*refs):
        if isinstance(dh, tuple):
            dh_value = _dot(refs[0][...], refs[1][...], NN)
            refs = refs[1:]
        else:
            dh_value = refs[0][...]
        _, x_ref, nw_ref, sc_ref, dr_ref = refs[:5]
        n_in = 5 if produced_by is None else 7
        dx_ref, dnw_ref, dsc_ref, dsh_ref = refs[n_in:n_in + 4]

        @pl.when(pl.program_id(0) == 0)
        def _():
            dnw_ref[...] = jnp.zeros_like(dnw_ref)
            dsc_ref[...] = jnp.zeros_like(dsc_ref)
            dsh_ref[...] = jnp.zeros_like(dsh_ref)
            if produced_by is not None:
                refs[n_in + 5][...] = jnp.zeros_like(refs[n_in + 5])

        xf, dh_ = x_ref[...], dh_value
        r = lax.rsqrt(jnp.mean(xf * xf, axis=-1, keepdims=True) + EPS)
        xn = xf * r
        one_sc = 1.0 + sc_ref[...]
        dsh_ref[...] += jnp.sum(dh_, axis=0, keepdims=True)
        t = dh_ * xn
        dsc_ref[...] += jnp.sum(t, axis=0, keepdims=True) * nw_ref[...]
        dnw_ref[...] += jnp.sum(t, axis=0, keepdims=True) * one_sc
        dxn = dh_ * (nw_ref[...] * one_sc)
        dx = dr_ref[...] + r * (dxn - xn * jnp.mean(dxn * xn, axis=-1, keepdims=True))
        dx_ref[...] = dx
        if produced_by is not None:
            y_ref, g_ref, dy_ref, dg_ref = refs[5], refs[6], refs[n_in + 4], refs[n_in + 5]
            dy_ref[...] = ((produced_by[2] * g_ref[...]) * dx).astype(dy_ref.dtype)
            dg_ref[...] += produced_by[2] * jnp.sum(dx * y_ref[...], axis=0, keepdims=True)

    row = pl.BlockSpec((tm, D), lambda i: (i, 0))
    vec = _vec_spec(D)
    vec_out = jax.ShapeDtypeStruct((1, D), F32)
    if isinstance(dh, tuple):
        k_dim = dh[0].shape[1]
        args, in_specs = [dh[0], dh[1]], [pl.BlockSpec((tm, k_dim), lambda i: (i, 0)), _resident((k_dim, D))]
    else:
        args, in_specs = [dh], [row]
    args, in_specs = args + [x, nw, scale, dres], in_specs + [row, vec, vec, row]
    out_specs, out_shape = [row, vec, vec, vec], [jax.ShapeDtypeStruct((T, D), F32), vec_out, vec_out, vec_out]
    if produced_by is not None:
        args += [produced_by[0], produced_by[1]]
        in_specs += [row, vec]
        out_specs += [row, vec]
        out_shape += [jax.ShapeDtypeStruct((T, D), MXU_DTYPE), vec_out]
    return pl.pallas_call(
        body, name=name, grid=(T // tm,),
        in_specs=in_specs, out_specs=out_specs, out_shape=out_shape,
        compiler_params=_params("arbitrary"),
    )(*args)


def _rms(xf):
    r = lax.rsqrt(jnp.mean(xf * xf, axis=-1, keepdims=True) + EPS)
    return r, xf * r


def _norm_bwd_math(dh, xf, nw, sc):
    r, xn = _rms(xf)
    t = dh * xn
    dxn = dh * (nw * (1.0 + sc))
    dx = r * (dxn - xn * jnp.mean(dxn * xn, axis=-1, keepdims=True))
    return dx, jnp.sum(dh, axis=0, keepdims=True), jnp.sum(t, axis=0, keepdims=True)


def _swiglu_up(x, wg_t, wu_t, nw, scale, shift, name):
    T, D = x.shape
    Fdim = wg_t.shape[0]
    tm, tf = _tile(T, 512), _tile(Fdim, FFN_HIDDEN_TILE)

    def body(x_ref, wg_ref, wu_ref, nw_ref, sc_ref, sh_ref, g_ref, u_ref, a_ref, h_ref):
        _, xn = _rms(x_ref[...])
        hh = (xn * nw_ref[...] * (1.0 + sc_ref[...]) + sh_ref[...]).astype(MXU_DTYPE)
        h_ref[...] = hh
        for k in range(Fdim // tf):
            ks = slice(k * tf, (k + 1) * tf)
            g = _dot(hh, wg_ref[ks, :], NT)
            u = _dot(hh, wu_ref[ks, :], NT)
            g_ref[:, ks] = g.astype(g_ref.dtype)
            u_ref[:, ks] = u.astype(u_ref.dtype)
            a_ref[:, ks] = ((g * _sigmoid(g)) * u).astype(a_ref.dtype)

    row = pl.BlockSpec((tm, D), lambda i: (i, 0))
    frow = pl.BlockSpec((tm, Fdim), lambda i: (i, 0))
    vec, wres = _vec_spec(D), _resident((Fdim, D))
    return pl.pallas_call(
        body, name=name, grid=(T // tm,),
        in_specs=[row, wres, wres, vec, vec, vec], out_specs=[frow, frow, frow, row],
        out_shape=[jax.ShapeDtypeStruct((T, Fdim), MXU_DTYPE)] * 3 + [jax.ShapeDtypeStruct((T, D), MXU_DTYPE)],
        compiler_params=_params("parallel"),
    )(x, wg_t, wu_t, nw, scale, shift)


def _swiglu_fwd(h, wg_t, wu_t, wd, name, norm_in=None, resid_out=None, loss_out=None):
    T, D = h.shape
    Fdim = wd.shape[0]
    tm, tf = _tile(T, FFN_TOKEN_TILE), _tile(Fdim, FFN_HIDDEN_TILE)
    row = pl.BlockSpec((tm, D), lambda i: (i, 0))
    frow = pl.BlockSpec((tm, Fdim), lambda i: (i, 0))
    vec, wres = _vec_spec(D), _resident((Fdim, D))
    f32_row, mxu_row = jax.ShapeDtypeStruct((T, D), F32), jax.ShapeDtypeStruct((T, D), MXU_DTYPE)
    vec_out = jax.ShapeDtypeStruct((1, D), F32)
    args, in_specs = [h, wg_t, wu_t, wd], [row, wres, wres, wres]
    out_shape, out_specs = [jax.ShapeDtypeStruct((T, Fdim), MXU_DTYPE)] * 3, [frow] * 3
    if norm_in is not None:
        args += list(norm_in)
        in_specs += [vec] * 3
        out_shape, out_specs = out_shape + [mxu_row], out_specs + [row]
    if resid_out is not None:
        x, gate, coef, nw, sc, sh = resid_out
        args += [x, gate, nw, sc, sh]
        in_specs += [row, vec, vec, vec, vec]
        out_shape, out_specs = out_shape + [f32_row, f32_row, mxu_row], out_specs + [row, row, row]
    if loss_out is not None:
        args += list(loss_out)
        in_specs += [row, vec, vec, row]
        out_shape += [jax.ShapeDtypeStruct((1, 128), F32), f32_row, vec_out, mxu_row, vec_out]
        out_specs += [pl.BlockSpec((1, 128), lambda i: (0, 0)), row, vec, row, vec]

    def body(*refs):
        it = iter(refs)
        h_ref, wg_ref, wu_ref, wd_ref = next(it), next(it), next(it), next(it)
        norm_refs = [next(it) for _ in range(3)] if norm_in is not None else None
        resid_refs = [next(it) for _ in range(5)] if resid_out is not None else None
        loss_refs = [next(it) for _ in range(4)] if loss_out is not None else None
        g_ref, u_ref, a_ref = next(it), next(it), next(it)
        if norm_in is not None:
            nw_ref, sc_ref, sh_ref = norm_refs
            _, xn = _rms(h_ref[...])
            hh = (xn * nw_ref[...] * (1.0 + sc_ref[...]) + sh_ref[...]).astype(MXU_DTYPE)
            next(it)[...] = hh
        else:
            hh = h_ref[...]
        y = None
        for k in range(Fdim // tf):
            ks = slice(k * tf, (k + 1) * tf)
            g = _dot(hh, wg_ref[ks, :], NT)
            u = _dot(hh, wu_ref[ks, :], NT)
            a = ((g * _sigmoid(g)) * u).astype(a_ref.dtype)
            g_ref[:, ks] = g.astype(g_ref.dtype)
            u_ref[:, ks] = u.astype(u_ref.dtype)
            a_ref[:, ks] = a
            part = _dot(a, wd_ref[ks, :], NN)
            y = part if y is None else y + part
        if resid_out is not None:
            x_ref, gt_ref, nw_ref, sc_ref, sh_ref = resid_refs
            y_ref, xo_ref, hn_ref = next(it), next(it), next(it)
            y_ref[...] = y
            xf = x_ref[...] + (resid_out[2] * gt_ref[...]) * y
            xo_ref[...] = xf
            _, xn = _rms(xf)
            hn_ref[...] = (xn * nw_ref[...] * (1.0 + sc_ref[...]) + sh_ref[...]).astype(hn_ref.dtype)
        if loss_out is not None:
            x_ref, gt_ref, fw_ref, t_ref = loss_refs
            loss_ref, dx_ref, dfw_ref, dy_ref, dg_ref = [next(it) for _ in range(5)]

            @pl.when(pl.program_id(0) == 0)
            def _():
                loss_ref[...] = jnp.zeros_like(loss_ref)
                dfw_ref[...] = jnp.zeros_like(dfw_ref)
                dg_ref[...] = jnp.zeros_like(dg_ref)

            r, xn = _rms(x_ref[...] + (0.5 * gt_ref[...]) * y)
            err = xn * fw_ref[...] - t_ref[...]
            per_tok = jnp.mean(err * err, axis=-1, keepdims=True)
            loss_ref[...] += 0.5 * jnp.sum(per_tok, axis=0, keepdims=True)
            d_out = err * (1.0 / D)
            dfw_ref[...] += jnp.sum(d_out * xn, axis=0, keepdims=True)
            dxn = d_out * fw_ref[...]
            dx = r * (dxn - xn * jnp.mean(dxn * xn, axis=-1, keepdims=True))
            dx_ref[...] = dx
            dy_ref[...] = ((0.5 * gt_ref[...]) * dx).astype(dy_ref.dtype)
            dg_ref[...] += 0.5 * jnp.sum(dx * y, axis=0, keepdims=True)

    return pl.pallas_call(
        body, name=name, grid=(T // tm,), in_specs=in_specs, out_specs=out_specs, out_shape=out_shape,
        compiler_params=_params("arbitrary" if loss_out is not None else "parallel"),
    )(*args)


def _swiglu_bwd(dy, wd, g, u, wg_t, wu_t, name, norm_in, dres, produced_by=None):
    T, D = dy.shape
    Fdim = wd.shape[0]
    tm, tf = _tile(T, FFN_TOKEN_TILE), _tile(Fdim, FFN_HIDDEN_TILE)
    row = pl.BlockSpec((tm, D), lambda i: (i, 0))
    frow = pl.BlockSpec((tm, Fdim), lambda i: (i, 0))
    vec, wres = _vec_spec(D), _resident((Fdim, D))
    vec_out = jax.ShapeDtypeStruct((1, D), F32)
    args, in_specs = [dy, wd, g, u, wg_t, wu_t, *norm_in, dres], [row, wres, frow, frow, wres, wres, row, vec, vec, row]
    out_shape = [jax.ShapeDtypeStruct((T, Fdim), MXU_DTYPE)] * 2 + [jax.ShapeDtypeStruct((T, D), F32)] + [vec_out] * 3
    out_specs = [frow, frow, row, vec, vec, vec]
    if produced_by is not None:
        args += [produced_by[0], produced_by[1]]
        in_specs += [row, vec]
        out_shape += [jax.ShapeDtypeStruct((T, D), MXU_DTYPE), vec_out]
        out_specs += [row, vec]

    def body(*refs):
        it = iter(refs)
        dy_ref, wd_ref, g_ref, u_ref, wg_ref, wu_ref, x_ref, nw_ref, sc_ref, dr_ref = [next(it) for _ in range(10)]
        prev_refs = [next(it), next(it)] if produced_by is not None else None
        dg_ref, du_ref, dx_ref, dnw_ref, dsc_ref, dsh_ref = [next(it) for _ in range(6)]
        prev_out = [next(it), next(it)] if produced_by is not None else None

        @pl.when(pl.program_id(0) == 0)
        def _():
            dnw_ref[...] = jnp.zeros_like(dnw_ref)
            dsc_ref[...] = jnp.zeros_like(dsc_ref)
            dsh_ref[...] = jnp.zeros_like(dsh_ref)
            if produced_by is not None:
                prev_out[1][...] = jnp.zeros_like(prev_out[1])

        dyy = dy_ref[...]
        dh = None
        for k in range(Fdim // tf):
            ks = slice(k * tf, (k + 1) * tf)
            da = _dot(dyy, wd_ref[ks, :], NT)
            gg = g_ref[:, ks].astype(F32)
            sig = _sigmoid(gg)
            dg = (da * u_ref[:, ks].astype(F32) * (sig * (1.0 + gg * (1.0 - sig)))).astype(dg_ref.dtype)
            du = (da * (gg * sig)).astype(du_ref.dtype)
            dg_ref[:, ks] = dg
            du_ref[:, ks] = du
            part = _dot(dg, wg_ref[ks, :], NN) + _dot(du, wu_ref[ks, :], NN)
            dh = part if dh is None else dh + part
        dx_norm, dsh_row, t_row = _norm_bwd_math(dh, x_ref[...], nw_ref[...], sc_ref[...])
        dsh_ref[...] += dsh_row
        dsc_ref[...] += t_row * nw_ref[...]
        dnw_ref[...] += t_row * (1.0 + sc_ref[...])
        dx = dr_ref[...] + dx_norm
        dx_ref[...] = dx
        if produced_by is not None:
            prev_out[0][...] = ((produced_by[2] * prev_refs[1][...]) * dx).astype(prev_out[0].dtype)
            prev_out[1][...] += produced_by[2] * jnp.sum(dx * prev_refs[0][...], axis=0, keepdims=True)

    return pl.pallas_call(
        body, name=name, grid=(T // tm,), in_specs=in_specs, out_specs=out_specs, out_shape=out_shape,
        compiler_params=_params("arbitrary"),
    )(*args)


def _resident(shape):
    return pl.BlockSpec(shape, lambda i: (0,) * len(shape), pipeline_mode=pl.Buffered(1))


def _conv_act(window, w):
    y = window * w[CONV_K - 1:CONV_K, :]
    for j in range(CONV_K - 1):
        y += _shift_rows(window, CONV_K - 1 - j) * w[j:j + 1, :]
    return y


def _gdn_prep(proj, conv_w, a_log_l, dt_bias_l, name):
    T = proj.shape[0]
    tm = _tile(T, 256)
    hb = tm // 8

    def body(cur_ref, halo_ref, ba_ref, w_ref, al_ref, dtb_ref, qkv_ref, bg_ref):
        i = pl.program_id(0)
        halo = jnp.where(i == 0, 0.0, halo_ref[...])
        window = jnp.concatenate([halo, cur_ref[...]], axis=0)
        y = _conv_act(window, w_ref[...])[8:, :]
        act = y * _sigmoid(y)
        for hh in range(3 * GDN_HEADS):
            blk = act[:, hh * HEAD_DIM:(hh + 1) * HEAD_DIM]
            if hh < 2 * GDN_HEADS:
                rn = lax.rsqrt(jnp.sum(blk * blk, axis=-1, keepdims=True) + EPS)
                blk = blk * rn
                if hh < GDN_HEADS:
                    blk = blk * (HEAD_DIM ** -0.5)
            qkv_ref[:, hh * HEAD_DIM:(hh + 1) * HEAD_DIM] = blk
        ba = ba_ref[...]
        lane = lax.broadcasted_iota(jnp.int32, ba.shape, 1)
        beta = _sigmoid(ba)
        g = -jnp.exp(al_ref[...]) * _softplus(ba + dtb_ref[...])
        bg_ref[...] = jnp.where(lane < GDN_HEADS, beta, jnp.where(lane < 2 * GDN_HEADS, g, 0.0))

    return pl.pallas_call(
        body, name=name, grid=(T // tm,),
        in_specs=[pl.BlockSpec((tm, QKV_WIDTH), lambda i: (i, 0)),
                  pl.BlockSpec((8, QKV_WIDTH), lambda i: (jnp.maximum(i * hb - 1, 0), 0)),
                  pl.BlockSpec((tm, 128), lambda i: (i, COL_BA // 128)),
                  pl.BlockSpec((CONV_K, QKV_WIDTH), lambda i: (0, 0)),
                  pl.BlockSpec((1, 128), lambda i: (0, 0)), pl.BlockSpec((1, 128), lambda i: (0, 0))],
        out_specs=[pl.BlockSpec((tm, QKV_WIDTH), lambda i: (i, 0)), pl.BlockSpec((tm, 128), lambda i: (i, 0))],
        out_shape=[jax.ShapeDtypeStruct((T, QKV_WIDTH), F32), jax.ShapeDtypeStruct((T, 128), F32)],
        compiler_params=_params("parallel"),
    )(proj, proj, proj, conv_w, a_log_l, dt_bias_l)


def _chunk_cumsum(v, reverse=False):
    row = lax.broadcasted_iota(jnp.int32, v.shape, 0)
    s = 1
    while s < CHUNK:
        if reverse:
            v = v + jnp.where(row < CHUNK - s, _shift_rows(v, -s), 0.0)
        else:
            v = v + jnp.where(row >= s, _shift_rows(v, s), 0.0)
        s *= 2
    return v


def _row_form(cols):
    padded = jnp.concatenate([cols, jnp.zeros((128 - CHUNK, 128), F32)], axis=0)
    return padded.T[:, :CHUNK]


def _chunk_masks():
    ri = lax.broadcasted_iota(jnp.int32, (CHUNK, CHUNK), 0)
    ci = lax.broadcasted_iota(jnp.int32, (CHUNK, CHUNK), 1)
    return ri >= ci, ri > ci, (ri == ci).astype(F32)


def _unit_lower_inverses(ms, eye):
    rs = [eye - m for m in ms]
    ps = [_split(-m) for m in ms]
    s = 2
    while s < CHUNK:
        ps = [_split(_dot3(p, p, NN)) for p in ps]
        r_parts = [_split(r) for r in rs]
        rs = [r + _dot3(p, rp, NN) for r, p, rp in zip(rs, ps, r_parts)]
        s *= 2
    return rs


def _head_elementwise(k, beta, gc, gcr, causal):
    decay = jnp.where(causal, jnp.exp(jnp.where(causal, gc - gcr, 0.0)), 0.0)
    return decay, k * beta, jnp.exp(gc)


def _head_slices(hh):
    return (slice(hh * HEAD_DIM, (hh + 1) * HEAD_DIM),
            slice(GDN_WIDTH + hh * HEAD_DIM, GDN_WIDTH + (hh + 1) * HEAD_DIM),
            slice(2 * GDN_WIDTH + hh * HEAD_DIM, 2 * GDN_WIDTH + (hh + 1) * HEAD_DIM))


def _gdn_chunk_fwd(qkv, bg, name):
    T = qkv.shape[0]
    cb = _tile(T // CHUNK, CHUNKS_PER_STEP)
    rows = cb * CHUNK

    def body(qkv_ref, bg_ref, tinv_ref, u_ref, w_ref, qd_ref, kd_ref, p_ref, cd_ref):
        masks = _chunk_masks()
        causal, strict, eye = masks
        heads = []
        for ci in range(cb):
            rs = slice(ci * CHUNK, (ci + 1) * CHUNK)
            bgv = bg_ref[rs, :]
            gc_all = _chunk_cumsum(bgv)
            gc_rows = _row_form(gc_all)
            cd_ref[rs, :] = jnp.exp(jnp.broadcast_to(gc_all[CHUNK - 1:CHUNK, :], (CHUNK, 128)))
            for hh in range(GDN_HEADS):
                qs, ks, vs = _head_slices(hh)
                q, k, v = qkv_ref[rs, qs], qkv_ref[rs, ks], qkv_ref[rs, vs]
                beta = bgv[:, hh:hh + 1]
                gc = gc_all[:, GDN_HEADS + hh:GDN_HEADS + hh + 1]
                decay, kb, eg = _head_elementwise(k, beta, gc, gc_rows[GDN_HEADS + hh:GDN_HEADS + hh + 1, :], causal)
                hs = slice(hh * HEAD_DIM, (hh + 1) * HEAD_DIM)
                cs = slice(hh * CHUNK, (hh + 1) * CHUNK)
                qd_ref[rs, hs] = (q * eg).astype(qd_ref.dtype)
                kd_ref[rs, hs] = (k * jnp.exp(gc[CHUNK - 1:CHUNK, :] - gc)).astype(kd_ref.dtype)
                heads.append((rs, hs, cs, q, k, v * beta, kb, kb * eg, decay))
        kks = [_mdot(kb, k, NT) for (_, _, _, _, k, _, kb, _, _) in heads]
        qks = [_mdot(q, k, NT) for (_, _, _, q, k, _, _, _, _) in heads]
        tinvs = _unit_lower_inverses([jnp.where(strict, kk * hd[8], 0.0) for kk, hd in zip(kks, heads)], eye)
        t_parts = [_split(t) for t in tinvs]
        us = [_dot3(tp, _split(hd[5]), NN) for tp, hd in zip(t_parts, heads)]
        ws = [_dot3(tp, _split(hd[7]), NN) for tp, hd in zip(t_parts, heads)]
        for hd, tinv, u, w, qk in zip(heads, tinvs, us, ws, qks):
            rs, hs, cs = hd[0], hd[1], hd[2]
            tinv_ref[rs, cs] = tinv
            u_ref[rs, hs] = u
            w_ref[rs, hs] = w.astype(w_ref.dtype)
            p_ref[rs, cs] = jnp.where(causal, qk * hd[8], 0.0).astype(p_ref.dtype)

    def spec(width):
        return pl.BlockSpec((rows, width), lambda n: (n, 0))

    hw, cw = GDN_WIDTH, GDN_HEADS * CHUNK
    return pl.pallas_call(
        body, name=name, grid=(T // rows,),
        in_specs=[spec(QKV_WIDTH), spec(128)],
        out_specs=[spec(cw), spec(hw), spec(hw), spec(hw), spec(hw), spec(cw), spec(128)],
        out_shape=[jax.ShapeDtypeStruct((T, cw), F32), jax.ShapeDtypeStruct((T, hw), F32),
                   jax.ShapeDtypeStruct((T, hw), MXU_DTYPE), jax.ShapeDtypeStruct((T, hw), MXU_DTYPE),
                   jax.ShapeDtypeStruct((T, hw), MXU_DTYPE), jax.ShapeDtypeStruct((T, cw), MXU_DTYPE),
                   jax.ShapeDtypeStruct((T, 128), F32)],
        compiler_params=_params("parallel"),
    )(qkv, bg)


def _gdn_scan_fwd(u, w, qd, kd, p, cd, name):
    T = u.shape[0]
    cb = _tile(T // CHUNK, SCAN_CHUNKS_PER_STEP)
    rows = cb * CHUNK

    def body(u_ref, w_ref, qd_ref, kd_ref, p_ref, cd_ref, o_ref, s_all_ref, vn_ref, s_ref):
        @pl.when(pl.program_id(0) == 0)
        def _():
            s_ref[...] = jnp.zeros_like(s_ref)

        hss = [slice(hh * HEAD_DIM, (hh + 1) * HEAD_DIM) for hh in range(GDN_HEADS)]
        css = [slice(hh * CHUNK, (hh + 1) * CHUNK) for hh in range(GDN_HEADS)]
        s_cur = [s_ref[hh] for hh in range(GDN_HEADS)]
        for ci in range(cb):
            rs = slice(ci * CHUNK, (ci + 1) * CHUNK)
            for hh in range(GDN_HEADS):
                s_all_ref[ci * GDN_WIDTH + hh * HEAD_DIM:ci * GDN_WIDTH + (hh + 1) * HEAD_DIM, :] = s_cur[hh]
            s_ms = [s.astype(MXU_DTYPE) for s in s_cur]
            w_s = [_dot(w_ref[rs, hs], s_m, NN) for hs, s_m in zip(hss, s_ms)]
            q_s = [_dot(qd_ref[rs, hs], s_m, NN) for hs, s_m in zip(hss, s_ms)]
            v_ms = [(u_ref[rs, hs] - ws_).astype(MXU_DTYPE) for hs, ws_ in zip(hss, w_s)]
            k_v = [_dot(kd_ref[rs, hs], v_m, TN) for hs, v_m in zip(hss, v_ms)]
            p_v = [_dot(p_ref[rs, cs], v_m, NN) for cs, v_m in zip(css, v_ms)]
            for hh in range(GDN_HEADS):
                vn_ref[rs, hss[hh]] = v_ms[hh]
                o_ref[rs, hss[hh]] = q_s[hh] + p_v[hh]
                c_dec = cd_ref[ci * CHUNK:ci * CHUNK + 1, GDN_HEADS + hh:GDN_HEADS + hh + 1]
                s_cur[hh] = s_cur[hh] * c_dec + k_v[hh]
        for hh in range(GDN_HEADS):
            s_ref[hh] = s_cur[hh]

    def spec(width):
        return pl.BlockSpec((rows, width), lambda n: (n, 0))

    hw, cw = GDN_WIDTH, GDN_HEADS * CHUNK
    return pl.pallas_call(
        body, name=name, grid=(T // rows,),
        in_specs=[spec(hw), spec(hw), spec(hw), spec(hw), spec(cw), spec(128)],
        out_specs=[spec(hw), pl.BlockSpec((cb * GDN_WIDTH, HEAD_DIM), lambda n: (n, 0)), spec(hw)],
        out_shape=[jax.ShapeDtypeStruct((T, hw), F32),
                   jax.ShapeDtypeStruct((T // CHUNK * GDN_WIDTH, HEAD_DIM), F32),
                   jax.ShapeDtypeStruct((T, hw), MXU_DTYPE)],
        scratch_shapes=[pltpu.VMEM((GDN_HEADS, HEAD_DIM, HEAD_DIM), F32)],
        compiler_params=_params("arbitrary"),
    )(u, w, qd, kd, p, cd)


def _gdn_scan_bwd(do, w, qd, kd, p, cd, s_all, vn, name):
    T = do.shape[0]
    cb = _tile(T // CHUNK, SCAN_CHUNKS_PER_STEP)
    rows = cb * CHUNK
    n_steps = T // rows

    def body(do_ref, w_ref, qd_ref, kd_ref, p_ref, cd_ref, s_all_ref, vn_ref,
             dvn_ref, dw_ref, dqd_ref, dkd_ref, dp_ref, dcd_ref, ds_ref):
        @pl.when(pl.program_id(0) == 0)
        def _():
            ds_ref[...] = jnp.zeros_like(ds_ref)

        causal, _, _ = _chunk_masks()
        lane = lax.broadcasted_iota(jnp.int32, (CHUNK, 128), 1)
        heads = range(GDN_HEADS)
        hss = [slice(hh * HEAD_DIM, (hh + 1) * HEAD_DIM) for hh in heads]
        css = [slice(hh * CHUNK, (hh + 1) * CHUNK) for hh in heads]
        ds_cur = [ds_ref[hh] for hh in heads]
        for ci in reversed(range(cb)):
            rs = slice(ci * CHUNK, (ci + 1) * CHUNK)
            ds_ms = [d.astype(MXU_DTYPE) for d in ds_cur]
            s_olds = [s_all_ref[ci * GDN_WIDTH + hh * HEAD_DIM:ci * GDN_WIDTH + (hh + 1) * HEAD_DIM, :] for hh in heads]
            s_ms = [s.astype(MXU_DTYPE) for s in s_olds]
            do_ms = [do_ref[rs, hs].astype(MXU_DTYPE) for hs in hss]
            p_do = [_dot(p_ref[rs, cs], do_m, TN) for cs, do_m in zip(css, do_ms)]
            k_ds = [_dot(kd_ref[rs, hs], ds_m, NN) for hs, ds_m in zip(hss, ds_ms)]
            q_do = [_dot(qd_ref[rs, hs], do_m, TN) for hs, do_m in zip(hss, do_ms)]
            dqds = [_dot(do_m, s_m, NT) for do_m, s_m in zip(do_ms, s_ms)]
            dkds = [_dot(vn_ref[rs, hs], ds_m, NT) for hs, ds_m in zip(hss, ds_ms)]
            dps = [_dot(do_m, vn_ref[rs, hs], NT) for hs, do_m in zip(hss, do_ms)]
            dv_news = [a + b for a, b in zip(p_do, k_ds)]
            dvn_ms = [d.astype(MXU_DTYPE) for d in dv_news]
            w_dv = [_dot(w_ref[rs, hs], dvn_m, TN) for hs, dvn_m in zip(hss, dvn_ms)]
            dws = [_dot(dvn_m, s_m, NT) for dvn_m, s_m in zip(dvn_ms, s_ms)]
            dcd_tile = jnp.zeros((CHUNK, 128), F32)
            for hh in heads:
                dvn_ref[rs, hss[hh]] = dv_news[hh]
                dw_ref[rs, hss[hh]] = -dws[hh]
                dqd_ref[rs, hss[hh]] = dqds[hh]
                dkd_ref[rs, hss[hh]] = dkds[hh]
                dp_ref[rs, css[hh]] = jnp.where(causal, dps[hh], 0.0)
                dcd = jnp.sum(jnp.sum(s_olds[hh] * ds_cur[hh], axis=1, keepdims=True), axis=0, keepdims=True)
                dcd_tile = jnp.where(lane == GDN_HEADS + hh, dcd, dcd_tile)
                c_dec = cd_ref[ci * CHUNK:ci * CHUNK + 1, GDN_HEADS + hh:GDN_HEADS + hh + 1]
                ds_cur[hh] = c_dec * ds_cur[hh] + q_do[hh] - w_dv[hh]
            dcd_ref[rs, :] = dcd_tile
        for hh in heads:
            ds_ref[hh] = ds_cur[hh]

    def spec(width):
        return pl.BlockSpec((rows, width), lambda n: (n_steps - 1 - n, 0))

    hw, cw = GDN_WIDTH, GDN_HEADS * CHUNK
    return pl.pallas_call(
        body, name=name, grid=(n_steps,),
        in_specs=[spec(hw), spec(hw), spec(hw), spec(hw), spec(cw), spec(128),
                  pl.BlockSpec((cb * GDN_WIDTH, HEAD_DIM), lambda n: (n_steps - 1 - n, 0)), spec(hw)],
        out_specs=[spec(hw), spec(hw), spec(hw), spec(hw), spec(cw), spec(128)],
        out_shape=[jax.ShapeDtypeStruct((T, hw), F32)] * 4
        + [jax.ShapeDtypeStruct((T, cw), F32), jax.ShapeDtypeStruct((T, 128), F32)],
        scratch_shapes=[pltpu.VMEM((GDN_HEADS, HEAD_DIM, HEAD_DIM), F32)],
        compiler_params=_params("arbitrary"),
    )(do, w, qd, kd, p, cd, s_all, vn)


def _gdn_chunk_bwd(qkv, bg, tinv_all, u, w, dvn, dw, dqd, dkd, dp, dcd, name):
    T = qkv.shape[0]
    cb = _tile(T // CHUNK, CHUNKS_PER_STEP)
    rows = cb * CHUNK

    def body(qkv_ref, bg_ref, tinv_ref, u_ref, w_ref, dvn_ref, dw_ref, dqd_ref, dkd_ref, dp_ref, dcd_ref,
             dqkv_ref, dbg_ref):
        masks = _chunk_masks()
        causal, strict, _ = masks
        lane = lax.broadcasted_iota(jnp.int32, (CHUNK, 128), 1)
        row = lax.broadcasted_iota(jnp.int32, (CHUNK, 128), 0)
        heads = []
        for ci in range(cb):
            rs = slice(ci * CHUNK, (ci + 1) * CHUNK)
            bgv = bg_ref[rs, :]
            gc_all = _chunk_cumsum(bgv)
            gc_rows = _row_form(gc_all)
            for hh in range(GDN_HEADS):
                qs, ks, vs = _head_slices(hh)
                q, k = qkv_ref[rs, qs], qkv_ref[rs, ks]
                beta = bgv[:, hh:hh + 1]
                gc = gc_all[:, GDN_HEADS + hh:GDN_HEADS + hh + 1]
                decay, kb, eg = _head_elementwise(k, beta, gc, gc_rows[GDN_HEADS + hh:GDN_HEADS + hh + 1, :], causal)
                heads.append(dict(ci=ci, hh=hh, rs=rs, hs=slice(hh * HEAD_DIM, (hh + 1) * HEAD_DIM),
                                  cs=slice(hh * CHUNK, (hh + 1) * CHUNK), q=q, k=k, beta=beta, gc=gc,
                                  decay=decay, kb=kb, eg=eg))
        for hd in heads:
            hd["t"] = _split(tinv_ref[hd["rs"], hd["cs"]])
        for hd in heads:
            hd["kk"] = _mdot(hd["kb"], hd["k"], NT)
            hd["qk"] = _mdot(hd["q"], hd["k"], NT)
        for hd in heads:
            hd["dvb"] = _dot3(hd["t"], _split(dvn_ref[hd["rs"], hd["hs"]]), TN)
            hd["dkbeg"] = _dot3(hd["t"], _split(dw_ref[hd["rs"], hd["hs"]]), TN)
        for hd in heads:
            rs, hs = hd["rs"], hd["hs"]
            da = -(_mdot(hd["dvb"], u_ref[rs, hs], NT) + _mdot(hd["dkbeg"], w_ref[rs, hs], NT))
            dm = jnp.where(strict, da, 0.0)
            dp_ = dp_ref[rs, hd["cs"]]
            hd["dkk"] = dm * hd["decay"]
            hd["dqk"] = dp_ * hd["decay"]
            hd["e"] = (hd["dkk"] * hd["kk"] + hd["dqk"] * hd["qk"])
        for hd in heads:
            hd["dkb"] = _mdot(hd["dkk"], hd["k"], NN)
            hd["dk"] = _mdot(hd["dkk"], hd["kb"], TN) + _mdot(hd["dqk"], hd["q"], TN)
            hd["dq"] = _mdot(hd["dqk"], hd["k"], NN)
            onehot = (lane == GDN_HEADS + hd["hh"]).astype(jnp.bfloat16)
            e_hi, e_lo = _split(hd["e"])
            hd["col_sums"] = _dot(e_lo, onehot, TN) + _dot(e_hi, onehot, TN)
        tiles = {}
        for hd in heads:
            ci, hh, rs, hs = hd["ci"], hd["hh"], hd["rs"], hd["hs"]
            qs, ks, vs = _head_slices(hh)
            q, k, beta, gc, eg, kb = hd["q"], hd["k"], hd["beta"], hd["gc"], hd["eg"], hd["kb"]
            v = qkv_ref[rs, vs]
            dqd_, dkd_ = dqd_ref[rs, hs], dkd_ref[rs, hs]
            gl = gc[CHUNK - 1:CHUNK, :]
            ek = jnp.exp(gl - gc)
            dkb = hd["dkb"] + hd["dkbeg"] * eg
            deg = jnp.sum(dqd_ * q, axis=1, keepdims=True) + jnp.sum(hd["dkbeg"] * kb, axis=1, keepdims=True)
            dek = jnp.sum(dkd_ * k, axis=1, keepdims=True)
            dcd_ = dcd_ref[ci * CHUNK:ci * CHUNK + 1, GDN_HEADS + hh:GDN_HEADS + hh + 1]
            dgl = jnp.sum(dek * ek, axis=0, keepdims=True) + dcd_ * jnp.exp(gl)
            dgc = jnp.sum(hd["e"], axis=1, keepdims=True) + deg * eg - dek * ek
            dbeta_tile, dgc_tile = tiles.get(ci, (jnp.zeros((CHUNK, 128), F32), jnp.zeros((CHUNK, 128), F32)))
            dgc_tile += jnp.where(lane == GDN_HEADS + hh, dgc, 0.0) - hd["col_sums"]
            dgc_tile += jnp.where((lane == GDN_HEADS + hh) & (row == CHUNK - 1), dgl, 0.0)
            dbeta = jnp.sum(dkb * k, axis=1, keepdims=True) + jnp.sum(hd["dvb"] * v, axis=1, keepdims=True)
            dbeta_tile += jnp.where(lane == hh, dbeta, 0.0)
            tiles[ci] = (dbeta_tile, dgc_tile)
            dqkv_ref[rs, qs] = hd["dq"] + dqd_ * eg
            dqkv_ref[rs, ks] = hd["dk"] + dkd_ * ek + dkb * beta
            dqkv_ref[rs, vs] = hd["dvb"] * beta
        for ci in range(cb):
            dbeta_tile, dgc_tile = tiles[ci]
            dbg_ref[ci * CHUNK:(ci + 1) * CHUNK, :] = dbeta_tile + _chunk_cumsum(dgc_tile, reverse=True)

    def spec(width):
        return pl.BlockSpec((rows, width), lambda n: (n, 0))

    hw, cw = GDN_WIDTH, GDN_HEADS * CHUNK
    return pl.pallas_call(
        body, name=name, grid=(T // rows,),
        in_specs=[spec(QKV_WIDTH), spec(128), spec(cw), spec(hw), spec(hw), spec(hw), spec(hw), spec(hw),
                  spec(hw), spec(cw), spec(128)],
        out_specs=[spec(QKV_WIDTH), spec(128)],
        out_shape=[jax.ShapeDtypeStruct((T, QKV_WIDTH), F32), jax.ShapeDtypeStruct((T, 128), F32)],
        compiler_params=_params("parallel"),
    )(qkv, bg, tinv_all, u, w, dvn, dw, dqd, dkd, dp, dcd)


def _pool_counts(i, tm, rows, offset):
    t = i * tm - offset + lax.broadcasted_iota(jnp.int32, (rows, 1), 0)
    return [jnp.minimum(t + 1, w).astype(F32) for w in POOL_WINDOWS]


def _window_sums(window, forward):
    sums, s, step = [], window, 1
    for _ in POOL_WINDOWS:
        s = s + _shift_rows(s, -step if forward else step)
        sums.append(s)
        step *= 2
    return sums


def _pooled(window, counts):
    sums = _window_sums(window, forward=False)
    out = []
    for gi in range(POOL_GROUPS):
        sl = slice(gi * 128, (gi + 1) * 128)
        out.append(sums[gi][HALO:, sl] / counts[gi] - window[HALO:, sl])
    return out


def _mix_post(o, proj, gdn_norm, pool_w, pool_scale, name):
    T = o.shape[0]
    tm = _tile(T, 256)
    hb = tm // HALO

    def body(o_ref, z_ref, p_ref, ph_ref, gn_ref, pw_ref, ps_ref, out_ref):
        i = pl.program_id(0)
        for hh in range(GDN_HEADS):
            sl = slice(hh * HEAD_DIM, (hh + 1) * HEAD_DIM)
            oh, zh = o_ref[:, sl], z_ref[:, sl]
            ro = lax.rsqrt(jnp.mean(oh * oh, axis=-1, keepdims=True) + EPS)
            out_ref[:, sl] = (((oh * ro) * gn_ref[...]) * (zh * _sigmoid(zh))).astype(out_ref.dtype)
        halo = jnp.where(i == 0, 0.0, ph_ref[...])
        window = jnp.concatenate([halo, p_ref[...]], axis=0)
        pooled = _pooled(window, _pool_counts(i, tm, tm, 0))
        for gi in range(POOL_GROUPS):
            pm = _mdot(pooled[gi], pw_ref[gi], NN)
            out_ref[:, GDN_WIDTH + gi * 128:GDN_WIDTH + (gi + 1) * 128] = (
                pm * ps_ref[:, gi * 128:(gi + 1) * 128]).astype(out_ref.dtype)

    return pl.pallas_call(
        body, name=name, grid=(T // tm,),
        in_specs=[pl.BlockSpec((tm, GDN_WIDTH), lambda i: (i, 0)),
                  pl.BlockSpec((tm, GDN_WIDTH), lambda i: (i, COL_Z // GDN_WIDTH)),
                  pl.BlockSpec((tm, POOL_WIDTH), lambda i: (i, COL_P // POOL_WIDTH)),
                  pl.BlockSpec((HALO, POOL_WIDTH), lambda i: (jnp.maximum(i * hb - 1, 0), COL_P // POOL_WIDTH)),
                  pl.BlockSpec((1, HEAD_DIM), lambda i: (0, 0)),
                  pl.BlockSpec((POOL_GROUPS, 128, 128), lambda i: (0, 0, 0)),
                  pl.BlockSpec((1, POOL_WIDTH), lambda i: (0, 0))],
        out_specs=pl.BlockSpec((tm, GDN_WIDTH + POOL_WIDTH), lambda i: (i, 0)),
        out_shape=jax.ShapeDtypeStruct((T, GDN_WIDTH + POOL_WIDTH), MXU_DTYPE),
        compiler_params=_params("parallel"),
    )(o, proj, proj, proj, gdn_norm, pool_w, pool_scale)


def _mix_post_bwd(dmix, o, proj, gdn_norm, pool_w, pool_scale, name):
    T = o.shape[0]
    tm = _tile(T, 256)
    hb = tm // HALO
    n_tiles = T // tm

    def body(dg_ref, dpo_ref, dpo_next_ref, o_ref, z_ref, p_ref, ph_ref, gn_ref, pw_ref, ps_ref,
             do_ref, dzp_ref, dgn_ref, dpw_ref, dps_ref):
        i = pl.program_id(0)

        @pl.when(i == 0)
        def _():
            dgn_ref[...] = jnp.zeros_like(dgn_ref)
            dpw_ref[...] = jnp.zeros_like(dpw_ref)
            dps_ref[...] = jnp.zeros_like(dps_ref)

        gn = gn_ref[...]
        dgn = jnp.zeros((1, HEAD_DIM), F32)
        for hh in range(GDN_HEADS):
            sl = slice(hh * HEAD_DIM, (hh + 1) * HEAD_DIM)
            oh, zh, dy = o_ref[:, sl], z_ref[:, sl], dg_ref[:, sl]
            ro = lax.rsqrt(jnp.mean(oh * oh, axis=-1, keepdims=True) + EPS)
            on = oh * ro
            sig = _sigmoid(zh)
            sz = zh * sig
            dzp_ref[:, sl] = (dy * (on * gn) * (sig * (1.0 + zh * (1.0 - sig)))).astype(dzp_ref.dtype)
            dgn += jnp.sum(dy * on * sz, axis=0, keepdims=True)
            don = dy * gn * sz
            do_ref[:, sl] = ro * (don - on * jnp.mean(don * on, axis=-1, keepdims=True))
        dgn_ref[...] += dgn

        halo = jnp.where(i == 0, 0.0, ph_ref[...])
        window = jnp.concatenate([halo, p_ref[...]], axis=0)
        counts = _pool_counts(i, tm, tm + HALO, 0)
        pooled = _pooled(window, [cn[:tm] for cn in counts])
        nxt = jnp.where(i == n_tiles - 1, 0.0, dpo_next_ref[...])
        dpo_w = jnp.concatenate([dpo_ref[...], nxt], axis=0)
        ps = ps_ref[...]
        dps = []
        scaled = []
        for gi in range(POOL_GROUPS):
            sl = slice(gi * 128, (gi + 1) * 128)
            dpm = dpo_w[:, sl] * ps[:, sl]
            pm = _mdot(pooled[gi], pw_ref[gi], NN)
            dps.append(jnp.sum(dpo_w[:tm, sl] * pm, axis=0, keepdims=True))
            dpw_ref[gi] += _mdot(pooled[gi], dpm[:tm], TN)
            dpooled = _mdot(dpm, pw_ref[gi], NT)
            scaled.append((dpooled, dpooled / counts[gi]))
        dps_ref[...] += jnp.concatenate(dps, axis=1)
        lead = _window_sums(jnp.concatenate([sc for _, sc in scaled], axis=1), forward=True)
        for gi in range(POOL_GROUPS):
            sl = slice(gi * 128, (gi + 1) * 128)
            dzp_ref[:, GDN_WIDTH + gi * 128:GDN_WIDTH + (gi + 1) * 128] = (
                lead[gi][:tm, sl] - scaled[gi][0][:tm]).astype(dzp_ref.dtype)

    last_halo = T // HALO - 1
    return pl.pallas_call(
        body, name=name, grid=(n_tiles,),
        in_specs=[pl.BlockSpec((tm, GDN_WIDTH), lambda i: (i, 0)),
                  pl.BlockSpec((tm, POOL_WIDTH), lambda i: (i, 1)),
                  pl.BlockSpec((HALO, POOL_WIDTH), lambda i: (jnp.minimum((i + 1) * hb, last_halo), 1)),
                  pl.BlockSpec((tm, GDN_WIDTH), lambda i: (i, 0)),
                  pl.BlockSpec((tm, GDN_WIDTH), lambda i: (i, COL_Z // GDN_WIDTH)),
                  pl.BlockSpec((tm, POOL_WIDTH), lambda i: (i, COL_P // POOL_WIDTH)),
                  pl.BlockSpec((HALO, POOL_WIDTH), lambda i: (jnp.maximum(i * hb - 1, 0), COL_P // POOL_WIDTH)),
                  pl.BlockSpec((1, HEAD_DIM), lambda i: (0, 0)),
                  pl.BlockSpec((POOL_GROUPS, 128, 128), lambda i: (0, 0, 0)),
                  pl.BlockSpec((1, POOL_WIDTH), lambda i: (0, 0))],
        out_specs=[pl.BlockSpec((tm, GDN_WIDTH), lambda i: (i, 0)),
                   pl.BlockSpec((tm, GDN_WIDTH + POOL_WIDTH), lambda i: (i, 0)),
                   pl.BlockSpec((1, HEAD_DIM), lambda i: (0, 0)),
                   pl.BlockSpec((POOL_GROUPS, 128, 128), lambda i: (0, 0, 0)),
                   pl.BlockSpec((1, POOL_WIDTH), lambda i: (0, 0))],
        out_shape=[jax.ShapeDtypeStruct((T, GDN_WIDTH), F32),
                   jax.ShapeDtypeStruct((T, GDN_WIDTH + POOL_WIDTH), MXU_DTYPE),
                   jax.ShapeDtypeStruct((1, HEAD_DIM), F32),
                   jax.ShapeDtypeStruct((POOL_GROUPS, 128, 128), F32),
                   jax.ShapeDtypeStruct((1, POOL_WIDTH), F32)],
        compiler_params=_params("arbitrary"),
    )(dmix, dmix, dmix, o, proj, proj, proj, gdn_norm, pool_w, pool_scale)


def _gdn_prep_bwd(proj, conv_w, a_log_l, dt_bias_l, dqkv, dbg, dzp, name):
    T = proj.shape[0]
    tm = _tile(T, 256)
    hb = tm // 8
    n_tiles = T // tm
    last_halo = T // 8 - 1

    def body(cur_ref, before_ref, after_ref, ba_ref, w_ref, al_ref, dtb_ref, dq_ref, dq_after_ref, dbg_ref,
             dzp_ref, dproj_ref, dw_ref, dal_ref, ddtb_ref):
        i = pl.program_id(0)

        @pl.when(i == 0)
        def _():
            dw_ref[...] = jnp.zeros_like(dw_ref)
            dal_ref[...] = jnp.zeros_like(dal_ref)
            ddtb_ref[...] = jnp.zeros_like(ddtb_ref)

        last = i == n_tiles - 1
        w = w_ref[...]
        before = jnp.where(i == 0, 0.0, before_ref[...])
        after = jnp.where(last, 0.0, after_ref[...])
        window = jnp.concatenate([before, cur_ref[...], after], axis=0)
        y = _conv_act(window, w)
        sig = _sigmoid(y)
        act = y * sig
        dq_w = jnp.concatenate([jnp.zeros((8, QKV_WIDTH), F32), dq_ref[...],
                                jnp.where(last, 0.0, dq_after_ref[...])], axis=0)
        dact = []
        for hh in range(3 * GDN_HEADS):
            sl = slice(hh * HEAD_DIM, (hh + 1) * HEAD_DIM)
            blk, dblk = act[:, sl], dq_w[:, sl]
            if hh < 2 * GDN_HEADS:
                rn = lax.rsqrt(jnp.sum(blk * blk, axis=-1, keepdims=True) + EPS)
                unit = blk * rn
                if hh < GDN_HEADS:
                    dblk = dblk * (HEAD_DIM ** -0.5)
                dblk = rn * (dblk - unit * jnp.sum(dblk * unit, axis=-1, keepdims=True))
            dact.append(dblk)
        dy = jnp.concatenate(dact, axis=1) * (sig * (1.0 + y * (1.0 - sig)))
        dx = dy * w[CONV_K - 1:CONV_K, :]
        dws = [None] * CONV_K
        dws[CONV_K - 1] = jnp.sum(dy[8:8 + tm] * window[8:8 + tm], axis=0, keepdims=True)
        for j in range(CONV_K - 1):
            s = CONV_K - 1 - j
            dx += _shift_rows(dy, -s) * w[j:j + 1, :]
            dws[j] = jnp.sum(dy[8:8 + tm] * _shift_rows(window, s)[8:8 + tm], axis=0, keepdims=True)
        dw_ref[...] += jnp.concatenate(dws, axis=0)
        dproj_ref[:, :QKV_WIDTH] = dx[8:8 + tm].astype(dproj_ref.dtype)
        dproj_ref[:, COL_Z:COL_BA] = dzp_ref[...]

        ba = ba_ref[...]
        dbg_ = dbg_ref[...]
        lane = lax.broadcasted_iota(jnp.int32, ba.shape, 1)
        beta = _sigmoid(ba)
        pre = ba + dtb_ref[...]
        neg_a = -jnp.exp(al_ref[...])
        g = neg_a * _softplus(pre)
        is_g = (lane >= GDN_HEADS) & (lane < 2 * GDN_HEADS)
        da_raw = jnp.where(is_g, dbg_ * neg_a * _sigmoid(pre), 0.0)
        dba = jnp.where(lane < GDN_HEADS, dbg_ * beta * (1.0 - beta), da_raw)
        dproj_ref[:, COL_BA:] = dba.astype(dproj_ref.dtype)
        dal_ref[...] += jnp.sum(jnp.where(is_g, dbg_ * g, 0.0), axis=0, keepdims=True)
        ddtb_ref[...] += jnp.sum(da_raw, axis=0, keepdims=True)

    lane_vec = pl.BlockSpec((1, 128), lambda i: (0, 0))
    return pl.pallas_call(
        body, name=name, grid=(n_tiles,),
        in_specs=[pl.BlockSpec((tm, QKV_WIDTH), lambda i: (i, 0)),
                  pl.BlockSpec((8, QKV_WIDTH), lambda i: (jnp.maximum(i * hb - 1, 0), 0)),
                  pl.BlockSpec((8, QKV_WIDTH), lambda i: (jnp.minimum((i + 1) * hb, last_halo), 0)),
                  pl.BlockSpec((tm, 128), lambda i: (i, COL_BA // 128)),
                  pl.BlockSpec((CONV_K, QKV_WIDTH), lambda i: (0, 0)), lane_vec, lane_vec,
                  pl.BlockSpec((tm, QKV_WIDTH), lambda i: (i, 0)),
                  pl.BlockSpec((8, QKV_WIDTH), lambda i: (jnp.minimum((i + 1) * hb, last_halo), 0)),
                  pl.BlockSpec((tm, 128), lambda i: (i, 0)),
                  pl.BlockSpec((tm, GDN_WIDTH + POOL_WIDTH), lambda i: (i, 0))],
        out_specs=[pl.BlockSpec((tm, D_IN_PAD), lambda i: (i, 0)),
                   pl.BlockSpec((CONV_K, QKV_WIDTH), lambda i: (0, 0)), lane_vec, lane_vec],
        out_shape=[jax.ShapeDtypeStruct((T, D_IN_PAD), MXU_DTYPE),
                   jax.ShapeDtypeStruct((CONV_K, QKV_WIDTH), F32),
                   jax.ShapeDtypeStruct((1, 128), F32), jax.ShapeDtypeStruct((1, 128), F32)],
        compiler_params=_params("arbitrary"),
    )(proj, proj, proj, proj, conv_w, a_log_l, dt_bias_l, dqkv, dqkv, dbg, dzp)


def _mod_part(c_all, w_ada, b_part, name):
    def body(c_ref, w_ref, b_ref, out_ref):
        cc = c_ref[...]
        out_ref[...] = _mdot(cc * _sigmoid(cc), w_ref[...], NN) + b_ref[...]

    return pl.pallas_call(
        body, name=name, out_shape=jax.ShapeDtypeStruct((c_all.shape[0], w_ada.shape[1]), F32),
        compiler_params=_params(),
    )(c_all, w_ada, b_part)


def _w_ada_grad(c_all, dmod_part, name):
    def body(c_ref, d_ref, out_ref):
        cc = c_ref[...]
        out_ref[...] = _mdot(cc * _sigmoid(cc), d_ref[...], TN)

    return pl.pallas_call(
        body, name=name, out_shape=jax.ShapeDtypeStruct((c_all.shape[1], dmod_part.shape[1]), F32),
        compiler_params=_params(),
    )(c_all, dmod_part)


def _sum_parts(parts, name):
    _, R, C = parts.shape
    tr = max([t for t in range(16, min(R, 512) + 1, 16) if R % t == 0], default=R)

    def body(p_ref, out_ref):
        acc = p_ref[0].astype(F32)
        for s in range(1, N_DEV):
            acc += p_ref[s].astype(F32)
        out_ref[...] = acc

    return pl.pallas_call(
        body, name=name, grid=(R // tr,),
        in_specs=[pl.BlockSpec((N_DEV, tr, C), lambda i: (0, i, 0))],
        out_specs=pl.BlockSpec((tr, C), lambda i: (i, 0)),
        out_shape=jax.ShapeDtypeStruct((R, C), F32),
        compiler_params=_params("parallel"),
    )(parts)


def _adamw_math(w, g, m, v):
    mm = ADAM_B1 * m + (1.0 - ADAM_B1) * g
    vv = ADAM_B2 * v + (1.0 - ADAM_B2) * (g * g)
    m_hat = mm / (1.0 - ADAM_B1 ** ADAM_STEP)
    v_hat = vv / (1.0 - ADAM_B2 ** ADAM_STEP)
    return -ADAM_LR * (m_hat / (jnp.sqrt(v_hat) + ADAM_EPS) + ADAM_WD * w), mm, vv


def _adamw_small(ws, gs, ms, vs, name):
    n = len(ws)

    def body(*refs):
        for i in range(n):
            d, mm, vv = _adamw_math(*[refs[k * n + i][...] for k in range(4)])
            refs[4 * n + i][...] = d
            refs[5 * n + i][...] = mm
            refs[6 * n + i][...] = vv

    out = pl.pallas_call(
        body, name=name, out_shape=[jax.ShapeDtypeStruct(w.shape, F32) for w in ws] * 3,
        compiler_params=_params(),
    )(*ws, *gs, *ms, *vs)
    return out[:n], out[n:2 * n], out[2 * n:]


def _adamw(w, g, m, v, name):
    R, C = w.shape
    tr = max([t for t in range(8, min(R, 512) + 1, 8) if R % t == 0], default=R)

    def body(w_ref, g_ref, m_ref, v_ref, d_ref, mo_ref, vo_ref):
        d_ref[...], mo_ref[...], vo_ref[...] = _adamw_math(w_ref[...], g_ref[...], m_ref[...], v_ref[...])

    spec = pl.BlockSpec((tr, C), lambda i: (i, 0))
    return pl.pallas_call(
        body, name=name, grid=(R // tr,),
        in_specs=[spec] * 4, out_specs=[spec] * 3,
        out_shape=[jax.ShapeDtypeStruct((R, C), F32)] * 3,
        compiler_params=_params("parallel"),
    )(w, g, m, v)


def _weight_grad(a, b, name, dep=None):
    return _matmul([(a, b)], TN, WIRE_DTYPE, name, tm=1408, tn=1024, tk=1024, dep=dep)


def _rows_of(flat, lanes=1024):
    flat = flat.reshape(-1)
    n = -(-flat.shape[0] // lanes) * lanes
    return jnp.pad(flat, (0, n - flat.shape[0])).reshape(n // lanes, lanes)


def _pad_rows(a, rows):
    return jnp.pad(a, ((0, rows - a.shape[0]), (0, 0)))


def kernel(x, c, w_ada, b_ada, norm_ffn1, ffn1_gate, ffn1_up, ffn1_down, norm_mix, w_in, conv_w, a_log, dt_bias, gdn_norm, pool_w, pool_scale, w_out, norm_ffn2, ffn2_gate, ffn2_up, ffn2_down, final_norm, loss_target, m_w_ada, m_b_ada, m_norm_ffn1, m_ffn1_gate, m_ffn1_up, m_ffn1_down, m_norm_mix, m_w_in, m_conv_w, m_a_log, m_dt_bias, m_gdn_norm, m_pool_w, m_pool_scale, m_w_out, m_norm_ffn2, m_ffn2_gate, m_ffn2_up, m_ffn2_down, m_final_norm, v_w_ada, v_b_ada, v_norm_ffn1, v_ffn1_gate, v_ffn1_up, v_ffn1_down, v_norm_mix, v_w_in, v_conv_w, v_a_log, v_dt_bias, v_gdn_norm, v_pool_w, v_pool_scale, v_w_out, v_norm_ffn2, v_ffn2_gate, v_ffn2_up, v_ffn2_down, v_final_norm):
    T, D = x.shape[1], x.shape[2]
    Fs = ffn1_gate.shape[2]
    Ws = w_in.shape[2]
    Ws_pad = -(-Ws // 16) * 16
    Os = w_out.shape[1]
    Ms = w_ada.shape[2]
    Cs = conv_w.shape[2]
    me = 4 * lax.axis_index("x") + 2 * lax.axis_index("y") + lax.axis_index("c")
    x0, target = x[0], loss_target[0]

    def wire(a):
        return a.astype(WIRE_DTYPE)

    def token(started):
        return started[4][:1, :1]

    def with_own(landed, own):
        return lax.dynamic_update_slice(landed, own[None], (me, 0, 0))

    def full(landed):
        return landed.reshape(-1, D).astype(MXU_DTYPE)

    no_dep = jnp.zeros((8, 128), F32)
    small = jnp.concatenate([_pad_rows(c, 8), _pad_rows(jnp.pad(conv_w[0], ((0, 0), (0, D - Cs))), 8)], axis=0)
    got, = _all_gather([small], "gather_small")
    c_all = got[:, 0, :]
    conv_full = jnp.transpose(got[:, 8:8 + CONV_K, :Cs], (1, 0, 2)).reshape(CONV_K, QKV_WIDTH)
    b_part = lax.dynamic_slice(b_ada, (0, me * Ms), (1, Ms))
    mod_part = _mod_part(c_all, w_ada[0], b_part, "mod_part")

    w1 = [wire(ffn1_gate[0].T), wire(ffn1_up[0].T)]
    w1d = [wire(ffn1_down[0])]
    w2 = [wire(_pad_rows(w_in[0].T, Ws_pad)), wire(w_out[0])]
    w3 = [wire(ffn2_gate[0].T), wire(ffn2_up[0].T), wire(ffn2_down[0])]
    off23 = [0, Ws_pad, Ws_pad + Os, Ws_pad + Os + Fs, Ws_pad + Os + 2 * Fs]
    mod_parts, *w1_all = _all_gather([mod_part] + w1, "gather_mod_w1")
    mod_all = jnp.transpose(mod_parts, (1, 0, 2)).reshape(N_DEV, N_MOD * D)
    mod = lax.dynamic_slice(mod_all, (me, 0), (1, N_MOD * D)).reshape(N_MOD, 1, D)
    sh1, sc1, gt1, sh2, sc2, gt2, sh3, sc3, gt3 = [mod[i] for i in range(N_MOD)]
    wg1_t, wu1_t = [full(w) for w in w1_all]
    w1d_sent = _exchange_start(w1d, True, w1_all[0], "w1d_start")
    w2_sent = _exchange_start(w2, True, w1d_sent[4], "w2_start")
    w3_sent = _exchange_start(w3, True, w2_sent[4], "w3_start")

    lane_pad = lambda a: jnp.pad(a, ((0, 0), (GDN_HEADS, 128 - 2 * GDN_HEADS)))
    a_log_l, dt_bias_l = lane_pad(a_log), lane_pad(dt_bias)
    pool_w_m = pool_w[0].astype(MXU_DTYPE)

    g1, u1, a1, h1 = _swiglu_up(x0, wg1_t, wu1_t, norm_ffn1, sc1 + token(w3_sent), sh1, "ffn1_up")
    wd1 = full(with_own(_exchange_wait(w1d_sent, h1, True, "w1d_wait")[0], w1d[0]))
    y1, x1, h2 = _matmul_resid_norm_mod(a1, wd1, x0, gt1, 0.5, norm_mix, sc2, sh2, "ffn1_down")
    w_in_all, wo_all = [with_own(z, own) for z, own in zip(_exchange_wait(w2_sent, h2, True, "w2_wait"), w2)]
    w_in_t = w_in_all[:, :Ws, :].reshape(-1, D).astype(MXU_DTYPE)
    wo = full(wo_all)
    w_in_re = jnp.concatenate([w_in_t[:COL_Z + GDN_WIDTH], w_in_t[D_IN - POOL_WIDTH:],
                               w_in_t[4 * GDN_WIDTH:4 * GDN_WIDTH + 2 * GDN_HEADS],
                               jnp.zeros((128 - 2 * GDN_HEADS, D), MXU_DTYPE)], axis=0)
    proj = _matmul([(h2, w_in_re)], NT, F32, "proj_in", tm=512, tn=D_IN_PAD, tk=D)
    qkv, bg = _gdn_prep(proj, conv_full, a_log_l, dt_bias_l, "gdn_prep")
    tinv, u_c, w_c, qd_c, kd_c, p_c, cd_c = _gdn_chunk_fwd(qkv, bg, "gdn_chunk_fwd")
    o, s_all, vn_c = _gdn_scan_fwd(u_c, w_c, qd_c, kd_c, p_c, cd_c, "gdn_scan_fwd")
    mix_in = _mix_post(o, proj, gdn_norm, pool_w_m, pool_scale, "mix_post")
    mixed, x2, h3 = _matmul_resid_norm_mod(mix_in, wo, x1, gt2, 1.0, norm_ffn2, sc3, sh3, "mix_out")
    wg2_t, wu2_t, wd2 = [full(with_own(z, own))
                         for z, own in zip(_exchange_wait(w3_sent, h3, True, "w3_wait"), w3)]
    g3, u3, a3, loss_row, d3, d_final, dy3, dgt3 = _swiglu_fwd(
        h3, wg2_t, wu2_t, wd2, "ffn2_fwd_loss", loss_out=(x2, gt3, final_norm.reshape(1, D), target))

    dg3, du3, d2, d_n3, dsc3, dsh3, dmixed, dgt2 = _swiglu_bwd(
        dy3, wd2, g3, u3, wg2_t, wu2_t, "ffn2_bwd_norm3_bwd", (x2, norm_ffn2, sc3), d3,
        produced_by=(mixed, gt2, 1.0))
    d_wd2 = _weight_grad(a3, dy3, "ffn2_dwd")
    d_wg2 = _weight_grad(dg3, h3, "ffn2_dwg")
    d_wu2 = _weight_grad(du3, h3, "ffn2_dwu")
    dmix_in = _matmul([(dmixed, wo)], NT, F32, "mix_out_bwd", tm=512, tn=GDN_WIDTH + POOL_WIDTH, tk=D)
    d_wo = _matmul([(mix_in, dmixed)], TN, WIRE_DTYPE, "mix_dwo", tm=GDN_WIDTH + POOL_WIDTH, tn=D, tk=1024)
    do, dzp, d_gn, d_pw, d_ps = _mix_post_bwd(dmix_in, o, proj, gdn_norm, pool_w_m, pool_scale, "mix_post_bwd")
    dvn, dw_c, dqd, dkd, dp_c, dcd = _gdn_scan_bwd(do, w_c, qd_c, kd_c, p_c, cd_c, s_all, vn_c, "gdn_scan_bwd")
    dqkv, dbg = _gdn_chunk_bwd(qkv, bg, tinv, u_c, w_c, dvn, dw_c, dqd, dkd, dp_c, dcd, "gdn_chunk_bwd")
    dproj, d_conv, d_al, d_dtb = _gdn_prep_bwd(proj, conv_full, a_log_l, dt_bias_l, dqkv, dbg, dzp, "gdn_prep_bwd")
    d_win_re = _matmul([(dproj, h2)], TN, WIRE_DTYPE, "proj_in_dw", tm=D_IN_PAD, tn=D, tk=1024)
    d_win_t = jnp.concatenate([d_win_re[:COL_Z + GDN_WIDTH], d_win_re[COL_BA:COL_BA + 2 * GDN_HEADS],
                               d_win_re[COL_P:COL_P + POOL_WIDTH]], axis=0)
    d_win_blocks = jnp.pad(d_win_t.reshape(N_DEV, Ws, D), ((0, 0), (0, Ws_pad - Ws), (0, 0)))
    parts23 = jnp.concatenate(
        [wire(d_win_blocks), wire(d_wo.reshape(N_DEV, Os, D)), wire(d_wg2.reshape(N_DEV, Fs, D)),
         wire(d_wu2.reshape(N_DEV, Fs, D)), wire(d_wd2.reshape(N_DEV, Fs, D))], axis=1)
    own23 = lax.dynamic_index_in_dim(parts23, me, 0, keepdims=False)
    g23_sent = _exchange_start([parts23], False, no_dep, "g23_start")
    d1, d_n2, dsc2, dsh2, dy1, dgt1 = _norm_bwd((dproj, w_in_re), x1, norm_mix, sc2 + token(g23_sent), d2,
                                                "proj_in_bwd_norm2_bwd", produced_by=(y1, gt1, 0.5))
    dg1, du1, grad_x, d_n1, dsc1, dsh1 = _swiglu_bwd(
        dy1, wd1, g1, u1, wg1_t, wu1_t, "ffn1_bwd_norm1_bwd", (x0, norm_ffn1, sc1), d1)

    dmod = jnp.concatenate([dsh1, dsc1, dgt1, dsh2, dsc2, dgt2, dsh3, dsc3, dgt3], axis=0)
    small_rows = [dmod.reshape(-1), d_n1[0], d_n2[0], d_n3[0], d_final[0], d_gn[0], d_ps[0],
                  d_al[0, GDN_HEADS:2 * GDN_HEADS], d_dtb[0, GDN_HEADS:2 * GDN_HEADS], loss_row[0, :1],
                  d_conv.reshape(-1), d_pw.reshape(-1)]
    lanes = 1024
    small_rows = [_rows_of(r, lanes) for r in small_rows]
    n_rows = [r.shape[0] for r in small_rows]
    row_off = [sum(n_rows[:i]) for i in range(len(n_rows))]
    total = -(-sum(n_rows) // 8) * 8
    slab = _pad_rows(jnp.concatenate(small_rows, axis=0), total)
    slab_sent = _exchange_start([slab], True, no_dep, "small_grads_start")

    def send_ffn1(a, b, which, dep):
        parts = _weight_grad(a, b, f"ffn1_{which}", dep=dep).reshape(N_DEV, Fs, D)
        own = lax.dynamic_index_in_dim(parts, me, 0, keepdims=False)
        return _exchange_start([parts], False, no_dep, f"g1_{which}_start"), own

    g1_wg, own_wg = send_ffn1(dg1, h1, "dwg", slab_sent[4])
    g1_wu, own_wu = send_ffn1(du1, h1, "dwu", g1_wg[4])
    g1_wd, own_wd = send_ffn1(a1, dy1, "dwd", g1_wu[4])

    slab_all = with_own(_exchange_wait(slab_sent, g1_wg[4], True, "small_grads_wait")[0], slab)
    summed = _sum_parts(slab_all, "sum_small_grads")

    def piece(idx, n):
        return summed[row_off[idx]:row_off[idx] + n_rows[idx]].reshape(-1)[:n]

    g_b_ada = piece(0, N_MOD * D).reshape(1, N_MOD * D)
    g_n1, g_n2, g_n3 = piece(1, D).reshape(1, D), piece(2, D).reshape(1, D), piece(3, D).reshape(1, D)
    g_final = piece(4, D)
    g_gn = piece(5, HEAD_DIM).reshape(1, HEAD_DIM)
    g_ps = piece(6, POOL_WIDTH).reshape(1, POOL_WIDTH)
    g_al = piece(7, GDN_HEADS).reshape(1, GDN_HEADS)
    g_dtb = piece(8, GDN_HEADS).reshape(1, GDN_HEADS)
    loss = piece(9, 1)[0]
    g_conv = lax.dynamic_slice(piece(10, CONV_K * QKV_WIDTH).reshape(1, CONV_K, QKV_WIDTH), (0, 0, me * Cs),
                               (1, CONV_K, Cs))
    g_pw = piece(11, POOL_GROUPS * 128 * 128).reshape(1, POOL_GROUPS, 128, 128)

    dmod_all = slab_all[:, row_off[0]:row_off[0] + n_rows[0], :].reshape(N_DEV, -1)[:, :N_MOD * D]
    g_w_ada = _w_ada_grad(c_all, lax.dynamic_slice(dmod_all, (0, me * Ms), (N_DEV, Ms)), "w_ada_grad")[None]

    big23 = _sum_parts(with_own(_exchange_wait(g23_sent, g1_wd[4], False, "g23_wait")[0], own23), "sum_grads23")
    g_rows = dict(w_in=big23[:Ws], w_out=big23[off23[1]:off23[1] + Os],
                  ffn2_gate=big23[off23[2]:off23[2] + Fs], ffn2_up=big23[off23[3]:off23[3] + Fs],
                  ffn2_down=big23[off23[4]:off23[4] + Fs])
    column_sharded = ("w_in", "ffn1_gate", "ffn1_up", "ffn2_gate", "ffn2_up")

    names = ["w_ada", "b_ada", "norm_ffn1", "ffn1_gate", "ffn1_up", "ffn1_down", "norm_mix", "w_in", "conv_w",
             "a_log", "dt_bias", "gdn_norm", "pool_w", "pool_scale", "w_out", "norm_ffn2", "ffn2_gate", "ffn2_up",
             "ffn2_down", "final_norm"]
    weights = dict(zip(names, [w_ada, b_ada, norm_ffn1, ffn1_gate, ffn1_up, ffn1_down, norm_mix, w_in, conv_w,
                               a_log, dt_bias, gdn_norm, pool_w, pool_scale, w_out, norm_ffn2, ffn2_gate, ffn2_up,
                               ffn2_down, final_norm]))
    ms = dict(zip(names, [m_w_ada, m_b_ada, m_norm_ffn1, m_ffn1_gate, m_ffn1_up, m_ffn1_down, m_norm_mix, m_w_in,
                          m_conv_w, m_a_log, m_dt_bias, m_gdn_norm, m_pool_w, m_pool_scale, m_w_out, m_norm_ffn2,
                          m_ffn2_gate, m_ffn2_up, m_ffn2_down, m_final_norm]))
    vs = dict(zip(names, [v_w_ada, v_b_ada, v_norm_ffn1, v_ffn1_gate, v_ffn1_up, v_ffn1_down, v_norm_mix, v_w_in,
                          v_conv_w, v_a_log, v_dt_bias, v_gdn_norm, v_pool_w, v_pool_scale, v_w_out, v_norm_ffn2,
                          v_ffn2_gate, v_ffn2_up, v_ffn2_down, v_final_norm]))
    grads = dict(w_ada=g_w_ada, b_ada=g_b_ada, norm_ffn1=g_n1, norm_mix=g_n2, conv_w=g_conv,
                 a_log=g_al, dt_bias=g_dtb, gdn_norm=g_gn, pool_w=g_pw, pool_scale=g_ps,
                 norm_ffn2=g_n3, final_norm=g_final)
    delta, new_m, new_v = {}, {}, {}

    def adamw_big(n):
        if n in column_sharded:
            view, back = (lambda a: a[0].T), (lambda a: a.T[None])
        else:
            view, back = (lambda a: a[0]), (lambda a: a[None])
        g = g_rows[n] if n in g_rows else view(grads[n])
        d_, m_, v_ = _adamw(view(weights[n]), g, view(ms[n]), view(vs[n]), f"adamw_{n}")
        grads[n], delta[n], new_m[n], new_v[n] = back(g), back(d_), back(m_), back(v_)

    early = ["w_ada", "w_in", "w_out", "ffn2_gate", "ffn2_up", "ffn2_down"]
    late = ["ffn1_gate", "ffn1_up", "ffn1_down"]
    for n in early:
        adamw_big(n)
    done = sum(delta[n][0, :1, :1] for n in early)

    def arrived(started, own, which):
        landed, = _exchange_wait(started, done, False, f"g1_{which}_wait")
        return _sum_parts(with_own(landed, own), f"sum_{which}")

    g_rows["ffn1_gate"] = arrived(g1_wg, own_wg, "dwg")
    g_rows["ffn1_up"] = arrived(g1_wu, own_wu, "dwu")
    g_rows["ffn1_down"] = arrived(g1_wd, own_wd, "dwd")
    for n in late:
        adamw_big(n)
    small_names = [n for n in names if n not in early + late]
    two_d = lambda a: a.reshape(-1, a.shape[-1])
    small_out = _adamw_small(*[[two_d(src[n]) for n in small_names] for src in (weights, grads, ms, vs)],
                             "adamw_small")
    for dst, outs in zip((delta, new_m, new_v), small_out):
        for n, a in zip(small_names, outs):
            dst[n] = a.reshape(weights[n].shape)

    return (loss, grad_x[None], *[grads[n] for n in names], *[delta[n] for n in names],
            *[new_m[n] for n in names], *[new_v[n] for n in names])
```

```python
import functools

import jax
import jax.numpy as jnp
from jax import lax
from jax.experimental import pallas as pl
from jax.experimental.pallas import tpu as pltpu

F32 = jnp.float32
MXU_DTYPE = jnp.bfloat16
WIRE_DTYPE = jnp.bfloat16
EPS = 1e-6
N_DEV = 8
GDN_HEADS = 4
HEAD_DIM = 128
GDN_WIDTH = GDN_HEADS * HEAD_DIM
POOL_WINDOWS = (2, 4, 8, 16)
POOL_GROUPS = len(POOL_WINDOWS)
POOL_WIDTH = 512
CONV_K = 4
CHUNK = 64
QKV_WIDTH = 3 * GDN_WIDTH
D_IN = 4 * GDN_WIDTH + 2 * GDN_HEADS + POOL_WIDTH
D_IN_PAD = 4 * GDN_WIDTH + POOL_WIDTH + 128
COL_Z = QKV_WIDTH
COL_P = 4 * GDN_WIDTH
COL_BA = 4 * GDN_WIDTH + POOL_WIDTH
N_MOD = 9
HALO = 16
VMEM_LIMIT = 56 * 1024 * 1024
ADAM_LR, ADAM_B1, ADAM_B2, ADAM_EPS, ADAM_WD, ADAM_STEP = 0.001, 0.9, 0.999, 1e-08, 0.01, 10
FFN_TOKEN_TILE = 256
FFN_HIDDEN_TILE = 1408
CHUNKS_PER_STEP = 4
SCAN_CHUNKS_PER_STEP = 8

NT = (((1,), (1,)), ((), ()))
NN = (((1,), (0,)), ((), ()))
TN = (((0,), (0,)), ((), ()))


def _params(*sem):
    return pltpu.CompilerParams(dimension_semantics=tuple(sem), vmem_limit_bytes=VMEM_LIMIT)


def _dot(a, b, dims):
    return lax.dot_general(a, b, dims, preferred_element_type=F32)


def _mdot(a, b, dims):
    return _dot(a.astype(MXU_DTYPE), b.astype(MXU_DTYPE), dims)


def _split(a):
    hi = a.astype(jnp.bfloat16)
    return hi, (a - hi.astype(F32)).astype(jnp.bfloat16)


def _dot3(a, b, dims):
    (ah, al), (bh, bl) = a, b
    return (_dot(al, bh, dims) + _dot(ah, bl, dims)) + _dot(ah, bh, dims)


def _sigmoid(v):
    return 0.5 * jnp.tanh(0.5 * v) + 0.5


def _softplus(v):
    return jnp.maximum(v, 0.0) + jnp.log(1.0 + jnp.exp(-jnp.abs(v)))


def _shift_rows(v, s):
    n = v.shape[0]
    s = s % n
    return v if s == 0 else pltpu.roll(v, s, 0)


def _tile(n, want):
    t = min(n, want)
    while n % t:
        t //= 2
    return t


def _all_gather(blocks, name, dep=None):
    n = len(blocks)

    def body(*refs):
        x_refs, out_refs = refs[:n], refs[-3 - n:-3]
        send_sems, recv_sems, local_sems = refs[-3:]
        x, y, c = lax.axis_index("x"), lax.axis_index("y"), lax.axis_index("c")
        me, sibling = (x, y, c), (x, y, 1 - c)
        chips = [(1 - x, y), (x, 1 - y), (1 - x, 1 - y)]

        def copy(a, k, blk, to, own=False):
            rows = out_refs[a].at[4 * blk[0] + 2 * blk[1] + blk[2]]
            return pltpu.make_async_remote_copy(
                src_ref=x_refs[a] if own else rows, dst_ref=rows,
                send_sem=send_sems.at[7 * a + k], recv_sem=recv_sems.at[7 * a + k],
                device_id=to, device_id_type=pl.DeviceIdType.MESH)

        mine = [pltpu.make_async_copy(x_refs[a], out_refs[a].at[4 * x + 2 * y + c], local_sems.at[a])
                for a in range(n)]
        for cp in mine:
            cp.start()
        sent = []
        for a in range(n):
            sent.append(copy(a, 0, me, sibling, own=True))
            sent += [copy(a, 1 + j, me, (*chip, c), own=True) for j, chip in enumerate(chips)]
        for cp in sent:
            cp.start()
        for a in range(n):
            for j, chip in enumerate(chips):
                copy(a, 1 + j, (*chip, c), me).wait_recv()
                passed = copy(a, 4 + j, (*chip, c), sibling)
                passed.start()
                sent.append(passed)
        for a in range(n):
            copy(a, 0, sibling, me).wait_recv()
            for j, chip in enumerate(chips):
                copy(a, 4 + j, (*chip, 1 - c), me).wait_recv()
        for cp in sent:
            cp.wait_send()
        for cp in mine:
            cp.wait()

    hbm = pl.BlockSpec(memory_space=pltpu.HBM)
    return pl.pallas_call(
        body, name=name,
        out_shape=[jax.ShapeDtypeStruct((N_DEV,) + b.shape, b.dtype) for b in blocks],
        in_specs=[hbm] * n + [pl.BlockSpec(memory_space=pl.ANY)] * (dep is not None),
        out_specs=[hbm] * n,
        scratch_shapes=[pltpu.SemaphoreType.DMA((7 * n,)), pltpu.SemaphoreType.DMA((7 * n,)),
                        pltpu.SemaphoreType.DMA((n,))],
    )(*(list(blocks) + ([] if dep is None else [dep])))


_HBM = pl.BlockSpec(memory_space=pltpu.HBM)
_SEM = pl.BlockSpec(memory_space=pltpu.SEMAPHORE)
_ANY = pl.BlockSpec(memory_space=pl.ANY)
_EFFECT = pltpu.SideEffectType.DATAFLOW_SIDE_EFFECTING
_FLIPS = [(0, 0, 1), (0, 1, 0), (0, 1, 1), (1, 0, 0), (1, 0, 1), (1, 1, 0), (1, 1, 1)]


def _peers():
    x, y, c = lax.axis_index("x"), lax.axis_index("y"), lax.axis_index("c")
    return 4 * x + 2 * y + c, [(1 - x if fx else x, 1 - y if fy else y, 1 - c if fc else c)
                               for fx, fy, fc in _FLIPS]


def _exchange_start(srcs, gather, dep, name):
    n = len(srcs)
    lands = [(N_DEV,) + tuple(s.shape if gather else s.shape[1:]) for s in srcs]

    def body(*refs):
        src_refs, land_refs = refs[:n], refs[n:2 * n]
        send_sems, recv_sems = refs[2 * n + 1], refs[2 * n + 2]
        token = refs[-1]
        me, peers = _peers()
        for a in range(n):
            for k, (px, py, pc) in enumerate(peers):
                pltpu.make_async_remote_copy(
                    src_ref=src_refs[a] if gather else src_refs[a].at[4 * px + 2 * py + pc],
                    dst_ref=land_refs[a].at[me],
                    send_sem=send_sems.at[7 * a + k], recv_sem=recv_sems.at[7 * a + k],
                    device_id=(px, py, pc), device_id_type=pl.DeviceIdType.MESH).start()
        token[...] = jnp.zeros_like(token)

    srcs = [pltpu.with_memory_space_constraint(s, pltpu.HBM) for s in srcs]
    empties = [pltpu.with_memory_space_constraint(lax.empty(shape, s.dtype), pltpu.HBM)
               for shape, s in zip(lands, srcs)]
    out = pl.pallas_call(
        body, name=name,
        out_shape=(pltpu.SemaphoreType.DMA((7 * n,)), pltpu.SemaphoreType.DMA((7 * n,)),
                   *[pltpu.HBM(shape, s.dtype) for shape, s in zip(lands, srcs)],
                   jax.ShapeDtypeStruct((8, 128), F32)),
        in_specs=(*[_HBM] * (2 * n), _ANY),
        out_specs=(_SEM, _SEM, *[_HBM] * n, pl.BlockSpec(memory_space=pltpu.VMEM)),
        input_output_aliases={n + a: 2 + a for a in range(n)},
        compiler_params=pltpu.CompilerParams(has_side_effects=_EFFECT),
    )(*srcs, *empties, dep)
    return out[0], out[1], srcs, list(out[2:2 + n]), out[-1]


def _exchange_wait(started, after, gather, name):
    send_sems, recv_sems, srcs, lands, _ = started
    n = len(srcs)

    def body(*refs):
        src_refs, land_refs = refs[:n], refs[n:2 * n]
        send_sems, recv_sems = refs[2 * n], refs[2 * n + 1]
        _, peers = _peers()
        for a in range(n):
            for k, peer in enumerate(peers):
                copy = pltpu.make_async_remote_copy(
                    src_ref=src_refs[a] if gather else src_refs[a].at[0], dst_ref=land_refs[a].at[0],
                    send_sem=send_sems.at[7 * a + k], recv_sem=recv_sems.at[7 * a + k],
                    device_id=peer, device_id_type=pl.DeviceIdType.MESH)
                copy.wait_send()
                copy.wait_recv()

    out = pl.pallas_call(
        body, name=name,
        out_shape=[pltpu.HBM(z.shape, z.dtype) for z in lands],
        in_specs=(*[_HBM] * (2 * n), _SEM, _SEM, _ANY), out_specs=[_HBM] * n,
        input_output_aliases={n + a: a for a in range(n)},
        compiler_params=pltpu.CompilerParams(has_side_effects=_EFFECT),
    )(*srcs, *lands, send_sems, recv_sems, after)
    return list(out)


def _matmul(pairs, dims, out_dtype, name, tm=512, tn=512, tk=512, dep=None):
    a0, b0 = pairs[0]
    if dims == TN:
        K, M = a0.shape
    else:
        M, K = a0.shape
    N = b0.shape[0] if dims == NT else b0.shape[1]
    tm, tn, tk = _tile(M, tm), _tile(N, tn), _tile(K, tk)
    nk = K // tk
    n_pairs = len(pairs)
    n_in = 2 * n_pairs + (dep is not None)

    def body(*refs):
        out_ref = refs[n_in]

        def product():
            total = _dot(refs[0][...], refs[1][...], dims)
            for p in range(1, n_pairs):
                total += _dot(refs[2 * p][...], refs[2 * p + 1][...], dims)
            return total

        if nk == 1:
            out_ref[...] = product().astype(out_ref.dtype)
            return
        acc_ref = refs[n_in + 1]
        k = pl.program_id(2)

        @pl.when(k == 0)
        def _():
            acc_ref[...] = product()

        @pl.when((k > 0) & (k < nk - 1))
        def _():
            acc_ref[...] += product()

        @pl.when(k == nk - 1)
        def _():
            out_ref[...] = (acc_ref[...] + product()).astype(out_ref.dtype)

    if dims == TN:
        a_spec = pl.BlockSpec((tk, tm), lambda i, j, k: (k, i))
    else:
        a_spec = pl.BlockSpec((tm, tk), lambda i, j, k: (i, k))
    if dims == NT:
        b_spec = pl.BlockSpec((tn, tk), lambda i, j, k: (j, k))
    else:
        b_spec = pl.BlockSpec((tk, tn), lambda i, j, k: (k, j))
    args, specs = [], []
    for a, b in pairs:
        args += [a, b]
        specs += [a_spec, b_spec]
    if dep is not None:
        args.append(dep)
        specs.append(_ANY)
    return pl.pallas_call(
        body, name=name, grid=(M // tm, N // tn, nk),
        in_specs=specs, out_specs=pl.BlockSpec((tm, tn), lambda i, j, k: (i, j)),
        out_shape=jax.ShapeDtypeStruct((M, N), out_dtype),
        scratch_shapes=[pltpu.VMEM((tm, tn), F32)] * (nk > 1),
        compiler_params=_params("parallel", "parallel", "arbitrary"),
    )(*args)


def _vec_spec(d):
    return pl.BlockSpec((1, d), lambda i: (0, 0))


def _matmul_resid_norm_mod(a, b, x, gate, coef, nw, scale, shift, name):
    T, K = a.shape
    D = b.shape[1]
    tm = _tile(T, 512)

    def body(a_ref, b_ref, x_ref, g_ref, nw_ref, sc_ref, sh_ref, y_ref, xo_ref, h_ref):
        y = _dot(a_ref[...], b_ref[...], NN)
        y_ref[...] = y
        xf = x_ref[...] + (coef * g_ref[...]) * y
        xo_ref[...] = xf
        r = lax.rsqrt(jnp.mean(xf * xf, axis=-1, keepdims=True) + EPS)
        h_ref[...] = ((xf * r) * nw_ref[...] * (1.0 + sc_ref[...]) + sh_ref[...]).astype(h_ref.dtype)

    row = pl.BlockSpec((tm, D), lambda i: (i, 0))
    vec = _vec_spec(D)
    return pl.pallas_call(
        body, name=name, grid=(T // tm,),
        in_specs=[pl.BlockSpec((tm, K), lambda i: (i, 0)), _resident((K, D)), row, vec, vec, vec, vec],
        out_specs=[row, row, row],
        out_shape=[jax.ShapeDtypeStruct((T, D), F32), jax.ShapeDtypeStruct((T, D), F32),
                   jax.ShapeDtypeStruct((T, D), MXU_DTYPE)],
        compiler_params=_params("parallel"),
    )(a, b, x, gate, nw, scale, shift)


def _norm_bwd(dh, x, nw, scale, dres, name, produced_by=None):
    T, D = x.shape
    tm = _tile(T, 512)

    def body(*refs):
        if isinstance(dh, tuple):
            dh_value = _dot(refs[0][...], refs[1][...], NN)
            refs = refs[1:]
        else:
            dh_value = refs[0][...]
        _, x_ref, nw_ref, sc_ref, dr_ref = refs[:5]
        n_in = 5 if produced_by is None else 7
        dx_ref, dnw_ref, dsc_ref, dsh_ref = refs[n_in:n_in + 4]

        @pl.when(pl.program_id(0) == 0)
        def _():
            dnw_ref[...] = jnp.zeros_like(dnw_ref)
            dsc_ref[...] = jnp.zeros_like(dsc_ref)
            dsh_ref[...] = jnp.zeros_like(dsh_ref)
            if produced_by is not None:
                refs[n_in + 5][...] = jnp.zeros_like(refs[n_in + 5])

        xf, dh_ = x_ref[...], dh_value
        r = lax.rsqrt(jnp.mean(xf * xf, axis=-1, keepdims=True) + EPS)
        xn = xf * r
        one_sc = 1.0 + sc_ref[...]
        dsh_ref[...] += jnp.sum(dh_, axis=0, keepdims=True)
        t = dh_ * xn
        dsc_ref[...] += jnp.sum(t, axis=0, keepdims=True) * nw_ref[...]
        dnw_ref[...] += jnp.sum(t, axis=0, keepdims=True) * one_sc
        dxn = dh_ * (nw_ref[...] * one_sc)
        dx = dr_ref[...] + r * (dxn - xn * jnp.mean(dxn * xn, axis=-1, keepdims=True))
        dx_ref[...] = dx
        if produced_by is not None:
            y_ref, g_ref, dy_ref, dg_ref = refs[5], refs[6], refs[n_in + 4], refs[n_in + 5]
            dy_ref[...] = ((produced_by[2] * g_ref[...]) * dx).astype(dy_ref.dtype)
            dg_ref[...] += produced_by[2] * jnp.sum(dx * y_ref[...], axis=0, keepdims=True)

    row = pl.BlockSpec((tm, D), lambda i: (i, 0))
    vec = _vec_spec(D)
    vec_out = jax.ShapeDtypeStruct((1, D), F32)
    if isinstance(dh, tuple):
        k_dim = dh[0].shape[1]
        args, in_specs = [dh[0], dh[1]], [pl.BlockSpec((tm, k_dim), lambda i: (i, 0)), _resident((k_dim, D))]
    else:
        args, in_specs = [dh], [row]
    args, in_specs = args + [x, nw, scale, dres], in_specs + [row, vec, vec, row]
    out_specs, out_shape = [row, vec, vec, vec], [jax.ShapeDtypeStruct((T, D), F32), vec_out, vec_out, vec_out]
    if produced_by is not None:
        args += [produced_by[0], produced_by[1]]
        in_specs += [row, vec]
        out_specs += [row, vec]
        out_shape += [jax.ShapeDtypeStruct((T, D), MXU_DTYPE), vec_out]
    return pl.pallas_call(
        body, name=name, grid=(T // tm,),
        in_specs=in_specs, out_specs=out_specs, out_shape=out_shape,
        compiler_params=_params("arbitrary"),
    )(*args)


def _rms(xf):
    r = lax.rsqrt(jnp.mean(xf * xf, axis=-1, keepdims=True) + EPS)
    return r, xf * r


def _norm_bwd_math(dh, xf, nw, sc):
    r, xn = _rms(xf)
    t = dh * xn
    dxn = dh * (nw * (1.0 + sc))
    dx = r * (dxn - xn * jnp.mean(dxn * xn, axis=-1, keepdims=True))
    return dx, jnp.sum(dh, axis=0, keepdims=True), jnp.sum(t, axis=0, keepdims=True)


def _swiglu_gate(x, wg_t, nw, scale, shift, name):
    T, D = x.shape
    Fdim = wg_t.shape[0]
    tm = _tile(T, 512)

    def body(x_ref, wg_ref, nw_ref, sc_ref, sh_ref, g_ref, h_ref):
        _, xn = _rms(x_ref[...])
        hh = (xn * nw_ref[...] * (1.0 + sc_ref[...]) + sh_ref[...]).astype(MXU_DTYPE)
        h_ref[...] = hh
        g_ref[...] = _dot(hh, wg_ref[...], NT).astype(g_ref.dtype)

    row = pl.BlockSpec((tm, D), lambda i: (i, 0))
    vec = _vec_spec(D)
    return pl.pallas_call(
        body, name=name, grid=(T // tm,),
        in_specs=[row, _resident((Fdim, D)), vec, vec, vec],
        out_specs=[pl.BlockSpec((tm, Fdim), lambda i: (i, 0)), row],
        out_shape=[jax.ShapeDtypeStruct((T, Fdim), MXU_DTYPE), jax.ShapeDtypeStruct((T, D), MXU_DTYPE)],
        compiler_params=_params("parallel"),
    )(x, wg_t, nw, scale, shift)


def _swiglu_up(h, wu_t, g, name):
    T, D = h.shape
    Fdim = wu_t.shape[0]
    tm = _tile(T, 512)

    def body(h_ref, wu_ref, g_ref, u_ref, a_ref):
        u = _dot(h_ref[...], wu_ref[...], NT)
        g = g_ref[...].astype(F32)
        u_ref[...] = u.astype(u_ref.dtype)
        a_ref[...] = ((g * _sigmoid(g)) * u).astype(a_ref.dtype)

    frow = pl.BlockSpec((tm, Fdim), lambda i: (i, 0))
    return pl.pallas_call(
        body, name=name, grid=(T // tm,),
        in_specs=[pl.BlockSpec((tm, D), lambda i: (i, 0)), _resident((Fdim, D)), frow],
        out_specs=[frow, frow],
        out_shape=[jax.ShapeDtypeStruct((T, Fdim), MXU_DTYPE)] * 2,
        compiler_params=_params("parallel"),
    )(h, wu_t, g)


def _swiglu_fwd_loss(h, wg_t, wu_t, wd, x, gate, fw, target, name):
    T, D = h.shape
    Fdim = wd.shape[0]
    tm, tf = _tile(T, FFN_TOKEN_TILE), _tile(Fdim, FFN_HIDDEN_TILE)
    row = pl.BlockSpec((tm, D), lambda i: (i, 0))
    frow = pl.BlockSpec((tm, Fdim), lambda i: (i, 0))
    vec, wres = _vec_spec(D), _resident((Fdim, D))
    vec_out = jax.ShapeDtypeStruct((1, D), F32)

    def body(h_ref, wg_ref, wu_ref, wd_ref, x_ref, gt_ref, fw_ref, t_ref,
             g_ref, u_ref, a_ref, loss_ref, dx_ref, dfw_ref, dy_ref, dg_ref):
        hh = h_ref[...]
        y = None
        for k in range(Fdim // tf):
            ks = slice(k * tf, (k + 1) * tf)
            g = _dot(hh, wg_ref[ks, :], NT)
            u = _dot(hh, wu_ref[ks, :], NT)
            a = ((g * _sigmoid(g)) * u).astype(a_ref.dtype)
            g_ref[:, ks] = g.astype(g_ref.dtype)
            u_ref[:, ks] = u.astype(u_ref.dtype)
            a_ref[:, ks] = a
            part = _dot(a, wd_ref[ks, :], NN)
            y = part if y is None else y + part

        @pl.when(pl.program_id(0) == 0)
        def _():
            loss_ref[...] = jnp.zeros_like(loss_ref)
            dfw_ref[...] = jnp.zeros_like(dfw_ref)
            dg_ref[...] = jnp.zeros_like(dg_ref)

        r, xn = _rms(x_ref[...] + (0.5 * gt_ref[...]) * y)
        err = xn * fw_ref[...] - t_ref[...]
        per_tok = jnp.mean(err * err, axis=-1, keepdims=True)
        loss_ref[...] += 0.5 * jnp.sum(per_tok, axis=0, keepdims=True)
        d_out = err * (1.0 / D)
        dfw_ref[...] += jnp.sum(d_out * xn, axis=0, keepdims=True)
        dxn = d_out * fw_ref[...]
        dx = r * (dxn - xn * jnp.mean(dxn * xn, axis=-1, keepdims=True))
        dx_ref[...] = dx
        dy_ref[...] = ((0.5 * gt_ref[...]) * dx).astype(dy_ref.dtype)
        dg_ref[...] += 0.5 * jnp.sum(dx * y, axis=0, keepdims=True)

    return pl.pallas_call(
        body, name=name, grid=(T // tm,),
        in_specs=[row, wres, wres, wres, row, vec, vec, row],
        out_specs=[frow, frow, frow, pl.BlockSpec((1, 128), lambda i: (0, 0)), row, vec, row, vec],
        out_shape=[jax.ShapeDtypeStruct((T, Fdim), MXU_DTYPE)] * 3
        + [jax.ShapeDtypeStruct((1, 128), F32), jax.ShapeDtypeStruct((T, D), F32), vec_out,
           jax.ShapeDtypeStruct((T, D), MXU_DTYPE), vec_out],
        compiler_params=_params("arbitrary"),
    )(h, wg_t, wu_t, wd, x, gate, fw, target)


def _swiglu_bwd(dy, wd, g, u, wg_t, wu_t, name, norm_in, dres, produced_by=None):
    T, D = dy.shape
    Fdim = wd.shape[0]
    tm, tf = _tile(T, FFN_TOKEN_TILE), _tile(Fdim, FFN_HIDDEN_TILE)
    row = pl.BlockSpec((tm, D), lambda i: (i, 0))
    frow = pl.BlockSpec((tm, Fdim), lambda i: (i, 0))
    vec, wres = _vec_spec(D), _resident((Fdim, D))
    vec_out = jax.ShapeDtypeStruct((1, D), F32)
    args, in_specs = [dy, wd, g, u, wg_t, wu_t, *norm_in, dres], [row, wres, frow, frow, wres, wres, row, vec, vec, row]
    out_shape = [jax.ShapeDtypeStruct((T, Fdim), MXU_DTYPE)] * 2 + [jax.ShapeDtypeStruct((T, D), F32)] + [vec_out] * 3
    out_specs = [frow, frow, row, vec, vec, vec]
    if produced_by is not None:
        args += [produced_by[0], produced_by[1]]
        in_specs += [row, vec]
        out_shape += [jax.ShapeDtypeStruct((T, D), MXU_DTYPE), vec_out]
        out_specs += [row, vec]

    def body(*refs):
        it = iter(refs)
        dy_ref, wd_ref, g_ref, u_ref, wg_ref, wu_ref, x_ref, nw_ref, sc_ref, dr_ref = [next(it) for _ in range(10)]
        prev_refs = [next(it), next(it)] if produced_by is not None else None
        dg_ref, du_ref, dx_ref, dnw_ref, dsc_ref, dsh_ref = [next(it) for _ in range(6)]
        prev_out = [next(it), next(it)] if produced_by is not None else None

        @pl.when(pl.program_id(0) == 0)
        def _():
            dnw_ref[...] = jnp.zeros_like(dnw_ref)
            dsc_ref[...] = jnp.zeros_like(dsc_ref)
            dsh_ref[...] = jnp.zeros_like(dsh_ref)
            if produced_by is not None:
                prev_out[1][...] = jnp.zeros_like(prev_out[1])

        dyy = dy_ref[...]
        dh = None
        for k in range(Fdim // tf):
            ks = slice(k * tf, (k + 1) * tf)
            da = _dot(dyy, wd_ref[ks, :], NT)
            gg = g_ref[:, ks].astype(F32)
            sig = _sigmoid(gg)
            dg = (da * u_ref[:, ks].astype(F32) * (sig * (1.0 + gg * (1.0 - sig)))).astype(dg_ref.dtype)
            du = (da * (gg * sig)).astype(du_ref.dtype)
            dg_ref[:, ks] = dg
            du_ref[:, ks] = du
            part = _dot(dg, wg_ref[ks, :], NN) + _dot(du, wu_ref[ks, :], NN)
            dh = part if dh is None else dh + part
        dx_norm, dsh_row, t_row = _norm_bwd_math(dh, x_ref[...], nw_ref[...], sc_ref[...])
        dsh_ref[...] += dsh_row
        dsc_ref[...] += t_row * nw_ref[...]
        dnw_ref[...] += t_row * (1.0 + sc_ref[...])
        dx = dr_ref[...] + dx_norm
        dx_ref[...] = dx
        if produced_by is not None:
            prev_out[0][...] = ((produced_by[2] * prev_refs[1][...]) * dx).astype(prev_out[0].dtype)
            prev_out[1][...] += produced_by[2] * jnp.sum(dx * prev_refs[0][...], axis=0, keepdims=True)

    return pl.pallas_call(
        body, name=name, grid=(T // tm,), in_specs=in_specs, out_specs=out_specs, out_shape=out_shape,
        compiler_params=_params("arbitrary"),
    )(*args)


def _resident(shape):
    return pl.BlockSpec(shape, lambda i: (0,) * len(shape), pipeline_mode=pl.Buffered(1))


def _conv_act(window, w):
    y = window * w[CONV_K - 1:CONV_K, :]
    for j in range(CONV_K - 1):
        y += _shift_rows(window, CONV_K - 1 - j) * w[j:j + 1, :]
    return y


def _gdn_prep(proj, conv_w, a_log_l, dt_bias_l, name):
    T = proj.shape[0]
    tm = _tile(T, 256)
    hb = tm // 8

    def body(cur_ref, halo_ref, ba_ref, w_ref, al_ref, dtb_ref, qkv_ref, bg_ref):
        i = pl.program_id(0)
        halo = jnp.where(i == 0, 0.0, halo_ref[...])
        window = jnp.concatenate([halo, cur_ref[...]], axis=0)
        y = _conv_act(window, w_ref[...])[8:, :]
        act = y * _sigmoid(y)
        for hh in range(3 * GDN_HEADS):
            blk = act[:, hh * HEAD_DIM:(hh + 1) * HEAD_DIM]
            if hh < 2 * GDN_HEADS:
                rn = lax.rsqrt(jnp.sum(blk * blk, axis=-1, keepdims=True) + EPS)
                blk = blk * rn
                if hh < GDN_HEADS:
                    blk = blk * (HEAD_DIM ** -0.5)
            qkv_ref[:, hh * HEAD_DIM:(hh + 1) * HEAD_DIM] = blk
        ba = ba_ref[...]
        lane = lax.broadcasted_iota(jnp.int32, ba.shape, 1)
        beta = _sigmoid(ba)
        g = -jnp.exp(al_ref[...]) * _softplus(ba + dtb_ref[...])
        bg_ref[...] = jnp.where(lane < GDN_HEADS, beta, jnp.where(lane < 2 * GDN_HEADS, g, 0.0))

    return pl.pallas_call(
        body, name=name, grid=(T // tm,),
        in_specs=[pl.BlockSpec((tm, QKV_WIDTH), lambda i: (i, 0)),
                  pl.BlockSpec((8, QKV_WIDTH), lambda i: (jnp.maximum(i * hb - 1, 0), 0)),
                  pl.BlockSpec((tm, 128), lambda i: (i, COL_BA // 128)),
                  pl.BlockSpec((CONV_K, QKV_WIDTH), lambda i: (0, 0)),
                  pl.BlockSpec((1, 128), lambda i: (0, 0)), pl.BlockSpec((1, 128), lambda i: (0, 0))],
        out_specs=[pl.BlockSpec((tm, QKV_WIDTH), lambda i: (i, 0)), pl.BlockSpec((tm, 128), lambda i: (i, 0))],
        out_shape=[jax.ShapeDtypeStruct((T, QKV_WIDTH), F32), jax.ShapeDtypeStruct((T, 128), F32)],
        compiler_params=_params("parallel"),
    )(proj, proj, proj, conv_w, a_log_l, dt_bias_l)


def _chunk_cumsum(v, reverse=False):
    row = lax.broadcasted_iota(jnp.int32, v.shape, 0)
    s = 1
    while s < CHUNK:
        if reverse:
            v = v + jnp.where(row < CHUNK - s, _shift_rows(v, -s), 0.0)
        else:
            v = v + jnp.where(row >= s, _shift_rows(v, s), 0.0)
        s *= 2
    return v


def _row_form(cols):
    padded = jnp.concatenate([cols, jnp.zeros((128 - CHUNK, 128), F32)], axis=0)
    return padded.T[:, :CHUNK]


def _chunk_masks():
    ri = lax.broadcasted_iota(jnp.int32, (CHUNK, CHUNK), 0)
    ci = lax.broadcasted_iota(jnp.int32, (CHUNK, CHUNK), 1)
    return ri >= ci, ri > ci, (ri == ci).astype(F32)


def _unit_lower_inverses(ms, eye):
    rs = [eye - m for m in ms]
    ps = [_split(-m) for m in ms]
    s = 2
    while s < CHUNK:
        ps = [_split(_dot3(p, p, NN)) for p in ps]
        r_parts = [_split(r) for r in rs]
        rs = [r + _dot3(p, rp, NN) for r, p, rp in zip(rs, ps, r_parts)]
        s *= 2
    return rs


def _head_elementwise(k, beta, gc, gcr, causal):
    decay = jnp.where(causal, jnp.exp(jnp.where(causal, gc - gcr, 0.0)), 0.0)
    return decay, k * beta, jnp.exp(gc)


def _head_slices(hh):
    return (slice(hh * HEAD_DIM, (hh + 1) * HEAD_DIM),
            slice(GDN_WIDTH + hh * HEAD_DIM, GDN_WIDTH + (hh + 1) * HEAD_DIM),
            slice(2 * GDN_WIDTH + hh * HEAD_DIM, 2 * GDN_WIDTH + (hh + 1) * HEAD_DIM))


def _gdn_chunk_fwd(qkv, bg, name):
    T = qkv.shape[0]
    cb = _tile(T // CHUNK, CHUNKS_PER_STEP)
    rows = cb * CHUNK

    def body(qkv_ref, bg_ref, tinv_ref, u_ref, w_ref, qd_ref, kd_ref, p_ref, cd_ref):
        masks = _chunk_masks()
        causal, strict, eye = masks
        heads = []
        for ci in range(cb):
            rs = slice(ci * CHUNK, (ci + 1) * CHUNK)
            bgv = bg_ref[rs, :]
            gc_all = _chunk_cumsum(bgv)
            gc_rows = _row_form(gc_all)
            cd_ref[rs, :] = jnp.exp(jnp.broadcast_to(gc_all[CHUNK - 1:CHUNK, :], (CHUNK, 128)))
            for hh in range(GDN_HEADS):
                qs, ks, vs = _head_slices(hh)
                q, k, v = qkv_ref[rs, qs], qkv_ref[rs, ks], qkv_ref[rs, vs]
                beta = bgv[:, hh:hh + 1]
                gc = gc_all[:, GDN_HEADS + hh:GDN_HEADS + hh + 1]
                decay, kb, eg = _head_elementwise(k, beta, gc, gc_rows[GDN_HEADS + hh:GDN_HEADS + hh + 1, :], causal)
                hs = slice(hh * HEAD_DIM, (hh + 1) * HEAD_DIM)
                cs = slice(hh * CHUNK, (hh + 1) * CHUNK)
                qd_ref[rs, hs] = (q * eg).astype(qd_ref.dtype)
                kd_ref[rs, hs] = (k * jnp.exp(gc[CHUNK - 1:CHUNK, :] - gc)).astype(kd_ref.dtype)
                heads.append((rs, hs, cs, q, k, v * beta, kb, kb * eg, decay))
        kks = [_mdot(kb, k, NT) for (_, _, _, _, k, _, kb, _, _) in heads]
        qks = [_mdot(q, k, NT) for (_, _, _, q, k, _, _, _, _) in heads]
        tinvs = _unit_lower_inverses([jnp.where(strict, kk * hd[8], 0.0) for kk, hd in zip(kks, heads)], eye)
        t_parts = [_split(t) for t in tinvs]
        us = [_dot3(tp, _split(hd[5]), NN) for tp, hd in zip(t_parts, heads)]
        ws = [_dot3(tp, _split(hd[7]), NN) for tp, hd in zip(t_parts, heads)]
        for hd, tinv, u, w, qk in zip(heads, tinvs, us, ws, qks):
            rs, hs, cs = hd[0], hd[1], hd[2]
            tinv_ref[rs, cs] = tinv
            u_ref[rs, hs] = u
            w_ref[rs, hs] = w.astype(w_ref.dtype)
            p_ref[rs, cs] = jnp.where(causal, qk * hd[8], 0.0).astype(p_ref.dtype)

    def spec(width):
        return pl.BlockSpec((rows, width), lambda n: (n, 0))

    hw, cw = GDN_WIDTH, GDN_HEADS * CHUNK
    return pl.pallas_call(
        body, name=name, grid=(T // rows,),
        in_specs=[spec(QKV_WIDTH), spec(128)],
        out_specs=[spec(cw), spec(hw), spec(hw), spec(hw), spec(hw), spec(cw), spec(128)],
        out_shape=[jax.ShapeDtypeStruct((T, cw), F32), jax.ShapeDtypeStruct((T, hw), F32),
                   jax.ShapeDtypeStruct((T, hw), MXU_DTYPE), jax.ShapeDtypeStruct((T, hw), MXU_DTYPE),
                   jax.ShapeDtypeStruct((T, hw), MXU_DTYPE), jax.ShapeDtypeStruct((T, cw), MXU_DTYPE),
                   jax.ShapeDtypeStruct((T, 128), F32)],
        compiler_params=_params("parallel"),
    )(qkv, bg)


def _gdn_scan_fwd(u, w, qd, kd, p, cd, name):
    T = u.shape[0]
    cb = _tile(T // CHUNK, SCAN_CHUNKS_PER_STEP)
    rows = cb * CHUNK

    def body(u_ref, w_ref, qd_ref, kd_ref, p_ref, cd_ref, o_ref, s_all_ref, vn_ref, s_ref):
        @pl.when(pl.program_id(0) == 0)
        def _():
            s_ref[...] = jnp.zeros_like(s_ref)

        hss = [slice(hh * HEAD_DIM, (hh + 1) * HEAD_DIM) for hh in range(GDN_HEADS)]
        css = [slice(hh * CHUNK, (hh + 1) * CHUNK) for hh in range(GDN_HEADS)]
        s_cur = [s_ref[hh] for hh in range(GDN_HEADS)]
        for ci in range(cb):
            rs = slice(ci * CHUNK, (ci + 1) * CHUNK)
            for hh in range(GDN_HEADS):
                s_all_ref[ci * GDN_WIDTH + hh * HEAD_DIM:ci * GDN_WIDTH + (hh + 1) * HEAD_DIM, :] = s_cur[hh]
            s_ms = [s.astype(MXU_DTYPE) for s in s_cur]
            w_s = [_dot(w_ref[rs, hs], s_m, NN) for hs, s_m in zip(hss, s_ms)]
            q_s = [_dot(qd_ref[rs, hs], s_m, NN) for hs, s_m in zip(hss, s_ms)]
            v_ms = [(u_ref[rs, hs] - ws_).astype(MXU_DTYPE) for hs, ws_ in zip(hss, w_s)]
            k_v = [_dot(kd_ref[rs, hs], v_m, TN) for hs, v_m in zip(hss, v_ms)]
            p_v = [_dot(p_ref[rs, cs], v_m, NN) for cs, v_m in zip(css, v_ms)]
            for hh in range(GDN_HEADS):
                vn_ref[rs, hss[hh]] = v_ms[hh]
                o_ref[rs, hss[hh]] = q_s[hh] + p_v[hh]
                c_dec = cd_ref[ci * CHUNK:ci * CHUNK + 1, GDN_HEADS + hh:GDN_HEADS + hh + 1]
                s_cur[hh] = s_cur[hh] * c_dec + k_v[hh]
        for hh in range(GDN_HEADS):
            s_ref[hh] = s_cur[hh]

    def spec(width):
        return pl.BlockSpec((rows, width), lambda n: (n, 0))

    hw, cw = GDN_WIDTH, GDN_HEADS * CHUNK
    return pl.pallas_call(
        body, name=name, grid=(T // rows,),
        in_specs=[spec(hw), spec(hw), spec(hw), spec(hw), spec(cw), spec(128)],
        out_specs=[spec(hw), pl.BlockSpec((cb * GDN_WIDTH, HEAD_DIM), lambda n: (n, 0)), spec(hw)],
        out_shape=[jax.ShapeDtypeStruct((T, hw), F32),
                   jax.ShapeDtypeStruct((T // CHUNK * GDN_WIDTH, HEAD_DIM), F32),
                   jax.ShapeDtypeStruct((T, hw), MXU_DTYPE)],
        scratch_shapes=[pltpu.VMEM((GDN_HEADS, HEAD_DIM, HEAD_DIM), F32)],
        compiler_params=_params("arbitrary"),
    )(u, w, qd, kd, p, cd)


def _gdn_scan_bwd(do, w, qd, kd, p, cd, s_all, vn, name):
    T = do.shape[0]
    cb = _tile(T // CHUNK, SCAN_CHUNKS_PER_STEP)
    rows = cb * CHUNK
    n_steps = T // rows

    def body(do_ref, w_ref, qd_ref, kd_ref, p_ref, cd_ref, s_all_ref, vn_ref,
             dvn_ref, dw_ref, dqd_ref, dkd_ref, dp_ref, dcd_ref, ds_ref):
        @pl.when(pl.program_id(0) == 0)
        def _():
            ds_ref[...] = jnp.zeros_like(ds_ref)

        causal, _, _ = _chunk_masks()
        lane = lax.broadcasted_iota(jnp.int32, (CHUNK, 128), 1)
        heads = range(GDN_HEADS)
        hss = [slice(hh * HEAD_DIM, (hh + 1) * HEAD_DIM) for hh in heads]
        css = [slice(hh * CHUNK, (hh + 1) * CHUNK) for hh in heads]
        ds_cur = [ds_ref[hh] for hh in heads]
        for ci in reversed(range(cb)):
            rs = slice(ci * CHUNK, (ci + 1) * CHUNK)
            ds_ms = [d.astype(MXU_DTYPE) for d in ds_cur]
            s_olds = [s_all_ref[ci * GDN_WIDTH + hh * HEAD_DIM:ci * GDN_WIDTH + (hh + 1) * HEAD_DIM, :] for hh in heads]
            s_ms = [s.astype(MXU_DTYPE) for s in s_olds]
            do_ms = [do_ref[rs, hs].astype(MXU_DTYPE) for hs in hss]
            p_do = [_dot(p_ref[rs, cs], do_m, TN) for cs, do_m in zip(css, do_ms)]
            k_ds = [_dot(kd_ref[rs, hs], ds_m, NN) for hs, ds_m in zip(hss, ds_ms)]
            q_do = [_dot(qd_ref[rs, hs], do_m, TN) for hs, do_m in zip(hss, do_ms)]
            dqds = [_dot(do_m, s_m, NT) for do_m, s_m in zip(do_ms, s_ms)]
            dkds = [_dot(vn_ref[rs, hs], ds_m, NT) for hs, ds_m in zip(hss, ds_ms)]
            dps = [_dot(do_m, vn_ref[rs, hs], NT) for hs, do_m in zip(hss, do_ms)]
            dv_news = [a + b for a, b in zip(p_do, k_ds)]
            dvn_ms = [d.astype(MXU_DTYPE) for d in dv_news]
            w_dv = [_dot(w_ref[rs, hs], dvn_m, TN) for hs, dvn_m in zip(hss, dvn_ms)]
            dws = [_dot(dvn_m, s_m, NT) for dvn_m, s_m in zip(dvn_ms, s_ms)]
            dcd_tile = jnp.zeros((CHUNK, 128), F32)
            for hh in heads:
                dvn_ref[rs, hss[hh]] = dv_news[hh]
                dw_ref[rs, hss[hh]] = -dws[hh]
                dqd_ref[rs, hss[hh]] = dqds[hh]
                dkd_ref[rs, hss[hh]] = dkds[hh]
                dp_ref[rs, css[hh]] = jnp.where(causal, dps[hh], 0.0)
                dcd = jnp.sum(jnp.sum(s_olds[hh] * ds_cur[hh], axis=1, keepdims=True), axis=0, keepdims=True)
                dcd_tile = jnp.where(lane == GDN_HEADS + hh, dcd, dcd_tile)
                c_dec = cd_ref[ci * CHUNK:ci * CHUNK + 1, GDN_HEADS + hh:GDN_HEADS + hh + 1]
                ds_cur[hh] = c_dec * ds_cur[hh] + q_do[hh] - w_dv[hh]
            dcd_ref[rs, :] = dcd_tile
        for hh in heads:
            ds_ref[hh] = ds_cur[hh]

    def spec(width):
        return pl.BlockSpec((rows, width), lambda n: (n_steps - 1 - n, 0))

    hw, cw = GDN_WIDTH, GDN_HEADS * CHUNK
    return pl.pallas_call(
        body, name=name, grid=(n_steps,),
        in_specs=[spec(hw), spec(hw), spec(hw), spec(hw), spec(cw), spec(128),
                  pl.BlockSpec((cb * GDN_WIDTH, HEAD_DIM), lambda n: (n_steps - 1 - n, 0)), spec(hw)],
        out_specs=[spec(hw), spec(hw), spec(hw), spec(hw), spec(cw), spec(128)],
        out_shape=[jax.ShapeDtypeStruct((T, hw), F32)] * 4
        + [jax.ShapeDtypeStruct((T, cw), F32), jax.ShapeDtypeStruct((T, 128), F32)],
        scratch_shapes=[pltpu.VMEM((GDN_HEADS, HEAD_DIM, HEAD_DIM), F32)],
        compiler_params=_params("arbitrary"),
    )(do, w, qd, kd, p, cd, s_all, vn)


def _gdn_chunk_bwd(qkv, bg, tinv_all, u, w, dvn, dw, dqd, dkd, dp, dcd, name):
    T = qkv.shape[0]
    cb = _tile(T // CHUNK, CHUNKS_PER_STEP)
    rows = cb * CHUNK

    def body(qkv_ref, bg_ref, tinv_ref, u_ref, w_ref, dvn_ref, dw_ref, dqd_ref, dkd_ref, dp_ref, dcd_ref,
             dqkv_ref, dbg_ref):
        masks = _chunk_masks()
        causal, strict, _ = masks
        lane = lax.broadcasted_iota(jnp.int32, (CHUNK, 128), 1)
        row = lax.broadcasted_iota(jnp.int32, (CHUNK, 128), 0)
        heads = []
        for ci in range(cb):
            rs = slice(ci * CHUNK, (ci + 1) * CHUNK)
            bgv = bg_ref[rs, :]
            gc_all = _chunk_cumsum(bgv)
            gc_rows = _row_form(gc_all)
            for hh in range(GDN_HEADS):
                qs, ks, vs = _head_slices(hh)
                q, k = qkv_ref[rs, qs], qkv_ref[rs, ks]
                beta = bgv[:, hh:hh + 1]
                gc = gc_all[:, GDN_HEADS + hh:GDN_HEADS + hh + 1]
                decay, kb, eg = _head_elementwise(k, beta, gc, gc_rows[GDN_HEADS + hh:GDN_HEADS + hh + 1, :], causal)
                heads.append(dict(ci=ci, hh=hh, rs=rs, hs=slice(hh * HEAD_DIM, (hh + 1) * HEAD_DIM),
                                  cs=slice(hh * CHUNK, (hh + 1) * CHUNK), q=q, k=k, beta=beta, gc=gc,
                                  decay=decay, kb=kb, eg=eg))
        for hd in heads:
            hd["t"] = _split(tinv_ref[hd["rs"], hd["cs"]])
        for hd in heads:
            hd["kk"] = _mdot(hd["kb"], hd["k"], NT)
            hd["qk"] = _mdot(hd["q"], hd["k"], NT)
        for hd in heads:
            hd["dvb"] = _dot3(hd["t"], _split(dvn_ref[hd["rs"], hd["hs"]]), TN)
            hd["dkbeg"] = _dot3(hd["t"], _split(dw_ref[hd["rs"], hd["hs"]]), TN)
        for hd in heads:
            rs, hs = hd["rs"], hd["hs"]
            da = -(_mdot(hd["dvb"], u_ref[rs, hs], NT) + _mdot(hd["dkbeg"], w_ref[rs, hs], NT))
            dm = jnp.where(strict, da, 0.0)
            dp_ = dp_ref[rs, hd["cs"]]
            hd["dkk"] = dm * hd["decay"]
            hd["dqk"] = dp_ * hd["decay"]
            hd["e"] = (hd["dkk"] * hd["kk"] + hd["dqk"] * hd["qk"])
        for hd in heads:
            hd["dkb"] = _mdot(hd["dkk"], hd["k"], NN)
            hd["dk"] = _mdot(hd["dkk"], hd["kb"], TN) + _mdot(hd["dqk"], hd["q"], TN)
            hd["dq"] = _mdot(hd["dqk"], hd["k"], NN)
            onehot = (lane == GDN_HEADS + hd["hh"]).astype(jnp.bfloat16)
            e_hi, e_lo = _split(hd["e"])
            hd["col_sums"] = _dot(e_lo, onehot, TN) + _dot(e_hi, onehot, TN)
        tiles = {}
        for hd in heads:
            ci, hh, rs, hs = hd["ci"], hd["hh"], hd["rs"], hd["hs"]
            qs, ks, vs = _head_slices(hh)
            q, k, beta, gc, eg, kb = hd["q"], hd["k"], hd["beta"], hd["gc"], hd["eg"], hd["kb"]
            v = qkv_ref[rs, vs]
            dqd_, dkd_ = dqd_ref[rs, hs], dkd_ref[rs, hs]
            gl = gc[CHUNK - 1:CHUNK, :]
            ek = jnp.exp(gl - gc)
            dkb = hd["dkb"] + hd["dkbeg"] * eg
            deg = jnp.sum(dqd_ * q, axis=1, keepdims=True) + jnp.sum(hd["dkbeg"] * kb, axis=1, keepdims=True)
            dek = jnp.sum(dkd_ * k, axis=1, keepdims=True)
            dcd_ = dcd_ref[ci * CHUNK:ci * CHUNK + 1, GDN_HEADS + hh:GDN_HEADS + hh + 1]
            dgl = jnp.sum(dek * ek, axis=0, keepdims=True) + dcd_ * jnp.exp(gl)
            dgc = jnp.sum(hd["e"], axis=1, keepdims=True) + deg * eg - dek * ek
            dbeta_tile, dgc_tile = tiles.get(ci, (jnp.zeros((CHUNK, 128), F32), jnp.zeros((CHUNK, 128), F32)))
            dgc_tile += jnp.where(lane == GDN_HEADS + hh, dgc, 0.0) - hd["col_sums"]
            dgc_tile += jnp.where((lane == GDN_HEADS + hh) & (row == CHUNK - 1), dgl, 0.0)
            dbeta = jnp.sum(dkb * k, axis=1, keepdims=True) + jnp.sum(hd["dvb"] * v, axis=1, keepdims=True)
            dbeta_tile += jnp.where(lane == hh, dbeta, 0.0)
            tiles[ci] = (dbeta_tile, dgc_tile)
            dqkv_ref[rs, qs] = hd["dq"] + dqd_ * eg
            dqkv_ref[rs, ks] = hd["dk"] + dkd_ * ek + dkb * beta
            dqkv_ref[rs, vs] = hd["dvb"] * beta
        for ci in range(cb):
            dbeta_tile, dgc_tile = tiles[ci]
            dbg_ref[ci * CHUNK:(ci + 1) * CHUNK, :] = dbeta_tile + _chunk_cumsum(dgc_tile, reverse=True)

    def spec(width):
        return pl.BlockSpec((rows, width), lambda n: (n, 0))

    hw, cw = GDN_WIDTH, GDN_HEADS * CHUNK
    return pl.pallas_call(
        body, name=name, grid=(T // rows,),
        in_specs=[spec(QKV_WIDTH), spec(128), spec(cw), spec(hw), spec(hw), spec(hw), spec(hw), spec(hw),
                  spec(hw), spec(cw), spec(128)],
        out_specs=[spec(QKV_WIDTH), spec(128)],
        out_shape=[jax.ShapeDtypeStruct((T, QKV_WIDTH), F32), jax.ShapeDtypeStruct((T, 128), F32)],
        compiler_params=_params("parallel"),
    )(qkv, bg, tinv_all, u, w, dvn, dw, dqd, dkd, dp, dcd)


def _pool_counts(i, tm, rows, offset):
    t = i * tm - offset + lax.broadcasted_iota(jnp.int32, (rows, 1), 0)
    return [jnp.minimum(t + 1, w).astype(F32) for w in POOL_WINDOWS]


def _window_sums(window, forward):
    sums, s, step = [], window, 1
    for _ in POOL_WINDOWS:
        s = s + _shift_rows(s, -step if forward else step)
        sums.append(s)
        step *= 2
    return sums


def _pooled(window, counts):
    sums = _window_sums(window, forward=False)
    out = []
    for gi in range(POOL_GROUPS):
        sl = slice(gi * 128, (gi + 1) * 128)
        out.append(sums[gi][HALO:, sl] / counts[gi] - window[HALO:, sl])
    return out


def _mix_post(o, proj, gdn_norm, pool_w, pool_scale, name):
    T = o.shape[0]
    tm = _tile(T, 256)
    hb = tm // HALO

    def body(o_ref, z_ref, p_ref, ph_ref, gn_ref, pw_ref, ps_ref, out_ref):
        i = pl.program_id(0)
        for hh in range(GDN_HEADS):
            sl = slice(hh * HEAD_DIM, (hh + 1) * HEAD_DIM)
            oh, zh = o_ref[:, sl], z_ref[:, sl]
            ro = lax.rsqrt(jnp.mean(oh * oh, axis=-1, keepdims=True) + EPS)
            out_ref[:, sl] = (((oh * ro) * gn_ref[...]) * (zh * _sigmoid(zh))).astype(out_ref.dtype)
        halo = jnp.where(i == 0, 0.0, ph_ref[...])
        window = jnp.concatenate([halo, p_ref[...]], axis=0)
        pooled = _pooled(window, _pool_counts(i, tm, tm, 0))
        for gi in range(POOL_GROUPS):
            pm = _mdot(pooled[gi], pw_ref[gi], NN)
            out_ref[:, GDN_WIDTH + gi * 128:GDN_WIDTH + (gi + 1) * 128] = (
                pm * ps_ref[:, gi * 128:(gi + 1) * 128]).astype(out_ref.dtype)

    return pl.pallas_call(
        body, name=name, grid=(T // tm,),
        in_specs=[pl.BlockSpec((tm, GDN_WIDTH), lambda i: (i, 0)),
                  pl.BlockSpec((tm, GDN_WIDTH), lambda i: (i, COL_Z // GDN_WIDTH)),
                  pl.BlockSpec((tm, POOL_WIDTH), lambda i: (i, COL_P // POOL_WIDTH)),
                  pl.BlockSpec((HALO, POOL_WIDTH), lambda i: (jnp.maximum(i * hb - 1, 0), COL_P // POOL_WIDTH)),
                  pl.BlockSpec((1, HEAD_DIM), lambda i: (0, 0)),
                  pl.BlockSpec((POOL_GROUPS, 128, 128), lambda i: (0, 0, 0)),
                  pl.BlockSpec((1, POOL_WIDTH), lambda i: (0, 0))],
        out_specs=pl.BlockSpec((tm, GDN_WIDTH + POOL_WIDTH), lambda i: (i, 0)),
        out_shape=jax.ShapeDtypeStruct((T, GDN_WIDTH + POOL_WIDTH), MXU_DTYPE),
        compiler_params=_params("parallel"),
    )(o, proj, proj, proj, gdn_norm, pool_w, pool_scale)


def _mix_post_bwd(dmix, o, proj, gdn_norm, pool_w, pool_scale, name):
    T = o.shape[0]
    tm = _tile(T, 256)
    hb = tm // HALO
    n_tiles = T // tm

    def body(dg_ref, dpo_ref, dpo_next_ref, o_ref, z_ref, p_ref, ph_ref, gn_ref, pw_ref, ps_ref,
             do_ref, dzp_ref, dgn_ref, dpw_ref, dps_ref):
        i = pl.program_id(0)

        @pl.when(i == 0)
        def _():
            dgn_ref[...] = jnp.zeros_like(dgn_ref)
            dpw_ref[...] = jnp.zeros_like(dpw_ref)
            dps_ref[...] = jnp.zeros_like(dps_ref)

        gn = gn_ref[...]
        dgn = jnp.zeros((1, HEAD_DIM), F32)
        for hh in range(GDN_HEADS):
            sl = slice(hh * HEAD_DIM, (hh + 1) * HEAD_DIM)
            oh, zh, dy = o_ref[:, sl], z_ref[:, sl], dg_ref[:, sl]
            ro = lax.rsqrt(jnp.mean(oh * oh, axis=-1, keepdims=True) + EPS)
            on = oh * ro
            sig = _sigmoid(zh)
            sz = zh * sig
            dzp_ref[:, sl] = (dy * (on * gn) * (sig * (1.0 + zh * (1.0 - sig)))).astype(dzp_ref.dtype)
            dgn += jnp.sum(dy * on * sz, axis=0, keepdims=True)
            don = dy * gn * sz
            do_ref[:, sl] = ro * (don - on * jnp.mean(don * on, axis=-1, keepdims=True))
        dgn_ref[...] += dgn

        halo = jnp.where(i == 0, 0.0, ph_ref[...])
        window = jnp.concatenate([halo, p_ref[...]], axis=0)
        counts = _pool_counts(i, tm, tm + HALO, 0)
        pooled = _pooled(window, [cn[:tm] for cn in counts])
        nxt = jnp.where(i == n_tiles - 1, 0.0, dpo_next_ref[...])
        dpo_w = jnp.concatenate([dpo_ref[...], nxt], axis=0)
        ps = ps_ref[...]
        dps = []
        scaled = []
        for gi in range(POOL_GROUPS):
            sl = slice(gi * 128, (gi + 1) * 128)
            dpm = dpo_w[:, sl] * ps[:, sl]
            pm = _mdot(pooled[gi], pw_ref[gi], NN)
            dps.append(jnp.sum(dpo_w[:tm, sl] * pm, axis=0, keepdims=True))
            dpw_ref[gi] += _mdot(pooled[gi], dpm[:tm], TN)
            dpooled = _mdot(dpm, pw_ref[gi], NT)
            scaled.append((dpooled, dpooled / counts[gi]))
        dps_ref[...] += jnp.concatenate(dps, axis=1)
        lead = _window_sums(jnp.concatenate([sc for _, sc in scaled], axis=1), forward=True)
        for gi in range(POOL_GROUPS):
            sl = slice(gi * 128, (gi + 1) * 128)
            dzp_ref[:, GDN_WIDTH + gi * 128:GDN_WIDTH + (gi + 1) * 128] = (
                lead[gi][:tm, sl] - scaled[gi][0][:tm]).astype(dzp_ref.dtype)

    last_halo = T // HALO - 1
    return pl.pallas_call(
        body, name=name, grid=(n_tiles,),
        in_specs=[pl.BlockSpec((tm, GDN_WIDTH), lambda i: (i, 0)),
                  pl.BlockSpec((tm, POOL_WIDTH), lambda i: (i, 1)),
                  pl.BlockSpec((HALO, POOL_WIDTH), lambda i: (jnp.minimum((i + 1) * hb, last_halo), 1)),
                  pl.BlockSpec((tm, GDN_WIDTH), lambda i: (i, 0)),
                  pl.BlockSpec((tm, GDN_WIDTH), lambda i: (i, COL_Z // GDN_WIDTH)),
                  pl.BlockSpec((tm, POOL_WIDTH), lambda i: (i, COL_P // POOL_WIDTH)),
                  pl.BlockSpec((HALO, POOL_WIDTH), lambda i: (jnp.maximum(i * hb - 1, 0), COL_P // POOL_WIDTH)),
                  pl.BlockSpec((1, HEAD_DIM), lambda i: (0, 0)),
                  pl.BlockSpec((POOL_GROUPS, 128, 128), lambda i: (0, 0, 0)),
                  pl.BlockSpec((1, POOL_WIDTH), lambda i: (0, 0))],
        out_specs=[pl.BlockSpec((tm, GDN_WIDTH), lambda i: (i, 0)),
                   pl.BlockSpec((tm, GDN_WIDTH + POOL_WIDTH), lambda i: (i, 0)),
                   pl.BlockSpec((1, HEAD_DIM), lambda i: (0, 0)),
                   pl.BlockSpec((POOL_GROUPS, 128, 128), lambda i: (0, 0, 0)),
                   pl.BlockSpec((1, POOL_WIDTH), lambda i: (0, 0))],
        out_shape=[jax.ShapeDtypeStruct((T, GDN_WIDTH), F32),
                   jax.ShapeDtypeStruct((T, GDN_WIDTH + POOL_WIDTH), MXU_DTYPE),
                   jax.ShapeDtypeStruct((1, HEAD_DIM), F32),
                   jax.ShapeDtypeStruct((POOL_GROUPS, 128, 128), F32),
                   jax.ShapeDtypeStruct((1, POOL_WIDTH), F32)],
        compiler_params=_params("arbitrary"),
    )(dmix, dmix, dmix, o, proj, proj, proj, gdn_norm, pool_w, pool_scale)


def _gdn_prep_bwd(proj, conv_w, a_log_l, dt_bias_l, dqkv, dbg, dzp, name):
    T = proj.shape[0]
    tm = _tile(T, 256)
    hb = tm // 8
    n_tiles = T // tm
    last_halo = T // 8 - 1

    def body(cur_ref, before_ref, after_ref, ba_ref, w_ref, al_ref, dtb_ref, dq_ref, dq_after_ref, dbg_ref,
             dzp_ref, dproj_ref, dw_ref, dal_ref, ddtb_ref):
        i = pl.program_id(0)

        @pl.when(i == 0)
        def _():
            dw_ref[...] = jnp.zeros_like(dw_ref)
            dal_ref[...] = jnp.zeros_like(dal_ref)
            ddtb_ref[...] = jnp.zeros_like(ddtb_ref)

        last = i == n_tiles - 1
        w = w_ref[...]
        before = jnp.where(i == 0, 0.0, before_ref[...])
        after = jnp.where(last, 0.0, after_ref[...])
        window = jnp.concatenate([before, cur_ref[...], after], axis=0)
        y = _conv_act(window, w)
        sig = _sigmoid(y)
        act = y * sig
        dq_w = jnp.concatenate([jnp.zeros((8, QKV_WIDTH), F32), dq_ref[...],
                                jnp.where(last, 0.0, dq_after_ref[...])], axis=0)
        dact = []
        for hh in range(3 * GDN_HEADS):
            sl = slice(hh * HEAD_DIM, (hh + 1) * HEAD_DIM)
            blk, dblk = act[:, sl], dq_w[:, sl]
            if hh < 2 * GDN_HEADS:
                rn = lax.rsqrt(jnp.sum(blk * blk, axis=-1, keepdims=True) + EPS)
                unit = blk * rn
                if hh < GDN_HEADS:
                    dblk = dblk * (HEAD_DIM ** -0.5)
                dblk = rn * (dblk - unit * jnp.sum(dblk * unit, axis=-1, keepdims=True))
            dact.append(dblk)
        dy = jnp.concatenate(dact, axis=1) * (sig * (1.0 + y * (1.0 - sig)))
        dx = dy * w[CONV_K - 1:CONV_K, :]
        dws = [None] * CONV_K
        dws[CONV_K - 1] = jnp.sum(dy[8:8 + tm] * window[8:8 + tm], axis=0, keepdims=True)
        for j in range(CONV_K - 1):
            s = CONV_K - 1 - j
            dx += _shift_rows(dy, -s) * w[j:j + 1, :]
            dws[j] = jnp.sum(dy[8:8 + tm] * _shift_rows(window, s)[8:8 + tm], axis=0, keepdims=True)
        dw_ref[...] += jnp.concatenate(dws, axis=0)
        dproj_ref[:, :QKV_WIDTH] = dx[8:8 + tm].astype(dproj_ref.dtype)
        dproj_ref[:, COL_Z:COL_BA] = dzp_ref[...]

        ba = ba_ref[...]
        dbg_ = dbg_ref[...]
        lane = lax.broadcasted_iota(jnp.int32, ba.shape, 1)
        beta = _sigmoid(ba)
        pre = ba + dtb_ref[...]
        neg_a = -jnp.exp(al_ref[...])
        g = neg_a * _softplus(pre)
        is_g = (lane >= GDN_HEADS) & (lane < 2 * GDN_HEADS)
        da_raw = jnp.where(is_g, dbg_ * neg_a * _sigmoid(pre), 0.0)
        dba = jnp.where(lane < GDN_HEADS, dbg_ * beta * (1.0 - beta), da_raw)
        dproj_ref[:, COL_BA:] = dba.astype(dproj_ref.dtype)
        dal_ref[...] += jnp.sum(jnp.where(is_g, dbg_ * g, 0.0), axis=0, keepdims=True)
        ddtb_ref[...] += jnp.sum(da_raw, axis=0, keepdims=True)

    lane_vec = pl.BlockSpec((1, 128), lambda i: (0, 0))
    return pl.pallas_call(
        body, name=name, grid=(n_tiles,),
        in_specs=[pl.BlockSpec((tm, QKV_WIDTH), lambda i: (i, 0)),
                  pl.BlockSpec((8, QKV_WIDTH), lambda i: (jnp.maximum(i * hb - 1, 0), 0)),
                  pl.BlockSpec((8, QKV_WIDTH), lambda i: (jnp.minimum((i + 1) * hb, last_halo), 0)),
                  pl.BlockSpec((tm, 128), lambda i: (i, COL_BA // 128)),
                  pl.BlockSpec((CONV_K, QKV_WIDTH), lambda i: (0, 0)), lane_vec, lane_vec,
                  pl.BlockSpec((tm, QKV_WIDTH), lambda i: (i, 0)),
                  pl.BlockSpec((8, QKV_WIDTH), lambda i: (jnp.minimum((i + 1) * hb, last_halo), 0)),
                  pl.BlockSpec((tm, 128), lambda i: (i, 0)),
                  pl.BlockSpec((tm, GDN_WIDTH + POOL_WIDTH), lambda i: (i, 0))],
        out_specs=[pl.BlockSpec((tm, D_IN_PAD), lambda i: (i, 0)),
                   pl.BlockSpec((CONV_K, QKV_WIDTH), lambda i: (0, 0)), lane_vec, lane_vec],
        out_shape=[jax.ShapeDtypeStruct((T, D_IN_PAD), MXU_DTYPE),
                   jax.ShapeDtypeStruct((CONV_K, QKV_WIDTH), F32),
                   jax.ShapeDtypeStruct((1, 128), F32), jax.ShapeDtypeStruct((1, 128), F32)],
        compiler_params=_params("arbitrary"),
    )(proj, proj, proj, proj, conv_w, a_log_l, dt_bias_l, dqkv, dqkv, dbg, dzp)


def _mod_part(c_all, w_ada, b_part, name):
    def body(c_ref, w_ref, b_ref, out_ref):
        cc = c_ref[...]
        out_ref[...] = _mdot(cc * _sigmoid(cc), w_ref[...], NN) + b_ref[...]

    return pl.pallas_call(
        body, name=name, out_shape=jax.ShapeDtypeStruct((c_all.shape[0], w_ada.shape[1]), F32),
        compiler_params=_params(),
    )(c_all, w_ada, b_part)


def _w_ada_grad(c_all, dmod_part, name):
    def body(c_ref, d_ref, out_ref):
        cc = c_ref[...]
        out_ref[...] = _mdot(cc * _sigmoid(cc), d_ref[...], TN)

    return pl.pallas_call(
        body, name=name, out_shape=jax.ShapeDtypeStruct((c_all.shape[1], dmod_part.shape[1]), F32),
        compiler_params=_params(),
    )(c_all, dmod_part)


def _sum_parts(parts, name):
    _, R, C = parts.shape
    tr = max([t for t in range(16, min(R, 512) + 1, 16) if R % t == 0], default=R)

    def body(p_ref, out_ref):
        acc = p_ref[0].astype(F32)
        for s in range(1, N_DEV):
            acc += p_ref[s].astype(F32)
        out_ref[...] = acc

    return pl.pallas_call(
        body, name=name, grid=(R // tr,),
        in_specs=[pl.BlockSpec((N_DEV, tr, C), lambda i: (0, i, 0))],
        out_specs=pl.BlockSpec((tr, C), lambda i: (i, 0)),
        out_shape=jax.ShapeDtypeStruct((R, C), F32),
        compiler_params=_params("parallel"),
    )(parts)


def _adamw_math(w, g, m, v):
    mm = ADAM_B1 * m + (1.0 - ADAM_B1) * g
    vv = ADAM_B2 * v + (1.0 - ADAM_B2) * (g * g)
    m_hat = mm / (1.0 - ADAM_B1 ** ADAM_STEP)
    v_hat = vv / (1.0 - ADAM_B2 ** ADAM_STEP)
    return -ADAM_LR * (m_hat / (jnp.sqrt(v_hat) + ADAM_EPS) + ADAM_WD * w), mm, vv


def _adamw_small(ws, gs, ms, vs, name):
    n = len(ws)

    def body(*refs):
        for i in range(n):
            d, mm, vv = _adamw_math(*[refs[k * n + i][...] for k in range(4)])
            refs[4 * n + i][...] = d
            refs[5 * n + i][...] = mm
            refs[6 * n + i][...] = vv

    out = pl.pallas_call(
        body, name=name, out_shape=[jax.ShapeDtypeStruct(w.shape, F32) for w in ws] * 3,
        compiler_params=_params(),
    )(*ws, *gs, *ms, *vs)
    return out[:n], out[n:2 * n], out[2 * n:]


def _adamw(w, g, m, v, name):
    R, C = w.shape
    tr = max([t for t in range(8, min(R, 512) + 1, 8) if R % t == 0], default=R)

    def body(w_ref, g_ref, m_ref, v_ref, d_ref, mo_ref, vo_ref):
        d_ref[...], mo_ref[...], vo_ref[...] = _adamw_math(w_ref[...], g_ref[...], m_ref[...], v_ref[...])

    spec = pl.BlockSpec((tr, C), lambda i: (i, 0))
    return pl.pallas_call(
        body, name=name, grid=(R // tr,),
        in_specs=[spec] * 4, out_specs=[spec] * 3,
        out_shape=[jax.ShapeDtypeStruct((R, C), F32)] * 3,
        compiler_params=_params("parallel"),
    )(w, g, m, v)


def _weight_grad(a, b, name, dep=None):
    return _matmul([(a, b)], TN, WIRE_DTYPE, name, tm=1408, tn=1024, tk=1024, dep=dep)


def _rows_of(flat, lanes=1024):
    flat = flat.reshape(-1)
    n = -(-flat.shape[0] // lanes) * lanes
    return jnp.pad(flat, (0, n - flat.shape[0])).reshape(n // lanes, lanes)


def _pad_rows(a, rows):
    return jnp.pad(a, ((0, rows - a.shape[0]), (0, 0)))


def kernel(x, c, w_ada, b_ada, norm_ffn1, ffn1_gate, ffn1_up, ffn1_down, norm_mix, w_in, conv_w, a_log, dt_bias, gdn_norm, pool_w, pool_scale, w_out, norm_ffn2, ffn2_gate, ffn2_up, ffn2_down, final_norm, loss_target, m_w_ada, m_b_ada, m_norm_ffn1, m_ffn1_gate, m_ffn1_up, m_ffn1_down, m_norm_mix, m_w_in, m_conv_w, m_a_log, m_dt_bias, m_gdn_norm, m_pool_w, m_pool_scale, m_w_out, m_norm_ffn2, m_ffn2_gate, m_ffn2_up, m_ffn2_down, m_final_norm, v_w_ada, v_b_ada, v_norm_ffn1, v_ffn1_gate, v_ffn1_up, v_ffn1_down, v_norm_mix, v_w_in, v_conv_w, v_a_log, v_dt_bias, v_gdn_norm, v_pool_w, v_pool_scale, v_w_out, v_norm_ffn2, v_ffn2_gate, v_ffn2_up, v_ffn2_down, v_final_norm):
    T, D = x.shape[1], x.shape[2]
    Fs = ffn1_gate.shape[2]
    Ws = w_in.shape[2]
    Ws_pad = -(-Ws // 16) * 16
    Os = w_out.shape[1]
    Ms = w_ada.shape[2]
    Cs = conv_w.shape[2]
    me = 4 * lax.axis_index("x") + 2 * lax.axis_index("y") + lax.axis_index("c")
    x0, target = x[0], loss_target[0]

    def wire(a):
        return a.astype(WIRE_DTYPE)

    def token(started):
        return started[4][:1, :1]

    def with_own(landed, own):
        return lax.dynamic_update_slice(landed, own[None], (me, 0, 0))

    def full(landed):
        return landed.reshape(-1, D).astype(MXU_DTYPE)

    no_dep = jnp.zeros((8, 128), F32)
    small = jnp.concatenate([_pad_rows(c, 8), _pad_rows(jnp.pad(conv_w[0], ((0, 0), (0, D - Cs))), 8)], axis=0)
    got, = _all_gather([small], "gather_small")
    c_all = got[:, 0, :]
    conv_full = jnp.transpose(got[:, 8:8 + CONV_K, :Cs], (1, 0, 2)).reshape(CONV_K, QKV_WIDTH)
    b_part = lax.dynamic_slice(b_ada, (0, me * Ms), (1, Ms))
    mod_part = _mod_part(c_all, w_ada[0], b_part, "mod_part")

    w1 = [wire(ffn1_gate[0].T)]
    w1u = [wire(ffn1_up[0].T)]
    w1d = [wire(ffn1_down[0])]
    w2 = [wire(_pad_rows(w_in[0].T, Ws_pad)), wire(w_out[0])]
    w3 = [wire(ffn2_gate[0].T), wire(ffn2_up[0].T), wire(ffn2_down[0])]
    off23 = [0, Ws_pad, Ws_pad + Os, Ws_pad + Os + Fs, Ws_pad + Os + 2 * Fs]
    mod_parts, *w1_all = _all_gather([mod_part] + w1, "gather_mod_w1")
    mod_all = jnp.transpose(mod_parts, (1, 0, 2)).reshape(N_DEV, N_MOD * D)
    mod = lax.dynamic_slice(mod_all, (me, 0), (1, N_MOD * D)).reshape(N_MOD, 1, D)
    sh1, sc1, gt1, sh2, sc2, gt2, sh3, sc3, gt3 = [mod[i] for i in range(N_MOD)]
    wg1_t = full(w1_all[0])
    w1u_sent = _exchange_start(w1u, True, w1_all[0], "w1u_start")
    w1d_sent = _exchange_start(w1d, True, w1u_sent[4], "w1d_start")
    w2_sent = _exchange_start(w2, True, w1d_sent[4], "w2_start")
    w3_sent = _exchange_start(w3, True, w2_sent[4], "w3_start")

    lane_pad = lambda a: jnp.pad(a, ((0, 0), (GDN_HEADS, 128 - 2 * GDN_HEADS)))
    a_log_l, dt_bias_l = lane_pad(a_log), lane_pad(dt_bias)
    pool_w_m = pool_w[0].astype(MXU_DTYPE)

    g1, h1 = _swiglu_gate(x0, wg1_t, norm_ffn1, sc1 + token(w3_sent), sh1, "ffn1_gate")
    wu1_t = full(with_own(_exchange_wait(w1u_sent, h1, True, "w1u_wait")[0], w1u[0]))
    u1, a1 = _swiglu_up(h1, wu1_t, g1, "ffn1_up")
    wd1 = full(with_own(_exchange_wait(w1d_sent, a1, True, "w1d_wait")[0], w1d[0]))
    y1, x1, h2 = _matmul_resid_norm_mod(a1, wd1, x0, gt1, 0.5, norm_mix, sc2, sh2, "ffn1_down")
    w_in_all, wo_all = [with_own(z, own) for z, own in zip(_exchange_wait(w2_sent, h2, True, "w2_wait"), w2)]
    w_in_t = w_in_all[:, :Ws, :].reshape(-1, D).astype(MXU_DTYPE)
    wo = full(wo_all)
    w_in_re = jnp.concatenate([w_in_t[:COL_Z + GDN_WIDTH], w_in_t[D_IN - POOL_WIDTH:],
                               w_in_t[4 * GDN_WIDTH:4 * GDN_WIDTH + 2 * GDN_HEADS],
                               jnp.zeros((128 - 2 * GDN_HEADS, D), MXU_DTYPE)], axis=0)
    proj = _matmul([(h2, w_in_re)], NT, F32, "proj_in", tm=512, tn=D_IN_PAD, tk=D)
    qkv, bg = _gdn_prep(proj, conv_full, a_log_l, dt_bias_l, "gdn_prep")
    tinv, u_c, w_c, qd_c, kd_c, p_c, cd_c = _gdn_chunk_fwd(qkv, bg, "gdn_chunk_fwd")
    o, s_all, vn_c = _gdn_scan_fwd(u_c, w_c, qd_c, kd_c, p_c, cd_c, "gdn_scan_fwd")
    mix_in = _mix_post(o, proj, gdn_norm, pool_w_m, pool_scale, "mix_post")
    mixed, x2, h3 = _matmul_resid_norm_mod(mix_in, wo, x1, gt2, 1.0, norm_ffn2, sc3, sh3, "mix_out")
    wg2_t, wu2_t, wd2 = [full(with_own(z, own))
                         for z, own in zip(_exchange_wait(w3_sent, h3, True, "w3_wait"), w3)]
    g3, u3, a3, loss_row, d3, d_final, dy3, dgt3 = _swiglu_fwd_loss(
        h3, wg2_t, wu2_t, wd2, x2, gt3, final_norm.reshape(1, D), target, "ffn2_fwd_loss")

    dg3, du3, d2, d_n3, dsc3, dsh3, dmixed, dgt2 = _swiglu_bwd(
        dy3, wd2, g3, u3, wg2_t, wu2_t, "ffn2_bwd_norm3_bwd", (x2, norm_ffn2, sc3), d3,
        produced_by=(mixed, gt2, 1.0))
    d_wd2 = _weight_grad(a3, dy3, "ffn2_dwd")
    d_wg2 = _weight_grad(dg3, h3, "ffn2_dwg")
    d_wu2 = _weight_grad(du3, h3, "ffn2_dwu")
    dmix_in = _matmul([(dmixed, wo)], NT, F32, "mix_out_bwd", tm=512, tn=GDN_WIDTH + POOL_WIDTH, tk=D)
    d_wo = _matmul([(mix_in, dmixed)], TN, WIRE_DTYPE, "mix_dwo", tm=GDN_WIDTH + POOL_WIDTH, tn=D, tk=1024)
    do, dzp, d_gn, d_pw, d_ps = _mix_post_bwd(dmix_in, o, proj, gdn_norm, pool_w_m, pool_scale, "mix_post_bwd")
    dvn, dw_c, dqd, dkd, dp_c, dcd = _gdn_scan_bwd(do, w_c, qd_c, kd_c, p_c, cd_c, s_all, vn_c, "gdn_scan_bwd")
    dqkv, dbg = _gdn_chunk_bwd(qkv, bg, tinv, u_c, w_c, dvn, dw_c, dqd, dkd, dp_c, dcd, "gdn_chunk_bwd")
    dproj, d_conv, d_al, d_dtb = _gdn_prep_bwd(proj, conv_full, a_log_l, dt_bias_l, dqkv, dbg, dzp, "gdn_prep_bwd")
    d_win_re = _matmul([(dproj, h2)], TN, WIRE_DTYPE, "proj_in_dw", tm=D_IN_PAD, tn=D, tk=1024)
    d_win_t = jnp.concatenate([d_win_re[:COL_Z + GDN_WIDTH], d_win_re[COL_BA:COL_BA + 2 * GDN_HEADS],
                               d_win_re[COL_P:COL_P + POOL_WIDTH]], axis=0)
    d_win_blocks = jnp.pad(d_win_t.reshape(N_DEV, Ws, D), ((0, 0), (0, Ws_pad - Ws), (0, 0)))
    parts23 = jnp.concatenate(
        [wire(d_win_blocks), wire(d_wo.reshape(N_DEV, Os, D)), wire(d_wg2.reshape(N_DEV, Fs, D)),
         wire(d_wu2.reshape(N_DEV, Fs, D)), wire(d_wd2.reshape(N_DEV, Fs, D))], axis=1)
    own23 = lax.dynamic_index_in_dim(parts23, me, 0, keepdims=False)
    g23_sent = _exchange_start([parts23], False, no_dep, "g23_start")
    d1, d_n2, dsc2, dsh2, dy1, dgt1 = _norm_bwd((dproj, w_in_re), x1, norm_mix, sc2 + token(g23_sent), d2,
                                                "proj_in_bwd_norm2_bwd", produced_by=(y1, gt1, 0.5))
    dg1, du1, grad_x, d_n1, dsc1, dsh1 = _swiglu_bwd(
        dy1, wd1, g1, u1, wg1_t, wu1_t, "ffn1_bwd_norm1_bwd", (x0, norm_ffn1, sc1), d1)

    dmod = jnp.concatenate([dsh1, dsc1, dgt1, dsh2, dsc2, dgt2, dsh3, dsc3, dgt3], axis=0)
    small_rows = [dmod.reshape(-1), d_n1[0], d_n2[0], d_n3[0], d_final[0], d_gn[0], d_ps[0],
                  d_al[0, GDN_HEADS:2 * GDN_HEADS], d_dtb[0, GDN_HEADS:2 * GDN_HEADS], loss_row[0, :1],
                  d_conv.reshape(-1), d_pw.reshape(-1)]
    lanes = 1024
    small_rows = [_rows_of(r, lanes) for r in small_rows]
    n_rows = [r.shape[0] for r in small_rows]
    row_off = [sum(n_rows[:i]) for i in range(len(n_rows))]
    total = -(-sum(n_rows) // 8) * 8
    slab = _pad_rows(jnp.concatenate(small_rows, axis=0), total)
    slab_sent = _exchange_start([slab], True, no_dep, "small_grads_start")

    def send_ffn1(a, b, which, dep):
        parts = _weight_grad(a, b, f"ffn1_{which}", dep=dep).reshape(N_DEV, Fs, D)
        own = lax.dynamic_index_in_dim(parts, me, 0, keepdims=False)
        return _exchange_start([parts], False, no_dep, f"g1_{which}_start"), own

    g1_wg, own_wg = send_ffn1(dg1, h1, "dwg", slab_sent[4])
    g1_wu, own_wu = send_ffn1(du1, h1, "dwu", g1_wg[4])
    g1_wd, own_wd = send_ffn1(a1, dy1, "dwd", g1_wu[4])

    slab_all = with_own(_exchange_wait(slab_sent, g1_wg[4], True, "small_grads_wait")[0], slab)
    summed = _sum_parts(slab_all, "sum_small_grads")

    def piece(idx, n):
        return summed[row_off[idx]:row_off[idx] + n_rows[idx]].reshape(-1)[:n]

    g_b_ada = piece(0, N_MOD * D).reshape(1, N_MOD * D)
    g_n1, g_n2, g_n3 = piece(1, D).reshape(1, D), piece(2, D).reshape(1, D), piece(3, D).reshape(1, D)
    g_final = piece(4, D)
    g_gn = piece(5, HEAD_DIM).reshape(1, HEAD_DIM)
    g_ps = piece(6, POOL_WIDTH).reshape(1, POOL_WIDTH)
    g_al = piece(7, GDN_HEADS).reshape(1, GDN_HEADS)
    g_dtb = piece(8, GDN_HEADS).reshape(1, GDN_HEADS)
    loss = piece(9, 1)[0]
    g_conv = lax.dynamic_slice(piece(10, CONV_K * QKV_WIDTH).reshape(1, CONV_K, QKV_WIDTH), (0, 0, me * Cs),
                               (1, CONV_K, Cs))
    g_pw = piece(11, POOL_GROUPS * 128 * 128).reshape(1, POOL_GROUPS, 128, 128)

    dmod_all = slab_all[:, row_off[0]:row_off[0] + n_rows[0], :].reshape(N_DEV, -1)[:, :N_MOD * D]
    g_w_ada = _w_ada_grad(c_all, lax.dynamic_slice(dmod_all, (0, me * Ms), (N_DEV, Ms)), "w_ada_grad")[None]

    big23 = _sum_parts(with_own(_exchange_wait(g23_sent, g1_wd[4], False, "g23_wait")[0], own23), "sum_grads23")
    g_rows = dict(w_in=big23[:Ws], w_out=big23[off23[1]:off23[1] + Os],
                  ffn2_gate=big23[off23[2]:off23[2] + Fs], ffn2_up=big23[off23[3]:off23[3] + Fs],
                  ffn2_down=big23[off23[4]:off23[4] + Fs])
    column_sharded = ("w_in", "ffn1_gate", "ffn1_up", "ffn2_gate", "ffn2_up")

    names = ["w_ada", "b_ada", "norm_ffn1", "ffn1_gate", "ffn1_up", "ffn1_down", "norm_mix", "w_in", "conv_w",
             "a_log", "dt_bias", "gdn_norm", "pool_w", "pool_scale", "w_out", "norm_ffn2", "ffn2_gate", "ffn2_up",
             "ffn2_down", "final_norm"]
    weights = dict(zip(names, [w_ada, b_ada, norm_ffn1, ffn1_gate, ffn1_up, ffn1_down, norm_mix, w_in, conv_w,
                               a_log, dt_bias, gdn_norm, pool_w, pool_scale, w_out, norm_ffn2, ffn2_gate, ffn2_up,
                               ffn2_down, final_norm]))
    ms = dict(zip(names, [m_w_ada, m_b_ada, m_norm_ffn1, m_ffn1_gate, m_ffn1_up, m_ffn1_down, m_norm_mix, m_w_in,
                          m_conv_w, m_a_log, m_dt_bias, m_gdn_norm, m_pool_w, m_pool_scale, m_w_out, m_norm_ffn2,
                          m_ffn2_gate, m_ffn2_up, m_ffn2_down, m_final_norm]))
    vs = dict(zip(names, [v_w_ada, v_b_ada, v_norm_ffn1, v_ffn1_gate, v_ffn1_up, v_ffn1_down, v_norm_mix, v_w_in,
                          v_conv_w, v_a_log, v_dt_bias, v_gdn_norm, v_pool_w, v_pool_scale, v_w_out, v_norm_ffn2,
                          v_ffn2_gate, v_ffn2_up, v_ffn2_down, v_final_norm]))
    grads = dict(w_ada=g_w_ada, b_ada=g_b_ada, norm_ffn1=g_n1, norm_mix=g_n2, conv_w=g_conv,
                 a_log=g_al, dt_bias=g_dtb, gdn_norm=g_gn, pool_w=g_pw, pool_scale=g_ps,
                 norm_ffn2=g_n3, final_norm=g_final)
    delta, new_m, new_v = {}, {}, {}

    def adamw_big(n):
        if n in column_sharded:
            view, back = (lambda a: a[0].T), (lambda a: a.T[None])
        else:
            view, back = (lambda a: a[0]), (lambda a: a[None])
        g = g_rows[n] if n in g_rows else view(grads[n])
        d_, m_, v_ = _adamw(view(weights[n]), g, view(ms[n]), view(vs[n]), f"adamw_{n}")
        grads[n], delta[n], new_m[n], new_v[n] = back(g), back(d_), back(m_), back(v_)

    early = ["w_ada", "w_in", "w_out", "ffn2_gate", "ffn2_up", "ffn2_down"]
    late = ["ffn1_gate", "ffn1_up", "ffn1_down"]
    for n in early:
        adamw_big(n)
    done = sum(delta[n][0, :1, :1] for n in early)

    def arrived(started, own, which):
        landed, = _exchange_wait(started, done, False, f"g1_{which}_wait")
        return _sum_parts(with_own(landed, own), f"sum_{which}")

    g_rows["ffn1_gate"] = arrived(g1_wg, own_wg, "dwg")
    g_rows["ffn1_up"] = arrived(g1_wu, own_wu, "dwu")
    g_rows["ffn1_down"] = arrived(g1_wd, own_wd, "dwd")
    for n in late:
        adamw_big(n)
    small_names = [n for n in names if n not in early + late]
    two_d = lambda a: a.reshape(-1, a.shape[-1])
    small_out = _adamw_small(*[[two_d(src[n]) for n in small_names] for src in (weights, grads, ms, vs)],
                             "adamw_small")
    for dst, outs in zip((delta, new_m, new_v), small_out):
        for n, a in zip(small_names, outs):
            dst[n] = a.reshape(weights[n].shape)

    return (loss, grad_x[None], *[grads[n] for n in names], *[delta[n] for n in names],
            *[new_m[n] for n in names], *[new_v[n] for n in names])
```

```python
import functools

import jax
import jax.numpy as jnp
from jax import lax
from jax.experimental import pallas as pl
from jax.experimental.pallas import tpu as pltpu

F32 = jnp.float32
MXU_DTYPE = jnp.bfloat16
WIRE_DTYPE = jnp.bfloat16
EPS = 1e-6
N_DEV = 8
GDN_HEADS = 4
HEAD_DIM = 128
GDN_WIDTH = GDN_HEADS * HEAD_DIM
POOL_WINDOWS = (2, 4, 8, 16)
POOL_GROUPS = len(POOL_WINDOWS)
POOL_WIDTH = 512
CONV_K = 4
CHUNK = 64
QKV_WIDTH = 3 * GDN_WIDTH
D_IN = 4 * GDN_WIDTH + 2 * GDN_HEADS + POOL_WIDTH
D_IN_PAD = 4 * GDN_WIDTH + POOL_WIDTH + 128
COL_Z = QKV_WIDTH
COL_P = 4 * GDN_WIDTH
COL_BA = 4 * GDN_WIDTH + POOL_WIDTH
N_MOD = 9
HALO = 16
VMEM_LIMIT = 56 * 1024 * 1024
ADAM_LR, ADAM_B1, ADAM_B2, ADAM_EPS, ADAM_WD, ADAM_STEP = 0.001, 0.9, 0.999, 1e-08, 0.01, 10
FFN_TOKEN_TILE = 256
FFN_HIDDEN_TILE = 1408
CHUNKS_PER_STEP = 4
SCAN_CHUNKS_PER_STEP = 8

NT = (((1,), (1,)), ((), ()))
NN = (((1,), (0,)), ((), ()))
TN = (((0,), (0,)), ((), ()))


def _params(*sem):
    return pltpu.CompilerParams(dimension_semantics=tuple(sem), vmem_limit_bytes=VMEM_LIMIT)


def _dot(a, b, dims):
    return lax.dot_general(a, b, dims, preferred_element_type=F32)


def _mdot(a, b, dims):
    return _dot(a.astype(MXU_DTYPE), b.astype(MXU_DTYPE), dims)


def _split(a):
    hi = a.astype(jnp.bfloat16)
    return hi, (a - hi.astype(F32)).astype(jnp.bfloat16)


def _dot3(a, b, dims):
    (ah, al), (bh, bl) = a, b
    return (_dot(al, bh, dims) + _dot(ah, bl, dims)) + _dot(ah, bh, dims)


def _sigmoid(v):
    return 0.5 * jnp.tanh(0.5 * v) + 0.5


def _softplus(v):
    return jnp.maximum(v, 0.0) + jnp.log(1.0 + jnp.exp(-jnp.abs(v)))


def _shift_rows(v, s):
    n = v.shape[0]
    s = s % n
    return v if s == 0 else pltpu.roll(v, s, 0)


def _tile(n, want):
    t = min(n, want)
    while n % t:
        t //= 2
    return t


def _all_gather(blocks, name, dep=None):
    n = len(blocks)

    def body(*refs):
        x_refs, out_refs = refs[:n], refs[-3 - n:-3]
        send_sems, recv_sems, local_sems = refs[-3:]
        x, y, c = lax.axis_index("x"), lax.axis_index("y"), lax.axis_index("c")
        me, sibling = (x, y, c), (x, y, 1 - c)
        chips = [(1 - x, y), (x, 1 - y), (1 - x, 1 - y)]

        def copy(a, k, blk, to, own=False):
            rows = out_refs[a].at[4 * blk[0] + 2 * blk[1] + blk[2]]
            return pltpu.make_async_remote_copy(
                src_ref=x_refs[a] if own else rows, dst_ref=rows,
                send_sem=send_sems.at[7 * a + k], recv_sem=recv_sems.at[7 * a + k],
                device_id=to, device_id_type=pl.DeviceIdType.MESH)

        mine = [pltpu.make_async_copy(x_refs[a], out_refs[a].at[4 * x + 2 * y + c], local_sems.at[a])
                for a in range(n)]
        for cp in mine:
            cp.start()
        sent = []
        for a in range(n):
            sent.append(copy(a, 0, me, sibling, own=True))
            sent += [copy(a, 1 + j, me, (*chip, c), own=True) for j, chip in enumerate(chips)]
        for cp in sent:
            cp.start()
        for a in range(n):
            for j, chip in enumerate(chips):
                copy(a, 1 + j, (*chip, c), me).wait_recv()
                passed = copy(a, 4 + j, (*chip, c), sibling)
                passed.start()
                sent.append(passed)
        for a in range(n):
            copy(a, 0, sibling, me).wait_recv()
            for j, chip in enumerate(chips):
                copy(a, 4 + j, (*chip, 1 - c), me).wait_recv()
        for cp in sent:
            cp.wait_send()
        for cp in mine:
            cp.wait()

    hbm = pl.BlockSpec(memory_space=pltpu.HBM)
    return pl.pallas_call(
        body, name=name,
        out_shape=[jax.ShapeDtypeStruct((N_DEV,) + b.shape, b.dtype) for b in blocks],
        in_specs=[hbm] * n + [pl.BlockSpec(memory_space=pl.ANY)] * (dep is not None),
        out_specs=[hbm] * n,
        scratch_shapes=[pltpu.SemaphoreType.DMA((7 * n,)), pltpu.SemaphoreType.DMA((7 * n,)),
                        pltpu.SemaphoreType.DMA((n,))],
    )(*(list(blocks) + ([] if dep is None else [dep])))


_HBM = pl.BlockSpec(memory_space=pltpu.HBM)
_SEM = pl.BlockSpec(memory_space=pltpu.SEMAPHORE)
_ANY = pl.BlockSpec(memory_space=pl.ANY)
_EFFECT = pltpu.SideEffectType.DATAFLOW_SIDE_EFFECTING
_FLIPS = [(0, 0, 1), (0, 1, 0), (0, 1, 1), (1, 0, 0), (1, 0, 1), (1, 1, 0), (1, 1, 1)]


def _peers():
    x, y, c = lax.axis_index("x"), lax.axis_index("y"), lax.axis_index("c")
    return 4 * x + 2 * y + c, [(1 - x if fx else x, 1 - y if fy else y, 1 - c if fc else c)
                               for fx, fy, fc in _FLIPS]


def _exchange_start(srcs, gather, dep, name):
    n = len(srcs)
    lands = [(N_DEV,) + tuple(s.shape if gather else s.shape[1:]) for s in srcs]

    def body(*refs):
        src_refs, land_refs = refs[:n], refs[n:2 * n]
        send_sems, recv_sems = refs[2 * n + 1], refs[2 * n + 2]
        token = refs[-1]
        me, peers = _peers()
        for a in range(n):
            for k, (px, py, pc) in enumerate(peers):
                pltpu.make_async_remote_copy(
                    src_ref=src_refs[a] if gather else src_refs[a].at[4 * px + 2 * py + pc],
                    dst_ref=land_refs[a].at[me],
                    send_sem=send_sems.at[7 * a + k], recv_sem=recv_sems.at[7 * a + k],
                    device_id=(px, py, pc), device_id_type=pl.DeviceIdType.MESH).start()
        token[...] = jnp.zeros_like(token)

    srcs = [pltpu.with_memory_space_constraint(s, pltpu.HBM) for s in srcs]
    empties = [pltpu.with_memory_space_constraint(lax.empty(shape, s.dtype), pltpu.HBM)
               for shape, s in zip(lands, srcs)]
    out = pl.pallas_call(
        body, name=name,
        out_shape=(pltpu.SemaphoreType.DMA((7 * n,)), pltpu.SemaphoreType.DMA((7 * n,)),
                   *[pltpu.HBM(shape, s.dtype) for shape, s in zip(lands, srcs)],
                   jax.ShapeDtypeStruct((8, 128), F32)),
        in_specs=(*[_HBM] * (2 * n), _ANY),
        out_specs=(_SEM, _SEM, *[_HBM] * n, pl.BlockSpec(memory_space=pltpu.VMEM)),
        input_output_aliases={n + a: 2 + a for a in range(n)},
        compiler_params=pltpu.CompilerParams(has_side_effects=_EFFECT),
    )(*srcs, *empties, dep)
    return out[0], out[1], srcs, list(out[2:2 + n]), out[-1]


def _exchange_wait(started, after, gather, name):
    send_sems, recv_sems, srcs, lands, _ = started
    n = len(srcs)

    def body(*refs):
        src_refs, land_refs = refs[:n], refs[n:2 * n]
        send_sems, recv_sems = refs[2 * n], refs[2 * n + 1]
        _, peers = _peers()
        for a in range(n):
            for k, peer in enumerate(peers):
                copy = pltpu.make_async_remote_copy(
                    src_ref=src_refs[a] if gather else src_refs[a].at[0], dst_ref=land_refs[a].at[0],
                    send_sem=send_sems.at[7 * a + k], recv_sem=recv_sems.at[7 * a + k],
                    device_id=peer, device_id_type=pl.DeviceIdType.MESH)
                copy.wait_send()
                copy.wait_recv()

    out = pl.pallas_call(
        body, name=name,
        out_shape=[pltpu.HBM(z.shape, z.dtype) for z in lands],
        in_specs=(*[_HBM] * (2 * n), _SEM, _SEM, _ANY), out_specs=[_HBM] * n,
        input_output_aliases={n + a: a for a in range(n)},
        compiler_params=pltpu.CompilerParams(has_side_effects=_EFFECT),
    )(*srcs, *lands, send_sems, recv_sems, after)
    return list(out)


def _matmul(pairs, dims, out_dtype, name, tm=512, tn=512, tk=512, dep=None):
    a0, b0 = pairs[0]
    if dims == TN:
        K, M = a0.shape
    else:
        M, K = a0.shape
    N = b0.shape[0] if dims == NT else b0.shape[1]
    tm, tn, tk = _tile(M, tm), _tile(N, tn), _tile(K, tk)
    nk = K // tk
    n_pairs = len(pairs)
    n_in = 2 * n_pairs + (dep is not None)

    def body(*refs):
        out_ref = refs[n_in]

        def product():
            total = _dot(refs[0][...], refs[1][...], dims)
            for p in range(1, n_pairs):
                total += _dot(refs[2 * p][...], refs[2 * p + 1][...], dims)
            return total

        if nk == 1:
            out_ref[...] = product().astype(out_ref.dtype)
            return
        acc_ref = refs[n_in + 1]
        k = pl.program_id(2)

        @pl.when(k == 0)
        def _():
            acc_ref[...] = product()

        @pl.when((k > 0) & (k < nk - 1))
        def _():
            acc_ref[...] += product()

        @pl.when(k == nk - 1)
        def _():
            out_ref[...] = (acc_ref[...] + product()).astype(out_ref.dtype)

    if dims == TN:
        a_spec = pl.BlockSpec((tk, tm), lambda i, j, k: (k, i))
    else:
        a_spec = pl.BlockSpec((tm, tk), lambda i, j, k: (i, k))
    if dims == NT:
        b_spec = pl.BlockSpec((tn, tk), lambda i, j, k: (j, k))
    else:
        b_spec = pl.BlockSpec((tk, tn), lambda i, j, k: (k, j))
    args, specs = [], []
    for a, b in pairs:
        args += [a, b]
        specs += [a_spec, b_spec]
    if dep is not None:
        args.append(dep)
        specs.append(_ANY)
    return pl.pallas_call(
        body, name=name, grid=(M // tm, N // tn, nk),
        in_specs=specs, out_specs=pl.BlockSpec((tm, tn), lambda i, j, k: (i, j)),
        out_shape=jax.ShapeDtypeStruct((M, N), out_dtype),
        scratch_shapes=[pltpu.VMEM((tm, tn), F32)] * (nk > 1),
        compiler_params=_params("parallel", "parallel", "arbitrary"),
    )(*args)


def _vec_spec(d):
    return pl.BlockSpec((1, d), lambda i: (0, 0))


def _matmul_resid_norm_mod(a, b, x, gate, coef, nw, scale, shift, name):
    T, K = a.shape
    D = b.shape[1]
    tm = _tile(T, 512)

    def body(a_ref, b_ref, x_ref, g_ref, nw_ref, sc_ref, sh_ref, y_ref, xo_ref, h_ref):
        y = _dot(a_ref[...], b_ref[...], NN)
        y_ref[...] = y
        xf = x_ref[...] + (coef * g_ref[...]) * y
        xo_ref[...] = xf
        r = lax.rsqrt(jnp.mean(xf * xf, axis=-1, keepdims=True) + EPS)
        h_ref[...] = ((xf * r) * nw_ref[...] * (1.0 + sc_ref[...]) + sh_ref[...]).astype(h_ref.dtype)

    row = pl.BlockSpec((tm, D), lambda i: (i, 0))
    vec = _vec_spec(D)
    return pl.pallas_call(
        body, name=name, grid=(T // tm,),
        in_specs=[pl.BlockSpec((tm, K), lambda i: (i, 0)), _resident((K, D)), row, vec, vec, vec, vec],
        out_specs=[row, row, row],
        out_shape=[jax.ShapeDtypeStruct((T, D), F32), jax.ShapeDtypeStruct((T, D), F32),
                   jax.ShapeDtypeStruct((T, D), MXU_DTYPE)],
        compiler_params=_params("parallel"),
    )(a, b, x, gate, nw, scale, shift)


def _norm_bwd(dh, x, nw, scale, dres, name, produced_by=None):
    T, D = x.shape
    tm = _tile(T, 512)

    def body(*refs):
        if isinstance(dh, tuple):
            dh_value = _dot(refs[0][...], refs[1][...], NN)
            refs = refs[1:]
        else:
            dh_value = refs[0][...]
        _, x_ref, nw_ref, sc_ref, dr_ref = refs[:5]
        n_in = 5 if produced_by is None else 7
        dx_ref, dnw_ref, dsc_ref, dsh_ref = refs[n_in:n_in + 4]

        @pl.when(pl.program_id(0) == 0)
        def _():
            dnw_ref[...] = jnp.zeros_like(dnw_ref)
            dsc_ref[...] = jnp.zeros_like(dsc_ref)
            dsh_ref[...] = jnp.zeros_like(dsh_ref)
            if produced_by is not None:
                refs[n_in + 5][...] = jnp.zeros_like(refs[n_in + 5])

        xf, dh_ = x_ref[...], dh_value
        r = lax.rsqrt(jnp.mean(xf * xf, axis=-1, keepdims=True) + EPS)
        xn = xf * r
        one_sc = 1.0 + sc_ref[...]
        dsh_ref[...] += jnp.sum(dh_, axis=0, keepdims=True)
        t = dh_ * xn
        dsc_ref[...] += jnp.sum(t, axis=0, keepdims=True) * nw_ref[...]
        dnw_ref[...] += jnp.sum(t, axis=0, keepdims=True) * one_sc
        dxn = dh_ * (nw_ref[...] * one_sc)
        dx = dr_ref[...] + r * (dxn - xn * jnp.mean(dxn * xn, axis=-1, keepdims=True))
        dx_ref[...] = dx
        if produced_by is not None:
            y_ref, g_ref, dy_ref, dg_ref = refs[5], refs[6], refs[n_in + 4], refs[n_in + 5]
            dy_ref[...] = ((produced_by[2] * g_ref[...]) * dx).astype(dy_ref.dtype)
            dg_ref[...] += produced_by[2] * jnp.sum(dx * y_ref[...], axis=0, keepdims=True)

    row = pl.BlockSpec((tm, D), lambda i: (i, 0))
    vec = _vec_spec(D)
    vec_out = jax.ShapeDtypeStruct((1, D), F32)
    if isinstance(dh, tuple):
        k_dim = dh[0].shape[1]
        args, in_specs = [dh[0], dh[1]], [pl.BlockSpec((tm, k_dim), lambda i: (i, 0)), _resident((k_dim, D))]
    else:
        args, in_specs = [dh], [row]
    args, in_specs = args + [x, nw, scale, dres], in_specs + [row, vec, vec, row]
    out_specs, out_shape = [row, vec, vec, vec], [jax.ShapeDtypeStruct((T, D), F32), vec_out, vec_out, vec_out]
    if produced_by is not None:
        args += [produced_by[0], produced_by[1]]
        in_specs += [row, vec]
        out_specs += [row, vec]
        out_shape += [jax.ShapeDtypeStruct((T, D), MXU_DTYPE), vec_out]
    return pl.pallas_call(
        body, name=name, grid=(T // tm,),
        in_specs=in_specs, out_specs=out_specs, out_shape=out_shape,
        compiler_params=_params("arbitrary"),
    )(*args)


def _rms(xf):
    r = lax.rsqrt(jnp.mean(xf * xf, axis=-1, keepdims=True) + EPS)
    return r, xf * r


def _norm_bwd_math(dh, xf, nw, sc):
    r, xn = _rms(xf)
    t = dh * xn
    dxn = dh * (nw * (1.0 + sc))
    dx = r * (dxn - xn * jnp.mean(dxn * xn, axis=-1, keepdims=True))
    return dx, jnp.sum(dh, axis=0, keepdims=True), jnp.sum(t, axis=0, keepdims=True)


def _swiglu_gate(x, wg_t, nw, scale, shift, name):
    T, D = x.shape
    Fdim = wg_t.shape[0]
    tm = _tile(T, 512)

    def body(x_ref, wg_ref, nw_ref, sc_ref, sh_ref, g_ref, h_ref):
        _, xn = _rms(x_ref[...])
        hh = (xn * nw_ref[...] * (1.0 + sc_ref[...]) + sh_ref[...]).astype(MXU_DTYPE)
        h_ref[...] = hh
        g_ref[...] = _dot(hh, wg_ref[...], NT).astype(g_ref.dtype)

    row = pl.BlockSpec((tm, D), lambda i: (i, 0))
    vec = _vec_spec(D)
    return pl.pallas_call(
        body, name=name, grid=(T // tm,),
        in_specs=[row, _resident((Fdim, D)), vec, vec, vec],
        out_specs=[pl.BlockSpec((tm, Fdim), lambda i: (i, 0)), row],
        out_shape=[jax.ShapeDtypeStruct((T, Fdim), MXU_DTYPE), jax.ShapeDtypeStruct((T, D), MXU_DTYPE)],
        compiler_params=_params("parallel"),
    )(x, wg_t, nw, scale, shift)


def _swiglu_up(h, wu_t, g, name):
    T, D = h.shape
    Fdim = wu_t.shape[0]
    tm = _tile(T, 512)

    def body(h_ref, wu_ref, g_ref, u_ref, a_ref):
        u = _dot(h_ref[...], wu_ref[...], NT)
        g = g_ref[...].astype(F32)
        u_ref[...] = u.astype(u_ref.dtype)
        a_ref[...] = ((g * _sigmoid(g)) * u).astype(a_ref.dtype)

    frow = pl.BlockSpec((tm, Fdim), lambda i: (i, 0))
    return pl.pallas_call(
        body, name=name, grid=(T // tm,),
        in_specs=[pl.BlockSpec((tm, D), lambda i: (i, 0)), _resident((Fdim, D)), frow],
        out_specs=[frow, frow],
        out_shape=[jax.ShapeDtypeStruct((T, Fdim), MXU_DTYPE)] * 2,
        compiler_params=_params("parallel"),
    )(h, wu_t, g)


def _swiglu_fwd_loss(h, wg_t, wu_t, wd, x, gate, fw, target, name):
    T, D = h.shape
    Fdim = wd.shape[0]
    tm, tf = _tile(T, FFN_TOKEN_TILE), _tile(Fdim, FFN_HIDDEN_TILE)
    row = pl.BlockSpec((tm, D), lambda i: (i, 0))
    frow = pl.BlockSpec((tm, Fdim), lambda i: (i, 0))
    vec, wres = _vec_spec(D), _resident((Fdim, D))
    vec_out = jax.ShapeDtypeStruct((1, D), F32)

    def body(h_ref, wg_ref, wu_ref, wd_ref, x_ref, gt_ref, fw_ref, t_ref,
             g_ref, u_ref, a_ref, loss_ref, dx_ref, dfw_ref, dy_ref, dg_ref):
        hh = h_ref[...]
        y = None
        for k in range(Fdim // tf):
            ks = slice(k * tf, (k + 1) * tf)
            g = _dot(hh, wg_ref[ks, :], NT)
            u = _dot(hh, wu_ref[ks, :], NT)
            a = ((g * _sigmoid(g)) * u).astype(a_ref.dtype)
            g_ref[:, ks] = g.astype(g_ref.dtype)
            u_ref[:, ks] = u.astype(u_ref.dtype)
            a_ref[:, ks] = a
            part = _dot(a, wd_ref[ks, :], NN)
            y = part if y is None else y + part

        @pl.when(pl.program_id(0) == 0)
        def _():
            loss_ref[...] = jnp.zeros_like(loss_ref)
            dfw_ref[...] = jnp.zeros_like(dfw_ref)
            dg_ref[...] = jnp.zeros_like(dg_ref)

        r, xn = _rms(x_ref[...] + (0.5 * gt_ref[...]) * y)
        err = xn * fw_ref[...] - t_ref[...]
        per_tok = jnp.mean(err * err, axis=-1, keepdims=True)
        loss_ref[...] += 0.5 * jnp.sum(per_tok, axis=0, keepdims=True)
        d_out = err * (1.0 / D)
        dfw_ref[...] += jnp.sum(d_out * xn, axis=0, keepdims=True)
        dxn = d_out * fw_ref[...]
        dx = r * (dxn - xn * jnp.mean(dxn * xn, axis=-1, keepdims=True))
        dx_ref[...] = dx
        dy_ref[...] = ((0.5 * gt_ref[...]) * dx).astype(dy_ref.dtype)
        dg_ref[...] += 0.5 * jnp.sum(dx * y, axis=0, keepdims=True)

    return pl.pallas_call(
        body, name=name, grid=(T // tm,),
        in_specs=[row, wres, wres, wres, row, vec, vec, row],
        out_specs=[frow, frow, frow, pl.BlockSpec((1, 128), lambda i: (0, 0)), row, vec, row, vec],
        out_shape=[jax.ShapeDtypeStruct((T, Fdim), MXU_DTYPE)] * 3
        + [jax.ShapeDtypeStruct((1, 128), F32), jax.ShapeDtypeStruct((T, D), F32), vec_out,
           jax.ShapeDtypeStruct((T, D), MXU_DTYPE), vec_out],
        compiler_params=_params("arbitrary"),
    )(h, wg_t, wu_t, wd, x, gate, fw, target)


def _swiglu_bwd(dy, wd, g, u, wg_t, wu_t, name, norm_in, dres, produced_by=None):
    T, D = dy.shape
    Fdim = wd.shape[0]
    tm, tf = _tile(T, FFN_TOKEN_TILE), _tile(Fdim, FFN_HIDDEN_TILE)
    row = pl.BlockSpec((tm, D), lambda i: (i, 0))
    frow = pl.BlockSpec((tm, Fdim), lambda i: (i, 0))
    vec, wres = _vec_spec(D), _resident((Fdim, D))
    vec_out = jax.ShapeDtypeStruct((1, D), F32)
    args, in_specs = [dy, wd, g, u, wg_t, wu_t, *norm_in, dres], [row, wres, frow, frow, wres, wres, row, vec, vec, row]
    out_shape = [jax.ShapeDtypeStruct((T, Fdim), MXU_DTYPE)] * 2 + [jax.ShapeDtypeStruct((T, D), F32)] + [vec_out] * 3
    out_specs = [frow, frow, row, vec, vec, vec]
    if produced_by is not None:
        args += [produced_by[0], produced_by[1]]
        in_specs += [row, vec]
        out_shape += [jax.ShapeDtypeStruct((T, D), MXU_DTYPE), vec_out]
        out_specs += [row, vec]

    def body(*refs):
        it = iter(refs)
        dy_ref, wd_ref, g_ref, u_ref, wg_ref, wu_ref, x_ref, nw_ref, sc_ref, dr_ref = [next(it) for _ in range(10)]
        prev_refs = [next(it), next(it)] if produced_by is not None else None
        dg_ref, du_ref, dx_ref, dnw_ref, dsc_ref, dsh_ref = [next(it) for _ in range(6)]
        prev_out = [next(it), next(it)] if produced_by is not None else None

        @pl.when(pl.program_id(0) == 0)
        def _():
            dnw_ref[...] = jnp.zeros_like(dnw_ref)
            dsc_ref[...] = jnp.zeros_like(dsc_ref)
            dsh_ref[...] = jnp.zeros_like(dsh_ref)
            if produced_by is not None:
                prev_out[1][...] = jnp.zeros_like(prev_out[1])

        dyy = dy_ref[...]
        dh = None
        for k in range(Fdim // tf):
            ks = slice(k * tf, (k + 1) * tf)
            da = _dot(dyy, wd_ref[ks, :], NT)
            gg = g_ref[:, ks].astype(F32)
            sig = _sigmoid(gg)
            dg = (da * u_ref[:, ks].astype(F32) * (sig * (1.0 + gg * (1.0 - sig)))).astype(dg_ref.dtype)
            du = (da * (gg * sig)).astype(du_ref.dtype)
            dg_ref[:, ks] = dg
            du_ref[:, ks] = du
            part = _dot(dg, wg_ref[ks, :], NN) + _dot(du, wu_ref[ks, :], NN)
            dh = part if dh is None else dh + part
        dx_norm, dsh_row, t_row = _norm_bwd_math(dh, x_ref[...], nw_ref[...], sc_ref[...])
        dsh_ref[...] += dsh_row
        dsc_ref[...] += t_row * nw_ref[...]
        dnw_ref[...] += t_row * (1.0 + sc_ref[...])
        dx = dr_ref[...] + dx_norm
        dx_ref[...] = dx
        if produced_by is not None:
            prev_out[0][...] = ((produced_by[2] * prev_refs[1][...]) * dx).astype(prev_out[0].dtype)
            prev_out[1][...] += produced_by[2] * jnp.sum(dx * prev_refs[0][...], axis=0, keepdims=True)

    return pl.pallas_call(
        body, name=name, grid=(T // tm,), in_specs=in_specs, out_specs=out_specs, out_shape=out_shape,
        compiler_params=_params("arbitrary"),
    )(*args)


def _resident(shape):
    return pl.BlockSpec(shape, lambda i: (0,) * len(shape), pipeline_mode=pl.Buffered(1))


def _conv_act(window, w):
    y = window * w[CONV_K - 1:CONV_K, :]
    for j in range(CONV_K - 1):
        y += _shift_rows(window, CONV_K - 1 - j) * w[j:j + 1, :]
    return y


def _gdn_prep(proj, conv_w, a_log_l, dt_bias_l, name):
    T = proj.shape[0]
    tm = _tile(T, 256)
    hb = tm // 8

    def body(cur_ref, halo_ref, ba_ref, w_ref, al_ref, dtb_ref, qkv_ref, bg_ref):
        i = pl.program_id(0)
        halo = jnp.where(i == 0, 0.0, halo_ref[...])
        window = jnp.concatenate([halo, cur_ref[...]], axis=0)
        y = _conv_act(window, w_ref[...])[8:, :]
        act = y * _sigmoid(y)
        for hh in range(3 * GDN_HEADS):
            blk = act[:, hh * HEAD_DIM:(hh + 1) * HEAD_DIM]
            if hh < 2 * GDN_HEADS:
                rn = lax.rsqrt(jnp.sum(blk * blk, axis=-1, keepdims=True) + EPS)
                blk = blk * rn
                if hh < GDN_HEADS:
                    blk = blk * (HEAD_DIM ** -0.5)
            qkv_ref[:, hh * HEAD_DIM:(hh + 1) * HEAD_DIM] = blk
        ba = ba_ref[...]
        lane = lax.broadcasted_iota(jnp.int32, ba.shape, 1)
        beta = _sigmoid(ba)
        g = -jnp.exp(al_ref[...]) * _softplus(ba + dtb_ref[...])
        bg_ref[...] = jnp.where(lane < GDN_HEADS, beta, jnp.where(lane < 2 * GDN_HEADS, g, 0.0))

    return pl.pallas_call(
        body, name=name, grid=(T // tm,),
        in_specs=[pl.BlockSpec((tm, QKV_WIDTH), lambda i: (i, 0)),
                  pl.BlockSpec((8, QKV_WIDTH), lambda i: (jnp.maximum(i * hb - 1, 0), 0)),
                  pl.BlockSpec((tm, 128), lambda i: (i, COL_BA // 128)),
                  pl.BlockSpec((CONV_K, QKV_WIDTH), lambda i: (0, 0)),
                  pl.BlockSpec((1, 128), lambda i: (0, 0)), pl.BlockSpec((1, 128), lambda i: (0, 0))],
        out_specs=[pl.BlockSpec((tm, QKV_WIDTH), lambda i: (i, 0)), pl.BlockSpec((tm, 128), lambda i: (i, 0))],
        out_shape=[jax.ShapeDtypeStruct((T, QKV_WIDTH), F32), jax.ShapeDtypeStruct((T, 128), F32)],
        compiler_params=_params("parallel"),
    )(proj, proj, proj, conv_w, a_log_l, dt_bias_l)


def _chunk_cumsum(v, reverse=False):
    row = lax.broadcasted_iota(jnp.int32, v.shape, 0)
    s = 1
    while s < CHUNK:
        if reverse:
            v = v + jnp.where(row < CHUNK - s, _shift_rows(v, -s), 0.0)
        else:
            v = v + jnp.where(row >= s, _shift_rows(v, s), 0.0)
        s *= 2
    return v


def _row_form(cols):
    padded = jnp.concatenate([cols, jnp.zeros((128 - CHUNK, 128), F32)], axis=0)
    return padded.T[:, :CHUNK]


def _chunk_masks():
    ri = lax.broadcasted_iota(jnp.int32, (CHUNK, CHUNK), 0)
    ci = lax.broadcasted_iota(jnp.int32, (CHUNK, CHUNK), 1)
    return ri >= ci, ri > ci, (ri == ci).astype(F32)


def _unit_lower_inverses(ms, eye):
    rs = [eye - m for m in ms]
    ps = [_split(-m) for m in ms]
    s = 2
    while s < CHUNK:
        ps = [_split(_dot3(p, p, NN)) for p in ps]
        r_parts = [_split(r) for r in rs]
        rs = [r + _dot3(p, rp, NN) for r, p, rp in zip(rs, ps, r_parts)]
        s *= 2
    return rs


def _head_elementwise(k, beta, gc, gcr, causal):
    decay = jnp.where(causal, jnp.exp(jnp.where(causal, gc - gcr, 0.0)), 0.0)
    return decay, k * beta, jnp.exp(gc)


def _head_slices(hh):
    return (slice(hh * HEAD_DIM, (hh + 1) * HEAD_DIM),
            slice(GDN_WIDTH + hh * HEAD_DIM, GDN_WIDTH + (hh + 1) * HEAD_DIM),
            slice(2 * GDN_WIDTH + hh * HEAD_DIM, 2 * GDN_WIDTH + (hh + 1) * HEAD_DIM))


def _gdn_chunk_fwd(qkv, bg, name):
    T = qkv.shape[0]
    cb = _tile(T // CHUNK, CHUNKS_PER_STEP)
    rows = cb * CHUNK

    def body(qkv_ref, bg_ref, tinv_ref, u_ref, w_ref, qd_ref, kd_ref, p_ref, cd_ref):
        masks = _chunk_masks()
        causal, strict, eye = masks
        heads = []
        for ci in range(cb):
            rs = slice(ci * CHUNK, (ci + 1) * CHUNK)
            bgv = bg_ref[rs, :]
            gc_all = _chunk_cumsum(bgv)
            gc_rows = _row_form(gc_all)
            cd_ref[rs, :] = jnp.exp(jnp.broadcast_to(gc_all[CHUNK - 1:CHUNK, :], (CHUNK, 128)))
            for hh in range(GDN_HEADS):
                qs, ks, vs = _head_slices(hh)
                q, k, v = qkv_ref[rs, qs], qkv_ref[rs, ks], qkv_ref[rs, vs]
                beta = bgv[:, hh:hh + 1]
                gc = gc_all[:, GDN_HEADS + hh:GDN_HEADS + hh + 1]
                decay, kb, eg = _head_elementwise(k, beta, gc, gc_rows[GDN_HEADS + hh:GDN_HEADS + hh + 1, :], causal)
                hs = slice(hh * HEAD_DIM, (hh + 1) * HEAD_DIM)
                cs = slice(hh * CHUNK, (hh + 1) * CHUNK)
                qd_ref[rs, hs] = (q * eg).astype(qd_ref.dtype)
                kd_ref[rs, hs] = (k * jnp.exp(gc[CHUNK - 1:CHUNK, :] - gc)).astype(kd_ref.dtype)
                heads.append((rs, hs, cs, q, k, v * beta, kb, kb * eg, decay))
        kks = [_mdot(kb, k, NT) for (_, _, _, _, k, _, kb, _, _) in heads]
        qks = [_mdot(q, k, NT) for (_, _, _, q, k, _, _, _, _) in heads]
        tinvs = _unit_lower_inverses([jnp.where(strict, kk * hd[8], 0.0) for kk, hd in zip(kks, heads)], eye)
        t_parts = [_split(t) for t in tinvs]
        us = [_dot3(tp, _split(hd[5]), NN) for tp, hd in zip(t_parts, heads)]
        ws = [_dot3(tp, _split(hd[7]), NN) for tp, hd in zip(t_parts, heads)]
        for hd, tinv, u, w, qk in zip(heads, tinvs, us, ws, qks):
            rs, hs, cs = hd[0], hd[1], hd[2]
            tinv_ref[rs, cs] = tinv
            u_ref[rs, hs] = u
            w_ref[rs, hs] = w.astype(w_ref.dtype)
            p_ref[rs, cs] = jnp.where(causal, qk * hd[8], 0.0).astype(p_ref.dtype)

    def spec(width):
        return pl.BlockSpec((rows, width), lambda n: (n, 0))

    hw, cw = GDN_WIDTH, GDN_HEADS * CHUNK
    return pl.pallas_call(
        body, name=name, grid=(T // rows,),
        in_specs=[spec(QKV_WIDTH), spec(128)],
        out_specs=[spec(cw), spec(hw), spec(hw), spec(hw), spec(hw), spec(cw), spec(128)],
        out_shape=[jax.ShapeDtypeStruct((T, cw), F32), jax.ShapeDtypeStruct((T, hw), F32),
                   jax.ShapeDtypeStruct((T, hw), MXU_DTYPE), jax.ShapeDtypeStruct((T, hw), MXU_DTYPE),
                   jax.ShapeDtypeStruct((T, hw), MXU_DTYPE), jax.ShapeDtypeStruct((T, cw), MXU_DTYPE),
                   jax.ShapeDtypeStruct((T, 128), F32)],
        compiler_params=_params("parallel"),
    )(qkv, bg)


def _gdn_scan_fwd(u, w, qd, kd, p, cd, name):
    T = u.shape[0]
    cb = _tile(T // CHUNK, SCAN_CHUNKS_PER_STEP)
    rows = cb * CHUNK

    def body(u_ref, w_ref, qd_ref, kd_ref, p_ref, cd_ref, o_ref, s_all_ref, vn_ref, s_ref):
        @pl.when(pl.program_id(0) == 0)
        def _():
            s_ref[...] = jnp.zeros_like(s_ref)

        hss = [slice(hh * HEAD_DIM, (hh + 1) * HEAD_DIM) for hh in range(GDN_HEADS)]
        css = [slice(hh * CHUNK, (hh + 1) * CHUNK) for hh in range(GDN_HEADS)]
        s_cur = [s_ref[hh] for hh in range(GDN_HEADS)]
        for ci in range(cb):
            rs = slice(ci * CHUNK, (ci + 1) * CHUNK)
            for hh in range(GDN_HEADS):
                s_all_ref[ci * GDN_WIDTH + hh * HEAD_DIM:ci * GDN_WIDTH + (hh + 1) * HEAD_DIM, :] = s_cur[hh]
            s_ms = [s.astype(MXU_DTYPE) for s in s_cur]
            w_s = [_dot(w_ref[rs, hs], s_m, NN) for hs, s_m in zip(hss, s_ms)]
            q_s = [_dot(qd_ref[rs, hs], s_m, NN) for hs, s_m in zip(hss, s_ms)]
            v_ms = [(u_ref[rs, hs] - ws_).astype(MXU_DTYPE) for hs, ws_ in zip(hss, w_s)]
            k_v = [_dot(kd_ref[rs, hs], v_m, TN) for hs, v_m in zip(hss, v_ms)]
            p_v = [_dot(p_ref[rs, cs], v_m, NN) for cs, v_m in zip(css, v_ms)]
            for hh in range(GDN_HEADS):
                vn_ref[rs, hss[hh]] = v_ms[hh]
                o_ref[rs, hss[hh]] = q_s[hh] + p_v[hh]
                c_dec = cd_ref[ci * CHUNK:ci * CHUNK + 1, GDN_HEADS + hh:GDN_HEADS + hh + 1]
                s_cur[hh] = s_cur[hh] * c_dec + k_v[hh]
        for hh in range(GDN_HEADS):
            s_ref[hh] = s_cur[hh]

    def spec(width):
        return pl.BlockSpec((rows, width), lambda n: (n, 0))

    hw, cw = GDN_WIDTH, GDN_HEADS * CHUNK
    return pl.pallas_call(
        body, name=name, grid=(T // rows,),
        in_specs=[spec(hw), spec(hw), spec(hw), spec(hw), spec(cw), spec(128)],
        out_specs=[spec(hw), pl.BlockSpec((cb * GDN_WIDTH, HEAD_DIM), lambda n: (n, 0)), spec(hw)],
        out_shape=[jax.ShapeDtypeStruct((T, hw), F32),
                   jax.ShapeDtypeStruct((T // CHUNK * GDN_WIDTH, HEAD_DIM), F32),
                   jax.ShapeDtypeStruct((T, hw), MXU_DTYPE)],
        scratch_shapes=[pltpu.VMEM((GDN_HEADS, HEAD_DIM, HEAD_DIM), F32)],
        compiler_params=_params("arbitrary"),
    )(u, w, qd, kd, p, cd)


def _gdn_scan_bwd(do, w, qd, kd, p, cd, s_all, vn, name):
    T = do.shape[0]
    cb = _tile(T // CHUNK, SCAN_CHUNKS_PER_STEP)
    rows = cb * CHUNK
    n_steps = T // rows

    def body(do_ref, w_ref, qd_ref, kd_ref, p_ref, cd_ref, s_all_ref, vn_ref,
             dvn_ref, dw_ref, dqd_ref, dkd_ref, dp_ref, dcd_ref, ds_ref):
        @pl.when(pl.program_id(0) == 0)
        def _():
            ds_ref[...] = jnp.zeros_like(ds_ref)

        causal, _, _ = _chunk_masks()
        lane = lax.broadcasted_iota(jnp.int32, (CHUNK, 128), 1)
        heads = range(GDN_HEADS)
        hss = [slice(hh * HEAD_DIM, (hh + 1) * HEAD_DIM) for hh in heads]
        css = [slice(hh * CHUNK, (hh + 1) * CHUNK) for hh in heads]
        ds_cur = [ds_ref[hh] for hh in heads]
        for ci in reversed(range(cb)):
            rs = slice(ci * CHUNK, (ci + 1) * CHUNK)
            ds_ms = [d.astype(MXU_DTYPE) for d in ds_cur]
            s_olds = [s_all_ref[ci * GDN_WIDTH + hh * HEAD_DIM:ci * GDN_WIDTH + (hh + 1) * HEAD_DIM, :] for hh in heads]
            s_ms = [s.astype(MXU_DTYPE) for s in s_olds]
            do_ms = [do_ref[rs, hs].astype(MXU_DTYPE) for hs in hss]
            p_do = [_dot(p_ref[rs, cs], do_m, TN) for cs, do_m in zip(css, do_ms)]
            k_ds = [_dot(kd_ref[rs, hs], ds_m, NN) for hs, ds_m in zip(hss, ds_ms)]
            q_do = [_dot(qd_ref[rs, hs], do_m, TN) for hs, do_m in zip(hss, do_ms)]
            dqds = [_dot(do_m, s_m, NT) for do_m, s_m in zip(do_ms, s_ms)]
            dkds = [_dot(vn_ref[rs, hs], ds_m, NT) for hs, ds_m in zip(hss, ds_ms)]
            dps = [_dot(do_m, vn_ref[rs, hs], NT) for hs, do_m in zip(hss, do_ms)]
            dv_news = [a + b for a, b in zip(p_do, k_ds)]
            dvn_ms = [d.astype(MXU_DTYPE) for d in dv_news]
            w_dv = [_dot(w_ref[rs, hs], dvn_m, TN) for hs, dvn_m in zip(hss, dvn_ms)]
            dws = [_dot(dvn_m, s_m, NT) for dvn_m, s_m in zip(dvn_ms, s_ms)]
            dcd_tile = jnp.zeros((CHUNK, 128), F32)
            for hh in heads:
                dvn_ref[rs, hss[hh]] = dv_news[hh]
                dw_ref[rs, hss[hh]] = -dws[hh]
                dqd_ref[rs, hss[hh]] = dqds[hh]
                dkd_ref[rs, hss[hh]] = dkds[hh]
                dp_ref[rs, css[hh]] = jnp.where(causal, dps[hh], 0.0)
                dcd = jnp.sum(jnp.sum(s_olds[hh] * ds_cur[hh], axis=1, keepdims=True), axis=0, keepdims=True)
                dcd_tile = jnp.where(lane == GDN_HEADS + hh, dcd, dcd_tile)
                c_dec = cd_ref[ci * CHUNK:ci * CHUNK + 1, GDN_HEADS + hh:GDN_HEADS + hh + 1]
                ds_cur[hh] = c_dec * ds_cur[hh] + q_do[hh] - w_dv[hh]
            dcd_ref[rs, :] = dcd_tile
        for hh in heads:
            ds_ref[hh] = ds_cur[hh]

    def spec(width):
        return pl.BlockSpec((rows, width), lambda n: (n_steps - 1 - n, 0))

    hw, cw = GDN_WIDTH, GDN_HEADS * CHUNK
    return pl.pallas_call(
        body, name=name, grid=(n_steps,),
        in_specs=[spec(hw), spec(hw), spec(hw), spec(hw), spec(cw), spec(128),
                  pl.BlockSpec((cb * GDN_WIDTH, HEAD_DIM), lambda n: (n_steps - 1 - n, 0)), spec(hw)],
        out_specs=[spec(hw), spec(hw), spec(hw), spec(hw), spec(cw), spec(128)],
        out_shape=[jax.ShapeDtypeStruct((T, hw), F32)] * 4
        + [jax.ShapeDtypeStruct((T, cw), F32), jax.ShapeDtypeStruct((T, 128), F32)],
        scratch_shapes=[pltpu.VMEM((GDN_HEADS, HEAD_DIM, HEAD_DIM), F32)],
        compiler_params=_params("arbitrary"),
    )(do, w, qd, kd, p, cd, s_all, vn)


def _gdn_chunk_bwd(qkv, bg, tinv_all, u, w, dvn, dw, dqd, dkd, dp, dcd, name):
    T = qkv.shape[0]
    cb = _tile(T // CHUNK, CHUNKS_PER_STEP)
    rows = cb * CHUNK

    def body(qkv_ref, bg_ref, tinv_ref, u_ref, w_ref, dvn_ref, dw_ref, dqd_ref, dkd_ref, dp_ref, dcd_ref,
             dqkv_ref, dbg_ref):
        masks = _chunk_masks()
        causal, strict, _ = masks
        lane = lax.broadcasted_iota(jnp.int32, (CHUNK, 128), 1)
        row = lax.broadcasted_iota(jnp.int32, (CHUNK, 128), 0)
        heads = []
        for ci in range(cb):
            rs = slice(ci * CHUNK, (ci + 1) * CHUNK)
            bgv = bg_ref[rs, :]
            gc_all = _chunk_cumsum(bgv)
            gc_rows = _row_form(gc_all)
            for hh in range(GDN_HEADS):
                qs, ks, vs = _head_slices(hh)
                q, k = qkv_ref[rs, qs], qkv_ref[rs, ks]
                beta = bgv[:, hh:hh + 1]
                gc = gc_all[:, GDN_HEADS + hh:GDN_HEADS + hh + 1]
                decay, kb, eg = _head_elementwise(k, beta, gc, gc_rows[GDN_HEADS + hh:GDN_HEADS + hh + 1, :], causal)
                heads.append(dict(ci=ci, hh=hh, rs=rs, hs=slice(hh * HEAD_DIM, (hh + 1) * HEAD_DIM),
                                  cs=slice(hh * CHUNK, (hh + 1) * CHUNK), q=q, k=k, beta=beta, gc=gc,
                                  decay=decay, kb=kb, eg=eg))
        for hd in heads:
            hd["t"] = _split(tinv_ref[hd["rs"], hd["cs"]])
        for hd in heads:
            hd["kk"] = _mdot(hd["kb"], hd["k"], NT)
            hd["qk"] = _mdot(hd["q"], hd["k"], NT)
        for hd in heads:
            hd["dvb"] = _dot3(hd["t"], _split(dvn_ref[hd["rs"], hd["hs"]]), TN)
            hd["dkbeg"] = _dot3(hd["t"], _split(dw_ref[hd["rs"], hd["hs"]]), TN)
        for hd in heads:
            rs, hs = hd["rs"], hd["hs"]
            da = -(_mdot(hd["dvb"], u_ref[rs, hs], NT) + _mdot(hd["dkbeg"], w_ref[rs, hs], NT))
            dm = jnp.where(strict, da, 0.0)
            dp_ = dp_ref[rs, hd["cs"]]
            hd["dkk"] = dm * hd["decay"]
            hd["dqk"] = dp_ * hd["decay"]
            hd["e"] = (hd["dkk"] * hd["kk"] + hd["dqk"] * hd["qk"])
        for hd in heads:
            hd["dkb"] = _mdot(hd["dkk"], hd["k"], NN)
            hd["dk"] = _mdot(hd["dkk"], hd["kb"], TN) + _mdot(hd["dqk"], hd["q"], TN)
            hd["dq"] = _mdot(hd["dqk"], hd["k"], NN)
            onehot = (lane == GDN_HEADS + hd["hh"]).astype(jnp.bfloat16)
            e_hi, e_lo = _split(hd["e"])
            hd["col_sums"] = _dot(e_lo, onehot, TN) + _dot(e_hi, onehot, TN)
        tiles = {}
        for hd in heads:
            ci, hh, rs, hs = hd["ci"], hd["hh"], hd["rs"], hd["hs"]
            qs, ks, vs = _head_slices(hh)
            q, k, beta, gc, eg, kb = hd["q"], hd["k"], hd["beta"], hd["gc"], hd["eg"], hd["kb"]
            v = qkv_ref[rs, vs]
            dqd_, dkd_ = dqd_ref[rs, hs], dkd_ref[rs, hs]
            gl = gc[CHUNK - 1:CHUNK, :]
            ek = jnp.exp(gl - gc)
            dkb = hd["dkb"] + hd["dkbeg"] * eg
            deg = jnp.sum(dqd_ * q, axis=1, keepdims=True) + jnp.sum(hd["dkbeg"] * kb, axis=1, keepdims=True)
            dek = jnp.sum(dkd_ * k, axis=1, keepdims=True)
            dcd_ = dcd_ref[ci * CHUNK:ci * CHUNK + 1, GDN_HEADS + hh:GDN_HEADS + hh + 1]
            dgl = jnp.sum(dek * ek, axis=0, keepdims=True) + dcd_ * jnp.exp(gl)
            dgc = jnp.sum(hd["e"], axis=1, keepdims=True) + deg * eg - dek * ek
            dbeta_tile, dgc_tile = tiles.get(ci, (jnp.zeros((CHUNK, 128), F32), jnp.zeros((CHUNK, 128), F32)))
            dgc_tile += jnp.where(lane == GDN_HEADS + hh, dgc, 0.0) - hd["col_sums"]
            dgc_tile += jnp.where((lane == GDN_HEADS + hh) & (row == CHUNK - 1), dgl, 0.0)
            dbeta = jnp.sum(dkb * k, axis=1, keepdims=True) + jnp.sum(hd["dvb"] * v, axis=1, keepdims=True)
            dbeta_tile += jnp.where(lane == hh, dbeta, 0.0)
            tiles[ci] = (dbeta_tile, dgc_tile)
            dqkv_ref[rs, qs] = hd["dq"] + dqd_ * eg
            dqkv_ref[rs, ks] = hd["dk"] + dkd_ * ek + dkb * beta
            dqkv_ref[rs, vs] = hd["dvb"] * beta
        for ci in range(cb):
            dbeta_tile, dgc_tile = tiles[ci]
            dbg_ref[ci * CHUNK:(ci + 1) * CHUNK, :] = dbeta_tile + _chunk_cumsum(dgc_tile, reverse=True)

    def spec(width):
        return pl.BlockSpec((rows, width), lambda n: (n, 0))

    hw, cw = GDN_WIDTH, GDN_HEADS * CHUNK
    return pl.pallas_call(
        body, name=name, grid=(T // rows,),
        in_specs=[spec(QKV_WIDTH), spec(128), spec(cw), spec(hw), spec(hw), spec(hw), spec(hw), spec(hw),
                  spec(hw), spec(cw), spec(128)],
        out_specs=[spec(QKV_WIDTH), spec(128)],
        out_shape=[jax.ShapeDtypeStruct((T, QKV_WIDTH), F32), jax.ShapeDtypeStruct((T, 128), F32)],
        compiler_params=_params("parallel"),
    )(qkv, bg, tinv_all, u, w, dvn, dw, dqd, dkd, dp, dcd)


def _pool_counts(i, tm, rows, offset):
    t = i * tm - offset + lax.broadcasted_iota(jnp.int32, (rows, 1), 0)
    return [jnp.minimum(t + 1, w).astype(F32) for w in POOL_WINDOWS]


def _window_sums(window, forward):
    sums, s, step = [], window, 1
    for _ in POOL_WINDOWS:
        s = s + _shift_rows(s, -step if forward else step)
        sums.append(s)
        step *= 2
    return sums


def _pooled(window, counts):
    sums = _window_sums(window, forward=False)
    out = []
    for gi in range(POOL_GROUPS):
        sl = slice(gi * 128, (gi + 1) * 128)
        out.append(sums[gi][HALO:, sl] / counts[gi] - window[HALO:, sl])
    return out


def _mix_post(o, proj, gdn_norm, pool_w, pool_scale, name):
    T = o.shape[0]
    tm = _tile(T, 256)
    hb = tm // HALO

    def body(o_ref, z_ref, p_ref, ph_ref, gn_ref, pw_ref, ps_ref, out_ref):
        i = pl.program_id(0)
        for hh in range(GDN_HEADS):
            sl = slice(hh * HEAD_DIM, (hh + 1) * HEAD_DIM)
            oh, zh = o_ref[:, sl], z_ref[:, sl]
            ro = lax.rsqrt(jnp.mean(oh * oh, axis=-1, keepdims=True) + EPS)
            out_ref[:, sl] = (((oh * ro) * gn_ref[...]) * (zh * _sigmoid(zh))).astype(out_ref.dtype)
        halo = jnp.where(i == 0, 0.0, ph_ref[...])
        window = jnp.concatenate([halo, p_ref[...]], axis=0)
        pooled = _pooled(window, _pool_counts(i, tm, tm, 0))
        for gi in range(POOL_GROUPS):
            pm = _mdot(pooled[gi], pw_ref[gi], NN)
            out_ref[:, GDN_WIDTH + gi * 128:GDN_WIDTH + (gi + 1) * 128] = (
                pm * ps_ref[:, gi * 128:(gi + 1) * 128]).astype(out_ref.dtype)

    return pl.pallas_call(
        body, name=name, grid=(T // tm,),
        in_specs=[pl.BlockSpec((tm, GDN_WIDTH), lambda i: (i, 0)),
                  pl.BlockSpec((tm, GDN_WIDTH), lambda i: (i, COL_Z // GDN_WIDTH)),
                  pl.BlockSpec((tm, POOL_WIDTH), lambda i: (i, COL_P // POOL_WIDTH)),
                  pl.BlockSpec((HALO, POOL_WIDTH), lambda i: (jnp.maximum(i * hb - 1, 0), COL_P // POOL_WIDTH)),
                  pl.BlockSpec((1, HEAD_DIM), lambda i: (0, 0)),
                  pl.BlockSpec((POOL_GROUPS, 128, 128), lambda i: (0, 0, 0)),
                  pl.BlockSpec((1, POOL_WIDTH), lambda i: (0, 0))],
        out_specs=pl.BlockSpec((tm, GDN_WIDTH + POOL_WIDTH), lambda i: (i, 0)),
        out_shape=jax.ShapeDtypeStruct((T, GDN_WIDTH + POOL_WIDTH), MXU_DTYPE),
        compiler_params=_params("parallel"),
    )(o, proj, proj, proj, gdn_norm, pool_w, pool_scale)


def _mix_post_bwd(dmix, o, proj, gdn_norm, pool_w, pool_scale, name):
    T = o.shape[0]
    tm = _tile(T, 256)
    hb = tm // HALO
    n_tiles = T // tm

    def body(dg_ref, dpo_ref, dpo_next_ref, o_ref, z_ref, p_ref, ph_ref, gn_ref, pw_ref, ps_ref,
             do_ref, dzp_ref, dgn_ref, dpw_ref, dps_ref):
        i = pl.program_id(0)

        @pl.when(i == 0)
        def _():
            dgn_ref[...] = jnp.zeros_like(dgn_ref)
            dpw_ref[...] = jnp.zeros_like(dpw_ref)
            dps_ref[...] = jnp.zeros_like(dps_ref)

        gn = gn_ref[...]
        dgn = jnp.zeros((1, HEAD_DIM), F32)
        for hh in range(GDN_HEADS):
            sl = slice(hh * HEAD_DIM, (hh + 1) * HEAD_DIM)
            oh, zh, dy = o_ref[:, sl], z_ref[:, sl], dg_ref[:, sl]
            ro = lax.rsqrt(jnp.mean(oh * oh, axis=-1, keepdims=True) + EPS)
            on = oh * ro
            sig = _sigmoid(zh)
            sz = zh * sig
            dzp_ref[:, sl] = (dy * (on * gn) * (sig * (1.0 + zh * (1.0 - sig)))).astype(dzp_ref.dtype)
            dgn += jnp.sum(dy * on * sz, axis=0, keepdims=True)
            don = dy * gn * sz
            do_ref[:, sl] = ro * (don - on * jnp.mean(don * on, axis=-1, keepdims=True))
        dgn_ref[...] += dgn

        halo = jnp.where(i == 0, 0.0, ph_ref[...])
        window = jnp.concatenate([halo, p_ref[...]], axis=0)
        counts = _pool_counts(i, tm, tm + HALO, 0)
        pooled = _pooled(window, [cn[:tm] for cn in counts])
        nxt = jnp.where(i == n_tiles - 1, 0.0, dpo_next_ref[...])
        dpo_w = jnp.concatenate([dpo_ref[...], nxt], axis=0)
        ps = ps_ref[...]
        dps = []
        scaled = []
        for gi in range(POOL_GROUPS):
            sl = slice(gi * 128, (gi + 1) * 128)
            dpm = dpo_w[:, sl] * ps[:, sl]
            pm = _mdot(pooled[gi], pw_ref[gi], NN)
            dps.append(jnp.sum(dpo_w[:tm, sl] * pm, axis=0, keepdims=True))
            dpw_ref[gi] += _mdot(pooled[gi], dpm[:tm], TN)
            dpooled = _mdot(dpm, pw_ref[gi], NT)
            scaled.append((dpooled, dpooled / counts[gi]))
        dps_ref[...] += jnp.concatenate(dps, axis=1)
        lead = _window_sums(jnp.concatenate([sc for _, sc in scaled], axis=1), forward=True)
        for gi in range(POOL_GROUPS):
            sl = slice(gi * 128, (gi + 1) * 128)
            dzp_ref[:, GDN_WIDTH + gi * 128:GDN_WIDTH + (gi + 1) * 128] = (
                lead[gi][:tm, sl] - scaled[gi][0][:tm]).astype(dzp_ref.dtype)

    last_halo = T // HALO - 1
    return pl.pallas_call(
        body, name=name, grid=(n_tiles,),
        in_specs=[pl.BlockSpec((tm, GDN_WIDTH), lambda i: (i, 0)),
                  pl.BlockSpec((tm, POOL_WIDTH), lambda i: (i, 1)),
                  pl.BlockSpec((HALO, POOL_WIDTH), lambda i: (jnp.minimum((i + 1) * hb, last_halo), 1)),
                  pl.BlockSpec((tm, GDN_WIDTH), lambda i: (i, 0)),
                  pl.BlockSpec((tm, GDN_WIDTH), lambda i: (i, COL_Z // GDN_WIDTH)),
                  pl.BlockSpec((tm, POOL_WIDTH), lambda i: (i, COL_P // POOL_WIDTH)),
                  pl.BlockSpec((HALO, POOL_WIDTH), lambda i: (jnp.maximum(i * hb - 1, 0), COL_P // POOL_WIDTH)),
                  pl.BlockSpec((1, HEAD_DIM), lambda i: (0, 0)),
                  pl.BlockSpec((POOL_GROUPS, 128, 128), lambda i: (0, 0, 0)),
                  pl.BlockSpec((1, POOL_WIDTH), lambda i: (0, 0))],
        out_specs=[pl.BlockSpec((tm, GDN_WIDTH), lambda i: (i, 0)),
                   pl.BlockSpec((tm, GDN_WIDTH + POOL_WIDTH), lambda i: (i, 0)),
                   pl.BlockSpec((1, HEAD_DIM), lambda i: (0, 0)),
                   pl.BlockSpec((POOL_GROUPS, 128, 128), lambda i: (0, 0, 0)),
                   pl.BlockSpec((1, POOL_WIDTH), lambda i: (0, 0))],
        out_shape=[jax.ShapeDtypeStruct((T, GDN_WIDTH), F32),
                   jax.ShapeDtypeStruct((T, GDN_WIDTH + POOL_WIDTH), MXU_DTYPE),
                   jax.ShapeDtypeStruct((1, HEAD_DIM), F32),
                   jax.ShapeDtypeStruct((POOL_GROUPS, 128, 128), F32),
                   jax.ShapeDtypeStruct((1, POOL_WIDTH), F32)],
        compiler_params=_params("arbitrary"),
    )(dmix, dmix, dmix, o, proj, proj, proj, gdn_norm, pool_w, pool_scale)


def _gdn_prep_bwd(proj, conv_w, a_log_l, dt_bias_l, dqkv, dbg, dzp, name):
    T = proj.shape[0]
    tm = _tile(T, 256)
    hb = tm // 8
    n_tiles = T // tm
    last_halo = T // 8 - 1

    def body(cur_ref, before_ref, after_ref, ba_ref, w_ref, al_ref, dtb_ref, dq_ref, dq_after_ref, dbg_ref,
             dzp_ref, dproj_ref, dw_ref, dal_ref, ddtb_ref):
        i = pl.program_id(0)

        @pl.when(i == 0)
        def _():
            dw_ref[...] = jnp.zeros_like(dw_ref)
            dal_ref[...] = jnp.zeros_like(dal_ref)
            ddtb_ref[...] = jnp.zeros_like(ddtb_ref)

        last = i == n_tiles - 1
        w = w_ref[...]
        before = jnp.where(i == 0, 0.0, before_ref[...])
        after = jnp.where(last, 0.0, after_ref[...])
        window = jnp.concatenate([before, cur_ref[...], after], axis=0)
        y = _conv_act(window, w)
        sig = _sigmoid(y)
        act = y * sig
        dq_w = jnp.concatenate([jnp.zeros((8, QKV_WIDTH), F32), dq_ref[...],
                                jnp.where(last, 0.0, dq_after_ref[...])], axis=0)
        dact = []
        for hh in range(3 * GDN_HEADS):
            sl = slice(hh * HEAD_DIM, (hh + 1) * HEAD_DIM)
            blk, dblk = act[:, sl], dq_w[:, sl]
            if hh < 2 * GDN_HEADS:
                rn = lax.rsqrt(jnp.sum(blk * blk, axis=-1, keepdims=True) + EPS)
                unit = blk * rn
                if hh < GDN_HEADS:
                    dblk = dblk * (HEAD_DIM ** -0.5)
                dblk = rn * (dblk - unit * jnp.sum(dblk * unit, axis=-1, keepdims=True))
            dact.append(dblk)
        dy = jnp.concatenate(dact, axis=1) * (sig * (1.0 + y * (1.0 - sig)))
        dx = dy * w[CONV_K - 1:CONV_K, :]
        dws = [None] * CONV_K
        dws[CONV_K - 1] = jnp.sum(dy[8:8 + tm] * window[8:8 + tm], axis=0, keepdims=True)
        for j in range(CONV_K - 1):
            s = CONV_K - 1 - j
            dx += _shift_rows(dy, -s) * w[j:j + 1, :]
            dws[j] = jnp.sum(dy[8:8 + tm] * _shift_rows(window, s)[8:8 + tm], axis=0, keepdims=True)
        dw_ref[...] += jnp.concatenate(dws, axis=0)
        dproj_ref[:, :QKV_WIDTH] = dx[8:8 + tm].astype(dproj_ref.dtype)
        dproj_ref[:, COL_Z:COL_BA] = dzp_ref[...]

        ba = ba_ref[...]
        dbg_ = dbg_ref[...]
        lane = lax.broadcasted_iota(jnp.int32, ba.shape, 1)
        beta = _sigmoid(ba)
        pre = ba + dtb_ref[...]
        neg_a = -jnp.exp(al_ref[...])
        g = neg_a * _softplus(pre)
        is_g = (lane >= GDN_HEADS) & (lane < 2 * GDN_HEADS)
        da_raw = jnp.where(is_g, dbg_ * neg_a * _sigmoid(pre), 0.0)
        dba = jnp.where(lane < GDN_HEADS, dbg_ * beta * (1.0 - beta), da_raw)
        dproj_ref[:, COL_BA:] = dba.astype(dproj_ref.dtype)
        dal_ref[...] += jnp.sum(jnp.where(is_g, dbg_ * g, 0.0), axis=0, keepdims=True)
        ddtb_ref[...] += jnp.sum(da_raw, axis=0, keepdims=True)

    lane_vec = pl.BlockSpec((1, 128), lambda i: (0, 0))
    return pl.pallas_call(
        body, name=name, grid=(n_tiles,),
        in_specs=[pl.BlockSpec((tm, QKV_WIDTH), lambda i: (i, 0)),
                  pl.BlockSpec((8, QKV_WIDTH), lambda i: (jnp.maximum(i * hb - 1, 0), 0)),
                  pl.BlockSpec((8, QKV_WIDTH), lambda i: (jnp.minimum((i + 1) * hb, last_halo), 0)),
                  pl.BlockSpec((tm, 128), lambda i: (i, COL_BA // 128)),
                  pl.BlockSpec((CONV_K, QKV_WIDTH), lambda i: (0, 0)), lane_vec, lane_vec,
                  pl.BlockSpec((tm, QKV_WIDTH), lambda i: (i, 0)),
                  pl.BlockSpec((8, QKV_WIDTH), lambda i: (jnp.minimum((i + 1) * hb, last_halo), 0)),
                  pl.BlockSpec((tm, 128), lambda i: (i, 0)),
                  pl.BlockSpec((tm, GDN_WIDTH + POOL_WIDTH), lambda i: (i, 0))],
        out_specs=[pl.BlockSpec((tm, D_IN_PAD), lambda i: (i, 0)),
                   pl.BlockSpec((CONV_K, QKV_WIDTH), lambda i: (0, 0)), lane_vec, lane_vec],
        out_shape=[jax.ShapeDtypeStruct((T, D_IN_PAD), MXU_DTYPE),
                   jax.ShapeDtypeStruct((CONV_K, QKV_WIDTH), F32),
                   jax.ShapeDtypeStruct((1, 128), F32), jax.ShapeDtypeStruct((1, 128), F32)],
        compiler_params=_params("arbitrary"),
    )(proj, proj, proj, proj, conv_w, a_log_l, dt_bias_l, dqkv, dqkv, dbg, dzp)


def _mod_part(c_all, w_ada, b_part, name):
    def body(c_ref, w_ref, b_ref, out_ref):
        cc = c_ref[...]
        out_ref[...] = _mdot(cc * _sigmoid(cc), w_ref[...], NN) + b_ref[...]

    return pl.pallas_call(
        body, name=name, out_shape=jax.ShapeDtypeStruct((c_all.shape[0], w_ada.shape[1]), F32),
        compiler_params=_params(),
    )(c_all, w_ada, b_part)


def _w_ada_grad(c_all, dmod_part, name):
    def body(c_ref, d_ref, out_ref):
        cc = c_ref[...]
        out_ref[...] = _mdot(cc * _sigmoid(cc), d_ref[...], TN)

    return pl.pallas_call(
        body, name=name, out_shape=jax.ShapeDtypeStruct((c_all.shape[1], dmod_part.shape[1]), F32),
        compiler_params=_params(),
    )(c_all, dmod_part)


def _sum_parts(parts, name):
    _, R, C = parts.shape
    tr = max([t for t in range(16, min(R, 512) + 1, 16) if R % t == 0], default=R)

    def body(p_ref, out_ref):
        acc = p_ref[0].astype(F32)
        for s in range(1, N_DEV):
            acc += p_ref[s].astype(F32)
        out_ref[...] = acc

    return pl.pallas_call(
        body, name=name, grid=(R // tr,),
        in_specs=[pl.BlockSpec((N_DEV, tr, C), lambda i: (0, i, 0))],
        out_specs=pl.BlockSpec((tr, C), lambda i: (i, 0)),
        out_shape=jax.ShapeDtypeStruct((R, C), F32),
        compiler_params=_params("parallel"),
    )(parts)


def _adamw_math(w, g, m, v):
    mm = ADAM_B1 * m + (1.0 - ADAM_B1) * g
    vv = ADAM_B2 * v + (1.0 - ADAM_B2) * (g * g)
    m_hat = mm / (1.0 - ADAM_B1 ** ADAM_STEP)
    v_hat = vv / (1.0 - ADAM_B2 ** ADAM_STEP)
    return -ADAM_LR * (m_hat / (jnp.sqrt(v_hat) + ADAM_EPS) + ADAM_WD * w), mm, vv


def _adamw_small(ws, gs, ms, vs, name):
    n = len(ws)

    def body(*refs):
        for i in range(n):
            d, mm, vv = _adamw_math(*[refs[k * n + i][...] for k in range(4)])
            refs[4 * n + i][...] = d
            refs[5 * n + i][...] = mm
            refs[6 * n + i][...] = vv

    out = pl.pallas_call(
        body, name=name, out_shape=[jax.ShapeDtypeStruct(w.shape, F32) for w in ws] * 3,
        compiler_params=_params(),
    )(*ws, *gs, *ms, *vs)
    return out[:n], out[n:2 * n], out[2 * n:]


def _adamw_sum(parts, w, m, v, name):
    R, C = w.shape
    tr = max([t for t in range(16, min(R, 512) + 1, 16) if R % t == 0], default=R)

    def body(p_ref, w_ref, m_ref, v_ref, g_ref, d_ref, mo_ref, vo_ref):
        g = p_ref[0].astype(F32)
        for s in range(1, N_DEV):
            g += p_ref[s].astype(F32)
        g_ref[...] = g
        d_ref[...], mo_ref[...], vo_ref[...] = _adamw_math(w_ref[...], g, m_ref[...], v_ref[...])

    spec = pl.BlockSpec((tr, C), lambda i: (i, 0))
    return pl.pallas_call(
        body, name=name, grid=(R // tr,),
        in_specs=[pl.BlockSpec((N_DEV, tr, C), lambda i: (0, i, 0)), spec, spec, spec], out_specs=[spec] * 4,
        out_shape=[jax.ShapeDtypeStruct((R, C), F32)] * 4,
        compiler_params=_params("parallel"),
    )(parts, w, m, v)


def _adamw(w, g, m, v, name):
    R, C = w.shape
    tr = max([t for t in range(8, min(R, 512) + 1, 8) if R % t == 0], default=R)

    def body(w_ref, g_ref, m_ref, v_ref, d_ref, mo_ref, vo_ref):
        d_ref[...], mo_ref[...], vo_ref[...] = _adamw_math(w_ref[...], g_ref[...], m_ref[...], v_ref[...])

    spec = pl.BlockSpec((tr, C), lambda i: (i, 0))
    return pl.pallas_call(
        body, name=name, grid=(R // tr,),
        in_specs=[spec] * 4, out_specs=[spec] * 3,
        out_shape=[jax.ShapeDtypeStruct((R, C), F32)] * 3,
        compiler_params=_params("parallel"),
    )(w, g, m, v)


def _weight_grad(a, b, name, dep=None):
    return _matmul([(a, b)], TN, WIRE_DTYPE, name, tm=1408, tn=1024, tk=1024, dep=dep)


def _rows_of(flat, lanes=1024):
    flat = flat.reshape(-1)
    n = -(-flat.shape[0] // lanes) * lanes
    return jnp.pad(flat, (0, n - flat.shape[0])).reshape(n // lanes, lanes)


def _pad_rows(a, rows):
    return jnp.pad(a, ((0, rows - a.shape[0]), (0, 0)))


def kernel(x, c, w_ada, b_ada, norm_ffn1, ffn1_gate, ffn1_up, ffn1_down, norm_mix, w_in, conv_w, a_log, dt_bias, gdn_norm, pool_w, pool_scale, w_out, norm_ffn2, ffn2_gate, ffn2_up, ffn2_down, final_norm, loss_target, m_w_ada, m_b_ada, m_norm_ffn1, m_ffn1_gate, m_ffn1_up, m_ffn1_down, m_norm_mix, m_w_in, m_conv_w, m_a_log, m_dt_bias, m_gdn_norm, m_pool_w, m_pool_scale, m_w_out, m_norm_ffn2, m_ffn2_gate, m_ffn2_up, m_ffn2_down, m_final_norm, v_w_ada, v_b_ada, v_norm_ffn1, v_ffn1_gate, v_ffn1_up, v_ffn1_down, v_norm_mix, v_w_in, v_conv_w, v_a_log, v_dt_bias, v_gdn_norm, v_pool_w, v_pool_scale, v_w_out, v_norm_ffn2, v_ffn2_gate, v_ffn2_up, v_ffn2_down, v_final_norm):
    T, D = x.shape[1], x.shape[2]
    Fs = ffn1_gate.shape[2]
    Ws = w_in.shape[2]
    Ws_pad = -(-Ws // 16) * 16
    Os = w_out.shape[1]
    Ms = w_ada.shape[2]
    Cs = conv_w.shape[2]
    me = 4 * lax.axis_index("x") + 2 * lax.axis_index("y") + lax.axis_index("c")
    x0, target = x[0], loss_target[0]

    def wire(a):
        return a.astype(WIRE_DTYPE)

    def token(started):
        return started[4][:1, :1]

    def with_own(landed, own):
        return lax.dynamic_update_slice(landed, own[None], (me, 0, 0))

    def full(landed):
        return landed.reshape(-1, D).astype(MXU_DTYPE)

    no_dep = jnp.zeros((8, 128), F32)
    small = jnp.concatenate([_pad_rows(c, 8), _pad_rows(jnp.pad(conv_w[0], ((0, 0), (0, D - Cs))), 8)], axis=0)
    got, = _all_gather([small], "gather_small")
    c_all = got[:, 0, :]
    conv_full = jnp.transpose(got[:, 8:8 + CONV_K, :Cs], (1, 0, 2)).reshape(CONV_K, QKV_WIDTH)
    b_part = lax.dynamic_slice(b_ada, (0, me * Ms), (1, Ms))
    mod_part = _mod_part(c_all, w_ada[0], b_part, "mod_part")

    w1 = [wire(ffn1_gate[0].T)]
    w1u = [wire(ffn1_up[0].T)]
    w1d = [wire(ffn1_down[0])]
    w2 = [wire(_pad_rows(w_in[0].T, Ws_pad)), wire(w_out[0])]
    w3 = [wire(ffn2_gate[0].T), wire(ffn2_up[0].T), wire(ffn2_down[0])]
    mod_parts, *w1_all = _all_gather([mod_part] + w1, "gather_mod_w1")
    mod_all = jnp.transpose(mod_parts, (1, 0, 2)).reshape(N_DEV, N_MOD * D)
    mod = lax.dynamic_slice(mod_all, (me, 0), (1, N_MOD * D)).reshape(N_MOD, 1, D)
    sh1, sc1, gt1, sh2, sc2, gt2, sh3, sc3, gt3 = [mod[i] for i in range(N_MOD)]
    wg1_t = full(w1_all[0])
    w1u_sent = _exchange_start(w1u, True, w1_all[0], "w1u_start")
    w1d_sent = _exchange_start(w1d, True, w1u_sent[4], "w1d_start")
    w2_sent = _exchange_start(w2, True, w1d_sent[4], "w2_start")
    w3_sent = _exchange_start(w3, True, w2_sent[4], "w3_start")

    lane_pad = lambda a: jnp.pad(a, ((0, 0), (GDN_HEADS, 128 - 2 * GDN_HEADS)))
    a_log_l, dt_bias_l = lane_pad(a_log), lane_pad(dt_bias)
    pool_w_m = pool_w[0].astype(MXU_DTYPE)

    g1, h1 = _swiglu_gate(x0, wg1_t, norm_ffn1, sc1 + token(w3_sent), sh1, "ffn1_gate")
    wu1_t = full(with_own(_exchange_wait(w1u_sent, h1, True, "w1u_wait")[0], w1u[0]))
    u1, a1 = _swiglu_up(h1, wu1_t, g1, "ffn1_up")
    wd1 = full(with_own(_exchange_wait(w1d_sent, a1, True, "w1d_wait")[0], w1d[0]))
    y1, x1, h2 = _matmul_resid_norm_mod(a1, wd1, x0, gt1, 0.5, norm_mix, sc2, sh2, "ffn1_down")
    w_in_all, wo_all = [with_own(z, own) for z, own in zip(_exchange_wait(w2_sent, h2, True, "w2_wait"), w2)]
    w_in_t = w_in_all[:, :Ws, :].reshape(-1, D).astype(MXU_DTYPE)
    wo = full(wo_all)
    w_in_re = jnp.concatenate([w_in_t[:COL_Z + GDN_WIDTH], w_in_t[D_IN - POOL_WIDTH:],
                               w_in_t[4 * GDN_WIDTH:4 * GDN_WIDTH + 2 * GDN_HEADS],
                               jnp.zeros((128 - 2 * GDN_HEADS, D), MXU_DTYPE)], axis=0)
    proj = _matmul([(h2, w_in_re)], NT, F32, "proj_in", tm=512, tn=D_IN_PAD, tk=D)
    qkv, bg = _gdn_prep(proj, conv_full, a_log_l, dt_bias_l, "gdn_prep")
    tinv, u_c, w_c, qd_c, kd_c, p_c, cd_c = _gdn_chunk_fwd(qkv, bg, "gdn_chunk_fwd")
    o, s_all, vn_c = _gdn_scan_fwd(u_c, w_c, qd_c, kd_c, p_c, cd_c, "gdn_scan_fwd")
    mix_in = _mix_post(o, proj, gdn_norm, pool_w_m, pool_scale, "mix_post")
    mixed, x2, h3 = _matmul_resid_norm_mod(mix_in, wo, x1, gt2, 1.0, norm_ffn2, sc3, sh3, "mix_out")
    wg2_t, wu2_t, wd2 = [full(with_own(z, own))
                         for z, own in zip(_exchange_wait(w3_sent, h3, True, "w3_wait"), w3)]
    g3, u3, a3, loss_row, d3, d_final, dy3, dgt3 = _swiglu_fwd_loss(
        h3, wg2_t, wu2_t, wd2, x2, gt3, final_norm.reshape(1, D), target, "ffn2_fwd_loss")

    dg3, du3, d2, d_n3, dsc3, dsh3, dmixed, dgt2 = _swiglu_bwd(
        dy3, wd2, g3, u3, wg2_t, wu2_t, "ffn2_bwd_norm3_bwd", (x2, norm_ffn2, sc3), d3,
        produced_by=(mixed, gt2, 1.0))
    d_wd2 = _weight_grad(a3, dy3, "ffn2_dwd")
    d_wg2 = _weight_grad(dg3, h3, "ffn2_dwg")
    d_wu2 = _weight_grad(du3, h3, "ffn2_dwu")
    dmix_in = _matmul([(dmixed, wo)], NT, F32, "mix_out_bwd", tm=512, tn=GDN_WIDTH + POOL_WIDTH, tk=D)
    d_wo = _matmul([(mix_in, dmixed)], TN, WIRE_DTYPE, "mix_dwo", tm=GDN_WIDTH + POOL_WIDTH, tn=D, tk=1024)
    do, dzp, d_gn, d_pw, d_ps = _mix_post_bwd(dmix_in, o, proj, gdn_norm, pool_w_m, pool_scale, "mix_post_bwd")
    dvn, dw_c, dqd, dkd, dp_c, dcd = _gdn_scan_bwd(do, w_c, qd_c, kd_c, p_c, cd_c, s_all, vn_c, "gdn_scan_bwd")
    dqkv, dbg = _gdn_chunk_bwd(qkv, bg, tinv, u_c, w_c, dvn, dw_c, dqd, dkd, dp_c, dcd, "gdn_chunk_bwd")
    dproj, d_conv, d_al, d_dtb = _gdn_prep_bwd(proj, conv_full, a_log_l, dt_bias_l, dqkv, dbg, dzp, "gdn_prep_bwd")
    d_win_re = _matmul([(dproj, h2)], TN, WIRE_DTYPE, "proj_in_dw", tm=D_IN_PAD, tn=D, tk=1024)
    d_win_t = jnp.concatenate([d_win_re[:COL_Z + GDN_WIDTH], d_win_re[COL_BA:COL_BA + 2 * GDN_HEADS],
                               d_win_re[COL_P:COL_P + POOL_WIDTH]], axis=0)
    d_win_blocks = jnp.pad(d_win_t.reshape(N_DEV, Ws, D), ((0, 0), (0, Ws_pad - Ws), (0, 0)))
    names23 = ["w_in", "w_out", "ffn2_gate", "ffn2_up", "ffn2_down"]
    parts23 = [wire(d_win_blocks), d_wo.reshape(N_DEV, Os, D), d_wg2.reshape(N_DEV, Fs, D),
               d_wu2.reshape(N_DEV, Fs, D), d_wd2.reshape(N_DEV, Fs, D)]
    own23 = [lax.dynamic_index_in_dim(p, me, 0, keepdims=False) for p in parts23]
    g23_sent = _exchange_start(parts23, False, no_dep, "g23_start")
    d1, d_n2, dsc2, dsh2, dy1, dgt1 = _norm_bwd((dproj, w_in_re), x1, norm_mix, sc2 + token(g23_sent), d2,
                                                "proj_in_bwd_norm2_bwd", produced_by=(y1, gt1, 0.5))
    dg1, du1, grad_x, d_n1, dsc1, dsh1 = _swiglu_bwd(
        dy1, wd1, g1, u1, wg1_t, wu1_t, "ffn1_bwd_norm1_bwd", (x0, norm_ffn1, sc1), d1)

    dmod = jnp.concatenate([dsh1, dsc1, dgt1, dsh2, dsc2, dgt2, dsh3, dsc3, dgt3], axis=0)
    small_rows = [dmod.reshape(-1), d_n1[0], d_n2[0], d_n3[0], d_final[0], d_gn[0], d_ps[0],
                  d_al[0, GDN_HEADS:2 * GDN_HEADS], d_dtb[0, GDN_HEADS:2 * GDN_HEADS], loss_row[0, :1],
                  d_conv.reshape(-1), d_pw.reshape(-1)]
    lanes = 1024
    small_rows = [_rows_of(r, lanes) for r in small_rows]
    n_rows = [r.shape[0] for r in small_rows]
    row_off = [sum(n_rows[:i]) for i in range(len(n_rows))]
    total = -(-sum(n_rows) // 8) * 8
    slab = _pad_rows(jnp.concatenate(small_rows, axis=0), total)
    slab_sent = _exchange_start([slab], True, no_dep, "small_grads_start")

    def send_ffn1(a, b, which, dep):
        parts = _weight_grad(a, b, f"ffn1_{which}", dep=dep).reshape(N_DEV, Fs, D)
        own = lax.dynamic_index_in_dim(parts, me, 0, keepdims=False)
        return _exchange_start([parts], False, no_dep, f"g1_{which}_start"), own

    g1_wg, own_wg = send_ffn1(dg1, h1, "dwg", slab_sent[4])
    g1_wu, own_wu = send_ffn1(du1, h1, "dwu", g1_wg[4])
    g1_wd, own_wd = send_ffn1(a1, dy1, "dwd", g1_wu[4])

    slab_all = with_own(_exchange_wait(slab_sent, g1_wg[4], True, "small_grads_wait")[0], slab)
    summed = _sum_parts(slab_all, "sum_small_grads")

    def piece(idx, n):
        return summed[row_off[idx]:row_off[idx] + n_rows[idx]].reshape(-1)[:n]

    g_b_ada = piece(0, N_MOD * D).reshape(1, N_MOD * D)
    g_n1, g_n2, g_n3 = piece(1, D).reshape(1, D), piece(2, D).reshape(1, D), piece(3, D).reshape(1, D)
    g_final = piece(4, D)
    g_gn = piece(5, HEAD_DIM).reshape(1, HEAD_DIM)
    g_ps = piece(6, POOL_WIDTH).reshape(1, POOL_WIDTH)
    g_al = piece(7, GDN_HEADS).reshape(1, GDN_HEADS)
    g_dtb = piece(8, GDN_HEADS).reshape(1, GDN_HEADS)
    loss = piece(9, 1)[0]
    g_conv = lax.dynamic_slice(piece(10, CONV_K * QKV_WIDTH).reshape(1, CONV_K, QKV_WIDTH), (0, 0, me * Cs),
                               (1, CONV_K, Cs))
    g_pw = piece(11, POOL_GROUPS * 128 * 128).reshape(1, POOL_GROUPS, 128, 128)

    dmod_all = slab_all[:, row_off[0]:row_off[0] + n_rows[0], :].reshape(N_DEV, -1)[:, :N_MOD * D]
    g_w_ada = _w_ada_grad(c_all, lax.dynamic_slice(dmod_all, (0, me * Ms), (N_DEV, Ms)), "w_ada_grad")[None]

    landed23 = _exchange_wait(g23_sent, g1_wd[4], False, "g23_wait")
    parts8 = {n: with_own(z, own) for n, z, own in zip(names23, landed23, own23)}
    g_rows = dict(w_in=_sum_parts(parts8.pop("w_in"), "sum_w_in")[:Ws])
    column_sharded = ("w_in", "ffn1_gate", "ffn1_up", "ffn2_gate", "ffn2_up")

    names = ["w_ada", "b_ada", "norm_ffn1", "ffn1_gate", "ffn1_up", "ffn1_down", "norm_mix", "w_in", "conv_w",
             "a_log", "dt_bias", "gdn_norm", "pool_w", "pool_scale", "w_out", "norm_ffn2", "ffn2_gate", "ffn2_up",
             "ffn2_down", "final_norm"]
    weights = dict(zip(names, [w_ada, b_ada, norm_ffn1, ffn1_gate, ffn1_up, ffn1_down, norm_mix, w_in, conv_w,
                               a_log, dt_bias, gdn_norm, pool_w, pool_scale, w_out, norm_ffn2, ffn2_gate, ffn2_up,
                               ffn2_down, final_norm]))
    ms = dict(zip(names, [m_w_ada, m_b_ada, m_norm_ffn1, m_ffn1_gate, m_ffn1_up, m_ffn1_down, m_norm_mix, m_w_in,
                          m_conv_w, m_a_log, m_dt_bias, m_gdn_norm, m_pool_w, m_pool_scale, m_w_out, m_norm_ffn2,
                          m_ffn2_gate, m_ffn2_up, m_ffn2_down, m_final_norm]))
    vs = dict(zip(names, [v_w_ada, v_b_ada, v_norm_ffn1, v_ffn1_gate, v_ffn1_up, v_ffn1_down, v_norm_mix, v_w_in,
                          v_conv_w, v_a_log, v_dt_bias, v_gdn_norm, v_pool_w, v_pool_scale, v_w_out, v_norm_ffn2,
                          v_ffn2_gate, v_ffn2_up, v_ffn2_down, v_final_norm]))
    grads = dict(w_ada=g_w_ada, b_ada=g_b_ada, norm_ffn1=g_n1, norm_mix=g_n2, conv_w=g_conv,
                 a_log=g_al, dt_bias=g_dtb, gdn_norm=g_gn, pool_w=g_pw, pool_scale=g_ps,
                 norm_ffn2=g_n3, final_norm=g_final)
    delta, new_m, new_v = {}, {}, {}

    def adamw_big(n):
        if n in column_sharded:
            view, back = (lambda a: a[0].T), (lambda a: a.T[None])
        else:
            view, back = (lambda a: a[0]), (lambda a: a[None])
        if n in parts8:
            g, d_, m_, v_ = _adamw_sum(parts8[n], view(weights[n]), view(ms[n]), view(vs[n]), f"adamw_{n}")
        else:
            g = g_rows[n] if n in g_rows else view(grads[n])
            d_, m_, v_ = _adamw(view(weights[n]), g, view(ms[n]), view(vs[n]), f"adamw_{n}")
        grads[n], delta[n], new_m[n], new_v[n] = back(g), back(d_), back(m_), back(v_)

    early = ["w_ada", "w_in", "w_out", "ffn2_gate", "ffn2_up", "ffn2_down"]
    late = ["ffn1_gate", "ffn1_up", "ffn1_down"]
    for n in early:
        adamw_big(n)
    done = sum(delta[n][0, :1, :1] for n in early)

    def arrived(started, own, which):
        landed, = _exchange_wait(started, done, False, f"g1_{which}_wait")
        return with_own(landed, own)

    parts8["ffn1_gate"] = arrived(g1_wg, own_wg, "dwg")
    parts8["ffn1_up"] = arrived(g1_wu, own_wu, "dwu")
    parts8["ffn1_down"] = arrived(g1_wd, own_wd, "dwd")
    for n in late:
        adamw_big(n)
    small_names = [n for n in names if n not in early + late]
    two_d = lambda a: a.reshape(-1, a.shape[-1])
    small_out = _adamw_small(*[[two_d(src[n]) for n in small_names] for src in (weights, grads, ms, vs)],
                             "adamw_small")
    for dst, outs in zip((delta, new_m, new_v), small_out):
        for n, a in zip(small_names, outs):
            dst[n] = a.reshape(weights[n].shape)

    return (loss, grad_x[None], *[grads[n] for n in names], *[delta[n] for n in names],
            *[new_m[n] for n in names], *[new_v[n] for n in names])
```

```python
import functools

import jax
import jax.numpy as jnp
from jax import lax
from jax.experimental import pallas as pl
from jax.experimental.pallas import tpu as pltpu

F32 = jnp.float32
MXU_DTYPE = jnp.bfloat16
WIRE_DTYPE = jnp.bfloat16
EPS = 1e-6
N_DEV = 8
GDN_HEADS = 4
HEAD_DIM = 128
GDN_WIDTH = GDN_HEADS * HEAD_DIM
POOL_WINDOWS = (2, 4, 8, 16)
POOL_GROUPS = len(POOL_WINDOWS)
POOL_WIDTH = 512
CONV_K = 4
CHUNK = 64
QKV_WIDTH = 3 * GDN_WIDTH
D_IN = 4 * GDN_WIDTH + 2 * GDN_HEADS + POOL_WIDTH
D_IN_PAD = 4 * GDN_WIDTH + POOL_WIDTH + 128
COL_Z = QKV_WIDTH
COL_P = 4 * GDN_WIDTH
COL_BA = 4 * GDN_WIDTH + POOL_WIDTH
N_MOD = 9
HALO = 16
VMEM_LIMIT = 56 * 1024 * 1024
ADAM_LR, ADAM_B1, ADAM_B2, ADAM_EPS, ADAM_WD, ADAM_STEP = 0.001, 0.9, 0.999, 1e-08, 0.01, 10
FFN_TOKEN_TILE = 256
FFN_HIDDEN_TILE = 1408
CHUNKS_PER_STEP = 4
SCAN_CHUNKS_PER_STEP = 8

NT = (((1,), (1,)), ((), ()))
NN = (((1,), (0,)), ((), ()))
TN = (((0,), (0,)), ((), ()))


def _params(*sem):
    return pltpu.CompilerParams(dimension_semantics=tuple(sem), vmem_limit_bytes=VMEM_LIMIT)


def _dot(a, b, dims):
    return lax.dot_general(a, b, dims, preferred_element_type=F32)


def _mdot(a, b, dims):
    return _dot(a.astype(MXU_DTYPE), b.astype(MXU_DTYPE), dims)


def _split(a):
    hi = a.astype(jnp.bfloat16)
    return hi, (a - hi.astype(F32)).astype(jnp.bfloat16)


def _dot3(a, b, dims):
    (ah, al), (bh, bl) = a, b
    return (_dot(al, bh, dims) + _dot(ah, bl, dims)) + _dot(ah, bh, dims)


def _sigmoid(v):
    return 0.5 * jnp.tanh(0.5 * v) + 0.5


def _softplus(v):
    return jnp.maximum(v, 0.0) + jnp.log(1.0 + jnp.exp(-jnp.abs(v)))


def _shift_rows(v, s):
    n = v.shape[0]
    s = s % n
    return v if s == 0 else pltpu.roll(v, s, 0)


def _tile(n, want):
    t = min(n, want)
    while n % t:
        t //= 2
    return t


def _all_gather(blocks, name, dep=None):
    n = len(blocks)

    def body(*refs):
        x_refs, out_refs = refs[:n], refs[-3 - n:-3]
        send_sems, recv_sems, local_sems = refs[-3:]
        x, y, c = lax.axis_index("x"), lax.axis_index("y"), lax.axis_index("c")
        me, sibling = (x, y, c), (x, y, 1 - c)
        chips = [(1 - x, y), (x, 1 - y), (1 - x, 1 - y)]

        def copy(a, k, blk, to, own=False):
            rows = out_refs[a].at[4 * blk[0] + 2 * blk[1] + blk[2]]
            return pltpu.make_async_remote_copy(
                src_ref=x_refs[a] if own else rows, dst_ref=rows,
                send_sem=send_sems.at[7 * a + k], recv_sem=recv_sems.at[7 * a + k],
                device_id=to, device_id_type=pl.DeviceIdType.MESH)

        mine = [pltpu.make_async_copy(x_refs[a], out_refs[a].at[4 * x + 2 * y + c], local_sems.at[a])
                for a in range(n)]
        for cp in mine:
            cp.start()
        sent = []
        for a in range(n):
            sent.append(copy(a, 0, me, sibling, own=True))
            sent += [copy(a, 1 + j, me, (*chip, c), own=True) for j, chip in enumerate(chips)]
        for cp in sent:
            cp.start()
        for a in range(n):
            for j, chip in enumerate(chips):
                copy(a, 1 + j, (*chip, c), me).wait_recv()
                passed = copy(a, 4 + j, (*chip, c), sibling)
                passed.start()
                sent.append(passed)
        for a in range(n):
            copy(a, 0, sibling, me).wait_recv()
            for j, chip in enumerate(chips):
                copy(a, 4 + j, (*chip, 1 - c), me).wait_recv()
        for cp in sent:
            cp.wait_send()
        for cp in mine:
            cp.wait()

    hbm = pl.BlockSpec(memory_space=pltpu.HBM)
    return pl.pallas_call(
        body, name=name,
        out_shape=[jax.ShapeDtypeStruct((N_DEV,) + b.shape, b.dtype) for b in blocks],
        in_specs=[hbm] * n + [pl.BlockSpec(memory_space=pl.ANY)] * (dep is not None),
        out_specs=[hbm] * n,
        scratch_shapes=[pltpu.SemaphoreType.DMA((7 * n,)), pltpu.SemaphoreType.DMA((7 * n,)),
                        pltpu.SemaphoreType.DMA((n,))],
    )(*(list(blocks) + ([] if dep is None else [dep])))


_HBM = pl.BlockSpec(memory_space=pltpu.HBM)
_SEM = pl.BlockSpec(memory_space=pltpu.SEMAPHORE)
_ANY = pl.BlockSpec(memory_space=pl.ANY)
_EFFECT = pltpu.SideEffectType.DATAFLOW_SIDE_EFFECTING
_FLIPS = [(0, 0, 1), (0, 1, 0), (0, 1, 1), (1, 0, 0), (1, 0, 1), (1, 1, 0), (1, 1, 1)]


def _peers():
    x, y, c = lax.axis_index("x"), lax.axis_index("y"), lax.axis_index("c")
    return 4 * x + 2 * y + c, [(1 - x if fx else x, 1 - y if fy else y, 1 - c if fc else c)
                               for fx, fy, fc in _FLIPS]


def _exchange_start(srcs, gather, dep, name):
    n = len(srcs)
    lands = [(N_DEV,) + tuple(s.shape if gather else s.shape[1:]) for s in srcs]

    def body(*refs):
        src_refs, land_refs = refs[:n], refs[n:2 * n]
        send_sems, recv_sems = refs[2 * n + 1], refs[2 * n + 2]
        token = refs[-1]
        me, peers = _peers()
        for a in range(n):
            for k, (px, py, pc) in enumerate(peers):
                pltpu.make_async_remote_copy(
                    src_ref=src_refs[a] if gather else src_refs[a].at[4 * px + 2 * py + pc],
                    dst_ref=land_refs[a].at[me],
                    send_sem=send_sems.at[7 * a + k], recv_sem=recv_sems.at[7 * a + k],
                    device_id=(px, py, pc), device_id_type=pl.DeviceIdType.MESH).start()
        token[...] = jnp.zeros_like(token)

    srcs = [pltpu.with_memory_space_constraint(s, pltpu.HBM) for s in srcs]
    empties = [pltpu.with_memory_space_constraint(lax.empty(shape, s.dtype), pltpu.HBM)
               for shape, s in zip(lands, srcs)]
    out = pl.pallas_call(
        body, name=name,
        out_shape=(pltpu.SemaphoreType.DMA((7 * n,)), pltpu.SemaphoreType.DMA((7 * n,)),
                   *[pltpu.HBM(shape, s.dtype) for shape, s in zip(lands, srcs)],
                   jax.ShapeDtypeStruct((8, 128), F32)),
        in_specs=(*[_HBM] * (2 * n), _ANY),
        out_specs=(_SEM, _SEM, *[_HBM] * n, pl.BlockSpec(memory_space=pltpu.VMEM)),
        input_output_aliases={n + a: 2 + a for a in range(n)},
        compiler_params=pltpu.CompilerParams(has_side_effects=_EFFECT),
    )(*srcs, *empties, dep)
    return out[0], out[1], srcs, list(out[2:2 + n]), out[-1]


def _exchange_wait(started, after, gather, name):
    send_sems, recv_sems, srcs, lands, _ = started
    n = len(srcs)

    def body(*refs):
        src_refs, land_refs = refs[:n], refs[n:2 * n]
        send_sems, recv_sems = refs[2 * n], refs[2 * n + 1]
        _, peers = _peers()
        for a in range(n):
            for k, peer in enumerate(peers):
                copy = pltpu.make_async_remote_copy(
                    src_ref=src_refs[a] if gather else src_refs[a].at[0], dst_ref=land_refs[a].at[0],
                    send_sem=send_sems.at[7 * a + k], recv_sem=recv_sems.at[7 * a + k],
                    device_id=peer, device_id_type=pl.DeviceIdType.MESH)
                copy.wait_send()
                copy.wait_recv()

    out = pl.pallas_call(
        body, name=name,
        out_shape=[pltpu.HBM(z.shape, z.dtype) for z in lands],
        in_specs=(*[_HBM] * (2 * n), _SEM, _SEM, _ANY), out_specs=[_HBM] * n,
        input_output_aliases={n + a: a for a in range(n)},
        compiler_params=pltpu.CompilerParams(has_side_effects=_EFFECT),
    )(*srcs, *lands, send_sems, recv_sems, after)
    return list(out)


def _matmul(pairs, dims, out_dtype, name, tm=512, tn=512, tk=512, dep=None):
    a0, b0 = pairs[0]
    if dims == TN:
        K, M = a0.shape
    else:
        M, K = a0.shape
    N = b0.shape[0] if dims == NT else b0.shape[1]
    tm, tn, tk = _tile(M, tm), _tile(N, tn), _tile(K, tk)
    nk = K // tk
    n_pairs = len(pairs)
    n_in = 2 * n_pairs + (dep is not None)

    def body(*refs):
        out_ref = refs[n_in]

        def product():
            total = _dot(refs[0][...], refs[1][...], dims)
            for p in range(1, n_pairs):
                total += _dot(refs[2 * p][...], refs[2 * p + 1][...], dims)
            return total

        if nk == 1:
            out_ref[...] = product().astype(out_ref.dtype)
            return
        acc_ref = refs[n_in + 1]
        k = pl.program_id(2)

        @pl.when(k == 0)
        def _():
            acc_ref[...] = product()

        @pl.when((k > 0) & (k < nk - 1))
        def _():
            acc_ref[...] += product()

        @pl.when(k == nk - 1)
        def _():
            out_ref[...] = (acc_ref[...] + product()).astype(out_ref.dtype)

    if dims == TN:
        a_spec = pl.BlockSpec((tk, tm), lambda i, j, k: (k, i))
    else:
        a_spec = pl.BlockSpec((tm, tk), lambda i, j, k: (i, k))
    if dims == NT:
        b_spec = pl.BlockSpec((tn, tk), lambda i, j, k: (j, k))
    else:
        b_spec = pl.BlockSpec((tk, tn), lambda i, j, k: (k, j))
    args, specs = [], []
    for a, b in pairs:
        args += [a, b]
        specs += [a_spec, b_spec]
    if dep is not None:
        args.append(dep)
        specs.append(_ANY)
    return pl.pallas_call(
        body, name=name, grid=(M // tm, N // tn, nk),
        in_specs=specs, out_specs=pl.BlockSpec((tm, tn), lambda i, j, k: (i, j)),
        out_shape=jax.ShapeDtypeStruct((M, N), out_dtype),
        scratch_shapes=[pltpu.VMEM((tm, tn), F32)] * (nk > 1),
        compiler_params=_params("parallel", "parallel", "arbitrary"),
    )(*args)


def _vec_spec(d):
    return pl.BlockSpec((1, d), lambda i: (0, 0))


def _matmul_resid_norm_mod(a, b, x, gate, coef, nw, scale, shift, name):
    T, K = a.shape
    D = b.shape[1]
    tm = _tile(T, 512)

    def body(a_ref, b_ref, x_ref, g_ref, nw_ref, sc_ref, sh_ref, y_ref, xo_ref, h_ref):
        y = _dot(a_ref[...], b_ref[...], NN)
        y_ref[...] = y
        xf = x_ref[...] + (coef * g_ref[...]) * y
        xo_ref[...] = xf
        r = lax.rsqrt(jnp.mean(xf * xf, axis=-1, keepdims=True) + EPS)
        h_ref[...] = ((xf * r) * nw_ref[...] * (1.0 + sc_ref[...]) + sh_ref[...]).astype(h_ref.dtype)

    row = pl.BlockSpec((tm, D), lambda i: (i, 0))
    vec = _vec_spec(D)
    return pl.pallas_call(
        body, name=name, grid=(T // tm,),
        in_specs=[pl.BlockSpec((tm, K), lambda i: (i, 0)), _resident((K, D)), row, vec, vec, vec, vec],
        out_specs=[row, row, row],
        out_shape=[jax.ShapeDtypeStruct((T, D), F32), jax.ShapeDtypeStruct((T, D), F32),
                   jax.ShapeDtypeStruct((T, D), MXU_DTYPE)],
        compiler_params=_params("parallel"),
    )(a, b, x, gate, nw, scale, shift)


def _norm_bwd(dh, x, nw, scale, dres, name, produced_by=None):
    T, D = x.shape
    tm = _tile(T, 512)

    def body(*refs):
        if isinstance(dh, tuple):
            dh_value = _dot(refs[0][...], refs[1][...], NN)
            refs = refs[1:]
        else:
            dh_value = refs[0][...]
        _, x_ref, nw_ref, sc_ref, dr_ref = refs[:5]
        n_in = 5 if produced_by is None else 7
        dx_ref, dnw_ref, dsc_ref, dsh_ref = refs[n_in:n_in + 4]

        @pl.when(pl.program_id(0) == 0)
        def _():
            dnw_ref[...] = jnp.zeros_like(dnw_ref)
            dsc_ref[...] = jnp.zeros_like(dsc_ref)
            dsh_ref[...] = jnp.zeros_like(dsh_ref)
            if produced_by is not None:
                refs[n_in + 5][...] = jnp.zeros_like(refs[n_in + 5])

        xf, dh_ = x_ref[...], dh_value
        r = lax.rsqrt(jnp.mean(xf * xf, axis=-1, keepdims=True) + EPS)
        xn = xf * r
        one_sc = 1.0 + sc_ref[...]
        dsh_ref[...] += jnp.sum(dh_, axis=0, keepdims=True)
        t = dh_ * xn
        dsc_ref[...] += jnp.sum(t, axis=0, keepdims=True) * nw_ref[...]
        dnw_ref[...] += jnp.sum(t, axis=0, keepdims=True) * one_sc
        dxn = dh_ * (nw_ref[...] * one_sc)
        dx = dr_ref[...] + r * (dxn - xn * jnp.mean(dxn * xn, axis=-1, keepdims=True))
        dx_ref[...] = dx
        if produced_by is not None:
            y_ref, g_ref, dy_ref, dg_ref = refs[5], refs[6], refs[n_in + 4], refs[n_in + 5]
            dy_ref[...] = ((produced_by[2] * g_ref[...]) * dx).astype(dy_ref.dtype)
            dg_ref[...] += produced_by[2] * jnp.sum(dx * y_ref[...], axis=0, keepdims=True)

    row = pl.BlockSpec((tm, D), lambda i: (i, 0))
    vec = _vec_spec(D)
    vec_out = jax.ShapeDtypeStruct((1, D), F32)
    if isinstance(dh, tuple):
        k_dim = dh[0].shape[1]
        args, in_specs = [dh[0], dh[1]], [pl.BlockSpec((tm, k_dim), lambda i: (i, 0)), _resident((k_dim, D))]
    else:
        args, in_specs = [dh], [row]
    args, in_specs = args + [x, nw, scale, dres], in_specs + [row, vec, vec, row]
    out_specs, out_shape = [row, vec, vec, vec], [jax.ShapeDtypeStruct((T, D), F32), vec_out, vec_out, vec_out]
    if produced_by is not None:
        args += [produced_by[0], produced_by[1]]
        in_specs += [row, vec]
        out_specs += [row, vec]
        out_shape += [jax.ShapeDtypeStruct((T, D), MXU_DTYPE), vec_out]
    return pl.pallas_call(
        body, name=name, grid=(T // tm,),
        in_specs=in_specs, out_specs=out_specs, out_shape=out_shape,
        compiler_params=_params("arbitrary"),
    )(*args)


def _rms(xf):
    r = lax.rsqrt(jnp.mean(xf * xf, axis=-1, keepdims=True) + EPS)
    return r, xf * r


def _norm_bwd_math(dh, xf, nw, sc):
    r, xn = _rms(xf)
    t = dh * xn
    dxn = dh * (nw * (1.0 + sc))
    dx = r * (dxn - xn * jnp.mean(dxn * xn, axis=-1, keepdims=True))
    return dx, jnp.sum(dh, axis=0, keepdims=True), jnp.sum(t, axis=0, keepdims=True)


def _swiglu_gate(x, wg_t, nw, scale, shift, name):
    T, D = x.shape
    Fdim = wg_t.shape[0]
    tm = _tile(T, 512)

    def body(x_ref, wg_ref, nw_ref, sc_ref, sh_ref, g_ref, h_ref):
        _, xn = _rms(x_ref[...])
        hh = (xn * nw_ref[...] * (1.0 + sc_ref[...]) + sh_ref[...]).astype(MXU_DTYPE)
        h_ref[...] = hh
        g_ref[...] = _dot(hh, wg_ref[...], NT).astype(g_ref.dtype)

    row = pl.BlockSpec((tm, D), lambda i: (i, 0))
    vec = _vec_spec(D)
    return pl.pallas_call(
        body, name=name, grid=(T // tm,),
        in_specs=[row, _resident((Fdim, D)), vec, vec, vec],
        out_specs=[pl.BlockSpec((tm, Fdim), lambda i: (i, 0)), row],
        out_shape=[jax.ShapeDtypeStruct((T, Fdim), MXU_DTYPE), jax.ShapeDtypeStruct((T, D), MXU_DTYPE)],
        compiler_params=_params("parallel"),
    )(x, wg_t, nw, scale, shift)


def _swiglu_up(h, wu_t, g, name):
    T, D = h.shape
    Fdim = wu_t.shape[0]
    tm = _tile(T, 512)

    def body(h_ref, wu_ref, g_ref, u_ref, a_ref):
        u = _dot(h_ref[...], wu_ref[...], NT)
        g = g_ref[...].astype(F32)
        u_ref[...] = u.astype(u_ref.dtype)
        a_ref[...] = ((g * _sigmoid(g)) * u).astype(a_ref.dtype)

    frow = pl.BlockSpec((tm, Fdim), lambda i: (i, 0))
    return pl.pallas_call(
        body, name=name, grid=(T // tm,),
        in_specs=[pl.BlockSpec((tm, D), lambda i: (i, 0)), _resident((Fdim, D)), frow],
        out_specs=[frow, frow],
        out_shape=[jax.ShapeDtypeStruct((T, Fdim), MXU_DTYPE)] * 2,
        compiler_params=_params("parallel"),
    )(h, wu_t, g)


def _swiglu_fwd_loss(h, wg_t, wu_t, wd, x, gate, fw, target, name):
    T, D = h.shape
    Fdim = wd.shape[0]
    tm, tf = _tile(T, FFN_TOKEN_TILE), _tile(Fdim, FFN_HIDDEN_TILE)
    row = pl.BlockSpec((tm, D), lambda i: (i, 0))
    frow = pl.BlockSpec((tm, Fdim), lambda i: (i, 0))
    vec, wres = _vec_spec(D), _resident((Fdim, D))
    vec_out = jax.ShapeDtypeStruct((1, D), F32)

    def body(h_ref, wg_ref, wu_ref, wd_ref, x_ref, gt_ref, fw_ref, t_ref,
             g_ref, u_ref, a_ref, loss_ref, dx_ref, dfw_ref, dy_ref, dg_ref):
        hh = h_ref[...]
        y = None
        for k in range(Fdim // tf):
            ks = slice(k * tf, (k + 1) * tf)
            g = _dot(hh, wg_ref[ks, :], NT)
            u = _dot(hh, wu_ref[ks, :], NT)
            a = ((g * _sigmoid(g)) * u).astype(a_ref.dtype)
            g_ref[:, ks] = g.astype(g_ref.dtype)
            u_ref[:, ks] = u.astype(u_ref.dtype)
            a_ref[:, ks] = a
            part = _dot(a, wd_ref[ks, :], NN)
            y = part if y is None else y + part

        @pl.when(pl.program_id(0) == 0)
        def _():
            loss_ref[...] = jnp.zeros_like(loss_ref)
            dfw_ref[...] = jnp.zeros_like(dfw_ref)
            dg_ref[...] = jnp.zeros_like(dg_ref)

        r, xn = _rms(x_ref[...] + (0.5 * gt_ref[...]) * y)
        err = xn * fw_ref[...] - t_ref[...]
        per_tok = jnp.mean(err * err, axis=-1, keepdims=True)
        loss_ref[...] += 0.5 * jnp.sum(per_tok, axis=0, keepdims=True)
        d_out = err * (1.0 / D)
        dfw_ref[...] += jnp.sum(d_out * xn, axis=0, keepdims=True)
        dxn = d_out * fw_ref[...]
        dx = r * (dxn - xn * jnp.mean(dxn * xn, axis=-1, keepdims=True))
        dx_ref[...] = dx
        dy_ref[...] = ((0.5 * gt_ref[...]) * dx).astype(dy_ref.dtype)
        dg_ref[...] += 0.5 * jnp.sum(dx * y, axis=0, keepdims=True)

    return pl.pallas_call(
        body, name=name, grid=(T // tm,),
        in_specs=[row, wres, wres, wres, row, vec, vec, row],
        out_specs=[frow, frow, frow, pl.BlockSpec((1, 128), lambda i: (0, 0)), row, vec, row, vec],
        out_shape=[jax.ShapeDtypeStruct((T, Fdim), MXU_DTYPE)] * 3
        + [jax.ShapeDtypeStruct((1, 128), F32), jax.ShapeDtypeStruct((T, D), F32), vec_out,
           jax.ShapeDtypeStruct((T, D), MXU_DTYPE), vec_out],
        compiler_params=_params("arbitrary"),
    )(h, wg_t, wu_t, wd, x, gate, fw, target)


def _swiglu_bwd(dy, wd, g, u, wg_t, wu_t, name, norm_in, dres, produced_by=None):
    T, D = dy.shape
    Fdim = wd.shape[0]
    tm, tf = _tile(T, FFN_TOKEN_TILE), _tile(Fdim, FFN_HIDDEN_TILE)
    row = pl.BlockSpec((tm, D), lambda i: (i, 0))
    frow = pl.BlockSpec((tm, Fdim), lambda i: (i, 0))
    vec, wres = _vec_spec(D), _resident((Fdim, D))
    vec_out = jax.ShapeDtypeStruct((1, D), F32)
    args, in_specs = [dy, wd, g, u, wg_t, wu_t, *norm_in, dres], [row, wres, frow, frow, wres, wres, row, vec, vec, row]
    out_shape = [jax.ShapeDtypeStruct((T, Fdim), MXU_DTYPE)] * 2 + [jax.ShapeDtypeStruct((T, D), F32)] + [vec_out] * 3
    out_specs = [frow, frow, row, vec, vec, vec]
    if produced_by is not None:
        args += [produced_by[0], produced_by[1]]
        in_specs += [row, vec]
        out_shape += [jax.ShapeDtypeStruct((T, D), MXU_DTYPE), vec_out]
        out_specs += [row, vec]

    def body(*refs):
        it = iter(refs)
        dy_ref, wd_ref, g_ref, u_ref, wg_ref, wu_ref, x_ref, nw_ref, sc_ref, dr_ref = [next(it) for _ in range(10)]
        prev_refs = [next(it), next(it)] if produced_by is not None else None
        dg_ref, du_ref, dx_ref, dnw_ref, dsc_ref, dsh_ref = [next(it) for _ in range(6)]
        prev_out = [next(it), next(it)] if produced_by is not None else None

        @pl.when(pl.program_id(0) == 0)
        def _():
            dnw_ref[...] = jnp.zeros_like(dnw_ref)
            dsc_ref[...] = jnp.zeros_like(dsc_ref)
            dsh_ref[...] = jnp.zeros_like(dsh_ref)
            if produced_by is not None:
                prev_out[1][...] = jnp.zeros_like(prev_out[1])

        dyy = dy_ref[...]
        dh = None
        for k in range(Fdim // tf):
            ks = slice(k * tf, (k + 1) * tf)
            da = _dot(dyy, wd_ref[ks, :], NT)
            gg = g_ref[:, ks].astype(F32)
            sig = _sigmoid(gg)
            dg = (da * u_ref[:, ks].astype(F32) * (sig * (1.0 + gg * (1.0 - sig)))).astype(dg_ref.dtype)
            du = (da * (gg * sig)).astype(du_ref.dtype)
            dg_ref[:, ks] = dg
            du_ref[:, ks] = du
            part = _dot(dg, wg_ref[ks, :], NN) + _dot(du, wu_ref[ks, :], NN)
            dh = part if dh is None else dh + part
        dx_norm, dsh_row, t_row = _norm_bwd_math(dh, x_ref[...], nw_ref[...], sc_ref[...])
        dsh_ref[...] += dsh_row
        dsc_ref[...] += t_row * nw_ref[...]
        dnw_ref[...] += t_row * (1.0 + sc_ref[...])
        dx = dr_ref[...] + dx_norm
        dx_ref[...] = dx
        if produced_by is not None:
            prev_out[0][...] = ((produced_by[2] * prev_refs[1][...]) * dx).astype(prev_out[0].dtype)
            prev_out[1][...] += produced_by[2] * jnp.sum(dx * prev_refs[0][...], axis=0, keepdims=True)

    return pl.pallas_call(
        body, name=name, grid=(T // tm,), in_specs=in_specs, out_specs=out_specs, out_shape=out_shape,
        compiler_params=_params("arbitrary"),
    )(*args)


def _resident(shape):
    return pl.BlockSpec(shape, lambda i: (0,) * len(shape), pipeline_mode=pl.Buffered(1))


def _conv_act(window, w):
    y = window * w[CONV_K - 1:CONV_K, :]
    for j in range(CONV_K - 1):
        y += _shift_rows(window, CONV_K - 1 - j) * w[j:j + 1, :]
    return y


def _gdn_prep(proj, conv_w, a_log_l, dt_bias_l, name):
    T = proj.shape[0]
    tm = _tile(T, 512)
    hb = tm // 8

    def body(cur_ref, halo_ref, ba_ref, w_ref, al_ref, dtb_ref, qkv_ref, bg_ref):
        i = pl.program_id(0)
        halo = jnp.where(i == 0, 0.0, halo_ref[...])
        window = jnp.concatenate([halo, cur_ref[...]], axis=0)
        y = _conv_act(window, w_ref[...])[8:, :]
        act = y * _sigmoid(y)
        for hh in range(3 * GDN_HEADS):
            blk = act[:, hh * HEAD_DIM:(hh + 1) * HEAD_DIM]
            if hh < 2 * GDN_HEADS:
                rn = lax.rsqrt(jnp.sum(blk * blk, axis=-1, keepdims=True) + EPS)
                blk = blk * rn
                if hh < GDN_HEADS:
                    blk = blk * (HEAD_DIM ** -0.5)
            qkv_ref[:, hh * HEAD_DIM:(hh + 1) * HEAD_DIM] = blk
        ba = ba_ref[...]
        lane = lax.broadcasted_iota(jnp.int32, ba.shape, 1)
        beta = _sigmoid(ba)
        g = -jnp.exp(al_ref[...]) * _softplus(ba + dtb_ref[...])
        bg_ref[...] = jnp.where(lane < GDN_HEADS, beta, jnp.where(lane < 2 * GDN_HEADS, g, 0.0))

    return pl.pallas_call(
        body, name=name, grid=(T // tm,),
        in_specs=[pl.BlockSpec((tm, QKV_WIDTH), lambda i: (i, 0)),
                  pl.BlockSpec((8, QKV_WIDTH), lambda i: (jnp.maximum(i * hb - 1, 0), 0)),
                  pl.BlockSpec((tm, 128), lambda i: (i, COL_BA // 128)),
                  pl.BlockSpec((CONV_K, QKV_WIDTH), lambda i: (0, 0)),
                  pl.BlockSpec((1, 128), lambda i: (0, 0)), pl.BlockSpec((1, 128), lambda i: (0, 0))],
        out_specs=[pl.BlockSpec((tm, QKV_WIDTH), lambda i: (i, 0)), pl.BlockSpec((tm, 128), lambda i: (i, 0))],
        out_shape=[jax.ShapeDtypeStruct((T, QKV_WIDTH), F32), jax.ShapeDtypeStruct((T, 128), F32)],
        compiler_params=_params("parallel"),
    )(proj, proj, proj, conv_w, a_log_l, dt_bias_l)


def _chunk_cumsum(v, reverse=False):
    row = lax.broadcasted_iota(jnp.int32, v.shape, 0)
    s = 1
    while s < CHUNK:
        if reverse:
            v = v + jnp.where(row < CHUNK - s, _shift_rows(v, -s), 0.0)
        else:
            v = v + jnp.where(row >= s, _shift_rows(v, s), 0.0)
        s *= 2
    return v


def _row_form(cols):
    padded = jnp.concatenate([cols, jnp.zeros((128 - CHUNK, 128), F32)], axis=0)
    return padded.T[:, :CHUNK]


def _chunk_masks():
    ri = lax.broadcasted_iota(jnp.int32, (CHUNK, CHUNK), 0)
    ci = lax.broadcasted_iota(jnp.int32, (CHUNK, CHUNK), 1)
    return ri >= ci, ri > ci, (ri == ci).astype(F32)


def _unit_lower_inverses(ms, eye):
    rs = [eye - m for m in ms]
    ps = [_split(-m) for m in ms]
    s = 2
    while s < CHUNK:
        ps = [_split(_dot3(p, p, NN)) for p in ps]
        r_parts = [_split(r) for r in rs]
        rs = [r + _dot3(p, rp, NN) for r, p, rp in zip(rs, ps, r_parts)]
        s *= 2
    return rs


def _head_elementwise(k, beta, gc, gcr, causal):
    decay = jnp.where(causal, jnp.exp(jnp.where(causal, gc - gcr, 0.0)), 0.0)
    return decay, k * beta, jnp.exp(gc)


def _head_slices(hh):
    return (slice(hh * HEAD_DIM, (hh + 1) * HEAD_DIM),
            slice(GDN_WIDTH + hh * HEAD_DIM, GDN_WIDTH + (hh + 1) * HEAD_DIM),
            slice(2 * GDN_WIDTH + hh * HEAD_DIM, 2 * GDN_WIDTH + (hh + 1) * HEAD_DIM))


def _gdn_chunk_fwd(qkv, bg, name):
    T = qkv.shape[0]
    cb = _tile(T // CHUNK, CHUNKS_PER_STEP)
    rows = cb * CHUNK

    def body(qkv_ref, bg_ref, tinv_ref, u_ref, w_ref, qd_ref, kd_ref, p_ref, cd_ref):
        masks = _chunk_masks()
        causal, strict, eye = masks
        heads = []
        for ci in range(cb):
            rs = slice(ci * CHUNK, (ci + 1) * CHUNK)
            bgv = bg_ref[rs, :]
            gc_all = _chunk_cumsum(bgv)
            gc_rows = _row_form(gc_all)
            cd_ref[rs, :] = jnp.exp(jnp.broadcast_to(gc_all[CHUNK - 1:CHUNK, :], (CHUNK, 128)))
            for hh in range(GDN_HEADS):
                qs, ks, vs = _head_slices(hh)
                q, k, v = qkv_ref[rs, qs], qkv_ref[rs, ks], qkv_ref[rs, vs]
                beta = bgv[:, hh:hh + 1]
                gc = gc_all[:, GDN_HEADS + hh:GDN_HEADS + hh + 1]
                decay, kb, eg = _head_elementwise(k, beta, gc, gc_rows[GDN_HEADS + hh:GDN_HEADS + hh + 1, :], causal)
                hs = slice(hh * HEAD_DIM, (hh + 1) * HEAD_DIM)
                cs = slice(hh * CHUNK, (hh + 1) * CHUNK)
                qd_ref[rs, hs] = (q * eg).astype(qd_ref.dtype)
                kd_ref[rs, hs] = (k * jnp.exp(gc[CHUNK - 1:CHUNK, :] - gc)).astype(kd_ref.dtype)
                heads.append((rs, hs, cs, q, k, v * beta, kb, kb * eg, decay))
        kks = [_mdot(kb, k, NT) for (_, _, _, _, k, _, kb, _, _) in heads]
        qks = [_mdot(q, k, NT) for (_, _, _, q, k, _, _, _, _) in heads]
        tinvs = _unit_lower_inverses([jnp.where(strict, kk * hd[8], 0.0) for kk, hd in zip(kks, heads)], eye)
        t_parts = [_split(t) for t in tinvs]
        us = [_dot3(tp, _split(hd[5]), NN) for tp, hd in zip(t_parts, heads)]
        ws = [_dot3(tp, _split(hd[7]), NN) for tp, hd in zip(t_parts, heads)]
        for hd, tinv, u, w, qk in zip(heads, tinvs, us, ws, qks):
            rs, hs, cs = hd[0], hd[1], hd[2]
            tinv_ref[rs, cs] = tinv
            u_ref[rs, hs] = u
            w_ref[rs, hs] = w.astype(w_ref.dtype)
            p_ref[rs, cs] = jnp.where(causal, qk * hd[8], 0.0).astype(p_ref.dtype)

    def spec(width):
        return pl.BlockSpec((rows, width), lambda n: (n, 0))

    hw, cw = GDN_WIDTH, GDN_HEADS * CHUNK
    return pl.pallas_call(
        body, name=name, grid=(T // rows,),
        in_specs=[spec(QKV_WIDTH), spec(128)],
        out_specs=[spec(cw), spec(hw), spec(hw), spec(hw), spec(hw), spec(cw), spec(128)],
        out_shape=[jax.ShapeDtypeStruct((T, cw), F32), jax.ShapeDtypeStruct((T, hw), F32),
                   jax.ShapeDtypeStruct((T, hw), MXU_DTYPE), jax.ShapeDtypeStruct((T, hw), MXU_DTYPE),
                   jax.ShapeDtypeStruct((T, hw), MXU_DTYPE), jax.ShapeDtypeStruct((T, cw), MXU_DTYPE),
                   jax.ShapeDtypeStruct((T, 128), F32)],
        compiler_params=_params("parallel"),
    )(qkv, bg)


def _gdn_scan_fwd(u, w, qd, kd, p, cd, name):
    T = u.shape[0]
    cb = _tile(T // CHUNK, SCAN_CHUNKS_PER_STEP)
    rows = cb * CHUNK

    def body(u_ref, w_ref, qd_ref, kd_ref, p_ref, cd_ref, o_ref, s_all_ref, vn_ref, s_ref):
        @pl.when(pl.program_id(0) == 0)
        def _():
            s_ref[...] = jnp.zeros_like(s_ref)

        hss = [slice(hh * HEAD_DIM, (hh + 1) * HEAD_DIM) for hh in range(GDN_HEADS)]
        css = [slice(hh * CHUNK, (hh + 1) * CHUNK) for hh in range(GDN_HEADS)]
        s_cur = [s_ref[hh] for hh in range(GDN_HEADS)]
        for ci in range(cb):
            rs = slice(ci * CHUNK, (ci + 1) * CHUNK)
            for hh in range(GDN_HEADS):
                s_all_ref[ci * GDN_WIDTH + hh * HEAD_DIM:ci * GDN_WIDTH + (hh + 1) * HEAD_DIM, :] = s_cur[hh]
            s_ms = [s.astype(MXU_DTYPE) for s in s_cur]
            w_s = [_dot(w_ref[rs, hs], s_m, NN) for hs, s_m in zip(hss, s_ms)]
            q_s = [_dot(qd_ref[rs, hs], s_m, NN) for hs, s_m in zip(hss, s_ms)]
            v_ms = [(u_ref[rs, hs] - ws_).astype(MXU_DTYPE) for hs, ws_ in zip(hss, w_s)]
            k_v = [_dot(kd_ref[rs, hs], v_m, TN) for hs, v_m in zip(hss, v_ms)]
            p_v = [_dot(p_ref[rs, cs], v_m, NN) for cs, v_m in zip(css, v_ms)]
            for hh in range(GDN_HEADS):
                vn_ref[rs, hss[hh]] = v_ms[hh]
                o_ref[rs, hss[hh]] = q_s[hh] + p_v[hh]
                c_dec = cd_ref[ci * CHUNK:ci * CHUNK + 1, GDN_HEADS + hh:GDN_HEADS + hh + 1]
                s_cur[hh] = s_cur[hh] * c_dec + k_v[hh]
        for hh in range(GDN_HEADS):
            s_ref[hh] = s_cur[hh]

    def spec(width):
        return pl.BlockSpec((rows, width), lambda n: (n, 0))

    hw, cw = GDN_WIDTH, GDN_HEADS * CHUNK
    return pl.pallas_call(
        body, name=name, grid=(T // rows,),
        in_specs=[spec(hw), spec(hw), spec(hw), spec(hw), spec(cw), spec(128)],
        out_specs=[spec(hw), pl.BlockSpec((cb * GDN_WIDTH, HEAD_DIM), lambda n: (n, 0)), spec(hw)],
        out_shape=[jax.ShapeDtypeStruct((T, hw), F32),
                   jax.ShapeDtypeStruct((T // CHUNK * GDN_WIDTH, HEAD_DIM), F32),
                   jax.ShapeDtypeStruct((T, hw), MXU_DTYPE)],
        scratch_shapes=[pltpu.VMEM((GDN_HEADS, HEAD_DIM, HEAD_DIM), F32)],
        compiler_params=_params("arbitrary"),
    )(u, w, qd, kd, p, cd)


def _gdn_scan_bwd(do, w, qd, kd, p, cd, s_all, vn, name):
    T = do.shape[0]
    cb = _tile(T // CHUNK, SCAN_CHUNKS_PER_STEP)
    rows = cb * CHUNK
    n_steps = T // rows

    def body(do_ref, w_ref, qd_ref, kd_ref, p_ref, cd_ref, s_all_ref, vn_ref,
             dvn_ref, dw_ref, dqd_ref, dkd_ref, dp_ref, dcd_ref, ds_ref):
        @pl.when(pl.program_id(0) == 0)
        def _():
            ds_ref[...] = jnp.zeros_like(ds_ref)

        causal, _, _ = _chunk_masks()
        lane = lax.broadcasted_iota(jnp.int32, (CHUNK, 128), 1)
        heads = range(GDN_HEADS)
        hss = [slice(hh * HEAD_DIM, (hh + 1) * HEAD_DIM) for hh in heads]
        css = [slice(hh * CHUNK, (hh + 1) * CHUNK) for hh in heads]
        ds_cur = [ds_ref[hh] for hh in heads]
        for ci in reversed(range(cb)):
            rs = slice(ci * CHUNK, (ci + 1) * CHUNK)
            ds_ms = [d.astype(MXU_DTYPE) for d in ds_cur]
            s_olds = [s_all_ref[ci * GDN_WIDTH + hh * HEAD_DIM:ci * GDN_WIDTH + (hh + 1) * HEAD_DIM, :] for hh in heads]
            s_ms = [s.astype(MXU_DTYPE) for s in s_olds]
            do_ms = [do_ref[rs, hs].astype(MXU_DTYPE) for hs in hss]
            p_do = [_dot(p_ref[rs, cs], do_m, TN) for cs, do_m in zip(css, do_ms)]
            k_ds = [_dot(kd_ref[rs, hs], ds_m, NN) for hs, ds_m in zip(hss, ds_ms)]
            q_do = [_dot(qd_ref[rs, hs], do_m, TN) for hs, do_m in zip(hss, do_ms)]
            dqds = [_dot(do_m, s_m, NT) for do_m, s_m in zip(do_ms, s_ms)]
            dkds = [_dot(vn_ref[rs, hs], ds_m, NT) for hs, ds_m in zip(hss, ds_ms)]
            dps = [_dot(do_m, vn_ref[rs, hs], NT) for hs, do_m in zip(hss, do_ms)]
            dv_news = [a + b for a, b in zip(p_do, k_ds)]
            dvn_ms = [d.astype(MXU_DTYPE) for d in dv_news]
            w_dv = [_dot(w_ref[rs, hs], dvn_m, TN) for hs, dvn_m in zip(hss, dvn_ms)]
            dws = [_dot(dvn_m, s_m, NT) for dvn_m, s_m in zip(dvn_ms, s_ms)]
            dcd_tile = jnp.zeros((CHUNK, 128), F32)
            for hh in heads:
                dvn_ref[rs, hss[hh]] = dv_news[hh]
                dw_ref[rs, hss[hh]] = -dws[hh]
                dqd_ref[rs, hss[hh]] = dqds[hh]
                dkd_ref[rs, hss[hh]] = dkds[hh]
                dp_ref[rs, css[hh]] = jnp.where(causal, dps[hh], 0.0)
                dcd = jnp.sum(jnp.sum(s_olds[hh] * ds_cur[hh], axis=1, keepdims=True), axis=0, keepdims=True)
                dcd_tile = jnp.where(lane == GDN_HEADS + hh, dcd, dcd_tile)
                c_dec = cd_ref[ci * CHUNK:ci * CHUNK + 1, GDN_HEADS + hh:GDN_HEADS + hh + 1]
                ds_cur[hh] = c_dec * ds_cur[hh] + q_do[hh] - w_dv[hh]
            dcd_ref[rs, :] = dcd_tile
        for hh in heads:
            ds_ref[hh] = ds_cur[hh]

    def spec(width):
        return pl.BlockSpec((rows, width), lambda n: (n_steps - 1 - n, 0))

    hw, cw = GDN_WIDTH, GDN_HEADS * CHUNK
    return pl.pallas_call(
        body, name=name, grid=(n_steps,),
        in_specs=[spec(hw), spec(hw), spec(hw), spec(hw), spec(cw), spec(128),
                  pl.BlockSpec((cb * GDN_WIDTH, HEAD_DIM), lambda n: (n_steps - 1 - n, 0)), spec(hw)],
        out_specs=[spec(hw), spec(hw), spec(hw), spec(hw), spec(cw), spec(128)],
        out_shape=[jax.ShapeDtypeStruct((T, hw), F32)] * 4
        + [jax.ShapeDtypeStruct((T, cw), F32), jax.ShapeDtypeStruct((T, 128), F32)],
        scratch_shapes=[pltpu.VMEM((GDN_HEADS, HEAD_DIM, HEAD_DIM), F32)],
        compiler_params=_params("arbitrary"),
    )(do, w, qd, kd, p, cd, s_all, vn)


def _gdn_chunk_bwd(qkv, bg, tinv_all, u, w, dvn, dw, dqd, dkd, dp, dcd, name):
    T = qkv.shape[0]
    cb = _tile(T // CHUNK, CHUNKS_PER_STEP)
    rows = cb * CHUNK

    def body(qkv_ref, bg_ref, tinv_ref, u_ref, w_ref, dvn_ref, dw_ref, dqd_ref, dkd_ref, dp_ref, dcd_ref,
             dqkv_ref, dbg_ref):
        masks = _chunk_masks()
        causal, strict, _ = masks
        lane = lax.broadcasted_iota(jnp.int32, (CHUNK, 128), 1)
        row = lax.broadcasted_iota(jnp.int32, (CHUNK, 128), 0)
        heads = []
        for ci in range(cb):
            rs = slice(ci * CHUNK, (ci + 1) * CHUNK)
            bgv = bg_ref[rs, :]
            gc_all = _chunk_cumsum(bgv)
            gc_rows = _row_form(gc_all)
            for hh in range(GDN_HEADS):
                qs, ks, vs = _head_slices(hh)
                q, k = qkv_ref[rs, qs], qkv_ref[rs, ks]
                beta = bgv[:, hh:hh + 1]
                gc = gc_all[:, GDN_HEADS + hh:GDN_HEADS + hh + 1]
                decay, kb, eg = _head_elementwise(k, beta, gc, gc_rows[GDN_HEADS + hh:GDN_HEADS + hh + 1, :], causal)
                heads.append(dict(ci=ci, hh=hh, rs=rs, hs=slice(hh * HEAD_DIM, (hh + 1) * HEAD_DIM),
                                  cs=slice(hh * CHUNK, (hh + 1) * CHUNK), q=q, k=k, beta=beta, gc=gc,
                                  decay=decay, kb=kb, eg=eg))
        for hd in heads:
            hd["t"] = _split(tinv_ref[hd["rs"], hd["cs"]])
        for hd in heads:
            hd["kk"] = _mdot(hd["kb"], hd["k"], NT)
            hd["qk"] = _mdot(hd["q"], hd["k"], NT)
        for hd in heads:
            hd["dvb"] = _dot3(hd["t"], _split(dvn_ref[hd["rs"], hd["hs"]]), TN)
            hd["dkbeg"] = _dot3(hd["t"], _split(dw_ref[hd["rs"], hd["hs"]]), TN)
        for hd in heads:
            rs, hs = hd["rs"], hd["hs"]
            da = -(_mdot(hd["dvb"], u_ref[rs, hs], NT) + _mdot(hd["dkbeg"], w_ref[rs, hs], NT))
            dm = jnp.where(strict, da, 0.0)
            dp_ = dp_ref[rs, hd["cs"]]
            hd["dkk"] = dm * hd["decay"]
            hd["dqk"] = dp_ * hd["decay"]
            hd["e"] = (hd["dkk"] * hd["kk"] + hd["dqk"] * hd["qk"])
        for hd in heads:
            hd["dkb"] = _mdot(hd["dkk"], hd["k"], NN)
            hd["dk"] = _mdot(hd["dkk"], hd["kb"], TN) + _mdot(hd["dqk"], hd["q"], TN)
            hd["dq"] = _mdot(hd["dqk"], hd["k"], NN)
            onehot = (lane == GDN_HEADS + hd["hh"]).astype(jnp.bfloat16)
            e_hi, e_lo = _split(hd["e"])
            hd["col_sums"] = _dot(e_lo, onehot, TN) + _dot(e_hi, onehot, TN)
        tiles = {}
        for hd in heads:
            ci, hh, rs, hs = hd["ci"], hd["hh"], hd["rs"], hd["hs"]
            qs, ks, vs = _head_slices(hh)
            q, k, beta, gc, eg, kb = hd["q"], hd["k"], hd["beta"], hd["gc"], hd["eg"], hd["kb"]
            v = qkv_ref[rs, vs]
            dqd_, dkd_ = dqd_ref[rs, hs], dkd_ref[rs, hs]
            gl = gc[CHUNK - 1:CHUNK, :]
            ek = jnp.exp(gl - gc)
            dkb = hd["dkb"] + hd["dkbeg"] * eg
            deg = jnp.sum(dqd_ * q, axis=1, keepdims=True) + jnp.sum(hd["dkbeg"] * kb, axis=1, keepdims=True)
            dek = jnp.sum(dkd_ * k, axis=1, keepdims=True)
            dcd_ = dcd_ref[ci * CHUNK:ci * CHUNK + 1, GDN_HEADS + hh:GDN_HEADS + hh + 1]
            dgl = jnp.sum(dek * ek, axis=0, keepdims=True) + dcd_ * jnp.exp(gl)
            dgc = jnp.sum(hd["e"], axis=1, keepdims=True) + deg * eg - dek * ek
            dbeta_tile, dgc_tile = tiles.get(ci, (jnp.zeros((CHUNK, 128), F32), jnp.zeros((CHUNK, 128), F32)))
            dgc_tile += jnp.where(lane == GDN_HEADS + hh, dgc, 0.0) - hd["col_sums"]
            dgc_tile += jnp.where((lane == GDN_HEADS + hh) & (row == CHUNK - 1), dgl, 0.0)
            dbeta = jnp.sum(dkb * k, axis=1, keepdims=True) + jnp.sum(hd["dvb"] * v, axis=1, keepdims=True)
            dbeta_tile += jnp.where(lane == hh, dbeta, 0.0)
            tiles[ci] = (dbeta_tile, dgc_tile)
            dqkv_ref[rs, qs] = hd["dq"] + dqd_ * eg
            dqkv_ref[rs, ks] = hd["dk"] + dkd_ * ek + dkb * beta
            dqkv_ref[rs, vs] = hd["dvb"] * beta
        for ci in range(cb):
            dbeta_tile, dgc_tile = tiles[ci]
            dbg_ref[ci * CHUNK:(ci + 1) * CHUNK, :] = dbeta_tile + _chunk_cumsum(dgc_tile, reverse=True)

    def spec(width):
        return pl.BlockSpec((rows, width), lambda n: (n, 0))

    hw, cw = GDN_WIDTH, GDN_HEADS * CHUNK
    return pl.pallas_call(
        body, name=name, grid=(T // rows,),
        in_specs=[spec(QKV_WIDTH), spec(128), spec(cw), spec(hw), spec(hw), spec(hw), spec(hw), spec(hw),
                  spec(hw), spec(cw), spec(128)],
        out_specs=[spec(QKV_WIDTH), spec(128)],
        out_shape=[jax.ShapeDtypeStruct((T, QKV_WIDTH), F32), jax.ShapeDtypeStruct((T, 128), F32)],
        compiler_params=_params("parallel"),
    )(qkv, bg, tinv_all, u, w, dvn, dw, dqd, dkd, dp, dcd)


def _pool_counts(i, tm, rows, offset):
    t = i * tm - offset + lax.broadcasted_iota(jnp.int32, (rows, 1), 0)
    return [jnp.minimum(t + 1, w).astype(F32) for w in POOL_WINDOWS]


def _window_sums(window, forward):
    sums, s, step = [], window, 1
    for _ in POOL_WINDOWS:
        s = s + _shift_rows(s, -step if forward else step)
        sums.append(s)
        step *= 2
    return sums


def _pooled(window, counts):
    sums = _window_sums(window, forward=False)
    out = []
    for gi in range(POOL_GROUPS):
        sl = slice(gi * 128, (gi + 1) * 128)
        out.append(sums[gi][HALO:, sl] / counts[gi] - window[HALO:, sl])
    return out


def _mix_post(o, proj, gdn_norm, pool_w, pool_scale, name):
    T = o.shape[0]
    tm = _tile(T, 512)
    hb = tm // HALO

    def body(o_ref, z_ref, p_ref, ph_ref, gn_ref, pw_ref, ps_ref, out_ref):
        i = pl.program_id(0)
        for hh in range(GDN_HEADS):
            sl = slice(hh * HEAD_DIM, (hh + 1) * HEAD_DIM)
            oh, zh = o_ref[:, sl], z_ref[:, sl]
            ro = lax.rsqrt(jnp.mean(oh * oh, axis=-1, keepdims=True) + EPS)
            out_ref[:, sl] = (((oh * ro) * gn_ref[...]) * (zh * _sigmoid(zh))).astype(out_ref.dtype)
        halo = jnp.where(i == 0, 0.0, ph_ref[...])
        window = jnp.concatenate([halo, p_ref[...]], axis=0)
        pooled = _pooled(window, _pool_counts(i, tm, tm, 0))
        for gi in range(POOL_GROUPS):
            pm = _mdot(pooled[gi], pw_ref[gi], NN)
            out_ref[:, GDN_WIDTH + gi * 128:GDN_WIDTH + (gi + 1) * 128] = (
                pm * ps_ref[:, gi * 128:(gi + 1) * 128]).astype(out_ref.dtype)

    return pl.pallas_call(
        body, name=name, grid=(T // tm,),
        in_specs=[pl.BlockSpec((tm, GDN_WIDTH), lambda i: (i, 0)),
                  pl.BlockSpec((tm, GDN_WIDTH), lambda i: (i, COL_Z // GDN_WIDTH)),
                  pl.BlockSpec((tm, POOL_WIDTH), lambda i: (i, COL_P // POOL_WIDTH)),
                  pl.BlockSpec((HALO, POOL_WIDTH), lambda i: (jnp.maximum(i * hb - 1, 0), COL_P // POOL_WIDTH)),
                  pl.BlockSpec((1, HEAD_DIM), lambda i: (0, 0)),
                  pl.BlockSpec((POOL_GROUPS, 128, 128), lambda i: (0, 0, 0)),
                  pl.BlockSpec((1, POOL_WIDTH), lambda i: (0, 0))],
        out_specs=pl.BlockSpec((tm, GDN_WIDTH + POOL_WIDTH), lambda i: (i, 0)),
        out_shape=jax.ShapeDtypeStruct((T, GDN_WIDTH + POOL_WIDTH), MXU_DTYPE),
        compiler_params=_params("parallel"),
    )(o, proj, proj, proj, gdn_norm, pool_w, pool_scale)


def _mix_post_bwd(dmix, o, proj, gdn_norm, pool_w, pool_scale, name):
    T = o.shape[0]
    tm = _tile(T, 256)
    hb = tm // HALO
    n_tiles = T // tm

    def body(dg_ref, dpo_ref, dpo_next_ref, o_ref, z_ref, p_ref, ph_ref, gn_ref, pw_ref, ps_ref,
             do_ref, dzp_ref, dgn_ref, dpw_ref, dps_ref):
        i = pl.program_id(0)

        @pl.when(i == 0)
        def _():
            dgn_ref[...] = jnp.zeros_like(dgn_ref)
            dpw_ref[...] = jnp.zeros_like(dpw_ref)
            dps_ref[...] = jnp.zeros_like(dps_ref)

        gn = gn_ref[...]
        dgn = jnp.zeros((1, HEAD_DIM), F32)
        for hh in range(GDN_HEADS):
            sl = slice(hh * HEAD_DIM, (hh + 1) * HEAD_DIM)
            oh, zh, dy = o_ref[:, sl], z_ref[:, sl], dg_ref[:, sl]
            ro = lax.rsqrt(jnp.mean(oh * oh, axis=-1, keepdims=True) + EPS)
            on = oh * ro
            sig = _sigmoid(zh)
            sz = zh * sig
            dzp_ref[:, sl] = (dy * (on * gn) * (sig * (1.0 + zh * (1.0 - sig)))).astype(dzp_ref.dtype)
            dgn += jnp.sum(dy * on * sz, axis=0, keepdims=True)
            don = dy * gn * sz
            do_ref[:, sl] = ro * (don - on * jnp.mean(don * on, axis=-1, keepdims=True))
        dgn_ref[...] += dgn

        halo = jnp.where(i == 0, 0.0, ph_ref[...])
        window = jnp.concatenate([halo, p_ref[...]], axis=0)
        counts = _pool_counts(i, tm, tm + HALO, 0)
        pooled = _pooled(window, [cn[:tm] for cn in counts])
        nxt = jnp.where(i == n_tiles - 1, 0.0, dpo_next_ref[...])
        dpo_w = jnp.concatenate([dpo_ref[...], nxt], axis=0)
        ps = ps_ref[...]
        dps = []
        scaled = []
        for gi in range(POOL_GROUPS):
            sl = slice(gi * 128, (gi + 1) * 128)
            dpm = dpo_w[:, sl] * ps[:, sl]
            pm = _mdot(pooled[gi], pw_ref[gi], NN)
            dps.append(jnp.sum(dpo_w[:tm, sl] * pm, axis=0, keepdims=True))
            dpw_ref[gi] += _mdot(pooled[gi], dpm[:tm], TN)
            dpooled = _mdot(dpm, pw_ref[gi], NT)
            scaled.append((dpooled, dpooled / counts[gi]))
        dps_ref[...] += jnp.concatenate(dps, axis=1)
        lead = _window_sums(jnp.concatenate([sc for _, sc in scaled], axis=1), forward=True)
        for gi in range(POOL_GROUPS):
            sl = slice(gi * 128, (gi + 1) * 128)
            dzp_ref[:, GDN_WIDTH + gi * 128:GDN_WIDTH + (gi + 1) * 128] = (
                lead[gi][:tm, sl] - scaled[gi][0][:tm]).astype(dzp_ref.dtype)

    last_halo = T // HALO - 1
    return pl.pallas_call(
        body, name=name, grid=(n_tiles,),
        in_specs=[pl.BlockSpec((tm, GDN_WIDTH), lambda i: (i, 0)),
                  pl.BlockSpec((tm, POOL_WIDTH), lambda i: (i, 1)),
                  pl.BlockSpec((HALO, POOL_WIDTH), lambda i: (jnp.minimum((i + 1) * hb, last_halo), 1)),
                  pl.BlockSpec((tm, GDN_WIDTH), lambda i: (i, 0)),
                  pl.BlockSpec((tm, GDN_WIDTH), lambda i: (i, COL_Z // GDN_WIDTH)),
                  pl.BlockSpec((tm, POOL_WIDTH), lambda i: (i, COL_P // POOL_WIDTH)),
                  pl.BlockSpec((HALO, POOL_WIDTH), lambda i: (jnp.maximum(i * hb - 1, 0), COL_P // POOL_WIDTH)),
                  pl.BlockSpec((1, HEAD_DIM), lambda i: (0, 0)),
                  pl.BlockSpec((POOL_GROUPS, 128, 128), lambda i: (0, 0, 0)),
                  pl.BlockSpec((1, POOL_WIDTH), lambda i: (0, 0))],
        out_specs=[pl.BlockSpec((tm, GDN_WIDTH), lambda i: (i, 0)),
                   pl.BlockSpec((tm, GDN_WIDTH + POOL_WIDTH), lambda i: (i, 0)),
                   pl.BlockSpec((1, HEAD_DIM), lambda i: (0, 0)),
                   pl.BlockSpec((POOL_GROUPS, 128, 128), lambda i: (0, 0, 0)),
                   pl.BlockSpec((1, POOL_WIDTH), lambda i: (0, 0))],
        out_shape=[jax.ShapeDtypeStruct((T, GDN_WIDTH), F32),
                   jax.ShapeDtypeStruct((T, GDN_WIDTH + POOL_WIDTH), MXU_DTYPE),
                   jax.ShapeDtypeStruct((1, HEAD_DIM), F32),
                   jax.ShapeDtypeStruct((POOL_GROUPS, 128, 128), F32),
                   jax.ShapeDtypeStruct((1, POOL_WIDTH), F32)],
        compiler_params=_params("arbitrary"),
    )(dmix, dmix, dmix, o, proj, proj, proj, gdn_norm, pool_w, pool_scale)


def _gdn_prep_bwd(proj, conv_w, a_log_l, dt_bias_l, dqkv, dbg, dzp, name):
    T = proj.shape[0]
    tm = _tile(T, 256)
    hb = tm // 8
    n_tiles = T // tm
    last_halo = T // 8 - 1

    def body(cur_ref, before_ref, after_ref, ba_ref, w_ref, al_ref, dtb_ref, dq_ref, dq_after_ref, dbg_ref,
             dzp_ref, dproj_ref, dw_ref, dal_ref, ddtb_ref):
        i = pl.program_id(0)

        @pl.when(i == 0)
        def _():
            dw_ref[...] = jnp.zeros_like(dw_ref)
            dal_ref[...] = jnp.zeros_like(dal_ref)
            ddtb_ref[...] = jnp.zeros_like(ddtb_ref)

        last = i == n_tiles - 1
        w = w_ref[...]
        before = jnp.where(i == 0, 0.0, before_ref[...])
        after = jnp.where(last, 0.0, after_ref[...])
        window = jnp.concatenate([before, cur_ref[...], after], axis=0)
        y = _conv_act(window, w)
        sig = _sigmoid(y)
        act = y * sig
        dq_w = jnp.concatenate([jnp.zeros((8, QKV_WIDTH), F32), dq_ref[...],
                                jnp.where(last, 0.0, dq_after_ref[...])], axis=0)
        dact = []
        for hh in range(3 * GDN_HEADS):
            sl = slice(hh * HEAD_DIM, (hh + 1) * HEAD_DIM)
            blk, dblk = act[:, sl], dq_w[:, sl]
            if hh < 2 * GDN_HEADS:
                rn = lax.rsqrt(jnp.sum(blk * blk, axis=-1, keepdims=True) + EPS)
                unit = blk * rn
                if hh < GDN_HEADS:
                    dblk = dblk * (HEAD_DIM ** -0.5)
                dblk = rn * (dblk - unit * jnp.sum(dblk * unit, axis=-1, keepdims=True))
            dact.append(dblk)
        dy = jnp.concatenate(dact, axis=1) * (sig * (1.0 + y * (1.0 - sig)))
        dx = dy * w[CONV_K - 1:CONV_K, :]
        dws = [None] * CONV_K
        dws[CONV_K - 1] = jnp.sum(dy[8:8 + tm] * window[8:8 + tm], axis=0, keepdims=True)
        for j in range(CONV_K - 1):
            s = CONV_K - 1 - j
            dx += _shift_rows(dy, -s) * w[j:j + 1, :]
            dws[j] = jnp.sum(dy[8:8 + tm] * _shift_rows(window, s)[8:8 + tm], axis=0, keepdims=True)
        dw_ref[...] += jnp.concatenate(dws, axis=0)
        dproj_ref[:, :QKV_WIDTH] = dx[8:8 + tm].astype(dproj_ref.dtype)
        dproj_ref[:, COL_Z:COL_BA] = dzp_ref[...]

        ba = ba_ref[...]
        dbg_ = dbg_ref[...]
        lane = lax.broadcasted_iota(jnp.int32, ba.shape, 1)
        beta = _sigmoid(ba)
        pre = ba + dtb_ref[...]
        neg_a = -jnp.exp(al_ref[...])
        g = neg_a * _softplus(pre)
        is_g = (lane >= GDN_HEADS) & (lane < 2 * GDN_HEADS)
        da_raw = jnp.where(is_g, dbg_ * neg_a * _sigmoid(pre), 0.0)
        dba = jnp.where(lane < GDN_HEADS, dbg_ * beta * (1.0 - beta), da_raw)
        dproj_ref[:, COL_BA:] = dba.astype(dproj_ref.dtype)
        dal_ref[...] += jnp.sum(jnp.where(is_g, dbg_ * g, 0.0), axis=0, keepdims=True)
        ddtb_ref[...] += jnp.sum(da_raw, axis=0, keepdims=True)

    lane_vec = pl.BlockSpec((1, 128), lambda i: (0, 0))
    return pl.pallas_call(
        body, name=name, grid=(n_tiles,),
        in_specs=[pl.BlockSpec((tm, QKV_WIDTH), lambda i: (i, 0)),
                  pl.BlockSpec((8, QKV_WIDTH), lambda i: (jnp.maximum(i * hb - 1, 0), 0)),
                  pl.BlockSpec((8, QKV_WIDTH), lambda i: (jnp.minimum((i + 1) * hb, last_halo), 0)),
                  pl.BlockSpec((tm, 128), lambda i: (i, COL_BA // 128)),
                  pl.BlockSpec((CONV_K, QKV_WIDTH), lambda i: (0, 0)), lane_vec, lane_vec,
                  pl.BlockSpec((tm, QKV_WIDTH), lambda i: (i, 0)),
                  pl.BlockSpec((8, QKV_WIDTH), lambda i: (jnp.minimum((i + 1) * hb, last_halo), 0)),
                  pl.BlockSpec((tm, 128), lambda i: (i, 0)),
                  pl.BlockSpec((tm, GDN_WIDTH + POOL_WIDTH), lambda i: (i, 0))],
        out_specs=[pl.BlockSpec((tm, D_IN_PAD), lambda i: (i, 0)),
                   pl.BlockSpec((CONV_K, QKV_WIDTH), lambda i: (0, 0)), lane_vec, lane_vec],
        out_shape=[jax.ShapeDtypeStruct((T, D_IN_PAD), MXU_DTYPE),
                   jax.ShapeDtypeStruct((CONV_K, QKV_WIDTH), F32),
                   jax.ShapeDtypeStruct((1, 128), F32), jax.ShapeDtypeStruct((1, 128), F32)],
        compiler_params=_params("arbitrary"),
    )(proj, proj, proj, proj, conv_w, a_log_l, dt_bias_l, dqkv, dqkv, dbg, dzp)


def _mod_part(c_all, w_ada, b_part, name):
    def body(c_ref, w_ref, b_ref, out_ref):
        cc = c_ref[...]
        out_ref[...] = _mdot(cc * _sigmoid(cc), w_ref[...], NN) + b_ref[...]

    return pl.pallas_call(
        body, name=name, out_shape=jax.ShapeDtypeStruct((c_all.shape[0], w_ada.shape[1]), F32),
        compiler_params=_params(),
    )(c_all, w_ada, b_part)


def _w_ada_grad(c_all, dmod_part, name):
    def body(c_ref, d_ref, out_ref):
        cc = c_ref[...]
        out_ref[...] = _mdot(cc * _sigmoid(cc), d_ref[...], TN)

    return pl.pallas_call(
        body, name=name, out_shape=jax.ShapeDtypeStruct((c_all.shape[1], dmod_part.shape[1]), F32),
        compiler_params=_params(),
    )(c_all, dmod_part)


def _sum_parts(parts, name):
    _, R, C = parts.shape
    tr = max([t for t in range(16, min(R, 512) + 1, 16) if R % t == 0], default=R)

    def body(p_ref, out_ref):
        acc = p_ref[0].astype(F32)
        for s in range(1, N_DEV):
            acc += p_ref[s].astype(F32)
        out_ref[...] = acc

    return pl.pallas_call(
        body, name=name, grid=(R // tr,),
        in_specs=[pl.BlockSpec((N_DEV, tr, C), lambda i: (0, i, 0))],
        out_specs=pl.BlockSpec((tr, C), lambda i: (i, 0)),
        out_shape=jax.ShapeDtypeStruct((R, C), F32),
        compiler_params=_params("parallel"),
    )(parts)


def _adamw_math(w, g, m, v):
    mm = ADAM_B1 * m + (1.0 - ADAM_B1) * g
    vv = ADAM_B2 * v + (1.0 - ADAM_B2) * (g * g)
    m_hat = mm / (1.0 - ADAM_B1 ** ADAM_STEP)
    v_hat = vv / (1.0 - ADAM_B2 ** ADAM_STEP)
    return -ADAM_LR * (m_hat / (jnp.sqrt(v_hat) + ADAM_EPS) + ADAM_WD * w), mm, vv


def _adamw_small(ws, gs, ms, vs, name):
    n = len(ws)

    def body(*refs):
        for i in range(n):
            d, mm, vv = _adamw_math(*[refs[k * n + i][...] for k in range(4)])
            refs[4 * n + i][...] = d
            refs[5 * n + i][...] = mm
            refs[6 * n + i][...] = vv

    out = pl.pallas_call(
        body, name=name, out_shape=[jax.ShapeDtypeStruct(w.shape, F32) for w in ws] * 3,
        compiler_params=_params(),
    )(*ws, *gs, *ms, *vs)
    return out[:n], out[n:2 * n], out[2 * n:]


def _adamw_sum(parts, w, m, v, name):
    R, C = w.shape
    tr = max([t for t in range(16, min(R, 512) + 1, 16) if R % t == 0], default=R)

    def body(p_ref, w_ref, m_ref, v_ref, g_ref, d_ref, mo_ref, vo_ref):
        g = p_ref[0].astype(F32)
        for s in range(1, N_DEV):
            g += p_ref[s].astype(F32)
        g_ref[...] = g
        d_ref[...], mo_ref[...], vo_ref[...] = _adamw_math(w_ref[...], g, m_ref[...], v_ref[...])

    spec = pl.BlockSpec((tr, C), lambda i: (i, 0))
    return pl.pallas_call(
        body, name=name, grid=(R // tr,),
        in_specs=[pl.BlockSpec((N_DEV, tr, C), lambda i: (0, i, 0)), spec, spec, spec], out_specs=[spec] * 4,
        out_shape=[jax.ShapeDtypeStruct((R, C), F32)] * 4,
        compiler_params=_params("parallel"),
    )(parts, w, m, v)


def _adamw(w, g, m, v, name):
    R, C = w.shape
    tr = max([t for t in range(8, min(R, 512) + 1, 8) if R % t == 0], default=R)

    def body(w_ref, g_ref, m_ref, v_ref, d_ref, mo_ref, vo_ref):
        d_ref[...], mo_ref[...], vo_ref[...] = _adamw_math(w_ref[...], g_ref[...], m_ref[...], v_ref[...])

    spec = pl.BlockSpec((tr, C), lambda i: (i, 0))
    return pl.pallas_call(
        body, name=name, grid=(R // tr,),
        in_specs=[spec] * 4, out_specs=[spec] * 3,
        out_shape=[jax.ShapeDtypeStruct((R, C), F32)] * 3,
        compiler_params=_params("parallel"),
    )(w, g, m, v)


def _weight_grad(a, b, name, dep=None):
    return _matmul([(a, b)], TN, WIRE_DTYPE, name, tm=1408, tn=1024, tk=2048, dep=dep)


def _rows_of(flat, lanes=1024):
    flat = flat.reshape(-1)
    n = -(-flat.shape[0] // lanes) * lanes
    return jnp.pad(flat, (0, n - flat.shape[0])).reshape(n // lanes, lanes)


def _pad_rows(a, rows):
    return jnp.pad(a, ((0, rows - a.shape[0]), (0, 0)))


def kernel(x, c, w_ada, b_ada, norm_ffn1, ffn1_gate, ffn1_up, ffn1_down, norm_mix, w_in, conv_w, a_log, dt_bias, gdn_norm, pool_w, pool_scale, w_out, norm_ffn2, ffn2_gate, ffn2_up, ffn2_down, final_norm, loss_target, m_w_ada, m_b_ada, m_norm_ffn1, m_ffn1_gate, m_ffn1_up, m_ffn1_down, m_norm_mix, m_w_in, m_conv_w, m_a_log, m_dt_bias, m_gdn_norm, m_pool_w, m_pool_scale, m_w_out, m_norm_ffn2, m_ffn2_gate, m_ffn2_up, m_ffn2_down, m_final_norm, v_w_ada, v_b_ada, v_norm_ffn1, v_ffn1_gate, v_ffn1_up, v_ffn1_down, v_norm_mix, v_w_in, v_conv_w, v_a_log, v_dt_bias, v_gdn_norm, v_pool_w, v_pool_scale, v_w_out, v_norm_ffn2, v_ffn2_gate, v_ffn2_up, v_ffn2_down, v_final_norm):
    T, D = x.shape[1], x.shape[2]
    Fs = ffn1_gate.shape[2]
    Ws = w_in.shape[2]
    Ws_pad = -(-Ws // 16) * 16
    Os = w_out.shape[1]
    Ms = w_ada.shape[2]
    Cs = conv_w.shape[2]
    me = 4 * lax.axis_index("x") + 2 * lax.axis_index("y") + lax.axis_index("c")
    x0, target = x[0], loss_target[0]

    def wire(a):
        return a.astype(WIRE_DTYPE)

    def token(started):
        return started[4][:1, :1]

    def with_own(landed, own):
        return lax.dynamic_update_slice(landed, own[None], (me, 0, 0))

    def full(landed):
        return landed.reshape(-1, D).astype(MXU_DTYPE)

    no_dep = jnp.zeros((8, 128), F32)
    small = jnp.concatenate([_pad_rows(c, 8), _pad_rows(jnp.pad(conv_w[0], ((0, 0), (0, D - Cs))), 8)], axis=0)
    got, = _all_gather([small], "gather_small")
    c_all = got[:, 0, :]
    conv_full = jnp.transpose(got[:, 8:8 + CONV_K, :Cs], (1, 0, 2)).reshape(CONV_K, QKV_WIDTH)
    b_part = lax.dynamic_slice(b_ada, (0, me * Ms), (1, Ms))
    mod_part = _mod_part(c_all, w_ada[0], b_part, "mod_part")

    w1 = [wire(ffn1_gate[0].T)]
    w1u = [wire(ffn1_up[0].T)]
    w1d = [wire(ffn1_down[0])]
    w2 = [wire(_pad_rows(w_in[0].T, Ws_pad)), wire(w_out[0])]
    w3 = [wire(ffn2_gate[0].T), wire(ffn2_up[0].T), wire(ffn2_down[0])]
    mod_parts, *w1_all = _all_gather([mod_part] + w1, "gather_mod_w1")
    mod_all = jnp.transpose(mod_parts, (1, 0, 2)).reshape(N_DEV, N_MOD * D)
    mod = lax.dynamic_slice(mod_all, (me, 0), (1, N_MOD * D)).reshape(N_MOD, 1, D)
    sh1, sc1, gt1, sh2, sc2, gt2, sh3, sc3, gt3 = [mod[i] for i in range(N_MOD)]
    wg1_t = full(w1_all[0])
    w1u_sent = _exchange_start(w1u, True, w1_all[0], "w1u_start")
    w1d_sent = _exchange_start(w1d, True, w1u_sent[4], "w1d_start")
    w2_sent = _exchange_start(w2, True, w1d_sent[4], "w2_start")
    w3_sent = _exchange_start(w3, True, w2_sent[4], "w3_start")

    lane_pad = lambda a: jnp.pad(a, ((0, 0), (GDN_HEADS, 128 - 2 * GDN_HEADS)))
    a_log_l, dt_bias_l = lane_pad(a_log), lane_pad(dt_bias)
    pool_w_m = pool_w[0].astype(MXU_DTYPE)

    g1, h1 = _swiglu_gate(x0, wg1_t, norm_ffn1, sc1 + token(w3_sent), sh1, "ffn1_gate")
    wu1_t = full(with_own(_exchange_wait(w1u_sent, h1, True, "w1u_wait")[0], w1u[0]))
    u1, a1 = _swiglu_up(h1, wu1_t, g1, "ffn1_up")
    wd1 = full(with_own(_exchange_wait(w1d_sent, a1, True, "w1d_wait")[0], w1d[0]))
    y1, x1, h2 = _matmul_resid_norm_mod(a1, wd1, x0, gt1, 0.5, norm_mix, sc2, sh2, "ffn1_down")
    w_in_all, wo_all = [with_own(z, own) for z, own in zip(_exchange_wait(w2_sent, h2, True, "w2_wait"), w2)]
    w_in_t = w_in_all[:, :Ws, :].reshape(-1, D).astype(MXU_DTYPE)
    wo = full(wo_all)
    w_in_re = jnp.concatenate([w_in_t[:COL_Z + GDN_WIDTH], w_in_t[D_IN - POOL_WIDTH:],
                               w_in_t[4 * GDN_WIDTH:4 * GDN_WIDTH + 2 * GDN_HEADS],
                               jnp.zeros((128 - 2 * GDN_HEADS, D), MXU_DTYPE)], axis=0)
    proj = _matmul([(h2, w_in_re)], NT, F32, "proj_in", tm=512, tn=D_IN_PAD, tk=D)
    qkv, bg = _gdn_prep(proj, conv_full, a_log_l, dt_bias_l, "gdn_prep")
    tinv, u_c, w_c, qd_c, kd_c, p_c, cd_c = _gdn_chunk_fwd(qkv, bg, "gdn_chunk_fwd")
    o, s_all, vn_c = _gdn_scan_fwd(u_c, w_c, qd_c, kd_c, p_c, cd_c, "gdn_scan_fwd")
    mix_in = _mix_post(o, proj, gdn_norm, pool_w_m, pool_scale, "mix_post")
    mixed, x2, h3 = _matmul_resid_norm_mod(mix_in, wo, x1, gt2, 1.0, norm_ffn2, sc3, sh3, "mix_out")
    wg2_t, wu2_t, wd2 = [full(with_own(z, own))
                         for z, own in zip(_exchange_wait(w3_sent, h3, True, "w3_wait"), w3)]
    g3, u3, a3, loss_row, d3, d_final, dy3, dgt3 = _swiglu_fwd_loss(
        h3, wg2_t, wu2_t, wd2, x2, gt3, final_norm.reshape(1, D), target, "ffn2_fwd_loss")

    dg3, du3, d2, d_n3, dsc3, dsh3, dmixed, dgt2 = _swiglu_bwd(
        dy3, wd2, g3, u3, wg2_t, wu2_t, "ffn2_bwd_norm3_bwd", (x2, norm_ffn2, sc3), d3,
        produced_by=(mixed, gt2, 1.0))
    d_wd2 = _weight_grad(a3, dy3, "ffn2_dwd")
    d_wg2 = _weight_grad(dg3, h3, "ffn2_dwg")
    d_wu2 = _weight_grad(du3, h3, "ffn2_dwu")
    dmix_in = _matmul([(dmixed, wo)], NT, F32, "mix_out_bwd", tm=512, tn=GDN_WIDTH + POOL_WIDTH, tk=D)
    d_wo = _matmul([(mix_in, dmixed)], TN, WIRE_DTYPE, "mix_dwo", tm=GDN_WIDTH + POOL_WIDTH, tn=D, tk=2048)
    do, dzp, d_gn, d_pw, d_ps = _mix_post_bwd(dmix_in, o, proj, gdn_norm, pool_w_m, pool_scale, "mix_post_bwd")
    dvn, dw_c, dqd, dkd, dp_c, dcd = _gdn_scan_bwd(do, w_c, qd_c, kd_c, p_c, cd_c, s_all, vn_c, "gdn_scan_bwd")
    dqkv, dbg = _gdn_chunk_bwd(qkv, bg, tinv, u_c, w_c, dvn, dw_c, dqd, dkd, dp_c, dcd, "gdn_chunk_bwd")
    dproj, d_conv, d_al, d_dtb = _gdn_prep_bwd(proj, conv_full, a_log_l, dt_bias_l, dqkv, dbg, dzp, "gdn_prep_bwd")
    d_win_re = _matmul([(dproj, h2)], TN, WIRE_DTYPE, "proj_in_dw", tm=D_IN_PAD, tn=D, tk=1024)
    d_win_t = jnp.concatenate([d_win_re[:COL_Z + GDN_WIDTH], d_win_re[COL_BA:COL_BA + 2 * GDN_HEADS],
                               d_win_re[COL_P:COL_P + POOL_WIDTH]], axis=0)
    d_win_blocks = jnp.pad(d_win_t.reshape(N_DEV, Ws, D), ((0, 0), (0, Ws_pad - Ws), (0, 0)))
    names23 = ["w_in", "w_out", "ffn2_gate", "ffn2_up", "ffn2_down"]
    parts23 = [wire(d_win_blocks), d_wo.reshape(N_DEV, Os, D), d_wg2.reshape(N_DEV, Fs, D),
               d_wu2.reshape(N_DEV, Fs, D), d_wd2.reshape(N_DEV, Fs, D)]
    own23 = [lax.dynamic_index_in_dim(p, me, 0, keepdims=False) for p in parts23]
    g23_sent = _exchange_start(parts23, False, no_dep, "g23_start")
    d1, d_n2, dsc2, dsh2, dy1, dgt1 = _norm_bwd((dproj, w_in_re), x1, norm_mix, sc2 + token(g23_sent), d2,
                                                "proj_in_bwd_norm2_bwd", produced_by=(y1, gt1, 0.5))
    dg1, du1, grad_x, d_n1, dsc1, dsh1 = _swiglu_bwd(
        dy1, wd1, g1, u1, wg1_t, wu1_t, "ffn1_bwd_norm1_bwd", (x0, norm_ffn1, sc1), d1)

    dmod = jnp.concatenate([dsh1, dsc1, dgt1, dsh2, dsc2, dgt2, dsh3, dsc3, dgt3], axis=0)
    small_rows = [dmod.reshape(-1), d_n1[0], d_n2[0], d_n3[0], d_final[0], d_gn[0], d_ps[0],
                  d_al[0, GDN_HEADS:2 * GDN_HEADS], d_dtb[0, GDN_HEADS:2 * GDN_HEADS], loss_row[0, :1],
                  d_conv.reshape(-1), d_pw.reshape(-1)]
    lanes = 1024
    small_rows = [_rows_of(r, lanes) for r in small_rows]
    n_rows = [r.shape[0] for r in small_rows]
    row_off = [sum(n_rows[:i]) for i in range(len(n_rows))]
    total = -(-sum(n_rows) // 8) * 8
    slab = _pad_rows(jnp.concatenate(small_rows, axis=0), total)
    slab_sent = _exchange_start([slab], True, no_dep, "small_grads_start")

    def send_ffn1(a, b, which, dep):
        parts = _weight_grad(a, b, f"ffn1_{which}", dep=dep).reshape(N_DEV, Fs, D)
        own = lax.dynamic_index_in_dim(parts, me, 0, keepdims=False)
        return _exchange_start([parts], False, no_dep, f"g1_{which}_start"), own

    g1_wg, own_wg = send_ffn1(dg1, h1, "dwg", slab_sent[4])
    g1_wu, own_wu = send_ffn1(du1, h1, "dwu", g1_wg[4])
    g1_wd, own_wd = send_ffn1(a1, dy1, "dwd", g1_wu[4])

    slab_all = with_own(_exchange_wait(slab_sent, g1_wg[4], True, "small_grads_wait")[0], slab)
    summed = _sum_parts(slab_all, "sum_small_grads")

    def piece(idx, n):
        return summed[row_off[idx]:row_off[idx] + n_rows[idx]].reshape(-1)[:n]

    g_b_ada = piece(0, N_MOD * D).reshape(1, N_MOD * D)
    g_n1, g_n2, g_n3 = piece(1, D).reshape(1, D), piece(2, D).reshape(1, D), piece(3, D).reshape(1, D)
    g_final = piece(4, D)
    g_gn = piece(5, HEAD_DIM).reshape(1, HEAD_DIM)
    g_ps = piece(6, POOL_WIDTH).reshape(1, POOL_WIDTH)
    g_al = piece(7, GDN_HEADS).reshape(1, GDN_HEADS)
    g_dtb = piece(8, GDN_HEADS).reshape(1, GDN_HEADS)
    loss = piece(9, 1)[0]
    g_conv = lax.dynamic_slice(piece(10, CONV_K * QKV_WIDTH).reshape(1, CONV_K, QKV_WIDTH), (0, 0, me * Cs),
                               (1, CONV_K, Cs))
    g_pw = piece(11, POOL_GROUPS * 128 * 128).reshape(1, POOL_GROUPS, 128, 128)

    dmod_all = slab_all[:, row_off[0]:row_off[0] + n_rows[0], :].reshape(N_DEV, -1)[:, :N_MOD * D]
    g_w_ada = _w_ada_grad(c_all, lax.dynamic_slice(dmod_all, (0, me * Ms), (N_DEV, Ms)), "w_ada_grad")[None]

    landed23 = _exchange_wait(g23_sent, g1_wd[4], False, "g23_wait")
    parts8 = {n: with_own(z, own) for n, z, own in zip(names23, landed23, own23)}
    g_rows = dict(w_in=_sum_parts(parts8.pop("w_in"), "sum_w_in")[:Ws])
    column_sharded = ("w_in", "ffn1_gate", "ffn1_up", "ffn2_gate", "ffn2_up")

    names = ["w_ada", "b_ada", "norm_ffn1", "ffn1_gate", "ffn1_up", "ffn1_down", "norm_mix", "w_in", "conv_w",
             "a_log", "dt_bias", "gdn_norm", "pool_w", "pool_scale", "w_out", "norm_ffn2", "ffn2_gate", "ffn2_up",
             "ffn2_down", "final_norm"]
    weights = dict(zip(names, [w_ada, b_ada, norm_ffn1, ffn1_gate, ffn1_up, ffn1_down, norm_mix, w_in, conv_w,
                               a_log, dt_bias, gdn_norm, pool_w, pool_scale, w_out, norm_ffn2, ffn2_gate, ffn2_up,
                               ffn2_down, final_norm]))
    ms = dict(zip(names, [m_w_ada, m_b_ada, m_norm_ffn1, m_ffn1_gate, m_ffn1_up, m_ffn1_down, m_norm_mix, m_w_in,
                          m_conv_w, m_a_log, m_dt_bias, m_gdn_norm, m_pool_w, m_pool_scale, m_w_out, m_norm_ffn2,
                          m_ffn2_gate, m_ffn2_up, m_ffn2_down, m_final_norm]))
    vs = dict(zip(names, [v_w_ada, v_b_ada, v_norm_ffn1, v_ffn1_gate, v_ffn1_up, v_ffn1_down, v_norm_mix, v_w_in,
                          v_conv_w, v_a_log, v_dt_bias, v_gdn_norm, v_pool_w, v_pool_scale, v_w_out, v_norm_ffn2,
                          v_ffn2_gate, v_ffn2_up, v_ffn2_down, v_final_norm]))
    grads = dict(w_ada=g_w_ada, b_ada=g_b_ada, norm_ffn1=g_n1, norm_mix=g_n2, conv_w=g_conv,
                 a_log=g_al, dt_bias=g_dtb, gdn_norm=g_gn, pool_w=g_pw, pool_scale=g_ps,
                 norm_ffn2=g_n3, final_norm=g_final)
    delta, new_m, new_v = {}, {}, {}

    def adamw_big(n):
        if n in column_sharded:
            view, back = (lambda a: a[0].T), (lambda a: a.T[None])
        else:
            view, back = (lambda a: a[0]), (lambda a: a[None])
        if n in parts8:
            g, d_, m_, v_ = _adamw_sum(parts8[n], view(weights[n]), view(ms[n]), view(vs[n]), f"adamw_{n}")
        else:
            g = g_rows[n] if n in g_rows else view(grads[n])
            d_, m_, v_ = _adamw(view(weights[n]), g, view(ms[n]), view(vs[n]), f"adamw_{n}")
        grads[n], delta[n], new_m[n], new_v[n] = back(g), back(d_), back(m_), back(v_)

    early = ["w_ada", "w_in", "w_out", "ffn2_gate", "ffn2_up", "ffn2_down"]
    late = ["ffn1_gate", "ffn1_up", "ffn1_down"]
    for n in early:
        adamw_big(n)
    done = sum(delta[n][0, :1, :1] for n in early)

    def arrived(started, own, which):
        landed, = _exchange_wait(started, done, False, f"g1_{which}_wait")
        return with_own(landed, own)

    parts8["ffn1_gate"] = arrived(g1_wg, own_wg, "dwg")
    parts8["ffn1_up"] = arrived(g1_wu, own_wu, "dwu")
    parts8["ffn1_down"] = arrived(g1_wd, own_wd, "dwd")
    for n in late:
        adamw_big(n)
    small_names = [n for n in names if n not in early + late]
    two_d = lambda a: a.reshape(-1, a.shape[-1])
    small_out = _adamw_small(*[[two_d(src[n]) for n in small_names] for src in (weights, grads, ms, vs)],
                             "adamw_small")
    for dst, outs in zip((delta, new_m, new_v), small_out):
        for n, a in zip(small_names, outs):
            dst[n] = a.reshape(weights[n].shape)

    return (loss, grad_x[None], *[grads[n] for n in names], *[delta[n] for n in names],
            *[new_m[n] for n in names], *[new_v[n] for n in names])
```

```python
import functools

import jax
import jax.numpy as jnp
from jax import lax
from jax.experimental import pallas as pl
from jax.experimental.pallas import tpu as pltpu

F32 = jnp.float32
MXU_DTYPE = jnp.bfloat16
WIRE_DTYPE = jnp.bfloat16
EPS = 1e-6
N_DEV = 8
GDN_HEADS = 4
HEAD_DIM = 128
GDN_WIDTH = GDN_HEADS * HEAD_DIM
POOL_WINDOWS = (2, 4, 8, 16)
POOL_GROUPS = len(POOL_WINDOWS)
POOL_WIDTH = 512
CONV_K = 4
CHUNK = 64
QKV_WIDTH = 3 * GDN_WIDTH
D_IN = 4 * GDN_WIDTH + 2 * GDN_HEADS + POOL_WIDTH
D_IN_PAD = 4 * GDN_WIDTH + POOL_WIDTH + 128
COL_Z = QKV_WIDTH
COL_P = 4 * GDN_WIDTH
COL_BA = 4 * GDN_WIDTH + POOL_WIDTH
N_MOD = 9
HALO = 16
VMEM_LIMIT = 56 * 1024 * 1024
ADAM_LR, ADAM_B1, ADAM_B2, ADAM_EPS, ADAM_WD, ADAM_STEP = 0.001, 0.9, 0.999, 1e-08, 0.01, 10
FFN_TOKEN_TILE = 256
FFN_HIDDEN_TILE = 1408
CHUNKS_PER_STEP = 4
SCAN_CHUNKS_PER_STEP = 8

NT = (((1,), (1,)), ((), ()))
NN = (((1,), (0,)), ((), ()))
TN = (((0,), (0,)), ((), ()))


def _params(*sem):
    return pltpu.CompilerParams(dimension_semantics=tuple(sem), vmem_limit_bytes=VMEM_LIMIT)


def _dot(a, b, dims):
    return lax.dot_general(a, b, dims, preferred_element_type=F32)


def _mdot(a, b, dims):
    return _dot(a.astype(MXU_DTYPE), b.astype(MXU_DTYPE), dims)


def _split(a):
    hi = a.astype(jnp.bfloat16)
    return hi, (a - hi.astype(F32)).astype(jnp.bfloat16)


def _dot3(a, b, dims):
    (ah, al), (bh, bl) = a, b
    return (_dot(al, bh, dims) + _dot(ah, bl, dims)) + _dot(ah, bh, dims)


def _sigmoid(v):
    return 0.5 * jnp.tanh(0.5 * v) + 0.5


def _softplus(v):
    return jnp.maximum(v, 0.0) + jnp.log(1.0 + jnp.exp(-jnp.abs(v)))


def _shift_rows(v, s):
    n = v.shape[0]
    s = s % n
    return v if s == 0 else pltpu.roll(v, s, 0)


def _tile(n, want):
    t = min(n, want)
    while n % t:
        t //= 2
    return t


def _all_gather(blocks, name, dep=None):
    n = len(blocks)

    def body(*refs):
        x_refs, out_refs = refs[:n], refs[-3 - n:-3]
        send_sems, recv_sems, local_sems = refs[-3:]
        x, y, c = lax.axis_index("x"), lax.axis_index("y"), lax.axis_index("c")
        me, sibling = (x, y, c), (x, y, 1 - c)
        chips = [(1 - x, y), (x, 1 - y), (1 - x, 1 - y)]

        def copy(a, k, blk, to, own=False):
            rows = out_refs[a].at[4 * blk[0] + 2 * blk[1] + blk[2]]
            return pltpu.make_async_remote_copy(
                src_ref=x_refs[a] if own else rows, dst_ref=rows,
                send_sem=send_sems.at[7 * a + k], recv_sem=recv_sems.at[7 * a + k],
                device_id=to, device_id_type=pl.DeviceIdType.MESH)

        mine = [pltpu.make_async_copy(x_refs[a], out_refs[a].at[4 * x + 2 * y + c], local_sems.at[a])
                for a in range(n)]
        for cp in mine:
            cp.start()
        sent = []
        for a in range(n):
            sent.append(copy(a, 0, me, sibling, own=True))
            sent += [copy(a, 1 + j, me, (*chip, c), own=True) for j, chip in enumerate(chips)]
        for cp in sent:
            cp.start()
        for a in range(n):
            for j, chip in enumerate(chips):
                copy(a, 1 + j, (*chip, c), me).wait_recv()
                passed = copy(a, 4 + j, (*chip, c), sibling)
                passed.start()
                sent.append(passed)
        for a in range(n):
            copy(a, 0, sibling, me).wait_recv()
            for j, chip in enumerate(chips):
                copy(a, 4 + j, (*chip, 1 - c), me).wait_recv()
        for cp in sent:
            cp.wait_send()
        for cp in mine:
            cp.wait()

    hbm = pl.BlockSpec(memory_space=pltpu.HBM)
    return pl.pallas_call(
        body, name=name,
        out_shape=[jax.ShapeDtypeStruct((N_DEV,) + b.shape, b.dtype) for b in blocks],
        in_specs=[hbm] * n + [pl.BlockSpec(memory_space=pl.ANY)] * (dep is not None),
        out_specs=[hbm] * n,
        scratch_shapes=[pltpu.SemaphoreType.DMA((7 * n,)), pltpu.SemaphoreType.DMA((7 * n,)),
                        pltpu.SemaphoreType.DMA((n,))],
    )(*(list(blocks) + ([] if dep is None else [dep])))


_HBM = pl.BlockSpec(memory_space=pltpu.HBM)
_SEM = pl.BlockSpec(memory_space=pltpu.SEMAPHORE)
_ANY = pl.BlockSpec(memory_space=pl.ANY)
_EFFECT = pltpu.SideEffectType.DATAFLOW_SIDE_EFFECTING
_FLIPS = [(0, 0, 1), (0, 1, 0), (0, 1, 1), (1, 0, 0), (1, 0, 1), (1, 1, 0), (1, 1, 1)]


def _peers():
    x, y, c = lax.axis_index("x"), lax.axis_index("y"), lax.axis_index("c")
    return 4 * x + 2 * y + c, [(1 - x if fx else x, 1 - y if fy else y, 1 - c if fc else c)
                               for fx, fy, fc in _FLIPS]


def _exchange_start(srcs, gather, dep, name):
    n = len(srcs)
    lands = [(N_DEV,) + tuple(s.shape if gather else s.shape[1:]) for s in srcs]

    def body(*refs):
        src_refs, land_refs = refs[:n], refs[n:2 * n]
        send_sems, recv_sems = refs[2 * n + 1], refs[2 * n + 2]
        token = refs[-1]
        me, peers = _peers()
        for a in range(n):
            for k, (px, py, pc) in enumerate(peers):
                pltpu.make_async_remote_copy(
                    src_ref=src_refs[a] if gather else src_refs[a].at[4 * px + 2 * py + pc],
                    dst_ref=land_refs[a].at[me],
                    send_sem=send_sems.at[7 * a + k], recv_sem=recv_sems.at[7 * a + k],
                    device_id=(px, py, pc), device_id_type=pl.DeviceIdType.MESH).start()
        token[...] = jnp.zeros_like(token)

    srcs = [pltpu.with_memory_space_constraint(s, pltpu.HBM) for s in srcs]
    empties = [pltpu.with_memory_space_constraint(lax.empty(shape, s.dtype), pltpu.HBM)
               for shape, s in zip(lands, srcs)]
    out = pl.pallas_call(
        body, name=name,
        out_shape=(pltpu.SemaphoreType.DMA((7 * n,)), pltpu.SemaphoreType.DMA((7 * n,)),
                   *[pltpu.HBM(shape, s.dtype) for shape, s in zip(lands, srcs)],
                   jax.ShapeDtypeStruct((8, 128), F32)),
        in_specs=(*[_HBM] * (2 * n), _ANY),
        out_specs=(_SEM, _SEM, *[_HBM] * n, pl.BlockSpec(memory_space=pltpu.VMEM)),
        input_output_aliases={n + a: 2 + a for a in range(n)},
        compiler_params=pltpu.CompilerParams(has_side_effects=_EFFECT),
    )(*srcs, *empties, dep)
    return out[0], out[1], srcs, list(out[2:2 + n]), out[-1]


def _exchange_wait(started, after, gather, name):
    send_sems, recv_sems, srcs, lands, _ = started
    n = len(srcs)

    def body(*refs):
        src_refs, land_refs = refs[:n], refs[n:2 * n]
        send_sems, recv_sems = refs[2 * n], refs[2 * n + 1]
        _, peers = _peers()
        for a in range(n):
            for k, peer in enumerate(peers):
                copy = pltpu.make_async_remote_copy(
                    src_ref=src_refs[a] if gather else src_refs[a].at[0], dst_ref=land_refs[a].at[0],
                    send_sem=send_sems.at[7 * a + k], recv_sem=recv_sems.at[7 * a + k],
                    device_id=peer, device_id_type=pl.DeviceIdType.MESH)
                copy.wait_send()
                copy.wait_recv()

    out = pl.pallas_call(
        body, name=name,
        out_shape=[pltpu.HBM(z.shape, z.dtype) for z in lands],
        in_specs=(*[_HBM] * (2 * n), _SEM, _SEM, _ANY), out_specs=[_HBM] * n,
        input_output_aliases={n + a: a for a in range(n)},
        compiler_params=pltpu.CompilerParams(has_side_effects=_EFFECT),
    )(*srcs, *lands, send_sems, recv_sems, after)
    return list(out)


def _matmul(pairs, dims, out_dtype, name, tm=512, tn=512, tk=512, dep=None):
    a0, b0 = pairs[0]
    if dims == TN:
        K, M = a0.shape
    else:
        M, K = a0.shape
    N = b0.shape[0] if dims == NT else b0.shape[1]
    tm, tn, tk = _tile(M, tm), _tile(N, tn), _tile(K, tk)
    nk = K // tk
    n_pairs = len(pairs)
    n_in = 2 * n_pairs + (dep is not None)

    def body(*refs):
        out_ref = refs[n_in]

        def product():
            total = _dot(refs[0][...], refs[1][...], dims)
            for p in range(1, n_pairs):
                total += _dot(refs[2 * p][...], refs[2 * p + 1][...], dims)
            return total

        if nk == 1:
            out_ref[...] = product().astype(out_ref.dtype)
            return
        acc_ref = refs[n_in + 1]
        k = pl.program_id(2)

        @pl.when(k == 0)
        def _():
            acc_ref[...] = product()

        @pl.when((k > 0) & (k < nk - 1))
        def _():
            acc_ref[...] += product()

        @pl.when(k == nk - 1)
        def _():
            out_ref[...] = (acc_ref[...] + product()).astype(out_ref.dtype)

    if dims == TN:
        a_spec = pl.BlockSpec((tk, tm), lambda i, j, k: (k, i))
    else:
        a_spec = pl.BlockSpec((tm, tk), lambda i, j, k: (i, k))
    if dims == NT:
        b_spec = pl.BlockSpec((tn, tk), lambda i, j, k: (j, k))
    else:
        b_spec = pl.BlockSpec((tk, tn), lambda i, j, k: (k, j))
    args, specs = [], []
    for a, b in pairs:
        args += [a, b]
        specs += [a_spec, b_spec]
    if dep is not None:
        args.append(dep)
        specs.append(_ANY)
    return pl.pallas_call(
        body, name=name, grid=(M // tm, N // tn, nk),
        in_specs=specs, out_specs=pl.BlockSpec((tm, tn), lambda i, j, k: (i, j)),
        out_shape=jax.ShapeDtypeStruct((M, N), out_dtype),
        scratch_shapes=[pltpu.VMEM((tm, tn), F32)] * (nk > 1),
        compiler_params=_params("parallel", "parallel", "arbitrary"),
    )(*args)


def _vec_spec(d):
    return pl.BlockSpec((1, d), lambda i: (0, 0))


def _matmul_resid_norm_mod(a, b, x, gate, coef, nw, scale, shift, name):
    T, K = a.shape
    D = b.shape[1]
    tm = _tile(T, 512)

    def body(a_ref, b_ref, x_ref, g_ref, nw_ref, sc_ref, sh_ref, y_ref, xo_ref, h_ref):
        y = _dot(a_ref[...], b_ref[...], NN)
        y_ref[...] = y
        xf = x_ref[...] + (coef * g_ref[...]) * y
        xo_ref[...] = xf
        r = lax.rsqrt(jnp.mean(xf * xf, axis=-1, keepdims=True) + EPS)
        h_ref[...] = ((xf * r) * nw_ref[...] * (1.0 + sc_ref[...]) + sh_ref[...]).astype(h_ref.dtype)

    row = pl.BlockSpec((tm, D), lambda i: (i, 0))
    vec = _vec_spec(D)
    return pl.pallas_call(
        body, name=name, grid=(T // tm,),
        in_specs=[pl.BlockSpec((tm, K), lambda i: (i, 0)), _resident((K, D)), row, vec, vec, vec, vec],
        out_specs=[row, row, row],
        out_shape=[jax.ShapeDtypeStruct((T, D), F32), jax.ShapeDtypeStruct((T, D), F32),
                   jax.ShapeDtypeStruct((T, D), MXU_DTYPE)],
        compiler_params=_params("parallel"),
    )(a, b, x, gate, nw, scale, shift)


def _norm_bwd(dh, x, nw, scale, dres, name, produced_by=None):
    T, D = x.shape
    tm = _tile(T, 512)

    def body(*refs):
        if isinstance(dh, tuple):
            dh_value = _dot(refs[0][...], refs[1][...], NN)
            refs = refs[1:]
        else:
            dh_value = refs[0][...]
        _, x_ref, nw_ref, sc_ref, dr_ref = refs[:5]
        n_in = 5 if produced_by is None else 7
        dx_ref, dnw_ref, dsc_ref, dsh_ref = refs[n_in:n_in + 4]

        @pl.when(pl.program_id(0) == 0)
        def _():
            dnw_ref[...] = jnp.zeros_like(dnw_ref)
            dsc_ref[...] = jnp.zeros_like(dsc_ref)
            dsh_ref[...] = jnp.zeros_like(dsh_ref)
            if produced_by is not None:
                refs[n_in + 5][...] = jnp.zeros_like(refs[n_in + 5])

        xf, dh_ = x_ref[...], dh_value
        r = lax.rsqrt(jnp.mean(xf * xf, axis=-1, keepdims=True) + EPS)
        xn = xf * r
        one_sc = 1.0 + sc_ref[...]
        dsh_ref[...] += jnp.sum(dh_, axis=0, keepdims=True)
        t = dh_ * xn
        dsc_ref[...] += jnp.sum(t, axis=0, keepdims=True) * nw_ref[...]
        dnw_ref[...] += jnp.sum(t, axis=0, keepdims=True) * one_sc
        dxn = dh_ * (nw_ref[...] * one_sc)
        dx = dr_ref[...] + r * (dxn - xn * jnp.mean(dxn * xn, axis=-1, keepdims=True))
        dx_ref[...] = dx
        if produced_by is not None:
            y_ref, g_ref, dy_ref, dg_ref = refs[5], refs[6], refs[n_in + 4], refs[n_in + 5]
            dy_ref[...] = ((produced_by[2] * g_ref[...]) * dx).astype(dy_ref.dtype)
            dg_ref[...] += produced_by[2] * jnp.sum(dx * y_ref[...], axis=0, keepdims=True)

    row = pl.BlockSpec((tm, D), lambda i: (i, 0))
    vec = _vec_spec(D)
    vec_out = jax.ShapeDtypeStruct((1, D), F32)
    if isinstance(dh, tuple):
        k_dim = dh[0].shape[1]
        args, in_specs = [dh[0], dh[1]], [pl.BlockSpec((tm, k_dim), lambda i: (i, 0)), _resident((k_dim, D))]
    else:
        args, in_specs = [dh], [row]
    args, in_specs = args + [x, nw, scale, dres], in_specs + [row, vec, vec, row]
    out_specs, out_shape = [row, vec, vec, vec], [jax.ShapeDtypeStruct((T, D), F32), vec_out, vec_out, vec_out]
    if produced_by is not None:
        args += [produced_by[0], produced_by[1]]
        in_specs += [row, vec]
        out_specs += [row, vec]
        out_shape += [jax.ShapeDtypeStruct((T, D), MXU_DTYPE), vec_out]
    return pl.pallas_call(
        body, name=name, grid=(T // tm,),
        in_specs=in_specs, out_specs=out_specs, out_shape=out_shape,
        compiler_params=_params("arbitrary"),
    )(*args)


def _rms(xf):
    r = lax.rsqrt(jnp.mean(xf * xf, axis=-1, keepdims=True) + EPS)
    return r, xf * r


def _norm_bwd_math(dh, xf, nw, sc):
    r, xn = _rms(xf)
    t = dh * xn
    dxn = dh * (nw * (1.0 + sc))
    dx = r * (dxn - xn * jnp.mean(dxn * xn, axis=-1, keepdims=True))
    return dx, jnp.sum(dh, axis=0, keepdims=True), jnp.sum(t, axis=0, keepdims=True)


def _swiglu_gate(x, wg_t, nw, scale, shift, name):
    T, D = x.shape
    Fdim = wg_t.shape[0]
    tm = _tile(T, 512)

    def body(x_ref, wg_ref, nw_ref, sc_ref, sh_ref, g_ref, h_ref):
        _, xn = _rms(x_ref[...])
        hh = (xn * nw_ref[...] * (1.0 + sc_ref[...]) + sh_ref[...]).astype(MXU_DTYPE)
        h_ref[...] = hh
        g_ref[...] = _dot(hh, wg_ref[...], NT).astype(g_ref.dtype)

    row = pl.BlockSpec((tm, D), lambda i: (i, 0))
    vec = _vec_spec(D)
    return pl.pallas_call(
        body, name=name, grid=(T // tm,),
        in_specs=[row, _resident((Fdim, D)), vec, vec, vec],
        out_specs=[pl.BlockSpec((tm, Fdim), lambda i: (i, 0)), row],
        out_shape=[jax.ShapeDtypeStruct((T, Fdim), MXU_DTYPE), jax.ShapeDtypeStruct((T, D), MXU_DTYPE)],
        compiler_params=_params("parallel"),
    )(x, wg_t, nw, scale, shift)


def _swiglu_up(h, wu_t, g, name):
    T, D = h.shape
    Fdim = wu_t.shape[0]
    tm = _tile(T, 512)

    def body(h_ref, wu_ref, g_ref, u_ref, a_ref):
        u = _dot(h_ref[...], wu_ref[...], NT)
        g = g_ref[...].astype(F32)
        u_ref[...] = u.astype(u_ref.dtype)
        a_ref[...] = ((g * _sigmoid(g)) * u).astype(a_ref.dtype)

    frow = pl.BlockSpec((tm, Fdim), lambda i: (i, 0))
    return pl.pallas_call(
        body, name=name, grid=(T // tm,),
        in_specs=[pl.BlockSpec((tm, D), lambda i: (i, 0)), _resident((Fdim, D)), frow],
        out_specs=[frow, frow],
        out_shape=[jax.ShapeDtypeStruct((T, Fdim), MXU_DTYPE)] * 2,
        compiler_params=_params("parallel"),
    )(h, wu_t, g)


def _swiglu_fwd_loss(h, wg_t, wu_t, wd, x, gate, fw, target, name):
    T, D = h.shape
    Fdim = wd.shape[0]
    tm, tf = _tile(T, FFN_TOKEN_TILE), _tile(Fdim, FFN_HIDDEN_TILE)
    row = pl.BlockSpec((tm, D), lambda i: (i, 0))
    frow = pl.BlockSpec((tm, Fdim), lambda i: (i, 0))
    vec, wres = _vec_spec(D), _resident((Fdim, D))
    vec_out = jax.ShapeDtypeStruct((1, D), F32)

    def body(h_ref, wg_ref, wu_ref, wd_ref, x_ref, gt_ref, fw_ref, t_ref,
             g_ref, u_ref, a_ref, loss_ref, dx_ref, dfw_ref, dy_ref, dg_ref):
        hh = h_ref[...]
        y = None
        for k in range(Fdim // tf):
            ks = slice(k * tf, (k + 1) * tf)
            g = _dot(hh, wg_ref[ks, :], NT)
            u = _dot(hh, wu_ref[ks, :], NT)
            a = ((g * _sigmoid(g)) * u).astype(a_ref.dtype)
            g_ref[:, ks] = g.astype(g_ref.dtype)
            u_ref[:, ks] = u.astype(u_ref.dtype)
            a_ref[:, ks] = a
            part = _dot(a, wd_ref[ks, :], NN)
            y = part if y is None else y + part

        @pl.when(pl.program_id(0) == 0)
        def _():
            loss_ref[...] = jnp.zeros_like(loss_ref)
            dfw_ref[...] = jnp.zeros_like(dfw_ref)
            dg_ref[...] = jnp.zeros_like(dg_ref)

        r, xn = _rms(x_ref[...] + (0.5 * gt_ref[...]) * y)
        err = xn * fw_ref[...] - t_ref[...]
        per_tok = jnp.mean(err * err, axis=-1, keepdims=True)
        loss_ref[...] += 0.5 * jnp.sum(per_tok, axis=0, keepdims=True)
        d_out = err * (1.0 / D)
        dfw_ref[...] += jnp.sum(d_out * xn, axis=0, keepdims=True)
        dxn = d_out * fw_ref[...]
        dx = r * (dxn - xn * jnp.mean(dxn * xn, axis=-1, keepdims=True))
        dx_ref[...] = dx
        dy_ref[...] = ((0.5 * gt_ref[...]) * dx).astype(dy_ref.dtype)
        dg_ref[...] += 0.5 * jnp.sum(dx * y, axis=0, keepdims=True)

    return pl.pallas_call(
        body, name=name, grid=(T // tm,),
        in_specs=[row, wres, wres, wres, row, vec, vec, row],
        out_specs=[frow, frow, frow, pl.BlockSpec((1, 128), lambda i: (0, 0)), row, vec, row, vec],
        out_shape=[jax.ShapeDtypeStruct((T, Fdim), MXU_DTYPE)] * 3
        + [jax.ShapeDtypeStruct((1, 128), F32), jax.ShapeDtypeStruct((T, D), F32), vec_out,
           jax.ShapeDtypeStruct((T, D), MXU_DTYPE), vec_out],
        compiler_params=_params("arbitrary"),
    )(h, wg_t, wu_t, wd, x, gate, fw, target)


def _swiglu_bwd(dy, wd, g, u, wg_t, wu_t, name, norm_in, dres, produced_by=None):
    T, D = dy.shape
    Fdim = wd.shape[0]
    tm, tf = _tile(T, FFN_TOKEN_TILE), _tile(Fdim, FFN_HIDDEN_TILE)
    row = pl.BlockSpec((tm, D), lambda i: (i, 0))
    frow = pl.BlockSpec((tm, Fdim), lambda i: (i, 0))
    vec, wres = _vec_spec(D), _resident((Fdim, D))
    vec_out = jax.ShapeDtypeStruct((1, D), F32)
    args, in_specs = [dy, wd, g, u, wg_t, wu_t, *norm_in, dres], [row, wres, frow, frow, wres, wres, row, vec, vec, row]
    out_shape = [jax.ShapeDtypeStruct((T, Fdim), MXU_DTYPE)] * 2 + [jax.ShapeDtypeStruct((T, D), F32)] + [vec_out] * 3
    out_specs = [frow, frow, row, vec, vec, vec]
    if produced_by is not None:
        args += [produced_by[0], produced_by[1]]
        in_specs += [row, vec]
        out_shape += [jax.ShapeDtypeStruct((T, D), MXU_DTYPE), vec_out]
        out_specs += [row, vec]

    def body(*refs):
        it = iter(refs)
        dy_ref, wd_ref, g_ref, u_ref, wg_ref, wu_ref, x_ref, nw_ref, sc_ref, dr_ref = [next(it) for _ in range(10)]
        prev_refs = [next(it), next(it)] if produced_by is not None else None
        dg_ref, du_ref, dx_ref, dnw_ref, dsc_ref, dsh_ref = [next(it) for _ in range(6)]
        prev_out = [next(it), next(it)] if produced_by is not None else None

        @pl.when(pl.program_id(0) == 0)
        def _():
            dnw_ref[...] = jnp.zeros_like(dnw_ref)
            dsc_ref[...] = jnp.zeros_like(dsc_ref)
            dsh_ref[...] = jnp.zeros_like(dsh_ref)
            if produced_by is not None:
                prev_out[1][...] = jnp.zeros_like(prev_out[1])

        dyy = dy_ref[...]
        dh = None
        for k in range(Fdim // tf):
            ks = slice(k * tf, (k + 1) * tf)
            da = _dot(dyy, wd_ref[ks, :], NT)
            gg = g_ref[:, ks].astype(F32)
            sig = _sigmoid(gg)
            dg = (da * u_ref[:, ks].astype(F32) * (sig * (1.0 + gg * (1.0 - sig)))).astype(dg_ref.dtype)
            du = (da * (gg * sig)).astype(du_ref.dtype)
            dg_ref[:, ks] = dg
            du_ref[:, ks] = du
            part = _dot(dg, wg_ref[ks, :], NN) + _dot(du, wu_ref[ks, :], NN)
            dh = part if dh is None else dh + part
        dx_norm, dsh_row, t_row = _norm_bwd_math(dh, x_ref[...], nw_ref[...], sc_ref[...])
        dsh_ref[...] += dsh_row
        dsc_ref[...] += t_row * nw_ref[...]
        dnw_ref[...] += t_row * (1.0 + sc_ref[...])
        dx = dr_ref[...] + dx_norm
        dx_ref[...] = dx
        if produced_by is not None:
            prev_out[0][...] = ((produced_by[2] * prev_refs[1][...]) * dx).astype(prev_out[0].dtype)
            prev_out[1][...] += produced_by[2] * jnp.sum(dx * prev_refs[0][...], axis=0, keepdims=True)

    return pl.pallas_call(
        body, name=name, grid=(T // tm,), in_specs=in_specs, out_specs=out_specs, out_shape=out_shape,
        compiler_params=_params("arbitrary"),
    )(*args)


def _resident(shape):
    return pl.BlockSpec(shape, lambda i: (0,) * len(shape), pipeline_mode=pl.Buffered(1))


def _conv_act(window, w):
    y = window * w[CONV_K - 1:CONV_K, :]
    for j in range(CONV_K - 1):
        y += _shift_rows(window, CONV_K - 1 - j) * w[j:j + 1, :]
    return y


def _gdn_prep(proj, conv_w, a_log_l, dt_bias_l, name):
    T = proj.shape[0]
    tm = _tile(T, 512)
    hb = tm // 8

    def body(cur_ref, halo_ref, ba_ref, w_ref, al_ref, dtb_ref, qkv_ref, bg_ref):
        i = pl.program_id(0)
        halo = jnp.where(i == 0, 0.0, halo_ref[...])
        window = jnp.concatenate([halo, cur_ref[...]], axis=0)
        y = _conv_act(window, w_ref[...])[8:, :]
        act = y * _sigmoid(y)
        for hh in range(3 * GDN_HEADS):
            blk = act[:, hh * HEAD_DIM:(hh + 1) * HEAD_DIM]
            if hh < 2 * GDN_HEADS:
                rn = lax.rsqrt(jnp.sum(blk * blk, axis=-1, keepdims=True) + EPS)
                blk = blk * rn
                if hh < GDN_HEADS:
                    blk = blk * (HEAD_DIM ** -0.5)
            qkv_ref[:, hh * HEAD_DIM:(hh + 1) * HEAD_DIM] = blk
        ba = ba_ref[...]
        lane = lax.broadcasted_iota(jnp.int32, ba.shape, 1)
        beta = _sigmoid(ba)
        g = -jnp.exp(al_ref[...]) * _softplus(ba + dtb_ref[...])
        bg_ref[...] = jnp.where(lane < GDN_HEADS, beta, jnp.where(lane < 2 * GDN_HEADS, g, 0.0))

    return pl.pallas_call(
        body, name=name, grid=(T // tm,),
        in_specs=[pl.BlockSpec((tm, QKV_WIDTH), lambda i: (i, 0)),
                  pl.BlockSpec((8, QKV_WIDTH), lambda i: (jnp.maximum(i * hb - 1, 0), 0)),
                  pl.BlockSpec((tm, 128), lambda i: (i, COL_BA // 128)),
                  pl.BlockSpec((CONV_K, QKV_WIDTH), lambda i: (0, 0)),
                  pl.BlockSpec((1, 128), lambda i: (0, 0)), pl.BlockSpec((1, 128), lambda i: (0, 0))],
        out_specs=[pl.BlockSpec((tm, QKV_WIDTH), lambda i: (i, 0)), pl.BlockSpec((tm, 128), lambda i: (i, 0))],
        out_shape=[jax.ShapeDtypeStruct((T, QKV_WIDTH), F32), jax.ShapeDtypeStruct((T, 128), F32)],
        compiler_params=_params("parallel"),
    )(proj, proj, proj, conv_w, a_log_l, dt_bias_l)


def _chunk_cumsum(v, reverse=False):
    row = lax.broadcasted_iota(jnp.int32, v.shape, 0)
    s = 1
    while s < CHUNK:
        if reverse:
            v = v + jnp.where(row < CHUNK - s, _shift_rows(v, -s), 0.0)
        else:
            v = v + jnp.where(row >= s, _shift_rows(v, s), 0.0)
        s *= 2
    return v


def _row_form(cols):
    padded = jnp.concatenate([cols, jnp.zeros((128 - CHUNK, 128), F32)], axis=0)
    return padded.T[:, :CHUNK]


def _chunk_masks():
    ri = lax.broadcasted_iota(jnp.int32, (CHUNK, CHUNK), 0)
    ci = lax.broadcasted_iota(jnp.int32, (CHUNK, CHUNK), 1)
    return ri >= ci, ri > ci, (ri == ci).astype(F32)


def _unit_lower_inverses(ms, eye):
    rs = [eye - m for m in ms]
    ps = [_split(-m) for m in ms]
    s = 2
    while s < CHUNK:
        ps = [_split(_dot3(p, p, NN)) for p in ps]
        r_parts = [_split(r) for r in rs]
        rs = [r + _dot3(p, rp, NN) for r, p, rp in zip(rs, ps, r_parts)]
        s *= 2
    return rs


def _head_elementwise(k, beta, gc, gcr, causal):
    decay = jnp.where(causal, jnp.exp(jnp.where(causal, gc - gcr, 0.0)), 0.0)
    return decay, k * beta, jnp.exp(gc)


def _head_slices(hh):
    return (slice(hh * HEAD_DIM, (hh + 1) * HEAD_DIM),
            slice(GDN_WIDTH + hh * HEAD_DIM, GDN_WIDTH + (hh + 1) * HEAD_DIM),
            slice(2 * GDN_WIDTH + hh * HEAD_DIM, 2 * GDN_WIDTH + (hh + 1) * HEAD_DIM))


def _gdn_chunk_fwd(qkv, bg, name):
    T = qkv.shape[0]
    cb = _tile(T // CHUNK, CHUNKS_PER_STEP)
    rows = cb * CHUNK

    def body(qkv_ref, bg_ref, tinv_ref, u_ref, w_ref, qd_ref, kd_ref, p_ref, cd_ref):
        masks = _chunk_masks()
        causal, strict, eye = masks
        heads = []
        for ci in range(cb):
            rs = slice(ci * CHUNK, (ci + 1) * CHUNK)
            bgv = bg_ref[rs, :]
            gc_all = _chunk_cumsum(bgv)
            gc_rows = _row_form(gc_all)
            cd_ref[rs, :] = jnp.exp(jnp.broadcast_to(gc_all[CHUNK - 1:CHUNK, :], (CHUNK, 128)))
            for hh in range(GDN_HEADS):
                qs, ks, vs = _head_slices(hh)
                q, k, v = qkv_ref[rs, qs], qkv_ref[rs, ks], qkv_ref[rs, vs]
                beta = bgv[:, hh:hh + 1]
                gc = gc_all[:, GDN_HEADS + hh:GDN_HEADS + hh + 1]
                decay, kb, eg = _head_elementwise(k, beta, gc, gc_rows[GDN_HEADS + hh:GDN_HEADS + hh + 1, :], causal)
                hs = slice(hh * HEAD_DIM, (hh + 1) * HEAD_DIM)
                cs = slice(hh * CHUNK, (hh + 1) * CHUNK)
                qd_ref[rs, hs] = (q * eg).astype(qd_ref.dtype)
                kd_ref[rs, hs] = (k * jnp.exp(gc[CHUNK - 1:CHUNK, :] - gc)).astype(kd_ref.dtype)
                heads.append((rs, hs, cs, q, k, v * beta, kb, kb * eg, decay))
        kks = [_mdot(kb, k, NT) for (_, _, _, _, k, _, kb, _, _) in heads]
        qks = [_mdot(q, k, NT) for (_, _, _, q, k, _, _, _, _) in heads]
        tinvs = _unit_lower_inverses([jnp.where(strict, kk * hd[8], 0.0) for kk, hd in zip(kks, heads)], eye)
        t_parts = [_split(t) for t in tinvs]
        us = [_dot3(tp, _split(hd[5]), NN) for tp, hd in zip(t_parts, heads)]
        ws = [_dot3(tp, _split(hd[7]), NN) for tp, hd in zip(t_parts, heads)]
        for hd, tinv, u, w, qk in zip(heads, tinvs, us, ws, qks):
            rs, hs, cs = hd[0], hd[1], hd[2]
            tinv_ref[rs, cs] = tinv
            u_ref[rs, hs] = u
            w_ref[rs, hs] = w.astype(w_ref.dtype)
            p_ref[rs, cs] = jnp.where(causal, qk * hd[8], 0.0).astype(p_ref.dtype)

    def spec(width):
        return pl.BlockSpec((rows, width), lambda n: (n, 0))

    hw, cw = GDN_WIDTH, GDN_HEADS * CHUNK
    return pl.pallas_call(
        body, name=name, grid=(T // rows,),
        in_specs=[spec(QKV_WIDTH), spec(128)],
        out_specs=[spec(cw), spec(hw), spec(hw), spec(hw), spec(hw), spec(cw), spec(128)],
        out_shape=[jax.ShapeDtypeStruct((T, cw), F32), jax.ShapeDtypeStruct((T, hw), F32),
                   jax.ShapeDtypeStruct((T, hw), MXU_DTYPE), jax.ShapeDtypeStruct((T, hw), MXU_DTYPE),
                   jax.ShapeDtypeStruct((T, hw), MXU_DTYPE), jax.ShapeDtypeStruct((T, cw), MXU_DTYPE),
                   jax.ShapeDtypeStruct((T, 128), F32)],
        compiler_params=_params("parallel"),
    )(qkv, bg)


def _gdn_scan_fwd(u, w, qd, kd, p, cd, name):
    T = u.shape[0]
    cb = _tile(T // CHUNK, SCAN_CHUNKS_PER_STEP)
    rows = cb * CHUNK

    def body(u_ref, w_ref, qd_ref, kd_ref, p_ref, cd_ref, o_ref, s_all_ref, vn_ref, s_ref):
        @pl.when(pl.program_id(0) == 0)
        def _():
            s_ref[...] = jnp.zeros_like(s_ref)

        hss = [slice(hh * HEAD_DIM, (hh + 1) * HEAD_DIM) for hh in range(GDN_HEADS)]
        css = [slice(hh * CHUNK, (hh + 1) * CHUNK) for hh in range(GDN_HEADS)]
        s_cur = [s_ref[hh] for hh in range(GDN_HEADS)]
        for ci in range(cb):
            rs = slice(ci * CHUNK, (ci + 1) * CHUNK)
            for hh in range(GDN_HEADS):
                s_all_ref[ci * GDN_WIDTH + hh * HEAD_DIM:ci * GDN_WIDTH + (hh + 1) * HEAD_DIM, :] = s_cur[hh]
            s_ms = [s.astype(MXU_DTYPE) for s in s_cur]
            w_s = [_dot(w_ref[rs, hs], s_m, NN) for hs, s_m in zip(hss, s_ms)]
            q_s = [_dot(qd_ref[rs, hs], s_m, NN) for hs, s_m in zip(hss, s_ms)]
            v_ms = [(u_ref[rs, hs] - ws_).astype(MXU_DTYPE) for hs, ws_ in zip(hss, w_s)]
            k_v = [_dot(kd_ref[rs, hs], v_m, TN) for hs, v_m in zip(hss, v_ms)]
            p_v = [_dot(p_ref[rs, cs], v_m, NN) for cs, v_m in zip(css, v_ms)]
            for hh in range(GDN_HEADS):
                vn_ref[rs, hss[hh]] = v_ms[hh]
                o_ref[rs, hss[hh]] = q_s[hh] + p_v[hh]
                c_dec = cd_ref[ci * CHUNK:ci * CHUNK + 1, GDN_HEADS + hh:GDN_HEADS + hh + 1]
                s_cur[hh] = s_cur[hh] * c_dec + k_v[hh]
        for hh in range(GDN_HEADS):
            s_ref[hh] = s_cur[hh]

    def spec(width):
        return pl.BlockSpec((rows, width), lambda n: (n, 0))

    hw, cw = GDN_WIDTH, GDN_HEADS * CHUNK
    return pl.pallas_call(
        body, name=name, grid=(T // rows,),
        in_specs=[spec(hw), spec(hw), spec(hw), spec(hw), spec(cw), spec(128)],
        out_specs=[spec(hw), pl.BlockSpec((cb * GDN_WIDTH, HEAD_DIM), lambda n: (n, 0)), spec(hw)],
        out_shape=[jax.ShapeDtypeStruct((T, hw), F32),
                   jax.ShapeDtypeStruct((T // CHUNK * GDN_WIDTH, HEAD_DIM), F32),
                   jax.ShapeDtypeStruct((T, hw), MXU_DTYPE)],
        scratch_shapes=[pltpu.VMEM((GDN_HEADS, HEAD_DIM, HEAD_DIM), F32)],
        compiler_params=_params("arbitrary"),
    )(u, w, qd, kd, p, cd)


def _gdn_scan_bwd(do, w, qd, kd, p, cd, s_all, vn, name):
    T = do.shape[0]
    cb = _tile(T // CHUNK, SCAN_CHUNKS_PER_STEP)
    rows = cb * CHUNK
    n_steps = T // rows

    def body(do_ref, w_ref, qd_ref, kd_ref, p_ref, cd_ref, s_all_ref, vn_ref,
             dvn_ref, dw_ref, dqd_ref, dkd_ref, dp_ref, dcd_ref, ds_ref):
        @pl.when(pl.program_id(0) == 0)
        def _():
            ds_ref[...] = jnp.zeros_like(ds_ref)

        causal, _, _ = _chunk_masks()
        lane = lax.broadcasted_iota(jnp.int32, (CHUNK, 128), 1)
        heads = range(GDN_HEADS)
        hss = [slice(hh * HEAD_DIM, (hh + 1) * HEAD_DIM) for hh in heads]
        css = [slice(hh * CHUNK, (hh + 1) * CHUNK) for hh in heads]
        ds_cur = [ds_ref[hh] for hh in heads]
        for ci in reversed(range(cb)):
            rs = slice(ci * CHUNK, (ci + 1) * CHUNK)
            ds_ms = [d.astype(MXU_DTYPE) for d in ds_cur]
            s_olds = [s_all_ref[ci * GDN_WIDTH + hh * HEAD_DIM:ci * GDN_WIDTH + (hh + 1) * HEAD_DIM, :] for hh in heads]
            s_ms = [s.astype(MXU_DTYPE) for s in s_olds]
            do_ms = [do_ref[rs, hs].astype(MXU_DTYPE) for hs in hss]
            p_do = [_dot(p_ref[rs, cs], do_m, TN) for cs, do_m in zip(css, do_ms)]
            k_ds = [_dot(kd_ref[rs, hs], ds_m, NN) for hs, ds_m in zip(hss, ds_ms)]
            q_do = [_dot(qd_ref[rs, hs], do_m, TN) for hs, do_m in zip(hss, do_ms)]
            dqds = [_dot(do_m, s_m, NT) for do_m, s_m in zip(do_ms, s_ms)]
            dkds = [_dot(vn_ref[rs, hs], ds_m, NT) for hs, ds_m in zip(hss, ds_ms)]
            dps = [_dot(do_m, vn_ref[rs, hs], NT) for hs, do_m in zip(hss, do_ms)]
            dv_news = [a + b for a, b in zip(p_do, k_ds)]
            dvn_ms = [d.astype(MXU_DTYPE) for d in dv_news]
            w_dv = [_dot(w_ref[rs, hs], dvn_m, TN) for hs, dvn_m in zip(hss, dvn_ms)]
            dws = [_dot(dvn_m, s_m, NT) for dvn_m, s_m in zip(dvn_ms, s_ms)]
            dcd_tile = jnp.zeros((CHUNK, 128), F32)
            for hh in heads:
                dvn_ref[rs, hss[hh]] = dv_news[hh]
                dw_ref[rs, hss[hh]] = -dws[hh]
                dqd_ref[rs, hss[hh]] = dqds[hh]
                dkd_ref[rs, hss[hh]] = dkds[hh]
                dp_ref[rs, css[hh]] = jnp.where(causal, dps[hh], 0.0)
                dcd = jnp.sum(jnp.sum(s_olds[hh] * ds_cur[hh], axis=1, keepdims=True), axis=0, keepdims=True)
                dcd_tile = jnp.where(lane == GDN_HEADS + hh, dcd, dcd_tile)
                c_dec = cd_ref[ci * CHUNK:ci * CHUNK + 1, GDN_HEADS + hh:GDN_HEADS + hh + 1]
                ds_cur[hh] = c_dec * ds_cur[hh] + q_do[hh] - w_dv[hh]
            dcd_ref[rs, :] = dcd_tile
        for hh in heads:
            ds_ref[hh] = ds_cur[hh]

    def spec(width):
        return pl.BlockSpec((rows, width), lambda n: (n_steps - 1 - n, 0))

    hw, cw = GDN_WIDTH, GDN_HEADS * CHUNK
    return pl.pallas_call(
        body, name=name, grid=(n_steps,),
        in_specs=[spec(hw), spec(hw), spec(hw), spec(hw), spec(cw), spec(128),
                  pl.BlockSpec((cb * GDN_WIDTH, HEAD_DIM), lambda n: (n_steps - 1 - n, 0)), spec(hw)],
        out_specs=[spec(hw), spec(hw), spec(hw), spec(hw), spec(cw), spec(128)],
        out_shape=[jax.ShapeDtypeStruct((T, hw), F32)] * 4
        + [jax.ShapeDtypeStruct((T, cw), F32), jax.ShapeDtypeStruct((T, 128), F32)],
        scratch_shapes=[pltpu.VMEM((GDN_HEADS, HEAD_DIM, HEAD_DIM), F32)],
        compiler_params=_params("arbitrary"),
    )(do, w, qd, kd, p, cd, s_all, vn)


def _gdn_chunk_bwd(qkv, bg, tinv_all, u, w, dvn, dw, dqd, dkd, dp, dcd, name):
    T = qkv.shape[0]
    cb = _tile(T // CHUNK, CHUNKS_PER_STEP)
    rows = cb * CHUNK

    def body(qkv_ref, bg_ref, tinv_ref, u_ref, w_ref, dvn_ref, dw_ref, dqd_ref, dkd_ref, dp_ref, dcd_ref,
             dqkv_ref, dbg_ref):
        masks = _chunk_masks()
        causal, strict, _ = masks
        lane = lax.broadcasted_iota(jnp.int32, (CHUNK, 128), 1)
        row = lax.broadcasted_iota(jnp.int32, (CHUNK, 128), 0)
        heads = []
        for ci in range(cb):
            rs = slice(ci * CHUNK, (ci + 1) * CHUNK)
            bgv = bg_ref[rs, :]
            gc_all = _chunk_cumsum(bgv)
            gc_rows = _row_form(gc_all)
            for hh in range(GDN_HEADS):
                qs, ks, vs = _head_slices(hh)
                q, k = qkv_ref[rs, qs], qkv_ref[rs, ks]
                beta = bgv[:, hh:hh + 1]
                gc = gc_all[:, GDN_HEADS + hh:GDN_HEADS + hh + 1]
                decay, kb, eg = _head_elementwise(k, beta, gc, gc_rows[GDN_HEADS + hh:GDN_HEADS + hh + 1, :], causal)
                heads.append(dict(ci=ci, hh=hh, rs=rs, hs=slice(hh * HEAD_DIM, (hh + 1) * HEAD_DIM),
                                  cs=slice(hh * CHUNK, (hh + 1) * CHUNK), q=q, k=k, beta=beta, gc=gc,
                                  decay=decay, kb=kb, eg=eg))
        for hd in heads:
            hd["t"] = _split(tinv_ref[hd["rs"], hd["cs"]])
        for hd in heads:
            hd["kk"] = _mdot(hd["kb"], hd["k"], NT)
            hd["qk"] = _mdot(hd["q"], hd["k"], NT)
        for hd in heads:
            hd["dvb"] = _dot3(hd["t"], _split(dvn_ref[hd["rs"], hd["hs"]]), TN)
            hd["dkbeg"] = _dot3(hd["t"], _split(dw_ref[hd["rs"], hd["hs"]]), TN)
        for hd in heads:
            rs, hs = hd["rs"], hd["hs"]
            da = -(_mdot(hd["dvb"], u_ref[rs, hs], NT) + _mdot(hd["dkbeg"], w_ref[rs, hs], NT))
            dm = jnp.where(strict, da, 0.0)
            dp_ = dp_ref[rs, hd["cs"]]
            hd["dkk"] = dm * hd["decay"]
            hd["dqk"] = dp_ * hd["decay"]
            hd["e"] = (hd["dkk"] * hd["kk"] + hd["dqk"] * hd["qk"])
        for hd in heads:
            hd["dkb"] = _mdot(hd["dkk"], hd["k"], NN)
            hd["dk"] = _mdot(hd["dkk"], hd["kb"], TN) + _mdot(hd["dqk"], hd["q"], TN)
            hd["dq"] = _mdot(hd["dqk"], hd["k"], NN)
            onehot = (lane == GDN_HEADS + hd["hh"]).astype(jnp.bfloat16)
            e_hi, e_lo = _split(hd["e"])
            hd["col_sums"] = _dot(e_lo, onehot, TN) + _dot(e_hi, onehot, TN)
        tiles = {}
        for hd in heads:
            ci, hh, rs, hs = hd["ci"], hd["hh"], hd["rs"], hd["hs"]
            qs, ks, vs = _head_slices(hh)
            q, k, beta, gc, eg, kb = hd["q"], hd["k"], hd["beta"], hd["gc"], hd["eg"], hd["kb"]
            v = qkv_ref[rs, vs]
            dqd_, dkd_ = dqd_ref[rs, hs], dkd_ref[rs, hs]
            gl = gc[CHUNK - 1:CHUNK, :]
            ek = jnp.exp(gl - gc)
            dkb = hd["dkb"] + hd["dkbeg"] * eg
            deg = jnp.sum(dqd_ * q, axis=1, keepdims=True) + jnp.sum(hd["dkbeg"] * kb, axis=1, keepdims=True)
            dek = jnp.sum(dkd_ * k, axis=1, keepdims=True)
            dcd_ = dcd_ref[ci * CHUNK:ci * CHUNK + 1, GDN_HEADS + hh:GDN_HEADS + hh + 1]
            dgl = jnp.sum(dek * ek, axis=0, keepdims=True) + dcd_ * jnp.exp(gl)
            dgc = jnp.sum(hd["e"], axis=1, keepdims=True) + deg * eg - dek * ek
            dbeta_tile, dgc_tile = tiles.get(ci, (jnp.zeros((CHUNK, 128), F32), jnp.zeros((CHUNK, 128), F32)))
            dgc_tile += jnp.where(lane == GDN_HEADS + hh, dgc, 0.0) - hd["col_sums"]
            dgc_tile += jnp.where((lane == GDN_HEADS + hh) & (row == CHUNK - 1), dgl, 0.0)
            dbeta = jnp.sum(dkb * k, axis=1, keepdims=True) + jnp.sum(hd["dvb"] * v, axis=1, keepdims=True)
            dbeta_tile += jnp.where(lane == hh, dbeta, 0.0)
            tiles[ci] = (dbeta_tile, dgc_tile)
            dqkv_ref[rs, qs] = hd["dq"] + dqd_ * eg
            dqkv_ref[rs, ks] = hd["dk"] + dkd_ * ek + dkb * beta
            dqkv_ref[rs, vs] = hd["dvb"] * beta
        for ci in range(cb):
            dbeta_tile, dgc_tile = tiles[ci]
            dbg_ref[ci * CHUNK:(ci + 1) * CHUNK, :] = dbeta_tile + _chunk_cumsum(dgc_tile, reverse=True)

    def spec(width):
        return pl.BlockSpec((rows, width), lambda n: (n, 0))

    hw, cw = GDN_WIDTH, GDN_HEADS * CHUNK
    return pl.pallas_call(
        body, name=name, grid=(T // rows,),
        in_specs=[spec(QKV_WIDTH), spec(128), spec(cw), spec(hw), spec(hw), spec(hw), spec(hw), spec(hw),
                  spec(hw), spec(cw), spec(128)],
        out_specs=[spec(QKV_WIDTH), spec(128)],
        out_shape=[jax.ShapeDtypeStruct((T, QKV_WIDTH), F32), jax.ShapeDtypeStruct((T, 128), F32)],
        compiler_params=_params("parallel"),
    )(qkv, bg, tinv_all, u, w, dvn, dw, dqd, dkd, dp, dcd)


def _pool_counts(i, tm, rows, offset):
    t = i * tm - offset + lax.broadcasted_iota(jnp.int32, (rows, 1), 0)
    return [jnp.minimum(t + 1, w).astype(F32) for w in POOL_WINDOWS]


def _window_sums(window, forward):
    sums, s, step = [], window, 1
    for _ in POOL_WINDOWS:
        s = s + _shift_rows(s, -step if forward else step)
        sums.append(s)
        step *= 2
    return sums


def _pooled(window, counts):
    sums = _window_sums(window, forward=False)
    out = []
    for gi in range(POOL_GROUPS):
        sl = slice(gi * 128, (gi + 1) * 128)
        out.append(sums[gi][HALO:, sl] / counts[gi] - window[HALO:, sl])
    return out


def _mix_post(o, proj, gdn_norm, pool_w, pool_scale, name):
    T = o.shape[0]
    tm = _tile(T, 512)
    hb = tm // HALO

    def body(o_ref, z_ref, p_ref, ph_ref, gn_ref, pw_ref, ps_ref, out_ref):
        i = pl.program_id(0)
        for hh in range(GDN_HEADS):
            sl = slice(hh * HEAD_DIM, (hh + 1) * HEAD_DIM)
            oh, zh = o_ref[:, sl], z_ref[:, sl]
            ro = lax.rsqrt(jnp.mean(oh * oh, axis=-1, keepdims=True) + EPS)
            out_ref[:, sl] = (((oh * ro) * gn_ref[...]) * (zh * _sigmoid(zh))).astype(out_ref.dtype)
        halo = jnp.where(i == 0, 0.0, ph_ref[...])
        window = jnp.concatenate([halo, p_ref[...]], axis=0)
        pooled = _pooled(window, _pool_counts(i, tm, tm, 0))
        for gi in range(POOL_GROUPS):
            pm = _mdot(pooled[gi], pw_ref[gi], NN)
            out_ref[:, GDN_WIDTH + gi * 128:GDN_WIDTH + (gi + 1) * 128] = (
                pm * ps_ref[:, gi * 128:(gi + 1) * 128]).astype(out_ref.dtype)

    return pl.pallas_call(
        body, name=name, grid=(T // tm,),
        in_specs=[pl.BlockSpec((tm, GDN_WIDTH), lambda i: (i, 0)),
                  pl.BlockSpec((tm, GDN_WIDTH), lambda i: (i, COL_Z // GDN_WIDTH)),
                  pl.BlockSpec((tm, POOL_WIDTH), lambda i: (i, COL_P // POOL_WIDTH)),
                  pl.BlockSpec((HALO, POOL_WIDTH), lambda i: (jnp.maximum(i * hb - 1, 0), COL_P // POOL_WIDTH)),
                  pl.BlockSpec((1, HEAD_DIM), lambda i: (0, 0)),
                  pl.BlockSpec((POOL_GROUPS, 128, 128), lambda i: (0, 0, 0)),
                  pl.BlockSpec((1, POOL_WIDTH), lambda i: (0, 0))],
        out_specs=pl.BlockSpec((tm, GDN_WIDTH + POOL_WIDTH), lambda i: (i, 0)),
        out_shape=jax.ShapeDtypeStruct((T, GDN_WIDTH + POOL_WIDTH), MXU_DTYPE),
        compiler_params=_params("parallel"),
    )(o, proj, proj, proj, gdn_norm, pool_w, pool_scale)


def _mix_post_bwd(dmix, o, proj, gdn_norm, pool_w, pool_scale, name):
    T = o.shape[0]
    tm = _tile(T, 512)
    hb = tm // HALO
    n_tiles = T // tm

    def body(dg_ref, dpo_ref, dpo_next_ref, o_ref, z_ref, p_ref, ph_ref, gn_ref, pw_ref, ps_ref,
             do_ref, dzp_ref, dgn_ref, dpw_ref, dps_ref):
        i = pl.program_id(0)

        @pl.when(i == 0)
        def _():
            dgn_ref[...] = jnp.zeros_like(dgn_ref)
            dpw_ref[...] = jnp.zeros_like(dpw_ref)
            dps_ref[...] = jnp.zeros_like(dps_ref)

        gn = gn_ref[...]
        dgn = jnp.zeros((1, HEAD_DIM), F32)
        for hh in range(GDN_HEADS):
            sl = slice(hh * HEAD_DIM, (hh + 1) * HEAD_DIM)
            oh, zh, dy = o_ref[:, sl], z_ref[:, sl], dg_ref[:, sl]
            ro = lax.rsqrt(jnp.mean(oh * oh, axis=-1, keepdims=True) + EPS)
            on = oh * ro
            sig = _sigmoid(zh)
            sz = zh * sig
            dzp_ref[:, sl] = (dy * (on * gn) * (sig * (1.0 + zh * (1.0 - sig)))).astype(dzp_ref.dtype)
            dgn += jnp.sum(dy * on * sz, axis=0, keepdims=True)
            don = dy * gn * sz
            do_ref[:, sl] = ro * (don - on * jnp.mean(don * on, axis=-1, keepdims=True))
        dgn_ref[...] += dgn

        halo = jnp.where(i == 0, 0.0, ph_ref[...])
        window = jnp.concatenate([halo, p_ref[...]], axis=0)
        counts = _pool_counts(i, tm, tm + HALO, 0)
        pooled = _pooled(window, [cn[:tm] for cn in counts])
        nxt = jnp.where(i == n_tiles - 1, 0.0, dpo_next_ref[...])
        dpo_w = jnp.concatenate([dpo_ref[...], nxt], axis=0)
        ps = ps_ref[...]
        dps = []
        scaled = []
        for gi in range(POOL_GROUPS):
            sl = slice(gi * 128, (gi + 1) * 128)
            dpm = dpo_w[:, sl] * ps[:, sl]
            pm = _mdot(pooled[gi], pw_ref[gi], NN)
            dps.append(jnp.sum(dpo_w[:tm, sl] * pm, axis=0, keepdims=True))
            dpw_ref[gi] += _mdot(pooled[gi], dpm[:tm], TN)
            dpooled = _mdot(dpm, pw_ref[gi], NT)
            scaled.append((dpooled, dpooled / counts[gi]))
        dps_ref[...] += jnp.concatenate(dps, axis=1)
        lead = _window_sums(jnp.concatenate([sc for _, sc in scaled], axis=1), forward=True)
        for gi in range(POOL_GROUPS):
            sl = slice(gi * 128, (gi + 1) * 128)
            dzp_ref[:, GDN_WIDTH + gi * 128:GDN_WIDTH + (gi + 1) * 128] = (
                lead[gi][:tm, sl] - scaled[gi][0][:tm]).astype(dzp_ref.dtype)

    last_halo = T // HALO - 1
    return pl.pallas_call(
        body, name=name, grid=(n_tiles,),
        in_specs=[pl.BlockSpec((tm, GDN_WIDTH), lambda i: (i, 0)),
                  pl.BlockSpec((tm, POOL_WIDTH), lambda i: (i, 1)),
                  pl.BlockSpec((HALO, POOL_WIDTH), lambda i: (jnp.minimum((i + 1) * hb, last_halo), 1)),
                  pl.BlockSpec((tm, GDN_WIDTH), lambda i: (i, 0)),
                  pl.BlockSpec((tm, GDN_WIDTH), lambda i: (i, COL_Z // GDN_WIDTH)),
                  pl.BlockSpec((tm, POOL_WIDTH), lambda i: (i, COL_P // POOL_WIDTH)),
                  pl.BlockSpec((HALO, POOL_WIDTH), lambda i: (jnp.maximum(i * hb - 1, 0), COL_P // POOL_WIDTH)),
                  pl.BlockSpec((1, HEAD_DIM), lambda i: (0, 0)),
                  pl.BlockSpec((POOL_GROUPS, 128, 128), lambda i: (0, 0, 0)),
                  pl.BlockSpec((1, POOL_WIDTH), lambda i: (0, 0))],
        out_specs=[pl.BlockSpec((tm, GDN_WIDTH), lambda i: (i, 0)),
                   pl.BlockSpec((tm, GDN_WIDTH + POOL_WIDTH), lambda i: (i, 0)),
                   pl.BlockSpec((1, HEAD_DIM), lambda i: (0, 0)),
                   pl.BlockSpec((POOL_GROUPS, 128, 128), lambda i: (0, 0, 0)),
                   pl.BlockSpec((1, POOL_WIDTH), lambda i: (0, 0))],
        out_shape=[jax.ShapeDtypeStruct((T, GDN_WIDTH), F32),
                   jax.ShapeDtypeStruct((T, GDN_WIDTH + POOL_WIDTH), MXU_DTYPE),
                   jax.ShapeDtypeStruct((1, HEAD_DIM), F32),
                   jax.ShapeDtypeStruct((POOL_GROUPS, 128, 128), F32),
                   jax.ShapeDtypeStruct((1, POOL_WIDTH), F32)],
        compiler_params=_params("arbitrary"),
    )(dmix, dmix, dmix, o, proj, proj, proj, gdn_norm, pool_w, pool_scale)


def _gdn_prep_bwd(proj, conv_w, a_log_l, dt_bias_l, dqkv, dbg, dzp, name):
    T = proj.shape[0]
    tm = _tile(T, 512)
    hb = tm // 8
    n_tiles = T // tm
    last_halo = T // 8 - 1

    def body(cur_ref, before_ref, after_ref, ba_ref, w_ref, al_ref, dtb_ref, dq_ref, dq_after_ref, dbg_ref,
             dzp_ref, dproj_ref, dw_ref, dal_ref, ddtb_ref):
        i = pl.program_id(0)

        @pl.when(i == 0)
        def _():
            dw_ref[...] = jnp.zeros_like(dw_ref)
            dal_ref[...] = jnp.zeros_like(dal_ref)
            ddtb_ref[...] = jnp.zeros_like(ddtb_ref)

        last = i == n_tiles - 1
        w = w_ref[...]
        before = jnp.where(i == 0, 0.0, before_ref[...])
        after = jnp.where(last, 0.0, after_ref[...])
        window = jnp.concatenate([before, cur_ref[...], after], axis=0)
        y = _conv_act(window, w)
        sig = _sigmoid(y)
        act = y * sig
        dq_w = jnp.concatenate([jnp.zeros((8, QKV_WIDTH), F32), dq_ref[...],
                                jnp.where(last, 0.0, dq_after_ref[...])], axis=0)
        dact = []
        for hh in range(3 * GDN_HEADS):
            sl = slice(hh * HEAD_DIM, (hh + 1) * HEAD_DIM)
            blk, dblk = act[:, sl], dq_w[:, sl]
            if hh < 2 * GDN_HEADS:
                rn = lax.rsqrt(jnp.sum(blk * blk, axis=-1, keepdims=True) + EPS)
                unit = blk * rn
                if hh < GDN_HEADS:
                    dblk = dblk * (HEAD_DIM ** -0.5)
                dblk = rn * (dblk - unit * jnp.sum(dblk * unit, axis=-1, keepdims=True))
            dact.append(dblk)
        dy = jnp.concatenate(dact, axis=1) * (sig * (1.0 + y * (1.0 - sig)))
        dx = dy * w[CONV_K - 1:CONV_K, :]
        dws = [None] * CONV_K
        dws[CONV_K - 1] = jnp.sum(dy[8:8 + tm] * window[8:8 + tm], axis=0, keepdims=True)
        for j in range(CONV_K - 1):
            s = CONV_K - 1 - j
            dx += _shift_rows(dy, -s) * w[j:j + 1, :]
            dws[j] = jnp.sum(dy[8:8 + tm] * _shift_rows(window, s)[8:8 + tm], axis=0, keepdims=True)
        dw_ref[...] += jnp.concatenate(dws, axis=0)
        dproj_ref[:, :QKV_WIDTH] = dx[8:8 + tm].astype(dproj_ref.dtype)
        dproj_ref[:, COL_Z:COL_BA] = dzp_ref[...]

        ba = ba_ref[...]
        dbg_ = dbg_ref[...]
        lane = lax.broadcasted_iota(jnp.int32, ba.shape, 1)
        beta = _sigmoid(ba)
        pre = ba + dtb_ref[...]
        neg_a = -jnp.exp(al_ref[...])
        g = neg_a * _softplus(pre)
        is_g = (lane >= GDN_HEADS) & (lane < 2 * GDN_HEADS)
        da_raw = jnp.where(is_g, dbg_ * neg_a * _sigmoid(pre), 0.0)
        dba = jnp.where(lane < GDN_HEADS, dbg_ * beta * (1.0 - beta), da_raw)
        dproj_ref[:, COL_BA:] = dba.astype(dproj_ref.dtype)
        dal_ref[...] += jnp.sum(jnp.where(is_g, dbg_ * g, 0.0), axis=0, keepdims=True)
        ddtb_ref[...] += jnp.sum(da_raw, axis=0, keepdims=True)

    lane_vec = pl.BlockSpec((1, 128), lambda i: (0, 0))
    return pl.pallas_call(
        body, name=name, grid=(n_tiles,),
        in_specs=[pl.BlockSpec((tm, QKV_WIDTH), lambda i: (i, 0)),
                  pl.BlockSpec((8, QKV_WIDTH), lambda i: (jnp.maximum(i * hb - 1, 0), 0)),
                  pl.BlockSpec((8, QKV_WIDTH), lambda i: (jnp.minimum((i + 1) * hb, last_halo), 0)),
                  pl.BlockSpec((tm, 128), lambda i: (i, COL_BA // 128)),
                  pl.BlockSpec((CONV_K, QKV_WIDTH), lambda i: (0, 0)), lane_vec, lane_vec,
                  pl.BlockSpec((tm, QKV_WIDTH), lambda i: (i, 0)),
                  pl.BlockSpec((8, QKV_WIDTH), lambda i: (jnp.minimum((i + 1) * hb, last_halo), 0)),
                  pl.BlockSpec((tm, 128), lambda i: (i, 0)),
                  pl.BlockSpec((tm, GDN_WIDTH + POOL_WIDTH), lambda i: (i, 0))],
        out_specs=[pl.BlockSpec((tm, D_IN_PAD), lambda i: (i, 0)),
                   pl.BlockSpec((CONV_K, QKV_WIDTH), lambda i: (0, 0)), lane_vec, lane_vec],
        out_shape=[jax.ShapeDtypeStruct((T, D_IN_PAD), MXU_DTYPE),
                   jax.ShapeDtypeStruct((CONV_K, QKV_WIDTH), F32),
                   jax.ShapeDtypeStruct((1, 128), F32), jax.ShapeDtypeStruct((1, 128), F32)],
        compiler_params=_params("arbitrary"),
    )(proj, proj, proj, proj, conv_w, a_log_l, dt_bias_l, dqkv, dqkv, dbg, dzp)


def _mod_part(c_all, w_ada, b_part, name):
    def body(c_ref, w_ref, b_ref, out_ref):
        cc = c_ref[...]
        out_ref[...] = _mdot(cc * _sigmoid(cc), w_ref[...], NN) + b_ref[...]

    return pl.pallas_call(
        body, name=name, out_shape=jax.ShapeDtypeStruct((c_all.shape[0], w_ada.shape[1]), F32),
        compiler_params=_params(),
    )(c_all, w_ada, b_part)


def _w_ada_grad(c_all, dmod_part, name):
    def body(c_ref, d_ref, out_ref):
        cc = c_ref[...]
        out_ref[...] = _mdot(cc * _sigmoid(cc), d_ref[...], TN)

    return pl.pallas_call(
        body, name=name, out_shape=jax.ShapeDtypeStruct((c_all.shape[1], dmod_part.shape[1]), F32),
        compiler_params=_params(),
    )(c_all, dmod_part)


def _sum_parts(parts, name):
    _, R, C = parts.shape
    tr = max([t for t in range(16, min(R, 512) + 1, 16) if R % t == 0], default=R)

    def body(p_ref, out_ref):
        acc = p_ref[0].astype(F32)
        for s in range(1, N_DEV):
            acc += p_ref[s].astype(F32)
        out_ref[...] = acc

    return pl.pallas_call(
        body, name=name, grid=(R // tr,),
        in_specs=[pl.BlockSpec((N_DEV, tr, C), lambda i: (0, i, 0))],
        out_specs=pl.BlockSpec((tr, C), lambda i: (i, 0)),
        out_shape=jax.ShapeDtypeStruct((R, C), F32),
        compiler_params=_params("parallel"),
    )(parts)


def _adamw_math(w, g, m, v):
    mm = ADAM_B1 * m + (1.0 - ADAM_B1) * g
    vv = ADAM_B2 * v + (1.0 - ADAM_B2) * (g * g)
    m_hat = mm / (1.0 - ADAM_B1 ** ADAM_STEP)
    v_hat = vv / (1.0 - ADAM_B2 ** ADAM_STEP)
    return -ADAM_LR * (m_hat / (jnp.sqrt(v_hat) + ADAM_EPS) + ADAM_WD * w), mm, vv


def _adamw_small(ws, gs, ms, vs, name):
    n = len(ws)

    def body(*refs):
        for i in range(n):
            d, mm, vv = _adamw_math(*[refs[k * n + i][...] for k in range(4)])
            refs[4 * n + i][...] = d
            refs[5 * n + i][...] = mm
            refs[6 * n + i][...] = vv

    out = pl.pallas_call(
        body, name=name, out_shape=[jax.ShapeDtypeStruct(w.shape, F32) for w in ws] * 3,
        compiler_params=_params(),
    )(*ws, *gs, *ms, *vs)
    return out[:n], out[n:2 * n], out[2 * n:]


def _adamw_sum(parts, w, m, v, name):
    R, C = w.shape
    tr = max([t for t in range(16, min(R, 512) + 1, 16) if R % t == 0], default=R)

    def body(p_ref, w_ref, m_ref, v_ref, g_ref, d_ref, mo_ref, vo_ref):
        g = p_ref[0].astype(F32)
        for s in range(1, N_DEV):
            g += p_ref[s].astype(F32)
        g_ref[...] = g
        d_ref[...], mo_ref[...], vo_ref[...] = _adamw_math(w_ref[...], g, m_ref[...], v_ref[...])

    spec = pl.BlockSpec((tr, C), lambda i: (i, 0))
    return pl.pallas_call(
        body, name=name, grid=(R // tr,),
        in_specs=[pl.BlockSpec((N_DEV, tr, C), lambda i: (0, i, 0)), spec, spec, spec], out_specs=[spec] * 4,
        out_shape=[jax.ShapeDtypeStruct((R, C), F32)] * 4,
        compiler_params=_params("parallel"),
    )(parts, w, m, v)


def _adamw(w, g, m, v, name):
    R, C = w.shape
    tr = max([t for t in range(8, min(R, 512) + 1, 8) if R % t == 0], default=R)

    def body(w_ref, g_ref, m_ref, v_ref, d_ref, mo_ref, vo_ref):
        d_ref[...], mo_ref[...], vo_ref[...] = _adamw_math(w_ref[...], g_ref[...], m_ref[...], v_ref[...])

    spec = pl.BlockSpec((tr, C), lambda i: (i, 0))
    return pl.pallas_call(
        body, name=name, grid=(R // tr,),
        in_specs=[spec] * 4, out_specs=[spec] * 3,
        out_shape=[jax.ShapeDtypeStruct((R, C), F32)] * 3,
        compiler_params=_params("parallel"),
    )(w, g, m, v)


def _weight_grad(a, b, name, dep=None):
    return _matmul([(a, b)], TN, WIRE_DTYPE, name, tm=1408, tn=1024, tk=2048, dep=dep)


def _rows_of(flat, lanes=1024):
    flat = flat.reshape(-1)
    n = -(-flat.shape[0] // lanes) * lanes
    return jnp.pad(flat, (0, n - flat.shape[0])).reshape(n // lanes, lanes)


def _pad_rows(a, rows):
    return jnp.pad(a, ((0, rows - a.shape[0]), (0, 0)))


def kernel(x, c, w_ada, b_ada, norm_ffn1, ffn1_gate, ffn1_up, ffn1_down, norm_mix, w_in, conv_w, a_log, dt_bias, gdn_norm, pool_w, pool_scale, w_out, norm_ffn2, ffn2_gate, ffn2_up, ffn2_down, final_norm, loss_target, m_w_ada, m_b_ada, m_norm_ffn1, m_ffn1_gate, m_ffn1_up, m_ffn1_down, m_norm_mix, m_w_in, m_conv_w, m_a_log, m_dt_bias, m_gdn_norm, m_pool_w, m_pool_scale, m_w_out, m_norm_ffn2, m_ffn2_gate, m_ffn2_up, m_ffn2_down, m_final_norm, v_w_ada, v_b_ada, v_norm_ffn1, v_ffn1_gate, v_ffn1_up, v_ffn1_down, v_norm_mix, v_w_in, v_conv_w, v_a_log, v_dt_bias, v_gdn_norm, v_pool_w, v_pool_scale, v_w_out, v_norm_ffn2, v_ffn2_gate, v_ffn2_up, v_ffn2_down, v_final_norm):
    T, D = x.shape[1], x.shape[2]
    Fs = ffn1_gate.shape[2]
    Ws = w_in.shape[2]
    Ws_pad = -(-Ws // 16) * 16
    Os = w_out.shape[1]
    Ms = w_ada.shape[2]
    Cs = conv_w.shape[2]
    me = 4 * lax.axis_index("x") + 2 * lax.axis_index("y") + lax.axis_index("c")
    x0, target = x[0], loss_target[0]

    def wire(a):
        return a.astype(WIRE_DTYPE)

    def token(started):
        return started[4][:1, :1]

    def with_own(landed, own):
        return lax.dynamic_update_slice(landed, own[None], (me, 0, 0))

    def full(landed):
        return landed.reshape(-1, D).astype(MXU_DTYPE)

    no_dep = jnp.zeros((8, 128), F32)
    small = jnp.concatenate([_pad_rows(c, 8), _pad_rows(jnp.pad(conv_w[0], ((0, 0), (0, D - Cs))), 8)], axis=0)
    got, = _all_gather([small], "gather_small")
    c_all = got[:, 0, :]
    conv_full = jnp.transpose(got[:, 8:8 + CONV_K, :Cs], (1, 0, 2)).reshape(CONV_K, QKV_WIDTH)
    b_part = lax.dynamic_slice(b_ada, (0, me * Ms), (1, Ms))
    mod_part = _mod_part(c_all, w_ada[0], b_part, "mod_part")

    w1 = [wire(ffn1_gate[0].T)]
    w1u = [wire(ffn1_up[0].T)]
    w1d = [wire(ffn1_down[0])]
    w2 = [wire(_pad_rows(w_in[0].T, Ws_pad)), wire(w_out[0])]
    w3 = [wire(ffn2_gate[0].T), wire(ffn2_up[0].T), wire(ffn2_down[0])]
    mod_parts, *w1_all = _all_gather([mod_part] + w1, "gather_mod_w1")
    mod_all = jnp.transpose(mod_parts, (1, 0, 2)).reshape(N_DEV, N_MOD * D)
    mod = lax.dynamic_slice(mod_all, (me, 0), (1, N_MOD * D)).reshape(N_MOD, 1, D)
    sh1, sc1, gt1, sh2, sc2, gt2, sh3, sc3, gt3 = [mod[i] for i in range(N_MOD)]
    wg1_t = full(w1_all[0])
    w1u_sent = _exchange_start(w1u, True, w1_all[0], "w1u_start")
    w1d_sent = _exchange_start(w1d, True, w1u_sent[4], "w1d_start")
    w2_sent = _exchange_start(w2, True, w1d_sent[4], "w2_start")
    w3_sent = _exchange_start(w3, True, w2_sent[4], "w3_start")

    lane_pad = lambda a: jnp.pad(a, ((0, 0), (GDN_HEADS, 128 - 2 * GDN_HEADS)))
    a_log_l, dt_bias_l = lane_pad(a_log), lane_pad(dt_bias)
    pool_w_m = pool_w[0].astype(MXU_DTYPE)

    g1, h1 = _swiglu_gate(x0, wg1_t, norm_ffn1, sc1 + token(w3_sent), sh1, "ffn1_gate")
    wu1_t = full(with_own(_exchange_wait(w1u_sent, h1, True, "w1u_wait")[0], w1u[0]))
    u1, a1 = _swiglu_up(h1, wu1_t, g1, "ffn1_up")
    wd1 = full(with_own(_exchange_wait(w1d_sent, a1, True, "w1d_wait")[0], w1d[0]))
    y1, x1, h2 = _matmul_resid_norm_mod(a1, wd1, x0, gt1, 0.5, norm_mix, sc2, sh2, "ffn1_down")
    w_in_all, wo_all = [with_own(z, own) for z, own in zip(_exchange_wait(w2_sent, h2, True, "w2_wait"), w2)]
    w_in_t = w_in_all[:, :Ws, :].reshape(-1, D).astype(MXU_DTYPE)
    wo = full(wo_all)
    w_in_re = jnp.concatenate([w_in_t[:COL_Z + GDN_WIDTH], w_in_t[D_IN - POOL_WIDTH:],
                               w_in_t[4 * GDN_WIDTH:4 * GDN_WIDTH + 2 * GDN_HEADS],
                               jnp.zeros((128 - 2 * GDN_HEADS, D), MXU_DTYPE)], axis=0)
    proj = _matmul([(h2, w_in_re)], NT, F32, "proj_in", tm=512, tn=D_IN_PAD, tk=D)
    qkv, bg = _gdn_prep(proj, conv_full, a_log_l, dt_bias_l, "gdn_prep")
    tinv, u_c, w_c, qd_c, kd_c, p_c, cd_c = _gdn_chunk_fwd(qkv, bg, "gdn_chunk_fwd")
    o, s_all, vn_c = _gdn_scan_fwd(u_c, w_c, qd_c, kd_c, p_c, cd_c, "gdn_scan_fwd")
    mix_in = _mix_post(o, proj, gdn_norm, pool_w_m, pool_scale, "mix_post")
    mixed, x2, h3 = _matmul_resid_norm_mod(mix_in, wo, x1, gt2, 1.0, norm_ffn2, sc3, sh3, "mix_out")
    wg2_t, wu2_t, wd2 = [full(with_own(z, own))
                         for z, own in zip(_exchange_wait(w3_sent, h3, True, "w3_wait"), w3)]
    g3, u3, a3, loss_row, d3, d_final, dy3, dgt3 = _swiglu_fwd_loss(
        h3, wg2_t, wu2_t, wd2, x2, gt3, final_norm.reshape(1, D), target, "ffn2_fwd_loss")

    dg3, du3, d2, d_n3, dsc3, dsh3, dmixed, dgt2 = _swiglu_bwd(
        dy3, wd2, g3, u3, wg2_t, wu2_t, "ffn2_bwd_norm3_bwd", (x2, norm_ffn2, sc3), d3,
        produced_by=(mixed, gt2, 1.0))
    d_wd2 = _weight_grad(a3, dy3, "ffn2_dwd")
    d_wg2 = _weight_grad(dg3, h3, "ffn2_dwg")
    d_wu2 = _weight_grad(du3, h3, "ffn2_dwu")
    dmix_in = _matmul([(dmixed, wo)], NT, F32, "mix_out_bwd", tm=512, tn=GDN_WIDTH + POOL_WIDTH, tk=D)
    d_wo = _matmul([(mix_in, dmixed)], TN, WIRE_DTYPE, "mix_dwo", tm=GDN_WIDTH + POOL_WIDTH, tn=D, tk=2048)
    do, dzp, d_gn, d_pw, d_ps = _mix_post_bwd(dmix_in, o, proj, gdn_norm, pool_w_m, pool_scale, "mix_post_bwd")
    dvn, dw_c, dqd, dkd, dp_c, dcd = _gdn_scan_bwd(do, w_c, qd_c, kd_c, p_c, cd_c, s_all, vn_c, "gdn_scan_bwd")
    dqkv, dbg = _gdn_chunk_bwd(qkv, bg, tinv, u_c, w_c, dvn, dw_c, dqd, dkd, dp_c, dcd, "gdn_chunk_bwd")
    dproj, d_conv, d_al, d_dtb = _gdn_prep_bwd(proj, conv_full, a_log_l, dt_bias_l, dqkv, dbg, dzp, "gdn_prep_bwd")
    d_win_re = _matmul([(dproj, h2)], TN, WIRE_DTYPE, "proj_in_dw", tm=D_IN_PAD, tn=D, tk=1024)
    d_win_t = jnp.concatenate([d_win_re[:COL_Z + GDN_WIDTH], d_win_re[COL_BA:COL_BA + 2 * GDN_HEADS],
                               d_win_re[COL_P:COL_P + POOL_WIDTH]], axis=0)
    d_win_blocks = jnp.pad(d_win_t.reshape(N_DEV, Ws, D), ((0, 0), (0, Ws_pad - Ws), (0, 0)))
    names23 = ["w_in", "w_out", "ffn2_gate", "ffn2_up", "ffn2_down"]
    parts23 = [wire(d_win_blocks), d_wo.reshape(N_DEV, Os, D), d_wg2.reshape(N_DEV, Fs, D),
               d_wu2.reshape(N_DEV, Fs, D), d_wd2.reshape(N_DEV, Fs, D)]
    own23 = [lax.dynamic_index_in_dim(p, me, 0, keepdims=False) for p in parts23]
    g23_sent = _exchange_start(parts23, False, no_dep, "g23_start")
    d1, d_n2, dsc2, dsh2, dy1, dgt1 = _norm_bwd((dproj, w_in_re), x1, norm_mix, sc2 + token(g23_sent), d2,
                                                "proj_in_bwd_norm2_bwd", produced_by=(y1, gt1, 0.5))
    dg1, du1, grad_x, d_n1, dsc1, dsh1 = _swiglu_bwd(
        dy1, wd1, g1, u1, wg1_t, wu1_t, "ffn1_bwd_norm1_bwd", (x0, norm_ffn1, sc1), d1)

    dmod = jnp.concatenate([dsh1, dsc1, dgt1, dsh2, dsc2, dgt2, dsh3, dsc3, dgt3], axis=0)
    small_rows = [dmod.reshape(-1), d_n1[0], d_n2[0], d_n3[0], d_final[0], d_gn[0], d_ps[0],
                  d_al[0, GDN_HEADS:2 * GDN_HEADS], d_dtb[0, GDN_HEADS:2 * GDN_HEADS], loss_row[0, :1],
                  d_conv.reshape(-1), d_pw.reshape(-1)]
    lanes = 1024
    small_rows = [_rows_of(r, lanes) for r in small_rows]
    n_rows = [r.shape[0] for r in small_rows]
    row_off = [sum(n_rows[:i]) for i in range(len(n_rows))]
    total = -(-sum(n_rows) // 8) * 8
    slab = _pad_rows(jnp.concatenate(small_rows, axis=0), total)
    slab_sent = _exchange_start([slab], True, no_dep, "small_grads_start")

    def send_ffn1(a, b, which, dep):
        parts = _weight_grad(a, b, f"ffn1_{which}", dep=dep).reshape(N_DEV, Fs, D)
        own = lax.dynamic_index_in_dim(parts, me, 0, keepdims=False)
        return _exchange_start([parts], False, no_dep, f"g1_{which}_start"), own

    g1_wg, own_wg = send_ffn1(dg1, h1, "dwg", slab_sent[4])
    g1_wu, own_wu = send_ffn1(du1, h1, "dwu", g1_wg[4])
    g1_wd, own_wd = send_ffn1(a1, dy1, "dwd", g1_wu[4])

    slab_all = with_own(_exchange_wait(slab_sent, g1_wg[4], True, "small_grads_wait")[0], slab)
    summed = _sum_parts(slab_all, "sum_small_grads")

    def piece(idx, n):
        return summed[row_off[idx]:row_off[idx] + n_rows[idx]].reshape(-1)[:n]

    g_b_ada = piece(0, N_MOD * D).reshape(1, N_MOD * D)
    g_n1, g_n2, g_n3 = piece(1, D).reshape(1, D), piece(2, D).reshape(1, D), piece(3, D).reshape(1, D)
    g_final = piece(4, D)
    g_gn = piece(5, HEAD_DIM).reshape(1, HEAD_DIM)
    g_ps = piece(6, POOL_WIDTH).reshape(1, POOL_WIDTH)
    g_al = piece(7, GDN_HEADS).reshape(1, GDN_HEADS)
    g_dtb = piece(8, GDN_HEADS).reshape(1, GDN_HEADS)
    loss = piece(9, 1)[0]
    g_conv = lax.dynamic_slice(piece(10, CONV_K * QKV_WIDTH).reshape(1, CONV_K, QKV_WIDTH), (0, 0, me * Cs),
                               (1, CONV_K, Cs))
    g_pw = piece(11, POOL_GROUPS * 128 * 128).reshape(1, POOL_GROUPS, 128, 128)

    dmod_all = slab_all[:, row_off[0]:row_off[0] + n_rows[0], :].reshape(N_DEV, -1)[:, :N_MOD * D]
    g_w_ada = _w_ada_grad(c_all, lax.dynamic_slice(dmod_all, (0, me * Ms), (N_DEV, Ms)), "w_ada_grad")[None]

    landed23 = _exchange_wait(g23_sent, g1_wd[4], False, "g23_wait")
    parts8 = {n: with_own(z, own) for n, z, own in zip(names23, landed23, own23)}
    g_rows = dict(w_in=_sum_parts(parts8.pop("w_in"), "sum_w_in")[:Ws])
    column_sharded = ("w_in", "ffn1_gate", "ffn1_up", "ffn2_gate", "ffn2_up")

    names = ["w_ada", "b_ada", "norm_ffn1", "ffn1_gate", "ffn1_up", "ffn1_down", "norm_mix", "w_in", "conv_w",
             "a_log", "dt_bias", "gdn_norm", "pool_w", "pool_scale", "w_out", "norm_ffn2", "ffn2_gate", "ffn2_up",
             "ffn2_down", "final_norm"]
    weights = dict(zip(names, [w_ada, b_ada, norm_ffn1, ffn1_gate, ffn1_up, ffn1_down, norm_mix, w_in, conv_w,
                               a_log, dt_bias, gdn_norm, pool_w, pool_scale, w_out, norm_ffn2, ffn2_gate, ffn2_up,
                               ffn2_down, final_norm]))
    ms = dict(zip(names, [m_w_ada, m_b_ada, m_norm_ffn1, m_ffn1_gate, m_ffn1_up, m_ffn1_down, m_norm_mix, m_w_in,
                          m_conv_w, m_a_log, m_dt_bias, m_gdn_norm, m_pool_w, m_pool_scale, m_w_out, m_norm_ffn2,
                          m_ffn2_gate, m_ffn2_up, m_ffn2_down, m_final_norm]))
    vs = dict(zip(names, [v_w_ada, v_b_ada, v_norm_ffn1, v_ffn1_gate, v_ffn1_up, v_ffn1_down, v_norm_mix, v_w_in,
                          v_conv_w, v_a_log, v_dt_bias, v_gdn_norm, v_pool_w, v_pool_scale, v_w_out, v_norm_ffn2,
                          v_ffn2_gate, v_ffn2_up, v_ffn2_down, v_final_norm]))
    grads = dict(w_ada=g_w_ada, b_ada=g_b_ada, norm_ffn1=g_n1, norm_mix=g_n2, conv_w=g_conv,
                 a_log=g_al, dt_bias=g_dtb, gdn_norm=g_gn, pool_w=g_pw, pool_scale=g_ps,
                 norm_ffn2=g_n3, final_norm=g_final)
    delta, new_m, new_v = {}, {}, {}

    def adamw_big(n):
        if n in column_sharded:
            view, back = (lambda a: a[0].T), (lambda a: a.T[None])
        else:
            view, back = (lambda a: a[0]), (lambda a: a[None])
        if n in parts8:
            g, d_, m_, v_ = _adamw_sum(parts8[n], view(weights[n]), view(ms[n]), view(vs[n]), f"adamw_{n}")
        else:
            g = g_rows[n] if n in g_rows else view(grads[n])
            d_, m_, v_ = _adamw(view(weights[n]), g, view(ms[n]), view(vs[n]), f"adamw_{n}")
        grads[n], delta[n], new_m[n], new_v[n] = back(g), back(d_), back(m_), back(v_)

    early = ["w_ada", "w_in", "w_out", "ffn2_gate", "ffn2_up", "ffn2_down"]
    late = ["ffn1_gate", "ffn1_up", "ffn1_down"]
    for n in early:
        adamw_big(n)
    done = sum(delta[n][0, :1, :1] for n in early)

    def arrived(started, own, which):
        landed, = _exchange_wait(started, done, False, f"g1_{which}_wait")
        return with_own(landed, own)

    parts8["ffn1_gate"] = arrived(g1_wg, own_wg, "dwg")
    parts8["ffn1_up"] = arrived(g1_wu, own_wu, "dwu")
    parts8["ffn1_down"] = arrived(g1_wd, own_wd, "dwd")
    for n in late:
        adamw_big(n)
    small_names = [n for n in names if n not in early + late]
    two_d = lambda a: a.reshape(-1, a.shape[-1])
    small_out = _adamw_small(*[[two_d(src[n]) for n in small_names] for src in (weights, grads, ms, vs)],
                             "adamw_small")
    for dst, outs in zip((delta, new_m, new_v), small_out):
        for n, a in zip(small_names, outs):
            dst[n] = a.reshape(weights[n].shape)

    return (loss, grad_x[None], *[grads[n] for n in names], *[delta[n] for n in names],
            *[new_m[n] for n in names], *[new_v[n] for n in names])
```

```python
import functools

import jax
import jax.numpy as jnp
from jax import lax
from jax.experimental import pallas as pl
from jax.experimental.pallas import tpu as pltpu

F32 = jnp.float32
MXU_DTYPE = jnp.bfloat16
WIRE_DTYPE = jnp.bfloat16
EPS = 1e-6
N_DEV = 8
GDN_HEADS = 4
HEAD_DIM = 128
GDN_WIDTH = GDN_HEADS * HEAD_DIM
POOL_WINDOWS = (2, 4, 8, 16)
POOL_GROUPS = len(POOL_WINDOWS)
POOL_WIDTH = 512
CONV_K = 4
CHUNK = 64
QKV_WIDTH = 3 * GDN_WIDTH
D_IN = 4 * GDN_WIDTH + 2 * GDN_HEADS + POOL_WIDTH
D_IN_PAD = 4 * GDN_WIDTH + POOL_WIDTH + 128
COL_Z = QKV_WIDTH
COL_P = 4 * GDN_WIDTH
COL_BA = 4 * GDN_WIDTH + POOL_WIDTH
N_MOD = 9
HALO = 16
VMEM_LIMIT = 56 * 1024 * 1024
ADAM_LR, ADAM_B1, ADAM_B2, ADAM_EPS, ADAM_WD, ADAM_STEP = 0.001, 0.9, 0.999, 1e-08, 0.01, 10
FFN_TOKEN_TILE = 256
FFN_HIDDEN_TILE = 1408
CHUNKS_PER_STEP = 4
SCAN_CHUNKS_PER_STEP = 16

NT = (((1,), (1,)), ((), ()))
NN = (((1,), (0,)), ((), ()))
TN = (((0,), (0,)), ((), ()))


def _params(*sem):
    return pltpu.CompilerParams(dimension_semantics=tuple(sem), vmem_limit_bytes=VMEM_LIMIT)


def _dot(a, b, dims):
    return lax.dot_general(a, b, dims, preferred_element_type=F32)


def _mdot(a, b, dims):
    return _dot(a.astype(MXU_DTYPE), b.astype(MXU_DTYPE), dims)


def _split(a):
    hi = a.astype(jnp.bfloat16)
    return hi, (a - hi.astype(F32)).astype(jnp.bfloat16)


def _dot3(a, b, dims):
    (ah, al), (bh, bl) = a, b
    return (_dot(al, bh, dims) + _dot(ah, bl, dims)) + _dot(ah, bh, dims)


def _sigmoid(v):
    return 0.5 * jnp.tanh(0.5 * v) + 0.5


def _softplus(v):
    return jnp.maximum(v, 0.0) + jnp.log(1.0 + jnp.exp(-jnp.abs(v)))


def _shift_rows(v, s):
    n = v.shape[0]
    s = s % n
    return v if s == 0 else pltpu.roll(v, s, 0)


def _tile(n, want):
    t = min(n, want)
    while n % t:
        t //= 2
    return t


def _all_gather(blocks, name, dep=None):
    n = len(blocks)

    def body(*refs):
        x_refs, out_refs = refs[:n], refs[-3 - n:-3]
        send_sems, recv_sems, local_sems = refs[-3:]
        x, y, c = lax.axis_index("x"), lax.axis_index("y"), lax.axis_index("c")
        me, sibling = (x, y, c), (x, y, 1 - c)
        chips = [(1 - x, y), (x, 1 - y), (1 - x, 1 - y)]

        def copy(a, k, blk, to, own=False):
            rows = out_refs[a].at[4 * blk[0] + 2 * blk[1] + blk[2]]
            return pltpu.make_async_remote_copy(
                src_ref=x_refs[a] if own else rows, dst_ref=rows,
                send_sem=send_sems.at[7 * a + k], recv_sem=recv_sems.at[7 * a + k],
                device_id=to, device_id_type=pl.DeviceIdType.MESH)

        mine = [pltpu.make_async_copy(x_refs[a], out_refs[a].at[4 * x + 2 * y + c], local_sems.at[a])
                for a in range(n)]
        for cp in mine:
            cp.start()
        sent = []
        for a in range(n):
            sent.append(copy(a, 0, me, sibling, own=True))
            sent += [copy(a, 1 + j, me, (*chip, c), own=True) for j, chip in enumerate(chips)]
        for cp in sent:
            cp.start()
        for a in range(n):
            for j, chip in enumerate(chips):
                copy(a, 1 + j, (*chip, c), me).wait_recv()
                passed = copy(a, 4 + j, (*chip, c), sibling)
                passed.start()
                sent.append(passed)
        for a in range(n):
            copy(a, 0, sibling, me).wait_recv()
            for j, chip in enumerate(chips):
                copy(a, 4 + j, (*chip, 1 - c), me).wait_recv()
        for cp in sent:
            cp.wait_send()
        for cp in mine:
            cp.wait()

    hbm = pl.BlockSpec(memory_space=pltpu.HBM)
    return pl.pallas_call(
        body, name=name,
        out_shape=[jax.ShapeDtypeStruct((N_DEV,) + b.shape, b.dtype) for b in blocks],
        in_specs=[hbm] * n + [pl.BlockSpec(memory_space=pl.ANY)] * (dep is not None),
        out_specs=[hbm] * n,
        scratch_shapes=[pltpu.SemaphoreType.DMA((7 * n,)), pltpu.SemaphoreType.DMA((7 * n,)),
                        pltpu.SemaphoreType.DMA((n,))],
    )(*(list(blocks) + ([] if dep is None else [dep])))


_HBM = pl.BlockSpec(memory_space=pltpu.HBM)
_SEM = pl.BlockSpec(memory_space=pltpu.SEMAPHORE)
_ANY = pl.BlockSpec(memory_space=pl.ANY)
_EFFECT = pltpu.SideEffectType.DATAFLOW_SIDE_EFFECTING
_FLIPS = [(0, 0, 1), (0, 1, 0), (0, 1, 1), (1, 0, 0), (1, 0, 1), (1, 1, 0), (1, 1, 1)]


def _peers():
    x, y, c = lax.axis_index("x"), lax.axis_index("y"), lax.axis_index("c")
    return 4 * x + 2 * y + c, [(1 - x if fx else x, 1 - y if fy else y, 1 - c if fc else c)
                               for fx, fy, fc in _FLIPS]


def _exchange_start(srcs, gather, dep, name):
    n = len(srcs)
    lands = [(N_DEV,) + tuple(s.shape if gather else s.shape[1:]) for s in srcs]

    def body(*refs):
        src_refs, land_refs = refs[:n], refs[n:2 * n]
        send_sems, recv_sems = refs[2 * n + 1], refs[2 * n + 2]
        token = refs[-1]
        me, peers = _peers()
        for a in range(n):
            for k, (px, py, pc) in enumerate(peers):
                pltpu.make_async_remote_copy(
                    src_ref=src_refs[a] if gather else src_refs[a].at[4 * px + 2 * py + pc],
                    dst_ref=land_refs[a].at[me],
                    send_sem=send_sems.at[7 * a + k], recv_sem=recv_sems.at[7 * a + k],
                    device_id=(px, py, pc), device_id_type=pl.DeviceIdType.MESH).start()
        token[...] = jnp.zeros_like(token)

    srcs = [pltpu.with_memory_space_constraint(s, pltpu.HBM) for s in srcs]
    empties = [pltpu.with_memory_space_constraint(lax.empty(shape, s.dtype), pltpu.HBM)
               for shape, s in zip(lands, srcs)]
    out = pl.pallas_call(
        body, name=name,
        out_shape=(pltpu.SemaphoreType.DMA((7 * n,)), pltpu.SemaphoreType.DMA((7 * n,)),
                   *[pltpu.HBM(shape, s.dtype) for shape, s in zip(lands, srcs)],
                   jax.ShapeDtypeStruct((8, 128), F32)),
        in_specs=(*[_HBM] * (2 * n), _ANY),
        out_specs=(_SEM, _SEM, *[_HBM] * n, pl.BlockSpec(memory_space=pltpu.VMEM)),
        input_output_aliases={n + a: 2 + a for a in range(n)},
        compiler_params=pltpu.CompilerParams(has_side_effects=_EFFECT),
    )(*srcs, *empties, dep)
    return out[0], out[1], srcs, list(out[2:2 + n]), out[-1]


def _exchange_wait(started, after, gather, name):
    send_sems, recv_sems, srcs, lands, _ = started
    n = len(srcs)

    def body(*refs):
        src_refs, land_refs = refs[:n], refs[n:2 * n]
        send_sems, recv_sems = refs[2 * n], refs[2 * n + 1]
        _, peers = _peers()
        for a in range(n):
            for k, peer in enumerate(peers):
                copy = pltpu.make_async_remote_copy(
                    src_ref=src_refs[a] if gather else src_refs[a].at[0], dst_ref=land_refs[a].at[0],
                    send_sem=send_sems.at[7 * a + k], recv_sem=recv_sems.at[7 * a + k],
                    device_id=peer, device_id_type=pl.DeviceIdType.MESH)
                copy.wait_send()
                copy.wait_recv()

    out = pl.pallas_call(
        body, name=name,
        out_shape=[pltpu.HBM(z.shape, z.dtype) for z in lands],
        in_specs=(*[_HBM] * (2 * n), _SEM, _SEM, _ANY), out_specs=[_HBM] * n,
        input_output_aliases={n + a: a for a in range(n)},
        compiler_params=pltpu.CompilerParams(has_side_effects=_EFFECT),
    )(*srcs, *lands, send_sems, recv_sems, after)
    return list(out)


def _matmul(pairs, dims, out_dtype, name, tm=512, tn=512, tk=512, dep=None):
    a0, b0 = pairs[0]
    if dims == TN:
        K, M = a0.shape
    else:
        M, K = a0.shape
    N = b0.shape[0] if dims == NT else b0.shape[1]
    tm, tn, tk = _tile(M, tm), _tile(N, tn), _tile(K, tk)
    nk = K // tk
    n_pairs = len(pairs)
    n_in = 2 * n_pairs + (dep is not None)

    def body(*refs):
        out_ref = refs[n_in]

        def product():
            total = _dot(refs[0][...], refs[1][...], dims)
            for p in range(1, n_pairs):
                total += _dot(refs[2 * p][...], refs[2 * p + 1][...], dims)
            return total

        if nk == 1:
            out_ref[...] = product().astype(out_ref.dtype)
            return
        acc_ref = refs[n_in + 1]
        k = pl.program_id(2)

        @pl.when(k == 0)
        def _():
            acc_ref[...] = product()

        @pl.when((k > 0) & (k < nk - 1))
        def _():
            acc_ref[...] += product()

        @pl.when(k == nk - 1)
        def _():
            out_ref[...] = (acc_ref[...] + product()).astype(out_ref.dtype)

    if dims == TN:
        a_spec = pl.BlockSpec((tk, tm), lambda i, j, k: (k, i))
    else:
        a_spec = pl.BlockSpec((tm, tk), lambda i, j, k: (i, k))
    if dims == NT:
        b_spec = pl.BlockSpec((tn, tk), lambda i, j, k: (j, k))
    else:
        b_spec = pl.BlockSpec((tk, tn), lambda i, j, k: (k, j))
    args, specs = [], []
    for a, b in pairs:
        args += [a, b]
        specs += [a_spec, b_spec]
    if dep is not None:
        args.append(dep)
        specs.append(_ANY)
    return pl.pallas_call(
        body, name=name, grid=(M // tm, N // tn, nk),
        in_specs=specs, out_specs=pl.BlockSpec((tm, tn), lambda i, j, k: (i, j)),
        out_shape=jax.ShapeDtypeStruct((M, N), out_dtype),
        scratch_shapes=[pltpu.VMEM((tm, tn), F32)] * (nk > 1),
        compiler_params=_params("parallel", "parallel", "arbitrary"),
    )(*args)


def _vec_spec(d):
    return pl.BlockSpec((1, d), lambda i: (0, 0))


def _matmul_resid_norm_mod(a, b, x, gate, coef, nw, scale, shift, name):
    T, K = a.shape
    D = b.shape[1]
    tm = _tile(T, 512)

    def body(a_ref, b_ref, x_ref, g_ref, nw_ref, sc_ref, sh_ref, y_ref, xo_ref, h_ref):
        y = _dot(a_ref[...], b_ref[...], NN)
        y_ref[...] = y
        xf = x_ref[...] + (coef * g_ref[...]) * y
        xo_ref[...] = xf
        r = lax.rsqrt(jnp.mean(xf * xf, axis=-1, keepdims=True) + EPS)
        h_ref[...] = ((xf * r) * nw_ref[...] * (1.0 + sc_ref[...]) + sh_ref[...]).astype(h_ref.dtype)

    row = pl.BlockSpec((tm, D), lambda i: (i, 0))
    vec = _vec_spec(D)
    return pl.pallas_call(
        body, name=name, grid=(T // tm,),
        in_specs=[pl.BlockSpec((tm, K), lambda i: (i, 0)), _resident((K, D)), row, vec, vec, vec, vec],
        out_specs=[row, row, row],
        out_shape=[jax.ShapeDtypeStruct((T, D), F32), jax.ShapeDtypeStruct((T, D), F32),
                   jax.ShapeDtypeStruct((T, D), MXU_DTYPE)],
        compiler_params=_params("parallel"),
    )(a, b, x, gate, nw, scale, shift)


def _norm_bwd(dh, x, nw, scale, dres, name, produced_by=None):
    T, D = x.shape
    tm = _tile(T, 512)

    def body(*refs):
        if isinstance(dh, tuple):
            dh_value = _dot(refs[0][...], refs[1][...], NN)
            refs = refs[1:]
        else:
            dh_value = refs[0][...]
        _, x_ref, nw_ref, sc_ref, dr_ref = refs[:5]
        n_in = 5 if produced_by is None else 7
        dx_ref, dnw_ref, dsc_ref, dsh_ref = refs[n_in:n_in + 4]

        @pl.when(pl.program_id(0) == 0)
        def _():
            dnw_ref[...] = jnp.zeros_like(dnw_ref)
            dsc_ref[...] = jnp.zeros_like(dsc_ref)
            dsh_ref[...] = jnp.zeros_like(dsh_ref)
            if produced_by is not None:
                refs[n_in + 5][...] = jnp.zeros_like(refs[n_in + 5])

        xf, dh_ = x_ref[...], dh_value
        r = lax.rsqrt(jnp.mean(xf * xf, axis=-1, keepdims=True) + EPS)
        xn = xf * r
        one_sc = 1.0 + sc_ref[...]
        dsh_ref[...] += jnp.sum(dh_, axis=0, keepdims=True)
        t = dh_ * xn
        dsc_ref[...] += jnp.sum(t, axis=0, keepdims=True) * nw_ref[...]
        dnw_ref[...] += jnp.sum(t, axis=0, keepdims=True) * one_sc
        dxn = dh_ * (nw_ref[...] * one_sc)
        dx = dr_ref[...] + r * (dxn - xn * jnp.mean(dxn * xn, axis=-1, keepdims=True))
        dx_ref[...] = dx
        if produced_by is not None:
            y_ref, g_ref, dy_ref, dg_ref = refs[5], refs[6], refs[n_in + 4], refs[n_in + 5]
            dy_ref[...] = ((produced_by[2] * g_ref[...]) * dx).astype(dy_ref.dtype)
            dg_ref[...] += produced_by[2] * jnp.sum(dx * y_ref[...], axis=0, keepdims=True)

    row = pl.BlockSpec((tm, D), lambda i: (i, 0))
    vec = _vec_spec(D)
    vec_out = jax.ShapeDtypeStruct((1, D), F32)
    if isinstance(dh, tuple):
        k_dim = dh[0].shape[1]
        args, in_specs = [dh[0], dh[1]], [pl.BlockSpec((tm, k_dim), lambda i: (i, 0)), _resident((k_dim, D))]
    else:
        args, in_specs = [dh], [row]
    args, in_specs = args + [x, nw, scale, dres], in_specs + [row, vec, vec, row]
    out_specs, out_shape = [row, vec, vec, vec], [jax.ShapeDtypeStruct((T, D), F32), vec_out, vec_out, vec_out]
    if produced_by is not None:
        args += [produced_by[0], produced_by[1]]
        in_specs += [row, vec]
        out_specs += [row, vec]
        out_shape += [jax.ShapeDtypeStruct((T, D), MXU_DTYPE), vec_out]
    return pl.pallas_call(
        body, name=name, grid=(T // tm,),
        in_specs=in_specs, out_specs=out_specs, out_shape=out_shape,
        compiler_params=_params("arbitrary"),
    )(*args)


def _rms(xf):
    r = lax.rsqrt(jnp.mean(xf * xf, axis=-1, keepdims=True) + EPS)
    return r, xf * r


def _norm_bwd_math(dh, xf, nw, sc):
    r, xn = _rms(xf)
    t = dh * xn
    dxn = dh * (nw * (1.0 + sc))
    dx = r * (dxn - xn * jnp.mean(dxn * xn, axis=-1, keepdims=True))
    return dx, jnp.sum(dh, axis=0, keepdims=True), jnp.sum(t, axis=0, keepdims=True)


def _swiglu_gate(x, wg_t, nw, scale, shift, name):
    T, D = x.shape
    Fdim = wg_t.shape[0]
    tm = _tile(T, 512)

    def body(x_ref, wg_ref, nw_ref, sc_ref, sh_ref, g_ref, h_ref):
        _, xn = _rms(x_ref[...])
        hh = (xn * nw_ref[...] * (1.0 + sc_ref[...]) + sh_ref[...]).astype(MXU_DTYPE)
        h_ref[...] = hh
        g_ref[...] = _dot(hh, wg_ref[...], NT).astype(g_ref.dtype)

    row = pl.BlockSpec((tm, D), lambda i: (i, 0))
    vec = _vec_spec(D)
    return pl.pallas_call(
        body, name=name, grid=(T // tm,),
        in_specs=[row, _resident((Fdim, D)), vec, vec, vec],
        out_specs=[pl.BlockSpec((tm, Fdim), lambda i: (i, 0)), row],
        out_shape=[jax.ShapeDtypeStruct((T, Fdim), MXU_DTYPE), jax.ShapeDtypeStruct((T, D), MXU_DTYPE)],
        compiler_params=_params("parallel"),
    )(x, wg_t, nw, scale, shift)


def _swiglu_up(h, wu_t, g, name):
    T, D = h.shape
    Fdim = wu_t.shape[0]
    tm = _tile(T, 512)

    def body(h_ref, wu_ref, g_ref, u_ref, a_ref):
        u = _dot(h_ref[...], wu_ref[...], NT)
        g = g_ref[...].astype(F32)
        u_ref[...] = u.astype(u_ref.dtype)
        a_ref[...] = ((g * _sigmoid(g)) * u).astype(a_ref.dtype)

    frow = pl.BlockSpec((tm, Fdim), lambda i: (i, 0))
    return pl.pallas_call(
        body, name=name, grid=(T // tm,),
        in_specs=[pl.BlockSpec((tm, D), lambda i: (i, 0)), _resident((Fdim, D)), frow],
        out_specs=[frow, frow],
        out_shape=[jax.ShapeDtypeStruct((T, Fdim), MXU_DTYPE)] * 2,
        compiler_params=_params("parallel"),
    )(h, wu_t, g)


def _swiglu_fwd_loss(h, wg_t, wu_t, wd, x, gate, fw, target, name):
    T, D = h.shape
    Fdim = wd.shape[0]
    tm, tf = _tile(T, FFN_TOKEN_TILE), _tile(Fdim, FFN_HIDDEN_TILE)
    row = pl.BlockSpec((tm, D), lambda i: (i, 0))
    frow = pl.BlockSpec((tm, Fdim), lambda i: (i, 0))
    vec, wres = _vec_spec(D), _resident((Fdim, D))
    vec_out = jax.ShapeDtypeStruct((1, D), F32)

    def body(h_ref, wg_ref, wu_ref, wd_ref, x_ref, gt_ref, fw_ref, t_ref,
             g_ref, u_ref, a_ref, loss_ref, dx_ref, dfw_ref, dy_ref, dg_ref):
        hh = h_ref[...]
        y = None
        for k in range(Fdim // tf):
            ks = slice(k * tf, (k + 1) * tf)
            g = _dot(hh, wg_ref[ks, :], NT)
            u = _dot(hh, wu_ref[ks, :], NT)
            a = ((g * _sigmoid(g)) * u).astype(a_ref.dtype)
            g_ref[:, ks] = g.astype(g_ref.dtype)
            u_ref[:, ks] = u.astype(u_ref.dtype)
            a_ref[:, ks] = a
            part = _dot(a, wd_ref[ks, :], NN)
            y = part if y is None else y + part

        @pl.when(pl.program_id(0) == 0)
        def _():
            loss_ref[...] = jnp.zeros_like(loss_ref)
            dfw_ref[...] = jnp.zeros_like(dfw_ref)
            dg_ref[...] = jnp.zeros_like(dg_ref)

        r, xn = _rms(x_ref[...] + (0.5 * gt_ref[...]) * y)
        err = xn * fw_ref[...] - t_ref[...]
        per_tok = jnp.mean(err * err, axis=-1, keepdims=True)
        loss_ref[...] += 0.5 * jnp.sum(per_tok, axis=0, keepdims=True)
        d_out = err * (1.0 / D)
        dfw_ref[...] += jnp.sum(d_out * xn, axis=0, keepdims=True)
        dxn = d_out * fw_ref[...]
        dx = r * (dxn - xn * jnp.mean(dxn * xn, axis=-1, keepdims=True))
        dx_ref[...] = dx
        dy_ref[...] = ((0.5 * gt_ref[...]) * dx).astype(dy_ref.dtype)
        dg_ref[...] += 0.5 * jnp.sum(dx * y, axis=0, keepdims=True)

    return pl.pallas_call(
        body, name=name, grid=(T // tm,),
        in_specs=[row, wres, wres, wres, row, vec, vec, row],
        out_specs=[frow, frow, frow, pl.BlockSpec((1, 128), lambda i: (0, 0)), row, vec, row, vec],
        out_shape=[jax.ShapeDtypeStruct((T, Fdim), MXU_DTYPE)] * 3
        + [jax.ShapeDtypeStruct((1, 128), F32), jax.ShapeDtypeStruct((T, D), F32), vec_out,
           jax.ShapeDtypeStruct((T, D), MXU_DTYPE), vec_out],
        compiler_params=_params("arbitrary"),
    )(h, wg_t, wu_t, wd, x, gate, fw, target)


def _swiglu_bwd(dy, wd, g, u, wg_t, wu_t, name, norm_in, dres, produced_by=None):
    T, D = dy.shape
    Fdim = wd.shape[0]
    tm, tf = _tile(T, FFN_TOKEN_TILE), _tile(Fdim, FFN_HIDDEN_TILE)
    row = pl.BlockSpec((tm, D), lambda i: (i, 0))
    frow = pl.BlockSpec((tm, Fdim), lambda i: (i, 0))
    vec, wres = _vec_spec(D), _resident((Fdim, D))
    vec_out = jax.ShapeDtypeStruct((1, D), F32)
    args, in_specs = [dy, wd, g, u, wg_t, wu_t, *norm_in, dres], [row, wres, frow, frow, wres, wres, row, vec, vec, row]
    out_shape = [jax.ShapeDtypeStruct((T, Fdim), MXU_DTYPE)] * 2 + [jax.ShapeDtypeStruct((T, D), F32)] + [vec_out] * 3
    out_specs = [frow, frow, row, vec, vec, vec]
    if produced_by is not None:
        args += [produced_by[0], produced_by[1]]
        in_specs += [row, vec]
        out_shape += [jax.ShapeDtypeStruct((T, D), MXU_DTYPE), vec_out]
        out_specs += [row, vec]

    def body(*refs):
        it = iter(refs)
        dy_ref, wd_ref, g_ref, u_ref, wg_ref, wu_ref, x_ref, nw_ref, sc_ref, dr_ref = [next(it) for _ in range(10)]
        prev_refs = [next(it), next(it)] if produced_by is not None else None
        dg_ref, du_ref, dx_ref, dnw_ref, dsc_ref, dsh_ref = [next(it) for _ in range(6)]
        prev_out = [next(it), next(it)] if produced_by is not None else None

        @pl.when(pl.program_id(0) == 0)
        def _():
            dnw_ref[...] = jnp.zeros_like(dnw_ref)
            dsc_ref[...] = jnp.zeros_like(dsc_ref)
            dsh_ref[...] = jnp.zeros_like(dsh_ref)
            if produced_by is not None:
                prev_out[1][...] = jnp.zeros_like(prev_out[1])

        dyy = dy_ref[...]
        dh = None
        for k in range(Fdim // tf):
            ks = slice(k * tf, (k + 1) * tf)
            da = _dot(dyy, wd_ref[ks, :], NT)
            gg = g_ref[:, ks].astype(F32)
            sig = _sigmoid(gg)
            dg = (da * u_ref[:, ks].astype(F32) * (sig * (1.0 + gg * (1.0 - sig)))).astype(dg_ref.dtype)
            du = (da * (gg * sig)).astype(du_ref.dtype)
            dg_ref[:, ks] = dg
            du_ref[:, ks] = du
            part = _dot(dg, wg_ref[ks, :], NN) + _dot(du, wu_ref[ks, :], NN)
            dh = part if dh is None else dh + part
        dx_norm, dsh_row, t_row = _norm_bwd_math(dh, x_ref[...], nw_ref[...], sc_ref[...])
        dsh_ref[...] += dsh_row
        dsc_ref[...] += t_row * nw_ref[...]
        dnw_ref[...] += t_row * (1.0 + sc_ref[...])
        dx = dr_ref[...] + dx_norm
        dx_ref[...] = dx
        if produced_by is not None:
            prev_out[0][...] = ((produced_by[2] * prev_refs[1][...]) * dx).astype(prev_out[0].dtype)
            prev_out[1][...] += produced_by[2] * jnp.sum(dx * prev_refs[0][...], axis=0, keepdims=True)

    return pl.pallas_call(
        body, name=name, grid=(T // tm,), in_specs=in_specs, out_specs=out_specs, out_shape=out_shape,
        compiler_params=_params("arbitrary"),
    )(*args)


def _resident(shape):
    return pl.BlockSpec(shape, lambda i: (0,) * len(shape), pipeline_mode=pl.Buffered(1))


def _conv_act(window, w):
    y = window * w[CONV_K - 1:CONV_K, :]
    for j in range(CONV_K - 1):
        y += _shift_rows(window, CONV_K - 1 - j) * w[j:j + 1, :]
    return y


def _gdn_prep(proj, conv_w, a_log_l, dt_bias_l, name):
    T = proj.shape[0]
    tm = _tile(T, 512)
    hb = tm // 8

    def body(cur_ref, halo_ref, ba_ref, w_ref, al_ref, dtb_ref, qkv_ref, bg_ref):
        i = pl.program_id(0)
        halo = jnp.where(i == 0, 0.0, halo_ref[...])
        window = jnp.concatenate([halo, cur_ref[...]], axis=0)
        y = _conv_act(window, w_ref[...])[8:, :]
        act = y * _sigmoid(y)
        for hh in range(3 * GDN_HEADS):
            blk = act[:, hh * HEAD_DIM:(hh + 1) * HEAD_DIM]
            if hh < 2 * GDN_HEADS:
                rn = lax.rsqrt(jnp.sum(blk * blk, axis=-1, keepdims=True) + EPS)
                blk = blk * rn
                if hh < GDN_HEADS:
                    blk = blk * (HEAD_DIM ** -0.5)
            qkv_ref[:, hh * HEAD_DIM:(hh + 1) * HEAD_DIM] = blk
        ba = ba_ref[...]
        lane = lax.broadcasted_iota(jnp.int32, ba.shape, 1)
        beta = _sigmoid(ba)
        g = -jnp.exp(al_ref[...]) * _softplus(ba + dtb_ref[...])
        bg_ref[...] = jnp.where(lane < GDN_HEADS, beta, jnp.where(lane < 2 * GDN_HEADS, g, 0.0))

    return pl.pallas_call(
        body, name=name, grid=(T // tm,),
        in_specs=[pl.BlockSpec((tm, QKV_WIDTH), lambda i: (i, 0)),
                  pl.BlockSpec((8, QKV_WIDTH), lambda i: (jnp.maximum(i * hb - 1, 0), 0)),
                  pl.BlockSpec((tm, 128), lambda i: (i, COL_BA // 128)),
                  pl.BlockSpec((CONV_K, QKV_WIDTH), lambda i: (0, 0)),
                  pl.BlockSpec((1, 128), lambda i: (0, 0)), pl.BlockSpec((1, 128), lambda i: (0, 0))],
        out_specs=[pl.BlockSpec((tm, QKV_WIDTH), lambda i: (i, 0)), pl.BlockSpec((tm, 128), lambda i: (i, 0))],
        out_shape=[jax.ShapeDtypeStruct((T, QKV_WIDTH), F32), jax.ShapeDtypeStruct((T, 128), F32)],
        compiler_params=_params("parallel"),
    )(proj, proj, proj, conv_w, a_log_l, dt_bias_l)


def _chunk_cumsum(v, reverse=False):
    row = lax.broadcasted_iota(jnp.int32, v.shape, 0)
    s = 1
    while s < CHUNK:
        if reverse:
            v = v + jnp.where(row < CHUNK - s, _shift_rows(v, -s), 0.0)
        else:
            v = v + jnp.where(row >= s, _shift_rows(v, s), 0.0)
        s *= 2
    return v


def _row_form(cols):
    padded = jnp.concatenate([cols, jnp.zeros((128 - CHUNK, 128), F32)], axis=0)
    return padded.T[:, :CHUNK]


def _chunk_masks():
    ri = lax.broadcasted_iota(jnp.int32, (CHUNK, CHUNK), 0)
    ci = lax.broadcasted_iota(jnp.int32, (CHUNK, CHUNK), 1)
    return ri >= ci, ri > ci, (ri == ci).astype(F32)


def _unit_lower_inverses(ms, eye):
    rs = [eye - m for m in ms]
    ps = [_split(-m) for m in ms]
    s = 2
    while s < CHUNK:
        ps = [_split(_dot3(p, p, NN)) for p in ps]
        r_parts = [_split(r) for r in rs]
        rs = [r + _dot3(p, rp, NN) for r, p, rp in zip(rs, ps, r_parts)]
        s *= 2
    return rs


def _head_elementwise(k, beta, gc, gcr, causal):
    decay = jnp.where(causal, jnp.exp(jnp.where(causal, gc - gcr, 0.0)), 0.0)
    return decay, k * beta, jnp.exp(gc)


def _head_slices(hh):
    return (slice(hh * HEAD_DIM, (hh + 1) * HEAD_DIM),
            slice(GDN_WIDTH + hh * HEAD_DIM, GDN_WIDTH + (hh + 1) * HEAD_DIM),
            slice(2 * GDN_WIDTH + hh * HEAD_DIM, 2 * GDN_WIDTH + (hh + 1) * HEAD_DIM))


def _gdn_chunk_fwd(qkv, bg, name):
    T = qkv.shape[0]
    cb = _tile(T // CHUNK, CHUNKS_PER_STEP)
    rows = cb * CHUNK

    def body(qkv_ref, bg_ref, tinv_ref, u_ref, w_ref, qd_ref, kd_ref, p_ref, cd_ref):
        masks = _chunk_masks()
        causal, strict, eye = masks
        heads = []
        for ci in range(cb):
            rs = slice(ci * CHUNK, (ci + 1) * CHUNK)
            bgv = bg_ref[rs, :]
            gc_all = _chunk_cumsum(bgv)
            gc_rows = _row_form(gc_all)
            cd_ref[rs, :] = jnp.exp(jnp.broadcast_to(gc_all[CHUNK - 1:CHUNK, :], (CHUNK, 128)))
            for hh in range(GDN_HEADS):
                qs, ks, vs = _head_slices(hh)
                q, k, v = qkv_ref[rs, qs], qkv_ref[rs, ks], qkv_ref[rs, vs]
                beta = bgv[:, hh:hh + 1]
                gc = gc_all[:, GDN_HEADS + hh:GDN_HEADS + hh + 1]
                decay, kb, eg = _head_elementwise(k, beta, gc, gc_rows[GDN_HEADS + hh:GDN_HEADS + hh + 1, :], causal)
                hs = slice(hh * HEAD_DIM, (hh + 1) * HEAD_DIM)
                cs = slice(hh * CHUNK, (hh + 1) * CHUNK)
                qd_ref[rs, hs] = (q * eg).astype(qd_ref.dtype)
                kd_ref[rs, hs] = (k * jnp.exp(gc[CHUNK - 1:CHUNK, :] - gc)).astype(kd_ref.dtype)
                heads.append((rs, hs, cs, q, k, v * beta, kb, kb * eg, decay))
        kks = [_mdot(kb, k, NT) for (_, _, _, _, k, _, kb, _, _) in heads]
        qks = [_mdot(q, k, NT) for (_, _, _, q, k, _, _, _, _) in heads]
        tinvs = _unit_lower_inverses([jnp.where(strict, kk * hd[8], 0.0) for kk, hd in zip(kks, heads)], eye)
        t_parts = [_split(t) for t in tinvs]
        us = [_dot3(tp, _split(hd[5]), NN) for tp, hd in zip(t_parts, heads)]
        ws = [_dot3(tp, _split(hd[7]), NN) for tp, hd in zip(t_parts, heads)]
        for hd, tinv, u, w, qk in zip(heads, tinvs, us, ws, qks):
            rs, hs, cs = hd[0], hd[1], hd[2]
            tinv_ref[rs, cs] = tinv
            u_ref[rs, hs] = u
            w_ref[rs, hs] = w.astype(w_ref.dtype)
            p_ref[rs, cs] = jnp.where(causal, qk * hd[8], 0.0).astype(p_ref.dtype)

    def spec(width):
        return pl.BlockSpec((rows, width), lambda n: (n, 0))

    hw, cw = GDN_WIDTH, GDN_HEADS * CHUNK
    return pl.pallas_call(
        body, name=name, grid=(T // rows,),
        in_specs=[spec(QKV_WIDTH), spec(128)],
        out_specs=[spec(cw), spec(hw), spec(hw), spec(hw), spec(hw), spec(cw), spec(128)],
        out_shape=[jax.ShapeDtypeStruct((T, cw), F32), jax.ShapeDtypeStruct((T, hw), F32),
                   jax.ShapeDtypeStruct((T, hw), MXU_DTYPE), jax.ShapeDtypeStruct((T, hw), MXU_DTYPE),
                   jax.ShapeDtypeStruct((T, hw), MXU_DTYPE), jax.ShapeDtypeStruct((T, cw), MXU_DTYPE),
                   jax.ShapeDtypeStruct((T, 128), F32)],
        compiler_params=_params("parallel"),
    )(qkv, bg)


def _gdn_scan_fwd(u, w, qd, kd, p, cd, name):
    T = u.shape[0]
    cb = _tile(T // CHUNK, SCAN_CHUNKS_PER_STEP)
    rows = cb * CHUNK

    def body(u_ref, w_ref, qd_ref, kd_ref, p_ref, cd_ref, o_ref, s_all_ref, vn_ref, s_ref):
        @pl.when(pl.program_id(0) == 0)
        def _():
            s_ref[...] = jnp.zeros_like(s_ref)

        hss = [slice(hh * HEAD_DIM, (hh + 1) * HEAD_DIM) for hh in range(GDN_HEADS)]
        css = [slice(hh * CHUNK, (hh + 1) * CHUNK) for hh in range(GDN_HEADS)]
        s_cur = [s_ref[hh] for hh in range(GDN_HEADS)]
        for ci in range(cb):
            rs = slice(ci * CHUNK, (ci + 1) * CHUNK)
            for hh in range(GDN_HEADS):
                s_all_ref[ci * GDN_WIDTH + hh * HEAD_DIM:ci * GDN_WIDTH + (hh + 1) * HEAD_DIM, :] = s_cur[hh]
            s_ms = [s.astype(MXU_DTYPE) for s in s_cur]
            w_s = [_dot(w_ref[rs, hs], s_m, NN) for hs, s_m in zip(hss, s_ms)]
            q_s = [_dot(qd_ref[rs, hs], s_m, NN) for hs, s_m in zip(hss, s_ms)]
            v_ms = [(u_ref[rs, hs] - ws_).astype(MXU_DTYPE) for hs, ws_ in zip(hss, w_s)]
            k_v = [_dot(kd_ref[rs, hs], v_m, TN) for hs, v_m in zip(hss, v_ms)]
            p_v = [_dot(p_ref[rs, cs], v_m, NN) for cs, v_m in zip(css, v_ms)]
            for hh in range(GDN_HEADS):
                vn_ref[rs, hss[hh]] = v_ms[hh]
                o_ref[rs, hss[hh]] = q_s[hh] + p_v[hh]
                c_dec = cd_ref[ci * CHUNK:ci * CHUNK + 1, GDN_HEADS + hh:GDN_HEADS + hh + 1]
                s_cur[hh] = s_cur[hh] * c_dec + k_v[hh]
        for hh in range(GDN_HEADS):
            s_ref[hh] = s_cur[hh]

    def spec(width):
        return pl.BlockSpec((rows, width), lambda n: (n, 0))

    hw, cw = GDN_WIDTH, GDN_HEADS * CHUNK
    return pl.pallas_call(
        body, name=name, grid=(T // rows,),
        in_specs=[spec(hw), spec(hw), spec(hw), spec(hw), spec(cw), spec(128)],
        out_specs=[spec(hw), pl.BlockSpec((cb * GDN_WIDTH, HEAD_DIM), lambda n: (n, 0)), spec(hw)],
        out_shape=[jax.ShapeDtypeStruct((T, hw), F32),
                   jax.ShapeDtypeStruct((T // CHUNK * GDN_WIDTH, HEAD_DIM), F32),
                   jax.ShapeDtypeStruct((T, hw), MXU_DTYPE)],
        scratch_shapes=[pltpu.VMEM((GDN_HEADS, HEAD_DIM, HEAD_DIM), F32)],
        compiler_params=_params("arbitrary"),
    )(u, w, qd, kd, p, cd)


def _gdn_scan_bwd(do, w, qd, kd, p, cd, s_all, vn, name):
    T = do.shape[0]
    cb = _tile(T // CHUNK, SCAN_CHUNKS_PER_STEP)
    rows = cb * CHUNK
    n_steps = T // rows

    def body(do_ref, w_ref, qd_ref, kd_ref, p_ref, cd_ref, s_all_ref, vn_ref,
             dvn_ref, dw_ref, dqd_ref, dkd_ref, dp_ref, dcd_ref, ds_ref):
        @pl.when(pl.program_id(0) == 0)
        def _():
            ds_ref[...] = jnp.zeros_like(ds_ref)

        causal, _, _ = _chunk_masks()
        lane = lax.broadcasted_iota(jnp.int32, (CHUNK, 128), 1)
        heads = range(GDN_HEADS)
        hss = [slice(hh * HEAD_DIM, (hh + 1) * HEAD_DIM) for hh in heads]
        css = [slice(hh * CHUNK, (hh + 1) * CHUNK) for hh in heads]
        ds_cur = [ds_ref[hh] for hh in heads]
        for ci in reversed(range(cb)):
            rs = slice(ci * CHUNK, (ci + 1) * CHUNK)
            ds_ms = [d.astype(MXU_DTYPE) for d in ds_cur]
            s_olds = [s_all_ref[ci * GDN_WIDTH + hh * HEAD_DIM:ci * GDN_WIDTH + (hh + 1) * HEAD_DIM, :] for hh in heads]
            s_ms = [s.astype(MXU_DTYPE) for s in s_olds]
            do_ms = [do_ref[rs, hs].astype(MXU_DTYPE) for hs in hss]
            p_do = [_dot(p_ref[rs, cs], do_m, TN) for cs, do_m in zip(css, do_ms)]
            k_ds = [_dot(kd_ref[rs, hs], ds_m, NN) for hs, ds_m in zip(hss, ds_ms)]
            q_do = [_dot(qd_ref[rs, hs], do_m, TN) for hs, do_m in zip(hss, do_ms)]
            dqds = [_dot(do_m, s_m, NT) for do_m, s_m in zip(do_ms, s_ms)]
            dkds = [_dot(vn_ref[rs, hs], ds_m, NT) for hs, ds_m in zip(hss, ds_ms)]
            dps = [_dot(do_m, vn_ref[rs, hs], NT) for hs, do_m in zip(hss, do_ms)]
            dv_news = [a + b for a, b in zip(p_do, k_ds)]
            dvn_ms = [d.astype(MXU_DTYPE) for d in dv_news]
            w_dv = [_dot(w_ref[rs, hs], dvn_m, TN) for hs, dvn_m in zip(hss, dvn_ms)]
            dws = [_dot(dvn_m, s_m, NT) for dvn_m, s_m in zip(dvn_ms, s_ms)]
            dcd_tile = jnp.zeros((CHUNK, 128), F32)
            for hh in heads:
                dvn_ref[rs, hss[hh]] = dv_news[hh]
                dw_ref[rs, hss[hh]] = -dws[hh]
                dqd_ref[rs, hss[hh]] = dqds[hh]
                dkd_ref[rs, hss[hh]] = dkds[hh]
                dp_ref[rs, css[hh]] = jnp.where(causal, dps[hh], 0.0)
                dcd = jnp.sum(jnp.sum(s_olds[hh] * ds_cur[hh], axis=1, keepdims=True), axis=0, keepdims=True)
                dcd_tile = jnp.where(lane == GDN_HEADS + hh, dcd, dcd_tile)
                c_dec = cd_ref[ci * CHUNK:ci * CHUNK + 1, GDN_HEADS + hh:GDN_HEADS + hh + 1]
                ds_cur[hh] = c_dec * ds_cur[hh] + q_do[hh] - w_dv[hh]
            dcd_ref[rs, :] = dcd_tile
        for hh in heads:
            ds_ref[hh] = ds_cur[hh]

    def spec(width):
        return pl.BlockSpec((rows, width), lambda n: (n_steps - 1 - n, 0))

    hw, cw = GDN_WIDTH, GDN_HEADS * CHUNK
    return pl.pallas_call(
        body, name=name, grid=(n_steps,),
        in_specs=[spec(hw), spec(hw), spec(hw), spec(hw), spec(cw), spec(128),
                  pl.BlockSpec((cb * GDN_WIDTH, HEAD_DIM), lambda n: (n_steps - 1 - n, 0)), spec(hw)],
        out_specs=[spec(hw), spec(hw), spec(hw), spec(hw), spec(cw), spec(128)],
        out_shape=[jax.ShapeDtypeStruct((T, hw), F32)] * 4
        + [jax.ShapeDtypeStruct((T, cw), F32), jax.ShapeDtypeStruct((T, 128), F32)],
        scratch_shapes=[pltpu.VMEM((GDN_HEADS, HEAD_DIM, HEAD_DIM), F32)],
        compiler_params=_params("arbitrary"),
    )(do, w, qd, kd, p, cd, s_all, vn)


def _gdn_chunk_bwd(qkv, bg, tinv_all, u, w, dvn, dw, dqd, dkd, dp, dcd, name):
    T = qkv.shape[0]
    cb = _tile(T // CHUNK, CHUNKS_PER_STEP)
    rows = cb * CHUNK

    def body(qkv_ref, bg_ref, tinv_ref, u_ref, w_ref, dvn_ref, dw_ref, dqd_ref, dkd_ref, dp_ref, dcd_ref,
             dqkv_ref, dbg_ref):
        masks = _chunk_masks()
        causal, strict, _ = masks
        lane = lax.broadcasted_iota(jnp.int32, (CHUNK, 128), 1)
        row = lax.broadcasted_iota(jnp.int32, (CHUNK, 128), 0)
        heads = []
        for ci in range(cb):
            rs = slice(ci * CHUNK, (ci + 1) * CHUNK)
            bgv = bg_ref[rs, :]
            gc_all = _chunk_cumsum(bgv)
            gc_rows = _row_form(gc_all)
            for hh in range(GDN_HEADS):
                qs, ks, vs = _head_slices(hh)
                q, k = qkv_ref[rs, qs], qkv_ref[rs, ks]
                beta = bgv[:, hh:hh + 1]
                gc = gc_all[:, GDN_HEADS + hh:GDN_HEADS + hh + 1]
                decay, kb, eg = _head_elementwise(k, beta, gc, gc_rows[GDN_HEADS + hh:GDN_HEADS + hh + 1, :], causal)
                heads.append(dict(ci=ci, hh=hh, rs=rs, hs=slice(hh * HEAD_DIM, (hh + 1) * HEAD_DIM),
                                  cs=slice(hh * CHUNK, (hh + 1) * CHUNK), q=q, k=k, beta=beta, gc=gc,
                                  decay=decay, kb=kb, eg=eg))
        for hd in heads:
            hd["t"] = _split(tinv_ref[hd["rs"], hd["cs"]])
        for hd in heads:
            hd["kk"] = _mdot(hd["kb"], hd["k"], NT)
            hd["qk"] = _mdot(hd["q"], hd["k"], NT)
        for hd in heads:
            hd["dvb"] = _dot3(hd["t"], _split(dvn_ref[hd["rs"], hd["hs"]]), TN)
            hd["dkbeg"] = _dot3(hd["t"], _split(dw_ref[hd["rs"], hd["hs"]]), TN)
        for hd in heads:
            rs, hs = hd["rs"], hd["hs"]
            da = -(_mdot(hd["dvb"], u_ref[rs, hs], NT) + _mdot(hd["dkbeg"], w_ref[rs, hs], NT))
            dm = jnp.where(strict, da, 0.0)
            dp_ = dp_ref[rs, hd["cs"]]
            hd["dkk"] = dm * hd["decay"]
            hd["dqk"] = dp_ * hd["decay"]
            hd["e"] = (hd["dkk"] * hd["kk"] + hd["dqk"] * hd["qk"])
        for hd in heads:
            hd["dkb"] = _mdot(hd["dkk"], hd["k"], NN)
            hd["dk"] = _mdot(hd["dkk"], hd["kb"], TN) + _mdot(hd["dqk"], hd["q"], TN)
            hd["dq"] = _mdot(hd["dqk"], hd["k"], NN)
            onehot = (lane == GDN_HEADS + hd["hh"]).astype(jnp.bfloat16)
            e_hi, e_lo = _split(hd["e"])
            hd["col_sums"] = _dot(e_lo, onehot, TN) + _dot(e_hi, onehot, TN)
        tiles = {}
        for hd in heads:
            ci, hh, rs, hs = hd["ci"], hd["hh"], hd["rs"], hd["hs"]
            qs, ks, vs = _head_slices(hh)
            q, k, beta, gc, eg, kb = hd["q"], hd["k"], hd["beta"], hd["gc"], hd["eg"], hd["kb"]
            v = qkv_ref[rs, vs]
            dqd_, dkd_ = dqd_ref[rs, hs], dkd_ref[rs, hs]
            gl = gc[CHUNK - 1:CHUNK, :]
            ek = jnp.exp(gl - gc)
            dkb = hd["dkb"] + hd["dkbeg"] * eg
            deg = jnp.sum(dqd_ * q, axis=1, keepdims=True) + jnp.sum(hd["dkbeg"] * kb, axis=1, keepdims=True)
            dek = jnp.sum(dkd_ * k, axis=1, keepdims=True)
            dcd_ = dcd_ref[ci * CHUNK:ci * CHUNK + 1, GDN_HEADS + hh:GDN_HEADS + hh + 1]
            dgl = jnp.sum(dek * ek, axis=0, keepdims=True) + dcd_ * jnp.exp(gl)
            dgc = jnp.sum(hd["e"], axis=1, keepdims=True) + deg * eg - dek * ek
            dbeta_tile, dgc_tile = tiles.get(ci, (jnp.zeros((CHUNK, 128), F32), jnp.zeros((CHUNK, 128), F32)))
            dgc_tile += jnp.where(lane == GDN_HEADS + hh, dgc, 0.0) - hd["col_sums"]
            dgc_tile += jnp.where((lane == GDN_HEADS + hh) & (row == CHUNK - 1), dgl, 0.0)
            dbeta = jnp.sum(dkb * k, axis=1, keepdims=True) + jnp.sum(hd["dvb"] * v, axis=1, keepdims=True)
            dbeta_tile += jnp.where(lane == hh, dbeta, 0.0)
            tiles[ci] = (dbeta_tile, dgc_tile)
            dqkv_ref[rs, qs] = hd["dq"] + dqd_ * eg
            dqkv_ref[rs, ks] = hd["dk"] + dkd_ * ek + dkb * beta
            dqkv_ref[rs, vs] = hd["dvb"] * beta
        for ci in range(cb):
            dbeta_tile, dgc_tile = tiles[ci]
            dbg_ref[ci * CHUNK:(ci + 1) * CHUNK, :] = dbeta_tile + _chunk_cumsum(dgc_tile, reverse=True)

    def spec(width):
        return pl.BlockSpec((rows, width), lambda n: (n, 0))

    hw, cw = GDN_WIDTH, GDN_HEADS * CHUNK
    return pl.pallas_call(
        body, name=name, grid=(T // rows,),
        in_specs=[spec(QKV_WIDTH), spec(128), spec(cw), spec(hw), spec(hw), spec(hw), spec(hw), spec(hw),
                  spec(hw), spec(cw), spec(128)],
        out_specs=[spec(QKV_WIDTH), spec(128)],
        out_shape=[jax.ShapeDtypeStruct((T, QKV_WIDTH), F32), jax.ShapeDtypeStruct((T, 128), F32)],
        compiler_params=_params("parallel"),
    )(qkv, bg, tinv_all, u, w, dvn, dw, dqd, dkd, dp, dcd)


def _pool_counts(i, tm, rows, offset):
    t = i * tm - offset + lax.broadcasted_iota(jnp.int32, (rows, 1), 0)
    return [jnp.minimum(t + 1, w).astype(F32) for w in POOL_WINDOWS]


def _window_sums(window, forward):
    sums, s, step = [], window, 1
    for _ in POOL_WINDOWS:
        s = s + _shift_rows(s, -step if forward else step)
        sums.append(s)
        step *= 2
    return sums


def _pooled(window, counts):
    sums = _window_sums(window, forward=False)
    out = []
    for gi in range(POOL_GROUPS):
        sl = slice(gi * 128, (gi + 1) * 128)
        out.append(sums[gi][HALO:, sl] / counts[gi] - window[HALO:, sl])
    return out


def _mix_post(o, proj, gdn_norm, pool_w, pool_scale, name):
    T = o.shape[0]
    tm = _tile(T, 512)
    hb = tm // HALO

    def body(o_ref, z_ref, p_ref, ph_ref, gn_ref, pw_ref, ps_ref, out_ref):
        i = pl.program_id(0)
        for hh in range(GDN_HEADS):
            sl = slice(hh * HEAD_DIM, (hh + 1) * HEAD_DIM)
            oh, zh = o_ref[:, sl], z_ref[:, sl]
            ro = lax.rsqrt(jnp.mean(oh * oh, axis=-1, keepdims=True) + EPS)
            out_ref[:, sl] = (((oh * ro) * gn_ref[...]) * (zh * _sigmoid(zh))).astype(out_ref.dtype)
        halo = jnp.where(i == 0, 0.0, ph_ref[...])
        window = jnp.concatenate([halo, p_ref[...]], axis=0)
        pooled = _pooled(window, _pool_counts(i, tm, tm, 0))
        for gi in range(POOL_GROUPS):
            pm = _mdot(pooled[gi], pw_ref[gi], NN)
            out_ref[:, GDN_WIDTH + gi * 128:GDN_WIDTH + (gi + 1) * 128] = (
                pm * ps_ref[:, gi * 128:(gi + 1) * 128]).astype(out_ref.dtype)

    return pl.pallas_call(
        body, name=name, grid=(T // tm,),
        in_specs=[pl.BlockSpec((tm, GDN_WIDTH), lambda i: (i, 0)),
                  pl.BlockSpec((tm, GDN_WIDTH), lambda i: (i, COL_Z // GDN_WIDTH)),
                  pl.BlockSpec((tm, POOL_WIDTH), lambda i: (i, COL_P // POOL_WIDTH)),
                  pl.BlockSpec((HALO, POOL_WIDTH), lambda i: (jnp.maximum(i * hb - 1, 0), COL_P // POOL_WIDTH)),
                  pl.BlockSpec((1, HEAD_DIM), lambda i: (0, 0)),
                  pl.BlockSpec((POOL_GROUPS, 128, 128), lambda i: (0, 0, 0)),
                  pl.BlockSpec((1, POOL_WIDTH), lambda i: (0, 0))],
        out_specs=pl.BlockSpec((tm, GDN_WIDTH + POOL_WIDTH), lambda i: (i, 0)),
        out_shape=jax.ShapeDtypeStruct((T, GDN_WIDTH + POOL_WIDTH), MXU_DTYPE),
        compiler_params=_params("parallel"),
    )(o, proj, proj, proj, gdn_norm, pool_w, pool_scale)


def _mix_post_bwd(dmix, o, proj, gdn_norm, pool_w, pool_scale, name):
    T = o.shape[0]
    tm = _tile(T, 512)
    hb = tm // HALO
    n_tiles = T // tm

    def body(dg_ref, dpo_ref, dpo_next_ref, o_ref, z_ref, p_ref, ph_ref, gn_ref, pw_ref, ps_ref,
             do_ref, dzp_ref, dgn_ref, dpw_ref, dps_ref):
        i = pl.program_id(0)

        @pl.when(i == 0)
        def _():
            dgn_ref[...] = jnp.zeros_like(dgn_ref)
            dpw_ref[...] = jnp.zeros_like(dpw_ref)
            dps_ref[...] = jnp.zeros_like(dps_ref)

        gn = gn_ref[...]
        dgn = jnp.zeros((1, HEAD_DIM), F32)
        for hh in range(GDN_HEADS):
            sl = slice(hh * HEAD_DIM, (hh + 1) * HEAD_DIM)
            oh, zh, dy = o_ref[:, sl], z_ref[:, sl], dg_ref[:, sl]
            ro = lax.rsqrt(jnp.mean(oh * oh, axis=-1, keepdims=True) + EPS)
            on = oh * ro
            sig = _sigmoid(zh)
            sz = zh * sig
            dzp_ref[:, sl] = (dy * (on * gn) * (sig * (1.0 + zh * (1.0 - sig)))).astype(dzp_ref.dtype)
            dgn += jnp.sum(dy * on * sz, axis=0, keepdims=True)
            don = dy * gn * sz
            do_ref[:, sl] = ro * (don - on * jnp.mean(don * on, axis=-1, keepdims=True))
        dgn_ref[...] += dgn

        halo = jnp.where(i == 0, 0.0, ph_ref[...])
        window = jnp.concatenate([halo, p_ref[...]], axis=0)
        counts = _pool_counts(i, tm, tm + HALO, 0)
        pooled = _pooled(window, [cn[:tm] for cn in counts])
        nxt = jnp.where(i == n_tiles - 1, 0.0, dpo_next_ref[...])
        dpo_w = jnp.concatenate([dpo_ref[...], nxt], axis=0)
        ps = ps_ref[...]
        dps = []
        scaled = []
        for gi in range(POOL_GROUPS):
            sl = slice(gi * 128, (gi + 1) * 128)
            dpm = dpo_w[:, sl] * ps[:, sl]
            pm = _mdot(pooled[gi], pw_ref[gi], NN)
            dps.append(jnp.sum(dpo_w[:tm, sl] * pm, axis=0, keepdims=True))
            dpw_ref[gi] += _mdot(pooled[gi], dpm[:tm], TN)
            dpooled = _mdot(dpm, pw_ref[gi], NT)
            scaled.append((dpooled, dpooled / counts[gi]))
        dps_ref[...] += jnp.concatenate(dps, axis=1)
        lead = _window_sums(jnp.concatenate([sc for _, sc in scaled], axis=1), forward=True)
        for gi in range(POOL_GROUPS):
            sl = slice(gi * 128, (gi + 1) * 128)
            dzp_ref[:, GDN_WIDTH + gi * 128:GDN_WIDTH + (gi + 1) * 128] = (
                lead[gi][:tm, sl] - scaled[gi][0][:tm]).astype(dzp_ref.dtype)

    last_halo = T // HALO - 1
    return pl.pallas_call(
        body, name=name, grid=(n_tiles,),
        in_specs=[pl.BlockSpec((tm, GDN_WIDTH), lambda i: (i, 0)),
                  pl.BlockSpec((tm, POOL_WIDTH), lambda i: (i, 1)),
                  pl.BlockSpec((HALO, POOL_WIDTH), lambda i: (jnp.minimum((i + 1) * hb, last_halo), 1)),
                  pl.BlockSpec((tm, GDN_WIDTH), lambda i: (i, 0)),
                  pl.BlockSpec((tm, GDN_WIDTH), lambda i: (i, COL_Z // GDN_WIDTH)),
                  pl.BlockSpec((tm, POOL_WIDTH), lambda i: (i, COL_P // POOL_WIDTH)),
                  pl.BlockSpec((HALO, POOL_WIDTH), lambda i: (jnp.maximum(i * hb - 1, 0), COL_P // POOL_WIDTH)),
                  pl.BlockSpec((1, HEAD_DIM), lambda i: (0, 0)),
                  pl.BlockSpec((POOL_GROUPS, 128, 128), lambda i: (0, 0, 0)),
                  pl.BlockSpec((1, POOL_WIDTH), lambda i: (0, 0))],
        out_specs=[pl.BlockSpec((tm, GDN_WIDTH), lambda i: (i, 0)),
                   pl.BlockSpec((tm, GDN_WIDTH + POOL_WIDTH), lambda i: (i, 0)),
                   pl.BlockSpec((1, HEAD_DIM), lambda i: (0, 0)),
                   pl.BlockSpec((POOL_GROUPS, 128, 128), lambda i: (0, 0, 0)),
                   pl.BlockSpec((1, POOL_WIDTH), lambda i: (0, 0))],
        out_shape=[jax.ShapeDtypeStruct((T, GDN_WIDTH), F32),
                   jax.ShapeDtypeStruct((T, GDN_WIDTH + POOL_WIDTH), MXU_DTYPE),
                   jax.ShapeDtypeStruct((1, HEAD_DIM), F32),
                   jax.ShapeDtypeStruct((POOL_GROUPS, 128, 128), F32),
                   jax.ShapeDtypeStruct((1, POOL_WIDTH), F32)],
        compiler_params=_params("arbitrary"),
    )(dmix, dmix, dmix, o, proj, proj, proj, gdn_norm, pool_w, pool_scale)


def _gdn_prep_bwd(proj, conv_w, a_log_l, dt_bias_l, dqkv, dbg, dzp, name):
    T = proj.shape[0]
    tm = _tile(T, 512)
    hb = tm // 8
    n_tiles = T // tm
    last_halo = T // 8 - 1

    def body(cur_ref, before_ref, after_ref, ba_ref, w_ref, al_ref, dtb_ref, dq_ref, dq_after_ref, dbg_ref,
             dzp_ref, dproj_ref, dw_ref, dal_ref, ddtb_ref):
        i = pl.program_id(0)

        @pl.when(i == 0)
        def _():
            dw_ref[...] = jnp.zeros_like(dw_ref)
            dal_ref[...] = jnp.zeros_like(dal_ref)
            ddtb_ref[...] = jnp.zeros_like(ddtb_ref)

        last = i == n_tiles - 1
        w = w_ref[...]
        before = jnp.where(i == 0, 0.0, before_ref[...])
        after = jnp.where(last, 0.0, after_ref[...])
        window = jnp.concatenate([before, cur_ref[...], after], axis=0)
        y = _conv_act(window, w)
        sig = _sigmoid(y)
        act = y * sig
        dq_w = jnp.concatenate([jnp.zeros((8, QKV_WIDTH), F32), dq_ref[...],
                                jnp.where(last, 0.0, dq_after_ref[...])], axis=0)
        dact = []
        for hh in range(3 * GDN_HEADS):
            sl = slice(hh * HEAD_DIM, (hh + 1) * HEAD_DIM)
            blk, dblk = act[:, sl], dq_w[:, sl]
            if hh < 2 * GDN_HEADS:
                rn = lax.rsqrt(jnp.sum(blk * blk, axis=-1, keepdims=True) + EPS)
                unit = blk * rn
                if hh < GDN_HEADS:
                    dblk = dblk * (HEAD_DIM ** -0.5)
                dblk = rn * (dblk - unit * jnp.sum(dblk * unit, axis=-1, keepdims=True))
            dact.append(dblk)
        dy = jnp.concatenate(dact, axis=1) * (sig * (1.0 + y * (1.0 - sig)))
        dx = dy * w[CONV_K - 1:CONV_K, :]
        dws = [None] * CONV_K
        dws[CONV_K - 1] = jnp.sum(dy[8:8 + tm] * window[8:8 + tm], axis=0, keepdims=True)
        for j in range(CONV_K - 1):
            s = CONV_K - 1 - j
            dx += _shift_rows(dy, -s) * w[j:j + 1, :]
            dws[j] = jnp.sum(dy[8:8 + tm] * _shift_rows(window, s)[8:8 + tm], axis=0, keepdims=True)
        dw_ref[...] += jnp.concatenate(dws, axis=0)
        dproj_ref[:, :QKV_WIDTH] = dx[8:8 + tm].astype(dproj_ref.dtype)
        dproj_ref[:, COL_Z:COL_BA] = dzp_ref[...]

        ba = ba_ref[...]
        dbg_ = dbg_ref[...]
        lane = lax.broadcasted_iota(jnp.int32, ba.shape, 1)
        beta = _sigmoid(ba)
        pre = ba + dtb_ref[...]
        neg_a = -jnp.exp(al_ref[...])
        g = neg_a * _softplus(pre)
        is_g = (lane >= GDN_HEADS) & (lane < 2 * GDN_HEADS)
        da_raw = jnp.where(is_g, dbg_ * neg_a * _sigmoid(pre), 0.0)
        dba = jnp.where(lane < GDN_HEADS, dbg_ * beta * (1.0 - beta), da_raw)
        dproj_ref[:, COL_BA:] = dba.astype(dproj_ref.dtype)
        dal_ref[...] += jnp.sum(jnp.where(is_g, dbg_ * g, 0.0), axis=0, keepdims=True)
        ddtb_ref[...] += jnp.sum(da_raw, axis=0, keepdims=True)

    lane_vec = pl.BlockSpec((1, 128), lambda i: (0, 0))
    return pl.pallas_call(
        body, name=name, grid=(n_tiles,),
        in_specs=[pl.BlockSpec((tm, QKV_WIDTH), lambda i: (i, 0)),
                  pl.BlockSpec((8, QKV_WIDTH), lambda i: (jnp.maximum(i * hb - 1, 0), 0)),
                  pl.BlockSpec((8, QKV_WIDTH), lambda i: (jnp.minimum((i + 1) * hb, last_halo), 0)),
                  pl.BlockSpec((tm, 128), lambda i: (i, COL_BA // 128)),
                  pl.BlockSpec((CONV_K, QKV_WIDTH), lambda i: (0, 0)), lane_vec, lane_vec,
                  pl.BlockSpec((tm, QKV_WIDTH), lambda i: (i, 0)),
                  pl.BlockSpec((8, QKV_WIDTH), lambda i: (jnp.minimum((i + 1) * hb, last_halo), 0)),
                  pl.BlockSpec((tm, 128), lambda i: (i, 0)),
                  pl.BlockSpec((tm, GDN_WIDTH + POOL_WIDTH), lambda i: (i, 0))],
        out_specs=[pl.BlockSpec((tm, D_IN_PAD), lambda i: (i, 0)),
                   pl.BlockSpec((CONV_K, QKV_WIDTH), lambda i: (0, 0)), lane_vec, lane_vec],
        out_shape=[jax.ShapeDtypeStruct((T, D_IN_PAD), MXU_DTYPE),
                   jax.ShapeDtypeStruct((CONV_K, QKV_WIDTH), F32),
                   jax.ShapeDtypeStruct((1, 128), F32), jax.ShapeDtypeStruct((1, 128), F32)],
        compiler_params=_params("arbitrary"),
    )(proj, proj, proj, proj, conv_w, a_log_l, dt_bias_l, dqkv, dqkv, dbg, dzp)


def _mod_part(c_all, w_ada, b_part, name):
    def body(c_ref, w_ref, b_ref, out_ref):
        cc = c_ref[...]
        out_ref[...] = _mdot(cc * _sigmoid(cc), w_ref[...], NN) + b_ref[...]

    return pl.pallas_call(
        body, name=name, out_shape=jax.ShapeDtypeStruct((c_all.shape[0], w_ada.shape[1]), F32),
        compiler_params=_params(),
    )(c_all, w_ada, b_part)


def _w_ada_grad(c_all, dmod_part, name):
    def body(c_ref, d_ref, out_ref):
        cc = c_ref[...]
        out_ref[...] = _mdot(cc * _sigmoid(cc), d_ref[...], TN)

    return pl.pallas_call(
        body, name=name, out_shape=jax.ShapeDtypeStruct((c_all.shape[1], dmod_part.shape[1]), F32),
        compiler_params=_params(),
    )(c_all, dmod_part)


def _sum_parts(parts, name):
    _, R, C = parts.shape
    tr = max([t for t in range(16, min(R, 512) + 1, 16) if R % t == 0], default=R)

    def body(p_ref, out_ref):
        acc = p_ref[0].astype(F32)
        for s in range(1, N_DEV):
            acc += p_ref[s].astype(F32)
        out_ref[...] = acc

    return pl.pallas_call(
        body, name=name, grid=(R // tr,),
        in_specs=[pl.BlockSpec((N_DEV, tr, C), lambda i: (0, i, 0))],
        out_specs=pl.BlockSpec((tr, C), lambda i: (i, 0)),
        out_shape=jax.ShapeDtypeStruct((R, C), F32),
        compiler_params=_params("parallel"),
    )(parts)


def _adamw_math(w, g, m, v):
    mm = ADAM_B1 * m + (1.0 - ADAM_B1) * g
    vv = ADAM_B2 * v + (1.0 - ADAM_B2) * (g * g)
    m_hat = mm / (1.0 - ADAM_B1 ** ADAM_STEP)
    v_hat = vv / (1.0 - ADAM_B2 ** ADAM_STEP)
    return -ADAM_LR * (m_hat / (jnp.sqrt(v_hat) + ADAM_EPS) + ADAM_WD * w), mm, vv


def _adamw_small(ws, gs, ms, vs, name):
    n = len(ws)

    def body(*refs):
        for i in range(n):
            d, mm, vv = _adamw_math(*[refs[k * n + i][...] for k in range(4)])
            refs[4 * n + i][...] = d
            refs[5 * n + i][...] = mm
            refs[6 * n + i][...] = vv

    out = pl.pallas_call(
        body, name=name, out_shape=[jax.ShapeDtypeStruct(w.shape, F32) for w in ws] * 3,
        compiler_params=_params(),
    )(*ws, *gs, *ms, *vs)
    return out[:n], out[n:2 * n], out[2 * n:]


def _adamw_sum(parts, w, m, v, name):
    R, C = w.shape
    tr = max([t for t in range(16, min(R, 512) + 1, 16) if R % t == 0], default=R)

    def body(p_ref, w_ref, m_ref, v_ref, g_ref, d_ref, mo_ref, vo_ref):
        g = p_ref[0].astype(F32)
        for s in range(1, N_DEV):
            g += p_ref[s].astype(F32)
        g_ref[...] = g
        d_ref[...], mo_ref[...], vo_ref[...] = _adamw_math(w_ref[...], g, m_ref[...], v_ref[...])

    spec = pl.BlockSpec((tr, C), lambda i: (i, 0))
    return pl.pallas_call(
        body, name=name, grid=(R // tr,),
        in_specs=[pl.BlockSpec((N_DEV, tr, C), lambda i: (0, i, 0)), spec, spec, spec], out_specs=[spec] * 4,
        out_shape=[jax.ShapeDtypeStruct((R, C), F32)] * 4,
        compiler_params=_params("parallel"),
    )(parts, w, m, v)


def _adamw(w, g, m, v, name):
    R, C = w.shape
    tr = max([t for t in range(8, min(R, 512) + 1, 8) if R % t == 0], default=R)

    def body(w_ref, g_ref, m_ref, v_ref, d_ref, mo_ref, vo_ref):
        d_ref[...], mo_ref[...], vo_ref[...] = _adamw_math(w_ref[...], g_ref[...], m_ref[...], v_ref[...])

    spec = pl.BlockSpec((tr, C), lambda i: (i, 0))
    return pl.pallas_call(
        body, name=name, grid=(R // tr,),
        in_specs=[spec] * 4, out_specs=[spec] * 3,
        out_shape=[jax.ShapeDtypeStruct((R, C), F32)] * 3,
        compiler_params=_params("parallel"),
    )(w, g, m, v)


def _weight_grad(a, b, name, dep=None):
    return _matmul([(a, b)], TN, WIRE_DTYPE, name, tm=1408, tn=1024, tk=2048, dep=dep)


def _rows_of(flat, lanes=1024):
    flat = flat.reshape(-1)
    n = -(-flat.shape[0] // lanes) * lanes
    return jnp.pad(flat, (0, n - flat.shape[0])).reshape(n // lanes, lanes)


def _pad_rows(a, rows):
    return jnp.pad(a, ((0, rows - a.shape[0]), (0, 0)))


def kernel(x, c, w_ada, b_ada, norm_ffn1, ffn1_gate, ffn1_up, ffn1_down, norm_mix, w_in, conv_w, a_log, dt_bias, gdn_norm, pool_w, pool_scale, w_out, norm_ffn2, ffn2_gate, ffn2_up, ffn2_down, final_norm, loss_target, m_w_ada, m_b_ada, m_norm_ffn1, m_ffn1_gate, m_ffn1_up, m_ffn1_down, m_norm_mix, m_w_in, m_conv_w, m_a_log, m_dt_bias, m_gdn_norm, m_pool_w, m_pool_scale, m_w_out, m_norm_ffn2, m_ffn2_gate, m_ffn2_up, m_ffn2_down, m_final_norm, v_w_ada, v_b_ada, v_norm_ffn1, v_ffn1_gate, v_ffn1_up, v_ffn1_down, v_norm_mix, v_w_in, v_conv_w, v_a_log, v_dt_bias, v_gdn_norm, v_pool_w, v_pool_scale, v_w_out, v_norm_ffn2, v_ffn2_gate, v_ffn2_up, v_ffn2_down, v_final_norm):
    T, D = x.shape[1], x.shape[2]
    Fs = ffn1_gate.shape[2]
    Ws = w_in.shape[2]
    Ws_pad = -(-Ws // 16) * 16
    Os = w_out.shape[1]
    Ms = w_ada.shape[2]
    Cs = conv_w.shape[2]
    me = 4 * lax.axis_index("x") + 2 * lax.axis_index("y") + lax.axis_index("c")
    x0, target = x[0], loss_target[0]

    def wire(a):
        return a.astype(WIRE_DTYPE)

    def token(started):
        return started[4][:1, :1]

    def with_own(landed, own):
        return lax.dynamic_update_slice(landed, own[None], (me, 0, 0))

    def full(landed):
        return landed.reshape(-1, D).astype(MXU_DTYPE)

    no_dep = jnp.zeros((8, 128), F32)
    small = jnp.concatenate([_pad_rows(c, 8), _pad_rows(jnp.pad(conv_w[0], ((0, 0), (0, D - Cs))), 8)], axis=0)
    got, = _all_gather([small], "gather_small")
    c_all = got[:, 0, :]
    conv_full = jnp.transpose(got[:, 8:8 + CONV_K, :Cs], (1, 0, 2)).reshape(CONV_K, QKV_WIDTH)
    b_part = lax.dynamic_slice(b_ada, (0, me * Ms), (1, Ms))
    mod_part = _mod_part(c_all, w_ada[0], b_part, "mod_part")

    w1 = [wire(ffn1_gate[0].T)]
    w1u = [wire(ffn1_up[0].T)]
    w1d = [wire(ffn1_down[0])]
    w2 = [wire(_pad_rows(w_in[0].T, Ws_pad)), wire(w_out[0])]
    w3 = [wire(ffn2_gate[0].T), wire(ffn2_up[0].T), wire(ffn2_down[0])]
    mod_parts, *w1_all = _all_gather([mod_part] + w1, "gather_mod_w1")
    mod_all = jnp.transpose(mod_parts, (1, 0, 2)).reshape(N_DEV, N_MOD * D)
    mod = lax.dynamic_slice(mod_all, (me, 0), (1, N_MOD * D)).reshape(N_MOD, 1, D)
    sh1, sc1, gt1, sh2, sc2, gt2, sh3, sc3, gt3 = [mod[i] for i in range(N_MOD)]
    wg1_t = full(w1_all[0])
    w1u_sent = _exchange_start(w1u, True, w1_all[0], "w1u_start")
    w1d_sent = _exchange_start(w1d, True, w1u_sent[4], "w1d_start")
    w2_sent = _exchange_start(w2, True, w1d_sent[4], "w2_start")
    w3_sent = _exchange_start(w3, True, w2_sent[4], "w3_start")

    lane_pad = lambda a: jnp.pad(a, ((0, 0), (GDN_HEADS, 128 - 2 * GDN_HEADS)))
    a_log_l, dt_bias_l = lane_pad(a_log), lane_pad(dt_bias)
    pool_w_m = pool_w[0].astype(MXU_DTYPE)

    g1, h1 = _swiglu_gate(x0, wg1_t, norm_ffn1, sc1 + token(w3_sent), sh1, "ffn1_gate")
    wu1_t = full(with_own(_exchange_wait(w1u_sent, h1, True, "w1u_wait")[0], w1u[0]))
    u1, a1 = _swiglu_up(h1, wu1_t, g1, "ffn1_up")
    wd1 = full(with_own(_exchange_wait(w1d_sent, a1, True, "w1d_wait")[0], w1d[0]))
    y1, x1, h2 = _matmul_resid_norm_mod(a1, wd1, x0, gt1, 0.5, norm_mix, sc2, sh2, "ffn1_down")
    w_in_all, wo_all = [with_own(z, own) for z, own in zip(_exchange_wait(w2_sent, h2, True, "w2_wait"), w2)]
    w_in_t = w_in_all[:, :Ws, :].reshape(-1, D).astype(MXU_DTYPE)
    wo = full(wo_all)
    w_in_re = jnp.concatenate([w_in_t[:COL_Z + GDN_WIDTH], w_in_t[D_IN - POOL_WIDTH:],
                               w_in_t[4 * GDN_WIDTH:4 * GDN_WIDTH + 2 * GDN_HEADS],
                               jnp.zeros((128 - 2 * GDN_HEADS, D), MXU_DTYPE)], axis=0)
    proj = _matmul([(h2, w_in_re)], NT, F32, "proj_in", tm=512, tn=D_IN_PAD, tk=D)
    qkv, bg = _gdn_prep(proj, conv_full, a_log_l, dt_bias_l, "gdn_prep")
    tinv, u_c, w_c, qd_c, kd_c, p_c, cd_c = _gdn_chunk_fwd(qkv, bg, "gdn_chunk_fwd")
    o, s_all, vn_c = _gdn_scan_fwd(u_c, w_c, qd_c, kd_c, p_c, cd_c, "gdn_scan_fwd")
    mix_in = _mix_post(o, proj, gdn_norm, pool_w_m, pool_scale, "mix_post")
    mixed, x2, h3 = _matmul_resid_norm_mod(mix_in, wo, x1, gt2, 1.0, norm_ffn2, sc3, sh3, "mix_out")
    wg2_t, wu2_t, wd2 = [full(with_own(z, own))
                         for z, own in zip(_exchange_wait(w3_sent, h3, True, "w3_wait"), w3)]
    g3, u3, a3, loss_row, d3, d_final, dy3, dgt3 = _swiglu_fwd_loss(
        h3, wg2_t, wu2_t, wd2, x2, gt3, final_norm.reshape(1, D), target, "ffn2_fwd_loss")

    dg3, du3, d2, d_n3, dsc3, dsh3, dmixed, dgt2 = _swiglu_bwd(
        dy3, wd2, g3, u3, wg2_t, wu2_t, "ffn2_bwd_norm3_bwd", (x2, norm_ffn2, sc3), d3,
        produced_by=(mixed, gt2, 1.0))
    d_wd2 = _weight_grad(a3, dy3, "ffn2_dwd")
    d_wg2 = _weight_grad(dg3, h3, "ffn2_dwg")
    d_wu2 = _weight_grad(du3, h3, "ffn2_dwu")
    dmix_in = _matmul([(dmixed, wo)], NT, F32, "mix_out_bwd", tm=512, tn=GDN_WIDTH + POOL_WIDTH, tk=D)
    d_wo = _matmul([(mix_in, dmixed)], TN, WIRE_DTYPE, "mix_dwo", tm=GDN_WIDTH + POOL_WIDTH, tn=D, tk=2048)
    do, dzp, d_gn, d_pw, d_ps = _mix_post_bwd(dmix_in, o, proj, gdn_norm, pool_w_m, pool_scale, "mix_post_bwd")
    dvn, dw_c, dqd, dkd, dp_c, dcd = _gdn_scan_bwd(do, w_c, qd_c, kd_c, p_c, cd_c, s_all, vn_c, "gdn_scan_bwd")
    dqkv, dbg = _gdn_chunk_bwd(qkv, bg, tinv, u_c, w_c, dvn, dw_c, dqd, dkd, dp_c, dcd, "gdn_chunk_bwd")
    dproj, d_conv, d_al, d_dtb = _gdn_prep_bwd(proj, conv_full, a_log_l, dt_bias_l, dqkv, dbg, dzp, "gdn_prep_bwd")
    d_win_re = _matmul([(dproj, h2)], TN, WIRE_DTYPE, "proj_in_dw", tm=D_IN_PAD, tn=D, tk=1024)
    d_win_t = jnp.concatenate([d_win_re[:COL_Z + GDN_WIDTH], d_win_re[COL_BA:COL_BA + 2 * GDN_HEADS],
                               d_win_re[COL_P:COL_P + POOL_WIDTH]], axis=0)
    d_win_blocks = jnp.pad(d_win_t.reshape(N_DEV, Ws, D), ((0, 0), (0, Ws_pad - Ws), (0, 0)))
    names23 = ["w_in", "w_out", "ffn2_gate", "ffn2_up", "ffn2_down"]
    parts23 = [wire(d_win_blocks), d_wo.reshape(N_DEV, Os, D), d_wg2.reshape(N_DEV, Fs, D),
               d_wu2.reshape(N_DEV, Fs, D), d_wd2.reshape(N_DEV, Fs, D)]
    own23 = [lax.dynamic_index_in_dim(p, me, 0, keepdims=False) for p in parts23]
    g23_sent = _exchange_start(parts23, False, no_dep, "g23_start")
    d1, d_n2, dsc2, dsh2, dy1, dgt1 = _norm_bwd((dproj, w_in_re), x1, norm_mix, sc2 + token(g23_sent), d2,
                                                "proj_in_bwd_norm2_bwd", produced_by=(y1, gt1, 0.5))
    dg1, du1, grad_x, d_n1, dsc1, dsh1 = _swiglu_bwd(
        dy1, wd1, g1, u1, wg1_t, wu1_t, "ffn1_bwd_norm1_bwd", (x0, norm_ffn1, sc1), d1)

    dmod = jnp.concatenate([dsh1, dsc1, dgt1, dsh2, dsc2, dgt2, dsh3, dsc3, dgt3], axis=0)
    small_rows = [dmod.reshape(-1), d_n1[0], d_n2[0], d_n3[0], d_final[0], d_gn[0], d_ps[0],
                  d_al[0, GDN_HEADS:2 * GDN_HEADS], d_dtb[0, GDN_HEADS:2 * GDN_HEADS], loss_row[0, :1],
                  d_conv.reshape(-1), d_pw.reshape(-1)]
    lanes = 1024
    small_rows = [_rows_of(r, lanes) for r in small_rows]
    n_rows = [r.shape[0] for r in small_rows]
    row_off = [sum(n_rows[:i]) for i in range(len(n_rows))]
    total = -(-sum(n_rows) // 8) * 8
    slab = _pad_rows(jnp.concatenate(small_rows, axis=0), total)
    slab_sent = _exchange_start([slab], True, no_dep, "small_grads_start")

    def send_ffn1(a, b, which, dep):
        parts = _weight_grad(a, b, f"ffn1_{which}", dep=dep).reshape(N_DEV, Fs, D)
        own = lax.dynamic_index_in_dim(parts, me, 0, keepdims=False)
        return _exchange_start([parts], False, no_dep, f"g1_{which}_start"), own

    g1_wg, own_wg = send_ffn1(dg1, h1, "dwg", slab_sent[4])
    g1_wu, own_wu = send_ffn1(du1, h1, "dwu", g1_wg[4])
    g1_wd, own_wd = send_ffn1(a1, dy1, "dwd", g1_wu[4])

    slab_all = with_own(_exchange_wait(slab_sent, g1_wg[4], True, "small_grads_wait")[0], slab)
    summed = _sum_parts(slab_all, "sum_small_grads")

    def piece(idx, n):
        return summed[row_off[idx]:row_off[idx] + n_rows[idx]].reshape(-1)[:n]

    g_b_ada = piece(0, N_MOD * D).reshape(1, N_MOD * D)
    g_n1, g_n2, g_n3 = piece(1, D).reshape(1, D), piece(2, D).reshape(1, D), piece(3, D).reshape(1, D)
    g_final = piece(4, D)
    g_gn = piece(5, HEAD_DIM).reshape(1, HEAD_DIM)
    g_ps = piece(6, POOL_WIDTH).reshape(1, POOL_WIDTH)
    g_al = piece(7, GDN_HEADS).reshape(1, GDN_HEADS)
    g_dtb = piece(8, GDN_HEADS).reshape(1, GDN_HEADS)
    loss = piece(9, 1)[0]
    g_conv = lax.dynamic_slice(piece(10, CONV_K * QKV_WIDTH).reshape(1, CONV_K, QKV_WIDTH), (0, 0, me * Cs),
                               (1, CONV_K, Cs))
    g_pw = piece(11, POOL_GROUPS * 128 * 128).reshape(1, POOL_GROUPS, 128, 128)

    dmod_all = slab_all[:, row_off[0]:row_off[0] + n_rows[0], :].reshape(N_DEV, -1)[:, :N_MOD * D]
    g_w_ada = _w_ada_grad(c_all, lax.dynamic_slice(dmod_all, (0, me * Ms), (N_DEV, Ms)), "w_ada_grad")[None]

    landed23 = _exchange_wait(g23_sent, g1_wd[4], False, "g23_wait")
    parts8 = {n: with_own(z, own) for n, z, own in zip(names23, landed23, own23)}
    g_rows = dict(w_in=_sum_parts(parts8.pop("w_in"), "sum_w_in")[:Ws])
    column_sharded = ("w_in", "ffn1_gate", "ffn1_up", "ffn2_gate", "ffn2_up")

    names = ["w_ada", "b_ada", "norm_ffn1", "ffn1_gate", "ffn1_up", "ffn1_down", "norm_mix", "w_in", "conv_w",
             "a_log", "dt_bias", "gdn_norm", "pool_w", "pool_scale", "w_out", "norm_ffn2", "ffn2_gate", "ffn2_up",
             "ffn2_down", "final_norm"]
    weights = dict(zip(names, [w_ada, b_ada, norm_ffn1, ffn1_gate, ffn1_up, ffn1_down, norm_mix, w_in, conv_w,
                               a_log, dt_bias, gdn_norm, pool_w, pool_scale, w_out, norm_ffn2, ffn2_gate, ffn2_up,
                               ffn2_down, final_norm]))
    ms = dict(zip(names, [m_w_ada, m_b_ada, m_norm_ffn1, m_ffn1_gate, m_ffn1_up, m_ffn1_down, m_norm_mix, m_w_in,
                          m_conv_w, m_a_log, m_dt_bias, m_gdn_norm, m_pool_w, m_pool_scale, m_w_out, m_norm_ffn2,
                          m_ffn2_gate, m_ffn2_up, m_ffn2_down, m_final_norm]))
    vs = dict(zip(names, [v_w_ada, v_b_ada, v_norm_ffn1, v_ffn1_gate, v_ffn1_up, v_ffn1_down, v_norm_mix, v_w_in,
                          v_conv_w, v_a_log, v_dt_bias, v_gdn_norm, v_pool_w, v_pool_scale, v_w_out, v_norm_ffn2,
                          v_ffn2_gate, v_ffn2_up, v_ffn2_down, v_final_norm]))
    grads = dict(w_ada=g_w_ada, b_ada=g_b_ada, norm_ffn1=g_n1, norm_mix=g_n2, conv_w=g_conv,
                 a_log=g_al, dt_bias=g_dtb, gdn_norm=g_gn, pool_w=g_pw, pool_scale=g_ps,
                 norm_ffn2=g_n3, final_norm=g_final)
    delta, new_m, new_v = {}, {}, {}

    def adamw_big(n):
        if n in column_sharded:
            view, back = (lambda a: a[0].T), (lambda a: a.T[None])
        else:
            view, back = (lambda a: a[0]), (lambda a: a[None])
        if n in parts8:
            g, d_, m_, v_ = _adamw_sum(parts8[n], view(weights[n]), view(ms[n]), view(vs[n]), f"adamw_{n}")
        else:
            g = g_rows[n] if n in g_rows else view(grads[n])
            d_, m_, v_ = _adamw(view(weights[n]), g, view(ms[n]), view(vs[n]), f"adamw_{n}")
        grads[n], delta[n], new_m[n], new_v[n] = back(g), back(d_), back(m_), back(v_)

    early = ["w_ada", "w_in", "w_out", "ffn2_gate", "ffn2_up", "ffn2_down"]
    late = ["ffn1_gate", "ffn1_up", "ffn1_down"]
    for n in early:
        adamw_big(n)
    done = sum(delta[n][0, :1, :1] for n in early)

    def arrived(started, own, which):
        landed, = _exchange_wait(started, done, False, f"g1_{which}_wait")
        return with_own(landed, own)

    parts8["ffn1_gate"] = arrived(g1_wg, own_wg, "dwg")
    parts8["ffn1_up"] = arrived(g1_wu, own_wu, "dwu")
    parts8["ffn1_down"] = arrived(g1_wd, own_wd, "dwd")
    for n in late:
        adamw_big(n)
    small_names = [n for n in names if n not in early + late]
    two_d = lambda a: a.reshape(-1, a.shape[-1])
    small_out = _adamw_small(*[[two_d(src[n]) for n in small_names] for src in (weights, grads, ms, vs)],
                             "adamw_small")
    for dst, outs in zip((delta, new_m, new_v), small_out):
        for n, a in zip(small_names, outs):
            dst[n] = a.reshape(weights[n].shape)

    return (loss, grad_x[None], *[grads[n] for n in names], *[delta[n] for n in names],
            *[new_m[n] for n in names], *[new_v[n] for n in names])
```

```python
import functools

import jax
import jax.numpy as jnp
from jax import lax
from jax.experimental import pallas as pl
from jax.experimental.pallas import tpu as pltpu

F32 = jnp.float32
MXU_DTYPE = jnp.bfloat16
WIRE_DTYPE = jnp.bfloat16
EPS = 1e-6
N_DEV = 8
GDN_HEADS = 4
HEAD_DIM = 128
GDN_WIDTH = GDN_HEADS * HEAD_DIM
POOL_WINDOWS = (2, 4, 8, 16)
POOL_GROUPS = len(POOL_WINDOWS)
POOL_WIDTH = 512
CONV_K = 4
CHUNK = 64
QKV_WIDTH = 3 * GDN_WIDTH
D_IN = 4 * GDN_WIDTH + 2 * GDN_HEADS + POOL_WIDTH
D_IN_PAD = 4 * GDN_WIDTH + POOL_WIDTH + 128
COL_Z = QKV_WIDTH
COL_P = 4 * GDN_WIDTH
COL_BA = 4 * GDN_WIDTH + POOL_WIDTH
N_MOD = 9
HALO = 16
VMEM_LIMIT = 56 * 1024 * 1024
ADAM_LR, ADAM_B1, ADAM_B2, ADAM_EPS, ADAM_WD, ADAM_STEP = 0.001, 0.9, 0.999, 1e-08, 0.01, 10
FFN_TOKEN_TILE = 256
FFN_HIDDEN_TILE = 1408
CHUNKS_PER_STEP = 4
SCAN_CHUNKS_PER_STEP = 8

NT = (((1,), (1,)), ((), ()))
NN = (((1,), (0,)), ((), ()))
TN = (((0,), (0,)), ((), ()))


def _params(*sem):
    return pltpu.CompilerParams(dimension_semantics=tuple(sem), vmem_limit_bytes=VMEM_LIMIT)


def _dot(a, b, dims):
    return lax.dot_general(a, b, dims, preferred_element_type=F32)


def _mdot(a, b, dims):
    return _dot(a.astype(MXU_DTYPE), b.astype(MXU_DTYPE), dims)


def _split(a):
    hi = a.astype(jnp.bfloat16)
    return hi, (a - hi.astype(F32)).astype(jnp.bfloat16)


def _dot3(a, b, dims):
    (ah, al), (bh, bl) = a, b
    return (_dot(al, bh, dims) + _dot(ah, bl, dims)) + _dot(ah, bh, dims)


def _sigmoid(v):
    return 0.5 * jnp.tanh(0.5 * v) + 0.5


def _softplus(v):
    return jnp.maximum(v, 0.0) + jnp.log(1.0 + jnp.exp(-jnp.abs(v)))


def _shift_rows(v, s):
    n = v.shape[0]
    s = s % n
    return v if s == 0 else pltpu.roll(v, s, 0)


def _tile(n, want):
    t = min(n, want)
    while n % t:
        t //= 2
    return t


def _all_gather(blocks, name, dep=None):
    n = len(blocks)

    def body(*refs):
        x_refs, out_refs = refs[:n], refs[-3 - n:-3]
        send_sems, recv_sems, local_sems = refs[-3:]
        x, y, c = lax.axis_index("x"), lax.axis_index("y"), lax.axis_index("c")
        me, sibling = (x, y, c), (x, y, 1 - c)
        chips = [(1 - x, y), (x, 1 - y), (1 - x, 1 - y)]

        def copy(a, k, blk, to, own=False):
            rows = out_refs[a].at[4 * blk[0] + 2 * blk[1] + blk[2]]
            return pltpu.make_async_remote_copy(
                src_ref=x_refs[a] if own else rows, dst_ref=rows,
                send_sem=send_sems.at[7 * a + k], recv_sem=recv_sems.at[7 * a + k],
                device_id=to, device_id_type=pl.DeviceIdType.MESH)

        mine = [pltpu.make_async_copy(x_refs[a], out_refs[a].at[4 * x + 2 * y + c], local_sems.at[a])
                for a in range(n)]
        for cp in mine:
            cp.start()
        sent = []
        for a in range(n):
            sent.append(copy(a, 0, me, sibling, own=True))
            sent += [copy(a, 1 + j, me, (*chip, c), own=True) for j, chip in enumerate(chips)]
        for cp in sent:
            cp.start()
        for a in range(n):
            for j, chip in enumerate(chips):
                copy(a, 1 + j, (*chip, c), me).wait_recv()
                passed = copy(a, 4 + j, (*chip, c), sibling)
                passed.start()
                sent.append(passed)
        for a in range(n):
            copy(a, 0, sibling, me).wait_recv()
            for j, chip in enumerate(chips):
                copy(a, 4 + j, (*chip, 1 - c), me).wait_recv()
        for cp in sent:
            cp.wait_send()
        for cp in mine:
            cp.wait()

    hbm = pl.BlockSpec(memory_space=pltpu.HBM)
    return pl.pallas_call(
        body, name=name,
        out_shape=[jax.ShapeDtypeStruct((N_DEV,) + b.shape, b.dtype) for b in blocks],
        in_specs=[hbm] * n + [pl.BlockSpec(memory_space=pl.ANY)] * (dep is not None),
        out_specs=[hbm] * n,
        scratch_shapes=[pltpu.SemaphoreType.DMA((7 * n,)), pltpu.SemaphoreType.DMA((7 * n,)),
                        pltpu.SemaphoreType.DMA((n,))],
    )(*(list(blocks) + ([] if dep is None else [dep])))


_HBM = pl.BlockSpec(memory_space=pltpu.HBM)
_SEM = pl.BlockSpec(memory_space=pltpu.SEMAPHORE)
_ANY = pl.BlockSpec(memory_space=pl.ANY)
_EFFECT = pltpu.SideEffectType.DATAFLOW_SIDE_EFFECTING
_FLIPS = [(0, 0, 1), (0, 1, 0), (0, 1, 1), (1, 0, 0), (1, 0, 1), (1, 1, 0), (1, 1, 1)]


def _peers():
    x, y, c = lax.axis_index("x"), lax.axis_index("y"), lax.axis_index("c")
    return 4 * x + 2 * y + c, [(1 - x if fx else x, 1 - y if fy else y, 1 - c if fc else c)
                               for fx, fy, fc in _FLIPS]


def _exchange_start(srcs, gather, dep, name):
    n = len(srcs)
    lands = [(N_DEV,) + tuple(s.shape if gather else s.shape[1:]) for s in srcs]

    def body(*refs):
        src_refs, land_refs = refs[:n], refs[n:2 * n]
        send_sems, recv_sems = refs[2 * n + 1], refs[2 * n + 2]
        token = refs[-1]
        me, peers = _peers()
        for a in range(n):
            for k, (px, py, pc) in enumerate(peers):
                pltpu.make_async_remote_copy(
                    src_ref=src_refs[a] if gather else src_refs[a].at[4 * px + 2 * py + pc],
                    dst_ref=land_refs[a].at[me],
                    send_sem=send_sems.at[7 * a + k], recv_sem=recv_sems.at[7 * a + k],
                    device_id=(px, py, pc), device_id_type=pl.DeviceIdType.MESH).start()
        token[...] = jnp.zeros_like(token)

    srcs = [pltpu.with_memory_space_constraint(s, pltpu.HBM) for s in srcs]
    empties = [pltpu.with_memory_space_constraint(lax.empty(shape, s.dtype), pltpu.HBM)
               for shape, s in zip(lands, srcs)]
    out = pl.pallas_call(
        body, name=name,
        out_shape=(pltpu.SemaphoreType.DMA((7 * n,)), pltpu.SemaphoreType.DMA((7 * n,)),
                   *[pltpu.HBM(shape, s.dtype) for shape, s in zip(lands, srcs)],
                   jax.ShapeDtypeStruct((8, 128), F32)),
        in_specs=(*[_HBM] * (2 * n), _ANY),
        out_specs=(_SEM, _SEM, *[_HBM] * n, pl.BlockSpec(memory_space=pltpu.VMEM)),
        input_output_aliases={n + a: 2 + a for a in range(n)},
        compiler_params=pltpu.CompilerParams(has_side_effects=_EFFECT),
    )(*srcs, *empties, dep)
    return out[0], out[1], srcs, list(out[2:2 + n]), out[-1]


def _exchange_wait(started, after, gather, name):
    send_sems, recv_sems, srcs, lands, _ = started
    n = len(srcs)

    def body(*refs):
        src_refs, land_refs = refs[:n], refs[n:2 * n]
        send_sems, recv_sems = refs[2 * n], refs[2 * n + 1]
        _, peers = _peers()
        for a in range(n):
            for k, peer in enumerate(peers):
                copy = pltpu.make_async_remote_copy(
                    src_ref=src_refs[a] if gather else src_refs[a].at[0], dst_ref=land_refs[a].at[0],
                    send_sem=send_sems.at[7 * a + k], recv_sem=recv_sems.at[7 * a + k],
                    device_id=peer, device_id_type=pl.DeviceIdType.MESH)
                copy.wait_send()
                copy.wait_recv()

    out = pl.pallas_call(
        body, name=name,
        out_shape=[pltpu.HBM(z.shape, z.dtype) for z in lands],
        in_specs=(*[_HBM] * (2 * n), _SEM, _SEM, _ANY), out_specs=[_HBM] * n,
        input_output_aliases={n + a: a for a in range(n)},
        compiler_params=pltpu.CompilerParams(has_side_effects=_EFFECT),
    )(*srcs, *lands, send_sems, recv_sems, after)
    return list(out)


def _matmul(pairs, dims, out_dtype, name, tm=512, tn=512, tk=512, dep=None):
    a0, b0 = pairs[0]
    if dims == TN:
        K, M = a0.shape
    else:
        M, K = a0.shape
    N = b0.shape[0] if dims == NT else b0.shape[1]
    tm, tn, tk = _tile(M, tm), _tile(N, tn), _tile(K, tk)
    nk = K // tk
    n_pairs = len(pairs)
    n_in = 2 * n_pairs + (dep is not None)

    def body(*refs):
        out_ref = refs[n_in]

        def product():
            total = _dot(refs[0][...], refs[1][...], dims)
            for p in range(1, n_pairs):
                total += _dot(refs[2 * p][...], refs[2 * p + 1][...], dims)
            return total

        if nk == 1:
            out_ref[...] = product().astype(out_ref.dtype)
            return
        acc_ref = refs[n_in + 1]
        k = pl.program_id(2)

        @pl.when(k == 0)
        def _():
            acc_ref[...] = product()

        @pl.when((k > 0) & (k < nk - 1))
        def _():
            acc_ref[...] += product()

        @pl.when(k == nk - 1)
        def _():
            out_ref[...] = (acc_ref[...] + product()).astype(out_ref.dtype)

    if dims == TN:
        a_spec = pl.BlockSpec((tk, tm), lambda i, j, k: (k, i))
    else:
        a_spec = pl.BlockSpec((tm, tk), lambda i, j, k: (i, k))
    if dims == NT:
        b_spec = pl.BlockSpec((tn, tk), lambda i, j, k: (j, k))
    else:
        b_spec = pl.BlockSpec((tk, tn), lambda i, j, k: (k, j))
    args, specs = [], []
    for a, b in pairs:
        args += [a, b]
        specs += [a_spec, b_spec]
    if dep is not None:
        args.append(dep)
        specs.append(_ANY)
    return pl.pallas_call(
        body, name=name, grid=(M // tm, N // tn, nk),
        in_specs=specs, out_specs=pl.BlockSpec((tm, tn), lambda i, j, k: (i, j)),
        out_shape=jax.ShapeDtypeStruct((M, N), out_dtype),
        scratch_shapes=[pltpu.VMEM((tm, tn), F32)] * (nk > 1),
        compiler_params=_params("parallel", "parallel", "arbitrary"),
    )(*args)


def _vec_spec(d):
    return pl.BlockSpec((1, d), lambda i: (0, 0))


def _matmul_resid_norm_mod(a, b, x, gate, coef, nw, scale, shift, name):
    T, K = a.shape
    D = b.shape[1]
    tm = _tile(T, 512)

    def body(a_ref, b_ref, x_ref, g_ref, nw_ref, sc_ref, sh_ref, y_ref, xo_ref, h_ref):
        y = _dot(a_ref[...], b_ref[...], NN)
        y_ref[...] = y
        xf = x_ref[...] + (coef * g_ref[...]) * y
        xo_ref[...] = xf
        r = lax.rsqrt(jnp.mean(xf * xf, axis=-1, keepdims=True) + EPS)
        h_ref[...] = ((xf * r) * nw_ref[...] * (1.0 + sc_ref[...]) + sh_ref[...]).astype(h_ref.dtype)

    row = pl.BlockSpec((tm, D), lambda i: (i, 0))
    vec = _vec_spec(D)
    return pl.pallas_call(
        body, name=name, grid=(T // tm,),
        in_specs=[pl.BlockSpec((tm, K), lambda i: (i, 0)), _resident((K, D)), row, vec, vec, vec, vec],
        out_specs=[row, row, row],
        out_shape=[jax.ShapeDtypeStruct((T, D), F32), jax.ShapeDtypeStruct((T, D), F32),
                   jax.ShapeDtypeStruct((T, D), MXU_DTYPE)],
        compiler_params=_params("parallel"),
    )(a, b, x, gate, nw, scale, shift)


def _norm_bwd(dh, x, nw, scale, dres, name, produced_by=None):
    T, D = x.shape
    tm = _tile(T, 512)

    def body(*refs):
        if isinstance(dh, tuple):
            dh_value = _dot(refs[0][...], refs[1][...], NN)
            refs = refs[1:]
        else:
            dh_value = refs[0][...]
        _, x_ref, nw_ref, sc_ref, dr_ref = refs[:5]
        n_in = 5 if produced_by is None else 7
        dx_ref, dnw_ref, dsc_ref, dsh_ref = refs[n_in:n_in + 4]

        @pl.when(pl.program_id(0) == 0)
        def _():
            dnw_ref[...] = jnp.zeros_like(dnw_ref)
            dsc_ref[...] = jnp.zeros_like(dsc_ref)
            dsh_ref[...] = jnp.zeros_like(dsh_ref)
            if produced_by is not None:
                refs[n_in + 5][...] = jnp.zeros_like(refs[n_in + 5])

        xf, dh_ = x_ref[...], dh_value
        r = lax.rsqrt(jnp.mean(xf * xf, axis=-1, keepdims=True) + EPS)
        xn = xf * r
        one_sc = 1.0 + sc_ref[...]
        dsh_ref[...] += jnp.sum(dh_, axis=0, keepdims=True)
        t = dh_ * xn
        dsc_ref[...] += jnp.sum(t, axis=0, keepdims=True) * nw_ref[...]
        dnw_ref[...] += jnp.sum(t, axis=0, keepdims=True) * one_sc
        dxn = dh_ * (nw_ref[...] * one_sc)
        dx = dr_ref[...] + r * (dxn - xn * jnp.mean(dxn * xn, axis=-1, keepdims=True))
        dx_ref[...] = dx
        if produced_by is not None:
            y_ref, g_ref, dy_ref, dg_ref = refs[5], refs[6], refs[n_in + 4], refs[n_in + 5]
            dy_ref[...] = ((produced_by[2] * g_ref[...]) * dx).astype(dy_ref.dtype)
            dg_ref[...] += produced_by[2] * jnp.sum(dx * y_ref[...], axis=0, keepdims=True)

    row = pl.BlockSpec((tm, D), lambda i: (i, 0))
    vec = _vec_spec(D)
    vec_out = jax.ShapeDtypeStruct((1, D), F32)
    if isinstance(dh, tuple):
        k_dim = dh[0].shape[1]
        args, in_specs = [dh[0], dh[1]], [pl.BlockSpec((tm, k_dim), lambda i: (i, 0)), _resident((k_dim, D))]
    else:
        args, in_specs = [dh], [row]
    args, in_specs = args + [x, nw, scale, dres], in_specs + [row, vec, vec, row]
    out_specs, out_shape = [row, vec, vec, vec], [jax.ShapeDtypeStruct((T, D), F32), vec_out, vec_out, vec_out]
    if produced_by is not None:
        args += [produced_by[0], produced_by[1]]
        in_specs += [row, vec]
        out_specs += [row, vec]
        out_shape += [jax.ShapeDtypeStruct((T, D), MXU_DTYPE), vec_out]
    return pl.pallas_call(
        body, name=name, grid=(T // tm,),
        in_specs=in_specs, out_specs=out_specs, out_shape=out_shape,
        compiler_params=_params("arbitrary"),
    )(*args)


def _rms(xf):
    r = lax.rsqrt(jnp.mean(xf * xf, axis=-1, keepdims=True) + EPS)
    return r, xf * r


def _norm_bwd_math(dh, xf, nw, sc):
    r, xn = _rms(xf)
    t = dh * xn
    dxn = dh * (nw * (1.0 + sc))
    dx = r * (dxn - xn * jnp.mean(dxn * xn, axis=-1, keepdims=True))
    return dx, jnp.sum(dh, axis=0, keepdims=True), jnp.sum(t, axis=0, keepdims=True)


def _swiglu_gate(x, wg_t, nw, scale, shift, name):
    T, D = x.shape
    Fdim = wg_t.shape[0]
    tm = _tile(T, 1024)

    def body(x_ref, wg_ref, nw_ref, sc_ref, sh_ref, g_ref, h_ref):
        _, xn = _rms(x_ref[...])
        hh = (xn * nw_ref[...] * (1.0 + sc_ref[...]) + sh_ref[...]).astype(MXU_DTYPE)
        h_ref[...] = hh
        g_ref[...] = _dot(hh, wg_ref[...], NT).astype(g_ref.dtype)

    row = pl.BlockSpec((tm, D), lambda i: (i, 0))
    vec = _vec_spec(D)
    return pl.pallas_call(
        body, name=name, grid=(T // tm,),
        in_specs=[row, _resident((Fdim, D)), vec, vec, vec],
        out_specs=[pl.BlockSpec((tm, Fdim), lambda i: (i, 0)), row],
        out_shape=[jax.ShapeDtypeStruct((T, Fdim), MXU_DTYPE), jax.ShapeDtypeStruct((T, D), MXU_DTYPE)],
        compiler_params=_params("parallel"),
    )(x, wg_t, nw, scale, shift)


def _swiglu_up(h, wu_t, g, name):
    T, D = h.shape
    Fdim = wu_t.shape[0]
    tm = _tile(T, 1024)

    def body(h_ref, wu_ref, g_ref, u_ref, a_ref):
        u = _dot(h_ref[...], wu_ref[...], NT)
        g = g_ref[...].astype(F32)
        u_ref[...] = u.astype(u_ref.dtype)
        a_ref[...] = ((g * _sigmoid(g)) * u).astype(a_ref.dtype)

    frow = pl.BlockSpec((tm, Fdim), lambda i: (i, 0))
    return pl.pallas_call(
        body, name=name, grid=(T // tm,),
        in_specs=[pl.BlockSpec((tm, D), lambda i: (i, 0)), _resident((Fdim, D)), frow],
        out_specs=[frow, frow],
        out_shape=[jax.ShapeDtypeStruct((T, Fdim), MXU_DTYPE)] * 2,
        compiler_params=_params("parallel"),
    )(h, wu_t, g)


def _swiglu_fwd_loss(h, wg_t, wu_t, wd, x, gate, fw, target, name):
    T, D = h.shape
    Fdim = wd.shape[0]
    tm, tf = _tile(T, FFN_TOKEN_TILE), _tile(Fdim, FFN_HIDDEN_TILE)
    row = pl.BlockSpec((tm, D), lambda i: (i, 0))
    frow = pl.BlockSpec((tm, Fdim), lambda i: (i, 0))
    vec, wres = _vec_spec(D), _resident((Fdim, D))
    vec_out = jax.ShapeDtypeStruct((1, D), F32)

    def body(h_ref, wg_ref, wu_ref, wd_ref, x_ref, gt_ref, fw_ref, t_ref,
             g_ref, u_ref, a_ref, loss_ref, dx_ref, dfw_ref, dy_ref, dg_ref):
        hh = h_ref[...]
        y = None
        for k in range(Fdim // tf):
            ks = slice(k * tf, (k + 1) * tf)
            g = _dot(hh, wg_ref[ks, :], NT)
            u = _dot(hh, wu_ref[ks, :], NT)
            a = ((g * _sigmoid(g)) * u).astype(a_ref.dtype)
            g_ref[:, ks] = g.astype(g_ref.dtype)
            u_ref[:, ks] = u.astype(u_ref.dtype)
            a_ref[:, ks] = a
            part = _dot(a, wd_ref[ks, :], NN)
            y = part if y is None else y + part

        @pl.when(pl.program_id(0) == 0)
        def _():
            loss_ref[...] = jnp.zeros_like(loss_ref)
            dfw_ref[...] = jnp.zeros_like(dfw_ref)
            dg_ref[...] = jnp.zeros_like(dg_ref)

        r, xn = _rms(x_ref[...] + (0.5 * gt_ref[...]) * y)
        err = xn * fw_ref[...] - t_ref[...]
        per_tok = jnp.mean(err * err, axis=-1, keepdims=True)
        loss_ref[...] += 0.5 * jnp.sum(per_tok, axis=0, keepdims=True)
        d_out = err * (1.0 / D)
        dfw_ref[...] += jnp.sum(d_out * xn, axis=0, keepdims=True)
        dxn = d_out * fw_ref[...]
        dx = r * (dxn - xn * jnp.mean(dxn * xn, axis=-1, keepdims=True))
        dx_ref[...] = dx
        dy_ref[...] = ((0.5 * gt_ref[...]) * dx).astype(dy_ref.dtype)
        dg_ref[...] += 0.5 * jnp.sum(dx * y, axis=0, keepdims=True)

    return pl.pallas_call(
        body, name=name, grid=(T // tm,),
        in_specs=[row, wres, wres, wres, row, vec, vec, row],
        out_specs=[frow, frow, frow, pl.BlockSpec((1, 128), lambda i: (0, 0)), row, vec, row, vec],
        out_shape=[jax.ShapeDtypeStruct((T, Fdim), MXU_DTYPE)] * 3
        + [jax.ShapeDtypeStruct((1, 128), F32), jax.ShapeDtypeStruct((T, D), F32), vec_out,
           jax.ShapeDtypeStruct((T, D), MXU_DTYPE), vec_out],
        compiler_params=_params("arbitrary"),
    )(h, wg_t, wu_t, wd, x, gate, fw, target)


def _swiglu_bwd(dy, wd, g, u, wg_t, wu_t, name, norm_in, dres, produced_by=None):
    T, D = dy.shape
    Fdim = wd.shape[0]
    tm, tf = _tile(T, FFN_TOKEN_TILE), _tile(Fdim, FFN_HIDDEN_TILE)
    row = pl.BlockSpec((tm, D), lambda i: (i, 0))
    frow = pl.BlockSpec((tm, Fdim), lambda i: (i, 0))
    vec, wres = _vec_spec(D), _resident((Fdim, D))
    vec_out = jax.ShapeDtypeStruct((1, D), F32)
    args, in_specs = [dy, wd, g, u, wg_t, wu_t, *norm_in, dres], [row, wres, frow, frow, wres, wres, row, vec, vec, row]
    out_shape = [jax.ShapeDtypeStruct((T, Fdim), MXU_DTYPE)] * 2 + [jax.ShapeDtypeStruct((T, D), F32)] + [vec_out] * 3
    out_specs = [frow, frow, row, vec, vec, vec]
    if produced_by is not None:
        args += [produced_by[0], produced_by[1]]
        in_specs += [row, vec]
        out_shape += [jax.ShapeDtypeStruct((T, D), MXU_DTYPE), vec_out]
        out_specs += [row, vec]

    def body(*refs):
        it = iter(refs)
        dy_ref, wd_ref, g_ref, u_ref, wg_ref, wu_ref, x_ref, nw_ref, sc_ref, dr_ref = [next(it) for _ in range(10)]
        prev_refs = [next(it), next(it)] if produced_by is not None else None
        dg_ref, du_ref, dx_ref, dnw_ref, dsc_ref, dsh_ref = [next(it) for _ in range(6)]
        prev_out = [next(it), next(it)] if produced_by is not None else None

        @pl.when(pl.program_id(0) == 0)
        def _():
            dnw_ref[...] = jnp.zeros_like(dnw_ref)
            dsc_ref[...] = jnp.zeros_like(dsc_ref)
            dsh_ref[...] = jnp.zeros_like(dsh_ref)
            if produced_by is not None:
                prev_out[1][...] = jnp.zeros_like(prev_out[1])

        dyy = dy_ref[...]
        dh = None
        for k in range(Fdim // tf):
            ks = slice(k * tf, (k + 1) * tf)
            da = _dot(dyy, wd_ref[ks, :], NT)
            gg = g_ref[:, ks].astype(F32)
            sig = _sigmoid(gg)
            dg = (da * u_ref[:, ks].astype(F32) * (sig * (1.0 + gg * (1.0 - sig)))).astype(dg_ref.dtype)
            du = (da * (gg * sig)).astype(du_ref.dtype)
            dg_ref[:, ks] = dg
            du_ref[:, ks] = du
            part = _dot(dg, wg_ref[ks, :], NN) + _dot(du, wu_ref[ks, :], NN)
            dh = part if dh is None else dh + part
        dx_norm, dsh_row, t_row = _norm_bwd_math(dh, x_ref[...], nw_ref[...], sc_ref[...])
        dsh_ref[...] += dsh_row
        dsc_ref[...] += t_row * nw_ref[...]
        dnw_ref[...] += t_row * (1.0 + sc_ref[...])
        dx = dr_ref[...] + dx_norm
        dx_ref[...] = dx
        if produced_by is not None:
            prev_out[0][...] = ((produced_by[2] * prev_refs[1][...]) * dx).astype(prev_out[0].dtype)
            prev_out[1][...] += produced_by[2] * jnp.sum(dx * prev_refs[0][...], axis=0, keepdims=True)

    return pl.pallas_call(
        body, name=name, grid=(T // tm,), in_specs=in_specs, out_specs=out_specs, out_shape=out_shape,
        compiler_params=_params("arbitrary"),
    )(*args)


def _resident(shape):
    return pl.BlockSpec(shape, lambda i: (0,) * len(shape), pipeline_mode=pl.Buffered(1))


def _conv_act(window, w):
    y = window * w[CONV_K - 1:CONV_K, :]
    for j in range(CONV_K - 1):
        y += _shift_rows(window, CONV_K - 1 - j) * w[j:j + 1, :]
    return y


def _gdn_prep(proj, conv_w, a_log_l, dt_bias_l, name):
    T = proj.shape[0]
    tm = _tile(T, 512)
    hb = tm // 8

    def body(cur_ref, halo_ref, ba_ref, w_ref, al_ref, dtb_ref, qkv_ref, bg_ref):
        i = pl.program_id(0)
        halo = jnp.where(i == 0, 0.0, halo_ref[...])
        window = jnp.concatenate([halo, cur_ref[...]], axis=0)
        y = _conv_act(window, w_ref[...])[8:, :]
        act = y * _sigmoid(y)
        for hh in range(3 * GDN_HEADS):
            blk = act[:, hh * HEAD_DIM:(hh + 1) * HEAD_DIM]
            if hh < 2 * GDN_HEADS:
                rn = lax.rsqrt(jnp.sum(blk * blk, axis=-1, keepdims=True) + EPS)
                blk = blk * rn
                if hh < GDN_HEADS:
                    blk = blk * (HEAD_DIM ** -0.5)
            qkv_ref[:, hh * HEAD_DIM:(hh + 1) * HEAD_DIM] = blk
        ba = ba_ref[...]
        lane = lax.broadcasted_iota(jnp.int32, ba.shape, 1)
        beta = _sigmoid(ba)
        g = -jnp.exp(al_ref[...]) * _softplus(ba + dtb_ref[...])
        bg_ref[...] = jnp.where(lane < GDN_HEADS, beta, jnp.where(lane < 2 * GDN_HEADS, g, 0.0))

    return pl.pallas_call(
        body, name=name, grid=(T // tm,),
        in_specs=[pl.BlockSpec((tm, QKV_WIDTH), lambda i: (i, 0)),
                  pl.BlockSpec((8, QKV_WIDTH), lambda i: (jnp.maximum(i * hb - 1, 0), 0)),
                  pl.BlockSpec((tm, 128), lambda i: (i, COL_BA // 128)),
                  pl.BlockSpec((CONV_K, QKV_WIDTH), lambda i: (0, 0)),
                  pl.BlockSpec((1, 128), lambda i: (0, 0)), pl.BlockSpec((1, 128), lambda i: (0, 0))],
        out_specs=[pl.BlockSpec((tm, QKV_WIDTH), lambda i: (i, 0)), pl.BlockSpec((tm, 128), lambda i: (i, 0))],
        out_shape=[jax.ShapeDtypeStruct((T, QKV_WIDTH), F32), jax.ShapeDtypeStruct((T, 128), F32)],
        compiler_params=_params("parallel"),
    )(proj, proj, proj, conv_w, a_log_l, dt_bias_l)


def _chunk_cumsum(v, reverse=False):
    row = lax.broadcasted_iota(jnp.int32, v.shape, 0)
    s = 1
    while s < CHUNK:
        if reverse:
            v = v + jnp.where(row < CHUNK - s, _shift_rows(v, -s), 0.0)
        else:
            v = v + jnp.where(row >= s, _shift_rows(v, s), 0.0)
        s *= 2
    return v


def _row_form(cols):
    padded = jnp.concatenate([cols, jnp.zeros((128 - CHUNK, 128), F32)], axis=0)
    return padded.T[:, :CHUNK]


def _chunk_masks():
    ri = lax.broadcasted_iota(jnp.int32, (CHUNK, CHUNK), 0)
    ci = lax.broadcasted_iota(jnp.int32, (CHUNK, CHUNK), 1)
    return ri >= ci, ri > ci, (ri == ci).astype(F32)


def _unit_lower_inverses(ms, eye):
    rs = [eye - m for m in ms]
    ps = [_split(-m) for m in ms]
    s = 2
    while s < CHUNK:
        ps = [_split(_dot3(p, p, NN)) for p in ps]
        r_parts = [_split(r) for r in rs]
        rs = [r + _dot3(p, rp, NN) for r, p, rp in zip(rs, ps, r_parts)]
        s *= 2
    return rs


def _head_elementwise(k, beta, gc, gcr, causal):
    decay = jnp.where(causal, jnp.exp(jnp.where(causal, gc - gcr, 0.0)), 0.0)
    return decay, k * beta, jnp.exp(gc)


def _head_slices(hh):
    return (slice(hh * HEAD_DIM, (hh + 1) * HEAD_DIM),
            slice(GDN_WIDTH + hh * HEAD_DIM, GDN_WIDTH + (hh + 1) * HEAD_DIM),
            slice(2 * GDN_WIDTH + hh * HEAD_DIM, 2 * GDN_WIDTH + (hh + 1) * HEAD_DIM))


def _gdn_chunk_fwd(qkv, bg, name):
    T = qkv.shape[0]
    cb = _tile(T // CHUNK, CHUNKS_PER_STEP)
    rows = cb * CHUNK

    def body(qkv_ref, bg_ref, tinv_ref, u_ref, w_ref, qd_ref, kd_ref, p_ref, cd_ref):
        masks = _chunk_masks()
        causal, strict, eye = masks
        heads = []
        for ci in range(cb):
            rs = slice(ci * CHUNK, (ci + 1) * CHUNK)
            bgv = bg_ref[rs, :]
            gc_all = _chunk_cumsum(bgv)
            gc_rows = _row_form(gc_all)
            cd_ref[rs, :] = jnp.exp(jnp.broadcast_to(gc_all[CHUNK - 1:CHUNK, :], (CHUNK, 128)))
            for hh in range(GDN_HEADS):
                qs, ks, vs = _head_slices(hh)
                q, k, v = qkv_ref[rs, qs], qkv_ref[rs, ks], qkv_ref[rs, vs]
                beta = bgv[:, hh:hh + 1]
                gc = gc_all[:, GDN_HEADS + hh:GDN_HEADS + hh + 1]
                decay, kb, eg = _head_elementwise(k, beta, gc, gc_rows[GDN_HEADS + hh:GDN_HEADS + hh + 1, :], causal)
                hs = slice(hh * HEAD_DIM, (hh + 1) * HEAD_DIM)
                cs = slice(hh * CHUNK, (hh + 1) * CHUNK)
                qd_ref[rs, hs] = (q * eg).astype(qd_ref.dtype)
                kd_ref[rs, hs] = (k * jnp.exp(gc[CHUNK - 1:CHUNK, :] - gc)).astype(kd_ref.dtype)
                heads.append((rs, hs, cs, q, k, v * beta, kb, kb * eg, decay))
        kks = [_mdot(kb, k, NT) for (_, _, _, _, k, _, kb, _, _) in heads]
        qks = [_mdot(q, k, NT) for (_, _, _, q, k, _, _, _, _) in heads]
        tinvs = _unit_lower_inverses([jnp.where(strict, kk * hd[8], 0.0) for kk, hd in zip(kks, heads)], eye)
        t_parts = [_split(t) for t in tinvs]
        us = [_dot3(tp, _split(hd[5]), NN) for tp, hd in zip(t_parts, heads)]
        ws = [_dot3(tp, _split(hd[7]), NN) for tp, hd in zip(t_parts, heads)]
        for hd, tinv, u, w, qk in zip(heads, tinvs, us, ws, qks):
            rs, hs, cs = hd[0], hd[1], hd[2]
            tinv_ref[rs, cs] = tinv
            u_ref[rs, hs] = u
            w_ref[rs, hs] = w.astype(w_ref.dtype)
            p_ref[rs, cs] = jnp.where(causal, qk * hd[8], 0.0).astype(p_ref.dtype)

    def spec(width):
        return pl.BlockSpec((rows, width), lambda n: (n, 0))

    hw, cw = GDN_WIDTH, GDN_HEADS * CHUNK
    return pl.pallas_call(
        body, name=name, grid=(T // rows,),
        in_specs=[spec(QKV_WIDTH), spec(128)],
        out_specs=[spec(cw), spec(hw), spec(hw), spec(hw), spec(hw), spec(cw), spec(128)],
        out_shape=[jax.ShapeDtypeStruct((T, cw), F32), jax.ShapeDtypeStruct((T, hw), F32),
                   jax.ShapeDtypeStruct((T, hw), MXU_DTYPE), jax.ShapeDtypeStruct((T, hw), MXU_DTYPE),
                   jax.ShapeDtypeStruct((T, hw), MXU_DTYPE), jax.ShapeDtypeStruct((T, cw), MXU_DTYPE),
                   jax.ShapeDtypeStruct((T, 128), F32)],
        compiler_params=_params("parallel"),
    )(qkv, bg)


def _gdn_scan_fwd(u, w, qd, kd, p, cd, name):
    T = u.shape[0]
    cb = _tile(T // CHUNK, SCAN_CHUNKS_PER_STEP)
    rows = cb * CHUNK

    def body(u_ref, w_ref, qd_ref, kd_ref, p_ref, cd_ref, o_ref, s_all_ref, vn_ref, s_ref):
        @pl.when(pl.program_id(0) == 0)
        def _():
            s_ref[...] = jnp.zeros_like(s_ref)

        hss = [slice(hh * HEAD_DIM, (hh + 1) * HEAD_DIM) for hh in range(GDN_HEADS)]
        css = [slice(hh * CHUNK, (hh + 1) * CHUNK) for hh in range(GDN_HEADS)]
        s_cur = [s_ref[hh] for hh in range(GDN_HEADS)]
        for ci in range(cb):
            rs = slice(ci * CHUNK, (ci + 1) * CHUNK)
            for hh in range(GDN_HEADS):
                s_all_ref[ci * GDN_WIDTH + hh * HEAD_DIM:ci * GDN_WIDTH + (hh + 1) * HEAD_DIM, :] = s_cur[hh]
            s_ms = [s.astype(MXU_DTYPE) for s in s_cur]
            w_s = [_dot(w_ref[rs, hs], s_m, NN) for hs, s_m in zip(hss, s_ms)]
            q_s = [_dot(qd_ref[rs, hs], s_m, NN) for hs, s_m in zip(hss, s_ms)]
            v_ms = [(u_ref[rs, hs] - ws_).astype(MXU_DTYPE) for hs, ws_ in zip(hss, w_s)]
            k_v = [_dot(kd_ref[rs, hs], v_m, TN) for hs, v_m in zip(hss, v_ms)]
            p_v = [_dot(p_ref[rs, cs], v_m, NN) for cs, v_m in zip(css, v_ms)]
            for hh in range(GDN_HEADS):
                vn_ref[rs, hss[hh]] = v_ms[hh]
                o_ref[rs, hss[hh]] = q_s[hh] + p_v[hh]
                c_dec = cd_ref[ci * CHUNK:ci * CHUNK + 1, GDN_HEADS + hh:GDN_HEADS + hh + 1]
                s_cur[hh] = s_cur[hh] * c_dec + k_v[hh]
        for hh in range(GDN_HEADS):
            s_ref[hh] = s_cur[hh]

    def spec(width):
        return pl.BlockSpec((rows, width), lambda n: (n, 0))

    hw, cw = GDN_WIDTH, GDN_HEADS * CHUNK
    return pl.pallas_call(
        body, name=name, grid=(T // rows,),
        in_specs=[spec(hw), spec(hw), spec(hw), spec(hw), spec(cw), spec(128)],
        out_specs=[spec(hw), pl.BlockSpec((cb * GDN_WIDTH, HEAD_DIM), lambda n: (n, 0)), spec(hw)],
        out_shape=[jax.ShapeDtypeStruct((T, hw), F32),
                   jax.ShapeDtypeStruct((T // CHUNK * GDN_WIDTH, HEAD_DIM), F32),
                   jax.ShapeDtypeStruct((T, hw), MXU_DTYPE)],
        scratch_shapes=[pltpu.VMEM((GDN_HEADS, HEAD_DIM, HEAD_DIM), F32)],
        compiler_params=_params("arbitrary"),
    )(u, w, qd, kd, p, cd)


def _gdn_scan_bwd(do, w, qd, kd, p, cd, s_all, vn, name):
    T = do.shape[0]
    cb = _tile(T // CHUNK, SCAN_CHUNKS_PER_STEP)
    rows = cb * CHUNK
    n_steps = T // rows

    def body(do_ref, w_ref, qd_ref, kd_ref, p_ref, cd_ref, s_all_ref, vn_ref,
             dvn_ref, dw_ref, dqd_ref, dkd_ref, dp_ref, dcd_ref, ds_ref):
        @pl.when(pl.program_id(0) == 0)
        def _():
            ds_ref[...] = jnp.zeros_like(ds_ref)

        causal, _, _ = _chunk_masks()
        lane = lax.broadcasted_iota(jnp.int32, (CHUNK, 128), 1)
        heads = range(GDN_HEADS)
        hss = [slice(hh * HEAD_DIM, (hh + 1) * HEAD_DIM) for hh in heads]
        css = [slice(hh * CHUNK, (hh + 1) * CHUNK) for hh in heads]
        ds_cur = [ds_ref[hh] for hh in heads]
        for ci in reversed(range(cb)):
            rs = slice(ci * CHUNK, (ci + 1) * CHUNK)
            ds_ms = [d.astype(MXU_DTYPE) for d in ds_cur]
            s_olds = [s_all_ref[ci * GDN_WIDTH + hh * HEAD_DIM:ci * GDN_WIDTH + (hh + 1) * HEAD_DIM, :] for hh in heads]
            s_ms = [s.astype(MXU_DTYPE) for s in s_olds]
            do_ms = [do_ref[rs, hs].astype(MXU_DTYPE) for hs in hss]
            p_do = [_dot(p_ref[rs, cs], do_m, TN) for cs, do_m in zip(css, do_ms)]
            k_ds = [_dot(kd_ref[rs, hs], ds_m, NN) for hs, ds_m in zip(hss, ds_ms)]
            q_do = [_dot(qd_ref[rs, hs], do_m, TN) for hs, do_m in zip(hss, do_ms)]
            dqds = [_dot(do_m, s_m, NT) for do_m, s_m in zip(do_ms, s_ms)]
            dkds = [_dot(vn_ref[rs, hs], ds_m, NT) for hs, ds_m in zip(hss, ds_ms)]
            dps = [_dot(do_m, vn_ref[rs, hs], NT) for hs, do_m in zip(hss, do_ms)]
            dv_news = [a + b for a, b in zip(p_do, k_ds)]
            dvn_ms = [d.astype(MXU_DTYPE) for d in dv_news]
            w_dv = [_dot(w_ref[rs, hs], dvn_m, TN) for hs, dvn_m in zip(hss, dvn_ms)]
            dws = [_dot(dvn_m, s_m, NT) for dvn_m, s_m in zip(dvn_ms, s_ms)]
            dcd_tile = jnp.zeros((CHUNK, 128), F32)
            for hh in heads:
                dvn_ref[rs, hss[hh]] = dv_news[hh]
                dw_ref[rs, hss[hh]] = -dws[hh]
                dqd_ref[rs, hss[hh]] = dqds[hh]
                dkd_ref[rs, hss[hh]] = dkds[hh]
                dp_ref[rs, css[hh]] = jnp.where(causal, dps[hh], 0.0)
                dcd = jnp.sum(jnp.sum(s_olds[hh] * ds_cur[hh], axis=1, keepdims=True), axis=0, keepdims=True)
                dcd_tile = jnp.where(lane == GDN_HEADS + hh, dcd, dcd_tile)
                c_dec = cd_ref[ci * CHUNK:ci * CHUNK + 1, GDN_HEADS + hh:GDN_HEADS + hh + 1]
                ds_cur[hh] = c_dec * ds_cur[hh] + q_do[hh] - w_dv[hh]
            dcd_ref[rs, :] = dcd_tile
        for hh in heads:
            ds_ref[hh] = ds_cur[hh]

    def spec(width):
        return pl.BlockSpec((rows, width), lambda n: (n_steps - 1 - n, 0))

    hw, cw = GDN_WIDTH, GDN_HEADS * CHUNK
    return pl.pallas_call(
        body, name=name, grid=(n_steps,),
        in_specs=[spec(hw), spec(hw), spec(hw), spec(hw), spec(cw), spec(128),
                  pl.BlockSpec((cb * GDN_WIDTH, HEAD_DIM), lambda n: (n_steps - 1 - n, 0)), spec(hw)],
        out_specs=[spec(hw), spec(hw), spec(hw), spec(hw), spec(cw), spec(128)],
        out_shape=[jax.ShapeDtypeStruct((T, hw), F32)] * 4
        + [jax.ShapeDtypeStruct((T, cw), F32), jax.ShapeDtypeStruct((T, 128), F32)],
        scratch_shapes=[pltpu.VMEM((GDN_HEADS, HEAD_DIM, HEAD_DIM), F32)],
        compiler_params=_params("arbitrary"),
    )(do, w, qd, kd, p, cd, s_all, vn)


def _gdn_chunk_bwd(qkv, bg, tinv_all, u, w, dvn, dw, dqd, dkd, dp, dcd, name):
    T = qkv.shape[0]
    cb = _tile(T // CHUNK, CHUNKS_PER_STEP)
    rows = cb * CHUNK

    def body(qkv_ref, bg_ref, tinv_ref, u_ref, w_ref, dvn_ref, dw_ref, dqd_ref, dkd_ref, dp_ref, dcd_ref,
             dqkv_ref, dbg_ref):
        masks = _chunk_masks()
        causal, strict, _ = masks
        lane = lax.broadcasted_iota(jnp.int32, (CHUNK, 128), 1)
        row = lax.broadcasted_iota(jnp.int32, (CHUNK, 128), 0)
        heads = []
        for ci in range(cb):
            rs = slice(ci * CHUNK, (ci + 1) * CHUNK)
            bgv = bg_ref[rs, :]
            gc_all = _chunk_cumsum(bgv)
            gc_rows = _row_form(gc_all)
            for hh in range(GDN_HEADS):
                qs, ks, vs = _head_slices(hh)
                q, k = qkv_ref[rs, qs], qkv_ref[rs, ks]
                beta = bgv[:, hh:hh + 1]
                gc = gc_all[:, GDN_HEADS + hh:GDN_HEADS + hh + 1]
                decay, kb, eg = _head_elementwise(k, beta, gc, gc_rows[GDN_HEADS + hh:GDN_HEADS + hh + 1, :], causal)
                heads.append(dict(ci=ci, hh=hh, rs=rs, hs=slice(hh * HEAD_DIM, (hh + 1) * HEAD_DIM),
                                  cs=slice(hh * CHUNK, (hh + 1) * CHUNK), q=q, k=k, beta=beta, gc=gc,
                                  decay=decay, kb=kb, eg=eg))
        for hd in heads:
            hd["t"] = _split(tinv_ref[hd["rs"], hd["cs"]])
        for hd in heads:
            hd["kk"] = _mdot(hd["kb"], hd["k"], NT)
            hd["qk"] = _mdot(hd["q"], hd["k"], NT)
        for hd in heads:
            hd["dvb"] = _dot3(hd["t"], _split(dvn_ref[hd["rs"], hd["hs"]]), TN)
            hd["dkbeg"] = _dot3(hd["t"], _split(dw_ref[hd["rs"], hd["hs"]]), TN)
        for hd in heads:
            rs, hs = hd["rs"], hd["hs"]
            da = -(_mdot(hd["dvb"], u_ref[rs, hs], NT) + _mdot(hd["dkbeg"], w_ref[rs, hs], NT))
            dm = jnp.where(strict, da, 0.0)
            dp_ = dp_ref[rs, hd["cs"]]
            hd["dkk"] = dm * hd["decay"]
            hd["dqk"] = dp_ * hd["decay"]
            hd["e"] = (hd["dkk"] * hd["kk"] + hd["dqk"] * hd["qk"])
        for hd in heads:
            hd["dkb"] = _mdot(hd["dkk"], hd["k"], NN)
            hd["dk"] = _mdot(hd["dkk"], hd["kb"], TN) + _mdot(hd["dqk"], hd["q"], TN)
            hd["dq"] = _mdot(hd["dqk"], hd["k"], NN)
            onehot = (lane == GDN_HEADS + hd["hh"]).astype(jnp.bfloat16)
            e_hi, e_lo = _split(hd["e"])
            hd["col_sums"] = _dot(e_lo, onehot, TN) + _dot(e_hi, onehot, TN)
        tiles = {}
        for hd in heads:
            ci, hh, rs, hs = hd["ci"], hd["hh"], hd["rs"], hd["hs"]
            qs, ks, vs = _head_slices(hh)
            q, k, beta, gc, eg, kb = hd["q"], hd["k"], hd["beta"], hd["gc"], hd["eg"], hd["kb"]
            v = qkv_ref[rs, vs]
            dqd_, dkd_ = dqd_ref[rs, hs], dkd_ref[rs, hs]
            gl = gc[CHUNK - 1:CHUNK, :]
            ek = jnp.exp(gl - gc)
            dkb = hd["dkb"] + hd["dkbeg"] * eg
            deg = jnp.sum(dqd_ * q, axis=1, keepdims=True) + jnp.sum(hd["dkbeg"] * kb, axis=1, keepdims=True)
            dek = jnp.sum(dkd_ * k, axis=1, keepdims=True)
            dcd_ = dcd_ref[ci * CHUNK:ci * CHUNK + 1, GDN_HEADS + hh:GDN_HEADS + hh + 1]
            dgl = jnp.sum(dek * ek, axis=0, keepdims=True) + dcd_ * jnp.exp(gl)
            dgc = jnp.sum(hd["e"], axis=1, keepdims=True) + deg * eg - dek * ek
            dbeta_tile, dgc_tile = tiles.get(ci, (jnp.zeros((CHUNK, 128), F32), jnp.zeros((CHUNK, 128), F32)))
            dgc_tile += jnp.where(lane == GDN_HEADS + hh, dgc, 0.0) - hd["col_sums"]
            dgc_tile += jnp.where((lane == GDN_HEADS + hh) & (row == CHUNK - 1), dgl, 0.0)
            dbeta = jnp.sum(dkb * k, axis=1, keepdims=True) + jnp.sum(hd["dvb"] * v, axis=1, keepdims=True)
            dbeta_tile += jnp.where(lane == hh, dbeta, 0.0)
            tiles[ci] = (dbeta_tile, dgc_tile)
            dqkv_ref[rs, qs] = hd["dq"] + dqd_ * eg
            dqkv_ref[rs, ks] = hd["dk"] + dkd_ * ek + dkb * beta
            dqkv_ref[rs, vs] = hd["dvb"] * beta
        for ci in range(cb):
            dbeta_tile, dgc_tile = tiles[ci]
            dbg_ref[ci * CHUNK:(ci + 1) * CHUNK, :] = dbeta_tile + _chunk_cumsum(dgc_tile, reverse=True)

    def spec(width):
        return pl.BlockSpec((rows, width), lambda n: (n, 0))

    hw, cw = GDN_WIDTH, GDN_HEADS * CHUNK
    return pl.pallas_call(
        body, name=name, grid=(T // rows,),
        in_specs=[spec(QKV_WIDTH), spec(128), spec(cw), spec(hw), spec(hw), spec(hw), spec(hw), spec(hw),
                  spec(hw), spec(cw), spec(128)],
        out_specs=[spec(QKV_WIDTH), spec(128)],
        out_shape=[jax.ShapeDtypeStruct((T, QKV_WIDTH), F32), jax.ShapeDtypeStruct((T, 128), F32)],
        compiler_params=_params("parallel"),
    )(qkv, bg, tinv_all, u, w, dvn, dw, dqd, dkd, dp, dcd)


def _pool_counts(i, tm, rows, offset):
    t = i * tm - offset + lax.broadcasted_iota(jnp.int32, (rows, 1), 0)
    return [jnp.minimum(t + 1, w).astype(F32) for w in POOL_WINDOWS]


def _window_sums(window, forward):
    sums, s, step = [], window, 1
    for _ in POOL_WINDOWS:
        s = s + _shift_rows(s, -step if forward else step)
        sums.append(s)
        step *= 2
    return sums


def _pooled(window, counts):
    sums = _window_sums(window, forward=False)
    out = []
    for gi in range(POOL_GROUPS):
        sl = slice(gi * 128, (gi + 1) * 128)
        out.append(sums[gi][HALO:, sl] / counts[gi] - window[HALO:, sl])
    return out


def _mix_post(o, proj, gdn_norm, pool_w, pool_scale, name):
    T = o.shape[0]
    tm = _tile(T, 512)
    hb = tm // HALO

    def body(o_ref, z_ref, p_ref, ph_ref, gn_ref, pw_ref, ps_ref, out_ref):
        i = pl.program_id(0)
        for hh in range(GDN_HEADS):
            sl = slice(hh * HEAD_DIM, (hh + 1) * HEAD_DIM)
            oh, zh = o_ref[:, sl], z_ref[:, sl]
            ro = lax.rsqrt(jnp.mean(oh * oh, axis=-1, keepdims=True) + EPS)
            out_ref[:, sl] = (((oh * ro) * gn_ref[...]) * (zh * _sigmoid(zh))).astype(out_ref.dtype)
        halo = jnp.where(i == 0, 0.0, ph_ref[...])
        window = jnp.concatenate([halo, p_ref[...]], axis=0)
        pooled = _pooled(window, _pool_counts(i, tm, tm, 0))
        for gi in range(POOL_GROUPS):
            pm = _mdot(pooled[gi], pw_ref[gi], NN)
            out_ref[:, GDN_WIDTH + gi * 128:GDN_WIDTH + (gi + 1) * 128] = (
                pm * ps_ref[:, gi * 128:(gi + 1) * 128]).astype(out_ref.dtype)

    return pl.pallas_call(
        body, name=name, grid=(T // tm,),
        in_specs=[pl.BlockSpec((tm, GDN_WIDTH), lambda i: (i, 0)),
                  pl.BlockSpec((tm, GDN_WIDTH), lambda i: (i, COL_Z // GDN_WIDTH)),
                  pl.BlockSpec((tm, POOL_WIDTH), lambda i: (i, COL_P // POOL_WIDTH)),
                  pl.BlockSpec((HALO, POOL_WIDTH), lambda i: (jnp.maximum(i * hb - 1, 0), COL_P // POOL_WIDTH)),
                  pl.BlockSpec((1, HEAD_DIM), lambda i: (0, 0)),
                  pl.BlockSpec((POOL_GROUPS, 128, 128), lambda i: (0, 0, 0)),
                  pl.BlockSpec((1, POOL_WIDTH), lambda i: (0, 0))],
        out_specs=pl.BlockSpec((tm, GDN_WIDTH + POOL_WIDTH), lambda i: (i, 0)),
        out_shape=jax.ShapeDtypeStruct((T, GDN_WIDTH + POOL_WIDTH), MXU_DTYPE),
        compiler_params=_params("parallel"),
    )(o, proj, proj, proj, gdn_norm, pool_w, pool_scale)


def _mix_post_bwd(dmix, o, proj, gdn_norm, pool_w, pool_scale, name):
    T = o.shape[0]
    tm = _tile(T, 512)
    hb = tm // HALO
    n_tiles = T // tm

    def body(dg_ref, dpo_ref, dpo_next_ref, o_ref, z_ref, p_ref, ph_ref, gn_ref, pw_ref, ps_ref,
             do_ref, dzp_ref, dgn_ref, dpw_ref, dps_ref):
        i = pl.program_id(0)

        @pl.when(i == 0)
        def _():
            dgn_ref[...] = jnp.zeros_like(dgn_ref)
            dpw_ref[...] = jnp.zeros_like(dpw_ref)
            dps_ref[...] = jnp.zeros_like(dps_ref)

        gn = gn_ref[...]
        dgn = jnp.zeros((1, HEAD_DIM), F32)
        for hh in range(GDN_HEADS):
            sl = slice(hh * HEAD_DIM, (hh + 1) * HEAD_DIM)
            oh, zh, dy = o_ref[:, sl], z_ref[:, sl], dg_ref[:, sl]
            ro = lax.rsqrt(jnp.mean(oh * oh, axis=-1, keepdims=True) + EPS)
            on = oh * ro
            sig = _sigmoid(zh)
            sz = zh * sig
            dzp_ref[:, sl] = (dy * (on * gn) * (sig * (1.0 + zh * (1.0 - sig)))).astype(dzp_ref.dtype)
            dgn += jnp.sum(dy * on * sz, axis=0, keepdims=True)
            don = dy * gn * sz
            do_ref[:, sl] = ro * (don - on * jnp.mean(don * on, axis=-1, keepdims=True))
        dgn_ref[...] += dgn

        halo = jnp.where(i == 0, 0.0, ph_ref[...])
        window = jnp.concatenate([halo, p_ref[...]], axis=0)
        counts = _pool_counts(i, tm, tm + HALO, 0)
        pooled = _pooled(window, [cn[:tm] for cn in counts])
        nxt = jnp.where(i == n_tiles - 1, 0.0, dpo_next_ref[...])
        dpo_w = jnp.concatenate([dpo_ref[...], nxt], axis=0)
        ps = ps_ref[...]
        dps = []
        scaled = []
        for gi in range(POOL_GROUPS):
            sl = slice(gi * 128, (gi + 1) * 128)
            dpm = dpo_w[:, sl] * ps[:, sl]
            pm = _mdot(pooled[gi], pw_ref[gi], NN)
            dps.append(jnp.sum(dpo_w[:tm, sl] * pm, axis=0, keepdims=True))
            dpw_ref[gi] += _mdot(pooled[gi], dpm[:tm], TN)
            dpooled = _mdot(dpm, pw_ref[gi], NT)
            scaled.append((dpooled, dpooled / counts[gi]))
        dps_ref[...] += jnp.concatenate(dps, axis=1)
        lead = _window_sums(jnp.concatenate([sc for _, sc in scaled], axis=1), forward=True)
        for gi in range(POOL_GROUPS):
            sl = slice(gi * 128, (gi + 1) * 128)
            dzp_ref[:, GDN_WIDTH + gi * 128:GDN_WIDTH + (gi + 1) * 128] = (
                lead[gi][:tm, sl] - scaled[gi][0][:tm]).astype(dzp_ref.dtype)

    last_halo = T // HALO - 1
    return pl.pallas_call(
        body, name=name, grid=(n_tiles,),
        in_specs=[pl.BlockSpec((tm, GDN_WIDTH), lambda i: (i, 0)),
                  pl.BlockSpec((tm, POOL_WIDTH), lambda i: (i, 1)),
                  pl.BlockSpec((HALO, POOL_WIDTH), lambda i: (jnp.minimum((i + 1) * hb, last_halo), 1)),
                  pl.BlockSpec((tm, GDN_WIDTH), lambda i: (i, 0)),
                  pl.BlockSpec((tm, GDN_WIDTH), lambda i: (i, COL_Z // GDN_WIDTH)),
                  pl.BlockSpec((tm, POOL_WIDTH), lambda i: (i, COL_P // POOL_WIDTH)),
                  pl.BlockSpec((HALO, POOL_WIDTH), lambda i: (jnp.maximum(i * hb - 1, 0), COL_P // POOL_WIDTH)),
                  pl.BlockSpec((1, HEAD_DIM), lambda i: (0, 0)),
                  pl.BlockSpec((POOL_GROUPS, 128, 128), lambda i: (0, 0, 0)),
                  pl.BlockSpec((1, POOL_WIDTH), lambda i: (0, 0))],
        out_specs=[pl.BlockSpec((tm, GDN_WIDTH), lambda i: (i, 0)),
                   pl.BlockSpec((tm, GDN_WIDTH + POOL_WIDTH), lambda i: (i, 0)),
                   pl.BlockSpec((1, HEAD_DIM), lambda i: (0, 0)),
                   pl.BlockSpec((POOL_GROUPS, 128, 128), lambda i: (0, 0, 0)),
                   pl.BlockSpec((1, POOL_WIDTH), lambda i: (0, 0))],
        out_shape=[jax.ShapeDtypeStruct((T, GDN_WIDTH), F32),
                   jax.ShapeDtypeStruct((T, GDN_WIDTH + POOL_WIDTH), MXU_DTYPE),
                   jax.ShapeDtypeStruct((1, HEAD_DIM), F32),
                   jax.ShapeDtypeStruct((POOL_GROUPS, 128, 128), F32),
                   jax.ShapeDtypeStruct((1, POOL_WIDTH), F32)],
        compiler_params=_params("arbitrary"),
    )(dmix, dmix, dmix, o, proj, proj, proj, gdn_norm, pool_w, pool_scale)


def _gdn_prep_bwd(proj, conv_w, a_log_l, dt_bias_l, dqkv, dbg, dzp, name):
    T = proj.shape[0]
    tm = _tile(T, 512)
    hb = tm // 8
    n_tiles = T // tm
    last_halo = T // 8 - 1

    def body(cur_ref, before_ref, after_ref, ba_ref, w_ref, al_ref, dtb_ref, dq_ref, dq_after_ref, dbg_ref,
             dzp_ref, dproj_ref, dw_ref, dal_ref, ddtb_ref):
        i = pl.program_id(0)

        @pl.when(i == 0)
        def _():
            dw_ref[...] = jnp.zeros_like(dw_ref)
            dal_ref[...] = jnp.zeros_like(dal_ref)
            ddtb_ref[...] = jnp.zeros_like(ddtb_ref)

        last = i == n_tiles - 1
        w = w_ref[...]
        before = jnp.where(i == 0, 0.0, before_ref[...])
        after = jnp.where(last, 0.0, after_ref[...])
        window = jnp.concatenate([before, cur_ref[...], after], axis=0)
        y = _conv_act(window, w)
        sig = _sigmoid(y)
        act = y * sig
        dq_w = jnp.concatenate([jnp.zeros((8, QKV_WIDTH), F32), dq_ref[...],
                                jnp.where(last, 0.0, dq_after_ref[...])], axis=0)
        dact = []
        for hh in range(3 * GDN_HEADS):
            sl = slice(hh * HEAD_DIM, (hh + 1) * HEAD_DIM)
            blk, dblk = act[:, sl], dq_w[:, sl]
            if hh < 2 * GDN_HEADS:
                rn = lax.rsqrt(jnp.sum(blk * blk, axis=-1, keepdims=True) + EPS)
                unit = blk * rn
                if hh < GDN_HEADS:
                    dblk = dblk * (HEAD_DIM ** -0.5)
                dblk = rn * (dblk - unit * jnp.sum(dblk * unit, axis=-1, keepdims=True))
            dact.append(dblk)
        dy = jnp.concatenate(dact, axis=1) * (sig * (1.0 + y * (1.0 - sig)))
        dx = dy * w[CONV_K - 1:CONV_K, :]
        dws = [None] * CONV_K
        dws[CONV_K - 1] = jnp.sum(dy[8:8 + tm] * window[8:8 + tm], axis=0, keepdims=True)
        for j in range(CONV_K - 1):
            s = CONV_K - 1 - j
            dx += _shift_rows(dy, -s) * w[j:j + 1, :]
            dws[j] = jnp.sum(dy[8:8 + tm] * _shift_rows(window, s)[8:8 + tm], axis=0, keepdims=True)
        dw_ref[...] += jnp.concatenate(dws, axis=0)
        dproj_ref[:, :QKV_WIDTH] = dx[8:8 + tm].astype(dproj_ref.dtype)
        dproj_ref[:, COL_Z:COL_BA] = dzp_ref[...]

        ba = ba_ref[...]
        dbg_ = dbg_ref[...]
        lane = lax.broadcasted_iota(jnp.int32, ba.shape, 1)
        beta = _sigmoid(ba)
        pre = ba + dtb_ref[...]
        neg_a = -jnp.exp(al_ref[...])
        g = neg_a * _softplus(pre)
        is_g = (lane >= GDN_HEADS) & (lane < 2 * GDN_HEADS)
        da_raw = jnp.where(is_g, dbg_ * neg_a * _sigmoid(pre), 0.0)
        dba = jnp.where(lane < GDN_HEADS, dbg_ * beta * (1.0 - beta), da_raw)
        dproj_ref[:, COL_BA:] = dba.astype(dproj_ref.dtype)
        dal_ref[...] += jnp.sum(jnp.where(is_g, dbg_ * g, 0.0), axis=0, keepdims=True)
        ddtb_ref[...] += jnp.sum(da_raw, axis=0, keepdims=True)

    lane_vec = pl.BlockSpec((1, 128), lambda i: (0, 0))
    return pl.pallas_call(
        body, name=name, grid=(n_tiles,),
        in_specs=[pl.BlockSpec((tm, QKV_WIDTH), lambda i: (i, 0)),
                  pl.BlockSpec((8, QKV_WIDTH), lambda i: (jnp.maximum(i * hb - 1, 0), 0)),
                  pl.BlockSpec((8, QKV_WIDTH), lambda i: (jnp.minimum((i + 1) * hb, last_halo), 0)),
                  pl.BlockSpec((tm, 128), lambda i: (i, COL_BA // 128)),
                  pl.BlockSpec((CONV_K, QKV_WIDTH), lambda i: (0, 0)), lane_vec, lane_vec,
                  pl.BlockSpec((tm, QKV_WIDTH), lambda i: (i, 0)),
                  pl.BlockSpec((8, QKV_WIDTH), lambda i: (jnp.minimum((i + 1) * hb, last_halo), 0)),
                  pl.BlockSpec((tm, 128), lambda i: (i, 0)),
                  pl.BlockSpec((tm, GDN_WIDTH + POOL_WIDTH), lambda i: (i, 0))],
        out_specs=[pl.BlockSpec((tm, D_IN_PAD), lambda i: (i, 0)),
                   pl.BlockSpec((CONV_K, QKV_WIDTH), lambda i: (0, 0)), lane_vec, lane_vec],
        out_shape=[jax.ShapeDtypeStruct((T, D_IN_PAD), MXU_DTYPE),
                   jax.ShapeDtypeStruct((CONV_K, QKV_WIDTH), F32),
                   jax.ShapeDtypeStruct((1, 128), F32), jax.ShapeDtypeStruct((1, 128), F32)],
        compiler_params=_params("arbitrary"),
    )(proj, proj, proj, proj, conv_w, a_log_l, dt_bias_l, dqkv, dqkv, dbg, dzp)


def _mod_part(c_all, w_ada, b_part, name):
    def body(c_ref, w_ref, b_ref, out_ref):
        cc = c_ref[...]
        out_ref[...] = _mdot(cc * _sigmoid(cc), w_ref[...], NN) + b_ref[...]

    return pl.pallas_call(
        body, name=name, out_shape=jax.ShapeDtypeStruct((c_all.shape[0], w_ada.shape[1]), F32),
        compiler_params=_params(),
    )(c_all, w_ada, b_part)


def _w_ada_grad(c_all, dmod_part, name):
    def body(c_ref, d_ref, out_ref):
        cc = c_ref[...]
        out_ref[...] = _mdot(cc * _sigmoid(cc), d_ref[...], TN)

    return pl.pallas_call(
        body, name=name, out_shape=jax.ShapeDtypeStruct((c_all.shape[1], dmod_part.shape[1]), F32),
        compiler_params=_params(),
    )(c_all, dmod_part)


def _sum_parts(parts, name):
    _, R, C = parts.shape
    tr = max([t for t in range(16, min(R, 512) + 1, 16) if R % t == 0], default=R)

    def body(p_ref, out_ref):
        acc = p_ref[0].astype(F32)
        for s in range(1, N_DEV):
            acc += p_ref[s].astype(F32)
        out_ref[...] = acc

    return pl.pallas_call(
        body, name=name, grid=(R // tr,),
        in_specs=[pl.BlockSpec((N_DEV, tr, C), lambda i: (0, i, 0))],
        out_specs=pl.BlockSpec((tr, C), lambda i: (i, 0)),
        out_shape=jax.ShapeDtypeStruct((R, C), F32),
        compiler_params=_params("parallel"),
    )(parts)


def _adamw_math(w, g, m, v):
    mm = ADAM_B1 * m + (1.0 - ADAM_B1) * g
    vv = ADAM_B2 * v + (1.0 - ADAM_B2) * (g * g)
    m_hat = mm / (1.0 - ADAM_B1 ** ADAM_STEP)
    v_hat = vv / (1.0 - ADAM_B2 ** ADAM_STEP)
    return -ADAM_LR * (m_hat / (jnp.sqrt(v_hat) + ADAM_EPS) + ADAM_WD * w), mm, vv


def _adamw_small(ws, gs, ms, vs, name):
    n = len(ws)

    def body(*refs):
        for i in range(n):
            d, mm, vv = _adamw_math(*[refs[k * n + i][...] for k in range(4)])
            refs[4 * n + i][...] = d
            refs[5 * n + i][...] = mm
            refs[6 * n + i][...] = vv

    out = pl.pallas_call(
        body, name=name, out_shape=[jax.ShapeDtypeStruct(w.shape, F32) for w in ws] * 3,
        compiler_params=_params(),
    )(*ws, *gs, *ms, *vs)
    return out[:n], out[n:2 * n], out[2 * n:]


def _adamw_sum(parts, w, m, v, name):
    R, C = w.shape
    tr = max([t for t in range(16, min(R, 512) + 1, 16) if R % t == 0], default=R)

    def body(p_ref, w_ref, m_ref, v_ref, g_ref, d_ref, mo_ref, vo_ref):
        g = p_ref[0].astype(F32)
        for s in range(1, N_DEV):
            g += p_ref[s].astype(F32)
        g_ref[...] = g
        d_ref[...], mo_ref[...], vo_ref[...] = _adamw_math(w_ref[...], g, m_ref[...], v_ref[...])

    spec = pl.BlockSpec((tr, C), lambda i: (i, 0))
    return pl.pallas_call(
        body, name=name, grid=(R // tr,),
        in_specs=[pl.BlockSpec((N_DEV, tr, C), lambda i: (0, i, 0)), spec, spec, spec], out_specs=[spec] * 4,
        out_shape=[jax.ShapeDtypeStruct((R, C), F32)] * 4,
        compiler_params=_params("parallel"),
    )(parts, w, m, v)


def _adamw(w, g, m, v, name):
    R, C = w.shape
    tr = max([t for t in range(8, min(R, 512) + 1, 8) if R % t == 0], default=R)

    def body(w_ref, g_ref, m_ref, v_ref, d_ref, mo_ref, vo_ref):
        d_ref[...], mo_ref[...], vo_ref[...] = _adamw_math(w_ref[...], g_ref[...], m_ref[...], v_ref[...])

    spec = pl.BlockSpec((tr, C), lambda i: (i, 0))
    return pl.pallas_call(
        body, name=name, grid=(R // tr,),
        in_specs=[spec] * 4, out_specs=[spec] * 3,
        out_shape=[jax.ShapeDtypeStruct((R, C), F32)] * 3,
        compiler_params=_params("parallel"),
    )(w, g, m, v)


def _weight_grad(a, b, name, dep=None):
    return _matmul([(a, b)], TN, WIRE_DTYPE, name, tm=1408, tn=1024, tk=2048, dep=dep)


def _rows_of(flat, lanes=1024):
    flat = flat.reshape(-1)
    n = -(-flat.shape[0] // lanes) * lanes
    return jnp.pad(flat, (0, n - flat.shape[0])).reshape(n // lanes, lanes)


def _pad_rows(a, rows):
    return jnp.pad(a, ((0, rows - a.shape[0]), (0, 0)))


def kernel(x, c, w_ada, b_ada, norm_ffn1, ffn1_gate, ffn1_up, ffn1_down, norm_mix, w_in, conv_w, a_log, dt_bias, gdn_norm, pool_w, pool_scale, w_out, norm_ffn2, ffn2_gate, ffn2_up, ffn2_down, final_norm, loss_target, m_w_ada, m_b_ada, m_norm_ffn1, m_ffn1_gate, m_ffn1_up, m_ffn1_down, m_norm_mix, m_w_in, m_conv_w, m_a_log, m_dt_bias, m_gdn_norm, m_pool_w, m_pool_scale, m_w_out, m_norm_ffn2, m_ffn2_gate, m_ffn2_up, m_ffn2_down, m_final_norm, v_w_ada, v_b_ada, v_norm_ffn1, v_ffn1_gate, v_ffn1_up, v_ffn1_down, v_norm_mix, v_w_in, v_conv_w, v_a_log, v_dt_bias, v_gdn_norm, v_pool_w, v_pool_scale, v_w_out, v_norm_ffn2, v_ffn2_gate, v_ffn2_up, v_ffn2_down, v_final_norm):
    T, D = x.shape[1], x.shape[2]
    Fs = ffn1_gate.shape[2]
    Ws = w_in.shape[2]
    Ws_pad = -(-Ws // 16) * 16
    Os = w_out.shape[1]
    Ms = w_ada.shape[2]
    Cs = conv_w.shape[2]
    me = 4 * lax.axis_index("x") + 2 * lax.axis_index("y") + lax.axis_index("c")
    x0, target = x[0], loss_target[0]

    def wire(a):
        return a.astype(WIRE_DTYPE)

    def token(started):
        return started[4][:1, :1]

    def with_own(landed, own):
        return lax.dynamic_update_slice(landed, own[None], (me, 0, 0))

    def full(landed):
        return landed.reshape(-1, D).astype(MXU_DTYPE)

    no_dep = jnp.zeros((8, 128), F32)
    small = jnp.concatenate([_pad_rows(c, 8), _pad_rows(jnp.pad(conv_w[0], ((0, 0), (0, D - Cs))), 8)], axis=0)
    got, = _all_gather([small], "gather_small")
    c_all = got[:, 0, :]
    conv_full = jnp.transpose(got[:, 8:8 + CONV_K, :Cs], (1, 0, 2)).reshape(CONV_K, QKV_WIDTH)
    b_part = lax.dynamic_slice(b_ada, (0, me * Ms), (1, Ms))
    mod_part = _mod_part(c_all, w_ada[0], b_part, "mod_part")

    w1 = [wire(ffn1_gate[0].T)]
    w1u = [wire(ffn1_up[0].T)]
    w1d = [wire(ffn1_down[0])]
    w2 = [wire(_pad_rows(w_in[0].T, Ws_pad)), wire(w_out[0])]
    w3 = [wire(ffn2_gate[0].T), wire(ffn2_up[0].T), wire(ffn2_down[0])]
    mod_parts, *w1_all = _all_gather([mod_part] + w1, "gather_mod_w1")
    mod_all = jnp.transpose(mod_parts, (1, 0, 2)).reshape(N_DEV, N_MOD * D)
    mod = lax.dynamic_slice(mod_all, (me, 0), (1, N_MOD * D)).reshape(N_MOD, 1, D)
    sh1, sc1, gt1, sh2, sc2, gt2, sh3, sc3, gt3 = [mod[i] for i in range(N_MOD)]
    wg1_t = full(w1_all[0])
    w1u_sent = _exchange_start(w1u, True, w1_all[0], "w1u_start")
    w1d_sent = _exchange_start(w1d, True, w1u_sent[4], "w1d_start")
    w2_sent = _exchange_start(w2, True, w1d_sent[4], "w2_start")
    w3_sent = _exchange_start(w3, True, w2_sent[4], "w3_start")

    lane_pad = lambda a: jnp.pad(a, ((0, 0), (GDN_HEADS, 128 - 2 * GDN_HEADS)))
    a_log_l, dt_bias_l = lane_pad(a_log), lane_pad(dt_bias)
    pool_w_m = pool_w[0].astype(MXU_DTYPE)

    g1, h1 = _swiglu_gate(x0, wg1_t, norm_ffn1, sc1 + token(w3_sent), sh1, "ffn1_gate")
    wu1_t = full(with_own(_exchange_wait(w1u_sent, h1, True, "w1u_wait")[0], w1u[0]))
    u1, a1 = _swiglu_up(h1, wu1_t, g1, "ffn1_up")
    wd1 = full(with_own(_exchange_wait(w1d_sent, a1, True, "w1d_wait")[0], w1d[0]))
    y1, x1, h2 = _matmul_resid_norm_mod(a1, wd1, x0, gt1, 0.5, norm_mix, sc2, sh2, "ffn1_down")
    w_in_all, wo_all = [with_own(z, own) for z, own in zip(_exchange_wait(w2_sent, h2, True, "w2_wait"), w2)]
    w_in_t = w_in_all[:, :Ws, :].reshape(-1, D).astype(MXU_DTYPE)
    wo = full(wo_all)
    w_in_re = jnp.concatenate([w_in_t[:COL_Z + GDN_WIDTH], w_in_t[D_IN - POOL_WIDTH:],
                               w_in_t[4 * GDN_WIDTH:4 * GDN_WIDTH + 2 * GDN_HEADS],
                               jnp.zeros((128 - 2 * GDN_HEADS, D), MXU_DTYPE)], axis=0)
    proj = _matmul([(h2, w_in_re)], NT, F32, "proj_in", tm=512, tn=D_IN_PAD, tk=D)
    qkv, bg = _gdn_prep(proj, conv_full, a_log_l, dt_bias_l, "gdn_prep")
    tinv, u_c, w_c, qd_c, kd_c, p_c, cd_c = _gdn_chunk_fwd(qkv, bg, "gdn_chunk_fwd")
    o, s_all, vn_c = _gdn_scan_fwd(u_c, w_c, qd_c, kd_c, p_c, cd_c, "gdn_scan_fwd")
    mix_in = _mix_post(o, proj, gdn_norm, pool_w_m, pool_scale, "mix_post")
    mixed, x2, h3 = _matmul_resid_norm_mod(mix_in, wo, x1, gt2, 1.0, norm_ffn2, sc3, sh3, "mix_out")
    wg2_t, wu2_t, wd2 = [full(with_own(z, own))
                         for z, own in zip(_exchange_wait(w3_sent, h3, True, "w3_wait"), w3)]
    g3, u3, a3, loss_row, d3, d_final, dy3, dgt3 = _swiglu_fwd_loss(
        h3, wg2_t, wu2_t, wd2, x2, gt3, final_norm.reshape(1, D), target, "ffn2_fwd_loss")

    dg3, du3, d2, d_n3, dsc3, dsh3, dmixed, dgt2 = _swiglu_bwd(
        dy3, wd2, g3, u3, wg2_t, wu2_t, "ffn2_bwd_norm3_bwd", (x2, norm_ffn2, sc3), d3,
        produced_by=(mixed, gt2, 1.0))
    d_wd2 = _weight_grad(a3, dy3, "ffn2_dwd")
    d_wg2 = _weight_grad(dg3, h3, "ffn2_dwg")
    d_wu2 = _weight_grad(du3, h3, "ffn2_dwu")
    dmix_in = _matmul([(dmixed, wo)], NT, F32, "mix_out_bwd", tm=512, tn=GDN_WIDTH + POOL_WIDTH, tk=D)
    d_wo = _matmul([(mix_in, dmixed)], TN, WIRE_DTYPE, "mix_dwo", tm=GDN_WIDTH + POOL_WIDTH, tn=D, tk=2048)
    do, dzp, d_gn, d_pw, d_ps = _mix_post_bwd(dmix_in, o, proj, gdn_norm, pool_w_m, pool_scale, "mix_post_bwd")
    dvn, dw_c, dqd, dkd, dp_c, dcd = _gdn_scan_bwd(do, w_c, qd_c, kd_c, p_c, cd_c, s_all, vn_c, "gdn_scan_bwd")
    dqkv, dbg = _gdn_chunk_bwd(qkv, bg, tinv, u_c, w_c, dvn, dw_c, dqd, dkd, dp_c, dcd, "gdn_chunk_bwd")
    dproj, d_conv, d_al, d_dtb = _gdn_prep_bwd(proj, conv_full, a_log_l, dt_bias_l, dqkv, dbg, dzp, "gdn_prep_bwd")
    d_win_re = _matmul([(dproj, h2)], TN, WIRE_DTYPE, "proj_in_dw", tm=D_IN_PAD, tn=D, tk=1024)
    d_win_t = jnp.concatenate([d_win_re[:COL_Z + GDN_WIDTH], d_win_re[COL_BA:COL_BA + 2 * GDN_HEADS],
                               d_win_re[COL_P:COL_P + POOL_WIDTH]], axis=0)
    d_win_blocks = jnp.pad(d_win_t.reshape(N_DEV, Ws, D), ((0, 0), (0, Ws_pad - Ws), (0, 0)))
    names23 = ["w_in", "w_out", "ffn2_gate", "ffn2_up", "ffn2_down"]
    parts23 = [wire(d_win_blocks), d_wo.reshape(N_DEV, Os, D), d_wg2.reshape(N_DEV, Fs, D),
               d_wu2.reshape(N_DEV, Fs, D), d_wd2.reshape(N_DEV, Fs, D)]
    own23 = [lax.dynamic_index_in_dim(p, me, 0, keepdims=False) for p in parts23]
    g23_sent = _exchange_start(parts23, False, no_dep, "g23_start")
    d1, d_n2, dsc2, dsh2, dy1, dgt1 = _norm_bwd((dproj, w_in_re), x1, norm_mix, sc2 + token(g23_sent), d2,
                                                "proj_in_bwd_norm2_bwd", produced_by=(y1, gt1, 0.5))
    dg1, du1, grad_x, d_n1, dsc1, dsh1 = _swiglu_bwd(
        dy1, wd1, g1, u1, wg1_t, wu1_t, "ffn1_bwd_norm1_bwd", (x0, norm_ffn1, sc1), d1)

    dmod = jnp.concatenate([dsh1, dsc1, dgt1, dsh2, dsc2, dgt2, dsh3, dsc3, dgt3], axis=0)
    small_rows = [dmod.reshape(-1), d_n1[0], d_n2[0], d_n3[0], d_final[0], d_gn[0], d_ps[0],
                  d_al[0, GDN_HEADS:2 * GDN_HEADS], d_dtb[0, GDN_HEADS:2 * GDN_HEADS], loss_row[0, :1],
                  d_conv.reshape(-1), d_pw.reshape(-1)]
    lanes = 1024
    small_rows = [_rows_of(r, lanes) for r in small_rows]
    n_rows = [r.shape[0] for r in small_rows]
    row_off = [sum(n_rows[:i]) for i in range(len(n_rows))]
    total = -(-sum(n_rows) // 8) * 8
    slab = _pad_rows(jnp.concatenate(small_rows, axis=0), total)
    slab_sent = _exchange_start([slab], True, no_dep, "small_grads_start")

    def send_ffn1(a, b, which, dep):
        parts = _weight_grad(a, b, f"ffn1_{which}", dep=dep).reshape(N_DEV, Fs, D)
        own = lax.dynamic_index_in_dim(parts, me, 0, keepdims=False)
        return _exchange_start([parts], False, no_dep, f"g1_{which}_start"), own

    g1_wg, own_wg = send_ffn1(dg1, h1, "dwg", slab_sent[4])
    g1_wu, own_wu = send_ffn1(du1, h1, "dwu", g1_wg[4])
    g1_wd, own_wd = send_ffn1(a1, dy1, "dwd", g1_wu[4])

    slab_all = with_own(_exchange_wait(slab_sent, g1_wg[4], True, "small_grads_wait")[0], slab)
    summed = _sum_parts(slab_all, "sum_small_grads")

    def piece(idx, n):
        return summed[row_off[idx]:row_off[idx] + n_rows[idx]].reshape(-1)[:n]

    g_b_ada = piece(0, N_MOD * D).reshape(1, N_MOD * D)
    g_n1, g_n2, g_n3 = piece(1, D).reshape(1, D), piece(2, D).reshape(1, D), piece(3, D).reshape(1, D)
    g_final = piece(4, D)
    g_gn = piece(5, HEAD_DIM).reshape(1, HEAD_DIM)
    g_ps = piece(6, POOL_WIDTH).reshape(1, POOL_WIDTH)
    g_al = piece(7, GDN_HEADS).reshape(1, GDN_HEADS)
    g_dtb = piece(8, GDN_HEADS).reshape(1, GDN_HEADS)
    loss = piece(9, 1)[0]
    g_conv = lax.dynamic_slice(piece(10, CONV_K * QKV_WIDTH).reshape(1, CONV_K, QKV_WIDTH), (0, 0, me * Cs),
                               (1, CONV_K, Cs))
    g_pw = piece(11, POOL_GROUPS * 128 * 128).reshape(1, POOL_GROUPS, 128, 128)

    dmod_all = slab_all[:, row_off[0]:row_off[0] + n_rows[0], :].reshape(N_DEV, -1)[:, :N_MOD * D]
    g_w_ada = _w_ada_grad(c_all, lax.dynamic_slice(dmod_all, (0, me * Ms), (N_DEV, Ms)), "w_ada_grad")[None]

    landed23 = _exchange_wait(g23_sent, g1_wd[4], False, "g23_wait")
    parts8 = {n: with_own(z, own) for n, z, own in zip(names23, landed23, own23)}
    g_rows = dict(w_in=_sum_parts(parts8.pop("w_in"), "sum_w_in")[:Ws])
    column_sharded = ("w_in", "ffn1_gate", "ffn1_up", "ffn2_gate", "ffn2_up")

    names = ["w_ada", "b_ada", "norm_ffn1", "ffn1_gate", "ffn1_up", "ffn1_down", "norm_mix", "w_in", "conv_w",
             "a_log", "dt_bias", "gdn_norm", "pool_w", "pool_scale", "w_out", "norm_ffn2", "ffn2_gate", "ffn2_up",
             "ffn2_down", "final_norm"]
    weights = dict(zip(names, [w_ada, b_ada, norm_ffn1, ffn1_gate, ffn1_up, ffn1_down, norm_mix, w_in, conv_w,
                               a_log, dt_bias, gdn_norm, pool_w, pool_scale, w_out, norm_ffn2, ffn2_gate, ffn2_up,
                               ffn2_down, final_norm]))
    ms = dict(zip(names, [m_w_ada, m_b_ada, m_norm_ffn1, m_ffn1_gate, m_ffn1_up, m_ffn1_down, m_norm_mix, m_w_in,
                          m_conv_w, m_a_log, m_dt_bias, m_gdn_norm, m_pool_w, m_pool_scale, m_w_out, m_norm_ffn2,
                          m_ffn2_gate, m_ffn2_up, m_ffn2_down, m_final_norm]))
    vs = dict(zip(names, [v_w_ada, v_b_ada, v_norm_ffn1, v_ffn1_gate, v_ffn1_up, v_ffn1_down, v_norm_mix, v_w_in,
                          v_conv_w, v_a_log, v_dt_bias, v_gdn_norm, v_pool_w, v_pool_scale, v_w_out, v_norm_ffn2,
                          v_ffn2_gate, v_ffn2_up, v_ffn2_down, v_final_norm]))
    grads = dict(w_ada=g_w_ada, b_ada=g_b_ada, norm_ffn1=g_n1, norm_mix=g_n2, conv_w=g_conv,
                 a_log=g_al, dt_bias=g_dtb, gdn_norm=g_gn, pool_w=g_pw, pool_scale=g_ps,
                 norm_ffn2=g_n3, final_norm=g_final)
    delta, new_m, new_v = {}, {}, {}

    def adamw_big(n):
        if n in column_sharded:
            view, back = (lambda a: a[0].T), (lambda a: a.T[None])
        else:
            view, back = (lambda a: a[0]), (lambda a: a[None])
        if n in parts8:
            g, d_, m_, v_ = _adamw_sum(parts8[n], view(weights[n]), view(ms[n]), view(vs[n]), f"adamw_{n}")
        else:
            g = g_rows[n] if n in g_rows else view(grads[n])
            d_, m_, v_ = _adamw(view(weights[n]), g, view(ms[n]), view(vs[n]), f"adamw_{n}")
        grads[n], delta[n], new_m[n], new_v[n] = back(g), back(d_), back(m_), back(v_)

    early = ["w_ada", "w_in", "w_out", "ffn2_gate", "ffn2_up", "ffn2_down"]
    late = ["ffn1_gate", "ffn1_up", "ffn1_down"]
    for n in early:
        adamw_big(n)
    done = sum(delta[n][0, :1, :1] for n in early)

    def arrived(started, own, which):
        landed, = _exchange_wait(started, done, False, f"g1_{which}_wait")
        return with_own(landed, own)

    parts8["ffn1_gate"] = arrived(g1_wg, own_wg, "dwg")
    parts8["ffn1_up"] = arrived(g1_wu, own_wu, "dwu")
    parts8["ffn1_down"] = arrived(g1_wd, own_wd, "dwd")
    for n in late:
        adamw_big(n)
    small_names = [n for n in names if n not in early + late]
    two_d = lambda a: a.reshape(-1, a.shape[-1])
    small_out = _adamw_small(*[[two_d(src[n]) for n in small_names] for src in (weights, grads, ms, vs)],
                             "adamw_small")
    for dst, outs in zip((delta, new_m, new_v), small_out):
        for n, a in zip(small_names, outs):
            dst[n] = a.reshape(weights[n].shape)

    return (loss, grad_x[None], *[grads[n] for n in names], *[delta[n] for n in names],
            *[new_m[n] for n in names], *[new_v[n] for n in names])
```
